```python
import math
import jax, jax.numpy as jnp
from jax import lax
import numpy as np

D_MODEL = 2048
BATCH = 8
SEQ = 4096
DEPTH = 2

HEAD_DIM = 128
A_WIDTH = D_MODEL // 2
A_GROUPS = A_WIDTH // 128
CHUNK = 128
B_HEADS = (D_MODEL // 2) // HEAD_DIM
B_WIDTH = B_HEADS * HEAD_DIM
DILATED_PAIRS = ((128, 1), (512, 4), (2048, 16))
ATT_BLOCK = 128
C_HEADS = D_MODEL // HEAD_DIM
C_WIDTH = C_HEADS * HEAD_DIM
D_FF = 4 * D_MODEL
N_EVEN = (DEPTH + 1) // 2
N_ODD = DEPTH // 2
RMS_EPS = 1e-6
LN_EPS = 1e-5

kernel_name = "hybrid_gmlp_dilated_stickbreak_trunk"


def rmsnorm(x, g):
    xf = x.astype(jnp.float32)
    y = xf * lax.rsqrt(jnp.mean(xf * xf, axis=-1, keepdims=True) + RMS_EPS)
    return (y * g.astype(jnp.float32)).astype(x.dtype)


def layernorm(x, g, b):
    xf = x.astype(jnp.float32)
    mu = jnp.mean(xf, axis=-1, keepdims=True)
    var = jnp.mean(jnp.square(xf - mu), axis=-1, keepdims=True)
    y = (xf - mu) * lax.rsqrt(var + LN_EPS)
    return (y * g.astype(jnp.float32) + b.astype(jnp.float32)).astype(x.dtype)


def alibi_slopes(n):
    return jnp.exp2(-8.0 * (jnp.arange(n, dtype=jnp.float32) + 1.0) / n)


def spatial_gating_unit(u, v, ln_g, ln_b, w_s, b_s):
    bsz, t, _ = v.shape
    v = layernorm(v, ln_g, ln_b)
    vg = v.reshape(bsz, t // CHUNK, CHUNK, A_GROUPS, A_WIDTH // A_GROUPS)
    w = jnp.tril(w_s).astype(v.dtype)
    mixed = jnp.einsum('gij,bcjgd->bcigd', w, vg) + b_s.T.astype(v.dtype)[None, None, :, :, None]
    return u * mixed.reshape(bsz, t, A_WIDTH)


def dilated_branch(q, k, v, window, dilation, slopes):
    bsz, t, h, dh = q.shape
    blk = ATT_BLOCK
    win_sub = window // dilation
    period = dilation * blk
    tp = -(-t // period) * period
    pad = tp - t
    sub_len = tp // dilation
    nb = sub_len // blk

    def to_sub(a):
        a = jnp.pad(a, ((0, 0), (0, pad), (0, 0), (0, 0)))
        a = a.reshape(bsz, sub_len, dilation, h, dh).transpose(0, 2, 3, 1, 4)
        return a.reshape(bsz, dilation, h, nb, blk, dh)

    qs, ks, vs = to_sub(q), to_sub(k), to_sub(v)

    def with_prev(a):
        prev = jnp.concatenate([jnp.zeros_like(a[:, :, :, :1]), a[:, :, :, :-1]], axis=3)
        return jnp.concatenate([prev, a], axis=4)

    kw, vw = with_prev(ks), with_prev(vs)
    s = jnp.einsum('brhnqe,brhnke->brhnqk', qs, kw).astype(jnp.float32)

    qi = jnp.arange(blk)[:, None]
    kj = jnp.arange(2 * blk)[None, :]
    dist = qi + blk - kj
    band = (dist >= 0) & (dist <= win_sub)
    first = (jnp.arange(nb) == 0)[:, None, None] & (kj < blk)[None]
    valid = band[None] & ~first
    bias = -slopes[:, None, None] * (dist * dilation).astype(jnp.float32)[None]
    s = s + bias[None, None, :, None]
    s = jnp.where(valid[None, None, None], s, -jnp.inf)
    m = jnp.max(s, axis=-1, keepdims=True)
    p = jnp.exp(s - m)
    den = jnp.sum(p, axis=-1, keepdims=True)
    o = jnp.einsum('brhnqk,brhnke->brhnqe', p, vw.astype(jnp.float32)) / den
    lse = (m + jnp.log(den))[..., 0]

    o = o.reshape(bsz, dilation, h, sub_len, dh).transpose(0, 3, 1, 2, 4).reshape(bsz, tp, h, dh)[:, :t]
    lse = lse.reshape(bsz, dilation, h, sub_len).transpose(0, 3, 1, 2).reshape(bsz, tp, h)[:, :t]
    return o, lse


def dilated_mixture(q, k, v):
    slopes = alibi_slopes(q.shape[2])
    outs, lses = [], []
    for window, dilation in DILATED_PAIRS:
        o, lse = dilated_branch(q, k, v, window, dilation, slopes)
        outs.append(o)
        lses.append(lse)
    wts = jax.nn.softmax(jnp.stack(lses, axis=0), axis=0)
    o = jnp.sum(wts[..., None] * jnp.stack(outs, axis=0), axis=0)
    return o.astype(q.dtype)


def gmlp_dilated_layer(h, w_in, ln_g, ln_b, w_s, b_s, w_out):
    bsz, t, _ = h.shape
    z = h @ w_in
    u, va, q, k, vb = jnp.split(z, [A_WIDTH, 2 * A_WIDTH, 2 * A_WIDTH + B_WIDTH, 2 * A_WIDTH + 2 * B_WIDTH], axis=-1)
    a_out = spatial_gating_unit(jax.nn.gelu(u, approximate=False), jax.nn.gelu(va, approximate=False), ln_g, ln_b, w_s, b_s)
    q = q.reshape(bsz, t, B_HEADS, HEAD_DIM) * (HEAD_DIM ** -0.5)
    k = k.reshape(bsz, t, B_HEADS, HEAD_DIM)
    vb = vb.reshape(bsz, t, B_HEADS, HEAD_DIM)
    b_out = dilated_mixture(q, k, vb).reshape(bsz, t, B_WIDTH)
    return jnp.concatenate([a_out, b_out], axis=-1) @ w_out


def stick_breaking_attention(q, k, v):
    bsz, h, t, dh = q.shape
    nb = t // ATT_BLOCK
    qb = q.reshape(bsz, h, nb, ATT_BLOCK, dh).transpose(2, 0, 1, 3, 4)
    kpos = jnp.arange(t)

    def block(args):
        qblk, start = args
        z = jnp.einsum('bhqe,bhke->bhqk', qblk, k).astype(jnp.float32)
        qpos = start + jnp.arange(ATT_BLOCK)
        causal = kpos[None, :] < qpos[:, None]
        log_1m_beta = jnp.where(causal, jax.nn.log_sigmoid(-z), 0.0)
        rev = lax.cumsum(log_1m_beta, axis=3, reverse=True)
        excl = jnp.concatenate([rev[..., 1:], jnp.zeros_like(rev[..., :1])], axis=-1)
        a = jnp.where(causal, jnp.exp(jax.nn.log_sigmoid(z) + excl), 0.0)
        return jnp.einsum('bhqk,bhke->bhqe', a.astype(v.dtype), v)

    o = lax.map(block, (qb, jnp.arange(nb) * ATT_BLOCK))
    return o.transpose(1, 0, 3, 2, 4).reshape(bsz, t, h * dh)


def stick_breaking_layer(h, w_in, w_out):
    bsz, t, _ = h.shape
    q, k, v = jnp.split(h @ w_in, 3, axis=-1)
    def heads(a):
        return a.reshape(bsz, t, C_HEADS, HEAD_DIM).transpose(0, 2, 1, 3)
    o = stick_breaking_attention(heads(q) * (HEAD_DIM ** -0.5), heads(k), heads(v))
    return o @ w_out


def squared_relu_mlp(h, w1, w2):
    return jnp.square(jax.nn.relu(h @ w1)) @ w2


def _fwd_setup_inputs(seed: int = 0) -> dict:
    key = jax.random.key(seed)
    ks = jax.random.split(key, 16)
    f32 = jnp.float32
    def nrm(k, shape, scale):
        return jax.random.normal(k, shape, f32) * scale
    def gain(k, shape):
        return 1.0 + 0.02 * jax.random.normal(k, shape, f32)
    return {
        "x": jax.random.normal(ks[0], (BATCH, SEQ, D_MODEL), f32),
        "norm_pre_mix": gain(ks[1], (DEPTH, D_MODEL)),
        "norm_post_mix": gain(ks[2], (DEPTH, D_MODEL)),
        "norm_pre_ffn": gain(ks[3], (DEPTH, D_MODEL)),
        "norm_post_ffn": gain(ks[4], (DEPTH, D_MODEL)),
        "ab_w_in": nrm(ks[5], (N_EVEN, D_MODEL, 2 * A_WIDTH + 3 * B_WIDTH), D_MODEL ** -0.5),
        "sgu_ln_g": gain(ks[6], (N_EVEN, A_WIDTH)),
        "sgu_ln_b": nrm(ks[7], (N_EVEN, A_WIDTH), 0.02),
        "sgu_w": nrm(ks[8], (N_EVEN, A_GROUPS, CHUNK, CHUNK), CHUNK ** -0.5),
        "sgu_b": 1.0 + nrm(ks[9], (N_EVEN, A_GROUPS, CHUNK), 0.1),
        "ab_w_out": nrm(ks[10], (N_EVEN, A_WIDTH + B_WIDTH, D_MODEL), (A_WIDTH + B_WIDTH) ** -0.5),
        "sb_w_in": nrm(ks[11], (N_ODD, D_MODEL, 3 * C_WIDTH), D_MODEL ** -0.5),
        "sb_w_out": nrm(ks[12], (N_ODD, C_WIDTH, D_MODEL), C_WIDTH ** -0.5),
        "ffn_w1": nrm(ks[13], (DEPTH, D_MODEL, D_FF), D_MODEL ** -0.5),
        "ffn_w2": nrm(ks[14], (DEPTH, D_FF, D_MODEL), D_FF ** -0.5),
    }


def _fwd_reference(x, norm_pre_mix, norm_post_mix, norm_pre_ffn, norm_post_ffn,
              ab_w_in, sgu_ln_g, sgu_ln_b, sgu_w, sgu_b, ab_w_out,
              sb_w_in, sb_w_out, ffn_w1, ffn_w2):
    for layer in range(DEPTH):
        h = rmsnorm(x, norm_pre_mix[layer])
        i = layer // 2
        if layer % 2 == 0:
            y = gmlp_dilated_layer(h, ab_w_in[i], sgu_ln_g[i], sgu_ln_b[i], sgu_w[i], sgu_b[i], ab_w_out[i])
        else:
            y = stick_breaking_layer(h, sb_w_in[i], sb_w_out[i])
        x = x + rmsnorm(y, norm_post_mix[layer])
        h = rmsnorm(x, norm_pre_ffn[layer])
        y = squared_relu_mlp(h, ffn_w1[layer], ffn_w2[layer])
        x = x + rmsnorm(y, norm_post_ffn[layer])
    return x


import jax as _jax
import jax.numpy as _jnp

TWIN_FORMAT = 'train_step'
FWD_PARAMS = ['x', 'norm_pre_mix', 'norm_post_mix', 'norm_pre_ffn', 'norm_post_ffn', 'ab_w_in', 'sgu_ln_g', 'sgu_ln_b', 'sgu_w', 'sgu_b', 'ab_w_out', 'sb_w_in', 'sb_w_out', 'ffn_w1', 'ffn_w2']
TWIN_WEIGHTS = ['norm_pre_mix', 'norm_post_mix', 'norm_pre_ffn', 'norm_post_ffn', 'ab_w_in', 'sgu_ln_g', 'sgu_ln_b', 'sgu_w', 'sgu_b', 'ab_w_out', 'sb_w_in', 'sb_w_out', 'ffn_w1', 'ffn_w2']
TWIN_DIFF_INPUT = 'x'
TWIN_INPUTS = ['x', 'norm_pre_mix', 'norm_post_mix', 'norm_pre_ffn', 'norm_post_ffn', 'ab_w_in', 'sgu_ln_g', 'sgu_ln_b', 'sgu_w', 'sgu_b', 'ab_w_out', 'sb_w_in', 'sb_w_out', 'ffn_w1', 'ffn_w2', 'loss_target', 'm_norm_pre_mix', 'm_norm_post_mix', 'm_norm_pre_ffn', 'm_norm_post_ffn', 'm_ab_w_in', 'm_sgu_ln_g', 'm_sgu_ln_b', 'm_sgu_w', 'm_sgu_b', 'm_ab_w_out', 'm_sb_w_in', 'm_sb_w_out', 'm_ffn_w1', 'm_ffn_w2', 'v_norm_pre_mix', 'v_norm_post_mix', 'v_norm_pre_ffn', 'v_norm_post_ffn', 'v_ab_w_in', 'v_sgu_ln_g', 'v_sgu_ln_b', 'v_sgu_w', 'v_sgu_b', 'v_ab_w_out', 'v_sb_w_in', 'v_sb_w_out', 'v_ffn_w1', 'v_ffn_w2']
TWIN_OUTPUTS = ['loss', 'grad_x', 'grad_norm_pre_mix', 'grad_norm_post_mix', 'grad_norm_pre_ffn', 'grad_norm_post_ffn', 'grad_ab_w_in', 'grad_sgu_ln_g', 'grad_sgu_ln_b', 'grad_sgu_w', 'grad_sgu_b', 'grad_ab_w_out', 'grad_sb_w_in', 'grad_sb_w_out', 'grad_ffn_w1', 'grad_ffn_w2', 'delta_norm_pre_mix', 'delta_norm_post_mix', 'delta_norm_pre_ffn', 'delta_norm_post_ffn', 'delta_ab_w_in', 'delta_sgu_ln_g', 'delta_sgu_ln_b', 'delta_sgu_w', 'delta_sgu_b', 'delta_ab_w_out', 'delta_sb_w_in', 'delta_sb_w_out', 'delta_ffn_w1', 'delta_ffn_w2', 'new_m_norm_pre_mix', 'new_m_norm_post_mix', 'new_m_norm_pre_ffn', 'new_m_norm_post_ffn', 'new_m_ab_w_in', 'new_m_sgu_ln_g', 'new_m_sgu_ln_b', 'new_m_sgu_w', 'new_m_sgu_b', 'new_m_ab_w_out', 'new_m_sb_w_in', 'new_m_sb_w_out', 'new_m_ffn_w1', 'new_m_ffn_w2', 'new_v_norm_pre_mix', 'new_v_norm_post_mix', 'new_v_norm_pre_ffn', 'new_v_norm_post_ffn', 'new_v_ab_w_in', 'new_v_sgu_ln_g', 'new_v_sgu_ln_b', 'new_v_sgu_w', 'new_v_sgu_b', 'new_v_ab_w_out', 'new_v_sb_w_in', 'new_v_sb_w_out', 'new_v_ffn_w1', 'new_v_ffn_w2']
TWIN_LEAF_KINDS = {'loss': 'loss', 'grad_x': 'grad_x', 'grad_norm_pre_mix': 'grad_w', 'grad_norm_post_mix': 'grad_w', 'grad_norm_pre_ffn': 'grad_w', 'grad_norm_post_ffn': 'grad_w', 'grad_ab_w_in': 'grad_w', 'grad_sgu_ln_g': 'grad_w', 'grad_sgu_ln_b': 'grad_w', 'grad_sgu_w': 'grad_w', 'grad_sgu_b': 'grad_w', 'grad_ab_w_out': 'grad_w', 'grad_sb_w_in': 'grad_w', 'grad_sb_w_out': 'grad_w', 'grad_ffn_w1': 'grad_w', 'grad_ffn_w2': 'grad_w', 'delta_norm_pre_mix': 'delta_w', 'delta_norm_post_mix': 'delta_w', 'delta_norm_pre_ffn': 'delta_w', 'delta_norm_post_ffn': 'delta_w', 'delta_ab_w_in': 'delta_w', 'delta_sgu_ln_g': 'delta_w', 'delta_sgu_ln_b': 'delta_w', 'delta_sgu_w': 'delta_w', 'delta_sgu_b': 'delta_w', 'delta_ab_w_out': 'delta_w', 'delta_sb_w_in': 'delta_w', 'delta_sb_w_out': 'delta_w', 'delta_ffn_w1': 'delta_w', 'delta_ffn_w2': 'delta_w', 'new_m_norm_pre_mix': 'new_m', 'new_m_norm_post_mix': 'new_m', 'new_m_norm_pre_ffn': 'new_m', 'new_m_norm_post_ffn': 'new_m', 'new_m_ab_w_in': 'new_m', 'new_m_sgu_ln_g': 'new_m', 'new_m_sgu_ln_b': 'new_m', 'new_m_sgu_w': 'new_m', 'new_m_sgu_b': 'new_m', 'new_m_ab_w_out': 'new_m', 'new_m_sb_w_in': 'new_m', 'new_m_sb_w_out': 'new_m', 'new_m_ffn_w1': 'new_m', 'new_m_ffn_w2': 'new_m', 'new_v_norm_pre_mix': 'new_v', 'new_v_norm_post_mix': 'new_v', 'new_v_norm_pre_ffn': 'new_v', 'new_v_norm_post_ffn': 'new_v', 'new_v_ab_w_in': 'new_v', 'new_v_sgu_ln_g': 'new_v', 'new_v_sgu_ln_b': 'new_v', 'new_v_sgu_w': 'new_v', 'new_v_sgu_b': 'new_v', 'new_v_ab_w_out': 'new_v', 'new_v_sb_w_in': 'new_v', 'new_v_sb_w_out': 'new_v', 'new_v_ffn_w1': 'new_v', 'new_v_ffn_w2': 'new_v'}


def _forward(args):
    return _fwd_reference(*[args[k] for k in FWD_PARAMS])


def _output_shape():
    def fwd():
        inp = _fwd_setup_inputs(0)
        return _fwd_reference(*[inp[k] for k in FWD_PARAMS])
    out = _jax.eval_shape(fwd)
    return out.shape, out.dtype

N_MICROBATCH = 1
ADAM_LR = 0.001
ADAM_B1 = 0.9
ADAM_B2 = 0.999
ADAM_EPS = 1e-08
ADAM_WD = 0.01
ADAM_STEP = 10
PER_EXAMPLE_BATCH_AXIS = {'x': 0, 'loss_target': 0}
SHARED_INPUTS = []
_WEIGHT_DTYPES = {'norm_pre_mix': _jnp.float32, 'norm_post_mix': _jnp.float32, 'norm_pre_ffn': _jnp.float32, 'norm_post_ffn': _jnp.float32, 'ab_w_in': _jnp.float32, 'sgu_ln_g': _jnp.float32, 'sgu_ln_b': _jnp.float32, 'sgu_w': _jnp.float32, 'sgu_b': _jnp.float32, 'ab_w_out': _jnp.float32, 'sb_w_in': _jnp.float32, 'sb_w_out': _jnp.float32, 'ffn_w1': _jnp.float32, 'ffn_w2': _jnp.float32}
MOMENT_SCALE = {'norm_pre_mix': 3.790305e+00, 'norm_post_mix': 1.711040e+01, 'norm_pre_ffn': 2.530939e+00, 'norm_post_ffn': 1.749969e+01, 'ab_w_in': 3.401114e-01, 'sgu_ln_g': 3.045926e-01, 'sgu_ln_b': 3.618797e-01, 'sgu_w': 2.574660e-01, 'sgu_b': 5.648661e-01, 'ab_w_out': 5.888389e+00, 'sb_w_in': 3.208320e+00, 'sb_w_out': 5.333265e+00, 'ffn_w1': 1.284415e+00, 'ffn_w2': 6.130340e+00}


def _to_microbatches(a, axis):
    t = _jnp.moveaxis(a, axis, 0)
    t = t.reshape((N_MICROBATCH, t.shape[0] // N_MICROBATCH) + t.shape[1:])
    return _jnp.moveaxis(t, 1, axis + 1)


def setup_inputs(seed: int = 0) -> dict:
    inp = _fwd_setup_inputs(seed)
    key = _jax.random.fold_in(_jax.random.key(seed), 7919)
    shape, _ = _output_shape()
    out = dict(inp)
    out["loss_target"] = _jax.random.normal(_jax.random.fold_in(key, 0), shape, _jnp.float32)
    for i, name in enumerate(TWIN_WEIGHTS):
        w = inp[name].astype(_jnp.float32)
        if MOMENT_SCALE is None:
            s = _jnp.sqrt(_jnp.mean(_jnp.square(w)) + 1e-30)
        else:
            s = MOMENT_SCALE[name]
        km, kv = _jax.random.split(_jax.random.fold_in(key, i + 1))
        out[name] = w
        out["m_" + name] = s * _jax.random.normal(km, w.shape, _jnp.float32)
        out["v_" + name] = (s * s) * _jax.random.uniform(kv, w.shape, _jnp.float32, 0.5, 1.5)
    if N_MICROBATCH > 1:
        for name, axis in PER_EXAMPLE_BATCH_AXIS.items():
            out[name] = _to_microbatches(out[name], axis)
    return {'x': out['x'], 'norm_pre_mix': out['norm_pre_mix'], 'norm_post_mix': out['norm_post_mix'], 'norm_pre_ffn': out['norm_pre_ffn'], 'norm_post_ffn': out['norm_post_ffn'], 'ab_w_in': out['ab_w_in'], 'sgu_ln_g': out['sgu_ln_g'], 'sgu_ln_b': out['sgu_ln_b'], 'sgu_w': out['sgu_w'], 'sgu_b': out['sgu_b'], 'ab_w_out': out['ab_w_out'], 'sb_w_in': out['sb_w_in'], 'sb_w_out': out['sb_w_out'], 'ffn_w1': out['ffn_w1'], 'ffn_w2': out['ffn_w2'], 'loss_target': out['loss_target'], 'm_norm_pre_mix': out['m_norm_pre_mix'], 'm_norm_post_mix': out['m_norm_post_mix'], 'm_norm_pre_ffn': out['m_norm_pre_ffn'], 'm_norm_post_ffn': out['m_norm_post_ffn'], 'm_ab_w_in': out['m_ab_w_in'], 'm_sgu_ln_g': out['m_sgu_ln_g'], 'm_sgu_ln_b': out['m_sgu_ln_b'], 'm_sgu_w': out['m_sgu_w'], 'm_sgu_b': out['m_sgu_b'], 'm_ab_w_out': out['m_ab_w_out'], 'm_sb_w_in': out['m_sb_w_in'], 'm_sb_w_out': out['m_sb_w_out'], 'm_ffn_w1': out['m_ffn_w1'], 'm_ffn_w2': out['m_ffn_w2'], 'v_norm_pre_mix': out['v_norm_pre_mix'], 'v_norm_post_mix': out['v_norm_post_mix'], 'v_norm_pre_ffn': out['v_norm_pre_ffn'], 'v_norm_post_ffn': out['v_norm_post_ffn'], 'v_ab_w_in': out['v_ab_w_in'], 'v_sgu_ln_g': out['v_sgu_ln_g'], 'v_sgu_ln_b': out['v_sgu_ln_b'], 'v_sgu_w': out['v_sgu_w'], 'v_sgu_b': out['v_sgu_b'], 'v_ab_w_out': out['v_ab_w_out'], 'v_sb_w_in': out['v_sb_w_in'], 'v_sb_w_out': out['v_sb_w_out'], 'v_ffn_w1': out['v_ffn_w1'], 'v_ffn_w2': out['v_ffn_w2']}


def _loss(weights, diff, rest, loss_target):
    with _jax.named_scope("forward"):
        args = {**rest, TWIN_DIFF_INPUT: diff, **{k: w.astype(_WEIGHT_DTYPES[k]) for k, w in weights.items()}}
        y = _forward(args)
    with _jax.named_scope("loss_head"):
        err = _jnp.square(y.astype(_jnp.float32) - loss_target)
        return 0.5 * _jnp.sum(_jnp.mean(err, axis=-1)) if err.ndim else 0.5 * err


def _adamw(w, g, m, v):
    m = ADAM_B1 * m + (1.0 - ADAM_B1) * g
    v = ADAM_B2 * v + (1.0 - ADAM_B2) * _jnp.square(g)
    m_hat = m / (1.0 - ADAM_B1 ** ADAM_STEP)
    v_hat = v / (1.0 - ADAM_B2 ** ADAM_STEP)
    delta = -ADAM_LR * (m_hat / (_jnp.sqrt(v_hat) + ADAM_EPS) + ADAM_WD * w)
    return delta, m, v


def reference(x, norm_pre_mix, norm_post_mix, norm_pre_ffn, norm_post_ffn, ab_w_in, sgu_ln_g, sgu_ln_b, sgu_w, sgu_b, ab_w_out, sb_w_in, sb_w_out, ffn_w1, ffn_w2, loss_target, m_norm_pre_mix, m_norm_post_mix, m_norm_pre_ffn, m_norm_post_ffn, m_ab_w_in, m_sgu_ln_g, m_sgu_ln_b, m_sgu_w, m_sgu_b, m_ab_w_out, m_sb_w_in, m_sb_w_out, m_ffn_w1, m_ffn_w2, v_norm_pre_mix, v_norm_post_mix, v_norm_pre_ffn, v_norm_post_ffn, v_ab_w_in, v_sgu_ln_g, v_sgu_ln_b, v_sgu_w, v_sgu_b, v_ab_w_out, v_sb_w_in, v_sb_w_out, v_ffn_w1, v_ffn_w2):
    given = dict(x=x, norm_pre_mix=norm_pre_mix, norm_post_mix=norm_post_mix, norm_pre_ffn=norm_pre_ffn, norm_post_ffn=norm_post_ffn, ab_w_in=ab_w_in, sgu_ln_g=sgu_ln_g, sgu_ln_b=sgu_ln_b, sgu_w=sgu_w, sgu_b=sgu_b, ab_w_out=ab_w_out, sb_w_in=sb_w_in, sb_w_out=sb_w_out, ffn_w1=ffn_w1, ffn_w2=ffn_w2, loss_target=loss_target, m_norm_pre_mix=m_norm_pre_mix, m_norm_post_mix=m_norm_post_mix, m_norm_pre_ffn=m_norm_pre_ffn, m_norm_post_ffn=m_norm_post_ffn, m_ab_w_in=m_ab_w_in, m_sgu_ln_g=m_sgu_ln_g, m_sgu_ln_b=m_sgu_ln_b, m_sgu_w=m_sgu_w, m_sgu_b=m_sgu_b, m_ab_w_out=m_ab_w_out, m_sb_w_in=m_sb_w_in, m_sb_w_out=m_sb_w_out, m_ffn_w1=m_ffn_w1, m_ffn_w2=m_ffn_w2, v_norm_pre_mix=v_norm_pre_mix, v_norm_post_mix=v_norm_post_mix, v_norm_pre_ffn=v_norm_pre_ffn, v_norm_post_ffn=v_norm_post_ffn, v_ab_w_in=v_ab_w_in, v_sgu_ln_g=v_sgu_ln_g, v_sgu_ln_b=v_sgu_ln_b, v_sgu_w=v_sgu_w, v_sgu_b=v_sgu_b, v_ab_w_out=v_ab_w_out, v_sb_w_in=v_sb_w_in, v_sb_w_out=v_sb_w_out, v_ffn_w1=v_ffn_w1, v_ffn_w2=v_ffn_w2)
    weights = {n: given[n] for n in TWIN_WEIGHTS}
    shared = {n: given[n] for n in SHARED_INPUTS}
    per_example = {n: given[n] for n in ['x']}
    grad_fn = _jax.value_and_grad(_loss, argnums=(0, 1))

    def one_microbatch(ex, loss_target):
        ex = dict(ex)
        diff = ex.pop(TWIN_DIFF_INPUT)
        return grad_fn(weights, diff, {**shared, **ex}, loss_target)

    if N_MICROBATCH == 1:
        loss, (grad_w, grad_x) = one_microbatch(per_example, given["loss_target"])
    else:
        def body(carry, xs):
            loss_sum, grad_sum = carry
            l_k, (gw_k, gx_k) = one_microbatch(xs[0], xs[1])
            with _jax.named_scope("update"):
                return (loss_sum + l_k, _jax.tree.map(_jnp.add, grad_sum, gw_k)), gx_k

        init = (_jnp.zeros((), _jnp.float32), _jax.tree.map(_jnp.zeros_like, weights))
        (loss, grad_w), grad_x = _jax.lax.scan(body, init, (per_example, given["loss_target"]))
    with _jax.named_scope("update"):
        delta_w, new_m, new_v = {}, {}, {}
        for n in TWIN_WEIGHTS:
            delta_w[n], new_m[n], new_v[n] = _adamw(weights[n], grad_w[n], given["m_" + n], given["v_" + n])
    return (loss, grad_x, *[grad_w[n] for n in TWIN_WEIGHTS], *[delta_w[n] for n in TWIN_WEIGHTS],
            *[new_m[n] for n in TWIN_WEIGHTS], *[new_v[n] for n in TWIN_WEIGHTS])
```

```python
import functools
import math

import jax
import jax.numpy as jnp
from jax import lax
from jax.experimental import pallas as pl
from jax.experimental.pallas import tpu as pltpu

F32 = jnp.float32
BF16 = jnp.bfloat16

HEAD_DIM = 128
CHUNK = 128
ATT_BLOCK = 128
DILATED_PAIRS = ((128, 1), (512, 4), (2048, 16))
RMS_EPS = 1e-6
LN_EPS = 1e-5
ADAM_LR = 0.001
ADAM_B1 = 0.9
ADAM_B2 = 0.999
ADAM_EPS = 1e-08
ADAM_WD = 0.01
ADAM_STEP = 10
N_DEV = 8
MESH_AXES = ("x", "y", "c")
MASKED = -1e30

V7X_VMEM_BYTES = 64 * 1024 * 1024
VMEM_LIMIT = V7X_VMEM_BYTES - 8 * 1024 * 1024

NN = (((1,), (0,)), ((), ()))
NT = (((1,), (1,)), ((), ()))
TN = (((0,), (0,)), ((), ()))


def _params(*sem):
    return pltpu.CompilerParams(dimension_semantics=sem, vmem_limit_bytes=VMEM_LIMIT)


def _dot(a, b, dims=NN):
    return lax.dot_general(a, b, dims, preferred_element_type=F32)


def _tile(n, preferred):
    if n <= preferred:
        return n
    t = preferred - preferred % 128
    while n % t:
        t -= 128
    assert t > 0, (n, preferred)
    return t


def _matmul(a, b, *, mode, name, out_dtype=F32, tm=1024, tn=512, tk=2048, epi=None, extras=()):
    if mode == "nn":
        (M, K), N = a.shape, b.shape[1]
    elif mode == "nt":
        (M, K), N = a.shape, b.shape[0]
    else:
        (K, M), N = a.shape, b.shape[1]
    tm, tn, tk = _tile(M, tm), _tile(N, tn), _tile(K, tk)
    nk = K // tk
    if mode == "tn":
        a_spec = pl.BlockSpec((tk, tm), lambda i, j, k: (k, i))
    else:
        a_spec = pl.BlockSpec((tm, tk), lambda i, j, k: (i, k))
    if mode == "nt":
        b_spec = pl.BlockSpec((tn, tk), lambda i, j, k: (j, k))
    else:
        b_spec = pl.BlockSpec((tk, tn), lambda i, j, k: (k, j))
    o_spec = pl.BlockSpec((tm, tn), lambda i, j, k: (i, j))
    dims = {"nn": NN, "nt": NT, "tn": TN}[mode]
    n_extra = len(extras)

    def finish(acc, refs):
        j = pl.program_id(1)
        if epi is None:
            return acc
        return epi(acc, j, *[r[...] for r in refs])

    if nk == 1:
        def body(a_ref, b_ref, *rest):
            o_ref = rest[n_extra]
            acc = _dot(a_ref[...], b_ref[...], dims)
            o_ref[...] = finish(acc, rest[:n_extra]).astype(o_ref.dtype)
        scratch = []
    else:
        def body(a_ref, b_ref, *rest):
            o_ref, acc_ref = rest[n_extra], rest[n_extra + 1]
            k = pl.program_id(2)

            @pl.when(k == 0)
            def _():
                acc_ref[...] = jnp.zeros_like(acc_ref)

            acc_ref[...] += _dot(a_ref[...], b_ref[...], dims)

            @pl.when(k == nk - 1)
            def _():
                o_ref[...] = finish(acc_ref[...], rest[:n_extra]).astype(o_ref.dtype)
        scratch = [pltpu.VMEM((tm, tn), F32)]

    return pl.pallas_call(
        body,
        name=name,
        grid=(M // tm, N // tn, nk),
        in_specs=[a_spec, b_spec] + [o_spec] * n_extra,
        out_specs=o_spec,
        out_shape=jax.ShapeDtypeStruct((M, N), out_dtype),
        scratch_shapes=scratch,
        compiler_params=_params("parallel", "parallel", "arbitrary"),
    )(a, b, *extras)


ROWS = 256


def _rms(x):
    return lax.rsqrt(jnp.mean(x * x, axis=-1, keepdims=True) + RMS_EPS)


def _prenorm(x, g, name):
    T, D = x.shape

    def body(x_ref, g_ref, h_ref):
        xv = x_ref[...]
        h_ref[...] = (xv * _rms(xv) * g_ref[...]).astype(BF16)

    row = pl.BlockSpec((ROWS, D), lambda i: (i, 0))
    vec = pl.BlockSpec((1, D), lambda i: (0, 0))
    return pl.pallas_call(
        body, name=name, grid=(T // ROWS,), in_specs=[row, vec], out_specs=row,
        out_shape=jax.ShapeDtypeStruct((T, D), BF16), compiler_params=_params("parallel"),
    )(x, g)


def _postnorm_prenorm(x, y, g_post, g_pre, name):
    T, D = x.shape

    def body(x_ref, y_ref, gp_ref, gn_ref, xo_ref, h_ref):
        yv = y_ref[...]
        xn = x_ref[...] + yv * _rms(yv) * gp_ref[...]
        xo_ref[...] = xn
        h_ref[...] = (xn * _rms(xn) * gn_ref[...]).astype(BF16)

    row = pl.BlockSpec((ROWS, D), lambda i: (i, 0))
    vec = pl.BlockSpec((1, D), lambda i: (0, 0))
    return pl.pallas_call(
        body, name=name, grid=(T // ROWS,), in_specs=[row, row, vec, vec], out_specs=[row, row],
        out_shape=[jax.ShapeDtypeStruct((T, D), F32), jax.ShapeDtypeStruct((T, D), BF16)],
        compiler_params=_params("parallel"),
    )(x, y, g_post, g_pre)


def _postnorm_loss(x, y, g_post, target, name):
    T, D = x.shape

    def body(x_ref, y_ref, gp_ref, t_ref, loss_ref, dx_ref):
        @pl.when(pl.program_id(0) == 0)
        def _():
            loss_ref[...] = jnp.zeros_like(loss_ref)

        yv = y_ref[...]
        err = x_ref[...] + yv * _rms(yv) * gp_ref[...] - t_ref[...]
        dx_ref[...] = err * (1.0 / D)
        loss_ref[...] += 0.5 * jnp.sum(jnp.sum(err * err, axis=-1, keepdims=True) * (1.0 / D))

    row = pl.BlockSpec((ROWS, D), lambda i: (i, 0))
    vec = pl.BlockSpec((1, D), lambda i: (0, 0))
    acc = pl.BlockSpec((8, 128), lambda i: (0, 0))
    return pl.pallas_call(
        body, name=name, grid=(T // ROWS,), in_specs=[row, row, vec, row], out_specs=[acc, row],
        out_shape=[jax.ShapeDtypeStruct((8, 128), F32), jax.ShapeDtypeStruct((T, D), F32)],
        compiler_params=_params("arbitrary"),
    )(x, y, g_post, target)


def _postnorm_bwd(dx, y, g_post, name):
    T, D = dx.shape

    def body(dx_ref, y_ref, g_ref, dy_ref, dg_ref):
        @pl.when(pl.program_id(0) == 0)
        def _():
            dg_ref[...] = jnp.zeros_like(dg_ref)

        yv, dn = y_ref[...], dx_ref[...]
        r = _rms(yv)
        yh = yv * r
        gd = dn * g_ref[...]
        dy_ref[...] = (r * (gd - yh * jnp.mean(yh * gd, axis=-1, keepdims=True))).astype(BF16)
        dg_ref[...] += jnp.sum(dn * yh, axis=0, keepdims=True)

    row = pl.BlockSpec((ROWS, D), lambda i: (i, 0))
    vec = pl.BlockSpec((1, D), lambda i: (0, 0))
    return pl.pallas_call(
        body, name=name, grid=(T // ROWS,), in_specs=[row, row, vec], out_specs=[row, vec],
        out_shape=[jax.ShapeDtypeStruct((T, D), BF16), jax.ShapeDtypeStruct((1, D), F32)],
        compiler_params=_params("arbitrary"),
    )(dx, y, g_post)


def _prenorm_bwd(dx_out, dh, x, g_pre, name):
    T, D = x.shape

    def body(dxo_ref, dh_ref, x_ref, g_ref, dx_ref, dg_ref):
        @pl.when(pl.program_id(0) == 0)
        def _():
            dg_ref[...] = jnp.zeros_like(dg_ref)

        xv, dhv = x_ref[...], dh_ref[...]
        r = _rms(xv)
        xh = xv * r
        gd = dhv * g_ref[...]
        dx_ref[...] = dxo_ref[...] + r * (gd - xh * jnp.mean(xh * gd, axis=-1, keepdims=True))
        dg_ref[...] += jnp.sum(dhv * xh, axis=0, keepdims=True)

    row = pl.BlockSpec((ROWS, D), lambda i: (i, 0))
    vec = pl.BlockSpec((1, D), lambda i: (0, 0))
    return pl.pallas_call(
        body, name=name, grid=(T // ROWS,), in_specs=[row, row, row, vec], out_specs=[row, vec],
        out_shape=[jax.ShapeDtypeStruct((T, D), F32), jax.ShapeDtypeStruct((1, D), F32)],
        compiler_params=_params("arbitrary"),
    )(dx_out, dh, x, g_pre)


_INV_SQRT2 = 1.0 / math.sqrt(2.0)
_INV_SQRT2PI = 1.0 / math.sqrt(2.0 * math.pi)


def _gelu(x):
    return 0.5 * x * (1.0 + lax.erf(x * _INV_SQRT2))


def _gelu_grad(x):
    return 0.5 * (1.0 + lax.erf(x * _INV_SQRT2)) + x * jnp.exp(-0.5 * x * x) * _INV_SQRT2PI


def _layernorm_stats(x):
    mu = jnp.mean(x, axis=-1, keepdims=True)
    xc = x - mu
    rstd = lax.rsqrt(jnp.mean(xc * xc, axis=-1, keepdims=True) + LN_EPS)
    return xc * rstd, rstd


def _tril_mask():
    i = lax.broadcasted_iota(jnp.int32, (CHUNK, CHUNK), 0)
    j = lax.broadcasted_iota(jnp.int32, (CHUNK, CHUNK), 1)
    return j <= i


SGU_ROWS = 512


def _sgu_fwd(z, ln_g, ln_b, w_s, b_t, name):
    T = z.shape[0]
    A = ln_g.shape[1]
    G = A // 128
    rows = min(SGU_ROWS, T)

    def body(u_ref, v_ref, g_ref, b_ref, w_ref, bt_ref, o_ref):
        mask = _tril_mask()
        for c in range(rows // CHUNK):
            rs = pl.ds(c * CHUNK, CHUNK)
            xh, _ = _layernorm_stats(_gelu(v_ref[rs, :]))
            vn = (xh * g_ref[...] + b_ref[...]).astype(BF16)
            for g in range(G):
                cs = pl.ds(g * 128, 128)
                w = jnp.where(mask, w_ref[g], 0.0).astype(BF16)
                mixed = _dot(w, vn[:, g * 128:(g + 1) * 128]) + bt_ref[:, g:g + 1]
                o_ref[rs, cs] = (_gelu(u_ref[rs, cs]) * mixed).astype(BF16)

    return pl.pallas_call(
        body, name=name, grid=(T // rows,),
        in_specs=[
            pl.BlockSpec((rows, A), lambda i: (i, 0)),
            pl.BlockSpec((rows, A), lambda i: (i, 1)),
            pl.BlockSpec((1, A), lambda i: (0, 0)),
            pl.BlockSpec((1, A), lambda i: (0, 0)),
            pl.BlockSpec((G, CHUNK, CHUNK), lambda i: (0, 0, 0)),
            pl.BlockSpec((CHUNK, G), lambda i: (0, 0)),
        ],
        out_specs=pl.BlockSpec((rows, A), lambda i: (i, 0)),
        out_shape=jax.ShapeDtypeStruct((T, A), BF16),
        compiler_params=_params("parallel"),
    )(z, z, ln_g, ln_b, w_s, b_t)


def _sgu_bwd(z, dcat, ln_g, ln_b, w_s, b_t, name):
    T = z.shape[0]
    A = ln_g.shape[1]
    G = A // 128
    rows = min(SGU_ROWS, T)

    def body(u_ref, v_ref, da_ref, g_ref, b_ref, w_ref, bt_ref, dz_ref, dg_ref, db_ref, dw_ref, dbt_ref, dvn_ref):
        @pl.when(pl.program_id(0) == 0)
        def _():
            dg_ref[...] = jnp.zeros_like(dg_ref)
            db_ref[...] = jnp.zeros_like(db_ref)
            dw_ref[...] = jnp.zeros_like(dw_ref)
            dbt_ref[...] = jnp.zeros_like(dbt_ref)

        mask = _tril_mask()
        for c in range(rows // CHUNK):
            rs = pl.ds(c * CHUNK, CHUNK)
            vv = v_ref[rs, :]
            gv = _gelu(vv)
            xh, rstd = _layernorm_stats(gv)
            vn = (xh * g_ref[...] + b_ref[...]).astype(BF16)
            for g in range(G):
                cs = pl.ds(g * 128, 128)
                w = jnp.where(mask, w_ref[g], 0.0).astype(BF16)
                vg = vn[:, g * 128:(g + 1) * 128]
                mixed = _dot(w, vg) + bt_ref[:, g:g + 1]
                uu = u_ref[rs, cs]
                da = da_ref[rs, cs]
                dz_ref[rs, cs] = (da * mixed * _gelu_grad(uu)).astype(BF16)
                dm = da * _gelu(uu)
                dmb = dm.astype(BF16)
                dbt_ref[:, g:g + 1] += jnp.sum(dm, axis=1, keepdims=True)
                dw_ref[g] += jnp.where(mask, _dot(dmb, vg, NT), 0.0)
                dvn_ref[:, cs] = _dot(w, dmb, TN)
            dvn = dvn_ref[...]
            dg_ref[...] += jnp.sum(dvn * xh, axis=0, keepdims=True)
            db_ref[...] += jnp.sum(dvn, axis=0, keepdims=True)
            dxh = dvn * g_ref[...]
            dgv = rstd * (dxh - jnp.mean(dxh, axis=-1, keepdims=True)
                          - xh * jnp.mean(dxh * xh, axis=-1, keepdims=True))
            dz_ref[rs, pl.ds(A, A)] = (dgv * _gelu_grad(vv)).astype(BF16)

    vec = pl.BlockSpec((1, A), lambda i: (0, 0))
    wsp = pl.BlockSpec((G, CHUNK, CHUNK), lambda i: (0, 0, 0))
    bsp = pl.BlockSpec((CHUNK, G), lambda i: (0, 0))
    return pl.pallas_call(
        body, name=name, grid=(T // rows,),
        in_specs=[
            pl.BlockSpec((rows, A), lambda i: (i, 0)),
            pl.BlockSpec((rows, A), lambda i: (i, 1)),
            pl.BlockSpec((rows, A), lambda i: (i, 0)),
            vec, vec, wsp, bsp,
        ],
        out_specs=[pl.BlockSpec((rows, 2 * A), lambda i: (i, 0)), vec, vec, wsp, bsp],
        out_shape=[
            jax.ShapeDtypeStruct((T, 2 * A), BF16),
            jax.ShapeDtypeStruct((1, A), F32),
            jax.ShapeDtypeStruct((1, A), F32),
            jax.ShapeDtypeStruct((G, CHUNK, CHUNK), F32),
            jax.ShapeDtypeStruct((CHUNK, G), F32),
        ],
        scratch_shapes=[pltpu.VMEM((CHUNK, A), F32)],
        compiler_params=_params("arbitrary"),
    )(z, z, dcat, ln_g, ln_b, w_s, b_t)


def _alibi_slope(h, n_heads):
    return 2.0 ** (-8.0 * (h + 1.0) / n_heads)


def _dil_scores(q, k, slope_d, valid, dist):
    s = _dot(q, k, NT) - slope_d * dist
    return jnp.where(valid, s, MASKED)


def _dilated_fwd(z, d, B, name):
    T = z.shape[0]
    H = B // HEAD_DIM
    nb = T // (d * ATT_BLOCK)
    zv = z.reshape(T // d, d * 5 * B)
    scale = HEAD_DIM ** -0.5
    blk = ATT_BLOCK

    def body(q_ref, kp_ref, kc_ref, vp_ref, vc_ref, o_ref, l_ref):
        n = pl.program_id(1)
        qi = lax.broadcasted_iota(jnp.int32, (blk, 2 * blk), 0)
        kj = lax.broadcasted_iota(jnp.int32, (blk, 2 * blk), 1)
        dist = qi + blk - kj
        valid = (dist >= 0) & (dist <= blk) & ((kj >= blk) | (n > 0))
        distf = dist.astype(F32)
        for h in range(H):
            cs = pl.ds(h * HEAD_DIM, HEAD_DIM)
            q = (q_ref[:, cs] * scale).astype(BF16)
            k = jnp.concatenate([kp_ref[:, cs], kc_ref[:, cs]], axis=0).astype(BF16)
            v = jnp.concatenate([vp_ref[:, cs], vc_ref[:, cs]], axis=0).astype(BF16)
            s = _dil_scores(q, k, _alibi_slope(h, H) * d, valid, distf)
            m = jnp.max(s, axis=-1, keepdims=True)
            p = jnp.exp(s - m)
            den = jnp.sum(p, axis=-1, keepdims=True)
            o_ref[:, cs] = _dot(p.astype(BF16), v) / den
            l_ref[:, cs] = jnp.broadcast_to(m + jnp.log(den), (blk, HEAD_DIM))

    def col(unit):
        return lambda r, n: (n, r * 5 + unit)

    def col_prev(unit):
        return lambda r, n: (jnp.maximum(n - 1, 0), r * 5 + unit)

    bs = (blk, B)
    out = pl.BlockSpec(bs, lambda r, n: (n, r))
    o, lse = pl.pallas_call(
        body, name=name, grid=(d, nb),
        in_specs=[pl.BlockSpec(bs, col(2)), pl.BlockSpec(bs, col_prev(3)), pl.BlockSpec(bs, col(3)),
                  pl.BlockSpec(bs, col_prev(4)), pl.BlockSpec(bs, col(4))],
        out_specs=[out, out],
        out_shape=[jax.ShapeDtypeStruct((T // d, d * B), F32)] * 2,
        compiler_params=_params("parallel", "parallel"),
    )(zv, zv, zv, zv, zv)
    return o.reshape(T, B), lse.reshape(T, B)


def _dilated_merge(os_, ls_, name):
    T, B = os_[0].shape

    def body(o1, o2, o3, l1, l2, l3, ob_ref, of_ref, lt_ref):
        a, b, c = l1[...], l2[...], l3[...]
        m = jnp.maximum(jnp.maximum(a, b), c)
        ea, eb, ec = jnp.exp(a - m), jnp.exp(b - m), jnp.exp(c - m)
        tot = ea + eb + ec
        o = (ea * o1[...] + eb * o2[...] + ec * o3[...]) / tot
        of_ref[...] = o
        ob_ref[...] = o.astype(BF16)
        lt_ref[...] = m + jnp.log(tot)

    row = pl.BlockSpec((ROWS, B), lambda i: (i, 0))
    return pl.pallas_call(
        body, name=name, grid=(T // ROWS,), in_specs=[row] * 6, out_specs=[row] * 3,
        out_shape=[jax.ShapeDtypeStruct((T, B), BF16), jax.ShapeDtypeStruct((T, B), F32),
                   jax.ShapeDtypeStruct((T, B), F32)],
        compiler_params=_params("parallel"),
    )(*os_, *ls_)


def _dilated_delta(dcat, o, name):
    T, B = o.shape
    H = B // HEAD_DIM

    def body(do_ref, o_ref, d_ref):
        for h in range(H):
            cs = pl.ds(h * HEAD_DIM, HEAD_DIM)
            dsum = jnp.sum(do_ref[:, cs] * o_ref[:, cs], axis=-1, keepdims=True)
            d_ref[:, cs] = jnp.broadcast_to(dsum, (ROWS, HEAD_DIM))

    row = pl.BlockSpec((ROWS, B), lambda i: (i, 0))
    return pl.pallas_call(
        body, name=name, grid=(T // ROWS,),
        in_specs=[pl.BlockSpec((ROWS, B), lambda i: (i, 1)), row], out_specs=row,
        out_shape=jax.ShapeDtypeStruct((T, B), F32), compiler_params=_params("parallel"),
    )(dcat, o)


def _dilated_dq(z, dcat, lse, delta, d, B, name):
    T = z.shape[0]
    H = B // HEAD_DIM
    nb = T // (d * ATT_BLOCK)
    zv = z.reshape(T // d, d * 5 * B)
    dov = dcat.reshape(T // d, d * 2 * B)
    lv = lse.reshape(T // d, d * B)
    dv_ = delta.reshape(T // d, d * B)
    scale = HEAD_DIM ** -0.5
    blk = ATT_BLOCK

    def body(q_ref, kp_ref, kc_ref, vp_ref, vc_ref, do_ref, l_ref, dl_ref, dq_ref):
        n = pl.program_id(1)
        qi = lax.broadcasted_iota(jnp.int32, (blk, 2 * blk), 0)
        kj = lax.broadcasted_iota(jnp.int32, (blk, 2 * blk), 1)
        dist = qi + blk - kj
        valid = (dist >= 0) & (dist <= blk) & ((kj >= blk) | (n > 0))
        distf = dist.astype(F32)
        for h in range(H):
            cs = pl.ds(h * HEAD_DIM, HEAD_DIM)
            q = (q_ref[:, cs] * scale).astype(BF16)
            k = jnp.concatenate([kp_ref[:, cs], kc_ref[:, cs]], axis=0).astype(BF16)
            v = jnp.concatenate([vp_ref[:, cs], vc_ref[:, cs]], axis=0).astype(BF16)
            s = _dil_scores(q, k, _alibi_slope(h, H) * d, valid, distf)
            p = jnp.exp(s - l_ref[:, h * HEAD_DIM:h * HEAD_DIM + 1])
            dp = _dot(do_ref[:, cs].astype(BF16), v, NT)
            ds = p * (dp - dl_ref[:, h * HEAD_DIM:h * HEAD_DIM + 1])
            dq_ref[:, cs] = _dot(ds.astype(BF16), k)

    def col(unit):
        return lambda r, n: (n, r * 5 + unit)

    def col_prev(unit):
        return lambda r, n: (jnp.maximum(n - 1, 0), r * 5 + unit)

    bs = (blk, B)
    out = pl.BlockSpec(bs, lambda r, n: (n, r))
    dq = pl.pallas_call(
        body, name=name, grid=(d, nb),
        in_specs=[pl.BlockSpec(bs, col(2)), pl.BlockSpec(bs, col_prev(3)), pl.BlockSpec(bs, col(3)),
                  pl.BlockSpec(bs, col_prev(4)), pl.BlockSpec(bs, col(4)),
                  pl.BlockSpec(bs, lambda r, n: (n, r * 2 + 1)), out, out],
        out_specs=out,
        out_shape=jax.ShapeDtypeStruct((T // d, d * B), F32),
        compiler_params=_params("parallel", "parallel"),
    )(zv, zv, zv, zv, zv, dov, lv, dv_)
    return dq.reshape(T, B)


def _dilated_dkv(z, dcat, lse, delta, d, B, name):
    T = z.shape[0]
    H = B // HEAD_DIM
    nb = T // (d * ATT_BLOCK)
    zv = z.reshape(T // d, d * 5 * B)
    dov = dcat.reshape(T // d, d * 2 * B)
    lv = lse.reshape(T // d, d * B)
    dv_ = delta.reshape(T // d, d * B)
    scale = HEAD_DIM ** -0.5
    blk = ATT_BLOCK

    def body(k_ref, v_ref, qa_ref, qb_ref, doa_ref, dob_ref, la_ref, lb_ref, da_ref, db_ref, dk_ref, dv_ref):
        m = pl.program_id(1)
        qi = lax.broadcasted_iota(jnp.int32, (2 * blk, blk), 0)
        kj = lax.broadcasted_iota(jnp.int32, (2 * blk, blk), 1)
        dist = qi - kj
        valid = (dist >= 0) & (dist <= blk) & ((qi < blk) | (m + 1 < nb))
        distf = dist.astype(F32)
        for h in range(H):
            cs = pl.ds(h * HEAD_DIM, HEAD_DIM)
            c1 = slice(h * HEAD_DIM, h * HEAD_DIM + 1)
            q = (jnp.concatenate([qa_ref[:, cs], qb_ref[:, cs]], axis=0) * scale).astype(BF16)
            do = jnp.concatenate([doa_ref[:, cs], dob_ref[:, cs]], axis=0).astype(BF16)
            lse2 = jnp.concatenate([la_ref[:, c1], lb_ref[:, c1]], axis=0)
            dl2 = jnp.concatenate([da_ref[:, c1], db_ref[:, c1]], axis=0)
            k = k_ref[:, cs].astype(BF16)
            v = v_ref[:, cs].astype(BF16)
            s = _dil_scores(q, k, _alibi_slope(h, H) * d, valid, distf)
            p = jnp.exp(jnp.where(valid, s - lse2, MASKED))
            dv_ref[:, cs] = _dot(p.astype(BF16), do, TN)
            dp = _dot(do, v, NT)
            ds = p * (dp - dl2)
            dk_ref[:, cs] = _dot(ds.astype(BF16), q, TN)

    def col(unit):
        return lambda r, m: (m, r * 5 + unit)

    def nxt(width, unit):
        return lambda r, m: (jnp.minimum(m + 1, nb - 1), r * width + unit)

    bs = (blk, B)
    out = pl.BlockSpec(bs, lambda r, m: (m, r))
    dk, dv = pl.pallas_call(
        body, name=name, grid=(d, nb),
        in_specs=[pl.BlockSpec(bs, col(3)), pl.BlockSpec(bs, col(4)),
                  pl.BlockSpec(bs, col(2)), pl.BlockSpec(bs, nxt(5, 2)),
                  pl.BlockSpec(bs, lambda r, m: (m, r * 2 + 1)), pl.BlockSpec(bs, nxt(2, 1)),
                  out, pl.BlockSpec(bs, nxt(1, 0)), out, pl.BlockSpec(bs, nxt(1, 0))],
        out_specs=[out, out],
        out_shape=[jax.ShapeDtypeStruct((T // d, d * B), F32)] * 2,
        compiler_params=_params("parallel", "parallel"),
    )(zv, zv, zv, zv, dov, dov, lv, lv, dv_, dv_)
    return dk.reshape(T, B), dv.reshape(T, B)


def _dilated_combine(dqs, dks, dvs, name):
    T, B = dqs[0].shape
    scale = HEAD_DIM ** -0.5

    def body(q1, q2, q3, k1, k2, k3, v1, v2, v3, o_ref):
        o_ref[:, pl.ds(0, B)] = ((q1[...] + q2[...] + q3[...]) * scale).astype(BF16)
        o_ref[:, pl.ds(B, B)] = (k1[...] + k2[...] + k3[...]).astype(BF16)
        o_ref[:, pl.ds(2 * B, B)] = (v1[...] + v2[...] + v3[...]).astype(BF16)

    row = pl.BlockSpec((ROWS, B), lambda i: (i, 0))
    return pl.pallas_call(
        body, name=name, grid=(T // ROWS,), in_specs=[row] * 9,
        out_specs=pl.BlockSpec((ROWS, 3 * B), lambda i: (i, 0)),
        out_shape=jax.ShapeDtypeStruct((T, 3 * B), BF16), compiler_params=_params("parallel"),
    )(*dqs, *dks, *dvs)


def _split_dot(x, tri):
    hi = x.astype(BF16)
    lo = (x - hi.astype(F32)).astype(BF16)
    return _dot(hi, tri) + _dot(lo, tri)


def _log_sigmoids(z):
    e = jnp.exp(-jnp.abs(z))
    sp = jnp.log(1.0 + e)
    return jnp.minimum(z, 0.0) - sp, jnp.minimum(-z, 0.0) - sp, e


def _sb_fwd(qkv, W, name):
    T = qkv.shape[0]
    H = W // HEAD_DIM
    nb = T // ATT_BLOCK
    blk = ATT_BLOCK

    def body(q_ref, k_ref, v_ref, o_ref, lt_ref):
        i = pl.program_id(1)
        q = q_ref[...]
        rows = lax.broadcasted_iota(jnp.int32, (blk, blk), 0)
        cols = lax.broadcasted_iota(jnp.int32, (blk, blk), 1)
        later = (rows > cols).astype(BF16)
        causal = cols < rows

        def tile(j, carry, o, diag):
            ks = pl.ds(pl.multiple_of(j * blk, blk), blk)
            z = _dot(q, k_ref[ks, :], NT)
            ls, lm, _ = _log_sigmoids(z)
            if diag:
                lm = jnp.where(causal, lm, 0.0)
            a = jnp.exp(ls + _split_dot(lm, later) + carry)
            if diag:
                a = jnp.where(causal, a, 0.0)
            o = o + _dot(a.astype(BF16), v_ref[ks, :])
            return carry + jnp.sum(lm, axis=-1, keepdims=True), o

        carry, o = tile(i, jnp.zeros((blk, 1), F32), jnp.zeros((blk, HEAD_DIM), F32), True)
        carry, o = lax.fori_loop(0, i, lambda jj, c: tile(i - 1 - jj, c[0], c[1], False), (carry, o))
        o_ref[...] = o.astype(BF16)
        lt_ref[...] = jnp.broadcast_to(carry, (blk, HEAD_DIM))

    qs = pl.BlockSpec((blk, HEAD_DIM), lambda h, i: (i, h))
    return pl.pallas_call(
        body, name=name, grid=(H, nb),
        in_specs=[qs, pl.BlockSpec((T, HEAD_DIM), lambda h, i: (0, H + h)),
                  pl.BlockSpec((T, HEAD_DIM), lambda h, i: (0, 2 * H + h))],
        out_specs=[qs, qs],
        out_shape=[jax.ShapeDtypeStruct((T, W), BF16), jax.ShapeDtypeStruct((T, W), F32)],
        compiler_params=_params("parallel", "arbitrary"),
    )(qkv, qkv, qkv)


def _sb_bwd(qkv, do, ltot, W, name):
    T = qkv.shape[0]
    H = W // HEAD_DIM
    nb = T // ATT_BLOCK
    blk = ATT_BLOCK

    def body(q_ref, k_ref, v_ref, do_ref, lt_ref, dq_ref, dk_ref, dv_ref):
        i = pl.program_id(1)

        @pl.when(i == 0)
        def _():
            dk_ref[...] = jnp.zeros_like(dk_ref)
            dv_ref[...] = jnp.zeros_like(dv_ref)

        q = q_ref[...]
        do = do_ref[...]
        ltot = lt_ref[:, 0:1]
        rows = lax.broadcasted_iota(jnp.int32, (blk, blk), 0)
        cols = lax.broadcasted_iota(jnp.int32, (blk, blk), 1)
        upto = (rows <= cols).astype(BF16)
        before = (rows < cols).astype(BF16)
        causal = cols < rows

        def tile(j, c, diag):
            plm, pg, dq = c
            ks = pl.ds(pl.multiple_of(j * blk, blk), blk)
            k = k_ref[ks, :]
            v = v_ref[ks, :]
            z = _dot(q, k, NT)
            ls, lm, e = _log_sigmoids(z)
            if diag:
                lm = jnp.where(causal, lm, 0.0)
            later_sum = ltot - (plm + _split_dot(lm, upto))
            a = jnp.exp(ls + later_sum)
            if diag:
                a = jnp.where(causal, a, 0.0)
            g = a * _dot(do, v, NT)
            gsum = pg + _split_dot(g, before)
            r = 1.0 / (1.0 + e)
            pos = z >= 0.0
            sig = jnp.where(pos, r, e * r)
            nsig = jnp.where(pos, e * r, r)
            dz = g * nsig - gsum * sig
            if diag:
                dz = jnp.where(causal, dz, 0.0)
            dzb = dz.astype(BF16)
            dk_ref[ks, :] += _dot(dzb, q, TN)
            dv_ref[ks, :] += _dot(a.astype(BF16), do, TN)
            return (plm + jnp.sum(lm, axis=-1, keepdims=True), pg + jnp.sum(g, axis=-1, keepdims=True),
                    dq + _dot(dzb, k))

        zero = jnp.zeros((blk, 1), F32)
        c = lax.fori_loop(0, i, lambda j, c: tile(j, c, False), (zero, zero, jnp.zeros((blk, HEAD_DIM), F32)))
        c = tile(i, c, True)
        dq_ref[...] = c[2]

    qs = pl.BlockSpec((blk, HEAD_DIM), lambda h, i: (i, h))
    res = pl.BlockSpec((T, HEAD_DIM), lambda h, i: (0, h))
    return pl.pallas_call(
        body, name=name, grid=(H, nb),
        in_specs=[qs, pl.BlockSpec((T, HEAD_DIM), lambda h, i: (0, H + h)),
                  pl.BlockSpec((T, HEAD_DIM), lambda h, i: (0, 2 * H + h)), qs, qs],
        out_specs=[qs, res, res],
        out_shape=[jax.ShapeDtypeStruct((T, W), F32)] * 3,
        compiler_params=_params("parallel", "arbitrary"),
    )(qkv, qkv, qkv, do, ltot)


def _sb_pack(dq, dk, dv, name):
    T, W = dq.shape
    scale = HEAD_DIM ** -0.5

    def body(q_ref, k_ref, v_ref, o_ref):
        o_ref[:, pl.ds(0, W)] = (q_ref[...] * scale).astype(BF16)
        o_ref[:, pl.ds(W, W)] = k_ref[...].astype(BF16)
        o_ref[:, pl.ds(2 * W, W)] = v_ref[...].astype(BF16)

    row = pl.BlockSpec((ROWS, W), lambda i: (i, 0))
    return pl.pallas_call(
        body, name=name, grid=(T // ROWS,), in_specs=[row] * 3,
        out_specs=pl.BlockSpec((ROWS, 3 * W), lambda i: (i, 0)),
        out_shape=jax.ShapeDtypeStruct((T, 3 * W), BF16), compiler_params=_params("parallel"),
    )(dq, dk, dv)


def _local_step(x, target, norms, sgu, w):
    T, D = x.shape
    A = D // 2
    pre_mix, post_mix, pre_ffn, post_ffn = norms
    ln_g, ln_b, w_s, b_s = sgu
    b_t = b_s.T
    scale = HEAD_DIM ** -0.5

    def vec(p, layer):
        return p[layer:layer + 1]

    h0 = _prenorm(x, vec(pre_mix, 0), "prenorm0")
    z = _matmul(h0, w["ab_in"], mode="nn", name="ab_in_fwd")
    a_out = _sgu_fwd(z, ln_g, ln_b, w_s, b_t, "sgu_fwd")
    branch = [_dilated_fwd(z, d, A, f"dilated_fwd_{d}") for _, d in DILATED_PAIRS]
    b_out, o_dil, lse_dil = _dilated_merge([b[0] for b in branch], [b[1] for b in branch], "dilated_merge")
    cat = jnp.concatenate([a_out, b_out], axis=-1)
    y0 = _matmul(cat, w["ab_out"], mode="nn", name="ab_out_fwd")
    x1, h1 = _postnorm_prenorm(x, y0, vec(post_mix, 0), vec(pre_ffn, 0), "norm_mix0")

    def relu2(acc, j):
        r = jnp.maximum(acc, 0.0)
        return r * r

    f0 = _matmul(h1, w["w1"][0], mode="nn", name="ffn0_w1_fwd", out_dtype=BF16, epi=relu2)
    y1 = _matmul(f0, w["w2"][0], mode="nn", name="ffn0_w2_fwd")
    x2, h2 = _postnorm_prenorm(x1, y1, vec(post_ffn, 0), vec(pre_mix, 1), "norm_ffn0")

    tn_qkv = _tile(D, 512)
    nq = D // tn_qkv

    def scale_q(acc, j):
        return jnp.where(j < nq, acc * scale, acc)

    qkv = _matmul(h2, w["sb_in"], mode="nn", name="sb_in_fwd", out_dtype=BF16, tn=tn_qkv, epi=scale_q)
    o_sb, ltot = _sb_fwd(qkv, D, "sb_fwd")
    y2 = _matmul(o_sb, w["sb_out"], mode="nn", name="sb_out_fwd")
    x3, h3 = _postnorm_prenorm(x2, y2, vec(post_mix, 1), vec(pre_ffn, 1), "norm_mix1")
    f1 = _matmul(h3, w["w1"][1], mode="nn", name="ffn1_w1_fwd", out_dtype=BF16, epi=relu2)
    y3 = _matmul(f1, w["w2"][1], mode="nn", name="ffn1_w2_fwd")
    loss_tile, dx4 = _postnorm_loss(x3, y3, vec(post_ffn, 1), target, "norm_loss")
    loss = loss_tile[0, 0]

    def relu2_bwd(acc, j, f):
        return acc * (2.0 * jnp.sqrt(f.astype(F32)))

    def ffn_bwd(dx_out, x_in, h, f, y, layer, tag):
        dy, dg_post = _postnorm_bwd(dx_out, y, vec(post_ffn, layer), f"ffn{tag}_postnorm_bwd")
        g_w2 = _matmul(f, dy, mode="tn", name=f"ffn{tag}_w2_wgrad", tm=1024, tn=1024, tk=1024)
        da = _matmul(dy, w["w2"][layer], mode="nt", name=f"ffn{tag}_w2_dgrad", out_dtype=BF16,
                     epi=relu2_bwd, extras=(f,))
        g_w1 = _matmul(h, da, mode="tn", name=f"ffn{tag}_w1_wgrad", tm=1024, tn=1024, tk=1024)
        dh = _matmul(da, w["w1"][layer], mode="nt", name=f"ffn{tag}_w1_dgrad")
        dx_in, dg_pre = _prenorm_bwd(dx_out, dh, x_in, vec(pre_ffn, layer), f"ffn{tag}_prenorm_bwd")
        return dx_in, g_w1, g_w2, dg_pre, dg_post

    dx3, g_w1_1, g_w2_1, dg_pre_ffn1, dg_post_ffn1 = ffn_bwd(dx4, x3, h3, f1, y3, 1, "1")

    dy2, dg_post_mix1 = _postnorm_bwd(dx3, y2, vec(post_mix, 1), "sb_postnorm_bwd")
    g_sb_out = _matmul(o_sb, dy2, mode="tn", name="sb_out_wgrad", tm=1024, tn=1024, tk=1024)
    do_sb = _matmul(dy2, w["sb_out"], mode="nt", name="sb_out_dgrad", out_dtype=BF16)
    dq, dk, dv = _sb_bwd(qkv, do_sb, ltot, D, "sb_bwd")
    dqkv = _sb_pack(dq, dk, dv, "sb_pack")
    g_sb_in = _matmul(h2, dqkv, mode="tn", name="sb_in_wgrad", tm=1024, tn=1024, tk=1024)
    dh2 = _matmul(dqkv, w["sb_in"], mode="nt", name="sb_in_dgrad")
    dx2, dg_pre_mix1 = _prenorm_bwd(dx3, dh2, x2, vec(pre_mix, 1), "sb_prenorm_bwd")

    dx1, g_w1_0, g_w2_0, dg_pre_ffn0, dg_post_ffn0 = ffn_bwd(dx2, x1, h1, f0, y1, 0, "0")

    dy0, dg_post_mix0 = _postnorm_bwd(dx1, y0, vec(post_mix, 0), "ab_postnorm_bwd")
    g_ab_out = _matmul(cat, dy0, mode="tn", name="ab_out_wgrad", tm=1024, tn=1024, tk=1024)
    dcat = _matmul(dy0, w["ab_out"], mode="nt", name="ab_out_dgrad")
    duv, d_ln_g, d_ln_b, d_w_s, d_b_t = _sgu_bwd(z, dcat, ln_g, ln_b, w_s, b_t, "sgu_bwd")
    delta = _dilated_delta(dcat, o_dil, "dilated_delta")
    dqs, dks, dvs = [], [], []
    for _, d in DILATED_PAIRS:
        dqs.append(_dilated_dq(z, dcat, lse_dil, delta, d, A, f"dilated_dq_{d}"))
        dk_b, dv_b = _dilated_dkv(z, dcat, lse_dil, delta, d, A, f"dilated_dkv_{d}")
        dks.append(dk_b)
        dvs.append(dv_b)
    dqkv0 = _dilated_combine(dqs, dks, dvs, "dilated_combine")
    dz = jnp.concatenate([duv, dqkv0], axis=-1)
    g_ab_in = _matmul(h0, dz, mode="tn", name="ab_in_wgrad", tm=1024, tn=1024, tk=1024)
    dh0 = _matmul(dz, w["ab_in"], mode="nt", name="ab_in_dgrad")
    dx0, dg_pre_mix0 = _prenorm_bwd(dx1, dh0, x, vec(pre_mix, 0), "ab_prenorm_bwd")

    grads = {"ab_in": g_ab_in, "ab_out": g_ab_out, "sb_in": g_sb_in, "sb_out": g_sb_out,
             "w1": (g_w1_0, g_w1_1), "w2": (g_w2_0, g_w2_1)}
    small = {
        "pre_mix": jnp.concatenate([dg_pre_mix0, dg_pre_mix1], axis=0),
        "post_mix": jnp.concatenate([dg_post_mix0, dg_post_mix1], axis=0),
        "pre_ffn": jnp.concatenate([dg_pre_ffn0, dg_pre_ffn1], axis=0),
        "post_ffn": jnp.concatenate([dg_post_ffn0, dg_post_ffn1], axis=0),
        "ln_g": d_ln_g, "ln_b": d_ln_b, "w_s": d_w_s, "b_s": d_b_t.T,
    }
    return loss, dx0, grads, small


MESH_ID = pl.DeviceIdType.MESH
ANY = pl.BlockSpec(memory_space=pl.ANY)


def _coords():
    return lax.axis_index("x"), lax.axis_index("y"), lax.axis_index("c")


def _shard_of(ref, kind, p):
    if kind == "col":
        n = ref.shape[1] // N_DEV
        return ref.at[:, pl.ds(pl.multiple_of(p * n, 128), n)]
    r = ref.shape[0] // N_DEV
    return ref.at[pl.ds(pl.multiple_of(p * r, 16), r), :]


def _full_shape(shard, kind):
    if kind == "col":
        return (shard.shape[0], shard.shape[1] * N_DEV)
    return (shard.shape[0] * N_DEV, shard.shape[1])


def _all_gather(shards, kinds):
    nt = len(shards)

    def body(*refs):
        ins, outs = refs[:nt], refs[nt:2 * nt]
        send_sems, recv_sems, local_sems = refs[2 * nt:]
        x, y, c = _coords()
        me, sibling = (x, y, c), (x, y, 1 - c)
        chips = [(1 - x, y), (x, 1 - y), (1 - x, 1 - y)]

        def slot(t, px, py, pc):
            return _shard_of(outs[t], kinds[t], 4 * px + 2 * py + pc)

        def copy(t, k, block, to, src=None):
            return pltpu.make_async_remote_copy(
                src_ref=slot(t, *block) if src is None else src, dst_ref=slot(t, *block),
                send_sem=send_sems.at[7 * t + k], recv_sem=recv_sems.at[7 * t + k],
                device_id=to, device_id_type=MESH_ID)

        local, sent = [], []
        for t in range(nt):
            local.append(pltpu.make_async_copy(ins[t], slot(t, *me), local_sems.at[t]))
            local[-1].start()
            first = [copy(t, 0, me, sibling, src=ins[t])]
            first += [copy(t, 1 + j, me, (*chip, c), src=ins[t]) for j, chip in enumerate(chips)]
            for cp in first:
                cp.start()
            sent += first
        for t in range(nt):
            for j, chip in enumerate(chips):
                copy(t, 1 + j, (*chip, c), me).wait_recv()
                sent.append(copy(t, 4 + j, (*chip, c), sibling))
                sent[-1].start()
        for t in range(nt):
            copy(t, 0, sibling, me).wait_recv()
            for j, chip in enumerate(chips):
                copy(t, 4 + j, (*chip, 1 - c), me).wait_recv()
        for cp in sent:
            cp.wait_send()
        for cp in local:
            cp.wait()

    return pl.pallas_call(
        body, name="all_gather_weights",
        in_specs=[ANY] * nt, out_specs=[ANY] * nt,
        out_shape=[jax.ShapeDtypeStruct(_full_shape(s, k), s.dtype) for s, k in zip(shards, kinds)],
        scratch_shapes=[pltpu.SemaphoreType.DMA((7 * nt,)), pltpu.SemaphoreType.DMA((7 * nt,)),
                        pltpu.SemaphoreType.DMA((nt,))],
    )(*shards)


def _shard_shape(full, kind):
    if kind == "col":
        return (full.shape[0], full.shape[1] // N_DEV)
    return (full.shape[0] // N_DEV, full.shape[1])


CHIPS = ((0, 0), (0, 1), (1, 0), (1, 1))


def _exchange_siblings(grads, kinds):
    nt = len(grads)

    def body(*refs):
        ins, outs = refs[:nt], refs[nt:2 * nt]
        send_sems, recv_sems = refs[2 * nt:]
        x, y, c = _coords()
        copies = []
        for t in range(nt):
            for q, (qx, qy) in enumerate(CHIPS):
                copies.append(pltpu.make_async_remote_copy(
                    src_ref=_shard_of(ins[t], kinds[t], 4 * qx + 2 * qy + (1 - c)), dst_ref=outs[t].at[q],
                    send_sem=send_sems.at[4 * t + q], recv_sem=recv_sems.at[4 * t + q],
                    device_id=(x, y, 1 - c), device_id_type=MESH_ID))
        for cp in copies:
            cp.start()
        for cp in copies:
            cp.wait()

    return pl.pallas_call(
        body, name="grad_exchange_siblings",
        in_specs=[ANY] * nt, out_specs=[ANY] * nt,
        out_shape=[jax.ShapeDtypeStruct((4,) + _shard_shape(g, k), F32) for g, k in zip(grads, kinds)],
        scratch_shapes=[pltpu.SemaphoreType.DMA((4 * nt,)), pltpu.SemaphoreType.DMA((4 * nt,))],
    )(*grads)


def _pair_sum(grad, other, kind, core, name):
    rows, cols = _shard_shape(grad, kind)
    tr = _tile(rows, 256)

    def body(c_ref, g_ref, o_ref, s_ref):
        s_ref[...] = (g_ref[...] + o_ref[...]).astype(BF16)

    if kind == "col":
        g_spec = pl.BlockSpec((tr, cols), lambda q, i, c_ref: (i, 2 * q + c_ref[0]))
    else:
        g_spec = pl.BlockSpec((tr, cols), lambda q, i, c_ref: ((2 * q + c_ref[0]) * (rows // tr) + i, 0))
    part = pl.BlockSpec((None, tr, cols), lambda q, i, c_ref: (q, i, 0))
    return pl.pallas_call(
        body, name=name,
        grid_spec=pltpu.PrefetchScalarGridSpec(
            num_scalar_prefetch=1, grid=(4, rows // tr), in_specs=[g_spec, part], out_specs=part),
        out_shape=jax.ShapeDtypeStruct((4, rows, cols), BF16),
        compiler_params=_params("parallel", "parallel"),
    )(core, grad, other)


def _exchange_chips(parts, groups):
    nt = len(parts)
    ng = len(groups)
    place = {t: (gi, li) for gi, grp in enumerate(groups) for li, t in enumerate(grp)}

    def body(*refs):
        ins, outs = refs[:nt], refs[nt:nt + ng]
        send_sems, recv_sems, local_sems = refs[nt + ng:]
        x, y, c = _coords()
        myq = 2 * x + y
        chips = [(1 - x, y), (x, 1 - y), (1 - x, 1 - y)]
        local, sent = [], []
        for t in range(nt):
            gi, li = place[t]
            land = outs[gi].at[myq, li]
            local.append(pltpu.make_async_copy(ins[t].at[myq], land, local_sems.at[t]))
            local[-1].start()
            for j, (qx, qy) in enumerate(chips):
                sent.append(pltpu.make_async_remote_copy(
                    src_ref=ins[t].at[2 * qx + qy], dst_ref=land,
                    send_sem=send_sems.at[3 * t + j], recv_sem=recv_sems.at[3 * t + j],
                    device_id=(qx, qy, c), device_id_type=MESH_ID))
                sent[-1].start()
        for cp in sent:
            cp.wait()
        for cp in local:
            cp.wait()

    out_shape = [jax.ShapeDtypeStruct((4, len(grp)) + parts[grp[0]].shape[1:], BF16) for grp in groups]
    return pl.pallas_call(
        body, name="grad_exchange_chips",
        in_specs=[ANY] * nt, out_specs=[ANY] * ng, out_shape=out_shape,
        scratch_shapes=[pltpu.SemaphoreType.DMA((3 * nt,)), pltpu.SemaphoreType.DMA((3 * nt,)),
                        pltpu.SemaphoreType.DMA((nt,))],
    )(*parts)


def _all_reduce_small(vec):
    R = vec.shape[0]

    def body(v_ref, o_ref, recv_ref, send_sems, recv_sems):
        x, y, c = _coords()
        me = 4 * x + 2 * y + c
        recv_ref[me] = v_ref[...]
        copies = []
        for k in range(1, N_DEV):
            bx, by, bc = (k >> 2) & 1, (k >> 1) & 1, k & 1
            peer = (1 - x if bx else x, 1 - y if by else y, 1 - c if bc else c)
            copies.append(pltpu.make_async_remote_copy(
                src_ref=v_ref, dst_ref=recv_ref.at[me],
                send_sem=send_sems.at[k - 1], recv_sem=recv_sems.at[k - 1],
                device_id=peer, device_id_type=MESH_ID))
        for cp in copies:
            cp.start()
        for cp in copies:
            cp.wait()
        total = recv_ref[0]
        for p in range(1, N_DEV):
            total = total + recv_ref[p]
        o_ref[...] = total

    return pl.pallas_call(
        body, name="all_reduce_small",
        in_specs=[pl.BlockSpec(memory_space=pltpu.VMEM)], out_specs=pl.BlockSpec(memory_space=pltpu.VMEM),
        out_shape=jax.ShapeDtypeStruct((R, 128), F32),
        scratch_shapes=[pltpu.VMEM((N_DEV, R, 128), F32), pltpu.SemaphoreType.DMA((N_DEV - 1,)),
                        pltpu.SemaphoreType.DMA((N_DEV - 1,))],
        compiler_params=pltpu.CompilerParams(vmem_limit_bytes=VMEM_LIMIT),
    )(vec)


def _adamw_math(w, g, m, v):
    m = ADAM_B1 * m + (1.0 - ADAM_B1) * g
    v = ADAM_B2 * v + (1.0 - ADAM_B2) * (g * g)
    m_hat = m / (1.0 - ADAM_B1 ** ADAM_STEP)
    v_hat = v / (1.0 - ADAM_B2 ** ADAM_STEP)
    delta = -ADAM_LR * (m_hat / (jnp.sqrt(v_hat) + ADAM_EPS) + ADAM_WD * w)
    return delta, m, v


def _adamw(w, parts, m, v, name):
    rows, cols = w.shape
    n = parts.shape[0]
    tr = rows if rows * cols <= 256 * 1024 else _tile(rows, 128)

    def body(w_ref, p_ref, m_ref, v_ref, g_ref, d_ref, mo_ref, vo_ref):
        g = p_ref[0].astype(F32)
        for q in range(1, n):
            g = g + p_ref[q].astype(F32)
        g_ref[...] = g
        d_ref[...], mo_ref[...], vo_ref[...] = _adamw_math(w_ref[...], g, m_ref[...], v_ref[...])

    blk = pl.BlockSpec((tr, cols), lambda i: (i, 0))
    return pl.pallas_call(
        body, name=name, grid=(rows // tr,),
        in_specs=[blk, pl.BlockSpec((n, tr, cols), lambda i: (0, i, 0)), blk, blk],
        out_specs=[blk] * 4, out_shape=[jax.ShapeDtypeStruct((rows, cols), F32)] * 4,
        compiler_params=_params("parallel"),
    )(w, parts, m, v)


def _pack(arrays):
    rows = []
    for a in arrays:
        flat = a.reshape(-1)
        pad = (-flat.shape[0]) % 1024
        rows.append(jnp.pad(flat, (0, pad)).reshape(-1, 128))
    return jnp.concatenate(rows, axis=0)


def _unpack(packed, like):
    out, r = [], 0
    for a in like:
        n = math.prod(a.shape)
        nr = (n + 1023) // 1024 * 8
        out.append(packed[r:r + nr].reshape(-1)[:n].reshape(a.shape))
        r += nr
    return out


SMALL = ("norm_pre_mix", "norm_post_mix", "norm_pre_ffn", "norm_post_ffn", "sgu_ln_g", "sgu_ln_b", "sgu_w", "sgu_b")
ORDER = ("norm_pre_mix", "norm_post_mix", "norm_pre_ffn", "norm_post_ffn", "ab_w_in", "sgu_ln_g", "sgu_ln_b", "sgu_w",
         "sgu_b", "ab_w_out", "sb_w_in", "sb_w_out", "ffn_w1", "ffn_w2")


def kernel(x, norm_pre_mix, norm_post_mix, norm_pre_ffn, norm_post_ffn, ab_w_in, sgu_ln_g, sgu_ln_b, sgu_w, sgu_b, ab_w_out, sb_w_in, sb_w_out, ffn_w1, ffn_w2, loss_target, m_norm_pre_mix, m_norm_post_mix, m_norm_pre_ffn, m_norm_post_ffn, m_ab_w_in, m_sgu_ln_g, m_sgu_ln_b, m_sgu_w, m_sgu_b, m_ab_w_out, m_sb_w_in, m_sb_w_out, m_ffn_w1, m_ffn_w2, v_norm_pre_mix, v_norm_post_mix, v_norm_pre_ffn, v_norm_post_ffn, v_ab_w_in, v_sgu_ln_g, v_sgu_ln_b, v_sgu_w, v_sgu_b, v_ab_w_out, v_sb_w_in, v_sb_w_out, v_ffn_w1, v_ffn_w2):
    W = dict(norm_pre_mix=norm_pre_mix, norm_post_mix=norm_post_mix, norm_pre_ffn=norm_pre_ffn,
             norm_post_ffn=norm_post_ffn, ab_w_in=ab_w_in, sgu_ln_g=sgu_ln_g, sgu_ln_b=sgu_ln_b, sgu_w=sgu_w,
             sgu_b=sgu_b, ab_w_out=ab_w_out, sb_w_in=sb_w_in, sb_w_out=sb_w_out, ffn_w1=ffn_w1, ffn_w2=ffn_w2)
    M = dict(norm_pre_mix=m_norm_pre_mix, norm_post_mix=m_norm_post_mix, norm_pre_ffn=m_norm_pre_ffn,
             norm_post_ffn=m_norm_post_ffn, ab_w_in=m_ab_w_in, sgu_ln_g=m_sgu_ln_g, sgu_ln_b=m_sgu_ln_b,
             sgu_w=m_sgu_w, sgu_b=m_sgu_b, ab_w_out=m_ab_w_out, sb_w_in=m_sb_w_in, sb_w_out=m_sb_w_out,
             ffn_w1=m_ffn_w1, ffn_w2=m_ffn_w2)
    V = dict(norm_pre_mix=v_norm_pre_mix, norm_post_mix=v_norm_post_mix, norm_pre_ffn=v_norm_pre_ffn,
             norm_post_ffn=v_norm_post_ffn, ab_w_in=v_ab_w_in, sgu_ln_g=v_sgu_ln_g, sgu_ln_b=v_sgu_ln_b,
             sgu_w=v_sgu_w, sgu_b=v_sgu_b, ab_w_out=v_ab_w_out, sb_w_in=v_sb_w_in, sb_w_out=v_sb_w_out,
             ffn_w1=v_ffn_w1, ffn_w2=v_ffn_w2)

    shards = [ab_w_in[0], ab_w_out[0], sb_w_in[0], sb_w_out[0], ffn_w1[0], ffn_w1[1], ffn_w2[0], ffn_w2[1]]
    kinds = ["col", "row", "col", "row", "col", "col", "row", "row"]
    full = _all_gather([s.astype(BF16) for s in shards], kinds)
    w = {"ab_in": full[0], "ab_out": full[1], "sb_in": full[2], "sb_out": full[3],
         "w1": (full[4], full[5]), "w2": (full[6], full[7])}

    norms = (norm_pre_mix, norm_post_mix, norm_pre_ffn, norm_post_ffn)
    sgu = (sgu_ln_g, sgu_ln_b, sgu_w[0], sgu_b[0])
    loss, dx, g, small = _local_step(x[0], loss_target[0], norms, sgu, w)
    loss = lax.psum(loss, MESH_AXES)

    grads = [g["ab_in"], g["ab_out"], g["sb_in"], g["sb_out"], g["w1"][0], g["w1"][1], g["w2"][0], g["w2"][1]]
    names = ["ab_in", "ab_out", "sb_in", "sb_out", "w1_0", "w1_1", "w2_0", "w2_1"]
    others = _exchange_siblings(grads, kinds)
    core = lax.axis_index("c").astype(jnp.int32).reshape(1)
    parts = [_pair_sum(gr, ot, k, core, f"pair_sum_{n}") for gr, ot, k, n in zip(grads, others, kinds, names)]
    landed = _exchange_chips(parts, [[0], [1], [2], [3], [4, 5], [6, 7]])

    out = {}
    for name, buf in zip(("ab_w_in", "ab_w_out", "sb_w_in", "sb_w_out", "ffn_w1", "ffn_w2"), landed):
        shape = W[name].shape
        flat = (shape[0] * shape[1], shape[2])
        res = _adamw(W[name].reshape(flat), buf.reshape((4,) + flat), M[name].reshape(flat), V[name].reshape(flat),
                     f"adamw_{name}")
        out[name] = [r.reshape(shape) for r in res]

    small_g = [small["pre_mix"], small["post_mix"], small["pre_ffn"], small["post_ffn"], small["ln_g"],
               small["ln_b"], small["w_s"][None], small["b_s"][None]]
    g_small = _all_reduce_small(_pack(small_g))
    res = _adamw(_pack([W[n] for n in SMALL]), g_small[None], _pack([M[n] for n in SMALL]),
                 _pack([V[n] for n in SMALL]), "adamw_small")
    like = [W[n] for n in SMALL]
    for n, *vals in zip(SMALL, *[_unpack(r, like) for r in res]):
        out[n] = vals

    return (loss, dx[None], *[out[n][0] for n in ORDER], *[out[n][1] for n in ORDER],
            *[out[n][2] for n in ORDER], *[out[n][3] for n in ORDER])
```

```python
import functools
import math

import jax
import jax.numpy as jnp
from jax import lax
from jax.experimental import pallas as pl
from jax.experimental.pallas import tpu as pltpu

F32 = jnp.float32
BF16 = jnp.bfloat16

HEAD_DIM = 128
CHUNK = 128
ATT_BLOCK = 128
DILATED_PAIRS = ((128, 1), (512, 4), (2048, 16))
RMS_EPS = 1e-6
LN_EPS = 1e-5
ADAM_LR = 0.001
ADAM_B1 = 0.9
ADAM_B2 = 0.999
ADAM_EPS = 1e-08
ADAM_WD = 0.01
ADAM_STEP = 10
N_DEV = 8
MESH_AXES = ("x", "y", "c")
MASKED = -1e30

V7X_VMEM_BYTES = 64 * 1024 * 1024
VMEM_LIMIT = V7X_VMEM_BYTES - 8 * 1024 * 1024

NN = (((1,), (0,)), ((), ()))
NT = (((1,), (1,)), ((), ()))
TN = (((0,), (0,)), ((), ()))


def _params(*sem):
    return pltpu.CompilerParams(dimension_semantics=sem, vmem_limit_bytes=VMEM_LIMIT)


def _dot(a, b, dims=NN):
    return lax.dot_general(a, b, dims, preferred_element_type=F32)


def _tile(n, preferred):
    if n <= preferred:
        return n
    t = preferred - preferred % 128
    while n % t:
        t -= 128
    assert t > 0, (n, preferred)
    return t


def _matmul(a, b, *, mode, name, out_dtype=F32, tm=1024, tn=512, tk=2048, epi=None, extras=()):
    if mode == "nn":
        (M, K), N = a.shape, b.shape[1]
    elif mode == "nt":
        (M, K), N = a.shape, b.shape[0]
    else:
        (K, M), N = a.shape, b.shape[1]
    tm, tn, tk = _tile(M, tm), _tile(N, tn), _tile(K, tk)
    nk = K // tk
    if mode == "tn":
        a_spec = pl.BlockSpec((tk, tm), lambda i, j, k: (k, i))
    else:
        a_spec = pl.BlockSpec((tm, tk), lambda i, j, k: (i, k))
    if mode == "nt":
        b_spec = pl.BlockSpec((tn, tk), lambda i, j, k: (j, k))
    else:
        b_spec = pl.BlockSpec((tk, tn), lambda i, j, k: (k, j))
    o_spec = pl.BlockSpec((tm, tn), lambda i, j, k: (i, j))
    dims = {"nn": NN, "nt": NT, "tn": TN}[mode]
    n_extra = len(extras)

    def finish(acc, refs):
        j = pl.program_id(1)
        if epi is None:
            return acc
        return epi(acc, j, *[r[...] for r in refs])

    if nk == 1:
        def body(a_ref, b_ref, *rest):
            o_ref = rest[n_extra]
            acc = _dot(a_ref[...], b_ref[...], dims)
            o_ref[...] = finish(acc, rest[:n_extra]).astype(o_ref.dtype)
        scratch = []
    else:
        def body(a_ref, b_ref, *rest):
            o_ref, acc_ref = rest[n_extra], rest[n_extra + 1]
            k = pl.program_id(2)

            @pl.when(k == 0)
            def _():
                acc_ref[...] = jnp.zeros_like(acc_ref)

            acc_ref[...] += _dot(a_ref[...], b_ref[...], dims)

            @pl.when(k == nk - 1)
            def _():
                o_ref[...] = finish(acc_ref[...], rest[:n_extra]).astype(o_ref.dtype)
        scratch = [pltpu.VMEM((tm, tn), F32)]

    return pl.pallas_call(
        body,
        name=name,
        grid=(M // tm, N // tn, nk),
        in_specs=[a_spec, b_spec] + [o_spec] * n_extra,
        out_specs=o_spec,
        out_shape=jax.ShapeDtypeStruct((M, N), out_dtype),
        scratch_shapes=scratch,
        compiler_params=_params("parallel", "parallel", "arbitrary"),
    )(a, b, *extras)


ROWS = 256


def _rms(x):
    return lax.rsqrt(jnp.mean(x * x, axis=-1, keepdims=True) + RMS_EPS)


def _prenorm(x, g, name):
    T, D = x.shape

    def body(x_ref, g_ref, h_ref):
        xv = x_ref[...]
        h_ref[...] = (xv * _rms(xv) * g_ref[...]).astype(BF16)

    row = pl.BlockSpec((ROWS, D), lambda i: (i, 0))
    vec = pl.BlockSpec((1, D), lambda i: (0, 0))
    return pl.pallas_call(
        body, name=name, grid=(T // ROWS,), in_specs=[row, vec], out_specs=row,
        out_shape=jax.ShapeDtypeStruct((T, D), BF16), compiler_params=_params("parallel"),
    )(x, g)


def _postnorm_prenorm(x, y, g_post, g_pre, name):
    T, D = x.shape

    def body(x_ref, y_ref, gp_ref, gn_ref, xo_ref, h_ref):
        yv = y_ref[...]
        xn = x_ref[...] + yv * _rms(yv) * gp_ref[...]
        xo_ref[...] = xn
        h_ref[...] = (xn * _rms(xn) * gn_ref[...]).astype(BF16)

    row = pl.BlockSpec((ROWS, D), lambda i: (i, 0))
    vec = pl.BlockSpec((1, D), lambda i: (0, 0))
    return pl.pallas_call(
        body, name=name, grid=(T // ROWS,), in_specs=[row, row, vec, vec], out_specs=[row, row],
        out_shape=[jax.ShapeDtypeStruct((T, D), F32), jax.ShapeDtypeStruct((T, D), BF16)],
        compiler_params=_params("parallel"),
    )(x, y, g_post, g_pre)


def _postnorm_loss(x, y, g_post, target, name):
    T, D = x.shape

    def body(x_ref, y_ref, gp_ref, t_ref, loss_ref, dx_ref):
        @pl.when(pl.program_id(0) == 0)
        def _():
            loss_ref[...] = jnp.zeros_like(loss_ref)

        yv = y_ref[...]
        err = x_ref[...] + yv * _rms(yv) * gp_ref[...] - t_ref[...]
        dx_ref[...] = err * (1.0 / D)
        loss_ref[...] += 0.5 * jnp.sum(jnp.sum(err * err, axis=-1, keepdims=True) * (1.0 / D))

    row = pl.BlockSpec((ROWS, D), lambda i: (i, 0))
    vec = pl.BlockSpec((1, D), lambda i: (0, 0))
    acc = pl.BlockSpec((8, 128), lambda i: (0, 0))
    return pl.pallas_call(
        body, name=name, grid=(T // ROWS,), in_specs=[row, row, vec, row], out_specs=[acc, row],
        out_shape=[jax.ShapeDtypeStruct((8, 128), F32), jax.ShapeDtypeStruct((T, D), F32)],
        compiler_params=_params("arbitrary"),
    )(x, y, g_post, target)


def _postnorm_bwd(dx, y, g_post, name):
    T, D = dx.shape

    def body(dx_ref, y_ref, g_ref, dy_ref, dg_ref):
        @pl.when(pl.program_id(0) == 0)
        def _():
            dg_ref[...] = jnp.zeros_like(dg_ref)

        yv, dn = y_ref[...], dx_ref[...]
        r = _rms(yv)
        yh = yv * r
        gd = dn * g_ref[...]
        dy_ref[...] = (r * (gd - yh * jnp.mean(yh * gd, axis=-1, keepdims=True))).astype(BF16)
        dg_ref[...] += jnp.sum(dn * yh, axis=0, keepdims=True)

    row = pl.BlockSpec((ROWS, D), lambda i: (i, 0))
    vec = pl.BlockSpec((1, D), lambda i: (0, 0))
    return pl.pallas_call(
        body, name=name, grid=(T // ROWS,), in_specs=[row, row, vec], out_specs=[row, vec],
        out_shape=[jax.ShapeDtypeStruct((T, D), BF16), jax.ShapeDtypeStruct((1, D), F32)],
        compiler_params=_params("arbitrary"),
    )(dx, y, g_post)


def _prenorm_bwd(dx_out, dh, x, g_pre, name):
    T, D = x.shape

    def body(dxo_ref, dh_ref, x_ref, g_ref, dx_ref, dg_ref):
        @pl.when(pl.program_id(0) == 0)
        def _():
            dg_ref[...] = jnp.zeros_like(dg_ref)

        xv, dhv = x_ref[...], dh_ref[...]
        r = _rms(xv)
        xh = xv * r
        gd = dhv * g_ref[...]
        dx_ref[...] = dxo_ref[...] + r * (gd - xh * jnp.mean(xh * gd, axis=-1, keepdims=True))
        dg_ref[...] += jnp.sum(dhv * xh, axis=0, keepdims=True)

    row = pl.BlockSpec((ROWS, D), lambda i: (i, 0))
    vec = pl.BlockSpec((1, D), lambda i: (0, 0))
    return pl.pallas_call(
        body, name=name, grid=(T // ROWS,), in_specs=[row, row, row, vec], out_specs=[row, vec],
        out_shape=[jax.ShapeDtypeStruct((T, D), F32), jax.ShapeDtypeStruct((1, D), F32)],
        compiler_params=_params("arbitrary"),
    )(dx_out, dh, x, g_pre)


_INV_SQRT2 = 1.0 / math.sqrt(2.0)
_INV_SQRT2PI = 1.0 / math.sqrt(2.0 * math.pi)


def _gelu(x):
    return 0.5 * x * (1.0 + lax.erf(x * _INV_SQRT2))


def _gelu_grad(x):
    return 0.5 * (1.0 + lax.erf(x * _INV_SQRT2)) + x * jnp.exp(-0.5 * x * x) * _INV_SQRT2PI


def _layernorm_stats(x):
    mu = jnp.mean(x, axis=-1, keepdims=True)
    xc = x - mu
    rstd = lax.rsqrt(jnp.mean(xc * xc, axis=-1, keepdims=True) + LN_EPS)
    return xc * rstd, rstd


def _tril_mask():
    i = lax.broadcasted_iota(jnp.int32, (CHUNK, CHUNK), 0)
    j = lax.broadcasted_iota(jnp.int32, (CHUNK, CHUNK), 1)
    return j <= i


SGU_ROWS = 512


def _sgu_fwd(z, ln_g, ln_b, w_s, b_t, name):
    T = z.shape[0]
    A = ln_g.shape[1]
    G = A // 128
    rows = min(SGU_ROWS, T)

    def body(u_ref, v_ref, g_ref, b_ref, w_ref, bt_ref, o_ref):
        mask = _tril_mask()
        for c in range(rows // CHUNK):
            rs = pl.ds(c * CHUNK, CHUNK)
            xh, _ = _layernorm_stats(_gelu(v_ref[rs, :]))
            vn = (xh * g_ref[...] + b_ref[...]).astype(BF16)
            for g in range(G):
                cs = pl.ds(g * 128, 128)
                w = jnp.where(mask, w_ref[g], 0.0).astype(BF16)
                mixed = _dot(w, vn[:, g * 128:(g + 1) * 128]) + bt_ref[:, g:g + 1]
                o_ref[rs, cs] = (_gelu(u_ref[rs, cs]) * mixed).astype(BF16)

    return pl.pallas_call(
        body, name=name, grid=(T // rows,),
        in_specs=[
            pl.BlockSpec((rows, A), lambda i: (i, 0)),
            pl.BlockSpec((rows, A), lambda i: (i, 1)),
            pl.BlockSpec((1, A), lambda i: (0, 0)),
            pl.BlockSpec((1, A), lambda i: (0, 0)),
            pl.BlockSpec((G, CHUNK, CHUNK), lambda i: (0, 0, 0)),
            pl.BlockSpec((CHUNK, G), lambda i: (0, 0)),
        ],
        out_specs=pl.BlockSpec((rows, A), lambda i: (i, 0)),
        out_shape=jax.ShapeDtypeStruct((T, A), BF16),
        compiler_params=_params("parallel"),
    )(z, z, ln_g, ln_b, w_s, b_t)


def _sgu_bwd(z, dcat, ln_g, ln_b, w_s, b_t, name):
    T = z.shape[0]
    A = ln_g.shape[1]
    G = A // 128
    rows = min(SGU_ROWS, T)

    def body(u_ref, v_ref, da_ref, g_ref, b_ref, w_ref, bt_ref, dz_ref, dg_ref, db_ref, dw_ref, dbt_ref, dvn_ref):
        @pl.when(pl.program_id(0) == 0)
        def _():
            dg_ref[...] = jnp.zeros_like(dg_ref)
            db_ref[...] = jnp.zeros_like(db_ref)
            dw_ref[...] = jnp.zeros_like(dw_ref)
            dbt_ref[...] = jnp.zeros_like(dbt_ref)

        mask = _tril_mask()
        for c in range(rows // CHUNK):
            rs = pl.ds(c * CHUNK, CHUNK)
            vv = v_ref[rs, :]
            gv = _gelu(vv)
            xh, rstd = _layernorm_stats(gv)
            vn = (xh * g_ref[...] + b_ref[...]).astype(BF16)
            for g in range(G):
                cs = pl.ds(g * 128, 128)
                w = jnp.where(mask, w_ref[g], 0.0).astype(BF16)
                vg = vn[:, g * 128:(g + 1) * 128]
                mixed = _dot(w, vg) + bt_ref[:, g:g + 1]
                uu = u_ref[rs, cs]
                da = da_ref[rs, cs]
                dz_ref[rs, cs] = (da * mixed * _gelu_grad(uu)).astype(BF16)
                dm = da * _gelu(uu)
                dmb = dm.astype(BF16)
                dbt_ref[:, g:g + 1] += jnp.sum(dm, axis=1, keepdims=True)
                dw_ref[g] += jnp.where(mask, _dot(dmb, vg, NT), 0.0)
                dvn_ref[:, cs] = _dot(w, dmb, TN)
            dvn = dvn_ref[...]
            dg_ref[...] += jnp.sum(dvn * xh, axis=0, keepdims=True)
            db_ref[...] += jnp.sum(dvn, axis=0, keepdims=True)
            dxh = dvn * g_ref[...]
            dgv = rstd * (dxh - jnp.mean(dxh, axis=-1, keepdims=True)
                          - xh * jnp.mean(dxh * xh, axis=-1, keepdims=True))
            dz_ref[rs, pl.ds(A, A)] = (dgv * _gelu_grad(vv)).astype(BF16)

    vec = pl.BlockSpec((1, A), lambda i: (0, 0))
    wsp = pl.BlockSpec((G, CHUNK, CHUNK), lambda i: (0, 0, 0))
    bsp = pl.BlockSpec((CHUNK, G), lambda i: (0, 0))
    return pl.pallas_call(
        body, name=name, grid=(T // rows,),
        in_specs=[
            pl.BlockSpec((rows, A), lambda i: (i, 0)),
            pl.BlockSpec((rows, A), lambda i: (i, 1)),
            pl.BlockSpec((rows, A), lambda i: (i, 0)),
            vec, vec, wsp, bsp,
        ],
        out_specs=[pl.BlockSpec((rows, 2 * A), lambda i: (i, 0)), vec, vec, wsp, bsp],
        out_shape=[
            jax.ShapeDtypeStruct((T, 2 * A), BF16),
            jax.ShapeDtypeStruct((1, A), F32),
            jax.ShapeDtypeStruct((1, A), F32),
            jax.ShapeDtypeStruct((G, CHUNK, CHUNK), F32),
            jax.ShapeDtypeStruct((CHUNK, G), F32),
        ],
        scratch_shapes=[pltpu.VMEM((CHUNK, A), F32)],
        compiler_params=_params("arbitrary"),
    )(z, z, dcat, ln_g, ln_b, w_s, b_t)


def _alibi_slope(h, n_heads):
    return 2.0 ** (-8.0 * (h + 1.0) / n_heads)


def _dil_scores(q, k, slope_d, valid, dist):
    s = _dot(q, k, NT) - slope_d * dist
    return jnp.where(valid, s, MASKED)


def _dilated_fwd(z, d, B, name):
    T = z.shape[0]
    H = B // HEAD_DIM
    nb = T // (d * ATT_BLOCK)
    zv = z.reshape(T // d, d * 5 * B)
    scale = HEAD_DIM ** -0.5
    blk = ATT_BLOCK

    def body(q_ref, kp_ref, kc_ref, vp_ref, vc_ref, o_ref, l_ref):
        n = pl.program_id(1)
        qi = lax.broadcasted_iota(jnp.int32, (blk, 2 * blk), 0)
        kj = lax.broadcasted_iota(jnp.int32, (blk, 2 * blk), 1)
        dist = qi + blk - kj
        valid = (dist >= 0) & (dist <= blk) & ((kj >= blk) | (n > 0))
        distf = dist.astype(F32)
        for h in range(H):
            cs = pl.ds(h * HEAD_DIM, HEAD_DIM)
            q = (q_ref[:, cs] * scale).astype(BF16)
            k = jnp.concatenate([kp_ref[:, cs], kc_ref[:, cs]], axis=0).astype(BF16)
            v = jnp.concatenate([vp_ref[:, cs], vc_ref[:, cs]], axis=0).astype(BF16)
            s = _dil_scores(q, k, _alibi_slope(h, H) * d, valid, distf)
            m = jnp.max(s, axis=-1, keepdims=True)
            p = jnp.exp(s - m)
            den = jnp.sum(p, axis=-1, keepdims=True)
            o_ref[:, cs] = _dot(p.astype(BF16), v) / den
            l_ref[:, cs] = jnp.broadcast_to(m + jnp.log(den), (blk, HEAD_DIM))

    def col(unit):
        return lambda r, n: (n, r * 5 + unit)

    def col_prev(unit):
        return lambda r, n: (jnp.maximum(n - 1, 0), r * 5 + unit)

    bs = (blk, B)
    out = pl.BlockSpec(bs, lambda r, n: (n, r))
    o, lse = pl.pallas_call(
        body, name=name, grid=(d, nb),
        in_specs=[pl.BlockSpec(bs, col(2)), pl.BlockSpec(bs, col_prev(3)), pl.BlockSpec(bs, col(3)),
                  pl.BlockSpec(bs, col_prev(4)), pl.BlockSpec(bs, col(4))],
        out_specs=[out, out],
        out_shape=[jax.ShapeDtypeStruct((T // d, d * B), F32)] * 2,
        compiler_params=_params("parallel", "parallel"),
    )(zv, zv, zv, zv, zv)
    return o.reshape(T, B), lse.reshape(T, B)


def _dilated_merge(os_, ls_, name):
    T, B = os_[0].shape

    def body(o1, o2, o3, l1, l2, l3, ob_ref, of_ref, lt_ref):
        a, b, c = l1[...], l2[...], l3[...]
        m = jnp.maximum(jnp.maximum(a, b), c)
        ea, eb, ec = jnp.exp(a - m), jnp.exp(b - m), jnp.exp(c - m)
        tot = ea + eb + ec
        o = (ea * o1[...] + eb * o2[...] + ec * o3[...]) / tot
        of_ref[...] = o
        ob_ref[...] = o.astype(BF16)
        lt_ref[...] = m + jnp.log(tot)

    row = pl.BlockSpec((ROWS, B), lambda i: (i, 0))
    return pl.pallas_call(
        body, name=name, grid=(T // ROWS,), in_specs=[row] * 6, out_specs=[row] * 3,
        out_shape=[jax.ShapeDtypeStruct((T, B), BF16), jax.ShapeDtypeStruct((T, B), F32),
                   jax.ShapeDtypeStruct((T, B), F32)],
        compiler_params=_params("parallel"),
    )(*os_, *ls_)


def _dilated_delta(dcat, o, name):
    T, B = o.shape
    H = B // HEAD_DIM

    def body(do_ref, o_ref, d_ref):
        for h in range(H):
            cs = pl.ds(h * HEAD_DIM, HEAD_DIM)
            dsum = jnp.sum(do_ref[:, cs] * o_ref[:, cs], axis=-1, keepdims=True)
            d_ref[:, cs] = jnp.broadcast_to(dsum, (ROWS, HEAD_DIM))

    row = pl.BlockSpec((ROWS, B), lambda i: (i, 0))
    return pl.pallas_call(
        body, name=name, grid=(T // ROWS,),
        in_specs=[pl.BlockSpec((ROWS, B), lambda i: (i, 1)), row], out_specs=row,
        out_shape=jax.ShapeDtypeStruct((T, B), F32), compiler_params=_params("parallel"),
    )(dcat, o)


def _dilated_dq(z, dcat, lse, delta, d, B, name):
    T = z.shape[0]
    H = B // HEAD_DIM
    nb = T // (d * ATT_BLOCK)
    zv = z.reshape(T // d, d * 5 * B)
    dov = dcat.reshape(T // d, d * 2 * B)
    lv = lse.reshape(T // d, d * B)
    dv_ = delta.reshape(T // d, d * B)
    scale = HEAD_DIM ** -0.5
    blk = ATT_BLOCK

    def body(q_ref, kp_ref, kc_ref, vp_ref, vc_ref, do_ref, l_ref, dl_ref, dq_ref):
        n = pl.program_id(1)
        qi = lax.broadcasted_iota(jnp.int32, (blk, 2 * blk), 0)
        kj = lax.broadcasted_iota(jnp.int32, (blk, 2 * blk), 1)
        dist = qi + blk - kj
        valid = (dist >= 0) & (dist <= blk) & ((kj >= blk) | (n > 0))
        distf = dist.astype(F32)
        for h in range(H):
            cs = pl.ds(h * HEAD_DIM, HEAD_DIM)
            q = (q_ref[:, cs] * scale).astype(BF16)
            k = jnp.concatenate([kp_ref[:, cs], kc_ref[:, cs]], axis=0).astype(BF16)
            v = jnp.concatenate([vp_ref[:, cs], vc_ref[:, cs]], axis=0).astype(BF16)
            s = _dil_scores(q, k, _alibi_slope(h, H) * d, valid, distf)
            p = jnp.exp(s - l_ref[:, h * HEAD_DIM:h * HEAD_DIM + 1])
            dp = _dot(do_ref[:, cs].astype(BF16), v, NT)
            ds = p * (dp - dl_ref[:, h * HEAD_DIM:h * HEAD_DIM + 1])
            dq_ref[:, cs] = _dot(ds.astype(BF16), k)

    def col(unit):
        return lambda r, n: (n, r * 5 + unit)

    def col_prev(unit):
        return lambda r, n: (jnp.maximum(n - 1, 0), r * 5 + unit)

    bs = (blk, B)
    out = pl.BlockSpec(bs, lambda r, n: (n, r))
    dq = pl.pallas_call(
        body, name=name, grid=(d, nb),
        in_specs=[pl.BlockSpec(bs, col(2)), pl.BlockSpec(bs, col_prev(3)), pl.BlockSpec(bs, col(3)),
                  pl.BlockSpec(bs, col_prev(4)), pl.BlockSpec(bs, col(4)),
                  pl.BlockSpec(bs, lambda r, n: (n, r * 2 + 1)), out, out],
        out_specs=out,
        out_shape=jax.ShapeDtypeStruct((T // d, d * B), F32),
        compiler_params=_params("parallel", "parallel"),
    )(zv, zv, zv, zv, zv, dov, lv, dv_)
    return dq.reshape(T, B)


def _dilated_dkv(z, dcat, lse, delta, d, B, name):
    T = z.shape[0]
    H = B // HEAD_DIM
    nb = T // (d * ATT_BLOCK)
    zv = z.reshape(T // d, d * 5 * B)
    dov = dcat.reshape(T // d, d * 2 * B)
    lv = lse.reshape(T // d, d * B)
    dv_ = delta.reshape(T // d, d * B)
    scale = HEAD_DIM ** -0.5
    blk = ATT_BLOCK

    def body(k_ref, v_ref, qa_ref, qb_ref, doa_ref, dob_ref, la_ref, lb_ref, da_ref, db_ref, dk_ref, dv_ref):
        m = pl.program_id(1)
        qi = lax.broadcasted_iota(jnp.int32, (2 * blk, blk), 0)
        kj = lax.broadcasted_iota(jnp.int32, (2 * blk, blk), 1)
        dist = qi - kj
        valid = (dist >= 0) & (dist <= blk) & ((qi < blk) | (m + 1 < nb))
        distf = dist.astype(F32)
        for h in range(H):
            cs = pl.ds(h * HEAD_DIM, HEAD_DIM)
            c1 = slice(h * HEAD_DIM, h * HEAD_DIM + 1)
            q = (jnp.concatenate([qa_ref[:, cs], qb_ref[:, cs]], axis=0) * scale).astype(BF16)
            do = jnp.concatenate([doa_ref[:, cs], dob_ref[:, cs]], axis=0).astype(BF16)
            lse2 = jnp.concatenate([la_ref[:, c1], lb_ref[:, c1]], axis=0)
            dl2 = jnp.concatenate([da_ref[:, c1], db_ref[:, c1]], axis=0)
            k = k_ref[:, cs].astype(BF16)
            v = v_ref[:, cs].astype(BF16)
            s = _dil_scores(q, k, _alibi_slope(h, H) * d, valid, distf)
            p = jnp.exp(jnp.where(valid, s - lse2, MASKED))
            dv_ref[:, cs] = _dot(p.astype(BF16), do, TN)
            dp = _dot(do, v, NT)
            ds = p * (dp - dl2)
            dk_ref[:, cs] = _dot(ds.astype(BF16), q, TN)

    def col(unit):
        return lambda r, m: (m, r * 5 + unit)

    def nxt(width, unit):
        return lambda r, m: (jnp.minimum(m + 1, nb - 1), r * width + unit)

    bs = (blk, B)
    out = pl.BlockSpec(bs, lambda r, m: (m, r))
    dk, dv = pl.pallas_call(
        body, name=name, grid=(d, nb),
        in_specs=[pl.BlockSpec(bs, col(3)), pl.BlockSpec(bs, col(4)),
                  pl.BlockSpec(bs, col(2)), pl.BlockSpec(bs, nxt(5, 2)),
                  pl.BlockSpec(bs, lambda r, m: (m, r * 2 + 1)), pl.BlockSpec(bs, nxt(2, 1)),
                  out, pl.BlockSpec(bs, nxt(1, 0)), out, pl.BlockSpec(bs, nxt(1, 0))],
        out_specs=[out, out],
        out_shape=[jax.ShapeDtypeStruct((T // d, d * B), F32)] * 2,
        compiler_params=_params("parallel", "parallel"),
    )(zv, zv, zv, zv, dov, dov, lv, lv, dv_, dv_)
    return dk.reshape(T, B), dv.reshape(T, B)


def _dilated_combine(dqs, dks, dvs, name):
    T, B = dqs[0].shape
    scale = HEAD_DIM ** -0.5

    def body(q1, q2, q3, k1, k2, k3, v1, v2, v3, o_ref):
        o_ref[:, pl.ds(0, B)] = ((q1[...] + q2[...] + q3[...]) * scale).astype(BF16)
        o_ref[:, pl.ds(B, B)] = (k1[...] + k2[...] + k3[...]).astype(BF16)
        o_ref[:, pl.ds(2 * B, B)] = (v1[...] + v2[...] + v3[...]).astype(BF16)

    row = pl.BlockSpec((ROWS, B), lambda i: (i, 0))
    return pl.pallas_call(
        body, name=name, grid=(T // ROWS,), in_specs=[row] * 9,
        out_specs=pl.BlockSpec((ROWS, 3 * B), lambda i: (i, 0)),
        out_shape=jax.ShapeDtypeStruct((T, 3 * B), BF16), compiler_params=_params("parallel"),
    )(*dqs, *dks, *dvs)


def _split_dot(x, tri):
    hi = x.astype(BF16)
    lo = (x - hi.astype(F32)).astype(BF16)
    return _dot(hi, tri) + _dot(lo, tri)


SB_QUERY_ROWS = 512


def _tri_and_ones(pred):
    rows = lax.broadcasted_iota(jnp.int32, (ATT_BLOCK, 2 * ATT_BLOCK), 0)
    cols = lax.broadcasted_iota(jnp.int32, (ATT_BLOCK, 2 * ATT_BLOCK), 1)
    return ((cols >= ATT_BLOCK) | pred(rows, cols)).astype(BF16)


def _sb_mask(query_rows, s):
    rows = lax.broadcasted_iota(jnp.int32, (query_rows, ATT_BLOCK), 0)
    cols = lax.broadcasted_iota(jnp.int32, (query_rows, ATT_BLOCK), 1)
    return cols + s * ATT_BLOCK < rows


def _log_sigmoids(z):
    e = jnp.exp(-jnp.abs(z))
    sp = jnp.log(1.0 + e)
    return jnp.minimum(z, 0.0) - sp, jnp.minimum(-z, 0.0) - sp, e


def _sb_fwd(qkv, W, name):
    T = qkv.shape[0]
    H = W // HEAD_DIM
    blk = ATT_BLOCK
    qb = min(SB_QUERY_ROWS, T)
    per = qb // blk

    def body(q_ref, k_ref, v_ref, o_ref, lt_ref, acc_ref):
        i = pl.program_id(1)
        q = q_ref[...]
        tri = _tri_and_ones(lambda r, c: r > c)
        lt_ref[...] = jnp.zeros_like(lt_ref)
        acc_ref[...] = jnp.zeros_like(acc_ref)

        def tile(j, mask):
            ks = pl.ds(pl.multiple_of(j * blk, blk), blk)
            z = _dot(q, k_ref[ks, :], NT)
            ls, lm, _ = _log_sigmoids(z)
            if mask is not None:
                lm = jnp.where(mask, lm, 0.0)
            both = _split_dot(lm, tri)
            a = jnp.exp(ls + both[:, :blk] + lt_ref[...])
            if mask is not None:
                a = jnp.where(mask, a, 0.0)
            acc_ref[...] += _dot(a.astype(BF16), v_ref[ks, :])
            lt_ref[...] += both[:, blk:]

        for s in reversed(range(per)):
            tile(i * per + s, _sb_mask(qb, s))

        def step(jj, _):
            tile(i * per - 1 - jj, None)
            return 0

        lax.fori_loop(0, i * per, step, 0)
        o_ref[...] = acc_ref[...].astype(BF16)

    qs = pl.BlockSpec((qb, HEAD_DIM), lambda h, i: (i, h))
    return pl.pallas_call(
        body, name=name, grid=(H, T // qb),
        in_specs=[qs, pl.BlockSpec((T, HEAD_DIM), lambda h, i: (0, H + h)),
                  pl.BlockSpec((T, HEAD_DIM), lambda h, i: (0, 2 * H + h))],
        out_specs=[qs, qs],
        out_shape=[jax.ShapeDtypeStruct((T, W), BF16), jax.ShapeDtypeStruct((T, W), F32)],
        scratch_shapes=[pltpu.VMEM((qb, HEAD_DIM), F32)],
        compiler_params=_params("parallel", "arbitrary"),
    )(qkv, qkv, qkv)


def _sb_bwd(qkv, do, ltot, W, name):
    T = qkv.shape[0]
    H = W // HEAD_DIM
    blk = ATT_BLOCK
    nkb = T // blk
    qb = min(SB_QUERY_ROWS, T)
    per = qb // blk

    def body(q_ref, k_ref, v_ref, do_ref, lt_ref, dq_ref, dkt_ref, dvt_ref, qt_ref, dot_ref, plm_ref, pg_ref):
        i = pl.program_id(1)

        @pl.when(i == 0)
        def _():
            dkt_ref[...] = jnp.zeros_like(dkt_ref)
            dvt_ref[...] = jnp.zeros_like(dvt_ref)

        q = q_ref[...]
        do = do_ref[...]
        qt_ref[...] = q.astype(F32).T.astype(BF16)
        dot_ref[...] = do.astype(F32).T.astype(BF16)
        upto = _tri_and_ones(lambda r, c: r <= c)
        before = _tri_and_ones(lambda r, c: r < c)
        plm_ref[...] = jnp.zeros_like(plm_ref)
        pg_ref[...] = jnp.zeros_like(pg_ref)
        dq_ref[...] = jnp.zeros_like(dq_ref)

        def tile(j, mask):
            ks = pl.ds(pl.multiple_of(j * blk, blk), blk)
            k = k_ref[ks, :]
            v = v_ref[ks, :]
            z = _dot(q, k, NT)
            ls, lm, e = _log_sigmoids(z)
            if mask is not None:
                lm = jnp.where(mask, lm, 0.0)
            both = _split_dot(lm, upto)
            a = jnp.exp(ls + lt_ref[...] - (plm_ref[...] + both[:, :blk]))
            if mask is not None:
                a = jnp.where(mask, a, 0.0)
            g = a * _dot(do, v, NT)
            gboth = _split_dot(g, before)
            gsum = pg_ref[...] + gboth[:, :blk]
            r = 1.0 / (1.0 + e)
            pos = z >= 0.0
            sig = jnp.where(pos, r, e * r)
            nsig = jnp.where(pos, e * r, r)
            dz = g * nsig - gsum * sig
            if mask is not None:
                dz = jnp.where(mask, dz, 0.0)
            dzb = dz.astype(BF16)
            dkt_ref[j] += _dot(qt_ref[...], dzb)
            dvt_ref[j] += _dot(dot_ref[...], a.astype(BF16))
            dq_ref[...] += _dot(dzb, k)
            plm_ref[...] += both[:, blk:]
            pg_ref[...] += gboth[:, blk:]

        def step(j, _):
            tile(j, None)
            return 0

        lax.fori_loop(0, i * per, step, 0)
        for s in range(per):
            tile(i * per + s, _sb_mask(qb, s))

    qs = pl.BlockSpec((qb, HEAD_DIM), lambda h, i: (i, h))
    res = pl.BlockSpec((None, nkb, HEAD_DIM, blk), lambda h, i: (h, 0, 0, 0))
    return pl.pallas_call(
        body, name=name, grid=(H, T // qb),
        in_specs=[qs, pl.BlockSpec((T, HEAD_DIM), lambda h, i: (0, H + h)),
                  pl.BlockSpec((T, HEAD_DIM), lambda h, i: (0, 2 * H + h)), qs, qs],
        out_specs=[qs, res, res],
        out_shape=[jax.ShapeDtypeStruct((T, W), F32)] + [jax.ShapeDtypeStruct((H, nkb, HEAD_DIM, blk), F32)] * 2,
        scratch_shapes=[pltpu.VMEM((HEAD_DIM, qb), BF16), pltpu.VMEM((HEAD_DIM, qb), BF16),
                        pltpu.VMEM((qb, HEAD_DIM), F32), pltpu.VMEM((qb, HEAD_DIM), F32)],
        compiler_params=_params("parallel", "arbitrary"),
    )(qkv, qkv, qkv, do, ltot)


def _sb_pack(dq, dkt, dvt, name):
    T, W = dq.shape
    H = W // HEAD_DIM
    blk = ATT_BLOCK
    scale = HEAD_DIM ** -0.5

    def body(q_ref, kt_ref, vt_ref, o_ref):
        o_ref[:, pl.ds(0, W)] = (q_ref[...] * scale).astype(BF16)
        for h in range(H):
            o_ref[:, pl.ds(W + h * HEAD_DIM, HEAD_DIM)] = kt_ref[h].T.astype(BF16)
            o_ref[:, pl.ds(2 * W + h * HEAD_DIM, HEAD_DIM)] = vt_ref[h].T.astype(BF16)

    tr = pl.BlockSpec((H, None, HEAD_DIM, blk), lambda i: (0, i, 0, 0))
    return pl.pallas_call(
        body, name=name, grid=(T // blk,), in_specs=[pl.BlockSpec((blk, W), lambda i: (i, 0)), tr, tr],
        out_specs=pl.BlockSpec((blk, 3 * W), lambda i: (i, 0)),
        out_shape=jax.ShapeDtypeStruct((T, 3 * W), BF16), compiler_params=_params("parallel"),
    )(dq, dkt, dvt)


def _local_step(x, target, norms, sgu, w):
    T, D = x.shape
    A = D // 2
    pre_mix, post_mix, pre_ffn, post_ffn = norms
    ln_g, ln_b, w_s, b_s = sgu
    b_t = b_s.T
    scale = HEAD_DIM ** -0.5

    def vec(p, layer):
        return p[layer:layer + 1]

    h0 = _prenorm(x, vec(pre_mix, 0), "prenorm0")
    z = _matmul(h0, w["ab_in"], mode="nn", name="ab_in_fwd")
    a_out = _sgu_fwd(z, ln_g, ln_b, w_s, b_t, "sgu_fwd")
    branch = [_dilated_fwd(z, d, A, f"dilated_fwd_{d}") for _, d in DILATED_PAIRS]
    b_out, o_dil, lse_dil = _dilated_merge([b[0] for b in branch], [b[1] for b in branch], "dilated_merge")
    cat = jnp.concatenate([a_out, b_out], axis=-1)
    y0 = _matmul(cat, w["ab_out"], mode="nn", name="ab_out_fwd")
    x1, h1 = _postnorm_prenorm(x, y0, vec(post_mix, 0), vec(pre_ffn, 0), "norm_mix0")

    def relu2(acc, j):
        r = jnp.maximum(acc, 0.0)
        return r * r

    f0 = _matmul(h1, w["w1"][0], mode="nn", name="ffn0_w1_fwd", out_dtype=BF16, epi=relu2)
    y1 = _matmul(f0, w["w2"][0], mode="nn", name="ffn0_w2_fwd")
    x2, h2 = _postnorm_prenorm(x1, y1, vec(post_ffn, 0), vec(pre_mix, 1), "norm_ffn0")

    tn_qkv = _tile(D, 512)
    nq = D // tn_qkv

    def scale_q(acc, j):
        return jnp.where(j < nq, acc * scale, acc)

    qkv = _matmul(h2, w["sb_in"], mode="nn", name="sb_in_fwd", out_dtype=BF16, tn=tn_qkv, epi=scale_q)
    o_sb, ltot = _sb_fwd(qkv, D, "sb_fwd")
    y2 = _matmul(o_sb, w["sb_out"], mode="nn", name="sb_out_fwd")
    x3, h3 = _postnorm_prenorm(x2, y2, vec(post_mix, 1), vec(pre_ffn, 1), "norm_mix1")
    f1 = _matmul(h3, w["w1"][1], mode="nn", name="ffn1_w1_fwd", out_dtype=BF16, epi=relu2)
    y3 = _matmul(f1, w["w2"][1], mode="nn", name="ffn1_w2_fwd")
    loss_tile, dx4 = _postnorm_loss(x3, y3, vec(post_ffn, 1), target, "norm_loss")
    loss = loss_tile[0, 0]

    def relu2_bwd(acc, j, f):
        return acc * (2.0 * jnp.sqrt(f.astype(F32)))

    def ffn_bwd(dx_out, x_in, h, f, y, layer, tag):
        dy, dg_post = _postnorm_bwd(dx_out, y, vec(post_ffn, layer), f"ffn{tag}_postnorm_bwd")
        g_w2 = _matmul(f, dy, mode="tn", name=f"ffn{tag}_w2_wgrad", tm=1024, tn=1024, tk=1024)
        da = _matmul(dy, w["w2"][layer], mode="nt", name=f"ffn{tag}_w2_dgrad", out_dtype=BF16,
                     epi=relu2_bwd, extras=(f,))
        g_w1 = _matmul(h, da, mode="tn", name=f"ffn{tag}_w1_wgrad", tm=1024, tn=1024, tk=1024)
        dh = _matmul(da, w["w1"][layer], mode="nt", name=f"ffn{tag}_w1_dgrad")
        dx_in, dg_pre = _prenorm_bwd(dx_out, dh, x_in, vec(pre_ffn, layer), f"ffn{tag}_prenorm_bwd")
        return dx_in, g_w1, g_w2, dg_pre, dg_post

    dx3, g_w1_1, g_w2_1, dg_pre_ffn1, dg_post_ffn1 = ffn_bwd(dx4, x3, h3, f1, y3, 1, "1")

    dy2, dg_post_mix1 = _postnorm_bwd(dx3, y2, vec(post_mix, 1), "sb_postnorm_bwd")
    g_sb_out = _matmul(o_sb, dy2, mode="tn", name="sb_out_wgrad", tm=1024, tn=1024, tk=1024)
    do_sb = _matmul(dy2, w["sb_out"], mode="nt", name="sb_out_dgrad", out_dtype=BF16)
    dq, dk, dv = _sb_bwd(qkv, do_sb, ltot, D, "sb_bwd")
    dqkv = _sb_pack(dq, dk, dv, "sb_pack")
    g_sb_in = _matmul(h2, dqkv, mode="tn", name="sb_in_wgrad", tm=1024, tn=1024, tk=1024)
    dh2 = _matmul(dqkv, w["sb_in"], mode="nt", name="sb_in_dgrad")
    dx2, dg_pre_mix1 = _prenorm_bwd(dx3, dh2, x2, vec(pre_mix, 1), "sb_prenorm_bwd")

    dx1, g_w1_0, g_w2_0, dg_pre_ffn0, dg_post_ffn0 = ffn_bwd(dx2, x1, h1, f0, y1, 0, "0")

    dy0, dg_post_mix0 = _postnorm_bwd(dx1, y0, vec(post_mix, 0), "ab_postnorm_bwd")
    g_ab_out = _matmul(cat, dy0, mode="tn", name="ab_out_wgrad", tm=1024, tn=1024, tk=1024)
    dcat = _matmul(dy0, w["ab_out"], mode="nt", name="ab_out_dgrad")
    duv, d_ln_g, d_ln_b, d_w_s, d_b_t = _sgu_bwd(z, dcat, ln_g, ln_b, w_s, b_t, "sgu_bwd")
    delta = _dilated_delta(dcat, o_dil, "dilated_delta")
    dqs, dks, dvs = [], [], []
    for _, d in DILATED_PAIRS:
        dqs.append(_dilated_dq(z, dcat, lse_dil, delta, d, A, f"dilated_dq_{d}"))
        dk_b, dv_b = _dilated_dkv(z, dcat, lse_dil, delta, d, A, f"dilated_dkv_{d}")
        dks.append(dk_b)
        dvs.append(dv_b)
    dqkv0 = _dilated_combine(dqs, dks, dvs, "dilated_combine")
    dz = jnp.concatenate([duv, dqkv0], axis=-1)
    g_ab_in = _matmul(h0, dz, mode="tn", name="ab_in_wgrad", tm=1024, tn=1024, tk=1024)
    dh0 = _matmul(dz, w["ab_in"], mode="nt", name="ab_in_dgrad")
    dx0, dg_pre_mix0 = _prenorm_bwd(dx1, dh0, x, vec(pre_mix, 0), "ab_prenorm_bwd")

    grads = {"ab_in": g_ab_in, "ab_out": g_ab_out, "sb_in": g_sb_in, "sb_out": g_sb_out,
             "w1": (g_w1_0, g_w1_1), "w2": (g_w2_0, g_w2_1)}
    small = {
        "pre_mix": jnp.concatenate([dg_pre_mix0, dg_pre_mix1], axis=0),
        "post_mix": jnp.concatenate([dg_post_mix0, dg_post_mix1], axis=0),
        "pre_ffn": jnp.concatenate([dg_pre_ffn0, dg_pre_ffn1], axis=0),
        "post_ffn": jnp.concatenate([dg_post_ffn0, dg_post_ffn1], axis=0),
        "ln_g": d_ln_g, "ln_b": d_ln_b, "w_s": d_w_s, "b_s": d_b_t.T,
    }
    return loss, dx0, grads, small


MESH_ID = pl.DeviceIdType.MESH
ANY = pl.BlockSpec(memory_space=pl.ANY)


def _coords():
    return lax.axis_index("x"), lax.axis_index("y"), lax.axis_index("c")


def _shard_of(ref, kind, p):
    if kind == "col":
        n = ref.shape[1] // N_DEV
        return ref.at[:, pl.ds(pl.multiple_of(p * n, 128), n)]
    r = ref.shape[0] // N_DEV
    return ref.at[pl.ds(pl.multiple_of(p * r, 16), r), :]


def _full_shape(shard, kind):
    if kind == "col":
        return (shard.shape[0], shard.shape[1] * N_DEV)
    return (shard.shape[0] * N_DEV, shard.shape[1])


def _all_gather(shards, kinds):
    nt = len(shards)

    def body(*refs):
        ins, outs = refs[:nt], refs[nt:2 * nt]
        send_sems, recv_sems, local_sems = refs[2 * nt:]
        x, y, c = _coords()
        me, sibling = (x, y, c), (x, y, 1 - c)
        chips = [(1 - x, y), (x, 1 - y), (1 - x, 1 - y)]

        def slot(t, px, py, pc):
            return _shard_of(outs[t], kinds[t], 4 * px + 2 * py + pc)

        def copy(t, k, block, to, src=None):
            return pltpu.make_async_remote_copy(
                src_ref=slot(t, *block) if src is None else src, dst_ref=slot(t, *block),
                send_sem=send_sems.at[7 * t + k], recv_sem=recv_sems.at[7 * t + k],
                device_id=to, device_id_type=MESH_ID)

        local, sent = [], []
        for t in range(nt):
            local.append(pltpu.make_async_copy(ins[t], slot(t, *me), local_sems.at[t]))
            local[-1].start()
            first = [copy(t, 0, me, sibling, src=ins[t])]
            first += [copy(t, 1 + j, me, (*chip, c), src=ins[t]) for j, chip in enumerate(chips)]
            for cp in first:
                cp.start()
            sent += first
        for t in range(nt):
            for j, chip in enumerate(chips):
                copy(t, 1 + j, (*chip, c), me).wait_recv()
                sent.append(copy(t, 4 + j, (*chip, c), sibling))
                sent[-1].start()
        for t in range(nt):
            copy(t, 0, sibling, me).wait_recv()
            for j, chip in enumerate(chips):
                copy(t, 4 + j, (*chip, 1 - c), me).wait_recv()
        for cp in sent:
            cp.wait_send()
        for cp in local:
            cp.wait()

    return pl.pallas_call(
        body, name="all_gather_weights",
        in_specs=[ANY] * nt, out_specs=[ANY] * nt,
        out_shape=[jax.ShapeDtypeStruct(_full_shape(s, k), s.dtype) for s, k in zip(shards, kinds)],
        scratch_shapes=[pltpu.SemaphoreType.DMA((7 * nt,)), pltpu.SemaphoreType.DMA((7 * nt,)),
                        pltpu.SemaphoreType.DMA((nt,))],
    )(*shards)


def _shard_shape(full, kind):
    if kind == "col":
        return (full.shape[0], full.shape[1] // N_DEV)
    return (full.shape[0] // N_DEV, full.shape[1])


CHIPS = ((0, 0), (0, 1), (1, 0), (1, 1))


def _exchange_siblings(grads, kinds):
    nt = len(grads)

    def body(*refs):
        ins, outs = refs[:nt], refs[nt:2 * nt]
        send_sems, recv_sems = refs[2 * nt:]
        x, y, c = _coords()
        copies = []
        for t in range(nt):
            for q, (qx, qy) in enumerate(CHIPS):
                copies.append(pltpu.make_async_remote_copy(
                    src_ref=_shard_of(ins[t], kinds[t], 4 * qx + 2 * qy + (1 - c)), dst_ref=outs[t].at[q],
                    send_sem=send_sems.at[4 * t + q], recv_sem=recv_sems.at[4 * t + q],
                    device_id=(x, y, 1 - c), device_id_type=MESH_ID))
        for cp in copies:
            cp.start()
        for cp in copies:
            cp.wait()

    return pl.pallas_call(
        body, name="grad_exchange_siblings",
        in_specs=[ANY] * nt, out_specs=[ANY] * nt,
        out_shape=[jax.ShapeDtypeStruct((4,) + _shard_shape(g, k), F32) for g, k in zip(grads, kinds)],
        scratch_shapes=[pltpu.SemaphoreType.DMA((4 * nt,)), pltpu.SemaphoreType.DMA((4 * nt,))],
    )(*grads)


def _pair_sum(grad, other, kind, core, name):
    rows, cols = _shard_shape(grad, kind)
    tr = _tile(rows, 256)

    def body(c_ref, g_ref, o_ref, s_ref):
        s_ref[...] = (g_ref[...] + o_ref[...]).astype(BF16)

    if kind == "col":
        g_spec = pl.BlockSpec((tr, cols), lambda q, i, c_ref: (i, 2 * q + c_ref[0]))
    else:
        g_spec = pl.BlockSpec((tr, cols), lambda q, i, c_ref: ((2 * q + c_ref[0]) * (rows // tr) + i, 0))
    part = pl.BlockSpec((None, tr, cols), lambda q, i, c_ref: (q, i, 0))
    return pl.pallas_call(
        body, name=name,
        grid_spec=pltpu.PrefetchScalarGridSpec(
            num_scalar_prefetch=1, grid=(4, rows // tr), in_specs=[g_spec, part], out_specs=part),
        out_shape=jax.ShapeDtypeStruct((4, rows, cols), BF16),
        compiler_params=_params("parallel", "parallel"),
    )(core, grad, other)


def _exchange_chips(parts, groups):
    nt = len(parts)
    ng = len(groups)
    place = {t: (gi, li) for gi, grp in enumerate(groups) for li, t in enumerate(grp)}

    def body(*refs):
        ins, outs = refs[:nt], refs[nt:nt + ng]
        send_sems, recv_sems, local_sems = refs[nt + ng:]
        x, y, c = _coords()
        myq = 2 * x + y
        chips = [(1 - x, y), (x, 1 - y), (1 - x, 1 - y)]
        local, sent = [], []
        for t in range(nt):
            gi, li = place[t]
            land = outs[gi].at[myq, li]
            local.append(pltpu.make_async_copy(ins[t].at[myq], land, local_sems.at[t]))
            local[-1].start()
            for j, (qx, qy) in enumerate(chips):
                sent.append(pltpu.make_async_remote_copy(
                    src_ref=ins[t].at[2 * qx + qy], dst_ref=land,
                    send_sem=send_sems.at[3 * t + j], recv_sem=recv_sems.at[3 * t + j],
                    device_id=(qx, qy, c), device_id_type=MESH_ID))
                sent[-1].start()
        for cp in sent:
            cp.wait()
        for cp in local:
            cp.wait()

    out_shape = [jax.ShapeDtypeStruct((4, len(grp)) + parts[grp[0]].shape[1:], BF16) for grp in groups]
    return pl.pallas_call(
        body, name="grad_exchange_chips",
        in_specs=[ANY] * nt, out_specs=[ANY] * ng, out_shape=out_shape,
        scratch_shapes=[pltpu.SemaphoreType.DMA((3 * nt,)), pltpu.SemaphoreType.DMA((3 * nt,)),
                        pltpu.SemaphoreType.DMA((nt,))],
    )(*parts)


def _all_reduce_small(vec):
    R = vec.shape[0]

    def body(v_ref, o_ref, recv_ref, send_sems, recv_sems):
        x, y, c = _coords()
        me = 4 * x + 2 * y + c
        recv_ref[me] = v_ref[...]
        copies = []
        for k in range(1, N_DEV):
            bx, by, bc = (k >> 2) & 1, (k >> 1) & 1, k & 1
            peer = (1 - x if bx else x, 1 - y if by else y, 1 - c if bc else c)
            copies.append(pltpu.make_async_remote_copy(
                src_ref=v_ref, dst_ref=recv_ref.at[me],
                send_sem=send_sems.at[k - 1], recv_sem=recv_sems.at[k - 1],
                device_id=peer, device_id_type=MESH_ID))
        for cp in copies:
            cp.start()
        for cp in copies:
            cp.wait()
        total = recv_ref[0]
        for p in range(1, N_DEV):
            total = total + recv_ref[p]
        o_ref[...] = total

    return pl.pallas_call(
        body, name="all_reduce_small",
        in_specs=[pl.BlockSpec(memory_space=pltpu.VMEM)], out_specs=pl.BlockSpec(memory_space=pltpu.VMEM),
        out_shape=jax.ShapeDtypeStruct((R, 128), F32),
        scratch_shapes=[pltpu.VMEM((N_DEV, R, 128), F32), pltpu.SemaphoreType.DMA((N_DEV - 1,)),
                        pltpu.SemaphoreType.DMA((N_DEV - 1,))],
        compiler_params=pltpu.CompilerParams(vmem_limit_bytes=VMEM_LIMIT),
    )(vec)


def _adamw_math(w, g, m, v):
    m = ADAM_B1 * m + (1.0 - ADAM_B1) * g
    v = ADAM_B2 * v + (1.0 - ADAM_B2) * (g * g)
    m_hat = m / (1.0 - ADAM_B1 ** ADAM_STEP)
    v_hat = v / (1.0 - ADAM_B2 ** ADAM_STEP)
    delta = -ADAM_LR * (m_hat / (jnp.sqrt(v_hat) + ADAM_EPS) + ADAM_WD * w)
    return delta, m, v


def _adamw(w, parts, m, v, name):
    rows, cols = w.shape
    n = parts.shape[0]
    tr = rows if rows * cols <= 256 * 1024 else _tile(rows, 128)

    def body(w_ref, p_ref, m_ref, v_ref, g_ref, d_ref, mo_ref, vo_ref):
        g = p_ref[0].astype(F32)
        for q in range(1, n):
            g = g + p_ref[q].astype(F32)
        g_ref[...] = g
        d_ref[...], mo_ref[...], vo_ref[...] = _adamw_math(w_ref[...], g, m_ref[...], v_ref[...])

    blk = pl.BlockSpec((tr, cols), lambda i: (i, 0))
    return pl.pallas_call(
        body, name=name, grid=(rows // tr,),
        in_specs=[blk, pl.BlockSpec((n, tr, cols), lambda i: (0, i, 0)), blk, blk],
        out_specs=[blk] * 4, out_shape=[jax.ShapeDtypeStruct((rows, cols), F32)] * 4,
        compiler_params=_params("parallel"),
    )(w, parts, m, v)


def _pack(arrays):
    rows = []
    for a in arrays:
        flat = a.reshape(-1)
        pad = (-flat.shape[0]) % 1024
        rows.append(jnp.pad(flat, (0, pad)).reshape(-1, 128))
    return jnp.concatenate(rows, axis=0)


def _unpack(packed, like):
    out, r = [], 0
    for a in like:
        n = math.prod(a.shape)
        nr = (n + 1023) // 1024 * 8
        out.append(packed[r:r + nr].reshape(-1)[:n].reshape(a.shape))
        r += nr
    return out


SMALL = ("norm_pre_mix", "norm_post_mix", "norm_pre_ffn", "norm_post_ffn", "sgu_ln_g", "sgu_ln_b", "sgu_w", "sgu_b")
ORDER = ("norm_pre_mix", "norm_post_mix", "norm_pre_ffn", "norm_post_ffn", "ab_w_in", "sgu_ln_g", "sgu_ln_b", "sgu_w",
         "sgu_b", "ab_w_out", "sb_w_in", "sb_w_out", "ffn_w1", "ffn_w2")


def kernel(x, norm_pre_mix, norm_post_mix, norm_pre_ffn, norm_post_ffn, ab_w_in, sgu_ln_g, sgu_ln_b, sgu_w, sgu_b, ab_w_out, sb_w_in, sb_w_out, ffn_w1, ffn_w2, loss_target, m_norm_pre_mix, m_norm_post_mix, m_norm_pre_ffn, m_norm_post_ffn, m_ab_w_in, m_sgu_ln_g, m_sgu_ln_b, m_sgu_w, m_sgu_b, m_ab_w_out, m_sb_w_in, m_sb_w_out, m_ffn_w1, m_ffn_w2, v_norm_pre_mix, v_norm_post_mix, v_norm_pre_ffn, v_norm_post_ffn, v_ab_w_in, v_sgu_ln_g, v_sgu_ln_b, v_sgu_w, v_sgu_b, v_ab_w_out, v_sb_w_in, v_sb_w_out, v_ffn_w1, v_ffn_w2):
    W = dict(norm_pre_mix=norm_pre_mix, norm_post_mix=norm_post_mix, norm_pre_ffn=norm_pre_ffn,
             norm_post_ffn=norm_post_ffn, ab_w_in=ab_w_in, sgu_ln_g=sgu_ln_g, sgu_ln_b=sgu_ln_b, sgu_w=sgu_w,
             sgu_b=sgu_b, ab_w_out=ab_w_out, sb_w_in=sb_w_in, sb_w_out=sb_w_out, ffn_w1=ffn_w1, ffn_w2=ffn_w2)
    M = dict(norm_pre_mix=m_norm_pre_mix, norm_post_mix=m_norm_post_mix, norm_pre_ffn=m_norm_pre_ffn,
             norm_post_ffn=m_norm_post_ffn, ab_w_in=m_ab_w_in, sgu_ln_g=m_sgu_ln_g, sgu_ln_b=m_sgu_ln_b,
             sgu_w=m_sgu_w, sgu_b=m_sgu_b, ab_w_out=m_ab_w_out, sb_w_in=m_sb_w_in, sb_w_out=m_sb_w_out,
             ffn_w1=m_ffn_w1, ffn_w2=m_ffn_w2)
    V = dict(norm_pre_mix=v_norm_pre_mix, norm_post_mix=v_norm_post_mix, norm_pre_ffn=v_norm_pre_ffn,
             norm_post_ffn=v_norm_post_ffn, ab_w_in=v_ab_w_in, sgu_ln_g=v_sgu_ln_g, sgu_ln_b=v_sgu_ln_b,
             sgu_w=v_sgu_w, sgu_b=v_sgu_b, ab_w_out=v_ab_w_out, sb_w_in=v_sb_w_in, sb_w_out=v_sb_w_out,
             ffn_w1=v_ffn_w1, ffn_w2=v_ffn_w2)

    shards = [ab_w_in[0], ab_w_out[0], sb_w_in[0], sb_w_out[0], ffn_w1[0], ffn_w1[1], ffn_w2[0], ffn_w2[1]]
    kinds = ["col", "row", "col", "row", "col", "col", "row", "row"]
    full = _all_gather([s.astype(BF16) for s in shards], kinds)
    w = {"ab_in": full[0], "ab_out": full[1], "sb_in": full[2], "sb_out": full[3],
         "w1": (full[4], full[5]), "w2": (full[6], full[7])}

    norms = (norm_pre_mix, norm_post_mix, norm_pre_ffn, norm_post_ffn)
    sgu = (sgu_ln_g, sgu_ln_b, sgu_w[0], sgu_b[0])
    loss, dx, g, small = _local_step(x[0], loss_target[0], norms, sgu, w)
    loss = lax.psum(loss, MESH_AXES)

    grads = [g["ab_in"], g["ab_out"], g["sb_in"], g["sb_out"], g["w1"][0], g["w1"][1], g["w2"][0], g["w2"][1]]
    names = ["ab_in", "ab_out", "sb_in", "sb_out", "w1_0", "w1_1", "w2_0", "w2_1"]
    others = _exchange_siblings(grads, kinds)
    core = lax.axis_index("c").astype(jnp.int32).reshape(1)
    parts = [_pair_sum(gr, ot, k, core, f"pair_sum_{n}") for gr, ot, k, n in zip(grads, others, kinds, names)]
    landed = _exchange_chips(parts, [[0], [1], [2], [3], [4, 5], [6, 7]])

    out = {}
    for name, buf in zip(("ab_w_in", "ab_w_out", "sb_w_in", "sb_w_out", "ffn_w1", "ffn_w2"), landed):
        shape = W[name].shape
        flat = (shape[0] * shape[1], shape[2])
        res = _adamw(W[name].reshape(flat), buf.reshape((4,) + flat), M[name].reshape(flat), V[name].reshape(flat),
                     f"adamw_{name}")
        out[name] = [r.reshape(shape) for r in res]

    small_g = [small["pre_mix"], small["post_mix"], small["pre_ffn"], small["post_ffn"], small["ln_g"],
               small["ln_b"], small["w_s"][None], small["b_s"][None]]
    g_small = _all_reduce_small(_pack(small_g))
    res = _adamw(_pack([W[n] for n in SMALL]), g_small[None], _pack([M[n] for n in SMALL]),
                 _pack([V[n] for n in SMALL]), "adamw_small")
    like = [W[n] for n in SMALL]
    for n, *vals in zip(SMALL, *[_unpack(r, like) for r in res]):
        out[n] = vals

    return (loss, dx[None], *[out[n][0] for n in ORDER], *[out[n][1] for n in ORDER],
            *[out[n][2] for n in ORDER], *[out[n][3] for n in ORDER])
```

```python
import functools
import math

import jax
import jax.numpy as jnp
from jax import lax
from jax.experimental import pallas as pl
from jax.experimental.pallas import tpu as pltpu

F32 = jnp.float32
BF16 = jnp.bfloat16

HEAD_DIM = 128
CHUNK = 128
ATT_BLOCK = 128
DILATED_PAIRS = ((128, 1), (512, 4), (2048, 16))
RMS_EPS = 1e-6
LN_EPS = 1e-5
ADAM_LR = 0.001
ADAM_B1 = 0.9
ADAM_B2 = 0.999
ADAM_EPS = 1e-08
ADAM_WD = 0.01
ADAM_STEP = 10
N_DEV = 8
MESH_AXES = ("x", "y", "c")
MASKED = -1e30

V7X_VMEM_BYTES = 64 * 1024 * 1024
VMEM_LIMIT = V7X_VMEM_BYTES - 8 * 1024 * 1024

NN = (((1,), (0,)), ((), ()))
NT = (((1,), (1,)), ((), ()))
TN = (((0,), (0,)), ((), ()))


def _params(*sem):
    return pltpu.CompilerParams(dimension_semantics=sem, vmem_limit_bytes=VMEM_LIMIT)


def _dot(a, b, dims=NN):
    return lax.dot_general(a, b, dims, preferred_element_type=F32)


def _tile(n, preferred):
    if n <= preferred:
        return n
    t = preferred - preferred % 128
    while n % t:
        t -= 128
    assert t > 0, (n, preferred)
    return t


def _matmul(a, b, *, mode, name, out_dtype=F32, tm=1024, tn=512, tk=2048, epi=None, extras=()):
    if mode == "nn":
        (M, K), N = a.shape, b.shape[1]
    elif mode == "nt":
        (M, K), N = a.shape, b.shape[0]
    else:
        (K, M), N = a.shape, b.shape[1]
    tm, tn, tk = _tile(M, tm), _tile(N, tn), _tile(K, tk)
    nk = K // tk
    if mode == "tn":
        a_spec = pl.BlockSpec((tk, tm), lambda i, j, k: (k, i))
    else:
        a_spec = pl.BlockSpec((tm, tk), lambda i, j, k: (i, k))
    if mode == "nt":
        b_spec = pl.BlockSpec((tn, tk), lambda i, j, k: (j, k))
    else:
        b_spec = pl.BlockSpec((tk, tn), lambda i, j, k: (k, j))
    o_spec = pl.BlockSpec((tm, tn), lambda i, j, k: (i, j))
    dims = {"nn": NN, "nt": NT, "tn": TN}[mode]
    n_extra = len(extras)

    def finish(acc, refs):
        j = pl.program_id(1)
        if epi is None:
            return acc
        return epi(acc, j, *[r[...] for r in refs])

    if nk == 1:
        def body(a_ref, b_ref, *rest):
            o_ref = rest[n_extra]
            acc = _dot(a_ref[...], b_ref[...], dims)
            o_ref[...] = finish(acc, rest[:n_extra]).astype(o_ref.dtype)
        scratch = []
    else:
        def body(a_ref, b_ref, *rest):
            o_ref, acc_ref = rest[n_extra], rest[n_extra + 1]
            k = pl.program_id(2)

            @pl.when(k == 0)
            def _():
                acc_ref[...] = jnp.zeros_like(acc_ref)

            acc_ref[...] += _dot(a_ref[...], b_ref[...], dims)

            @pl.when(k == nk - 1)
            def _():
                o_ref[...] = finish(acc_ref[...], rest[:n_extra]).astype(o_ref.dtype)
        scratch = [pltpu.VMEM((tm, tn), F32)]

    return pl.pallas_call(
        body,
        name=name,
        grid=(M // tm, N // tn, nk),
        in_specs=[a_spec, b_spec] + [o_spec] * n_extra,
        out_specs=o_spec,
        out_shape=jax.ShapeDtypeStruct((M, N), out_dtype),
        scratch_shapes=scratch,
        compiler_params=_params("parallel", "parallel", "arbitrary"),
    )(a, b, *extras)


ROWS = 256


def _rms(x):
    return lax.rsqrt(jnp.mean(x * x, axis=-1, keepdims=True) + RMS_EPS)


def _prenorm(x, g, name):
    T, D = x.shape

    def body(x_ref, g_ref, h_ref):
        xv = x_ref[...]
        h_ref[...] = (xv * _rms(xv) * g_ref[...]).astype(BF16)

    row = pl.BlockSpec((ROWS, D), lambda i: (i, 0))
    vec = pl.BlockSpec((1, D), lambda i: (0, 0))
    return pl.pallas_call(
        body, name=name, grid=(T // ROWS,), in_specs=[row, vec], out_specs=row,
        out_shape=jax.ShapeDtypeStruct((T, D), BF16), compiler_params=_params("parallel"),
    )(x, g)


def _postnorm_prenorm(x, y, g_post, g_pre, name):
    T, D = x.shape

    def body(x_ref, y_ref, gp_ref, gn_ref, xo_ref, h_ref):
        yv = y_ref[...]
        xn = x_ref[...] + yv * _rms(yv) * gp_ref[...]
        xo_ref[...] = xn
        h_ref[...] = (xn * _rms(xn) * gn_ref[...]).astype(BF16)

    row = pl.BlockSpec((ROWS, D), lambda i: (i, 0))
    vec = pl.BlockSpec((1, D), lambda i: (0, 0))
    return pl.pallas_call(
        body, name=name, grid=(T // ROWS,), in_specs=[row, row, vec, vec], out_specs=[row, row],
        out_shape=[jax.ShapeDtypeStruct((T, D), F32), jax.ShapeDtypeStruct((T, D), BF16)],
        compiler_params=_params("parallel"),
    )(x, y, g_post, g_pre)


def _postnorm_loss(x, y, g_post, target, name):
    T, D = x.shape

    def body(x_ref, y_ref, gp_ref, t_ref, loss_ref, dx_ref):
        @pl.when(pl.program_id(0) == 0)
        def _():
            loss_ref[...] = jnp.zeros_like(loss_ref)

        yv = y_ref[...]
        err = x_ref[...] + yv * _rms(yv) * gp_ref[...] - t_ref[...]
        dx_ref[...] = err * (1.0 / D)
        loss_ref[...] += 0.5 * jnp.sum(jnp.sum(err * err, axis=-1, keepdims=True) * (1.0 / D))

    row = pl.BlockSpec((ROWS, D), lambda i: (i, 0))
    vec = pl.BlockSpec((1, D), lambda i: (0, 0))
    acc = pl.BlockSpec((8, 128), lambda i: (0, 0))
    return pl.pallas_call(
        body, name=name, grid=(T // ROWS,), in_specs=[row, row, vec, row], out_specs=[acc, row],
        out_shape=[jax.ShapeDtypeStruct((8, 128), F32), jax.ShapeDtypeStruct((T, D), F32)],
        compiler_params=_params("arbitrary"),
    )(x, y, g_post, target)


def _postnorm_bwd(dx, y, g_post, name):
    T, D = dx.shape

    def body(dx_ref, y_ref, g_ref, dy_ref, dg_ref):
        @pl.when(pl.program_id(0) == 0)
        def _():
            dg_ref[...] = jnp.zeros_like(dg_ref)

        yv, dn = y_ref[...], dx_ref[...]
        r = _rms(yv)
        yh = yv * r
        gd = dn * g_ref[...]
        dy_ref[...] = (r * (gd - yh * jnp.mean(yh * gd, axis=-1, keepdims=True))).astype(BF16)
        dg_ref[...] += jnp.sum(dn * yh, axis=0, keepdims=True)

    row = pl.BlockSpec((ROWS, D), lambda i: (i, 0))
    vec = pl.BlockSpec((1, D), lambda i: (0, 0))
    return pl.pallas_call(
        body, name=name, grid=(T // ROWS,), in_specs=[row, row, vec], out_specs=[row, vec],
        out_shape=[jax.ShapeDtypeStruct((T, D), BF16), jax.ShapeDtypeStruct((1, D), F32)],
        compiler_params=_params("arbitrary"),
    )(dx, y, g_post)


def _prenorm_bwd(dx_out, dh, x, g_pre, name):
    T, D = x.shape

    def body(dxo_ref, dh_ref, x_ref, g_ref, dx_ref, dg_ref):
        @pl.when(pl.program_id(0) == 0)
        def _():
            dg_ref[...] = jnp.zeros_like(dg_ref)

        xv, dhv = x_ref[...], dh_ref[...]
        r = _rms(xv)
        xh = xv * r
        gd = dhv * g_ref[...]
        dx_ref[...] = dxo_ref[...] + r * (gd - xh * jnp.mean(xh * gd, axis=-1, keepdims=True))
        dg_ref[...] += jnp.sum(dhv * xh, axis=0, keepdims=True)

    row = pl.BlockSpec((ROWS, D), lambda i: (i, 0))
    vec = pl.BlockSpec((1, D), lambda i: (0, 0))
    return pl.pallas_call(
        body, name=name, grid=(T // ROWS,), in_specs=[row, row, row, vec], out_specs=[row, vec],
        out_shape=[jax.ShapeDtypeStruct((T, D), F32), jax.ShapeDtypeStruct((1, D), F32)],
        compiler_params=_params("arbitrary"),
    )(dx_out, dh, x, g_pre)


_INV_SQRT2 = 1.0 / math.sqrt(2.0)
_INV_SQRT2PI = 1.0 / math.sqrt(2.0 * math.pi)


def _gelu(x):
    return 0.5 * x * (1.0 + lax.erf(x * _INV_SQRT2))


def _gelu_grad(x):
    return 0.5 * (1.0 + lax.erf(x * _INV_SQRT2)) + x * jnp.exp(-0.5 * x * x) * _INV_SQRT2PI


def _layernorm_stats(x):
    mu = jnp.mean(x, axis=-1, keepdims=True)
    xc = x - mu
    rstd = lax.rsqrt(jnp.mean(xc * xc, axis=-1, keepdims=True) + LN_EPS)
    return xc * rstd, rstd


def _tril_mask():
    i = lax.broadcasted_iota(jnp.int32, (CHUNK, CHUNK), 0)
    j = lax.broadcasted_iota(jnp.int32, (CHUNK, CHUNK), 1)
    return j <= i


SGU_ROWS = 512


def _sgu_fwd(z, ln_g, ln_b, w_s, b_t, name):
    T = z.shape[0]
    A = ln_g.shape[1]
    G = A // 128
    rows = min(SGU_ROWS, T)

    def body(u_ref, v_ref, g_ref, b_ref, w_ref, bt_ref, o_ref):
        mask = _tril_mask()
        for c in range(rows // CHUNK):
            rs = pl.ds(c * CHUNK, CHUNK)
            xh, _ = _layernorm_stats(_gelu(v_ref[rs, :]))
            vn = (xh * g_ref[...] + b_ref[...]).astype(BF16)
            for g in range(G):
                cs = pl.ds(g * 128, 128)
                w = jnp.where(mask, w_ref[g], 0.0).astype(BF16)
                mixed = _dot(w, vn[:, g * 128:(g + 1) * 128]) + bt_ref[:, g:g + 1]
                o_ref[rs, cs] = (_gelu(u_ref[rs, cs]) * mixed).astype(BF16)

    return pl.pallas_call(
        body, name=name, grid=(T // rows,),
        in_specs=[
            pl.BlockSpec((rows, A), lambda i: (i, 0)),
            pl.BlockSpec((rows, A), lambda i: (i, 1)),
            pl.BlockSpec((1, A), lambda i: (0, 0)),
            pl.BlockSpec((1, A), lambda i: (0, 0)),
            pl.BlockSpec((G, CHUNK, CHUNK), lambda i: (0, 0, 0)),
            pl.BlockSpec((CHUNK, G), lambda i: (0, 0)),
        ],
        out_specs=pl.BlockSpec((rows, A), lambda i: (i, 0)),
        out_shape=jax.ShapeDtypeStruct((T, A), BF16),
        compiler_params=_params("parallel"),
    )(z, z, ln_g, ln_b, w_s, b_t)


def _sgu_bwd(z, dcat, ln_g, ln_b, w_s, b_t, name):
    T = z.shape[0]
    A = ln_g.shape[1]
    G = A // 128
    rows = min(SGU_ROWS, T)

    def body(u_ref, v_ref, da_ref, g_ref, b_ref, w_ref, bt_ref, dz_ref, dg_ref, db_ref, dw_ref, dbt_ref, dvn_ref):
        @pl.when(pl.program_id(0) == 0)
        def _():
            dg_ref[...] = jnp.zeros_like(dg_ref)
            db_ref[...] = jnp.zeros_like(db_ref)
            dw_ref[...] = jnp.zeros_like(dw_ref)
            dbt_ref[...] = jnp.zeros_like(dbt_ref)

        mask = _tril_mask()
        for c in range(rows // CHUNK):
            rs = pl.ds(c * CHUNK, CHUNK)
            vv = v_ref[rs, :]
            gv = _gelu(vv)
            xh, rstd = _layernorm_stats(gv)
            vn = (xh * g_ref[...] + b_ref[...]).astype(BF16)
            for g in range(G):
                cs = pl.ds(g * 128, 128)
                w = jnp.where(mask, w_ref[g], 0.0).astype(BF16)
                vg = vn[:, g * 128:(g + 1) * 128]
                mixed = _dot(w, vg) + bt_ref[:, g:g + 1]
                uu = u_ref[rs, cs]
                da = da_ref[rs, cs]
                dz_ref[rs, cs] = (da * mixed * _gelu_grad(uu)).astype(BF16)
                dm = da * _gelu(uu)
                dmb = dm.astype(BF16)
                dbt_ref[:, g:g + 1] += jnp.sum(dm, axis=1, keepdims=True)
                dw_ref[g] += jnp.where(mask, _dot(dmb, vg, NT), 0.0)
                dvn_ref[:, cs] = _dot(w, dmb, TN)
            dvn = dvn_ref[...]
            dg_ref[...] += jnp.sum(dvn * xh, axis=0, keepdims=True)
            db_ref[...] += jnp.sum(dvn, axis=0, keepdims=True)
            dxh = dvn * g_ref[...]
            dgv = rstd * (dxh - jnp.mean(dxh, axis=-1, keepdims=True)
                          - xh * jnp.mean(dxh * xh, axis=-1, keepdims=True))
            dz_ref[rs, pl.ds(A, A)] = (dgv * _gelu_grad(vv)).astype(BF16)

    vec = pl.BlockSpec((1, A), lambda i: (0, 0))
    wsp = pl.BlockSpec((G, CHUNK, CHUNK), lambda i: (0, 0, 0))
    bsp = pl.BlockSpec((CHUNK, G), lambda i: (0, 0))
    return pl.pallas_call(
        body, name=name, grid=(T // rows,),
        in_specs=[
            pl.BlockSpec((rows, A), lambda i: (i, 0)),
            pl.BlockSpec((rows, A), lambda i: (i, 1)),
            pl.BlockSpec((rows, A), lambda i: (i, 0)),
            vec, vec, wsp, bsp,
        ],
        out_specs=[pl.BlockSpec((rows, 2 * A), lambda i: (i, 0)), vec, vec, wsp, bsp],
        out_shape=[
            jax.ShapeDtypeStruct((T, 2 * A), BF16),
            jax.ShapeDtypeStruct((1, A), F32),
            jax.ShapeDtypeStruct((1, A), F32),
            jax.ShapeDtypeStruct((G, CHUNK, CHUNK), F32),
            jax.ShapeDtypeStruct((CHUNK, G), F32),
        ],
        scratch_shapes=[pltpu.VMEM((CHUNK, A), F32)],
        compiler_params=_params("arbitrary"),
    )(z, z, dcat, ln_g, ln_b, w_s, b_t)


def _alibi_slope(h, n_heads):
    return 2.0 ** (-8.0 * (h + 1.0) / n_heads)


def _dil_scores(q, k, slope_d, valid, dist):
    s = _dot(q, k, NT) - slope_d * dist
    return jnp.where(valid, s, MASKED)


def _dilated_fwd(z, d, B, name):
    T = z.shape[0]
    H = B // HEAD_DIM
    nb = T // (d * ATT_BLOCK)
    zv = z.reshape(T // d, d * 5 * B)
    scale = HEAD_DIM ** -0.5
    blk = ATT_BLOCK

    def body(q_ref, kp_ref, kc_ref, vp_ref, vc_ref, o_ref, l_ref):
        n = pl.program_id(1)
        qi = lax.broadcasted_iota(jnp.int32, (blk, 2 * blk), 0)
        kj = lax.broadcasted_iota(jnp.int32, (blk, 2 * blk), 1)
        dist = qi + blk - kj
        valid = (dist >= 0) & (dist <= blk) & ((kj >= blk) | (n > 0))
        distf = dist.astype(F32)
        for h in range(H):
            cs = pl.ds(h * HEAD_DIM, HEAD_DIM)
            q = (q_ref[:, cs] * scale).astype(BF16)
            k = jnp.concatenate([kp_ref[:, cs], kc_ref[:, cs]], axis=0).astype(BF16)
            v = jnp.concatenate([vp_ref[:, cs], vc_ref[:, cs]], axis=0).astype(BF16)
            s = _dil_scores(q, k, _alibi_slope(h, H) * d, valid, distf)
            m = jnp.max(s, axis=-1, keepdims=True)
            p = jnp.exp(s - m)
            den = jnp.sum(p, axis=-1, keepdims=True)
            o_ref[:, cs] = _dot(p.astype(BF16), v) / den
            l_ref[:, cs] = jnp.broadcast_to(m + jnp.log(den), (blk, HEAD_DIM))

    def col(unit):
        return lambda r, n: (n, r * 5 + unit)

    def col_prev(unit):
        return lambda r, n: (jnp.maximum(n - 1, 0), r * 5 + unit)

    bs = (blk, B)
    out = pl.BlockSpec(bs, lambda r, n: (n, r))
    o, lse = pl.pallas_call(
        body, name=name, grid=(d, nb),
        in_specs=[pl.BlockSpec(bs, col(2)), pl.BlockSpec(bs, col_prev(3)), pl.BlockSpec(bs, col(3)),
                  pl.BlockSpec(bs, col_prev(4)), pl.BlockSpec(bs, col(4))],
        out_specs=[out, out],
        out_shape=[jax.ShapeDtypeStruct((T // d, d * B), F32)] * 2,
        compiler_params=_params("parallel", "parallel"),
    )(zv, zv, zv, zv, zv)
    return o.reshape(T, B), lse.reshape(T, B)


def _dilated_merge(os_, ls_, name):
    T, B = os_[0].shape

    def body(o1, o2, o3, l1, l2, l3, ob_ref, of_ref, lt_ref):
        a, b, c = l1[...], l2[...], l3[...]
        m = jnp.maximum(jnp.maximum(a, b), c)
        ea, eb, ec = jnp.exp(a - m), jnp.exp(b - m), jnp.exp(c - m)
        tot = ea + eb + ec
        o = (ea * o1[...] + eb * o2[...] + ec * o3[...]) / tot
        of_ref[...] = o
        ob_ref[...] = o.astype(BF16)
        lt_ref[...] = m + jnp.log(tot)

    row = pl.BlockSpec((ROWS, B), lambda i: (i, 0))
    return pl.pallas_call(
        body, name=name, grid=(T // ROWS,), in_specs=[row] * 6, out_specs=[row] * 3,
        out_shape=[jax.ShapeDtypeStruct((T, B), BF16), jax.ShapeDtypeStruct((T, B), F32),
                   jax.ShapeDtypeStruct((T, B), F32)],
        compiler_params=_params("parallel"),
    )(*os_, *ls_)


def _dilated_delta(dcat, o, name):
    T, B = o.shape
    H = B // HEAD_DIM

    def body(do_ref, o_ref, d_ref):
        for h in range(H):
            cs = pl.ds(h * HEAD_DIM, HEAD_DIM)
            dsum = jnp.sum(do_ref[:, cs] * o_ref[:, cs], axis=-1, keepdims=True)
            d_ref[:, cs] = jnp.broadcast_to(dsum, (ROWS, HEAD_DIM))

    row = pl.BlockSpec((ROWS, B), lambda i: (i, 0))
    return pl.pallas_call(
        body, name=name, grid=(T // ROWS,),
        in_specs=[pl.BlockSpec((ROWS, B), lambda i: (i, 1)), row], out_specs=row,
        out_shape=jax.ShapeDtypeStruct((T, B), F32), compiler_params=_params("parallel"),
    )(dcat, o)


def _dilated_dq(z, dcat, lse, delta, d, B, name):
    T = z.shape[0]
    H = B // HEAD_DIM
    nb = T // (d * ATT_BLOCK)
    zv = z.reshape(T // d, d * 5 * B)
    dov = dcat.reshape(T // d, d * 2 * B)
    lv = lse.reshape(T // d, d * B)
    dv_ = delta.reshape(T // d, d * B)
    scale = HEAD_DIM ** -0.5
    blk = ATT_BLOCK

    def body(q_ref, kp_ref, kc_ref, vp_ref, vc_ref, do_ref, l_ref, dl_ref, dq_ref):
        n = pl.program_id(1)
        qi = lax.broadcasted_iota(jnp.int32, (blk, 2 * blk), 0)
        kj = lax.broadcasted_iota(jnp.int32, (blk, 2 * blk), 1)
        dist = qi + blk - kj
        valid = (dist >= 0) & (dist <= blk) & ((kj >= blk) | (n > 0))
        distf = dist.astype(F32)
        for h in range(H):
            cs = pl.ds(h * HEAD_DIM, HEAD_DIM)
            q = (q_ref[:, cs] * scale).astype(BF16)
            k = jnp.concatenate([kp_ref[:, cs], kc_ref[:, cs]], axis=0).astype(BF16)
            v = jnp.concatenate([vp_ref[:, cs], vc_ref[:, cs]], axis=0).astype(BF16)
            s = _dil_scores(q, k, _alibi_slope(h, H) * d, valid, distf)
            p = jnp.exp(s - l_ref[:, h * HEAD_DIM:h * HEAD_DIM + 1])
            dp = _dot(do_ref[:, cs].astype(BF16), v, NT)
            ds = p * (dp - dl_ref[:, h * HEAD_DIM:h * HEAD_DIM + 1])
            dq_ref[:, cs] = _dot(ds.astype(BF16), k)

    def col(unit):
        return lambda r, n: (n, r * 5 + unit)

    def col_prev(unit):
        return lambda r, n: (jnp.maximum(n - 1, 0), r * 5 + unit)

    bs = (blk, B)
    out = pl.BlockSpec(bs, lambda r, n: (n, r))
    dq = pl.pallas_call(
        body, name=name, grid=(d, nb),
        in_specs=[pl.BlockSpec(bs, col(2)), pl.BlockSpec(bs, col_prev(3)), pl.BlockSpec(bs, col(3)),
                  pl.BlockSpec(bs, col_prev(4)), pl.BlockSpec(bs, col(4)),
                  pl.BlockSpec(bs, lambda r, n: (n, r * 2 + 1)), out, out],
        out_specs=out,
        out_shape=jax.ShapeDtypeStruct((T // d, d * B), F32),
        compiler_params=_params("parallel", "parallel"),
    )(zv, zv, zv, zv, zv, dov, lv, dv_)
    return dq.reshape(T, B)


def _dilated_dkv(z, dcat, lse, delta, d, B, name):
    T = z.shape[0]
    H = B // HEAD_DIM
    nb = T // (d * ATT_BLOCK)
    zv = z.reshape(T // d, d * 5 * B)
    dov = dcat.reshape(T // d, d * 2 * B)
    lv = lse.reshape(T // d, d * B)
    dv_ = delta.reshape(T // d, d * B)
    scale = HEAD_DIM ** -0.5
    blk = ATT_BLOCK

    def body(k_ref, v_ref, qa_ref, qb_ref, doa_ref, dob_ref, la_ref, lb_ref, da_ref, db_ref, dk_ref, dv_ref):
        m = pl.program_id(1)
        qi = lax.broadcasted_iota(jnp.int32, (2 * blk, blk), 0)
        kj = lax.broadcasted_iota(jnp.int32, (2 * blk, blk), 1)
        dist = qi - kj
        valid = (dist >= 0) & (dist <= blk) & ((qi < blk) | (m + 1 < nb))
        distf = dist.astype(F32)
        for h in range(H):
            cs = pl.ds(h * HEAD_DIM, HEAD_DIM)
            c1 = slice(h * HEAD_DIM, h * HEAD_DIM + 1)
            q = (jnp.concatenate([qa_ref[:, cs], qb_ref[:, cs]], axis=0) * scale).astype(BF16)
            do = jnp.concatenate([doa_ref[:, cs], dob_ref[:, cs]], axis=0).astype(BF16)
            lse2 = jnp.concatenate([la_ref[:, c1], lb_ref[:, c1]], axis=0)
            dl2 = jnp.concatenate([da_ref[:, c1], db_ref[:, c1]], axis=0)
            k = k_ref[:, cs].astype(BF16)
            v = v_ref[:, cs].astype(BF16)
            s = _dil_scores(q, k, _alibi_slope(h, H) * d, valid, distf)
            p = jnp.exp(jnp.where(valid, s - lse2, MASKED))
            dv_ref[:, cs] = _dot(p.astype(BF16), do, TN)
            dp = _dot(do, v, NT)
            ds = p * (dp - dl2)
            dk_ref[:, cs] = _dot(ds.astype(BF16), q, TN)

    def col(unit):
        return lambda r, m: (m, r * 5 + unit)

    def nxt(width, unit):
        return lambda r, m: (jnp.minimum(m + 1, nb - 1), r * width + unit)

    bs = (blk, B)
    out = pl.BlockSpec(bs, lambda r, m: (m, r))
    dk, dv = pl.pallas_call(
        body, name=name, grid=(d, nb),
        in_specs=[pl.BlockSpec(bs, col(3)), pl.BlockSpec(bs, col(4)),
                  pl.BlockSpec(bs, col(2)), pl.BlockSpec(bs, nxt(5, 2)),
                  pl.BlockSpec(bs, lambda r, m: (m, r * 2 + 1)), pl.BlockSpec(bs, nxt(2, 1)),
                  out, pl.BlockSpec(bs, nxt(1, 0)), out, pl.BlockSpec(bs, nxt(1, 0))],
        out_specs=[out, out],
        out_shape=[jax.ShapeDtypeStruct((T // d, d * B), F32)] * 2,
        compiler_params=_params("parallel", "parallel"),
    )(zv, zv, zv, zv, dov, dov, lv, lv, dv_, dv_)
    return dk.reshape(T, B), dv.reshape(T, B)


def _dilated_combine(dqs, dks, dvs, name):
    T, B = dqs[0].shape
    scale = HEAD_DIM ** -0.5

    def body(q1, q2, q3, k1, k2, k3, v1, v2, v3, o_ref):
        o_ref[:, pl.ds(0, B)] = ((q1[...] + q2[...] + q3[...]) * scale).astype(BF16)
        o_ref[:, pl.ds(B, B)] = (k1[...] + k2[...] + k3[...]).astype(BF16)
        o_ref[:, pl.ds(2 * B, B)] = (v1[...] + v2[...] + v3[...]).astype(BF16)

    row = pl.BlockSpec((ROWS, B), lambda i: (i, 0))
    return pl.pallas_call(
        body, name=name, grid=(T // ROWS,), in_specs=[row] * 9,
        out_specs=pl.BlockSpec((ROWS, 3 * B), lambda i: (i, 0)),
        out_shape=jax.ShapeDtypeStruct((T, 3 * B), BF16), compiler_params=_params("parallel"),
    )(*dqs, *dks, *dvs)


def _split_dot(x, tri):
    hi = x.astype(BF16)
    lo = (x - hi.astype(F32)).astype(BF16)
    return _dot(hi, tri) + _dot(lo, tri)


SB_QUERY_ROWS = 512


def _tri_and_ones(pred):
    rows = lax.broadcasted_iota(jnp.int32, (ATT_BLOCK, 2 * ATT_BLOCK), 0)
    cols = lax.broadcasted_iota(jnp.int32, (ATT_BLOCK, 2 * ATT_BLOCK), 1)
    return ((cols >= ATT_BLOCK) | pred(rows, cols)).astype(BF16)


def _sb_mask(query_rows, s):
    rows = lax.broadcasted_iota(jnp.int32, (query_rows, ATT_BLOCK), 0)
    cols = lax.broadcasted_iota(jnp.int32, (query_rows, ATT_BLOCK), 1)
    return cols + s * ATT_BLOCK < rows


def _log_sigmoids(z):
    e = jnp.exp(-jnp.abs(z))
    sp = jnp.log(1.0 + e)
    return jnp.minimum(z, 0.0) - sp, jnp.minimum(-z, 0.0) - sp, e


def _sb_fwd(qkv, W, name):
    T = qkv.shape[0]
    H = W // HEAD_DIM
    blk = ATT_BLOCK
    qb = min(SB_QUERY_ROWS, T)
    per = qb // blk

    def body(q_ref, k_ref, v_ref, o_ref, lt_ref, acc_ref):
        i = pl.program_id(1)
        q = q_ref[...]
        tri = _tri_and_ones(lambda r, c: r > c)
        lt_ref[...] = jnp.zeros_like(lt_ref)
        acc_ref[...] = jnp.zeros_like(acc_ref)

        def tile(j, mask):
            ks = pl.ds(pl.multiple_of(j * blk, blk), blk)
            z = _dot(q, k_ref[ks, :], NT)
            ls, lm, _ = _log_sigmoids(z)
            if mask is not None:
                lm = jnp.where(mask, lm, 0.0)
            both = _split_dot(lm, tri)
            a = jnp.exp(ls + both[:, :blk] + lt_ref[...])
            if mask is not None:
                a = jnp.where(mask, a, 0.0)
            acc_ref[...] += _dot(a.astype(BF16), v_ref[ks, :])
            lt_ref[...] += both[:, blk:]

        for s in reversed(range(per)):
            tile(i * per + s, _sb_mask(qb, s))

        def step(jj, _):
            tile(i * per - 1 - jj, None)
            return 0

        lax.fori_loop(0, i * per, step, 0)
        o_ref[...] = acc_ref[...].astype(BF16)

    qs = pl.BlockSpec((qb, HEAD_DIM), lambda h, i: (i, h))
    return pl.pallas_call(
        body, name=name, grid=(H, T // qb),
        in_specs=[qs, pl.BlockSpec((T, HEAD_DIM), lambda h, i: (0, H + h)),
                  pl.BlockSpec((T, HEAD_DIM), lambda h, i: (0, 2 * H + h))],
        out_specs=[qs, qs],
        out_shape=[jax.ShapeDtypeStruct((T, W), BF16), jax.ShapeDtypeStruct((T, W), F32)],
        scratch_shapes=[pltpu.VMEM((qb, HEAD_DIM), F32)],
        compiler_params=_params("parallel", "arbitrary"),
    )(qkv, qkv, qkv)


def _sb_bwd(qkv, do, ltot, W, name):
    T = qkv.shape[0]
    H = W // HEAD_DIM
    blk = ATT_BLOCK
    nkb = T // blk
    qb = min(SB_QUERY_ROWS, T)
    per = qb // blk

    def body(q_ref, k_ref, v_ref, do_ref, lt_ref, dq_ref, dkt_ref, dvt_ref, qt_ref, dot_ref, plm_ref, pg_ref):
        i = pl.program_id(1)

        @pl.when(i == 0)
        def _():
            dkt_ref[...] = jnp.zeros_like(dkt_ref)
            dvt_ref[...] = jnp.zeros_like(dvt_ref)

        q = q_ref[...]
        do = do_ref[...]
        qt_ref[...] = q.astype(F32).T.astype(BF16)
        dot_ref[...] = do.astype(F32).T.astype(BF16)
        upto = _tri_and_ones(lambda r, c: r <= c)
        before = _tri_and_ones(lambda r, c: r < c)
        plm_ref[...] = jnp.zeros_like(plm_ref)
        pg_ref[...] = jnp.zeros_like(pg_ref)
        dq_ref[...] = jnp.zeros_like(dq_ref)

        def tile(j, mask):
            ks = pl.ds(pl.multiple_of(j * blk, blk), blk)
            k = k_ref[ks, :]
            v = v_ref[ks, :]
            z = _dot(q, k, NT)
            ls, lm, e = _log_sigmoids(z)
            if mask is not None:
                lm = jnp.where(mask, lm, 0.0)
            both = _split_dot(lm, upto)
            a = jnp.exp(ls + lt_ref[...] - (plm_ref[...] + both[:, :blk]))
            if mask is not None:
                a = jnp.where(mask, a, 0.0)
            g = a * _dot(do, v, NT)
            gboth = _split_dot(g, before)
            gsum = pg_ref[...] + gboth[:, :blk]
            r = 1.0 / (1.0 + e)
            pos = z >= 0.0
            sig = jnp.where(pos, r, e * r)
            nsig = jnp.where(pos, e * r, r)
            dz = g * nsig - gsum * sig
            if mask is not None:
                dz = jnp.where(mask, dz, 0.0)
            dzb = dz.astype(BF16)
            dkt_ref[j] += _dot(qt_ref[...], dzb)
            dvt_ref[j] += _dot(dot_ref[...], a.astype(BF16))
            dq_ref[...] += _dot(dzb, k)
            plm_ref[...] += both[:, blk:]
            pg_ref[...] += gboth[:, blk:]

        def step(j, _):
            tile(j, None)
            return 0

        lax.fori_loop(0, i * per, step, 0)
        for s in range(per):
            tile(i * per + s, _sb_mask(qb, s))

    qs = pl.BlockSpec((qb, HEAD_DIM), lambda h, i: (i, h))
    res = pl.BlockSpec((None, nkb, HEAD_DIM, blk), lambda h, i: (h, 0, 0, 0))
    return pl.pallas_call(
        body, name=name, grid=(H, T // qb),
        in_specs=[qs, pl.BlockSpec((T, HEAD_DIM), lambda h, i: (0, H + h)),
                  pl.BlockSpec((T, HEAD_DIM), lambda h, i: (0, 2 * H + h)), qs, qs],
        out_specs=[qs, res, res],
        out_shape=[jax.ShapeDtypeStruct((T, W), F32)] + [jax.ShapeDtypeStruct((H, nkb, HEAD_DIM, blk), F32)] * 2,
        scratch_shapes=[pltpu.VMEM((HEAD_DIM, qb), BF16), pltpu.VMEM((HEAD_DIM, qb), BF16),
                        pltpu.VMEM((qb, HEAD_DIM), F32), pltpu.VMEM((qb, HEAD_DIM), F32)],
        compiler_params=_params("parallel", "arbitrary"),
    )(qkv, qkv, qkv, do, ltot)


def _sb_pack(dq, dkt, dvt, name):
    T, W = dq.shape
    H = W // HEAD_DIM
    blk = ATT_BLOCK
    scale = HEAD_DIM ** -0.5

    def body(q_ref, kt_ref, vt_ref, o_ref):
        o_ref[:, pl.ds(0, W)] = (q_ref[...] * scale).astype(BF16)
        for h in range(H):
            o_ref[:, pl.ds(W + h * HEAD_DIM, HEAD_DIM)] = kt_ref[h].T.astype(BF16)
            o_ref[:, pl.ds(2 * W + h * HEAD_DIM, HEAD_DIM)] = vt_ref[h].T.astype(BF16)

    tr = pl.BlockSpec((H, None, HEAD_DIM, blk), lambda i: (0, i, 0, 0))
    return pl.pallas_call(
        body, name=name, grid=(T // blk,), in_specs=[pl.BlockSpec((blk, W), lambda i: (i, 0)), tr, tr],
        out_specs=pl.BlockSpec((blk, 3 * W), lambda i: (i, 0)),
        out_shape=jax.ShapeDtypeStruct((T, 3 * W), BF16), compiler_params=_params("parallel"),
    )(dq, dkt, dvt)


def _local_step(x, target, norms, sgu, comm):
    T, D = x.shape
    A = D // 2
    pre_mix, post_mix, pre_ffn, post_ffn = norms
    ln_g, ln_b, w_s, b_s = sgu
    b_t = b_s.T
    scale = HEAD_DIM ** -0.5

    def vec(p, layer):
        return comm.tie(p[layer:layer + 1])

    h0 = _prenorm(x, vec(pre_mix, 0), "prenorm0")
    z = _matmul(h0, comm.weight("ab_in"), mode="nn", name="ab_in_fwd")
    a_out = _sgu_fwd(z, ln_g, ln_b, w_s, b_t, "sgu_fwd")
    branch = [_dilated_fwd(z, d, A, f"dilated_fwd_{d}") for _, d in DILATED_PAIRS]
    b_out, o_dil, lse_dil = _dilated_merge([b[0] for b in branch], [b[1] for b in branch], "dilated_merge")
    cat = jnp.concatenate([a_out, b_out], axis=-1)
    y0 = _matmul(cat, comm.weight("ab_out"), mode="nn", name="ab_out_fwd")
    comm.arrive("ffn0", after=y0)
    x1, h1 = _postnorm_prenorm(x, y0, vec(post_mix, 0), vec(pre_ffn, 0), "norm_mix0")
    comm.land("ffn0", after=h1)

    def relu2(acc, j):
        r = jnp.maximum(acc, 0.0)
        return r * r

    f0 = _matmul(h1, comm.weight("w1_0"), mode="nn", name="ffn0_w1_fwd", out_dtype=BF16, epi=relu2)
    y1 = _matmul(f0, comm.weight("w2_0"), mode="nn", name="ffn0_w2_fwd")
    comm.arrive("rest", after=y1)
    x2, h2 = _postnorm_prenorm(x1, y1, vec(post_ffn, 0), vec(pre_mix, 1), "norm_ffn0")
    comm.land("rest", after=h2)

    tn_qkv = _tile(D, 512)
    nq = D // tn_qkv

    def scale_q(acc, j):
        return jnp.where(j < nq, acc * scale, acc)

    qkv = _matmul(h2, comm.weight("sb_in"), mode="nn", name="sb_in_fwd", out_dtype=BF16, tn=tn_qkv, epi=scale_q)
    o_sb, ltot = _sb_fwd(qkv, D, "sb_fwd")
    y2 = _matmul(o_sb, comm.weight("sb_out"), mode="nn", name="sb_out_fwd")
    x3, h3 = _postnorm_prenorm(x2, y2, vec(post_mix, 1), vec(pre_ffn, 1), "norm_mix1")
    f1 = _matmul(h3, comm.weight("w1_1"), mode="nn", name="ffn1_w1_fwd", out_dtype=BF16, epi=relu2)
    y3 = _matmul(f1, comm.weight("w2_1"), mode="nn", name="ffn1_w2_fwd")
    loss_tile, dx4 = _postnorm_loss(x3, y3, vec(post_ffn, 1), target, "norm_loss")
    loss = loss_tile[0, 0]

    def relu2_bwd(acc, j, f):
        return acc * (2.0 * jnp.sqrt(f.astype(F32)))

    def ffn_bwd(dx_out, x_in, h, f, y, layer, tag):
        dy, dg_post = _postnorm_bwd(dx_out, y, vec(post_ffn, layer), f"ffn{tag}_postnorm_bwd")
        g_w2 = _matmul(f, dy, mode="tn", name=f"ffn{tag}_w2_wgrad", tm=1024, tn=1024, tk=1024)
        da = _matmul(dy, comm.weight(f"w2_{layer}"), mode="nt", name=f"ffn{tag}_w2_dgrad", out_dtype=BF16,
                     epi=relu2_bwd, extras=(f,))
        g_w1 = _matmul(h, da, mode="tn", name=f"ffn{tag}_w1_wgrad", tm=1024, tn=1024, tk=1024)
        comm.reduce(f"ffn{tag}", {f"w2_{layer}": g_w2, f"w1_{layer}": g_w1})
        dh = _matmul(da, comm.weight(f"w1_{layer}"), mode="nt", name=f"ffn{tag}_w1_dgrad")
        dx_in, dg_pre = _prenorm_bwd(dx_out, dh, x_in, vec(pre_ffn, layer), f"ffn{tag}_prenorm_bwd")
        return dx_in, dg_pre, dg_post

    dx3, dg_pre_ffn1, dg_post_ffn1 = ffn_bwd(dx4, x3, h3, f1, y3, 1, "1")

    dy2, dg_post_mix1 = _postnorm_bwd(dx3, y2, vec(post_mix, 1), "sb_postnorm_bwd")
    g_sb_out = _matmul(o_sb, dy2, mode="tn", name="sb_out_wgrad", tm=1024, tn=1024, tk=1024)
    do_sb = _matmul(dy2, comm.weight("sb_out"), mode="nt", name="sb_out_dgrad", out_dtype=BF16)
    dq, dk, dv = _sb_bwd(qkv, do_sb, ltot, D, "sb_bwd")
    dqkv = _sb_pack(dq, dk, dv, "sb_pack")
    g_sb_in = _matmul(h2, dqkv, mode="tn", name="sb_in_wgrad", tm=1024, tn=1024, tk=1024)
    comm.reduce("sb", {"sb_out": g_sb_out, "sb_in": g_sb_in})
    dh2 = _matmul(dqkv, comm.weight("sb_in"), mode="nt", name="sb_in_dgrad")
    dx2, dg_pre_mix1 = _prenorm_bwd(dx3, dh2, x2, vec(pre_mix, 1), "sb_prenorm_bwd")

    dx1, dg_pre_ffn0, dg_post_ffn0 = ffn_bwd(dx2, x1, h1, f0, y1, 0, "0")

    dy0, dg_post_mix0 = _postnorm_bwd(dx1, y0, vec(post_mix, 0), "ab_postnorm_bwd")
    g_ab_out = _matmul(cat, dy0, mode="tn", name="ab_out_wgrad", tm=1024, tn=1024, tk=1024)
    dcat = _matmul(dy0, comm.weight("ab_out"), mode="nt", name="ab_out_dgrad")
    duv, d_ln_g, d_ln_b, d_w_s, d_b_t = _sgu_bwd(z, dcat, ln_g, ln_b, w_s, b_t, "sgu_bwd")
    delta = _dilated_delta(dcat, o_dil, "dilated_delta")
    dqs, dks, dvs = [], [], []
    for _, d in DILATED_PAIRS:
        dqs.append(_dilated_dq(z, dcat, lse_dil, delta, d, A, f"dilated_dq_{d}"))
        dk_b, dv_b = _dilated_dkv(z, dcat, lse_dil, delta, d, A, f"dilated_dkv_{d}")
        dks.append(dk_b)
        dvs.append(dv_b)
    dqkv0 = _dilated_combine(dqs, dks, dvs, "dilated_combine")
    dz = jnp.concatenate([duv, dqkv0], axis=-1)
    g_ab_in = _matmul(h0, dz, mode="tn", name="ab_in_wgrad", tm=1024, tn=1024, tk=1024)
    comm.reduce("ab", {"ab_out": g_ab_out, "ab_in": g_ab_in})
    dh0 = _matmul(dz, comm.weight("ab_in"), mode="nt", name="ab_in_dgrad")
    dx0, dg_pre_mix0 = _prenorm_bwd(dx1, dh0, x, vec(pre_mix, 0), "ab_prenorm_bwd")

    small = {
        "pre_mix": jnp.concatenate([dg_pre_mix0, dg_pre_mix1], axis=0),
        "post_mix": jnp.concatenate([dg_post_mix0, dg_post_mix1], axis=0),
        "pre_ffn": jnp.concatenate([dg_pre_ffn0, dg_pre_ffn1], axis=0),
        "post_ffn": jnp.concatenate([dg_post_ffn0, dg_post_ffn1], axis=0),
        "ln_g": d_ln_g, "ln_b": d_ln_b, "w_s": d_w_s, "b_s": d_b_t.T,
    }
    return loss, dx0, small


MESH_ID = pl.DeviceIdType.MESH
ANY = pl.BlockSpec(memory_space=pl.ANY)


def _coords():
    return lax.axis_index("x"), lax.axis_index("y"), lax.axis_index("c")


def _shard_of(ref, kind, p):
    if kind == "col":
        n = ref.shape[1] // N_DEV
        return ref.at[:, pl.ds(pl.multiple_of(p * n, 128), n)]
    r = ref.shape[0] // N_DEV
    return ref.at[pl.ds(pl.multiple_of(p * r, 16), r), :]


def _full_shape(shard, kind):
    if kind == "col":
        return (shard.shape[0], shard.shape[1] * N_DEV)
    return (shard.shape[0] * N_DEV, shard.shape[1])


def _all_gather(shards, kinds, n_whole):
    nt = len(shards)

    def body(*refs):
        ins, outs = refs[:nt], refs[nt:2 * nt]
        send_sems, recv_sems, local_sems = refs[2 * nt:]
        x, y, c = _coords()
        me, sibling = (x, y, c), (x, y, 1 - c)
        chips = [(1 - x, y), (x, 1 - y), (1 - x, 1 - y)]

        def slot(t, px, py, pc):
            return _shard_of(outs[t], kinds[t], 4 * px + 2 * py + pc)

        def copy(t, k, block, to, src=None):
            return pltpu.make_async_remote_copy(
                src_ref=slot(t, *block) if src is None else src, dst_ref=slot(t, *block),
                send_sem=send_sems.at[7 * t + k], recv_sem=recv_sems.at[7 * t + k],
                device_id=to, device_id_type=MESH_ID)

        local, sent = [], []
        for t in range(nt):
            local.append(pltpu.make_async_copy(ins[t], slot(t, *me), local_sems.at[t]))
            local[-1].start()
        for t in range(n_whole):
            first = [copy(t, 0, me, sibling, src=ins[t])]
            first += [copy(t, 1 + j, me, (*chip, c), src=ins[t]) for j, chip in enumerate(chips)]
            for cp in first:
                cp.start()
            sent += first
        for t in range(n_whole):
            for j, chip in enumerate(chips):
                copy(t, 1 + j, (*chip, c), me).wait_recv()
                sent.append(copy(t, 4 + j, (*chip, c), sibling))
                sent[-1].start()
        for t in range(n_whole):
            copy(t, 0, sibling, me).wait_recv()
            for j, chip in enumerate(chips):
                copy(t, 4 + j, (*chip, 1 - c), me).wait_recv()
        for cp in sent:
            cp.wait_send()
        for cp in local:
            cp.wait()

    return pl.pallas_call(
        body, name="all_gather_weights",
        in_specs=[ANY] * nt, out_specs=[ANY] * nt,
        out_shape=[jax.ShapeDtypeStruct(_full_shape(s, k), s.dtype) for s, k in zip(shards, kinds)],
        scratch_shapes=[pltpu.SemaphoreType.DMA((7 * n_whole,)), pltpu.SemaphoreType.DMA((7 * n_whole,)),
                        pltpu.SemaphoreType.DMA((nt,))],
    )(*shards)


HBM = pl.BlockSpec(memory_space=pltpu.HBM)
SEM = pl.BlockSpec(memory_space=pltpu.SEMAPHORE)
FLOWS = pltpu.SideEffectType.DATAFLOW_SIDE_EFFECTING


def _in_hbm(a):
    return pltpu.with_memory_space_constraint(a, pltpu.HBM)


def _hbm_like(bufs):
    return [pltpu.HBM(b.shape, b.dtype) for b in bufs]


def _copies_start(name, bufs, plan, n):
    nb = len(bufs)

    def body(*refs):
        send_sems, recv_sems, token = refs[nb], refs[nb + 1], refs[-1]
        for cp in plan(refs[:nb], send_sems, recv_sems):
            cp.start()
        token[...] = jnp.zeros_like(token)

    out = pl.pallas_call(
        body, name=name, in_specs=[HBM] * nb,
        out_specs=[SEM, SEM] + [HBM] * nb + [pl.BlockSpec(memory_space=pltpu.VMEM)],
        out_shape=[pltpu.SemaphoreType.DMA((n,)), pltpu.SemaphoreType.DMA((n,))] + _hbm_like(bufs)
        + [jax.ShapeDtypeStruct((8, 128), F32)],
        input_output_aliases={i: 2 + i for i in range(nb)},
        compiler_params=pltpu.CompilerParams(has_side_effects=FLOWS),
    )(*[_in_hbm(b) for b in bufs])
    return (out[0], out[1]), list(out[2:2 + nb]), out[-1]


def _copies_wait(name, bufs, sems, after, plan):
    nb = len(bufs)

    def body(*refs):
        for cp in plan(refs[:nb], refs[nb], refs[nb + 1]):
            cp.wait_send()
            cp.wait_recv()

    out = pl.pallas_call(
        body, name=name, in_specs=[HBM] * nb + [SEM, SEM, ANY], out_specs=[HBM] * nb,
        out_shape=_hbm_like(bufs), input_output_aliases={i: i for i in range(nb)},
        compiler_params=pltpu.CompilerParams(has_side_effects=FLOWS),
    )(*bufs, *sems, after)
    return list(out)


def _copies_wait_start(name, bufs, sems, after, plan, next_plan, n_next):
    nb = len(bufs)

    def body(*refs):
        ins = refs[:nb]
        for cp in plan(ins, refs[nb], refs[nb + 1]):
            cp.wait_send()
            cp.wait_recv()
        send_sems, recv_sems, token = refs[nb + 3], refs[nb + 4], refs[-1]
        for cp in next_plan(ins, send_sems, recv_sems):
            cp.start()
        token[...] = jnp.zeros_like(token)

    out = pl.pallas_call(
        body, name=name, in_specs=[HBM] * nb + [SEM, SEM, ANY],
        out_specs=[SEM, SEM] + [HBM] * nb + [pl.BlockSpec(memory_space=pltpu.VMEM)],
        out_shape=[pltpu.SemaphoreType.DMA((n_next,)), pltpu.SemaphoreType.DMA((n_next,))] + _hbm_like(bufs)
        + [jax.ShapeDtypeStruct((8, 128), F32)],
        input_output_aliases={i: 2 + i for i in range(nb)},
        compiler_params=pltpu.CompilerParams(has_side_effects=FLOWS),
    )(*bufs, *sems, after)
    return (out[0], out[1]), list(out[2:2 + nb]), out[-1]


def _gather_plans(kinds):
    nt = len(kinds)

    def slot(refs, t, px, py, pc):
        return _shard_of(refs[nt + t], kinds[t], 4 * px + 2 * py + pc)

    def to_chips(refs, send_sems, recv_sems):
        x, y, c = _coords()
        peers = [(x, y, 1 - c), (1 - x, y, c), (x, 1 - y, c), (1 - x, 1 - y, c)]
        return [pltpu.make_async_remote_copy(
            src_ref=refs[t], dst_ref=slot(refs, t, x, y, c), send_sem=send_sems.at[4 * t + k],
            recv_sem=recv_sems.at[4 * t + k], device_id=peer, device_id_type=MESH_ID)
            for t in range(nt) for k, peer in enumerate(peers)]

    def to_sibling(refs, send_sems, recv_sems):
        x, y, c = _coords()
        chips = [(1 - x, y), (x, 1 - y), (1 - x, 1 - y)]
        return [pltpu.make_async_remote_copy(
            src_ref=slot(refs, t, *chip, c), dst_ref=slot(refs, t, *chip, c), send_sem=send_sems.at[3 * t + j],
            recv_sem=recv_sems.at[3 * t + j], device_id=(x, y, 1 - c), device_id_type=MESH_ID)
            for t in range(nt) for j, chip in enumerate(chips)]

    return to_chips, to_sibling


def _shard_shape(full, kind):
    if kind == "col":
        return (full.shape[0], full.shape[1] // N_DEV)
    return (full.shape[0] // N_DEV, full.shape[1])


CHIPS = ((0, 0), (0, 1), (1, 0), (1, 1))


def _exchange_siblings(grads, kinds, name):
    nt = len(grads)

    def body(*refs):
        ins, outs = refs[:nt], refs[nt:2 * nt]
        send_sems, recv_sems = refs[2 * nt:]
        x, y, c = _coords()
        copies = []
        for t in range(nt):
            for q, (qx, qy) in enumerate(CHIPS):
                copies.append(pltpu.make_async_remote_copy(
                    src_ref=_shard_of(ins[t], kinds[t], 4 * qx + 2 * qy + (1 - c)), dst_ref=outs[t].at[q],
                    send_sem=send_sems.at[4 * t + q], recv_sem=recv_sems.at[4 * t + q],
                    device_id=(x, y, 1 - c), device_id_type=MESH_ID))
        for cp in copies:
            cp.start()
        for cp in copies:
            cp.wait()

    return pl.pallas_call(
        body, name=name,
        in_specs=[ANY] * nt, out_specs=[ANY] * nt,
        out_shape=[jax.ShapeDtypeStruct((4,) + _shard_shape(g, k), F32) for g, k in zip(grads, kinds)],
        scratch_shapes=[pltpu.SemaphoreType.DMA((4 * nt,)), pltpu.SemaphoreType.DMA((4 * nt,))],
    )(*grads)


def _pair_sum(grad, other, kind, where, name):
    rows, cols = _shard_shape(grad, kind)
    tr = _tile(rows, 256)

    def body(w_ref, g_ref, o_ref, s_ref, land_ref):
        s = (g_ref[...] + o_ref[...]).astype(BF16)
        s_ref[...] = s

        @pl.when(pl.program_id(1) == w_ref[1])
        def _():
            land_ref[...] = s

    if kind == "col":
        g_spec = pl.BlockSpec((tr, cols), lambda i, q, w_ref: (i, 2 * q + w_ref[0]))
    else:
        g_spec = pl.BlockSpec((tr, cols), lambda i, q, w_ref: ((2 * q + w_ref[0]) * (rows // tr) + i, 0))
    part = pl.BlockSpec((None, tr, cols), lambda i, q, w_ref: (q, i, 0))
    mine = pl.BlockSpec((None, tr, cols), lambda i, q, w_ref: (w_ref[1], i, 0))
    return pl.pallas_call(
        body, name=name,
        grid_spec=pltpu.PrefetchScalarGridSpec(
            num_scalar_prefetch=1, grid=(rows // tr, 4), in_specs=[g_spec, part], out_specs=[part, mine]),
        out_shape=[jax.ShapeDtypeStruct((4, rows, cols), BF16)] * 2,
        compiler_params=_params("parallel", "arbitrary"),
    )(where, grad, other)


def _scatter_plan(nt):
    def plan(refs, send_sems, recv_sems):
        x, y, c = _coords()
        chips = [(1 - x, y), (x, 1 - y), (1 - x, 1 - y)]
        return [pltpu.make_async_remote_copy(
            src_ref=refs[t].at[2 * qx + qy], dst_ref=refs[nt + t].at[2 * x + y], send_sem=send_sems.at[3 * t + j],
            recv_sem=recv_sems.at[3 * t + j], device_id=(qx, qy, c), device_id_type=MESH_ID)
            for t in range(nt) for j, (qx, qy) in enumerate(chips)]
    return plan


def _all_reduce_small(vec):
    R = vec.shape[0]

    def body(v_ref, o_ref, recv_ref, send_sems, recv_sems):
        x, y, c = _coords()
        me = 4 * x + 2 * y + c
        recv_ref[me] = v_ref[...]
        copies = []
        for k in range(1, N_DEV):
            bx, by, bc = (k >> 2) & 1, (k >> 1) & 1, k & 1
            peer = (1 - x if bx else x, 1 - y if by else y, 1 - c if bc else c)
            copies.append(pltpu.make_async_remote_copy(
                src_ref=v_ref, dst_ref=recv_ref.at[me],
                send_sem=send_sems.at[k - 1], recv_sem=recv_sems.at[k - 1],
                device_id=peer, device_id_type=MESH_ID))
        for cp in copies:
            cp.start()
        for cp in copies:
            cp.wait()
        total = recv_ref[0]
        for p in range(1, N_DEV):
            total = total + recv_ref[p]
        o_ref[...] = total

    return pl.pallas_call(
        body, name="all_reduce_small",
        in_specs=[pl.BlockSpec(memory_space=pltpu.VMEM)], out_specs=pl.BlockSpec(memory_space=pltpu.VMEM),
        out_shape=jax.ShapeDtypeStruct((R, 128), F32),
        scratch_shapes=[pltpu.VMEM((N_DEV, R, 128), F32), pltpu.SemaphoreType.DMA((N_DEV - 1,)),
                        pltpu.SemaphoreType.DMA((N_DEV - 1,))],
        compiler_params=pltpu.CompilerParams(vmem_limit_bytes=VMEM_LIMIT),
    )(vec)


def _adamw_math(w, g, m, v):
    m = ADAM_B1 * m + (1.0 - ADAM_B1) * g
    v = ADAM_B2 * v + (1.0 - ADAM_B2) * (g * g)
    m_hat = m / (1.0 - ADAM_B1 ** ADAM_STEP)
    v_hat = v / (1.0 - ADAM_B2 ** ADAM_STEP)
    delta = -ADAM_LR * (m_hat / (jnp.sqrt(v_hat) + ADAM_EPS) + ADAM_WD * w)
    return delta, m, v


def _adamw(w, parts, m, v, name):
    layers, rows, cols = w.shape
    tr = rows if rows * cols <= 256 * 1024 else _tile(rows, 128)
    out = None
    for layer in range(layers):
        n = parts[layer].shape[0]

        def body(w_ref, p_ref, m_ref, v_ref, *rest, n=n):
            g_ref, d_ref, mo_ref, vo_ref = rest[-4:]
            g = p_ref[0].astype(F32)
            for q in range(1, n):
                g = g + p_ref[q].astype(F32)
            g_ref[...] = g
            d_ref[...], mo_ref[...], vo_ref[...] = _adamw_math(w_ref[...], g, m_ref[...], v_ref[...])

        blk = pl.BlockSpec((None, tr, cols), lambda i, layer=layer: (layer, i, 0))
        earlier = [] if out is None else list(out)
        out = pl.pallas_call(
            body, name=f"{name}_{layer}", grid=(rows // tr,),
            in_specs=[blk, pl.BlockSpec((n, tr, cols), lambda i: (0, i, 0)), blk, blk] + [ANY] * len(earlier),
            out_specs=[blk] * 4, out_shape=[jax.ShapeDtypeStruct((layers, rows, cols), F32)] * 4,
            input_output_aliases={4 + k: k for k in range(len(earlier))},
            compiler_params=_params("parallel"),
        )(w, parts[layer], m, v, *earlier)
    return out


def _pack(arrays):
    rows = []
    for a in arrays:
        flat = a.reshape(-1)
        pad = (-flat.shape[0]) % 1024
        rows.append(jnp.pad(flat, (0, pad)).reshape(-1, 128))
    return jnp.concatenate(rows, axis=0)


def _unpack(packed, like):
    out, r = [], 0
    for a in like:
        n = math.prod(a.shape)
        nr = (n + 1023) // 1024 * 8
        out.append(packed[r:r + nr].reshape(-1)[:n].reshape(a.shape))
        r += nr
    return out


KIND = {"ab_in": "col", "ab_out": "row", "sb_in": "col", "sb_out": "row",
        "w1_0": "col", "w1_1": "col", "w2_0": "row", "w2_1": "row"}
GATHER_FIRST = ("ab_in", "ab_out")
GATHER_LATER = {"ffn0": ("w1_0", "w2_0"),
                "rest": ("sb_in", "sb_out", "w1_1", "w2_1")}


class _Exchange:
    def __init__(self, shards):
        names = list(GATHER_FIRST) + [n for group in GATHER_LATER.values() for n in group]
        full = _all_gather([shards[n] for n in names], [KIND[n] for n in names], len(GATHER_FIRST))
        self.full = dict(zip(names, full))
        x, y, c = _coords()
        self.where = jnp.stack([c, 2 * x + y]).astype(jnp.int32)
        self.tokens = []
        self.gathers = {}
        self.scatters = {}
        self.landed = {}
        for key, group in GATHER_LATER.items():
            to_chips, to_sibling = _gather_plans([KIND[n] for n in group])
            bufs = [shards[n] for n in group] + [self.full[n] for n in group]
            sems, bufs, token = _copies_start(f"gather_start_{key}", bufs, to_chips, 4 * len(group))
            self.tokens.append(token)
            self.gathers[key] = (group, sems, bufs, to_chips, to_sibling)

    def tie(self, small):
        for token in self.tokens:
            small = small + token[0:1, 0:1]
        self.tokens = []
        return small

    def weight(self, name):
        return self.full[name]

    def arrive(self, key, after):
        group, sems, bufs, to_chips, to_sibling = self.gathers[key]
        sems, bufs, token = _copies_wait_start(f"gather_pass_{key}", bufs, sems, after, to_chips, to_sibling,
                                               3 * len(group))
        self.tokens.append(token)
        self.gathers[key] = (group, sems, bufs, to_chips, to_sibling)

    def land(self, key, after):
        group, sems, bufs, _, to_sibling = self.gathers.pop(key)
        bufs = _copies_wait(f"gather_done_{key}", bufs, sems, after, to_sibling)
        for n, b in zip(group, bufs[len(group):]):
            self.full[n] = b

    def reduce(self, key, grads):
        names = list(grads)
        kinds = [KIND[n] for n in names]
        others = _exchange_siblings([grads[n] for n in names], kinds, f"grad_siblings_{key}")
        pairs = [_pair_sum(grads[n], o, k, self.where, f"pair_sum_{n}") for n, o, k in zip(names, others, kinds)]
        parts, lands = [p[0] for p in pairs], [p[1] for p in pairs]
        plan = _scatter_plan(len(names))
        sems, bufs, token = _copies_start(f"scatter_start_{key}", parts + lands, plan, 3 * len(names))
        self.tokens.append(token)
        self.scatters[key] = (names, sems, bufs, plan)

    def finish(self, after):
        for key, (names, sems, bufs, plan) in self.scatters.items():
            bufs = _copies_wait(f"scatter_done_{key}", bufs, sems, after, plan)
            self.landed.update(zip(names, bufs[len(names):]))
        self.scatters = {}
        return self.landed


SMALL = ("norm_pre_mix", "norm_post_mix", "norm_pre_ffn", "norm_post_ffn", "sgu_ln_g", "sgu_ln_b", "sgu_w", "sgu_b")
ORDER = ("norm_pre_mix", "norm_post_mix", "norm_pre_ffn", "norm_post_ffn", "ab_w_in", "sgu_ln_g", "sgu_ln_b", "sgu_w",
         "sgu_b", "ab_w_out", "sb_w_in", "sb_w_out", "ffn_w1", "ffn_w2")


def kernel(x, norm_pre_mix, norm_post_mix, norm_pre_ffn, norm_post_ffn, ab_w_in, sgu_ln_g, sgu_ln_b, sgu_w, sgu_b, ab_w_out, sb_w_in, sb_w_out, ffn_w1, ffn_w2, loss_target, m_norm_pre_mix, m_norm_post_mix, m_norm_pre_ffn, m_norm_post_ffn, m_ab_w_in, m_sgu_ln_g, m_sgu_ln_b, m_sgu_w, m_sgu_b, m_ab_w_out, m_sb_w_in, m_sb_w_out, m_ffn_w1, m_ffn_w2, v_norm_pre_mix, v_norm_post_mix, v_norm_pre_ffn, v_norm_post_ffn, v_ab_w_in, v_sgu_ln_g, v_sgu_ln_b, v_sgu_w, v_sgu_b, v_ab_w_out, v_sb_w_in, v_sb_w_out, v_ffn_w1, v_ffn_w2):
    W = dict(norm_pre_mix=norm_pre_mix, norm_post_mix=norm_post_mix, norm_pre_ffn=norm_pre_ffn,
             norm_post_ffn=norm_post_ffn, ab_w_in=ab_w_in, sgu_ln_g=sgu_ln_g, sgu_ln_b=sgu_ln_b, sgu_w=sgu_w,
             sgu_b=sgu_b, ab_w_out=ab_w_out, sb_w_in=sb_w_in, sb_w_out=sb_w_out, ffn_w1=ffn_w1, ffn_w2=ffn_w2)
    M = dict(norm_pre_mix=m_norm_pre_mix, norm_post_mix=m_norm_post_mix, norm_pre_ffn=m_norm_pre_ffn,
             norm_post_ffn=m_norm_post_ffn, ab_w_in=m_ab_w_in, sgu_ln_g=m_sgu_ln_g, sgu_ln_b=m_sgu_ln_b,
             sgu_w=m_sgu_w, sgu_b=m_sgu_b, ab_w_out=m_ab_w_out, sb_w_in=m_sb_w_in, sb_w_out=m_sb_w_out,
             ffn_w1=m_ffn_w1, ffn_w2=m_ffn_w2)
    V = dict(norm_pre_mix=v_norm_pre_mix, norm_post_mix=v_norm_post_mix, norm_pre_ffn=v_norm_pre_ffn,
             norm_post_ffn=v_norm_post_ffn, ab_w_in=v_ab_w_in, sgu_ln_g=v_sgu_ln_g, sgu_ln_b=v_sgu_ln_b,
             sgu_w=v_sgu_w, sgu_b=v_sgu_b, ab_w_out=v_ab_w_out, sb_w_in=v_sb_w_in, sb_w_out=v_sb_w_out,
             ffn_w1=v_ffn_w1, ffn_w2=v_ffn_w2)

    shards = {"ab_in": ab_w_in[0], "ab_out": ab_w_out[0], "w1_0": ffn_w1[0], "w2_0": ffn_w2[0],
              "sb_in": sb_w_in[0], "sb_out": sb_w_out[0], "w1_1": ffn_w1[1], "w2_1": ffn_w2[1]}
    comm = _Exchange({n: s.astype(BF16) for n, s in shards.items()})
    norms = (norm_pre_mix, norm_post_mix, norm_pre_ffn, norm_post_ffn)
    sgu = (sgu_ln_g, sgu_ln_b, sgu_w[0], sgu_b[0])
    loss, dx, small = _local_step(x[0], loss_target[0], norms, sgu, comm)
    loss = lax.psum(loss, MESH_AXES)
    landed = comm.finish(after=dx)

    out = {}
    for name, layers in (("ab_w_in", ["ab_in"]), ("ab_w_out", ["ab_out"]), ("sb_w_in", ["sb_in"]),
                         ("sb_w_out", ["sb_out"]), ("ffn_w1", ["w1_0", "w1_1"]), ("ffn_w2", ["w2_0", "w2_1"])):
        out[name] = _adamw(W[name], [landed[n] for n in layers], M[name], V[name], f"adamw_{name}")

    small_g = [small["pre_mix"], small["post_mix"], small["pre_ffn"], small["post_ffn"], small["ln_g"],
               small["ln_b"], small["w_s"][None], small["b_s"][None]]
    g_small = _all_reduce_small(_pack(small_g))
    res = _adamw(_pack([W[n] for n in SMALL])[None], [g_small[None]], _pack([M[n] for n in SMALL])[None],
                 _pack([V[n] for n in SMALL])[None], "adamw_small")
    like = [W[n] for n in SMALL]
    for n, *vals in zip(SMALL, *[_unpack(r[0], like) for r in res]):
        out[n] = vals

    return (loss, dx[None], *[out[n][0] for n in ORDER], *[out[n][1] for n in ORDER],
            *[out[n][2] for n in ORDER], *[out[n][3] for n in ORDER])
```

```python
import functools
import math

import jax
import jax.numpy as jnp
from jax import lax
from jax.experimental import pallas as pl
from jax.experimental.pallas import tpu as pltpu

F32 = jnp.float32
BF16 = jnp.bfloat16

HEAD_DIM = 128
CHUNK = 128
ATT_BLOCK = 128
DILATED_PAIRS = ((128, 1), (512, 4), (2048, 16))
RMS_EPS = 1e-6
LN_EPS = 1e-5
ADAM_LR = 0.001
ADAM_B1 = 0.9
ADAM_B2 = 0.999
ADAM_EPS = 1e-08
ADAM_WD = 0.01
ADAM_STEP = 10
N_DEV = 8
MESH_AXES = ("x", "y", "c")
MASKED = -1e30

V7X_VMEM_BYTES = 64 * 1024 * 1024
VMEM_LIMIT = V7X_VMEM_BYTES - 8 * 1024 * 1024

NN = (((1,), (0,)), ((), ()))
NT = (((1,), (1,)), ((), ()))
TN = (((0,), (0,)), ((), ()))


def _params(*sem):
    return pltpu.CompilerParams(dimension_semantics=sem, vmem_limit_bytes=VMEM_LIMIT)


def _dot(a, b, dims=NN):
    return lax.dot_general(a, b, dims, preferred_element_type=F32)


def _tile(n, preferred):
    if n <= preferred:
        return n
    t = preferred - preferred % 128
    while n % t:
        t -= 128
    assert t > 0, (n, preferred)
    return t


def _matmul(a, b, *, mode, name, out_dtype=F32, tm=1024, tn=512, tk=2048, epi=None, extras=(), after=()):
    if mode == "nn":
        (M, K), N = a.shape, b.shape[1]
    elif mode == "nt":
        (M, K), N = a.shape, b.shape[0]
    else:
        (K, M), N = a.shape, b.shape[1]
    tm, tn, tk = _tile(M, tm), _tile(N, tn), _tile(K, tk)
    nk = K // tk
    if mode == "tn":
        a_spec = pl.BlockSpec((tk, tm), lambda i, j, k: (k, i))
    else:
        a_spec = pl.BlockSpec((tm, tk), lambda i, j, k: (i, k))
    if mode == "nt":
        b_spec = pl.BlockSpec((tn, tk), lambda i, j, k: (j, k))
    else:
        b_spec = pl.BlockSpec((tk, tn), lambda i, j, k: (k, j))
    o_spec = pl.BlockSpec((tm, tn), lambda i, j, k: (i, j))
    dims = {"nn": NN, "nt": NT, "tn": TN}[mode]
    n_extra = len(extras)
    n_in = n_extra + len(after)

    def finish(acc, refs):
        j = pl.program_id(1)
        if epi is None:
            return acc
        return epi(acc, j, *[r[...] for r in refs])

    if nk == 1:
        def body(a_ref, b_ref, *rest):
            o_ref = rest[n_in]
            acc = _dot(a_ref[...], b_ref[...], dims)
            o_ref[...] = finish(acc, rest[:n_extra]).astype(o_ref.dtype)
        scratch = []
    else:
        def body(a_ref, b_ref, *rest):
            o_ref, acc_ref = rest[n_in], rest[n_in + 1]
            k = pl.program_id(2)

            @pl.when(k == 0)
            def _():
                acc_ref[...] = jnp.zeros_like(acc_ref)

            acc_ref[...] += _dot(a_ref[...], b_ref[...], dims)

            @pl.when(k == nk - 1)
            def _():
                o_ref[...] = finish(acc_ref[...], rest[:n_extra]).astype(o_ref.dtype)
        scratch = [pltpu.VMEM((tm, tn), F32)]

    return pl.pallas_call(
        body,
        name=name,
        grid=(M // tm, N // tn, nk),
        in_specs=[a_spec, b_spec] + [o_spec] * n_extra + [ANY] * len(after),
        out_specs=o_spec,
        out_shape=jax.ShapeDtypeStruct((M, N), out_dtype),
        scratch_shapes=scratch,
        compiler_params=_params("parallel", "parallel", "arbitrary"),
    )(a, b, *extras, *after)


ROWS = 256


def _rms(x):
    return lax.rsqrt(jnp.mean(x * x, axis=-1, keepdims=True) + RMS_EPS)


def _prenorm(x, g, name):
    T, D = x.shape

    def body(x_ref, g_ref, h_ref):
        xv = x_ref[...]
        h_ref[...] = (xv * _rms(xv) * g_ref[...]).astype(BF16)

    row = pl.BlockSpec((ROWS, D), lambda i: (i, 0))
    vec = pl.BlockSpec((1, D), lambda i: (0, 0))
    return pl.pallas_call(
        body, name=name, grid=(T // ROWS,), in_specs=[row, vec], out_specs=row,
        out_shape=jax.ShapeDtypeStruct((T, D), BF16), compiler_params=_params("parallel"),
    )(x, g)


def _postnorm_prenorm(x, y, g_post, g_pre, name):
    T, D = x.shape

    def body(x_ref, y_ref, gp_ref, gn_ref, xo_ref, h_ref):
        yv = y_ref[...]
        xn = x_ref[...] + yv * _rms(yv) * gp_ref[...]
        xo_ref[...] = xn
        h_ref[...] = (xn * _rms(xn) * gn_ref[...]).astype(BF16)

    row = pl.BlockSpec((ROWS, D), lambda i: (i, 0))
    vec = pl.BlockSpec((1, D), lambda i: (0, 0))
    return pl.pallas_call(
        body, name=name, grid=(T // ROWS,), in_specs=[row, row, vec, vec], out_specs=[row, row],
        out_shape=[jax.ShapeDtypeStruct((T, D), F32), jax.ShapeDtypeStruct((T, D), BF16)],
        compiler_params=_params("parallel"),
    )(x, y, g_post, g_pre)


def _postnorm_loss(x, y, g_post, target, name):
    T, D = x.shape

    def body(x_ref, y_ref, gp_ref, t_ref, loss_ref, dx_ref):
        @pl.when(pl.program_id(0) == 0)
        def _():
            loss_ref[...] = jnp.zeros_like(loss_ref)

        yv = y_ref[...]
        err = x_ref[...] + yv * _rms(yv) * gp_ref[...] - t_ref[...]
        dx_ref[...] = err * (1.0 / D)
        loss_ref[...] += 0.5 * jnp.sum(jnp.sum(err * err, axis=-1, keepdims=True) * (1.0 / D))

    row = pl.BlockSpec((ROWS, D), lambda i: (i, 0))
    vec = pl.BlockSpec((1, D), lambda i: (0, 0))
    acc = pl.BlockSpec((8, 128), lambda i: (0, 0))
    return pl.pallas_call(
        body, name=name, grid=(T // ROWS,), in_specs=[row, row, vec, row], out_specs=[acc, row],
        out_shape=[jax.ShapeDtypeStruct((8, 128), F32), jax.ShapeDtypeStruct((T, D), F32)],
        compiler_params=_params("arbitrary"),
    )(x, y, g_post, target)


def _postnorm_bwd(dx, y, g_post, name):
    T, D = dx.shape

    def body(dx_ref, y_ref, g_ref, dy_ref, dg_ref):
        @pl.when(pl.program_id(0) == 0)
        def _():
            dg_ref[...] = jnp.zeros_like(dg_ref)

        yv, dn = y_ref[...], dx_ref[...]
        r = _rms(yv)
        yh = yv * r
        gd = dn * g_ref[...]
        dy_ref[...] = (r * (gd - yh * jnp.mean(yh * gd, axis=-1, keepdims=True))).astype(BF16)
        dg_ref[...] += jnp.sum(dn * yh, axis=0, keepdims=True)

    row = pl.BlockSpec((ROWS, D), lambda i: (i, 0))
    vec = pl.BlockSpec((1, D), lambda i: (0, 0))
    return pl.pallas_call(
        body, name=name, grid=(T // ROWS,), in_specs=[row, row, vec], out_specs=[row, vec],
        out_shape=[jax.ShapeDtypeStruct((T, D), BF16), jax.ShapeDtypeStruct((1, D), F32)],
        compiler_params=_params("arbitrary"),
    )(dx, y, g_post)


def _prenorm_bwd(dx_out, dh, x, g_pre, name):
    T, D = x.shape

    def body(dxo_ref, dh_ref, x_ref, g_ref, dx_ref, dg_ref):
        @pl.when(pl.program_id(0) == 0)
        def _():
            dg_ref[...] = jnp.zeros_like(dg_ref)

        xv, dhv = x_ref[...], dh_ref[...]
        r = _rms(xv)
        xh = xv * r
        gd = dhv * g_ref[...]
        dx_ref[...] = dxo_ref[...] + r * (gd - xh * jnp.mean(xh * gd, axis=-1, keepdims=True))
        dg_ref[...] += jnp.sum(dhv * xh, axis=0, keepdims=True)

    row = pl.BlockSpec((ROWS, D), lambda i: (i, 0))
    vec = pl.BlockSpec((1, D), lambda i: (0, 0))
    return pl.pallas_call(
        body, name=name, grid=(T // ROWS,), in_specs=[row, row, row, vec], out_specs=[row, vec],
        out_shape=[jax.ShapeDtypeStruct((T, D), F32), jax.ShapeDtypeStruct((1, D), F32)],
        compiler_params=_params("arbitrary"),
    )(dx_out, dh, x, g_pre)


_INV_SQRT2 = 1.0 / math.sqrt(2.0)
_INV_SQRT2PI = 1.0 / math.sqrt(2.0 * math.pi)


def _gelu(x):
    return 0.5 * x * (1.0 + lax.erf(x * _INV_SQRT2))


def _gelu_grad(x):
    return 0.5 * (1.0 + lax.erf(x * _INV_SQRT2)) + x * jnp.exp(-0.5 * x * x) * _INV_SQRT2PI


def _layernorm_stats(x):
    mu = jnp.mean(x, axis=-1, keepdims=True)
    xc = x - mu
    rstd = lax.rsqrt(jnp.mean(xc * xc, axis=-1, keepdims=True) + LN_EPS)
    return xc * rstd, rstd


def _tril_mask():
    i = lax.broadcasted_iota(jnp.int32, (CHUNK, CHUNK), 0)
    j = lax.broadcasted_iota(jnp.int32, (CHUNK, CHUNK), 1)
    return j <= i


SGU_ROWS = 512


def _sgu_fwd(z, ln_g, ln_b, w_s, b_t, name):
    T = z.shape[0]
    A = ln_g.shape[1]
    G = A // 128
    rows = min(SGU_ROWS, T)

    def body(u_ref, v_ref, g_ref, b_ref, w_ref, bt_ref, o_ref):
        mask = _tril_mask()
        for c in range(rows // CHUNK):
            rs = pl.ds(c * CHUNK, CHUNK)
            xh, _ = _layernorm_stats(_gelu(v_ref[rs, :]))
            vn = (xh * g_ref[...] + b_ref[...]).astype(BF16)
            for g in range(G):
                cs = pl.ds(g * 128, 128)
                w = jnp.where(mask, w_ref[g], 0.0).astype(BF16)
                mixed = _dot(w, vn[:, g * 128:(g + 1) * 128]) + bt_ref[:, g:g + 1]
                o_ref[rs, cs] = (_gelu(u_ref[rs, cs]) * mixed).astype(BF16)

    return pl.pallas_call(
        body, name=name, grid=(T // rows,),
        in_specs=[
            pl.BlockSpec((rows, A), lambda i: (i, 0)),
            pl.BlockSpec((rows, A), lambda i: (i, 1)),
            pl.BlockSpec((1, A), lambda i: (0, 0)),
            pl.BlockSpec((1, A), lambda i: (0, 0)),
            pl.BlockSpec((G, CHUNK, CHUNK), lambda i: (0, 0, 0)),
            pl.BlockSpec((CHUNK, G), lambda i: (0, 0)),
        ],
        out_specs=pl.BlockSpec((rows, A), lambda i: (i, 0)),
        out_shape=jax.ShapeDtypeStruct((T, A), BF16),
        compiler_params=_params("parallel"),
    )(z, z, ln_g, ln_b, w_s, b_t)


def _sgu_bwd(z, dcat, ln_g, ln_b, w_s, b_t, name):
    T = z.shape[0]
    A = ln_g.shape[1]
    G = A // 128
    rows = min(SGU_ROWS, T)

    def body(u_ref, v_ref, da_ref, g_ref, b_ref, w_ref, bt_ref, dz_ref, dg_ref, db_ref, dw_ref, dbt_ref, dvn_ref):
        @pl.when(pl.program_id(0) == 0)
        def _():
            dg_ref[...] = jnp.zeros_like(dg_ref)
            db_ref[...] = jnp.zeros_like(db_ref)
            dw_ref[...] = jnp.zeros_like(dw_ref)
            dbt_ref[...] = jnp.zeros_like(dbt_ref)

        mask = _tril_mask()
        for c in range(rows // CHUNK):
            rs = pl.ds(c * CHUNK, CHUNK)
            vv = v_ref[rs, :]
            gv = _gelu(vv)
            xh, rstd = _layernorm_stats(gv)
            vn = (xh * g_ref[...] + b_ref[...]).astype(BF16)
            for g in range(G):
                cs = pl.ds(g * 128, 128)
                w = jnp.where(mask, w_ref[g], 0.0).astype(BF16)
                vg = vn[:, g * 128:(g + 1) * 128]
                mixed = _dot(w, vg) + bt_ref[:, g:g + 1]
                uu = u_ref[rs, cs]
                da = da_ref[rs, cs]
                dz_ref[rs, cs] = (da * mixed * _gelu_grad(uu)).astype(BF16)
                dm = da * _gelu(uu)
                dmb = dm.astype(BF16)
                dbt_ref[:, g:g + 1] += jnp.sum(dm, axis=1, keepdims=True)
                dw_ref[g] += jnp.where(mask, _dot(dmb, vg, NT), 0.0)
                dvn_ref[:, cs] = _dot(w, dmb, TN)
            dvn = dvn_ref[...]
            dg_ref[...] += jnp.sum(dvn * xh, axis=0, keepdims=True)
            db_ref[...] += jnp.sum(dvn, axis=0, keepdims=True)
            dxh = dvn * g_ref[...]
            dgv = rstd * (dxh - jnp.mean(dxh, axis=-1, keepdims=True)
                          - xh * jnp.mean(dxh * xh, axis=-1, keepdims=True))
            dz_ref[rs, pl.ds(A, A)] = (dgv * _gelu_grad(vv)).astype(BF16)

    vec = pl.BlockSpec((1, A), lambda i: (0, 0))
    wsp = pl.BlockSpec((G, CHUNK, CHUNK), lambda i: (0, 0, 0))
    bsp = pl.BlockSpec((CHUNK, G), lambda i: (0, 0))
    return pl.pallas_call(
        body, name=name, grid=(T // rows,),
        in_specs=[
            pl.BlockSpec((rows, A), lambda i: (i, 0)),
            pl.BlockSpec((rows, A), lambda i: (i, 1)),
            pl.BlockSpec((rows, A), lambda i: (i, 0)),
            vec, vec, wsp, bsp,
        ],
        out_specs=[pl.BlockSpec((rows, 2 * A), lambda i: (i, 0)), vec, vec, wsp, bsp],
        out_shape=[
            jax.ShapeDtypeStruct((T, 2 * A), BF16),
            jax.ShapeDtypeStruct((1, A), F32),
            jax.ShapeDtypeStruct((1, A), F32),
            jax.ShapeDtypeStruct((G, CHUNK, CHUNK), F32),
            jax.ShapeDtypeStruct((CHUNK, G), F32),
        ],
        scratch_shapes=[pltpu.VMEM((CHUNK, A), F32)],
        compiler_params=_params("arbitrary"),
    )(z, z, dcat, ln_g, ln_b, w_s, b_t)


def _alibi_slope(h, n_heads):
    return 2.0 ** (-8.0 * (h + 1.0) / n_heads)


def _dil_scores(q, k, slope_d, valid, dist):
    s = _dot(q, k, NT) - slope_d * dist
    return jnp.where(valid, s, MASKED)


def _dilated_fwd(z, d, B, name):
    T = z.shape[0]
    H = B // HEAD_DIM
    nb = T // (d * ATT_BLOCK)
    zv = z.reshape(T // d, d * 5 * B)
    scale = HEAD_DIM ** -0.5
    blk = ATT_BLOCK

    def body(q_ref, kp_ref, kc_ref, vp_ref, vc_ref, o_ref, l_ref):
        n = pl.program_id(1)
        qi = lax.broadcasted_iota(jnp.int32, (blk, 2 * blk), 0)
        kj = lax.broadcasted_iota(jnp.int32, (blk, 2 * blk), 1)
        dist = qi + blk - kj
        valid = (dist >= 0) & (dist <= blk) & ((kj >= blk) | (n > 0))
        distf = dist.astype(F32)
        for h in range(H):
            cs = pl.ds(h * HEAD_DIM, HEAD_DIM)
            q = (q_ref[:, cs] * scale).astype(BF16)
            k = jnp.concatenate([kp_ref[:, cs], kc_ref[:, cs]], axis=0).astype(BF16)
            v = jnp.concatenate([vp_ref[:, cs], vc_ref[:, cs]], axis=0).astype(BF16)
            s = _dil_scores(q, k, _alibi_slope(h, H) * d, valid, distf)
            m = jnp.max(s, axis=-1, keepdims=True)
            p = jnp.exp(s - m)
            den = jnp.sum(p, axis=-1, keepdims=True)
            o_ref[:, cs] = _dot(p.astype(BF16), v) / den
            l_ref[:, cs] = jnp.broadcast_to(m + jnp.log(den), (blk, HEAD_DIM))

    def col(unit):
        return lambda r, n: (n, r * 5 + unit)

    def col_prev(unit):
        return lambda r, n: (jnp.maximum(n - 1, 0), r * 5 + unit)

    bs = (blk, B)
    out = pl.BlockSpec(bs, lambda r, n: (n, r))
    o, lse = pl.pallas_call(
        body, name=name, grid=(d, nb),
        in_specs=[pl.BlockSpec(bs, col(2)), pl.BlockSpec(bs, col_prev(3)), pl.BlockSpec(bs, col(3)),
                  pl.BlockSpec(bs, col_prev(4)), pl.BlockSpec(bs, col(4))],
        out_specs=[out, out],
        out_shape=[jax.ShapeDtypeStruct((T // d, d * B), F32)] * 2,
        compiler_params=_params("parallel", "parallel"),
    )(zv, zv, zv, zv, zv)
    return o.reshape(T, B), lse.reshape(T, B)


def _dilated_merge(os_, ls_, name):
    T, B = os_[0].shape

    def body(o1, o2, o3, l1, l2, l3, ob_ref, of_ref, lt_ref):
        a, b, c = l1[...], l2[...], l3[...]
        m = jnp.maximum(jnp.maximum(a, b), c)
        ea, eb, ec = jnp.exp(a - m), jnp.exp(b - m), jnp.exp(c - m)
        tot = ea + eb + ec
        o = (ea * o1[...] + eb * o2[...] + ec * o3[...]) / tot
        of_ref[...] = o
        ob_ref[...] = o.astype(BF16)
        lt_ref[...] = m + jnp.log(tot)

    row = pl.BlockSpec((ROWS, B), lambda i: (i, 0))
    return pl.pallas_call(
        body, name=name, grid=(T // ROWS,), in_specs=[row] * 6, out_specs=[row] * 3,
        out_shape=[jax.ShapeDtypeStruct((T, B), BF16), jax.ShapeDtypeStruct((T, B), F32),
                   jax.ShapeDtypeStruct((T, B), F32)],
        compiler_params=_params("parallel"),
    )(*os_, *ls_)


def _dilated_delta(dcat, o, name):
    T, B = o.shape
    H = B // HEAD_DIM

    def body(do_ref, o_ref, d_ref):
        for h in range(H):
            cs = pl.ds(h * HEAD_DIM, HEAD_DIM)
            dsum = jnp.sum(do_ref[:, cs] * o_ref[:, cs], axis=-1, keepdims=True)
            d_ref[:, cs] = jnp.broadcast_to(dsum, (ROWS, HEAD_DIM))

    row = pl.BlockSpec((ROWS, B), lambda i: (i, 0))
    return pl.pallas_call(
        body, name=name, grid=(T // ROWS,),
        in_specs=[pl.BlockSpec((ROWS, B), lambda i: (i, 1)), row], out_specs=row,
        out_shape=jax.ShapeDtypeStruct((T, B), F32), compiler_params=_params("parallel"),
    )(dcat, o)


def _dilated_dq(z, dcat, lse, delta, d, B, name):
    T = z.shape[0]
    H = B // HEAD_DIM
    nb = T // (d * ATT_BLOCK)
    zv = z.reshape(T // d, d * 5 * B)
    dov = dcat.reshape(T // d, d * 2 * B)
    lv = lse.reshape(T // d, d * B)
    dv_ = delta.reshape(T // d, d * B)
    scale = HEAD_DIM ** -0.5
    blk = ATT_BLOCK

    def body(q_ref, kp_ref, kc_ref, vp_ref, vc_ref, do_ref, l_ref, dl_ref, dq_ref):
        n = pl.program_id(1)
        qi = lax.broadcasted_iota(jnp.int32, (blk, 2 * blk), 0)
        kj = lax.broadcasted_iota(jnp.int32, (blk, 2 * blk), 1)
        dist = qi + blk - kj
        valid = (dist >= 0) & (dist <= blk) & ((kj >= blk) | (n > 0))
        distf = dist.astype(F32)
        for h in range(H):
            cs = pl.ds(h * HEAD_DIM, HEAD_DIM)
            q = (q_ref[:, cs] * scale).astype(BF16)
            k = jnp.concatenate([kp_ref[:, cs], kc_ref[:, cs]], axis=0).astype(BF16)
            v = jnp.concatenate([vp_ref[:, cs], vc_ref[:, cs]], axis=0).astype(BF16)
            s = _dil_scores(q, k, _alibi_slope(h, H) * d, valid, distf)
            p = jnp.exp(s - l_ref[:, h * HEAD_DIM:h * HEAD_DIM + 1])
            dp = _dot(do_ref[:, cs].astype(BF16), v, NT)
            ds = p * (dp - dl_ref[:, h * HEAD_DIM:h * HEAD_DIM + 1])
            dq_ref[:, cs] = _dot(ds.astype(BF16), k)

    def col(unit):
        return lambda r, n: (n, r * 5 + unit)

    def col_prev(unit):
        return lambda r, n: (jnp.maximum(n - 1, 0), r * 5 + unit)

    bs = (blk, B)
    out = pl.BlockSpec(bs, lambda r, n: (n, r))
    dq = pl.pallas_call(
        body, name=name, grid=(d, nb),
        in_specs=[pl.BlockSpec(bs, col(2)), pl.BlockSpec(bs, col_prev(3)), pl.BlockSpec(bs, col(3)),
                  pl.BlockSpec(bs, col_prev(4)), pl.BlockSpec(bs, col(4)),
                  pl.BlockSpec(bs, lambda r, n: (n, r * 2 + 1)), out, out],
        out_specs=out,
        out_shape=jax.ShapeDtypeStruct((T // d, d * B), F32),
        compiler_params=_params("parallel", "parallel"),
    )(zv, zv, zv, zv, zv, dov, lv, dv_)
    return dq.reshape(T, B)


def _dilated_dkv(z, dcat, lse, delta, d, B, name):
    T = z.shape[0]
    H = B // HEAD_DIM
    nb = T // (d * ATT_BLOCK)
    zv = z.reshape(T // d, d * 5 * B)
    dov = dcat.reshape(T // d, d * 2 * B)
    lv = lse.reshape(T // d, d * B)
    dv_ = delta.reshape(T // d, d * B)
    scale = HEAD_DIM ** -0.5
    blk = ATT_BLOCK

    def body(k_ref, v_ref, qa_ref, qb_ref, doa_ref, dob_ref, la_ref, lb_ref, da_ref, db_ref, dk_ref, dv_ref):
        m = pl.program_id(1)
        qi = lax.broadcasted_iota(jnp.int32, (2 * blk, blk), 0)
        kj = lax.broadcasted_iota(jnp.int32, (2 * blk, blk), 1)
        dist = qi - kj
        valid = (dist >= 0) & (dist <= blk) & ((qi < blk) | (m + 1 < nb))
        distf = dist.astype(F32)
        for h in range(H):
            cs = pl.ds(h * HEAD_DIM, HEAD_DIM)
            c1 = slice(h * HEAD_DIM, h * HEAD_DIM + 1)
            q = (jnp.concatenate([qa_ref[:, cs], qb_ref[:, cs]], axis=0) * scale).astype(BF16)
            do = jnp.concatenate([doa_ref[:, cs], dob_ref[:, cs]], axis=0).astype(BF16)
            lse2 = jnp.concatenate([la_ref[:, c1], lb_ref[:, c1]], axis=0)
            dl2 = jnp.concatenate([da_ref[:, c1], db_ref[:, c1]], axis=0)
            k = k_ref[:, cs].astype(BF16)
            v = v_ref[:, cs].astype(BF16)
            s = _dil_scores(q, k, _alibi_slope(h, H) * d, valid, distf)
            p = jnp.exp(jnp.where(valid, s - lse2, MASKED))
            dv_ref[:, cs] = _dot(p.astype(BF16), do, TN)
            dp = _dot(do, v, NT)
            ds = p * (dp - dl2)
            dk_ref[:, cs] = _dot(ds.astype(BF16), q, TN)

    def col(unit):
        return lambda r, m: (m, r * 5 + unit)

    def nxt(width, unit):
        return lambda r, m: (jnp.minimum(m + 1, nb - 1), r * width + unit)

    bs = (blk, B)
    out = pl.BlockSpec(bs, lambda r, m: (m, r))
    dk, dv = pl.pallas_call(
        body, name=name, grid=(d, nb),
        in_specs=[pl.BlockSpec(bs, col(3)), pl.BlockSpec(bs, col(4)),
                  pl.BlockSpec(bs, col(2)), pl.BlockSpec(bs, nxt(5, 2)),
                  pl.BlockSpec(bs, lambda r, m: (m, r * 2 + 1)), pl.BlockSpec(bs, nxt(2, 1)),
                  out, pl.BlockSpec(bs, nxt(1, 0)), out, pl.BlockSpec(bs, nxt(1, 0))],
        out_specs=[out, out],
        out_shape=[jax.ShapeDtypeStruct((T // d, d * B), F32)] * 2,
        compiler_params=_params("parallel", "parallel"),
    )(zv, zv, zv, zv, dov, dov, lv, lv, dv_, dv_)
    return dk.reshape(T, B), dv.reshape(T, B)


def _dilated_combine(dqs, dks, dvs, name):
    T, B = dqs[0].shape
    scale = HEAD_DIM ** -0.5

    def body(q1, q2, q3, k1, k2, k3, v1, v2, v3, o_ref):
        o_ref[:, pl.ds(0, B)] = ((q1[...] + q2[...] + q3[...]) * scale).astype(BF16)
        o_ref[:, pl.ds(B, B)] = (k1[...] + k2[...] + k3[...]).astype(BF16)
        o_ref[:, pl.ds(2 * B, B)] = (v1[...] + v2[...] + v3[...]).astype(BF16)

    row = pl.BlockSpec((ROWS, B), lambda i: (i, 0))
    return pl.pallas_call(
        body, name=name, grid=(T // ROWS,), in_specs=[row] * 9,
        out_specs=pl.BlockSpec((ROWS, 3 * B), lambda i: (i, 0)),
        out_shape=jax.ShapeDtypeStruct((T, 3 * B), BF16), compiler_params=_params("parallel"),
    )(*dqs, *dks, *dvs)


def _split_dot(x, tri):
    hi = x.astype(BF16)
    lo = (x - hi.astype(F32)).astype(BF16)
    return _dot(hi, tri) + _dot(lo, tri)


SB_QUERY_ROWS = 512


def _tri_and_ones(pred):
    rows = lax.broadcasted_iota(jnp.int32, (ATT_BLOCK, 2 * ATT_BLOCK), 0)
    cols = lax.broadcasted_iota(jnp.int32, (ATT_BLOCK, 2 * ATT_BLOCK), 1)
    return ((cols >= ATT_BLOCK) | pred(rows, cols)).astype(BF16)


def _sb_mask(query_rows, s):
    rows = lax.broadcasted_iota(jnp.int32, (query_rows, ATT_BLOCK), 0)
    cols = lax.broadcasted_iota(jnp.int32, (query_rows, ATT_BLOCK), 1)
    return cols + s * ATT_BLOCK < rows


def _log_sigmoids(z):
    e = jnp.exp(-jnp.abs(z))
    sp = jnp.log(1.0 + e)
    return jnp.minimum(z, 0.0) - sp, jnp.minimum(-z, 0.0) - sp, e


def _sb_fwd(qkv, W, name):
    T = qkv.shape[0]
    H = W // HEAD_DIM
    blk = ATT_BLOCK
    qb = min(SB_QUERY_ROWS, T)
    per = qb // blk

    def body(q_ref, k_ref, v_ref, o_ref, lt_ref, acc_ref):
        i = pl.program_id(1)
        q = q_ref[...]
        tri = _tri_and_ones(lambda r, c: r > c)
        lt_ref[...] = jnp.zeros_like(lt_ref)
        acc_ref[...] = jnp.zeros_like(acc_ref)

        def tile(j, mask):
            ks = pl.ds(pl.multiple_of(j * blk, blk), blk)
            z = _dot(q, k_ref[ks, :], NT)
            ls, lm, _ = _log_sigmoids(z)
            if mask is not None:
                lm = jnp.where(mask, lm, 0.0)
            both = _split_dot(lm, tri)
            a = jnp.exp(ls + both[:, :blk] + lt_ref[...])
            if mask is not None:
                a = jnp.where(mask, a, 0.0)
            acc_ref[...] += _dot(a.astype(BF16), v_ref[ks, :])
            lt_ref[...] += both[:, blk:]

        for s in reversed(range(per)):
            tile(i * per + s, _sb_mask(qb, s))

        def step(jj, _):
            tile(i * per - 1 - jj, None)
            return 0

        lax.fori_loop(0, i * per, step, 0)
        o_ref[...] = acc_ref[...].astype(BF16)

    qs = pl.BlockSpec((qb, HEAD_DIM), lambda h, i: (i, h))
    return pl.pallas_call(
        body, name=name, grid=(H, T // qb),
        in_specs=[qs, pl.BlockSpec((T, HEAD_DIM), lambda h, i: (0, H + h)),
                  pl.BlockSpec((T, HEAD_DIM), lambda h, i: (0, 2 * H + h))],
        out_specs=[qs, qs],
        out_shape=[jax.ShapeDtypeStruct((T, W), BF16), jax.ShapeDtypeStruct((T, W), F32)],
        scratch_shapes=[pltpu.VMEM((qb, HEAD_DIM), F32)],
        compiler_params=_params("parallel", "arbitrary"),
    )(qkv, qkv, qkv)


def _sb_bwd(qkv, do, ltot, W, name):
    T = qkv.shape[0]
    H = W // HEAD_DIM
    blk = ATT_BLOCK
    nkb = T // blk
    qb = min(SB_QUERY_ROWS, T)
    per = qb // blk

    def body(q_ref, k_ref, v_ref, do_ref, lt_ref, dq_ref, dkt_ref, dvt_ref, qt_ref, dot_ref, plm_ref, pg_ref):
        i = pl.program_id(1)

        @pl.when(i == 0)
        def _():
            dkt_ref[...] = jnp.zeros_like(dkt_ref)
            dvt_ref[...] = jnp.zeros_like(dvt_ref)

        q = q_ref[...]
        do = do_ref[...]
        qt_ref[...] = q.astype(F32).T.astype(BF16)
        dot_ref[...] = do.astype(F32).T.astype(BF16)
        upto = _tri_and_ones(lambda r, c: r <= c)
        before = _tri_and_ones(lambda r, c: r < c)
        plm_ref[...] = jnp.zeros_like(plm_ref)
        pg_ref[...] = jnp.zeros_like(pg_ref)
        dq_ref[...] = jnp.zeros_like(dq_ref)

        def tile(j, mask):
            ks = pl.ds(pl.multiple_of(j * blk, blk), blk)
            k = k_ref[ks, :]
            v = v_ref[ks, :]
            z = _dot(q, k, NT)
            ls, lm, e = _log_sigmoids(z)
            if mask is not None:
                lm = jnp.where(mask, lm, 0.0)
            both = _split_dot(lm, upto)
            a = jnp.exp(ls + lt_ref[...] - (plm_ref[...] + both[:, :blk]))
            if mask is not None:
                a = jnp.where(mask, a, 0.0)
            g = a * _dot(do, v, NT)
            gboth = _split_dot(g, before)
            gsum = pg_ref[...] + gboth[:, :blk]
            r = 1.0 / (1.0 + e)
            pos = z >= 0.0
            sig = jnp.where(pos, r, e * r)
            nsig = jnp.where(pos, e * r, r)
            dz = g * nsig - gsum * sig
            if mask is not None:
                dz = jnp.where(mask, dz, 0.0)
            dzb = dz.astype(BF16)
            dkt_ref[j] += _dot(qt_ref[...], dzb)
            dvt_ref[j] += _dot(dot_ref[...], a.astype(BF16))
            dq_ref[...] += _dot(dzb, k)
            plm_ref[...] += both[:, blk:]
            pg_ref[...] += gboth[:, blk:]

        def step(j, _):
            tile(j, None)
            return 0

        lax.fori_loop(0, i * per, step, 0)
        for s in range(per):
            tile(i * per + s, _sb_mask(qb, s))

    qs = pl.BlockSpec((qb, HEAD_DIM), lambda h, i: (i, h))
    res = pl.BlockSpec((None, nkb, HEAD_DIM, blk), lambda h, i: (h, 0, 0, 0))
    return pl.pallas_call(
        body, name=name, grid=(H, T // qb),
        in_specs=[qs, pl.BlockSpec((T, HEAD_DIM), lambda h, i: (0, H + h)),
                  pl.BlockSpec((T, HEAD_DIM), lambda h, i: (0, 2 * H + h)), qs, qs],
        out_specs=[qs, res, res],
        out_shape=[jax.ShapeDtypeStruct((T, W), F32)] + [jax.ShapeDtypeStruct((H, nkb, HEAD_DIM, blk), F32)] * 2,
        scratch_shapes=[pltpu.VMEM((HEAD_DIM, qb), BF16), pltpu.VMEM((HEAD_DIM, qb), BF16),
                        pltpu.VMEM((qb, HEAD_DIM), F32), pltpu.VMEM((qb, HEAD_DIM), F32)],
        compiler_params=_params("parallel", "arbitrary"),
    )(qkv, qkv, qkv, do, ltot)


def _sb_pack(dq, dkt, dvt, name):
    T, W = dq.shape
    H = W // HEAD_DIM
    blk = ATT_BLOCK
    scale = HEAD_DIM ** -0.5

    def body(q_ref, kt_ref, vt_ref, o_ref):
        o_ref[:, pl.ds(0, W)] = (q_ref[...] * scale).astype(BF16)
        for h in range(H):
            o_ref[:, pl.ds(W + h * HEAD_DIM, HEAD_DIM)] = kt_ref[h].T.astype(BF16)
            o_ref[:, pl.ds(2 * W + h * HEAD_DIM, HEAD_DIM)] = vt_ref[h].T.astype(BF16)

    tr = pl.BlockSpec((H, None, HEAD_DIM, blk), lambda i: (0, i, 0, 0))
    return pl.pallas_call(
        body, name=name, grid=(T // blk,), in_specs=[pl.BlockSpec((blk, W), lambda i: (i, 0)), tr, tr],
        out_specs=pl.BlockSpec((blk, 3 * W), lambda i: (i, 0)),
        out_shape=jax.ShapeDtypeStruct((T, 3 * W), BF16), compiler_params=_params("parallel"),
    )(dq, dkt, dvt)


def _local_step(x, target, norms, sgu, comm):
    T, D = x.shape
    A = D // 2
    pre_mix, post_mix, pre_ffn, post_ffn = norms
    ln_g, ln_b, w_s, b_s = sgu
    b_t = b_s.T
    scale = HEAD_DIM ** -0.5

    def vec(p, layer):
        return comm.tie(p[layer:layer + 1])

    h0 = _prenorm(x, vec(pre_mix, 0), "prenorm0")
    z = _matmul(h0, comm.weight("ab_in"), mode="nn", name="ab_in_fwd")
    a_out = _sgu_fwd(z, ln_g, ln_b, w_s, b_t, "sgu_fwd")
    branch = [_dilated_fwd(z, d, A, f"dilated_fwd_{d}") for _, d in DILATED_PAIRS]
    b_out, o_dil, lse_dil = _dilated_merge([b[0] for b in branch], [b[1] for b in branch], "dilated_merge")
    cat = jnp.concatenate([a_out, b_out], axis=-1)
    y0 = _matmul(cat, comm.weight("ab_out"), mode="nn", name="ab_out_fwd")
    comm.arrive("ffn0", after=y0)
    x1, h1 = _postnorm_prenorm(x, y0, vec(post_mix, 0), vec(pre_ffn, 0), "norm_mix0")
    comm.land("ffn0", after=h1)

    def relu2(acc, j):
        r = jnp.maximum(acc, 0.0)
        return r * r

    f0 = _matmul(h1, comm.weight("w1_0"), mode="nn", name="ffn0_w1_fwd", out_dtype=BF16, epi=relu2)
    y1 = _matmul(f0, comm.weight("w2_0"), mode="nn", name="ffn0_w2_fwd")
    comm.arrive("rest", after=y1)
    x2, h2 = _postnorm_prenorm(x1, y1, vec(post_ffn, 0), vec(pre_mix, 1), "norm_ffn0")
    comm.land("rest", after=h2)

    tn_qkv = _tile(D, 512)
    nq = D // tn_qkv

    def scale_q(acc, j):
        return jnp.where(j < nq, acc * scale, acc)

    qkv = _matmul(h2, comm.weight("sb_in"), mode="nn", name="sb_in_fwd", out_dtype=BF16, tn=tn_qkv, epi=scale_q)
    o_sb, ltot = _sb_fwd(qkv, D, "sb_fwd")
    y2 = _matmul(o_sb, comm.weight("sb_out"), mode="nn", name="sb_out_fwd")
    x3, h3 = _postnorm_prenorm(x2, y2, vec(post_mix, 1), vec(pre_ffn, 1), "norm_mix1")
    f1 = _matmul(h3, comm.weight("w1_1"), mode="nn", name="ffn1_w1_fwd", out_dtype=BF16, epi=relu2)
    y3 = _matmul(f1, comm.weight("w2_1"), mode="nn", name="ffn1_w2_fwd")
    loss_tile, dx4 = _postnorm_loss(x3, y3, vec(post_ffn, 1), target, "norm_loss")
    loss = loss_tile[0, 0]

    def relu2_bwd(acc, j, f):
        return acc * (2.0 * jnp.sqrt(f.astype(F32)))

    def ffn_bwd(dx_out, x_in, h, f, y, layer, tag):
        dy, dg_post = _postnorm_bwd(dx_out, y, vec(post_ffn, layer), f"ffn{tag}_postnorm_bwd")
        g_w2 = _matmul(f, dy, mode="tn", name=f"ffn{tag}_w2_wgrad", tm=1024, tn=1024, tk=1024)
        da = _matmul(dy, comm.weight(f"w2_{layer}"), mode="nt", name=f"ffn{tag}_w2_dgrad", out_dtype=BF16,
                     epi=relu2_bwd, extras=(f,))
        g_w1 = _matmul(h, da, mode="tn", name=f"ffn{tag}_w1_wgrad", tm=1024, tn=1024, tk=1024)
        comm.reduce(f"ffn{tag}", {f"w2_{layer}": g_w2, f"w1_{layer}": g_w1})
        dh = _matmul(da, comm.weight(f"w1_{layer}"), mode="nt", name=f"ffn{tag}_w1_dgrad", after=comm.started())
        dx_in, dg_pre = _prenorm_bwd(dx_out, dh, x_in, vec(pre_ffn, layer), f"ffn{tag}_prenorm_bwd")
        return dx_in, dg_pre, dg_post

    dx3, dg_pre_ffn1, dg_post_ffn1 = ffn_bwd(dx4, x3, h3, f1, y3, 1, "1")

    dy2, dg_post_mix1 = _postnorm_bwd(dx3, y2, vec(post_mix, 1), "sb_postnorm_bwd")
    g_sb_out = _matmul(o_sb, dy2, mode="tn", name="sb_out_wgrad", tm=1024, tn=1024, tk=1024)
    do_sb = _matmul(dy2, comm.weight("sb_out"), mode="nt", name="sb_out_dgrad", out_dtype=BF16)
    dq, dk, dv = _sb_bwd(qkv, do_sb, ltot, D, "sb_bwd")
    dqkv = _sb_pack(dq, dk, dv, "sb_pack")
    g_sb_in = _matmul(h2, dqkv, mode="tn", name="sb_in_wgrad", tm=1024, tn=1024, tk=1024)
    comm.reduce("sb", {"sb_out": g_sb_out, "sb_in": g_sb_in})
    dh2 = _matmul(dqkv, comm.weight("sb_in"), mode="nt", name="sb_in_dgrad", after=comm.started())
    dx2, dg_pre_mix1 = _prenorm_bwd(dx3, dh2, x2, vec(pre_mix, 1), "sb_prenorm_bwd")

    dx1, dg_pre_ffn0, dg_post_ffn0 = ffn_bwd(dx2, x1, h1, f0, y1, 0, "0")

    dy0, dg_post_mix0 = _postnorm_bwd(dx1, y0, vec(post_mix, 0), "ab_postnorm_bwd")
    g_ab_out = _matmul(cat, dy0, mode="tn", name="ab_out_wgrad", tm=1024, tn=1024, tk=1024)
    dcat = _matmul(dy0, comm.weight("ab_out"), mode="nt", name="ab_out_dgrad")
    duv, d_ln_g, d_ln_b, d_w_s, d_b_t = _sgu_bwd(z, dcat, ln_g, ln_b, w_s, b_t, "sgu_bwd")
    delta = _dilated_delta(dcat, o_dil, "dilated_delta")
    dqs, dks, dvs = [], [], []
    for _, d in DILATED_PAIRS:
        dqs.append(_dilated_dq(z, dcat, lse_dil, delta, d, A, f"dilated_dq_{d}"))
        dk_b, dv_b = _dilated_dkv(z, dcat, lse_dil, delta, d, A, f"dilated_dkv_{d}")
        dks.append(dk_b)
        dvs.append(dv_b)
    dqkv0 = _dilated_combine(dqs, dks, dvs, "dilated_combine")
    dz = jnp.concatenate([duv, dqkv0], axis=-1)
    g_ab_in = _matmul(h0, dz, mode="tn", name="ab_in_wgrad", tm=1024, tn=1024, tk=1024)
    comm.reduce("ab", {"ab_out": g_ab_out, "ab_in": g_ab_in})
    dh0 = _matmul(dz, comm.weight("ab_in"), mode="nt", name="ab_in_dgrad", after=comm.started())
    dx0, dg_pre_mix0 = _prenorm_bwd(dx1, dh0, x, vec(pre_mix, 0), "ab_prenorm_bwd")

    small = {
        "pre_mix": jnp.concatenate([dg_pre_mix0, dg_pre_mix1], axis=0),
        "post_mix": jnp.concatenate([dg_post_mix0, dg_post_mix1], axis=0),
        "pre_ffn": jnp.concatenate([dg_pre_ffn0, dg_pre_ffn1], axis=0),
        "post_ffn": jnp.concatenate([dg_post_ffn0, dg_post_ffn1], axis=0),
        "ln_g": d_ln_g, "ln_b": d_ln_b, "w_s": d_w_s, "b_s": d_b_t.T,
    }
    return loss, dx0, small


MESH_ID = pl.DeviceIdType.MESH
ANY = pl.BlockSpec(memory_space=pl.ANY)


def _coords():
    return lax.axis_index("x"), lax.axis_index("y"), lax.axis_index("c")


def _shard_of(ref, kind, p):
    if kind == "col":
        n = ref.shape[1] // N_DEV
        return ref.at[:, pl.ds(pl.multiple_of(p * n, 128), n)]
    r = ref.shape[0] // N_DEV
    return ref.at[pl.ds(pl.multiple_of(p * r, 16), r), :]


def _full_shape(shard, kind):
    if kind == "col":
        return (shard.shape[0], shard.shape[1] * N_DEV)
    return (shard.shape[0] * N_DEV, shard.shape[1])


def _place(shard, kind, block, name):
    rows, cols = shard.shape
    tr = _tile(rows, 512)

    def body(b_ref, s_ref, o_ref):
        o_ref[...] = s_ref[...].astype(BF16)

    if kind == "col":
        out = pl.BlockSpec((tr, cols), lambda i, b_ref: (i, b_ref[0]))
    else:
        out = pl.BlockSpec((tr, cols), lambda i, b_ref: (b_ref[0] * (rows // tr) + i, 0))
    return pl.pallas_call(
        body, name=name,
        grid_spec=pltpu.PrefetchScalarGridSpec(
            num_scalar_prefetch=1, grid=(rows // tr,),
            in_specs=[pl.BlockSpec((tr, cols), lambda i, b_ref: (i, 0))], out_specs=out),
        out_shape=jax.ShapeDtypeStruct(_full_shape(shard, kind), BF16),
        compiler_params=_params("parallel"),
    )(block, shard)


def _all_gather(fulls, kinds):
    nt = len(fulls)

    def body(*refs):
        ins = refs[:nt]
        send_sems, recv_sems = refs[2 * nt:]
        x, y, c = _coords()
        me, sibling = (x, y, c), (x, y, 1 - c)
        chips = [(1 - x, y), (x, 1 - y), (1 - x, 1 - y)]

        def copy(t, k, block, to):
            px, py, pc = block
            slot = _shard_of(ins[t], kinds[t], 4 * px + 2 * py + pc)
            return pltpu.make_async_remote_copy(
                src_ref=slot, dst_ref=slot, send_sem=send_sems.at[7 * t + k], recv_sem=recv_sems.at[7 * t + k],
                device_id=to, device_id_type=MESH_ID)

        sent = []
        for t in range(nt):
            first = [copy(t, 0, me, sibling)] + [copy(t, 1 + j, me, (*chip, c)) for j, chip in enumerate(chips)]
            for cp in first:
                cp.start()
            sent += first
        for t in range(nt):
            for j, chip in enumerate(chips):
                copy(t, 1 + j, (*chip, c), me).wait_recv()
                sent.append(copy(t, 4 + j, (*chip, c), sibling))
                sent[-1].start()
        for t in range(nt):
            copy(t, 0, sibling, me).wait_recv()
            for j, chip in enumerate(chips):
                copy(t, 4 + j, (*chip, 1 - c), me).wait_recv()
        for cp in sent:
            cp.wait_send()

    return pl.pallas_call(
        body, name="all_gather_weights",
        in_specs=[ANY] * nt, out_specs=[ANY] * nt,
        out_shape=[jax.ShapeDtypeStruct(f.shape, f.dtype) for f in fulls],
        input_output_aliases={t: t for t in range(nt)},
        scratch_shapes=[pltpu.SemaphoreType.DMA((7 * nt,)), pltpu.SemaphoreType.DMA((7 * nt,))],
    )(*fulls)


HBM = pl.BlockSpec(memory_space=pltpu.HBM)
SEM = pl.BlockSpec(memory_space=pltpu.SEMAPHORE)
FLOWS = pltpu.SideEffectType.DATAFLOW_SIDE_EFFECTING


def _in_hbm(a):
    return pltpu.with_memory_space_constraint(a, pltpu.HBM)


def _hbm_like(bufs):
    return [pltpu.HBM(b.shape, b.dtype) for b in bufs]


def _copies_start(name, bufs, plan, n, after):
    nb = len(bufs)

    def body(*refs):
        send_sems, recv_sems, token = refs[nb + 1], refs[nb + 2], refs[-1]
        for cp in plan(refs[:nb], send_sems, recv_sems):
            cp.start()
        token[...] = jnp.zeros_like(token)

    out = pl.pallas_call(
        body, name=name, in_specs=[HBM] * nb + [ANY],
        out_specs=[SEM, SEM] + [HBM] * nb + [pl.BlockSpec(memory_space=pltpu.VMEM)],
        out_shape=[pltpu.SemaphoreType.DMA((n,)), pltpu.SemaphoreType.DMA((n,))] + _hbm_like(bufs)
        + [jax.ShapeDtypeStruct((8, 128), F32)],
        input_output_aliases={i: 2 + i for i in range(nb)},
        compiler_params=pltpu.CompilerParams(has_side_effects=FLOWS),
    )(*[_in_hbm(b) for b in bufs], after)
    return (out[0], out[1]), list(out[2:2 + nb]), out[-1]


def _copies_wait(name, bufs, sems, after, plan):
    nb = len(bufs)

    def body(*refs):
        for cp in plan(refs[:nb], refs[nb], refs[nb + 1]):
            cp.wait_send()
            cp.wait_recv()

    out = pl.pallas_call(
        body, name=name, in_specs=[HBM] * nb + [SEM, SEM, ANY], out_specs=[HBM] * nb,
        out_shape=_hbm_like(bufs), input_output_aliases={i: i for i in range(nb)},
        compiler_params=pltpu.CompilerParams(has_side_effects=FLOWS),
    )(*bufs, *sems, after)
    return list(out)


def _copies_wait_start(name, bufs, sems, after, plan, next_plan, n_next):
    nb = len(bufs)

    def body(*refs):
        ins = refs[:nb]
        for cp in plan(ins, refs[nb], refs[nb + 1]):
            cp.wait_send()
            cp.wait_recv()
        send_sems, recv_sems, token = refs[nb + 3], refs[nb + 4], refs[-1]
        for cp in next_plan(ins, send_sems, recv_sems):
            cp.start()
        token[...] = jnp.zeros_like(token)

    out = pl.pallas_call(
        body, name=name, in_specs=[HBM] * nb + [SEM, SEM, ANY],
        out_specs=[SEM, SEM] + [HBM] * nb + [pl.BlockSpec(memory_space=pltpu.VMEM)],
        out_shape=[pltpu.SemaphoreType.DMA((n_next,)), pltpu.SemaphoreType.DMA((n_next,))] + _hbm_like(bufs)
        + [jax.ShapeDtypeStruct((8, 128), F32)],
        input_output_aliases={i: 2 + i for i in range(nb)},
        compiler_params=pltpu.CompilerParams(has_side_effects=FLOWS),
    )(*bufs, *sems, after)
    return (out[0], out[1]), list(out[2:2 + nb]), out[-1]


def _gather_plans(kinds):
    nt = len(kinds)

    def slot(refs, t, px, py, pc):
        return _shard_of(refs[t], kinds[t], 4 * px + 2 * py + pc)

    def to_chips(refs, send_sems, recv_sems):
        x, y, c = _coords()
        peers = [(x, y, 1 - c), (1 - x, y, c), (x, 1 - y, c), (1 - x, 1 - y, c)]
        return [pltpu.make_async_remote_copy(
            src_ref=slot(refs, t, x, y, c), dst_ref=slot(refs, t, x, y, c), send_sem=send_sems.at[4 * t + k],
            recv_sem=recv_sems.at[4 * t + k], device_id=peer, device_id_type=MESH_ID)
            for t in range(nt) for k, peer in enumerate(peers)]

    def to_sibling(refs, send_sems, recv_sems):
        x, y, c = _coords()
        chips = [(1 - x, y), (x, 1 - y), (1 - x, 1 - y)]
        return [pltpu.make_async_remote_copy(
            src_ref=slot(refs, t, *chip, c), dst_ref=slot(refs, t, *chip, c), send_sem=send_sems.at[3 * t + j],
            recv_sem=recv_sems.at[3 * t + j], device_id=(x, y, 1 - c), device_id_type=MESH_ID)
            for t in range(nt) for j, chip in enumerate(chips)]

    return to_chips, to_sibling


def _shard_shape(full, kind):
    if kind == "col":
        return (full.shape[0], full.shape[1] // N_DEV)
    return (full.shape[0] // N_DEV, full.shape[1])


CHIPS = ((0, 0), (0, 1), (1, 0), (1, 1))


def _exchange_siblings(grads, kinds, name):
    nt = len(grads)

    def body(*refs):
        ins, outs = refs[:nt], refs[nt:2 * nt]
        send_sems, recv_sems = refs[2 * nt:]
        x, y, c = _coords()
        copies = []
        for t in range(nt):
            for q, (qx, qy) in enumerate(CHIPS):
                copies.append(pltpu.make_async_remote_copy(
                    src_ref=_shard_of(ins[t], kinds[t], 4 * qx + 2 * qy + (1 - c)), dst_ref=outs[t].at[q],
                    send_sem=send_sems.at[4 * t + q], recv_sem=recv_sems.at[4 * t + q],
                    device_id=(x, y, 1 - c), device_id_type=MESH_ID))
        for cp in copies:
            cp.start()
        for cp in copies:
            cp.wait()

    return pl.pallas_call(
        body, name=name,
        in_specs=[ANY] * nt, out_specs=[ANY] * nt,
        out_shape=[jax.ShapeDtypeStruct((4,) + _shard_shape(g, k), F32) for g, k in zip(grads, kinds)],
        scratch_shapes=[pltpu.SemaphoreType.DMA((4 * nt,)), pltpu.SemaphoreType.DMA((4 * nt,))],
    )(*grads)


def _pair_sum(grad, other, kind, where, name):
    rows, cols = _shard_shape(grad, kind)
    tr = _tile(rows, 256)

    def body(w_ref, g_ref, o_ref, s_ref, land_ref):
        s = (g_ref[...] + o_ref[...]).astype(BF16)
        s_ref[...] = s

        @pl.when(pl.program_id(1) == w_ref[1])
        def _():
            land_ref[...] = s

    if kind == "col":
        g_spec = pl.BlockSpec((tr, cols), lambda i, q, w_ref: (i, 2 * q + w_ref[0]))
    else:
        g_spec = pl.BlockSpec((tr, cols), lambda i, q, w_ref: ((2 * q + w_ref[0]) * (rows // tr) + i, 0))
    part = pl.BlockSpec((None, tr, cols), lambda i, q, w_ref: (q, i, 0))
    mine = pl.BlockSpec((None, tr, cols), lambda i, q, w_ref: (w_ref[1], i, 0))
    return pl.pallas_call(
        body, name=name,
        grid_spec=pltpu.PrefetchScalarGridSpec(
            num_scalar_prefetch=1, grid=(rows // tr, 4), in_specs=[g_spec, part], out_specs=[part, mine]),
        out_shape=[jax.ShapeDtypeStruct((4, rows, cols), BF16)] * 2,
        compiler_params=_params("parallel", "arbitrary"),
    )(where, grad, other)


def _scatter_plan(nt):
    def plan(refs, send_sems, recv_sems):
        x, y, c = _coords()
        chips = [(1 - x, y), (x, 1 - y), (1 - x, 1 - y)]
        return [pltpu.make_async_remote_copy(
            src_ref=refs[t].at[2 * qx + qy], dst_ref=refs[nt + t].at[2 * x + y], send_sem=send_sems.at[3 * t + j],
            recv_sem=recv_sems.at[3 * t + j], device_id=(qx, qy, c), device_id_type=MESH_ID)
            for t in range(nt) for j, (qx, qy) in enumerate(chips)]
    return plan


def _all_reduce_small(vec):
    R = vec.shape[0]

    def body(v_ref, o_ref, recv_ref, send_sems, recv_sems):
        x, y, c = _coords()
        me = 4 * x + 2 * y + c
        recv_ref[me] = v_ref[...]
        copies = []
        for k in range(1, N_DEV):
            bx, by, bc = (k >> 2) & 1, (k >> 1) & 1, k & 1
            peer = (1 - x if bx else x, 1 - y if by else y, 1 - c if bc else c)
            copies.append(pltpu.make_async_remote_copy(
                src_ref=v_ref, dst_ref=recv_ref.at[me],
                send_sem=send_sems.at[k - 1], recv_sem=recv_sems.at[k - 1],
                device_id=peer, device_id_type=MESH_ID))
        for cp in copies:
            cp.start()
        for cp in copies:
            cp.wait()
        total = recv_ref[0]
        for p in range(1, N_DEV):
            total = total + recv_ref[p]
        o_ref[...] = total

    return pl.pallas_call(
        body, name="all_reduce_small",
        in_specs=[pl.BlockSpec(memory_space=pltpu.VMEM)], out_specs=pl.BlockSpec(memory_space=pltpu.VMEM),
        out_shape=jax.ShapeDtypeStruct((R, 128), F32),
        scratch_shapes=[pltpu.VMEM((N_DEV, R, 128), F32), pltpu.SemaphoreType.DMA((N_DEV - 1,)),
                        pltpu.SemaphoreType.DMA((N_DEV - 1,))],
        compiler_params=pltpu.CompilerParams(vmem_limit_bytes=VMEM_LIMIT),
    )(vec)


def _adamw_math(w, g, m, v):
    m = ADAM_B1 * m + (1.0 - ADAM_B1) * g
    v = ADAM_B2 * v + (1.0 - ADAM_B2) * (g * g)
    m_hat = m / (1.0 - ADAM_B1 ** ADAM_STEP)
    v_hat = v / (1.0 - ADAM_B2 ** ADAM_STEP)
    delta = -ADAM_LR * (m_hat / (jnp.sqrt(v_hat) + ADAM_EPS) + ADAM_WD * w)
    return delta, m, v


def _adamw(w, parts, m, v, name):
    layers, rows, cols = w.shape
    tr = rows if rows * cols <= 256 * 1024 else _tile(rows, 128)
    out = None
    for layer in range(layers):
        n = parts[layer].shape[0]

        def body(w_ref, p_ref, m_ref, v_ref, *rest, n=n):
            g_ref, d_ref, mo_ref, vo_ref = rest[-4:]
            g = p_ref[0].astype(F32)
            for q in range(1, n):
                g = g + p_ref[q].astype(F32)
            g_ref[...] = g
            d_ref[...], mo_ref[...], vo_ref[...] = _adamw_math(w_ref[...], g, m_ref[...], v_ref[...])

        blk = pl.BlockSpec((None, tr, cols), lambda i, layer=layer: (layer, i, 0))
        earlier = [] if out is None else list(out)
        out = pl.pallas_call(
            body, name=f"{name}_{layer}", grid=(rows // tr,),
            in_specs=[blk, pl.BlockSpec((n, tr, cols), lambda i: (0, i, 0)), blk, blk] + [ANY] * len(earlier),
            out_specs=[blk] * 4, out_shape=[jax.ShapeDtypeStruct((layers, rows, cols), F32)] * 4,
            input_output_aliases={4 + k: k for k in range(len(earlier))},
            compiler_params=_params("parallel"),
        )(w, parts[layer], m, v, *earlier)
    return out


def _pack(arrays):
    rows = []
    for a in arrays:
        flat = a.reshape(-1)
        pad = (-flat.shape[0]) % 1024
        rows.append(jnp.pad(flat, (0, pad)).reshape(-1, 128))
    return jnp.concatenate(rows, axis=0)


def _unpack(packed, like):
    out, r = [], 0
    for a in like:
        n = math.prod(a.shape)
        nr = (n + 1023) // 1024 * 8
        out.append(packed[r:r + nr].reshape(-1)[:n].reshape(a.shape))
        r += nr
    return out


KIND = {"ab_in": "col", "ab_out": "row", "sb_in": "col", "sb_out": "row",
        "w1_0": "col", "w1_1": "col", "w2_0": "row", "w2_1": "row"}
GATHER_FIRST = ("ab_in", "ab_out")
GATHER_LATER = {"ffn0": ("w1_0", "w2_0"),
                "rest": ("sb_in", "sb_out", "w1_1", "w2_1")}


class _Exchange:
    def __init__(self, shards):
        x, y, c = _coords()
        self.where = jnp.stack([c, 2 * x + y]).astype(jnp.int32)
        block = (4 * x + 2 * y + c).astype(jnp.int32).reshape(1)
        self.full = {n: _place(s, KIND[n], block, f"place_{n}") for n, s in shards.items()}
        first = _all_gather([self.full[n] for n in GATHER_FIRST], [KIND[n] for n in GATHER_FIRST])
        self.full.update(zip(GATHER_FIRST, first))
        self.tokens = []
        self.gathers = {}
        self.scatters = {}
        self.landed = {}
        after = first[-1]
        for key, group in GATHER_LATER.items():
            to_chips, to_sibling = _gather_plans([KIND[n] for n in group])
            bufs = [self.full[n] for n in group]
            sems, bufs, after = _copies_start(f"gather_start_{key}", bufs, to_chips, 4 * len(group), after)
            self.tokens.append(after)
            self.gathers[key] = (group, sems, bufs, to_chips, to_sibling)

    def tie(self, small):
        for token in self.tokens:
            small = small + token[0:1, 0:1]
        self.tokens = []
        return small

    def started(self):
        return tuple(self.tokens)

    def weight(self, name):
        return self.full[name]

    def arrive(self, key, after):
        group, sems, bufs, to_chips, to_sibling = self.gathers[key]
        sems, bufs, token = _copies_wait_start(f"gather_pass_{key}", bufs, sems, after, to_chips, to_sibling,
                                               3 * len(group))
        self.tokens.append(token)
        self.gathers[key] = (group, sems, bufs, to_chips, to_sibling)

    def land(self, key, after):
        group, sems, bufs, _, to_sibling = self.gathers.pop(key)
        self.full.update(zip(group, _copies_wait(f"gather_done_{key}", bufs, sems, after, to_sibling)))

    def reduce(self, key, grads):
        names = list(grads)
        kinds = [KIND[n] for n in names]
        others = _exchange_siblings([grads[n] for n in names], kinds, f"grad_siblings_{key}")
        pairs = [_pair_sum(grads[n], o, k, self.where, f"pair_sum_{n}") for n, o, k in zip(names, others, kinds)]
        parts, lands = [p[0] for p in pairs], [p[1] for p in pairs]
        plan = _scatter_plan(len(names))
        sems, bufs, token = _copies_start(f"scatter_start_{key}", parts + lands, plan, 3 * len(names), parts[0])
        self.tokens.append(token)
        self.scatters[key] = (names, sems, bufs, plan)

    def finish(self, after):
        for key, (names, sems, bufs, plan) in self.scatters.items():
            bufs = _copies_wait(f"scatter_done_{key}", bufs, sems, after, plan)
            self.landed.update(zip(names, bufs[len(names):]))
        self.scatters = {}
        return self.landed


SMALL = ("norm_pre_mix", "norm_post_mix", "norm_pre_ffn", "norm_post_ffn", "sgu_ln_g", "sgu_ln_b", "sgu_w", "sgu_b")
ORDER = ("norm_pre_mix", "norm_post_mix", "norm_pre_ffn", "norm_post_ffn", "ab_w_in", "sgu_ln_g", "sgu_ln_b", "sgu_w",
         "sgu_b", "ab_w_out", "sb_w_in", "sb_w_out", "ffn_w1", "ffn_w2")


def kernel(x, norm_pre_mix, norm_post_mix, norm_pre_ffn, norm_post_ffn, ab_w_in, sgu_ln_g, sgu_ln_b, sgu_w, sgu_b, ab_w_out, sb_w_in, sb_w_out, ffn_w1, ffn_w2, loss_target, m_norm_pre_mix, m_norm_post_mix, m_norm_pre_ffn, m_norm_post_ffn, m_ab_w_in, m_sgu_ln_g, m_sgu_ln_b, m_sgu_w, m_sgu_b, m_ab_w_out, m_sb_w_in, m_sb_w_out, m_ffn_w1, m_ffn_w2, v_norm_pre_mix, v_norm_post_mix, v_norm_pre_ffn, v_norm_post_ffn, v_ab_w_in, v_sgu_ln_g, v_sgu_ln_b, v_sgu_w, v_sgu_b, v_ab_w_out, v_sb_w_in, v_sb_w_out, v_ffn_w1, v_ffn_w2):
    W = dict(norm_pre_mix=norm_pre_mix, norm_post_mix=norm_post_mix, norm_pre_ffn=norm_pre_ffn,
             norm_post_ffn=norm_post_ffn, ab_w_in=ab_w_in, sgu_ln_g=sgu_ln_g, sgu_ln_b=sgu_ln_b, sgu_w=sgu_w,
             sgu_b=sgu_b, ab_w_out=ab_w_out, sb_w_in=sb_w_in, sb_w_out=sb_w_out, ffn_w1=ffn_w1, ffn_w2=ffn_w2)
    M = dict(norm_pre_mix=m_norm_pre_mix, norm_post_mix=m_norm_post_mix, norm_pre_ffn=m_norm_pre_ffn,
             norm_post_ffn=m_norm_post_ffn, ab_w_in=m_ab_w_in, sgu_ln_g=m_sgu_ln_g, sgu_ln_b=m_sgu_ln_b,
             sgu_w=m_sgu_w, sgu_b=m_sgu_b, ab_w_out=m_ab_w_out, sb_w_in=m_sb_w_in, sb_w_out=m_sb_w_out,
             ffn_w1=m_ffn_w1, ffn_w2=m_ffn_w2)
    V = dict(norm_pre_mix=v_norm_pre_mix, norm_post_mix=v_norm_post_mix, norm_pre_ffn=v_norm_pre_ffn,
             norm_post_ffn=v_norm_post_ffn, ab_w_in=v_ab_w_in, sgu_ln_g=v_sgu_ln_g, sgu_ln_b=v_sgu_ln_b,
             sgu_w=v_sgu_w, sgu_b=v_sgu_b, ab_w_out=v_ab_w_out, sb_w_in=v_sb_w_in, sb_w_out=v_sb_w_out,
             ffn_w1=v_ffn_w1, ffn_w2=v_ffn_w2)

    shards = {"ab_in": ab_w_in[0], "ab_out": ab_w_out[0], "w1_0": ffn_w1[0], "w2_0": ffn_w2[0],
              "sb_in": sb_w_in[0], "sb_out": sb_w_out[0], "w1_1": ffn_w1[1], "w2_1": ffn_w2[1]}
    comm = _Exchange(shards)
    norms = (norm_pre_mix, norm_post_mix, norm_pre_ffn, norm_post_ffn)
    sgu = (sgu_ln_g, sgu_ln_b, sgu_w[0], sgu_b[0])
    loss, dx, small = _local_step(x[0], loss_target[0], norms, sgu, comm)
    loss = lax.psum(loss, MESH_AXES)
    landed = comm.finish(after=dx)

    out = {}
    for name, layers in (("ab_w_in", ["ab_in"]), ("ab_w_out", ["ab_out"]), ("sb_w_in", ["sb_in"]),
                         ("sb_w_out", ["sb_out"]), ("ffn_w1", ["w1_0", "w1_1"]), ("ffn_w2", ["w2_0", "w2_1"])):
        out[name] = _adamw(W[name], [landed[n] for n in layers], M[name], V[name], f"adamw_{name}")

    small_g = [small["pre_mix"], small["post_mix"], small["pre_ffn"], small["post_ffn"], small["ln_g"],
               small["ln_b"], small["w_s"][None], small["b_s"][None]]
    g_small = _all_reduce_small(_pack(small_g))
    res = _adamw(_pack([W[n] for n in SMALL])[None], [g_small[None]], _pack([M[n] for n in SMALL])[None],
                 _pack([V[n] for n in SMALL])[None], "adamw_small")
    like = [W[n] for n in SMALL]
    for n, *vals in zip(SMALL, *[_unpack(r[0], like) for r in res]):
        out[n] = vals

    return (loss, dx[None], *[out[n][0] for n in ORDER], *[out[n][1] for n in ORDER],
            *[out[n][2] for n in ORDER], *[out[n][3] for n in ORDER])
```

```python
import functools
import math

import jax
import jax.numpy as jnp
from jax import lax
from jax.experimental import pallas as pl
from jax.experimental.pallas import tpu as pltpu

F32 = jnp.float32
BF16 = jnp.bfloat16

HEAD_DIM = 128
CHUNK = 128
ATT_BLOCK = 128
DILATED_PAIRS = ((128, 1), (512, 4), (2048, 16))
RMS_EPS = 1e-6
LN_EPS = 1e-5
ADAM_LR = 0.001
ADAM_B1 = 0.9
ADAM_B2 = 0.999
ADAM_EPS = 1e-08
ADAM_WD = 0.01
ADAM_STEP = 10
N_DEV = 8
MESH_AXES = ("x", "y", "c")
MASKED = -1e30

V7X_VMEM_BYTES = 64 * 1024 * 1024
VMEM_LIMIT = V7X_VMEM_BYTES - 8 * 1024 * 1024

NN = (((1,), (0,)), ((), ()))
NT = (((1,), (1,)), ((), ()))
TN = (((0,), (0,)), ((), ()))


def _params(*sem):
    return pltpu.CompilerParams(dimension_semantics=sem, vmem_limit_bytes=VMEM_LIMIT)


def _dot(a, b, dims=NN):
    return lax.dot_general(a, b, dims, preferred_element_type=F32)


def _tile(n, preferred):
    if n <= preferred:
        return n
    t = preferred - preferred % 128
    while n % t:
        t -= 128
    assert t > 0, (n, preferred)
    return t


def _matmul(a, b, *, mode, name, out_dtype=F32, tm=1024, tn=512, tk=2048, epi=None, extras=(), after=()):
    if mode == "nn":
        (M, K), N = a.shape, b.shape[1]
    elif mode == "nt":
        (M, K), N = a.shape, b.shape[0]
    else:
        (K, M), N = a.shape, b.shape[1]
    tm, tn, tk = _tile(M, tm), _tile(N, tn), _tile(K, tk)
    nk = K // tk
    if mode == "tn":
        a_spec = pl.BlockSpec((tk, tm), lambda i, j, k: (k, i))
    else:
        a_spec = pl.BlockSpec((tm, tk), lambda i, j, k: (i, k))
    if mode == "nt":
        b_spec = pl.BlockSpec((tn, tk), lambda i, j, k: (j, k))
    else:
        b_spec = pl.BlockSpec((tk, tn), lambda i, j, k: (k, j))
    o_spec = pl.BlockSpec((tm, tn), lambda i, j, k: (i, j))
    dims = {"nn": NN, "nt": NT, "tn": TN}[mode]
    n_extra = len(extras)
    n_in = n_extra + len(after)

    def finish(acc, refs):
        j = pl.program_id(1)
        if epi is None:
            return acc
        return epi(acc, j, *[r[...] for r in refs])

    if nk == 1:
        def body(a_ref, b_ref, *rest):
            o_ref = rest[n_in]
            acc = _dot(a_ref[...], b_ref[...], dims)
            o_ref[...] = finish(acc, rest[:n_extra]).astype(o_ref.dtype)
        scratch = []
    else:
        def body(a_ref, b_ref, *rest):
            o_ref, acc_ref = rest[n_in], rest[n_in + 1]
            k = pl.program_id(2)

            @pl.when(k == 0)
            def _():
                acc_ref[...] = jnp.zeros_like(acc_ref)

            acc_ref[...] += _dot(a_ref[...], b_ref[...], dims)

            @pl.when(k == nk - 1)
            def _():
                o_ref[...] = finish(acc_ref[...], rest[:n_extra]).astype(o_ref.dtype)
        scratch = [pltpu.VMEM((tm, tn), F32)]

    return pl.pallas_call(
        body,
        name=name,
        grid=(M // tm, N // tn, nk),
        in_specs=[a_spec, b_spec] + [o_spec] * n_extra + [ANY] * len(after),
        out_specs=o_spec,
        out_shape=jax.ShapeDtypeStruct((M, N), out_dtype),
        scratch_shapes=scratch,
        compiler_params=_params("parallel", "parallel", "arbitrary"),
    )(a, b, *extras, *after)


ROWS = 256


def _rms(x):
    return lax.rsqrt(jnp.mean(x * x, axis=-1, keepdims=True) + RMS_EPS)


def _prenorm(x, g, name):
    T, D = x.shape

    def body(x_ref, g_ref, h_ref):
        xv = x_ref[...]
        h_ref[...] = (xv * _rms(xv) * g_ref[...]).astype(BF16)

    row = pl.BlockSpec((ROWS, D), lambda i: (i, 0))
    vec = pl.BlockSpec((1, D), lambda i: (0, 0))
    return pl.pallas_call(
        body, name=name, grid=(T // ROWS,), in_specs=[row, vec], out_specs=row,
        out_shape=jax.ShapeDtypeStruct((T, D), BF16), compiler_params=_params("parallel"),
    )(x, g)


def _postnorm_prenorm(x, y, g_post, g_pre, name):
    T, D = x.shape

    def body(x_ref, y_ref, gp_ref, gn_ref, xo_ref, h_ref):
        yv = y_ref[...]
        xn = x_ref[...] + yv * _rms(yv) * gp_ref[...]
        xo_ref[...] = xn
        h_ref[...] = (xn * _rms(xn) * gn_ref[...]).astype(BF16)

    row = pl.BlockSpec((ROWS, D), lambda i: (i, 0))
    vec = pl.BlockSpec((1, D), lambda i: (0, 0))
    return pl.pallas_call(
        body, name=name, grid=(T // ROWS,), in_specs=[row, row, vec, vec], out_specs=[row, row],
        out_shape=[jax.ShapeDtypeStruct((T, D), F32), jax.ShapeDtypeStruct((T, D), BF16)],
        compiler_params=_params("parallel"),
    )(x, y, g_post, g_pre)


def _postnorm_loss(x, y, g_post, target, name):
    T, D = x.shape

    def body(x_ref, y_ref, gp_ref, t_ref, loss_ref, dx_ref):
        @pl.when(pl.program_id(0) == 0)
        def _():
            loss_ref[...] = jnp.zeros_like(loss_ref)

        yv = y_ref[...]
        err = x_ref[...] + yv * _rms(yv) * gp_ref[...] - t_ref[...]
        dx_ref[...] = err * (1.0 / D)
        loss_ref[...] += 0.5 * jnp.sum(jnp.sum(err * err, axis=-1, keepdims=True) * (1.0 / D))

    row = pl.BlockSpec((ROWS, D), lambda i: (i, 0))
    vec = pl.BlockSpec((1, D), lambda i: (0, 0))
    acc = pl.BlockSpec((8, 128), lambda i: (0, 0))
    return pl.pallas_call(
        body, name=name, grid=(T // ROWS,), in_specs=[row, row, vec, row], out_specs=[acc, row],
        out_shape=[jax.ShapeDtypeStruct((8, 128), F32), jax.ShapeDtypeStruct((T, D), F32)],
        compiler_params=_params("arbitrary"),
    )(x, y, g_post, target)


def _postnorm_bwd(dx, y, g_post, name):
    T, D = dx.shape

    def body(dx_ref, y_ref, g_ref, dy_ref, dg_ref):
        @pl.when(pl.program_id(0) == 0)
        def _():
            dg_ref[...] = jnp.zeros_like(dg_ref)

        yv, dn = y_ref[...], dx_ref[...]
        r = _rms(yv)
        yh = yv * r
        gd = dn * g_ref[...]
        dy_ref[...] = (r * (gd - yh * jnp.mean(yh * gd, axis=-1, keepdims=True))).astype(BF16)
        dg_ref[...] += jnp.sum(dn * yh, axis=0, keepdims=True)

    row = pl.BlockSpec((ROWS, D), lambda i: (i, 0))
    vec = pl.BlockSpec((1, D), lambda i: (0, 0))
    return pl.pallas_call(
        body, name=name, grid=(T // ROWS,), in_specs=[row, row, vec], out_specs=[row, vec],
        out_shape=[jax.ShapeDtypeStruct((T, D), BF16), jax.ShapeDtypeStruct((1, D), F32)],
        compiler_params=_params("arbitrary"),
    )(dx, y, g_post)


def _prenorm_bwd(dx_out, dh, x, g_pre, name):
    T, D = x.shape

    def body(dxo_ref, dh_ref, x_ref, g_ref, dx_ref, dg_ref):
        @pl.when(pl.program_id(0) == 0)
        def _():
            dg_ref[...] = jnp.zeros_like(dg_ref)

        xv, dhv = x_ref[...], dh_ref[...]
        r = _rms(xv)
        xh = xv * r
        gd = dhv * g_ref[...]
        dx_ref[...] = dxo_ref[...] + r * (gd - xh * jnp.mean(xh * gd, axis=-1, keepdims=True))
        dg_ref[...] += jnp.sum(dhv * xh, axis=0, keepdims=True)

    row = pl.BlockSpec((ROWS, D), lambda i: (i, 0))
    vec = pl.BlockSpec((1, D), lambda i: (0, 0))
    return pl.pallas_call(
        body, name=name, grid=(T // ROWS,), in_specs=[row, row, row, vec], out_specs=[row, vec],
        out_shape=[jax.ShapeDtypeStruct((T, D), F32), jax.ShapeDtypeStruct((1, D), F32)],
        compiler_params=_params("arbitrary"),
    )(dx_out, dh, x, g_pre)


_INV_SQRT2 = 1.0 / math.sqrt(2.0)
_INV_SQRT2PI = 1.0 / math.sqrt(2.0 * math.pi)


def _gelu(x):
    return 0.5 * x * (1.0 + lax.erf(x * _INV_SQRT2))


def _gelu_grad(x):
    return 0.5 * (1.0 + lax.erf(x * _INV_SQRT2)) + x * jnp.exp(-0.5 * x * x) * _INV_SQRT2PI


def _layernorm_stats(x):
    mu = jnp.mean(x, axis=-1, keepdims=True)
    xc = x - mu
    rstd = lax.rsqrt(jnp.mean(xc * xc, axis=-1, keepdims=True) + LN_EPS)
    return xc * rstd, rstd


def _tril_mask():
    i = lax.broadcasted_iota(jnp.int32, (CHUNK, CHUNK), 0)
    j = lax.broadcasted_iota(jnp.int32, (CHUNK, CHUNK), 1)
    return j <= i


SGU_ROWS = 512


def _sgu_fwd(z, ln_g, ln_b, w_s, b_t, name):
    T = z.shape[0]
    A = ln_g.shape[1]
    G = A // 128
    rows = min(SGU_ROWS, T)

    def body(u_ref, v_ref, g_ref, b_ref, w_ref, bt_ref, o_ref):
        mask = _tril_mask()
        for c in range(rows // CHUNK):
            rs = pl.ds(c * CHUNK, CHUNK)
            xh, _ = _layernorm_stats(_gelu(v_ref[rs, :]))
            vn = (xh * g_ref[...] + b_ref[...]).astype(BF16)
            for g in range(G):
                cs = pl.ds(g * 128, 128)
                w = jnp.where(mask, w_ref[g], 0.0).astype(BF16)
                mixed = _dot(w, vn[:, g * 128:(g + 1) * 128]) + bt_ref[:, g:g + 1]
                o_ref[rs, cs] = (_gelu(u_ref[rs, cs]) * mixed).astype(BF16)

    return pl.pallas_call(
        body, name=name, grid=(T // rows,),
        in_specs=[
            pl.BlockSpec((rows, A), lambda i: (i, 0)),
            pl.BlockSpec((rows, A), lambda i: (i, 1)),
            pl.BlockSpec((1, A), lambda i: (0, 0)),
            pl.BlockSpec((1, A), lambda i: (0, 0)),
            pl.BlockSpec((G, CHUNK, CHUNK), lambda i: (0, 0, 0)),
            pl.BlockSpec((CHUNK, G), lambda i: (0, 0)),
        ],
        out_specs=pl.BlockSpec((rows, A), lambda i: (i, 0)),
        out_shape=jax.ShapeDtypeStruct((T, A), BF16),
        compiler_params=_params("parallel"),
    )(z, z, ln_g, ln_b, w_s, b_t)


def _sgu_bwd(z, dcat, ln_g, ln_b, w_s, b_t, name):
    T = z.shape[0]
    A = ln_g.shape[1]
    G = A // 128
    rows = min(SGU_ROWS, T)

    def body(u_ref, v_ref, da_ref, g_ref, b_ref, w_ref, bt_ref, dz_ref, dg_ref, db_ref, dw_ref, dbt_ref, dvn_ref):
        @pl.when(pl.program_id(0) == 0)
        def _():
            dg_ref[...] = jnp.zeros_like(dg_ref)
            db_ref[...] = jnp.zeros_like(db_ref)
            dw_ref[...] = jnp.zeros_like(dw_ref)
            dbt_ref[...] = jnp.zeros_like(dbt_ref)

        mask = _tril_mask()
        for c in range(rows // CHUNK):
            rs = pl.ds(c * CHUNK, CHUNK)
            vv = v_ref[rs, :]
            gv = _gelu(vv)
            xh, rstd = _layernorm_stats(gv)
            vn = (xh * g_ref[...] + b_ref[...]).astype(BF16)
            for g in range(G):
                cs = pl.ds(g * 128, 128)
                w = jnp.where(mask, w_ref[g], 0.0).astype(BF16)
                vg = vn[:, g * 128:(g + 1) * 128]
                mixed = _dot(w, vg) + bt_ref[:, g:g + 1]
                uu = u_ref[rs, cs]
                da = da_ref[rs, cs]
                dz_ref[rs, cs] = (da * mixed * _gelu_grad(uu)).astype(BF16)
                dm = da * _gelu(uu)
                dmb = dm.astype(BF16)
                dbt_ref[:, g:g + 1] += jnp.sum(dm, axis=1, keepdims=True)
                dw_ref[g] += jnp.where(mask, _dot(dmb, vg, NT), 0.0)
                dvn_ref[:, cs] = _dot(w, dmb, TN)
            dvn = dvn_ref[...]
            dg_ref[...] += jnp.sum(dvn * xh, axis=0, keepdims=True)
            db_ref[...] += jnp.sum(dvn, axis=0, keepdims=True)
            dxh = dvn * g_ref[...]
            dgv = rstd * (dxh - jnp.mean(dxh, axis=-1, keepdims=True)
                          - xh * jnp.mean(dxh * xh, axis=-1, keepdims=True))
            dz_ref[rs, pl.ds(A, A)] = (dgv * _gelu_grad(vv)).astype(BF16)

    vec = pl.BlockSpec((1, A), lambda i: (0, 0))
    wsp = pl.BlockSpec((G, CHUNK, CHUNK), lambda i: (0, 0, 0))
    bsp = pl.BlockSpec((CHUNK, G), lambda i: (0, 0))
    return pl.pallas_call(
        body, name=name, grid=(T // rows,),
        in_specs=[
            pl.BlockSpec((rows, A), lambda i: (i, 0)),
            pl.BlockSpec((rows, A), lambda i: (i, 1)),
            pl.BlockSpec((rows, A), lambda i: (i, 0)),
            vec, vec, wsp, bsp,
        ],
        out_specs=[pl.BlockSpec((rows, 2 * A), lambda i: (i, 0)), vec, vec, wsp, bsp],
        out_shape=[
            jax.ShapeDtypeStruct((T, 2 * A), BF16),
            jax.ShapeDtypeStruct((1, A), F32),
            jax.ShapeDtypeStruct((1, A), F32),
            jax.ShapeDtypeStruct((G, CHUNK, CHUNK), F32),
            jax.ShapeDtypeStruct((CHUNK, G), F32),
        ],
        scratch_shapes=[pltpu.VMEM((CHUNK, A), F32)],
        compiler_params=_params("arbitrary"),
    )(z, z, dcat, ln_g, ln_b, w_s, b_t)


def _alibi_row(B, d):
    H = B // HEAD_DIM
    slopes = [d * 2.0 ** (-8.0 * (h + 1.0) / H) for h in range(H)]
    return jnp.repeat(jnp.asarray(slopes, F32), HEAD_DIM)[None, :]


def _dil_scores(q, k, slope_d, valid, dist):
    s = _dot(q, k, NT) - slope_d * dist
    return jnp.where(valid, s, MASKED)


def _dil_layout(T, B, d):
    H = B // HEAD_DIM
    hp = H if d == 1 else 1
    return hp, H // hp, T // (d * ATT_BLOCK)


def _dil_rows(ref, r, d, cs):
    return ref[pl.ds(r, ATT_BLOCK, stride=d), cs] if d > 1 else ref[:, cs]


def _dil_put(ref, r, d, cs, val):
    if d > 1:
        ref[pl.ds(r, ATT_BLOCK, stride=d), cs] = val
    else:
        ref[:, cs] = val


def _dilated_fwd(z, d, B, name):
    T = z.shape[0]
    H = B // HEAD_DIM
    hp, groups, nb = _dil_layout(T, B, d)
    cw = hp * HEAD_DIM
    scale = HEAD_DIM ** -0.5
    blk = ATT_BLOCK

    def body(q_ref, kp_ref, kc_ref, vp_ref, vc_ref, sl_ref, o_ref, l_ref):
        n = pl.program_id(1)
        qi = lax.broadcasted_iota(jnp.int32, (blk, 2 * blk), 0)
        kj = lax.broadcasted_iota(jnp.int32, (blk, 2 * blk), 1)
        dist = qi + blk - kj
        valid = (dist >= 0) & (dist <= blk) & ((kj >= blk) | (n > 0))
        distf = dist.astype(F32)
        for r in range(d):
            for hh in range(hp):
                cs = pl.ds(hh * HEAD_DIM, HEAD_DIM)
                q = (_dil_rows(q_ref, r, d, cs) * scale).astype(BF16)
                k = jnp.concatenate([_dil_rows(kp_ref, r, d, cs), _dil_rows(kc_ref, r, d, cs)], axis=0).astype(BF16)
                v = jnp.concatenate([_dil_rows(vp_ref, r, d, cs), _dil_rows(vc_ref, r, d, cs)], axis=0).astype(BF16)
                s = _dil_scores(q, k, sl_ref[:, cs][:, :1], valid, distf)
                m = jnp.max(s, axis=-1, keepdims=True)
                p = jnp.exp(s - m)
                den = jnp.sum(p, axis=-1, keepdims=True)
                _dil_put(o_ref, r, d, cs, _dot(p.astype(BF16), v) / den)
                _dil_put(l_ref, r, d, cs, jnp.broadcast_to(m + jnp.log(den), (blk, HEAD_DIM)))

    def col(unit):
        return lambda g, n: (n, unit * groups + g)

    def col_prev(unit):
        return lambda g, n: (jnp.maximum(n - 1, 0), unit * groups + g)

    bs = (d * blk, cw)
    out = pl.BlockSpec(bs, lambda g, n: (n, g))
    return pl.pallas_call(
        body, name=name, grid=(groups, nb),
        in_specs=[pl.BlockSpec(bs, col(2)), pl.BlockSpec(bs, col_prev(3)), pl.BlockSpec(bs, col(3)),
                  pl.BlockSpec(bs, col_prev(4)), pl.BlockSpec(bs, col(4)), pl.BlockSpec((1, cw), lambda g, n: (0, g))],
        out_specs=[out, out],
        out_shape=[jax.ShapeDtypeStruct((T, B), F32)] * 2,
        compiler_params=_params("parallel", "parallel"),
    )(z, z, z, z, z, _alibi_row(B, d))


def _dilated_merge(os_, ls_, name):
    T, B = os_[0].shape

    def body(o1, o2, o3, l1, l2, l3, ob_ref, of_ref, lt_ref):
        a, b, c = l1[...], l2[...], l3[...]
        m = jnp.maximum(jnp.maximum(a, b), c)
        ea, eb, ec = jnp.exp(a - m), jnp.exp(b - m), jnp.exp(c - m)
        tot = ea + eb + ec
        o = (ea * o1[...] + eb * o2[...] + ec * o3[...]) / tot
        of_ref[...] = o
        ob_ref[...] = o.astype(BF16)
        lt_ref[...] = m + jnp.log(tot)

    row = pl.BlockSpec((ROWS, B), lambda i: (i, 0))
    return pl.pallas_call(
        body, name=name, grid=(T // ROWS,), in_specs=[row] * 6, out_specs=[row] * 3,
        out_shape=[jax.ShapeDtypeStruct((T, B), BF16), jax.ShapeDtypeStruct((T, B), F32),
                   jax.ShapeDtypeStruct((T, B), F32)],
        compiler_params=_params("parallel"),
    )(*os_, *ls_)


def _dilated_delta(dcat, o, name):
    T, B = o.shape
    H = B // HEAD_DIM

    def body(do_ref, o_ref, d_ref):
        for h in range(H):
            cs = pl.ds(h * HEAD_DIM, HEAD_DIM)
            dsum = jnp.sum(do_ref[:, cs] * o_ref[:, cs], axis=-1, keepdims=True)
            d_ref[:, cs] = jnp.broadcast_to(dsum, (ROWS, HEAD_DIM))

    row = pl.BlockSpec((ROWS, B), lambda i: (i, 0))
    return pl.pallas_call(
        body, name=name, grid=(T // ROWS,),
        in_specs=[pl.BlockSpec((ROWS, B), lambda i: (i, 1)), row], out_specs=row,
        out_shape=jax.ShapeDtypeStruct((T, B), F32), compiler_params=_params("parallel"),
    )(dcat, o)


def _dilated_dq(z, dcat, lse, delta, d, B, name):
    T = z.shape[0]
    hp, groups, nb = _dil_layout(T, B, d)
    cw = hp * HEAD_DIM
    scale = HEAD_DIM ** -0.5
    blk = ATT_BLOCK

    def body(q_ref, kp_ref, kc_ref, vp_ref, vc_ref, do_ref, l_ref, dl_ref, sl_ref, dq_ref):
        n = pl.program_id(1)
        qi = lax.broadcasted_iota(jnp.int32, (blk, 2 * blk), 0)
        kj = lax.broadcasted_iota(jnp.int32, (blk, 2 * blk), 1)
        dist = qi + blk - kj
        valid = (dist >= 0) & (dist <= blk) & ((kj >= blk) | (n > 0))
        distf = dist.astype(F32)
        for r in range(d):
            for hh in range(hp):
                cs = pl.ds(hh * HEAD_DIM, HEAD_DIM)
                q = (_dil_rows(q_ref, r, d, cs) * scale).astype(BF16)
                k = jnp.concatenate([_dil_rows(kp_ref, r, d, cs), _dil_rows(kc_ref, r, d, cs)], axis=0).astype(BF16)
                v = jnp.concatenate([_dil_rows(vp_ref, r, d, cs), _dil_rows(vc_ref, r, d, cs)], axis=0).astype(BF16)
                s = _dil_scores(q, k, sl_ref[:, cs][:, :1], valid, distf)
                p = jnp.exp(s - _dil_rows(l_ref, r, d, cs)[:, :1])
                dp = _dot(_dil_rows(do_ref, r, d, cs).astype(BF16), v, NT)
                ds = p * (dp - _dil_rows(dl_ref, r, d, cs)[:, :1])
                _dil_put(dq_ref, r, d, cs, _dot(ds.astype(BF16), k))

    def col(unit):
        return lambda g, n: (n, unit * groups + g)

    def col_prev(unit):
        return lambda g, n: (jnp.maximum(n - 1, 0), unit * groups + g)

    bs = (d * blk, cw)
    out = pl.BlockSpec(bs, lambda g, n: (n, g))
    return pl.pallas_call(
        body, name=name, grid=(groups, nb),
        in_specs=[pl.BlockSpec(bs, col(2)), pl.BlockSpec(bs, col_prev(3)), pl.BlockSpec(bs, col(3)),
                  pl.BlockSpec(bs, col_prev(4)), pl.BlockSpec(bs, col(4)), pl.BlockSpec(bs, col(1)), out, out,
                  pl.BlockSpec((1, cw), lambda g, n: (0, g))],
        out_specs=out,
        out_shape=jax.ShapeDtypeStruct((T, B), F32),
        compiler_params=_params("parallel", "parallel"),
    )(z, z, z, z, z, dcat, lse, delta, _alibi_row(B, d))


def _dilated_dkv(z, dcat, lse, delta, d, B, name):
    T = z.shape[0]
    hp, groups, nb = _dil_layout(T, B, d)
    cw = hp * HEAD_DIM
    scale = HEAD_DIM ** -0.5
    blk = ATT_BLOCK

    def body(k_ref, v_ref, qa_ref, qb_ref, doa_ref, dob_ref, la_ref, lb_ref, da_ref, db_ref, sl_ref, dk_ref, dv_ref):
        m = pl.program_id(1)
        qi = lax.broadcasted_iota(jnp.int32, (2 * blk, blk), 0)
        kj = lax.broadcasted_iota(jnp.int32, (2 * blk, blk), 1)
        dist = qi - kj
        valid = (dist >= 0) & (dist <= blk) & ((qi < blk) | (m + 1 < nb))
        distf = dist.astype(F32)
        for r in range(d):
            for hh in range(hp):
                cs = pl.ds(hh * HEAD_DIM, HEAD_DIM)

                def both(a_ref, b_ref):
                    return jnp.concatenate([_dil_rows(a_ref, r, d, cs), _dil_rows(b_ref, r, d, cs)], axis=0)

                q = (both(qa_ref, qb_ref) * scale).astype(BF16)
                do = both(doa_ref, dob_ref).astype(BF16)
                k = _dil_rows(k_ref, r, d, cs).astype(BF16)
                v = _dil_rows(v_ref, r, d, cs).astype(BF16)
                s = _dil_scores(q, k, sl_ref[:, cs][:, :1], valid, distf)
                p = jnp.exp(jnp.where(valid, s - both(la_ref, lb_ref)[:, :1], MASKED))
                _dil_put(dv_ref, r, d, cs, _dot(p.astype(BF16), do, TN))
                ds = p * (_dot(do, v, NT) - both(da_ref, db_ref)[:, :1])
                _dil_put(dk_ref, r, d, cs, _dot(ds.astype(BF16), q, TN))

    def col(unit):
        return lambda g, m: (m, unit * groups + g)

    def nxt(unit):
        return lambda g, m: (jnp.minimum(m + 1, nb - 1), unit * groups + g)

    bs = (d * blk, cw)
    out = pl.BlockSpec(bs, lambda g, m: (m, g))
    return pl.pallas_call(
        body, name=name, grid=(groups, nb),
        in_specs=[pl.BlockSpec(bs, col(3)), pl.BlockSpec(bs, col(4)),
                  pl.BlockSpec(bs, col(2)), pl.BlockSpec(bs, nxt(2)),
                  pl.BlockSpec(bs, col(1)), pl.BlockSpec(bs, nxt(1)),
                  out, pl.BlockSpec(bs, nxt(0)), out, pl.BlockSpec(bs, nxt(0)),
                  pl.BlockSpec((1, cw), lambda g, m: (0, g))],
        out_specs=[out, out],
        out_shape=[jax.ShapeDtypeStruct((T, B), F32)] * 2,
        compiler_params=_params("parallel", "parallel"),
    )(z, z, z, z, dcat, dcat, lse, lse, delta, delta, _alibi_row(B, d))


def _dilated_combine(dqs, dks, dvs, name):
    T, B = dqs[0].shape
    scale = HEAD_DIM ** -0.5

    def body(q1, q2, q3, k1, k2, k3, v1, v2, v3, o_ref):
        o_ref[:, pl.ds(0, B)] = ((q1[...] + q2[...] + q3[...]) * scale).astype(BF16)
        o_ref[:, pl.ds(B, B)] = (k1[...] + k2[...] + k3[...]).astype(BF16)
        o_ref[:, pl.ds(2 * B, B)] = (v1[...] + v2[...] + v3[...]).astype(BF16)

    row = pl.BlockSpec((ROWS, B), lambda i: (i, 0))
    return pl.pallas_call(
        body, name=name, grid=(T // ROWS,), in_specs=[row] * 9,
        out_specs=pl.BlockSpec((ROWS, 3 * B), lambda i: (i, 0)),
        out_shape=jax.ShapeDtypeStruct((T, 3 * B), BF16), compiler_params=_params("parallel"),
    )(*dqs, *dks, *dvs)


SB_QUERY_ROWS = 512
SB_KEYS = 2 * ATT_BLOCK


def _tri_and_ones(pred):
    rows = lax.broadcasted_iota(jnp.int32, (2 * ATT_BLOCK, 2 * ATT_BLOCK), 0) % ATT_BLOCK
    cols = lax.broadcasted_iota(jnp.int32, (2 * ATT_BLOCK, 2 * ATT_BLOCK), 1)
    return ((cols >= ATT_BLOCK) | pred(rows, cols)).astype(BF16)


def _running(x, tri):
    hi = x.astype(BF16)
    lo = (x - hi.astype(F32)).astype(BF16)
    return _dot(jnp.concatenate([hi, lo], axis=1), tri)


def _sb_mask(query_rows, s):
    rows = lax.broadcasted_iota(jnp.int32, (query_rows, SB_KEYS), 0)
    cols = lax.broadcasted_iota(jnp.int32, (query_rows, SB_KEYS), 1)
    return cols + s * SB_KEYS < rows


def _log_sigmoids(z):
    e = jnp.exp(-jnp.abs(z))
    sp = jnp.log(1.0 + e)
    return jnp.minimum(z, 0.0) - sp, jnp.minimum(-z, 0.0) - sp, e


def _sb_fwd(qkv, W, name):
    T = qkv.shape[0]
    H = W // HEAD_DIM
    blk = ATT_BLOCK
    qb = min(SB_QUERY_ROWS, T)
    per = qb // SB_KEYS

    def body(q_ref, k_ref, v_ref, o_ref, lt_ref, acc_ref):
        i = pl.program_id(1)
        q = q_ref[...]
        tri = _tri_and_ones(lambda r, c: r > c)
        lt_ref[...] = jnp.zeros_like(lt_ref)
        acc_ref[...] = jnp.zeros_like(acc_ref)

        def tile(j, mask):
            ks = pl.ds(pl.multiple_of(j * SB_KEYS, SB_KEYS), SB_KEYS)
            z = _dot(q, k_ref[ks, :], NT)
            ls, lm, _ = _log_sigmoids(z)
            if mask is not None:
                lm = jnp.where(mask, lm, 0.0)
            later = lt_ref[...]
            second = _running(lm[:, blk:], tri)
            first = _running(lm[:, :blk], tri)
            after_first = later + second[:, blk:]
            a = jnp.exp(ls + jnp.concatenate([first[:, :blk] + after_first, second[:, :blk] + later], axis=1))
            if mask is not None:
                a = jnp.where(mask, a, 0.0)
            acc_ref[...] += _dot(a.astype(BF16), v_ref[ks, :])
            lt_ref[...] = after_first + first[:, blk:]

        for s in reversed(range(per)):
            tile(i * per + s, _sb_mask(qb, s))

        def step(jj, _):
            tile(i * per - 1 - jj, None)
            return 0

        lax.fori_loop(0, i * per, step, 0)
        o_ref[...] = acc_ref[...].astype(BF16)

    qs = pl.BlockSpec((qb, HEAD_DIM), lambda h, i: (i, h))
    return pl.pallas_call(
        body, name=name, grid=(H, T // qb),
        in_specs=[qs, pl.BlockSpec((T, HEAD_DIM), lambda h, i: (0, H + h)),
                  pl.BlockSpec((T, HEAD_DIM), lambda h, i: (0, 2 * H + h))],
        out_specs=[qs, qs],
        out_shape=[jax.ShapeDtypeStruct((T, W), BF16), jax.ShapeDtypeStruct((T, W), F32)],
        scratch_shapes=[pltpu.VMEM((qb, HEAD_DIM), F32)],
        compiler_params=_params("parallel", "arbitrary"),
    )(qkv, qkv, qkv)


def _sb_bwd(qkv, do, ltot, W, name):
    T = qkv.shape[0]
    H = W // HEAD_DIM
    blk = ATT_BLOCK
    nkb = T // SB_KEYS
    qb = min(SB_QUERY_ROWS, T)
    per = qb // SB_KEYS

    def body(q_ref, k_ref, v_ref, do_ref, lt_ref, dq_ref, dkt_ref, dvt_ref, qt_ref, dot_ref, plm_ref, pg_ref):
        i = pl.program_id(1)

        @pl.when(i == 0)
        def _():
            dkt_ref[...] = jnp.zeros_like(dkt_ref)
            dvt_ref[...] = jnp.zeros_like(dvt_ref)

        q = q_ref[...]
        do = do_ref[...]
        qt_ref[...] = q.astype(F32).T.astype(BF16)
        dot_ref[...] = do.astype(F32).T.astype(BF16)
        upto = _tri_and_ones(lambda r, c: r <= c)
        before = _tri_and_ones(lambda r, c: r < c)
        plm_ref[...] = jnp.zeros_like(plm_ref)
        pg_ref[...] = jnp.zeros_like(pg_ref)
        dq_ref[...] = jnp.zeros_like(dq_ref)

        def tile(j, mask):
            ks = pl.ds(pl.multiple_of(j * SB_KEYS, SB_KEYS), SB_KEYS)
            k = k_ref[ks, :]
            v = v_ref[ks, :]
            z = _dot(q, k, NT)
            ls, lm, e = _log_sigmoids(z)
            if mask is not None:
                lm = jnp.where(mask, lm, 0.0)
            earlier = plm_ref[...]
            first = _running(lm[:, :blk], upto)
            second = _running(lm[:, blk:], upto)
            upto_first = earlier + first[:, blk:]
            seen = jnp.concatenate([first[:, :blk] + earlier, second[:, :blk] + upto_first], axis=1)
            ltot = lt_ref[...]
            a = jnp.exp(ls + (jnp.concatenate([ltot, ltot], axis=1) - seen))
            if mask is not None:
                a = jnp.where(mask, a, 0.0)
            g = a * _dot(do, v, NT)
            g_earlier = pg_ref[...]
            g_first = _running(g[:, :blk], before)
            g_second = _running(g[:, blk:], before)
            g_upto_first = g_earlier + g_first[:, blk:]
            gsum = jnp.concatenate([g_first[:, :blk] + g_earlier, g_second[:, :blk] + g_upto_first], axis=1)
            r = 1.0 / (1.0 + e)
            pos = z >= 0.0
            sig = jnp.where(pos, r, e * r)
            nsig = jnp.where(pos, e * r, r)
            dz = g * nsig - gsum * sig
            if mask is not None:
                dz = jnp.where(mask, dz, 0.0)
            dzb = dz.astype(BF16)
            dkt_ref[j] += _dot(qt_ref[...], dzb)
            dvt_ref[j] += _dot(dot_ref[...], a.astype(BF16))
            dq_ref[...] += _dot(dzb, k)
            plm_ref[...] = upto_first + second[:, blk:]
            pg_ref[...] = g_upto_first + g_second[:, blk:]

        def step(j, _):
            tile(j, None)
            return 0

        lax.fori_loop(0, i * per, step, 0)
        for s in range(per):
            tile(i * per + s, _sb_mask(qb, s))

    qs = pl.BlockSpec((qb, HEAD_DIM), lambda h, i: (i, h))
    res = pl.BlockSpec((None, nkb, HEAD_DIM, SB_KEYS), lambda h, i: (h, 0, 0, 0))
    return pl.pallas_call(
        body, name=name, grid=(H, T // qb),
        in_specs=[qs, pl.BlockSpec((T, HEAD_DIM), lambda h, i: (0, H + h)),
                  pl.BlockSpec((T, HEAD_DIM), lambda h, i: (0, 2 * H + h)), qs, qs],
        out_specs=[qs, res, res],
        out_shape=[jax.ShapeDtypeStruct((T, W), F32)] + [jax.ShapeDtypeStruct((H, nkb, HEAD_DIM, SB_KEYS), F32)] * 2,
        scratch_shapes=[pltpu.VMEM((HEAD_DIM, qb), BF16), pltpu.VMEM((HEAD_DIM, qb), BF16),
                        pltpu.VMEM((qb, HEAD_DIM), F32), pltpu.VMEM((qb, HEAD_DIM), F32)],
        compiler_params=_params("parallel", "arbitrary"),
    )(qkv, qkv, qkv, do, ltot)


def _sb_pack(dq, dkt, dvt, name):
    T, W = dq.shape
    H = W // HEAD_DIM
    blk = SB_KEYS
    scale = HEAD_DIM ** -0.5

    def body(q_ref, kt_ref, vt_ref, o_ref):
        o_ref[:, pl.ds(0, W)] = (q_ref[...] * scale).astype(BF16)
        for h in range(H):
            o_ref[:, pl.ds(W + h * HEAD_DIM, HEAD_DIM)] = kt_ref[h].T.astype(BF16)
            o_ref[:, pl.ds(2 * W + h * HEAD_DIM, HEAD_DIM)] = vt_ref[h].T.astype(BF16)

    tr = pl.BlockSpec((H, None, HEAD_DIM, blk), lambda i: (0, i, 0, 0))
    return pl.pallas_call(
        body, name=name, grid=(T // blk,), in_specs=[pl.BlockSpec((blk, W), lambda i: (i, 0)), tr, tr],
        out_specs=pl.BlockSpec((blk, 3 * W), lambda i: (i, 0)),
        out_shape=jax.ShapeDtypeStruct((T, 3 * W), BF16), compiler_params=_params("parallel"),
    )(dq, dkt, dvt)


def _local_step(x, target, norms, sgu, comm):
    T, D = x.shape
    A = D // 2
    pre_mix, post_mix, pre_ffn, post_ffn = norms
    ln_g, ln_b, w_s, b_s = sgu
    b_t = b_s.T
    scale = HEAD_DIM ** -0.5

    def vec(p, layer):
        return comm.tie(p[layer:layer + 1])

    h0 = _prenorm(x, vec(pre_mix, 0), "prenorm0")
    z = _matmul(h0, comm.weight("ab_in"), mode="nn", name="ab_in_fwd")
    a_out = _sgu_fwd(z, ln_g, ln_b, w_s, b_t, "sgu_fwd")
    branch = [_dilated_fwd(z, d, A, f"dilated_fwd_{d}") for _, d in DILATED_PAIRS]
    b_out, o_dil, lse_dil = _dilated_merge([b[0] for b in branch], [b[1] for b in branch], "dilated_merge")
    cat = jnp.concatenate([a_out, b_out], axis=-1)
    y0 = _matmul(cat, comm.weight("ab_out"), mode="nn", name="ab_out_fwd")
    comm.arrive("ffn0", after=y0)
    x1, h1 = _postnorm_prenorm(x, y0, vec(post_mix, 0), vec(pre_ffn, 0), "norm_mix0")
    comm.land("ffn0", after=h1)

    def relu2(acc, j):
        r = jnp.maximum(acc, 0.0)
        return r * r

    f0 = _matmul(h1, comm.weight("w1_0"), mode="nn", name="ffn0_w1_fwd", out_dtype=BF16, epi=relu2)
    y1 = _matmul(f0, comm.weight("w2_0"), mode="nn", name="ffn0_w2_fwd")
    comm.arrive("rest", after=y1)
    x2, h2 = _postnorm_prenorm(x1, y1, vec(post_ffn, 0), vec(pre_mix, 1), "norm_ffn0")
    comm.land("rest", after=h2)

    tn_qkv = _tile(D, 512)
    nq = D // tn_qkv

    def scale_q(acc, j):
        return jnp.where(j < nq, acc * scale, acc)

    qkv = _matmul(h2, comm.weight("sb_in"), mode="nn", name="sb_in_fwd", out_dtype=BF16, tn=tn_qkv, epi=scale_q)
    o_sb, ltot = _sb_fwd(qkv, D, "sb_fwd")
    y2 = _matmul(o_sb, comm.weight("sb_out"), mode="nn", name="sb_out_fwd")
    x3, h3 = _postnorm_prenorm(x2, y2, vec(post_mix, 1), vec(pre_ffn, 1), "norm_mix1")
    f1 = _matmul(h3, comm.weight("w1_1"), mode="nn", name="ffn1_w1_fwd", out_dtype=BF16, epi=relu2)
    y3 = _matmul(f1, comm.weight("w2_1"), mode="nn", name="ffn1_w2_fwd")
    loss_tile, dx4 = _postnorm_loss(x3, y3, vec(post_ffn, 1), target, "norm_loss")
    loss = loss_tile[0, 0]

    def relu2_bwd(acc, j, f):
        return acc * (2.0 * jnp.sqrt(f.astype(F32)))

    def ffn_bwd(dx_out, x_in, h, f, y, layer, tag):
        dy, dg_post = _postnorm_bwd(dx_out, y, vec(post_ffn, layer), f"ffn{tag}_postnorm_bwd")
        g_w2 = _matmul(f, dy, mode="tn", name=f"ffn{tag}_w2_wgrad", tm=1024, tn=1024, tk=1024)
        da = _matmul(dy, comm.weight(f"w2_{layer}"), mode="nt", name=f"ffn{tag}_w2_dgrad", out_dtype=BF16,
                     epi=relu2_bwd, extras=(f,))
        g_w1 = _matmul(h, da, mode="tn", name=f"ffn{tag}_w1_wgrad", tm=1024, tn=1024, tk=1024)
        comm.reduce(f"ffn{tag}", {f"w2_{layer}": g_w2, f"w1_{layer}": g_w1})
        dh = _matmul(da, comm.weight(f"w1_{layer}"), mode="nt", name=f"ffn{tag}_w1_dgrad", after=comm.started())
        dx_in, dg_pre = _prenorm_bwd(dx_out, dh, x_in, vec(pre_ffn, layer), f"ffn{tag}_prenorm_bwd")
        return dx_in, dg_pre, dg_post

    dx3, dg_pre_ffn1, dg_post_ffn1 = ffn_bwd(dx4, x3, h3, f1, y3, 1, "1")

    dy2, dg_post_mix1 = _postnorm_bwd(dx3, y2, vec(post_mix, 1), "sb_postnorm_bwd")
    g_sb_out = _matmul(o_sb, dy2, mode="tn", name="sb_out_wgrad", tm=1024, tn=1024, tk=1024)
    do_sb = _matmul(dy2, comm.weight("sb_out"), mode="nt", name="sb_out_dgrad", out_dtype=BF16)
    dq, dk, dv = _sb_bwd(qkv, do_sb, ltot, D, "sb_bwd")
    dqkv = _sb_pack(dq, dk, dv, "sb_pack")
    g_sb_in = _matmul(h2, dqkv, mode="tn", name="sb_in_wgrad", tm=1024, tn=1024, tk=1024)
    comm.reduce("sb", {"sb_out": g_sb_out, "sb_in": g_sb_in})
    dh2 = _matmul(dqkv, comm.weight("sb_in"), mode="nt", name="sb_in_dgrad", after=comm.started())
    dx2, dg_pre_mix1 = _prenorm_bwd(dx3, dh2, x2, vec(pre_mix, 1), "sb_prenorm_bwd")

    dx1, dg_pre_ffn0, dg_post_ffn0 = ffn_bwd(dx2, x1, h1, f0, y1, 0, "0")

    dy0, dg_post_mix0 = _postnorm_bwd(dx1, y0, vec(post_mix, 0), "ab_postnorm_bwd")
    g_ab_out = _matmul(cat, dy0, mode="tn", name="ab_out_wgrad", tm=1024, tn=1024, tk=1024)
    dcat = _matmul(dy0, comm.weight("ab_out"), mode="nt", name="ab_out_dgrad")
    duv, d_ln_g, d_ln_b, d_w_s, d_b_t = _sgu_bwd(z, dcat, ln_g, ln_b, w_s, b_t, "sgu_bwd")
    delta = _dilated_delta(dcat, o_dil, "dilated_delta")
    dqs, dks, dvs = [], [], []
    for _, d in DILATED_PAIRS:
        dqs.append(_dilated_dq(z, dcat, lse_dil, delta, d, A, f"dilated_dq_{d}"))
        dk_b, dv_b = _dilated_dkv(z, dcat, lse_dil, delta, d, A, f"dilated_dkv_{d}")
        dks.append(dk_b)
        dvs.append(dv_b)
    dqkv0 = _dilated_combine(dqs, dks, dvs, "dilated_combine")
    dz = jnp.concatenate([duv, dqkv0], axis=-1)
    g_ab_in = _matmul(h0, dz, mode="tn", name="ab_in_wgrad", tm=1024, tn=1024, tk=1024)
    comm.reduce("ab", {"ab_out": g_ab_out, "ab_in": g_ab_in})
    dh0 = _matmul(dz, comm.weight("ab_in"), mode="nt", name="ab_in_dgrad", after=comm.started())
    dx0, dg_pre_mix0 = _prenorm_bwd(dx1, dh0, x, vec(pre_mix, 0), "ab_prenorm_bwd")

    small = {
        "pre_mix": jnp.concatenate([dg_pre_mix0, dg_pre_mix1], axis=0),
        "post_mix": jnp.concatenate([dg_post_mix0, dg_post_mix1], axis=0),
        "pre_ffn": jnp.concatenate([dg_pre_ffn0, dg_pre_ffn1], axis=0),
        "post_ffn": jnp.concatenate([dg_post_ffn0, dg_post_ffn1], axis=0),
        "ln_g": d_ln_g, "ln_b": d_ln_b, "w_s": d_w_s, "b_s": d_b_t.T,
    }
    return loss, dx0, small


MESH_ID = pl.DeviceIdType.MESH
ANY = pl.BlockSpec(memory_space=pl.ANY)


def _coords():
    return lax.axis_index("x"), lax.axis_index("y"), lax.axis_index("c")


def _shard_of(ref, kind, p):
    if kind == "col":
        n = ref.shape[1] // N_DEV
        return ref.at[:, pl.ds(pl.multiple_of(p * n, 128), n)]
    r = ref.shape[0] // N_DEV
    return ref.at[pl.ds(pl.multiple_of(p * r, 16), r), :]


def _full_shape(shard, kind):
    if kind == "col":
        return (shard.shape[0], shard.shape[1] * N_DEV)
    return (shard.shape[0] * N_DEV, shard.shape[1])


def _place(shard, kind, block, name):
    rows, cols = shard.shape
    tr = _tile(rows, 512)

    def body(b_ref, s_ref, o_ref):
        o_ref[...] = s_ref[...].astype(BF16)

    if kind == "col":
        out = pl.BlockSpec((tr, cols), lambda i, b_ref: (i, b_ref[0]))
    else:
        out = pl.BlockSpec((tr, cols), lambda i, b_ref: (b_ref[0] * (rows // tr) + i, 0))
    return pl.pallas_call(
        body, name=name,
        grid_spec=pltpu.PrefetchScalarGridSpec(
            num_scalar_prefetch=1, grid=(rows // tr,),
            in_specs=[pl.BlockSpec((tr, cols), lambda i, b_ref: (i, 0))], out_specs=out),
        out_shape=jax.ShapeDtypeStruct(_full_shape(shard, kind), BF16),
        compiler_params=_params("parallel"),
    )(block, shard)


def _all_gather(fulls, kinds):
    nt = len(fulls)

    def body(*refs):
        ins = refs[:nt]
        send_sems, recv_sems = refs[2 * nt:]
        x, y, c = _coords()
        me, sibling = (x, y, c), (x, y, 1 - c)
        chips = [(1 - x, y), (x, 1 - y), (1 - x, 1 - y)]

        def copy(t, k, block, to):
            px, py, pc = block
            slot = _shard_of(ins[t], kinds[t], 4 * px + 2 * py + pc)
            return pltpu.make_async_remote_copy(
                src_ref=slot, dst_ref=slot, send_sem=send_sems.at[7 * t + k], recv_sem=recv_sems.at[7 * t + k],
                device_id=to, device_id_type=MESH_ID)

        sent = []
        for t in range(nt):
            first = [copy(t, 0, me, sibling)] + [copy(t, 1 + j, me, (*chip, c)) for j, chip in enumerate(chips)]
            for cp in first:
                cp.start()
            sent += first
        for t in range(nt):
            for j, chip in enumerate(chips):
                copy(t, 1 + j, (*chip, c), me).wait_recv()
                sent.append(copy(t, 4 + j, (*chip, c), sibling))
                sent[-1].start()
        for t in range(nt):
            copy(t, 0, sibling, me).wait_recv()
            for j, chip in enumerate(chips):
                copy(t, 4 + j, (*chip, 1 - c), me).wait_recv()
        for cp in sent:
            cp.wait_send()

    return pl.pallas_call(
        body, name="all_gather_weights",
        in_specs=[ANY] * nt, out_specs=[ANY] * nt,
        out_shape=[jax.ShapeDtypeStruct(f.shape, f.dtype) for f in fulls],
        input_output_aliases={t: t for t in range(nt)},
        scratch_shapes=[pltpu.SemaphoreType.DMA((7 * nt,)), pltpu.SemaphoreType.DMA((7 * nt,))],
    )(*fulls)


HBM = pl.BlockSpec(memory_space=pltpu.HBM)
SEM = pl.BlockSpec(memory_space=pltpu.SEMAPHORE)
FLOWS = pltpu.SideEffectType.DATAFLOW_SIDE_EFFECTING


def _in_hbm(a):
    return pltpu.with_memory_space_constraint(a, pltpu.HBM)


def _hbm_like(bufs):
    return [pltpu.HBM(b.shape, b.dtype) for b in bufs]


def _copies_start(name, bufs, plan, n, after):
    nb = len(bufs)

    def body(*refs):
        send_sems, recv_sems, token = refs[nb + 1], refs[nb + 2], refs[-1]
        for cp in plan(refs[:nb], send_sems, recv_sems):
            cp.start()
        token[...] = jnp.zeros_like(token)

    out = pl.pallas_call(
        body, name=name, in_specs=[HBM] * nb + [ANY],
        out_specs=[SEM, SEM] + [HBM] * nb + [pl.BlockSpec(memory_space=pltpu.VMEM)],
        out_shape=[pltpu.SemaphoreType.DMA((n,)), pltpu.SemaphoreType.DMA((n,))] + _hbm_like(bufs)
        + [jax.ShapeDtypeStruct((8, 128), F32)],
        input_output_aliases={i: 2 + i for i in range(nb)},
        compiler_params=pltpu.CompilerParams(has_side_effects=FLOWS),
    )(*[_in_hbm(b) for b in bufs], after)
    return (out[0], out[1]), list(out[2:2 + nb]), out[-1]


def _copies_wait(name, bufs, sems, after, plan):
    nb = len(bufs)

    def body(*refs):
        for cp in plan(refs[:nb], refs[nb], refs[nb + 1]):
            cp.wait_send()
            cp.wait_recv()

    out = pl.pallas_call(
        body, name=name, in_specs=[HBM] * nb + [SEM, SEM, ANY], out_specs=[HBM] * nb,
        out_shape=_hbm_like(bufs), input_output_aliases={i: i for i in range(nb)},
        compiler_params=pltpu.CompilerParams(has_side_effects=FLOWS),
    )(*bufs, *sems, after)
    return list(out)


def _copies_wait_start(name, bufs, sems, after, plan, next_plan, n_next):
    nb = len(bufs)

    def body(*refs):
        ins = refs[:nb]
        for cp in plan(ins, refs[nb], refs[nb + 1]):
            cp.wait_send()
            cp.wait_recv()
        send_sems, recv_sems, token = refs[nb + 3], refs[nb + 4], refs[-1]
        for cp in next_plan(ins, send_sems, recv_sems):
            cp.start()
        token[...] = jnp.zeros_like(token)

    out = pl.pallas_call(
        body, name=name, in_specs=[HBM] * nb + [SEM, SEM, ANY],
        out_specs=[SEM, SEM] + [HBM] * nb + [pl.BlockSpec(memory_space=pltpu.VMEM)],
        out_shape=[pltpu.SemaphoreType.DMA((n_next,)), pltpu.SemaphoreType.DMA((n_next,))] + _hbm_like(bufs)
        + [jax.ShapeDtypeStruct((8, 128), F32)],
        input_output_aliases={i: 2 + i for i in range(nb)},
        compiler_params=pltpu.CompilerParams(has_side_effects=FLOWS),
    )(*bufs, *sems, after)
    return (out[0], out[1]), list(out[2:2 + nb]), out[-1]


def _gather_plans(kinds):
    nt = len(kinds)

    def slot(refs, t, px, py, pc):
        return _shard_of(refs[t], kinds[t], 4 * px + 2 * py + pc)

    def to_chips(refs, send_sems, recv_sems):
        x, y, c = _coords()
        peers = [(x, y, 1 - c), (1 - x, y, c), (x, 1 - y, c), (1 - x, 1 - y, c)]
        return [pltpu.make_async_remote_copy(
            src_ref=slot(refs, t, x, y, c), dst_ref=slot(refs, t, x, y, c), send_sem=send_sems.at[4 * t + k],
            recv_sem=recv_sems.at[4 * t + k], device_id=peer, device_id_type=MESH_ID)
            for t in range(nt) for k, peer in enumerate(peers)]

    def to_sibling(refs, send_sems, recv_sems):
        x, y, c = _coords()
        chips = [(1 - x, y), (x, 1 - y), (1 - x, 1 - y)]
        return [pltpu.make_async_remote_copy(
            src_ref=slot(refs, t, *chip, c), dst_ref=slot(refs, t, *chip, c), send_sem=send_sems.at[3 * t + j],
            recv_sem=recv_sems.at[3 * t + j], device_id=(x, y, 1 - c), device_id_type=MESH_ID)
            for t in range(nt) for j, chip in enumerate(chips)]

    return to_chips, to_sibling


def _shard_shape(full, kind):
    if kind == "col":
        return (full.shape[0], full.shape[1] // N_DEV)
    return (full.shape[0] // N_DEV, full.shape[1])


CHIPS = ((0, 0), (0, 1), (1, 0), (1, 1))


def _exchange_siblings(grads, kinds, name):
    nt = len(grads)

    def body(*refs):
        ins, outs = refs[:nt], refs[nt:2 * nt]
        send_sems, recv_sems = refs[2 * nt:]
        x, y, c = _coords()
        copies = []
        for t in range(nt):
            for q, (qx, qy) in enumerate(CHIPS):
                copies.append(pltpu.make_async_remote_copy(
                    src_ref=_shard_of(ins[t], kinds[t], 4 * qx + 2 * qy + (1 - c)), dst_ref=outs[t].at[q],
                    send_sem=send_sems.at[4 * t + q], recv_sem=recv_sems.at[4 * t + q],
                    device_id=(x, y, 1 - c), device_id_type=MESH_ID))
        for cp in copies:
            cp.start()
        for cp in copies:
            cp.wait()

    return pl.pallas_call(
        body, name=name,
        in_specs=[ANY] * nt, out_specs=[ANY] * nt,
        out_shape=[jax.ShapeDtypeStruct((4,) + _shard_shape(g, k), F32) for g, k in zip(grads, kinds)],
        scratch_shapes=[pltpu.SemaphoreType.DMA((4 * nt,)), pltpu.SemaphoreType.DMA((4 * nt,))],
    )(*grads)


def _pair_sum(grad, other, kind, where, name):
    rows, cols = _shard_shape(grad, kind)
    tr = _tile(rows, 256)

    def body(w_ref, g_ref, o_ref, s_ref, land_ref):
        s = (g_ref[...] + o_ref[...]).astype(BF16)
        s_ref[...] = s

        @pl.when(pl.program_id(1) == w_ref[1])
        def _():
            land_ref[...] = s

    if kind == "col":
        g_spec = pl.BlockSpec((tr, cols), lambda i, q, w_ref: (i, 2 * q + w_ref[0]))
    else:
        g_spec = pl.BlockSpec((tr, cols), lambda i, q, w_ref: ((2 * q + w_ref[0]) * (rows // tr) + i, 0))
    part = pl.BlockSpec((None, tr, cols), lambda i, q, w_ref: (q, i, 0))
    mine = pl.BlockSpec((None, tr, cols), lambda i, q, w_ref: (w_ref[1], i, 0))
    return pl.pallas_call(
        body, name=name,
        grid_spec=pltpu.PrefetchScalarGridSpec(
            num_scalar_prefetch=1, grid=(rows // tr, 4), in_specs=[g_spec, part], out_specs=[part, mine]),
        out_shape=[jax.ShapeDtypeStruct((4, rows, cols), BF16)] * 2,
        compiler_params=_params("parallel", "arbitrary"),
    )(where, grad, other)


def _scatter_plan(nt):
    def plan(refs, send_sems, recv_sems):
        x, y, c = _coords()
        chips = [(1 - x, y), (x, 1 - y), (1 - x, 1 - y)]
        return [pltpu.make_async_remote_copy(
            src_ref=refs[t].at[2 * qx + qy], dst_ref=refs[nt + t].at[2 * x + y], send_sem=send_sems.at[3 * t + j],
            recv_sem=recv_sems.at[3 * t + j], device_id=(qx, qy, c), device_id_type=MESH_ID)
            for t in range(nt) for j, (qx, qy) in enumerate(chips)]
    return plan


def _all_reduce_small(vec):
    R = vec.shape[0]

    def body(v_ref, o_ref, recv_ref, send_sems, recv_sems):
        x, y, c = _coords()
        me = 4 * x + 2 * y + c
        recv_ref[me] = v_ref[...]
        copies = []
        for k in range(1, N_DEV):
            bx, by, bc = (k >> 2) & 1, (k >> 1) & 1, k & 1
            peer = (1 - x if bx else x, 1 - y if by else y, 1 - c if bc else c)
            copies.append(pltpu.make_async_remote_copy(
                src_ref=v_ref, dst_ref=recv_ref.at[me],
                send_sem=send_sems.at[k - 1], recv_sem=recv_sems.at[k - 1],
                device_id=peer, device_id_type=MESH_ID))
        for cp in copies:
            cp.start()
        for cp in copies:
            cp.wait()
        total = recv_ref[0]
        for p in range(1, N_DEV):
            total = total + recv_ref[p]
        o_ref[...] = total

    return pl.pallas_call(
        body, name="all_reduce_small",
        in_specs=[pl.BlockSpec(memory_space=pltpu.VMEM)], out_specs=pl.BlockSpec(memory_space=pltpu.VMEM),
        out_shape=jax.ShapeDtypeStruct((R, 128), F32),
        scratch_shapes=[pltpu.VMEM((N_DEV, R, 128), F32), pltpu.SemaphoreType.DMA((N_DEV - 1,)),
                        pltpu.SemaphoreType.DMA((N_DEV - 1,))],
        compiler_params=pltpu.CompilerParams(vmem_limit_bytes=VMEM_LIMIT),
    )(vec)


def _adamw_math(w, g, m, v):
    m = ADAM_B1 * m + (1.0 - ADAM_B1) * g
    v = ADAM_B2 * v + (1.0 - ADAM_B2) * (g * g)
    m_hat = m / (1.0 - ADAM_B1 ** ADAM_STEP)
    v_hat = v / (1.0 - ADAM_B2 ** ADAM_STEP)
    delta = -ADAM_LR * (m_hat / (jnp.sqrt(v_hat) + ADAM_EPS) + ADAM_WD * w)
    return delta, m, v


def _adamw(w, parts, m, v, name):
    layers, rows, cols = w.shape
    tr = rows if rows * cols <= 256 * 1024 else _tile(rows, 128)
    out = None
    for layer in range(layers):
        n = parts[layer].shape[0]

        def body(w_ref, p_ref, m_ref, v_ref, *rest, n=n):
            g_ref, d_ref, mo_ref, vo_ref = rest[-4:]
            g = p_ref[0].astype(F32)
            for q in range(1, n):
                g = g + p_ref[q].astype(F32)
            g_ref[...] = g
            d_ref[...], mo_ref[...], vo_ref[...] = _adamw_math(w_ref[...], g, m_ref[...], v_ref[...])

        blk = pl.BlockSpec((None, tr, cols), lambda i, layer=layer: (layer, i, 0))
        earlier = [] if out is None else list(out)
        out = pl.pallas_call(
            body, name=f"{name}_{layer}", grid=(rows // tr,),
            in_specs=[blk, pl.BlockSpec((n, tr, cols), lambda i: (0, i, 0)), blk, blk] + [ANY] * len(earlier),
            out_specs=[blk] * 4, out_shape=[jax.ShapeDtypeStruct((layers, rows, cols), F32)] * 4,
            input_output_aliases={4 + k: k for k in range(len(earlier))},
            compiler_params=_params("parallel"),
        )(w, parts[layer], m, v, *earlier)
    return out


def _pack(arrays):
    rows = []
    for a in arrays:
        flat = a.reshape(-1)
        pad = (-flat.shape[0]) % 1024
        rows.append(jnp.pad(flat, (0, pad)).reshape(-1, 128))
    return jnp.concatenate(rows, axis=0)


def _unpack(packed, like):
    out, r = [], 0
    for a in like:
        n = math.prod(a.shape)
        nr = (n + 1023) // 1024 * 8
        out.append(packed[r:r + nr].reshape(-1)[:n].reshape(a.shape))
        r += nr
    return out


KIND = {"ab_in": "col", "ab_out": "row", "sb_in": "col", "sb_out": "row",
        "w1_0": "col", "w1_1": "col", "w2_0": "row", "w2_1": "row"}
GATHER_FIRST = ("ab_in", "ab_out")
GATHER_LATER = {"ffn0": ("w1_0", "w2_0"),
                "rest": ("sb_in", "sb_out", "w1_1", "w2_1")}


class _Exchange:
    def __init__(self, shards):
        x, y, c = _coords()
        self.where = jnp.stack([c, 2 * x + y]).astype(jnp.int32)
        block = (4 * x + 2 * y + c).astype(jnp.int32).reshape(1)
        self.full = {n: _place(s, KIND[n], block, f"place_{n}") for n, s in shards.items()}
        first = _all_gather([self.full[n] for n in GATHER_FIRST], [KIND[n] for n in GATHER_FIRST])
        self.full.update(zip(GATHER_FIRST, first))
        self.tokens = []
        self.gathers = {}
        self.scatters = {}
        self.landed = {}
        after = first[-1]
        for key, group in GATHER_LATER.items():
            to_chips, to_sibling = _gather_plans([KIND[n] for n in group])
            bufs = [self.full[n] for n in group]
            sems, bufs, after = _copies_start(f"gather_start_{key}", bufs, to_chips, 4 * len(group), after)
            self.tokens.append(after)
            self.gathers[key] = (group, sems, bufs, to_chips, to_sibling)

    def tie(self, small):
        for token in self.tokens:
            small = small + token[0:1, 0:1]
        self.tokens = []
        return small

    def started(self):
        return tuple(self.tokens)

    def weight(self, name):
        return self.full[name]

    def arrive(self, key, after):
        group, sems, bufs, to_chips, to_sibling = self.gathers[key]
        sems, bufs, token = _copies_wait_start(f"gather_pass_{key}", bufs, sems, after, to_chips, to_sibling,
                                               3 * len(group))
        self.tokens.append(token)
        self.gathers[key] = (group, sems, bufs, to_chips, to_sibling)

    def land(self, key, after):
        group, sems, bufs, _, to_sibling = self.gathers.pop(key)
        self.full.update(zip(group, _copies_wait(f"gather_done_{key}", bufs, sems, after, to_sibling)))

    def reduce(self, key, grads):
        names = list(grads)
        kinds = [KIND[n] for n in names]
        others = _exchange_siblings([grads[n] for n in names], kinds, f"grad_siblings_{key}")
        pairs = [_pair_sum(grads[n], o, k, self.where, f"pair_sum_{n}") for n, o, k in zip(names, others, kinds)]
        parts, lands = [p[0] for p in pairs], [p[1] for p in pairs]
        plan = _scatter_plan(len(names))
        sems, bufs, token = _copies_start(f"scatter_start_{key}", parts + lands, plan, 3 * len(names), parts[0])
        self.tokens.append(token)
        self.scatters[key] = (names, sems, bufs, plan)

    def finish(self, after):
        for key, (names, sems, bufs, plan) in self.scatters.items():
            bufs = _copies_wait(f"scatter_done_{key}", bufs, sems, after, plan)
            self.landed.update(zip(names, bufs[len(names):]))
        self.scatters = {}
        return self.landed


SMALL = ("norm_pre_mix", "norm_post_mix", "norm_pre_ffn", "norm_post_ffn", "sgu_ln_g", "sgu_ln_b", "sgu_w", "sgu_b")
ORDER = ("norm_pre_mix", "norm_post_mix", "norm_pre_ffn", "norm_post_ffn", "ab_w_in", "sgu_ln_g", "sgu_ln_b", "sgu_w",
         "sgu_b", "ab_w_out", "sb_w_in", "sb_w_out", "ffn_w1", "ffn_w2")


def kernel(x, norm_pre_mix, norm_post_mix, norm_pre_ffn, norm_post_ffn, ab_w_in, sgu_ln_g, sgu_ln_b, sgu_w, sgu_b, ab_w_out, sb_w_in, sb_w_out, ffn_w1, ffn_w2, loss_target, m_norm_pre_mix, m_norm_post_mix, m_norm_pre_ffn, m_norm_post_ffn, m_ab_w_in, m_sgu_ln_g, m_sgu_ln_b, m_sgu_w, m_sgu_b, m_ab_w_out, m_sb_w_in, m_sb_w_out, m_ffn_w1, m_ffn_w2, v_norm_pre_mix, v_norm_post_mix, v_norm_pre_ffn, v_norm_post_ffn, v_ab_w_in, v_sgu_ln_g, v_sgu_ln_b, v_sgu_w, v_sgu_b, v_ab_w_out, v_sb_w_in, v_sb_w_out, v_ffn_w1, v_ffn_w2):
    W = dict(norm_pre_mix=norm_pre_mix, norm_post_mix=norm_post_mix, norm_pre_ffn=norm_pre_ffn,
             norm_post_ffn=norm_post_ffn, ab_w_in=ab_w_in, sgu_ln_g=sgu_ln_g, sgu_ln_b=sgu_ln_b, sgu_w=sgu_w,
             sgu_b=sgu_b, ab_w_out=ab_w_out, sb_w_in=sb_w_in, sb_w_out=sb_w_out, ffn_w1=ffn_w1, ffn_w2=ffn_w2)
    M = dict(norm_pre_mix=m_norm_pre_mix, norm_post_mix=m_norm_post_mix, norm_pre_ffn=m_norm_pre_ffn,
             norm_post_ffn=m_norm_post_ffn, ab_w_in=m_ab_w_in, sgu_ln_g=m_sgu_ln_g, sgu_ln_b=m_sgu_ln_b,
             sgu_w=m_sgu_w, sgu_b=m_sgu_b, ab_w_out=m_ab_w_out, sb_w_in=m_sb_w_in, sb_w_out=m_sb_w_out,
             ffn_w1=m_ffn_w1, ffn_w2=m_ffn_w2)
    V = dict(norm_pre_mix=v_norm_pre_mix, norm_post_mix=v_norm_post_mix, norm_pre_ffn=v_norm_pre_ffn,
             norm_post_ffn=v_norm_post_ffn, ab_w_in=v_ab_w_in, sgu_ln_g=v_sgu_ln_g, sgu_ln_b=v_sgu_ln_b,
             sgu_w=v_sgu_w, sgu_b=v_sgu_b, ab_w_out=v_ab_w_out, sb_w_in=v_sb_w_in, sb_w_out=v_sb_w_out,
             ffn_w1=v_ffn_w1, ffn_w2=v_ffn_w2)

    shards = {"ab_in": ab_w_in[0], "ab_out": ab_w_out[0], "w1_0": ffn_w1[0], "w2_0": ffn_w2[0],
              "sb_in": sb_w_in[0], "sb_out": sb_w_out[0], "w1_1": ffn_w1[1], "w2_1": ffn_w2[1]}
    comm = _Exchange(shards)
    norms = (norm_pre_mix, norm_post_mix, norm_pre_ffn, norm_post_ffn)
    sgu = (sgu_ln_g, sgu_ln_b, sgu_w[0], sgu_b[0])
    loss, dx, small = _local_step(x[0], loss_target[0], norms, sgu, comm)
    loss = lax.psum(loss, MESH_AXES)
    landed = comm.finish(after=dx)

    out = {}
    for name, layers in (("ab_w_in", ["ab_in"]), ("ab_w_out", ["ab_out"]), ("sb_w_in", ["sb_in"]),
                         ("sb_w_out", ["sb_out"]), ("ffn_w1", ["w1_0", "w1_1"]), ("ffn_w2", ["w2_0", "w2_1"])):
        out[name] = _adamw(W[name], [landed[n] for n in layers], M[name], V[name], f"adamw_{name}")

    small_g = [small["pre_mix"], small["post_mix"], small["pre_ffn"], small["post_ffn"], small["ln_g"],
               small["ln_b"], small["w_s"][None], small["b_s"][None]]
    g_small = _all_reduce_small(_pack(small_g))
    res = _adamw(_pack([W[n] for n in SMALL])[None], [g_small[None]], _pack([M[n] for n in SMALL])[None],
                 _pack([V[n] for n in SMALL])[None], "adamw_small")
    like = [W[n] for n in SMALL]
    for n, *vals in zip(SMALL, *[_unpack(r[0], like) for r in res]):
        out[n] = vals

    return (loss, dx[None], *[out[n][0] for n in ORDER], *[out[n][1] for n in ORDER],
            *[out[n][2] for n in ORDER], *[out[n][3] for n in ORDER])
```

```python
import functools
import math

import jax
import jax.numpy as jnp
from jax import lax
from jax.experimental import pallas as pl
from jax.experimental.pallas import tpu as pltpu

F32 = jnp.float32
BF16 = jnp.bfloat16

HEAD_DIM = 128
CHUNK = 128
ATT_BLOCK = 128
DILATED_PAIRS = ((128, 1), (512, 4), (2048, 16))
RMS_EPS = 1e-6
LN_EPS = 1e-5
ADAM_LR = 0.001
ADAM_B1 = 0.9
ADAM_B2 = 0.999
ADAM_EPS = 1e-08
ADAM_WD = 0.01
ADAM_STEP = 10
N_DEV = 8
MESH_AXES = ("x", "y", "c")
MASKED = -1e30

V7X_VMEM_BYTES = 64 * 1024 * 1024
VMEM_LIMIT = V7X_VMEM_BYTES - 8 * 1024 * 1024

NN = (((1,), (0,)), ((), ()))
NT = (((1,), (1,)), ((), ()))
TN = (((0,), (0,)), ((), ()))


def _params(*sem):
    return pltpu.CompilerParams(dimension_semantics=sem, vmem_limit_bytes=VMEM_LIMIT)


def _dot(a, b, dims=NN):
    return lax.dot_general(a, b, dims, preferred_element_type=F32)


def _tile(n, preferred):
    if n <= preferred:
        return n
    t = preferred - preferred % 128
    while n % t:
        t -= 128
    assert t > 0, (n, preferred)
    return t


def _matmul(a, b, *, mode, name, out_dtype=F32, tm=1024, tn=1024, tk=2048, epi=None, extras=(), after=()):
    if mode == "nn":
        (M, K), N = a.shape, b.shape[1]
    elif mode == "nt":
        (M, K), N = a.shape, b.shape[0]
    else:
        (K, M), N = a.shape, b.shape[1]
    tm, tn, tk = _tile(M, tm), _tile(N, tn), _tile(K, tk)
    nk = K // tk
    if mode == "tn":
        a_spec = pl.BlockSpec((tk, tm), lambda i, j, k: (k, i))
    else:
        a_spec = pl.BlockSpec((tm, tk), lambda i, j, k: (i, k))
    if mode == "nt":
        b_spec = pl.BlockSpec((tn, tk), lambda i, j, k: (j, k))
    else:
        b_spec = pl.BlockSpec((tk, tn), lambda i, j, k: (k, j))
    o_spec = pl.BlockSpec((tm, tn), lambda i, j, k: (i, j))
    dims = {"nn": NN, "nt": NT, "tn": TN}[mode]
    n_extra = len(extras)
    n_in = n_extra + len(after)

    def finish(acc, refs):
        j = pl.program_id(1)
        if epi is None:
            return acc
        return epi(acc, j, *[r[...] for r in refs])

    if nk == 1:
        def body(a_ref, b_ref, *rest):
            o_ref = rest[n_in]
            acc = _dot(a_ref[...], b_ref[...], dims)
            o_ref[...] = finish(acc, rest[:n_extra]).astype(o_ref.dtype)
        scratch = []
    else:
        def body(a_ref, b_ref, *rest):
            o_ref, acc_ref = rest[n_in], rest[n_in + 1]
            k = pl.program_id(2)

            @pl.when(k == 0)
            def _():
                acc_ref[...] = jnp.zeros_like(acc_ref)

            acc_ref[...] += _dot(a_ref[...], b_ref[...], dims)

            @pl.when(k == nk - 1)
            def _():
                o_ref[...] = finish(acc_ref[...], rest[:n_extra]).astype(o_ref.dtype)
        scratch = [pltpu.VMEM((tm, tn), F32)]

    return pl.pallas_call(
        body,
        name=name,
        grid=(M // tm, N // tn, nk),
        in_specs=[a_spec, b_spec] + [o_spec] * n_extra + [ANY] * len(after),
        out_specs=o_spec,
        out_shape=jax.ShapeDtypeStruct((M, N), out_dtype),
        scratch_shapes=scratch,
        compiler_params=_params("parallel", "parallel", "arbitrary"),
    )(a, b, *extras, *after)


ROWS = 256


def _rms(x):
    return lax.rsqrt(jnp.mean(x * x, axis=-1, keepdims=True) + RMS_EPS)


def _prenorm(x, g, name):
    T, D = x.shape

    def body(x_ref, g_ref, h_ref):
        xv = x_ref[...]
        h_ref[...] = (xv * _rms(xv) * g_ref[...]).astype(BF16)

    row = pl.BlockSpec((ROWS, D), lambda i: (i, 0))
    vec = pl.BlockSpec((1, D), lambda i: (0, 0))
    return pl.pallas_call(
        body, name=name, grid=(T // ROWS,), in_specs=[row, vec], out_specs=row,
        out_shape=jax.ShapeDtypeStruct((T, D), BF16), compiler_params=_params("parallel"),
    )(x, g)


def _postnorm_prenorm(x, y, g_post, g_pre, name):
    T, D = x.shape

    def body(x_ref, y_ref, gp_ref, gn_ref, xo_ref, h_ref):
        yv = y_ref[...]
        xn = x_ref[...] + yv * _rms(yv) * gp_ref[...]
        xo_ref[...] = xn
        h_ref[...] = (xn * _rms(xn) * gn_ref[...]).astype(BF16)

    row = pl.BlockSpec((ROWS, D), lambda i: (i, 0))
    vec = pl.BlockSpec((1, D), lambda i: (0, 0))
    return pl.pallas_call(
        body, name=name, grid=(T // ROWS,), in_specs=[row, row, vec, vec], out_specs=[row, row],
        out_shape=[jax.ShapeDtypeStruct((T, D), F32), jax.ShapeDtypeStruct((T, D), BF16)],
        compiler_params=_params("parallel"),
    )(x, y, g_post, g_pre)


def _postnorm_loss(x, y, g_post, target, name):
    T, D = x.shape

    def body(x_ref, y_ref, gp_ref, t_ref, loss_ref, dx_ref):
        @pl.when(pl.program_id(0) == 0)
        def _():
            loss_ref[...] = jnp.zeros_like(loss_ref)

        yv = y_ref[...]
        err = x_ref[...] + yv * _rms(yv) * gp_ref[...] - t_ref[...]
        dx_ref[...] = err * (1.0 / D)
        loss_ref[...] += 0.5 * jnp.sum(jnp.sum(err * err, axis=-1, keepdims=True) * (1.0 / D))

    row = pl.BlockSpec((ROWS, D), lambda i: (i, 0))
    vec = pl.BlockSpec((1, D), lambda i: (0, 0))
    acc = pl.BlockSpec((8, 128), lambda i: (0, 0))
    return pl.pallas_call(
        body, name=name, grid=(T // ROWS,), in_specs=[row, row, vec, row], out_specs=[acc, row],
        out_shape=[jax.ShapeDtypeStruct((8, 128), F32), jax.ShapeDtypeStruct((T, D), F32)],
        compiler_params=_params("arbitrary"),
    )(x, y, g_post, target)


def _postnorm_bwd(dx, y, g_post, name):
    T, D = dx.shape

    def body(dx_ref, y_ref, g_ref, dy_ref, dg_ref):
        @pl.when(pl.program_id(0) == 0)
        def _():
            dg_ref[...] = jnp.zeros_like(dg_ref)

        yv, dn = y_ref[...], dx_ref[...]
        r = _rms(yv)
        yh = yv * r
        gd = dn * g_ref[...]
        dy_ref[...] = (r * (gd - yh * jnp.mean(yh * gd, axis=-1, keepdims=True))).astype(BF16)
        dg_ref[...] += jnp.sum(dn * yh, axis=0, keepdims=True)

    row = pl.BlockSpec((ROWS, D), lambda i: (i, 0))
    vec = pl.BlockSpec((1, D), lambda i: (0, 0))
    return pl.pallas_call(
        body, name=name, grid=(T // ROWS,), in_specs=[row, row, vec], out_specs=[row, vec],
        out_shape=[jax.ShapeDtypeStruct((T, D), BF16), jax.ShapeDtypeStruct((1, D), F32)],
        compiler_params=_params("arbitrary"),
    )(dx, y, g_post)


def _prenorm_bwd(dx_out, dh, x, g_pre, name):
    T, D = x.shape

    def body(dxo_ref, dh_ref, x_ref, g_ref, dx_ref, dg_ref):
        @pl.when(pl.program_id(0) == 0)
        def _():
            dg_ref[...] = jnp.zeros_like(dg_ref)

        xv, dhv = x_ref[...], dh_ref[...]
        r = _rms(xv)
        xh = xv * r
        gd = dhv * g_ref[...]
        dx_ref[...] = dxo_ref[...] + r * (gd - xh * jnp.mean(xh * gd, axis=-1, keepdims=True))
        dg_ref[...] += jnp.sum(dhv * xh, axis=0, keepdims=True)

    row = pl.BlockSpec((ROWS, D), lambda i: (i, 0))
    vec = pl.BlockSpec((1, D), lambda i: (0, 0))
    return pl.pallas_call(
        body, name=name, grid=(T // ROWS,), in_specs=[row, row, row, vec], out_specs=[row, vec],
        out_shape=[jax.ShapeDtypeStruct((T, D), F32), jax.ShapeDtypeStruct((1, D), F32)],
        compiler_params=_params("arbitrary"),
    )(dx_out, dh, x, g_pre)


_INV_SQRT2 = 1.0 / math.sqrt(2.0)
_INV_SQRT2PI = 1.0 / math.sqrt(2.0 * math.pi)


def _gelu(x):
    return 0.5 * x * (1.0 + lax.erf(x * _INV_SQRT2))


def _gelu_grad(x):
    return 0.5 * (1.0 + lax.erf(x * _INV_SQRT2)) + x * jnp.exp(-0.5 * x * x) * _INV_SQRT2PI


def _layernorm_stats(x):
    mu = jnp.mean(x, axis=-1, keepdims=True)
    xc = x - mu
    rstd = lax.rsqrt(jnp.mean(xc * xc, axis=-1, keepdims=True) + LN_EPS)
    return xc * rstd, rstd


def _tril_mask():
    i = lax.broadcasted_iota(jnp.int32, (CHUNK, CHUNK), 0)
    j = lax.broadcasted_iota(jnp.int32, (CHUNK, CHUNK), 1)
    return j <= i


SGU_ROWS = 512


def _sgu_fwd(z, ln_g, ln_b, w_s, b_t, name):
    T = z.shape[0]
    A = ln_g.shape[1]
    G = A // 128
    rows = min(SGU_ROWS, T)

    def body(u_ref, v_ref, g_ref, b_ref, w_ref, bt_ref, o_ref):
        mask = _tril_mask()
        for c in range(rows // CHUNK):
            rs = pl.ds(c * CHUNK, CHUNK)
            xh, _ = _layernorm_stats(_gelu(v_ref[rs, :]))
            vn = (xh * g_ref[...] + b_ref[...]).astype(BF16)
            for g in range(G):
                cs = pl.ds(g * 128, 128)
                w = jnp.where(mask, w_ref[g], 0.0).astype(BF16)
                mixed = _dot(w, vn[:, g * 128:(g + 1) * 128]) + bt_ref[:, g:g + 1]
                o_ref[rs, cs] = (_gelu(u_ref[rs, cs]) * mixed).astype(BF16)

    return pl.pallas_call(
        body, name=name, grid=(T // rows,),
        in_specs=[
            pl.BlockSpec((rows, A), lambda i: (i, 0)),
            pl.BlockSpec((rows, A), lambda i: (i, 1)),
            pl.BlockSpec((1, A), lambda i: (0, 0)),
            pl.BlockSpec((1, A), lambda i: (0, 0)),
            pl.BlockSpec((G, CHUNK, CHUNK), lambda i: (0, 0, 0)),
            pl.BlockSpec((CHUNK, G), lambda i: (0, 0)),
        ],
        out_specs=pl.BlockSpec((rows, A), lambda i: (i, 0)),
        out_shape=jax.ShapeDtypeStruct((T, A), BF16),
        compiler_params=_params("parallel"),
    )(z, z, ln_g, ln_b, w_s, b_t)


def _sgu_bwd(z, dcat, ln_g, ln_b, w_s, b_t, name):
    T = z.shape[0]
    A = ln_g.shape[1]
    G = A // 128
    rows = min(SGU_ROWS, T)

    def body(u_ref, v_ref, da_ref, g_ref, b_ref, w_ref, bt_ref, dz_ref, dg_ref, db_ref, dw_ref, dbt_ref, dvn_ref):
        @pl.when(pl.program_id(0) == 0)
        def _():
            dg_ref[...] = jnp.zeros_like(dg_ref)
            db_ref[...] = jnp.zeros_like(db_ref)
            dw_ref[...] = jnp.zeros_like(dw_ref)
            dbt_ref[...] = jnp.zeros_like(dbt_ref)

        mask = _tril_mask()
        for c in range(rows // CHUNK):
            rs = pl.ds(c * CHUNK, CHUNK)
            vv = v_ref[rs, :]
            gv = _gelu(vv)
            xh, rstd = _layernorm_stats(gv)
            vn = (xh * g_ref[...] + b_ref[...]).astype(BF16)
            for g in range(G):
                cs = pl.ds(g * 128, 128)
                w = jnp.where(mask, w_ref[g], 0.0).astype(BF16)
                vg = vn[:, g * 128:(g + 1) * 128]
                mixed = _dot(w, vg) + bt_ref[:, g:g + 1]
                uu = u_ref[rs, cs]
                da = da_ref[rs, cs]
                dz_ref[rs, cs] = (da * mixed * _gelu_grad(uu)).astype(BF16)
                dm = da * _gelu(uu)
                dmb = dm.astype(BF16)
                dbt_ref[:, g:g + 1] += jnp.sum(dm, axis=1, keepdims=True)
                dw_ref[g] += jnp.where(mask, _dot(dmb, vg, NT), 0.0)
                dvn_ref[:, cs] = _dot(w, dmb, TN)
            dvn = dvn_ref[...]
            dg_ref[...] += jnp.sum(dvn * xh, axis=0, keepdims=True)
            db_ref[...] += jnp.sum(dvn, axis=0, keepdims=True)
            dxh = dvn * g_ref[...]
            dgv = rstd * (dxh - jnp.mean(dxh, axis=-1, keepdims=True)
                          - xh * jnp.mean(dxh * xh, axis=-1, keepdims=True))
            dz_ref[rs, pl.ds(A, A)] = (dgv * _gelu_grad(vv)).astype(BF16)

    vec = pl.BlockSpec((1, A), lambda i: (0, 0))
    wsp = pl.BlockSpec((G, CHUNK, CHUNK), lambda i: (0, 0, 0))
    bsp = pl.BlockSpec((CHUNK, G), lambda i: (0, 0))
    return pl.pallas_call(
        body, name=name, grid=(T // rows,),
        in_specs=[
            pl.BlockSpec((rows, A), lambda i: (i, 0)),
            pl.BlockSpec((rows, A), lambda i: (i, 1)),
            pl.BlockSpec((rows, A), lambda i: (i, 0)),
            vec, vec, wsp, bsp,
        ],
        out_specs=[pl.BlockSpec((rows, 2 * A), lambda i: (i, 0)), vec, vec, wsp, bsp],
        out_shape=[
            jax.ShapeDtypeStruct((T, 2 * A), BF16),
            jax.ShapeDtypeStruct((1, A), F32),
            jax.ShapeDtypeStruct((1, A), F32),
            jax.ShapeDtypeStruct((G, CHUNK, CHUNK), F32),
            jax.ShapeDtypeStruct((CHUNK, G), F32),
        ],
        scratch_shapes=[pltpu.VMEM((CHUNK, A), F32)],
        compiler_params=_params("arbitrary"),
    )(z, z, dcat, ln_g, ln_b, w_s, b_t)


def _alibi_row(B, d):
    H = B // HEAD_DIM
    slopes = [d * 2.0 ** (-8.0 * (h + 1.0) / H) for h in range(H)]
    return jnp.repeat(jnp.asarray(slopes, F32), HEAD_DIM)[None, :]


def _dil_scores(q, k, slope_d, valid, dist):
    s = _dot(q, k, NT) - slope_d * dist
    return jnp.where(valid, s, MASKED)


def _dil_layout(T, B, d):
    H = B // HEAD_DIM
    hp = H if d == 1 else 1
    return hp, H // hp, T // (d * ATT_BLOCK)


def _dil_rows(ref, r, d, cs):
    return ref[pl.ds(r, ATT_BLOCK, stride=d), cs] if d > 1 else ref[:, cs]


def _dil_put(ref, r, d, cs, val):
    if d > 1:
        ref[pl.ds(r, ATT_BLOCK, stride=d), cs] = val
    else:
        ref[:, cs] = val


def _dilated_fwd(z, d, B, name):
    T = z.shape[0]
    H = B // HEAD_DIM
    hp, groups, nb = _dil_layout(T, B, d)
    cw = hp * HEAD_DIM
    scale = HEAD_DIM ** -0.5
    blk = ATT_BLOCK

    def body(q_ref, kp_ref, kc_ref, vp_ref, vc_ref, sl_ref, o_ref, l_ref):
        n = pl.program_id(1)
        qi = lax.broadcasted_iota(jnp.int32, (blk, 2 * blk), 0)
        kj = lax.broadcasted_iota(jnp.int32, (blk, 2 * blk), 1)
        dist = qi + blk - kj
        valid = (dist >= 0) & (dist <= blk) & ((kj >= blk) | (n > 0))
        distf = dist.astype(F32)
        for r in range(d):
            for hh in range(hp):
                cs = pl.ds(hh * HEAD_DIM, HEAD_DIM)
                q = (_dil_rows(q_ref, r, d, cs) * scale).astype(BF16)
                k = jnp.concatenate([_dil_rows(kp_ref, r, d, cs), _dil_rows(kc_ref, r, d, cs)], axis=0).astype(BF16)
                v = jnp.concatenate([_dil_rows(vp_ref, r, d, cs), _dil_rows(vc_ref, r, d, cs)], axis=0).astype(BF16)
                s = _dil_scores(q, k, sl_ref[:, cs][:, :1], valid, distf)
                m = jnp.max(s, axis=-1, keepdims=True)
                p = jnp.exp(s - m)
                den = jnp.sum(p, axis=-1, keepdims=True)
                _dil_put(o_ref, r, d, cs, _dot(p.astype(BF16), v) / den)
                _dil_put(l_ref, r, d, cs, jnp.broadcast_to(m + jnp.log(den), (blk, HEAD_DIM)))

    def col(unit):
        return lambda g, n: (n, unit * groups + g)

    def col_prev(unit):
        return lambda g, n: (jnp.maximum(n - 1, 0), unit * groups + g)

    bs = (d * blk, cw)
    out = pl.BlockSpec(bs, lambda g, n: (n, g))
    return pl.pallas_call(
        body, name=name, grid=(groups, nb),
        in_specs=[pl.BlockSpec(bs, col(2)), pl.BlockSpec(bs, col_prev(3)), pl.BlockSpec(bs, col(3)),
                  pl.BlockSpec(bs, col_prev(4)), pl.BlockSpec(bs, col(4)), pl.BlockSpec((1, cw), lambda g, n: (0, g))],
        out_specs=[out, out],
        out_shape=[jax.ShapeDtypeStruct((T, B), F32)] * 2,
        compiler_params=_params("parallel", "parallel"),
    )(z, z, z, z, z, _alibi_row(B, d))


def _dilated_merge(os_, ls_, name):
    T, B = os_[0].shape

    def body(o1, o2, o3, l1, l2, l3, ob_ref, of_ref, lt_ref):
        a, b, c = l1[...], l2[...], l3[...]
        m = jnp.maximum(jnp.maximum(a, b), c)
        ea, eb, ec = jnp.exp(a - m), jnp.exp(b - m), jnp.exp(c - m)
        tot = ea + eb + ec
        o = (ea * o1[...] + eb * o2[...] + ec * o3[...]) / tot
        of_ref[...] = o
        ob_ref[...] = o.astype(BF16)
        lt_ref[...] = m + jnp.log(tot)

    row = pl.BlockSpec((ROWS, B), lambda i: (i, 0))
    return pl.pallas_call(
        body, name=name, grid=(T // ROWS,), in_specs=[row] * 6, out_specs=[row] * 3,
        out_shape=[jax.ShapeDtypeStruct((T, B), BF16), jax.ShapeDtypeStruct((T, B), F32),
                   jax.ShapeDtypeStruct((T, B), F32)],
        compiler_params=_params("parallel"),
    )(*os_, *ls_)


def _dilated_delta(dcat, o, name):
    T, B = o.shape
    H = B // HEAD_DIM

    def body(do_ref, o_ref, d_ref):
        for h in range(H):
            cs = pl.ds(h * HEAD_DIM, HEAD_DIM)
            dsum = jnp.sum(do_ref[:, cs] * o_ref[:, cs], axis=-1, keepdims=True)
            d_ref[:, cs] = jnp.broadcast_to(dsum, (ROWS, HEAD_DIM))

    row = pl.BlockSpec((ROWS, B), lambda i: (i, 0))
    return pl.pallas_call(
        body, name=name, grid=(T // ROWS,),
        in_specs=[pl.BlockSpec((ROWS, B), lambda i: (i, 1)), row], out_specs=row,
        out_shape=jax.ShapeDtypeStruct((T, B), F32), compiler_params=_params("parallel"),
    )(dcat, o)


def _dilated_dq(z, dcat, lse, delta, d, B, name):
    T = z.shape[0]
    hp, groups, nb = _dil_layout(T, B, d)
    cw = hp * HEAD_DIM
    scale = HEAD_DIM ** -0.5
    blk = ATT_BLOCK

    def body(q_ref, kp_ref, kc_ref, vp_ref, vc_ref, do_ref, l_ref, dl_ref, sl_ref, dq_ref):
        n = pl.program_id(1)
        qi = lax.broadcasted_iota(jnp.int32, (blk, 2 * blk), 0)
        kj = lax.broadcasted_iota(jnp.int32, (blk, 2 * blk), 1)
        dist = qi + blk - kj
        valid = (dist >= 0) & (dist <= blk) & ((kj >= blk) | (n > 0))
        distf = dist.astype(F32)
        for r in range(d):
            for hh in range(hp):
                cs = pl.ds(hh * HEAD_DIM, HEAD_DIM)
                q = (_dil_rows(q_ref, r, d, cs) * scale).astype(BF16)
                k = jnp.concatenate([_dil_rows(kp_ref, r, d, cs), _dil_rows(kc_ref, r, d, cs)], axis=0).astype(BF16)
                v = jnp.concatenate([_dil_rows(vp_ref, r, d, cs), _dil_rows(vc_ref, r, d, cs)], axis=0).astype(BF16)
                s = _dil_scores(q, k, sl_ref[:, cs][:, :1], valid, distf)
                p = jnp.exp(s - _dil_rows(l_ref, r, d, cs)[:, :1])
                dp = _dot(_dil_rows(do_ref, r, d, cs).astype(BF16), v, NT)
                ds = p * (dp - _dil_rows(dl_ref, r, d, cs)[:, :1])
                _dil_put(dq_ref, r, d, cs, _dot(ds.astype(BF16), k))

    def col(unit):
        return lambda g, n: (n, unit * groups + g)

    def col_prev(unit):
        return lambda g, n: (jnp.maximum(n - 1, 0), unit * groups + g)

    bs = (d * blk, cw)
    out = pl.BlockSpec(bs, lambda g, n: (n, g))
    return pl.pallas_call(
        body, name=name, grid=(groups, nb),
        in_specs=[pl.BlockSpec(bs, col(2)), pl.BlockSpec(bs, col_prev(3)), pl.BlockSpec(bs, col(3)),
                  pl.BlockSpec(bs, col_prev(4)), pl.BlockSpec(bs, col(4)), pl.BlockSpec(bs, col(1)), out, out,
                  pl.BlockSpec((1, cw), lambda g, n: (0, g))],
        out_specs=out,
        out_shape=jax.ShapeDtypeStruct((T, B), F32),
        compiler_params=_params("parallel", "parallel"),
    )(z, z, z, z, z, dcat, lse, delta, _alibi_row(B, d))


def _dilated_dkv(z, dcat, lse, delta, d, B, name):
    T = z.shape[0]
    hp, groups, nb = _dil_layout(T, B, d)
    cw = hp * HEAD_DIM
    scale = HEAD_DIM ** -0.5
    blk = ATT_BLOCK

    def body(k_ref, v_ref, qa_ref, qb_ref, doa_ref, dob_ref, la_ref, lb_ref, da_ref, db_ref, sl_ref, dk_ref, dv_ref):
        m = pl.program_id(1)
        qi = lax.broadcasted_iota(jnp.int32, (2 * blk, blk), 0)
        kj = lax.broadcasted_iota(jnp.int32, (2 * blk, blk), 1)
        dist = qi - kj
        valid = (dist >= 0) & (dist <= blk) & ((qi < blk) | (m + 1 < nb))
        distf = dist.astype(F32)
        for r in range(d):
            for hh in range(hp):
                cs = pl.ds(hh * HEAD_DIM, HEAD_DIM)

                def both(a_ref, b_ref):
                    return jnp.concatenate([_dil_rows(a_ref, r, d, cs), _dil_rows(b_ref, r, d, cs)], axis=0)

                q = (both(qa_ref, qb_ref) * scale).astype(BF16)
                do = both(doa_ref, dob_ref).astype(BF16)
                k = _dil_rows(k_ref, r, d, cs).astype(BF16)
                v = _dil_rows(v_ref, r, d, cs).astype(BF16)
                s = _dil_scores(q, k, sl_ref[:, cs][:, :1], valid, distf)
                p = jnp.exp(jnp.where(valid, s - both(la_ref, lb_ref)[:, :1], MASKED))
                _dil_put(dv_ref, r, d, cs, _dot(p.astype(BF16), do, TN))
                ds = p * (_dot(do, v, NT) - both(da_ref, db_ref)[:, :1])
                _dil_put(dk_ref, r, d, cs, _dot(ds.astype(BF16), q, TN))

    def col(unit):
        return lambda g, m: (m, unit * groups + g)

    def nxt(unit):
        return lambda g, m: (jnp.minimum(m + 1, nb - 1), unit * groups + g)

    bs = (d * blk, cw)
    out = pl.BlockSpec(bs, lambda g, m: (m, g))
    return pl.pallas_call(
        body, name=name, grid=(groups, nb),
        in_specs=[pl.BlockSpec(bs, col(3)), pl.BlockSpec(bs, col(4)),
                  pl.BlockSpec(bs, col(2)), pl.BlockSpec(bs, nxt(2)),
                  pl.BlockSpec(bs, col(1)), pl.BlockSpec(bs, nxt(1)),
                  out, pl.BlockSpec(bs, nxt(0)), out, pl.BlockSpec(bs, nxt(0)),
                  pl.BlockSpec((1, cw), lambda g, m: (0, g))],
        out_specs=[out, out],
        out_shape=[jax.ShapeDtypeStruct((T, B), F32)] * 2,
        compiler_params=_params("parallel", "parallel"),
    )(z, z, z, z, dcat, dcat, lse, lse, delta, delta, _alibi_row(B, d))


def _dilated_combine(dqs, dks, dvs, name):
    T, B = dqs[0].shape
    scale = HEAD_DIM ** -0.5

    def body(q1, q2, q3, k1, k2, k3, v1, v2, v3, o_ref):
        o_ref[:, pl.ds(0, B)] = ((q1[...] + q2[...] + q3[...]) * scale).astype(BF16)
        o_ref[:, pl.ds(B, B)] = (k1[...] + k2[...] + k3[...]).astype(BF16)
        o_ref[:, pl.ds(2 * B, B)] = (v1[...] + v2[...] + v3[...]).astype(BF16)

    row = pl.BlockSpec((ROWS, B), lambda i: (i, 0))
    return pl.pallas_call(
        body, name=name, grid=(T // ROWS,), in_specs=[row] * 9,
        out_specs=pl.BlockSpec((ROWS, 3 * B), lambda i: (i, 0)),
        out_shape=jax.ShapeDtypeStruct((T, 3 * B), BF16), compiler_params=_params("parallel"),
    )(*dqs, *dks, *dvs)


SB_QUERY_ROWS = 512
SB_KEYS = 2 * ATT_BLOCK


def _tri_and_ones(pred):
    rows = lax.broadcasted_iota(jnp.int32, (2 * ATT_BLOCK, 2 * ATT_BLOCK), 0) % ATT_BLOCK
    cols = lax.broadcasted_iota(jnp.int32, (2 * ATT_BLOCK, 2 * ATT_BLOCK), 1)
    return ((cols >= ATT_BLOCK) | pred(rows, cols)).astype(BF16)


def _running(x, tri):
    hi = x.astype(BF16)
    lo = (x - hi.astype(F32)).astype(BF16)
    return _dot(jnp.concatenate([hi, lo], axis=1), tri)


def _sb_mask(query_rows, s):
    rows = lax.broadcasted_iota(jnp.int32, (query_rows, SB_KEYS), 0)
    cols = lax.broadcasted_iota(jnp.int32, (query_rows, SB_KEYS), 1)
    return cols + s * SB_KEYS < rows


def _log_sigmoids(z):
    e = jnp.exp(-jnp.abs(z))
    ls = jnp.minimum(z, 0.0) - jnp.log(1.0 + e)
    return ls, ls - z, e


def _sb_fwd(qkv, W, name):
    T = qkv.shape[0]
    H = W // HEAD_DIM
    blk = ATT_BLOCK
    qb = min(SB_QUERY_ROWS, T)
    per = qb // SB_KEYS

    def body(q_ref, k_ref, v_ref, o_ref, lt_ref, acc_ref):
        i = pl.program_id(1)
        q = q_ref[...]
        tri = _tri_and_ones(lambda r, c: r > c)
        lt_ref[...] = jnp.zeros_like(lt_ref)
        acc_ref[...] = jnp.zeros_like(acc_ref)

        def tile(j, mask):
            ks = pl.ds(pl.multiple_of(j * SB_KEYS, SB_KEYS), SB_KEYS)
            z = _dot(q, k_ref[ks, :], NT)
            ls, lm, _ = _log_sigmoids(z)
            if mask is not None:
                lm = jnp.where(mask, lm, 0.0)
            later = lt_ref[...]
            second = _running(lm[:, blk:], tri)
            first = _running(lm[:, :blk], tri)
            after_first = later + second[:, blk:]
            a = jnp.exp(ls + jnp.concatenate([first[:, :blk] + after_first, second[:, :blk] + later], axis=1))
            if mask is not None:
                a = jnp.where(mask, a, 0.0)
            acc_ref[...] += _dot(a.astype(BF16), v_ref[ks, :])
            lt_ref[...] = after_first + first[:, blk:]

        for s in reversed(range(per)):
            tile(i * per + s, _sb_mask(qb, s))

        def step(jj, _):
            for s in range(per):
                tile((i - jj) * per - 1 - s, None)
            return 0

        lax.fori_loop(0, i, step, 0)
        o_ref[...] = acc_ref[...].astype(BF16)

    qs = pl.BlockSpec((qb, HEAD_DIM), lambda h, i: (i, h))
    return pl.pallas_call(
        body, name=name, grid=(H, T // qb),
        in_specs=[qs, pl.BlockSpec((T, HEAD_DIM), lambda h, i: (0, H + h)),
                  pl.BlockSpec((T, HEAD_DIM), lambda h, i: (0, 2 * H + h))],
        out_specs=[qs, qs],
        out_shape=[jax.ShapeDtypeStruct((T, W), BF16), jax.ShapeDtypeStruct((T, W), F32)],
        scratch_shapes=[pltpu.VMEM((qb, HEAD_DIM), F32)],
        compiler_params=_params("parallel", "arbitrary"),
    )(qkv, qkv, qkv)


def _sb_bwd(qkv, do, ltot, W, name):
    T = qkv.shape[0]
    H = W // HEAD_DIM
    blk = ATT_BLOCK
    nkb = T // SB_KEYS
    qb = min(SB_QUERY_ROWS, T)
    per = qb // SB_KEYS

    def body(q_ref, k_ref, v_ref, do_ref, lt_ref, dq_ref, dkt_ref, dvt_ref, qt_ref, dot_ref, plm_ref, pg_ref):
        i = pl.program_id(1)

        @pl.when(i == 0)
        def _():
            dkt_ref[...] = jnp.zeros_like(dkt_ref)
            dvt_ref[...] = jnp.zeros_like(dvt_ref)

        q = q_ref[...]
        do = do_ref[...]
        qt_ref[...] = q.astype(F32).T.astype(BF16)
        dot_ref[...] = do.astype(F32).T.astype(BF16)
        upto = _tri_and_ones(lambda r, c: r <= c)
        before = _tri_and_ones(lambda r, c: r < c)
        plm_ref[...] = jnp.zeros_like(plm_ref)
        pg_ref[...] = jnp.zeros_like(pg_ref)
        dq_ref[...] = jnp.zeros_like(dq_ref)

        def tile(j, mask):
            ks = pl.ds(pl.multiple_of(j * SB_KEYS, SB_KEYS), SB_KEYS)
            k = k_ref[ks, :]
            v = v_ref[ks, :]
            z = _dot(q, k, NT)
            ls, lm, e = _log_sigmoids(z)
            if mask is not None:
                lm = jnp.where(mask, lm, 0.0)
            earlier = plm_ref[...]
            first = _running(lm[:, :blk], upto)
            second = _running(lm[:, blk:], upto)
            upto_first = earlier + first[:, blk:]
            seen = jnp.concatenate([first[:, :blk] + earlier, second[:, :blk] + upto_first], axis=1)
            ltot = lt_ref[...]
            a = jnp.exp(ls + (jnp.concatenate([ltot, ltot], axis=1) - seen))
            if mask is not None:
                a = jnp.where(mask, a, 0.0)
            g = a * _dot(do, v, NT)
            g_earlier = pg_ref[...]
            g_first = _running(g[:, :blk], before)
            g_second = _running(g[:, blk:], before)
            g_upto_first = g_earlier + g_first[:, blk:]
            gsum = jnp.concatenate([g_first[:, :blk] + g_earlier, g_second[:, :blk] + g_upto_first], axis=1)
            r = 1.0 / (1.0 + e)
            pos = z >= 0.0
            sig = jnp.where(pos, r, e * r)
            nsig = jnp.where(pos, e * r, r)
            dz = g * nsig - gsum * sig
            if mask is not None:
                dz = jnp.where(mask, dz, 0.0)
            dzb = dz.astype(BF16)
            dkt_ref[j] += _dot(qt_ref[...], dzb)
            dvt_ref[j] += _dot(dot_ref[...], a.astype(BF16))
            dq_ref[...] += _dot(dzb, k)
            plm_ref[...] = upto_first + second[:, blk:]
            pg_ref[...] = g_upto_first + g_second[:, blk:]

        def step(jj, _):
            for s in range(per):
                tile(jj * per + s, None)
            return 0

        lax.fori_loop(0, i, step, 0)
        for s in range(per):
            tile(i * per + s, _sb_mask(qb, s))

    qs = pl.BlockSpec((qb, HEAD_DIM), lambda h, i: (i, h))
    res = pl.BlockSpec((None, nkb, HEAD_DIM, SB_KEYS), lambda h, i: (h, 0, 0, 0))
    return pl.pallas_call(
        body, name=name, grid=(H, T // qb),
        in_specs=[qs, pl.BlockSpec((T, HEAD_DIM), lambda h, i: (0, H + h)),
                  pl.BlockSpec((T, HEAD_DIM), lambda h, i: (0, 2 * H + h)), qs, qs],
        out_specs=[qs, res, res],
        out_shape=[jax.ShapeDtypeStruct((T, W), F32)] + [jax.ShapeDtypeStruct((H, nkb, HEAD_DIM, SB_KEYS), F32)] * 2,
        scratch_shapes=[pltpu.VMEM((HEAD_DIM, qb), BF16), pltpu.VMEM((HEAD_DIM, qb), BF16),
                        pltpu.VMEM((qb, HEAD_DIM), F32), pltpu.VMEM((qb, HEAD_DIM), F32)],
        compiler_params=_params("parallel", "arbitrary"),
    )(qkv, qkv, qkv, do, ltot)


def _sb_pack(dq, dkt, dvt, name):
    T, W = dq.shape
    H = W // HEAD_DIM
    blk = SB_KEYS
    scale = HEAD_DIM ** -0.5

    def body(q_ref, kt_ref, vt_ref, o_ref):
        o_ref[:, pl.ds(0, W)] = (q_ref[...] * scale).astype(BF16)
        for h in range(H):
            o_ref[:, pl.ds(W + h * HEAD_DIM, HEAD_DIM)] = kt_ref[h].T.astype(BF16)
            o_ref[:, pl.ds(2 * W + h * HEAD_DIM, HEAD_DIM)] = vt_ref[h].T.astype(BF16)

    tr = pl.BlockSpec((H, None, HEAD_DIM, blk), lambda i: (0, i, 0, 0))
    return pl.pallas_call(
        body, name=name, grid=(T // blk,), in_specs=[pl.BlockSpec((blk, W), lambda i: (i, 0)), tr, tr],
        out_specs=pl.BlockSpec((blk, 3 * W), lambda i: (i, 0)),
        out_shape=jax.ShapeDtypeStruct((T, 3 * W), BF16), compiler_params=_params("parallel"),
    )(dq, dkt, dvt)


def _local_step(x, target, norms, sgu, comm):
    T, D = x.shape
    A = D // 2
    pre_mix, post_mix, pre_ffn, post_ffn = norms
    ln_g, ln_b, w_s, b_s = sgu
    b_t = b_s.T
    scale = HEAD_DIM ** -0.5

    def vec(p, layer):
        return comm.tie(p[layer:layer + 1])

    h0 = _prenorm(x, vec(pre_mix, 0), "prenorm0")
    z = _matmul(h0, comm.weight("ab_in"), mode="nn", name="ab_in_fwd")
    a_out = _sgu_fwd(z, ln_g, ln_b, w_s, b_t, "sgu_fwd")
    branch = [_dilated_fwd(z, d, A, f"dilated_fwd_{d}") for _, d in DILATED_PAIRS]
    b_out, o_dil, lse_dil = _dilated_merge([b[0] for b in branch], [b[1] for b in branch], "dilated_merge")
    cat = jnp.concatenate([a_out, b_out], axis=-1)
    y0 = _matmul(cat, comm.weight("ab_out"), mode="nn", name="ab_out_fwd")
    comm.arrive("ffn0", after=y0)
    x1, h1 = _postnorm_prenorm(x, y0, vec(post_mix, 0), vec(pre_ffn, 0), "norm_mix0")
    comm.land("ffn0", after=h1)

    def relu2(acc, j):
        r = jnp.maximum(acc, 0.0)
        return r * r

    f0 = _matmul(h1, comm.weight("w1_0"), mode="nn", name="ffn0_w1_fwd", out_dtype=BF16, epi=relu2)
    y1 = _matmul(f0, comm.weight("w2_0"), mode="nn", name="ffn0_w2_fwd")
    comm.arrive("rest", after=y1)
    x2, h2 = _postnorm_prenorm(x1, y1, vec(post_ffn, 0), vec(pre_mix, 1), "norm_ffn0")
    comm.land("rest", after=h2)

    tn_qkv = _tile(D, 1024)
    nq = D // tn_qkv

    def scale_q(acc, j):
        return jnp.where(j < nq, acc * scale, acc)

    qkv = _matmul(h2, comm.weight("sb_in"), mode="nn", name="sb_in_fwd", out_dtype=BF16, tn=tn_qkv, epi=scale_q)
    o_sb, ltot = _sb_fwd(qkv, D, "sb_fwd")
    y2 = _matmul(o_sb, comm.weight("sb_out"), mode="nn", name="sb_out_fwd")
    x3, h3 = _postnorm_prenorm(x2, y2, vec(post_mix, 1), vec(pre_ffn, 1), "norm_mix1")
    f1 = _matmul(h3, comm.weight("w1_1"), mode="nn", name="ffn1_w1_fwd", out_dtype=BF16, epi=relu2)
    y3 = _matmul(f1, comm.weight("w2_1"), mode="nn", name="ffn1_w2_fwd")
    loss_tile, dx4 = _postnorm_loss(x3, y3, vec(post_ffn, 1), target, "norm_loss")
    loss = loss_tile[0, 0]

    def relu2_bwd(acc, j, f):
        return acc * (2.0 * jnp.sqrt(f.astype(F32)))

    def ffn_bwd(dx_out, x_in, h, f, y, layer, tag):
        dy, dg_post = _postnorm_bwd(dx_out, y, vec(post_ffn, layer), f"ffn{tag}_postnorm_bwd")
        g_w2 = _matmul(f, dy, mode="tn", name=f"ffn{tag}_w2_wgrad", tk=2048)
        da = _matmul(dy, comm.weight(f"w2_{layer}"), mode="nt", name=f"ffn{tag}_w2_dgrad", out_dtype=BF16,
                     epi=relu2_bwd, extras=(f,))
        g_w1 = _matmul(h, da, mode="tn", name=f"ffn{tag}_w1_wgrad", tk=2048)
        comm.reduce(f"ffn{tag}", {f"w2_{layer}": g_w2, f"w1_{layer}": g_w1})
        dh = _matmul(da, comm.weight(f"w1_{layer}"), mode="nt", name=f"ffn{tag}_w1_dgrad", after=comm.started())
        dx_in, dg_pre = _prenorm_bwd(dx_out, dh, x_in, vec(pre_ffn, layer), f"ffn{tag}_prenorm_bwd")
        return dx_in, dg_pre, dg_post

    dx3, dg_pre_ffn1, dg_post_ffn1 = ffn_bwd(dx4, x3, h3, f1, y3, 1, "1")

    dy2, dg_post_mix1 = _postnorm_bwd(dx3, y2, vec(post_mix, 1), "sb_postnorm_bwd")
    g_sb_out = _matmul(o_sb, dy2, mode="tn", name="sb_out_wgrad", tk=2048)
    do_sb = _matmul(dy2, comm.weight("sb_out"), mode="nt", name="sb_out_dgrad", out_dtype=BF16)
    dq, dk, dv = _sb_bwd(qkv, do_sb, ltot, D, "sb_bwd")
    dqkv = _sb_pack(dq, dk, dv, "sb_pack")
    g_sb_in = _matmul(h2, dqkv, mode="tn", name="sb_in_wgrad", tk=2048)
    comm.reduce("sb", {"sb_out": g_sb_out, "sb_in": g_sb_in})
    dh2 = _matmul(dqkv, comm.weight("sb_in"), mode="nt", name="sb_in_dgrad", after=comm.started())
    dx2, dg_pre_mix1 = _prenorm_bwd(dx3, dh2, x2, vec(pre_mix, 1), "sb_prenorm_bwd")

    dx1, dg_pre_ffn0, dg_post_ffn0 = ffn_bwd(dx2, x1, h1, f0, y1, 0, "0")

    dy0, dg_post_mix0 = _postnorm_bwd(dx1, y0, vec(post_mix, 0), "ab_postnorm_bwd")
    g_ab_out = _matmul(cat, dy0, mode="tn", name="ab_out_wgrad", tk=2048)
    dcat = _matmul(dy0, comm.weight("ab_out"), mode="nt", name="ab_out_dgrad")
    duv, d_ln_g, d_ln_b, d_w_s, d_b_t = _sgu_bwd(z, dcat, ln_g, ln_b, w_s, b_t, "sgu_bwd")
    delta = _dilated_delta(dcat, o_dil, "dilated_delta")
    dqs, dks, dvs = [], [], []
    for _, d in DILATED_PAIRS:
        dqs.append(_dilated_dq(z, dcat, lse_dil, delta, d, A, f"dilated_dq_{d}"))
        dk_b, dv_b = _dilated_dkv(z, dcat, lse_dil, delta, d, A, f"dilated_dkv_{d}")
        dks.append(dk_b)
        dvs.append(dv_b)
    dqkv0 = _dilated_combine(dqs, dks, dvs, "dilated_combine")
    dz = jnp.concatenate([duv, dqkv0], axis=-1)
    g_ab_in = _matmul(h0, dz, mode="tn", name="ab_in_wgrad", tk=2048)
    comm.reduce("ab", {"ab_out": g_ab_out, "ab_in": g_ab_in})
    dh0 = _matmul(dz, comm.weight("ab_in"), mode="nt", name="ab_in_dgrad", after=comm.started())
    dx0, dg_pre_mix0 = _prenorm_bwd(dx1, dh0, x, vec(pre_mix, 0), "ab_prenorm_bwd")

    small = {
        "pre_mix": jnp.concatenate([dg_pre_mix0, dg_pre_mix1], axis=0),
        "post_mix": jnp.concatenate([dg_post_mix0, dg_post_mix1], axis=0),
        "pre_ffn": jnp.concatenate([dg_pre_ffn0, dg_pre_ffn1], axis=0),
        "post_ffn": jnp.concatenate([dg_post_ffn0, dg_post_ffn1], axis=0),
        "ln_g": d_ln_g, "ln_b": d_ln_b, "w_s": d_w_s, "b_s": d_b_t.T,
    }
    return loss, dx0, small


MESH_ID = pl.DeviceIdType.MESH
ANY = pl.BlockSpec(memory_space=pl.ANY)


def _coords():
    return lax.axis_index("x"), lax.axis_index("y"), lax.axis_index("c")


def _shard_of(ref, kind, p):
    if kind == "col":
        n = ref.shape[1] // N_DEV
        return ref.at[:, pl.ds(pl.multiple_of(p * n, 128), n)]
    r = ref.shape[0] // N_DEV
    return ref.at[pl.ds(pl.multiple_of(p * r, 16), r), :]


def _full_shape(shard, kind):
    if kind == "col":
        return (shard.shape[0], shard.shape[1] * N_DEV)
    return (shard.shape[0] * N_DEV, shard.shape[1])


def _place(shard, kind, block, name):
    rows, cols = shard.shape
    tr = _tile(rows, 512)

    def body(b_ref, s_ref, o_ref):
        o_ref[...] = s_ref[...].astype(BF16)

    if kind == "col":
        out = pl.BlockSpec((tr, cols), lambda i, b_ref: (i, b_ref[0]))
    else:
        out = pl.BlockSpec((tr, cols), lambda i, b_ref: (b_ref[0] * (rows // tr) + i, 0))
    return pl.pallas_call(
        body, name=name,
        grid_spec=pltpu.PrefetchScalarGridSpec(
            num_scalar_prefetch=1, grid=(rows // tr,),
            in_specs=[pl.BlockSpec((tr, cols), lambda i, b_ref: (i, 0))], out_specs=out),
        out_shape=jax.ShapeDtypeStruct(_full_shape(shard, kind), BF16),
        compiler_params=_params("parallel"),
    )(block, shard)


def _all_gather(fulls, kinds):
    nt = len(fulls)

    def body(*refs):
        ins = refs[:nt]
        send_sems, recv_sems = refs[2 * nt:]
        x, y, c = _coords()
        me, sibling = (x, y, c), (x, y, 1 - c)
        chips = [(1 - x, y), (x, 1 - y), (1 - x, 1 - y)]

        def copy(t, k, block, to):
            px, py, pc = block
            slot = _shard_of(ins[t], kinds[t], 4 * px + 2 * py + pc)
            return pltpu.make_async_remote_copy(
                src_ref=slot, dst_ref=slot, send_sem=send_sems.at[7 * t + k], recv_sem=recv_sems.at[7 * t + k],
                device_id=to, device_id_type=MESH_ID)

        sent = []
        for t in range(nt):
            first = [copy(t, 0, me, sibling)] + [copy(t, 1 + j, me, (*chip, c)) for j, chip in enumerate(chips)]
            for cp in first:
                cp.start()
            sent += first
        for t in range(nt):
            for j, chip in enumerate(chips):
                copy(t, 1 + j, (*chip, c), me).wait_recv()
                sent.append(copy(t, 4 + j, (*chip, c), sibling))
                sent[-1].start()
        for t in range(nt):
            copy(t, 0, sibling, me).wait_recv()
            for j, chip in enumerate(chips):
                copy(t, 4 + j, (*chip, 1 - c), me).wait_recv()
        for cp in sent:
            cp.wait_send()

    return pl.pallas_call(
        body, name="all_gather_weights",
        in_specs=[ANY] * nt, out_specs=[ANY] * nt,
        out_shape=[jax.ShapeDtypeStruct(f.shape, f.dtype) for f in fulls],
        input_output_aliases={t: t for t in range(nt)},
        scratch_shapes=[pltpu.SemaphoreType.DMA((7 * nt,)), pltpu.SemaphoreType.DMA((7 * nt,))],
    )(*fulls)


HBM = pl.BlockSpec(memory_space=pltpu.HBM)
SEM = pl.BlockSpec(memory_space=pltpu.SEMAPHORE)
FLOWS = pltpu.SideEffectType.DATAFLOW_SIDE_EFFECTING


def _in_hbm(a):
    return pltpu.with_memory_space_constraint(a, pltpu.HBM)


def _hbm_like(bufs):
    return [pltpu.HBM(b.shape, b.dtype) for b in bufs]


def _copies_start(name, bufs, plan, n, after):
    nb = len(bufs)

    def body(*refs):
        send_sems, recv_sems, token = refs[nb + 1], refs[nb + 2], refs[-1]
        for cp in plan(refs[:nb], send_sems, recv_sems):
            cp.start()
        token[...] = jnp.zeros_like(token)

    out = pl.pallas_call(
        body, name=name, in_specs=[HBM] * nb + [ANY],
        out_specs=[SEM, SEM] + [HBM] * nb + [pl.BlockSpec(memory_space=pltpu.VMEM)],
        out_shape=[pltpu.SemaphoreType.DMA((n,)), pltpu.SemaphoreType.DMA((n,))] + _hbm_like(bufs)
        + [jax.ShapeDtypeStruct((8, 128), F32)],
        input_output_aliases={i: 2 + i for i in range(nb)},
        compiler_params=pltpu.CompilerParams(has_side_effects=FLOWS),
    )(*[_in_hbm(b) for b in bufs], after)
    return (out[0], out[1]), list(out[2:2 + nb]), out[-1]


def _copies_wait(name, bufs, sems, after, plan):
    nb = len(bufs)

    def body(*refs):
        for cp in plan(refs[:nb], refs[nb], refs[nb + 1]):
            cp.wait_send()
            cp.wait_recv()

    out = pl.pallas_call(
        body, name=name, in_specs=[HBM] * nb + [SEM, SEM, ANY], out_specs=[HBM] * nb,
        out_shape=_hbm_like(bufs), input_output_aliases={i: i for i in range(nb)},
        compiler_params=pltpu.CompilerParams(has_side_effects=FLOWS),
    )(*bufs, *sems, after)
    return list(out)


def _copies_wait_start(name, bufs, sems, after, plan, next_plan, n_next):
    nb = len(bufs)

    def body(*refs):
        ins = refs[:nb]
        for cp in plan(ins, refs[nb], refs[nb + 1]):
            cp.wait_send()
            cp.wait_recv()
        send_sems, recv_sems, token = refs[nb + 3], refs[nb + 4], refs[-1]
        for cp in next_plan(ins, send_sems, recv_sems):
            cp.start()
        token[...] = jnp.zeros_like(token)

    out = pl.pallas_call(
        body, name=name, in_specs=[HBM] * nb + [SEM, SEM, ANY],
        out_specs=[SEM, SEM] + [HBM] * nb + [pl.BlockSpec(memory_space=pltpu.VMEM)],
        out_shape=[pltpu.SemaphoreType.DMA((n_next,)), pltpu.SemaphoreType.DMA((n_next,))] + _hbm_like(bufs)
        + [jax.ShapeDtypeStruct((8, 128), F32)],
        input_output_aliases={i: 2 + i for i in range(nb)},
        compiler_params=pltpu.CompilerParams(has_side_effects=FLOWS),
    )(*bufs, *sems, after)
    return (out[0], out[1]), list(out[2:2 + nb]), out[-1]


def _gather_plans(kinds):
    nt = len(kinds)

    def slot(refs, t, px, py, pc):
        return _shard_of(refs[t], kinds[t], 4 * px + 2 * py + pc)

    def to_chips(refs, send_sems, recv_sems):
        x, y, c = _coords()
        peers = [(x, y, 1 - c), (1 - x, y, c), (x, 1 - y, c), (1 - x, 1 - y, c)]
        return [pltpu.make_async_remote_copy(
            src_ref=slot(refs, t, x, y, c), dst_ref=slot(refs, t, x, y, c), send_sem=send_sems.at[4 * t + k],
            recv_sem=recv_sems.at[4 * t + k], device_id=peer, device_id_type=MESH_ID)
            for t in range(nt) for k, peer in enumerate(peers)]

    def to_sibling(refs, send_sems, recv_sems):
        x, y, c = _coords()
        chips = [(1 - x, y), (x, 1 - y), (1 - x, 1 - y)]
        return [pltpu.make_async_remote_copy(
            src_ref=slot(refs, t, *chip, c), dst_ref=slot(refs, t, *chip, c), send_sem=send_sems.at[3 * t + j],
            recv_sem=recv_sems.at[3 * t + j], device_id=(x, y, 1 - c), device_id_type=MESH_ID)
            for t in range(nt) for j, chip in enumerate(chips)]

    return to_chips, to_sibling


def _shard_shape(full, kind):
    if kind == "col":
        return (full.shape[0], full.shape[1] // N_DEV)
    return (full.shape[0] // N_DEV, full.shape[1])


CHIPS = ((0, 0), (0, 1), (1, 0), (1, 1))


def _exchange_siblings(grads, kinds, name):
    nt = len(grads)

    def body(*refs):
        ins, outs = refs[:nt], refs[nt:2 * nt]
        send_sems, recv_sems = refs[2 * nt:]
        x, y, c = _coords()
        copies = []
        for t in range(nt):
            for q, (qx, qy) in enumerate(CHIPS):
                copies.append(pltpu.make_async_remote_copy(
                    src_ref=_shard_of(ins[t], kinds[t], 4 * qx + 2 * qy + (1 - c)), dst_ref=outs[t].at[q],
                    send_sem=send_sems.at[4 * t + q], recv_sem=recv_sems.at[4 * t + q],
                    device_id=(x, y, 1 - c), device_id_type=MESH_ID))
        for cp in copies:
            cp.start()
        for cp in copies:
            cp.wait()

    return pl.pallas_call(
        body, name=name,
        in_specs=[ANY] * nt, out_specs=[ANY] * nt,
        out_shape=[jax.ShapeDtypeStruct((4,) + _shard_shape(g, k), F32) for g, k in zip(grads, kinds)],
        scratch_shapes=[pltpu.SemaphoreType.DMA((4 * nt,)), pltpu.SemaphoreType.DMA((4 * nt,))],
    )(*grads)


def _pair_sum(grad, other, kind, where, name):
    rows, cols = _shard_shape(grad, kind)
    tr = _tile(rows, 256)

    def body(w_ref, g_ref, o_ref, s_ref, land_ref):
        s = (g_ref[...] + o_ref[...]).astype(BF16)
        s_ref[...] = s

        @pl.when(pl.program_id(1) == w_ref[1])
        def _():
            land_ref[...] = s

    if kind == "col":
        g_spec = pl.BlockSpec((tr, cols), lambda i, q, w_ref: (i, 2 * q + w_ref[0]))
    else:
        g_spec = pl.BlockSpec((tr, cols), lambda i, q, w_ref: ((2 * q + w_ref[0]) * (rows // tr) + i, 0))
    part = pl.BlockSpec((None, tr, cols), lambda i, q, w_ref: (q, i, 0))
    mine = pl.BlockSpec((None, tr, cols), lambda i, q, w_ref: (w_ref[1], i, 0))
    return pl.pallas_call(
        body, name=name,
        grid_spec=pltpu.PrefetchScalarGridSpec(
            num_scalar_prefetch=1, grid=(rows // tr, 4), in_specs=[g_spec, part], out_specs=[part, mine]),
        out_shape=[jax.ShapeDtypeStruct((4, rows, cols), BF16)] * 2,
        compiler_params=_params("parallel", "arbitrary"),
    )(where, grad, other)


def _scatter_plan(nt):
    def plan(refs, send_sems, recv_sems):
        x, y, c = _coords()
        chips = [(1 - x, y), (x, 1 - y), (1 - x, 1 - y)]
        return [pltpu.make_async_remote_copy(
            src_ref=refs[t].at[2 * qx + qy], dst_ref=refs[nt + t].at[2 * x + y], send_sem=send_sems.at[3 * t + j],
            recv_sem=recv_sems.at[3 * t + j], device_id=(qx, qy, c), device_id_type=MESH_ID)
            for t in range(nt) for j, (qx, qy) in enumerate(chips)]
    return plan


def _all_reduce_small(vec):
    R = vec.shape[0]

    def body(v_ref, o_ref, recv_ref, send_sems, recv_sems):
        x, y, c = _coords()
        me = 4 * x + 2 * y + c
        recv_ref[me] = v_ref[...]
        copies = []
        for k in range(1, N_DEV):
            bx, by, bc = (k >> 2) & 1, (k >> 1) & 1, k & 1
            peer = (1 - x if bx else x, 1 - y if by else y, 1 - c if bc else c)
            copies.append(pltpu.make_async_remote_copy(
                src_ref=v_ref, dst_ref=recv_ref.at[me],
                send_sem=send_sems.at[k - 1], recv_sem=recv_sems.at[k - 1],
                device_id=peer, device_id_type=MESH_ID))
        for cp in copies:
            cp.start()
        for cp in copies:
            cp.wait()
        total = recv_ref[0]
        for p in range(1, N_DEV):
            total = total + recv_ref[p]
        o_ref[...] = total

    return pl.pallas_call(
        body, name="all_reduce_small",
        in_specs=[pl.BlockSpec(memory_space=pltpu.VMEM)], out_specs=pl.BlockSpec(memory_space=pltpu.VMEM),
        out_shape=jax.ShapeDtypeStruct((R, 128), F32),
        scratch_shapes=[pltpu.VMEM((N_DEV, R, 128), F32), pltpu.SemaphoreType.DMA((N_DEV - 1,)),
                        pltpu.SemaphoreType.DMA((N_DEV - 1,))],
        compiler_params=pltpu.CompilerParams(vmem_limit_bytes=VMEM_LIMIT),
    )(vec)


def _adamw_math(w, g, m, v):
    m = ADAM_B1 * m + (1.0 - ADAM_B1) * g
    v = ADAM_B2 * v + (1.0 - ADAM_B2) * (g * g)
    m_hat = m / (1.0 - ADAM_B1 ** ADAM_STEP)
    v_hat = v / (1.0 - ADAM_B2 ** ADAM_STEP)
    delta = -ADAM_LR * (m_hat / (jnp.sqrt(v_hat) + ADAM_EPS) + ADAM_WD * w)
    return delta, m, v


def _adamw(w, parts, m, v, name):
    layers, rows, cols = w.shape
    tr = rows if rows * cols <= 256 * 1024 else _tile(rows, 128)
    out = None
    for layer in range(layers):
        n = parts[layer].shape[0]

        def body(w_ref, p_ref, m_ref, v_ref, *rest, n=n):
            g_ref, d_ref, mo_ref, vo_ref = rest[-4:]
            g = p_ref[0].astype(F32)
            for q in range(1, n):
                g = g + p_ref[q].astype(F32)
            g_ref[...] = g
            d_ref[...], mo_ref[...], vo_ref[...] = _adamw_math(w_ref[...], g, m_ref[...], v_ref[...])

        blk = pl.BlockSpec((None, tr, cols), lambda i, layer=layer: (layer, i, 0))
        earlier = [] if out is None else list(out)
        out = pl.pallas_call(
            body, name=f"{name}_{layer}", grid=(rows // tr,),
            in_specs=[blk, pl.BlockSpec((n, tr, cols), lambda i: (0, i, 0)), blk, blk] + [ANY] * len(earlier),
            out_specs=[blk] * 4, out_shape=[jax.ShapeDtypeStruct((layers, rows, cols), F32)] * 4,
            input_output_aliases={4 + k: k for k in range(len(earlier))},
            compiler_params=_params("parallel"),
        )(w, parts[layer], m, v, *earlier)
    return out


def _pack(arrays):
    rows = []
    for a in arrays:
        flat = a.reshape(-1)
        pad = (-flat.shape[0]) % 1024
        rows.append(jnp.pad(flat, (0, pad)).reshape(-1, 128))
    return jnp.concatenate(rows, axis=0)


def _unpack(packed, like):
    out, r = [], 0
    for a in like:
        n = math.prod(a.shape)
        nr = (n + 1023) // 1024 * 8
        out.append(packed[r:r + nr].reshape(-1)[:n].reshape(a.shape))
        r += nr
    return out


KIND = {"ab_in": "col", "ab_out": "row", "sb_in": "col", "sb_out": "row",
        "w1_0": "col", "w1_1": "col", "w2_0": "row", "w2_1": "row"}
GATHER_FIRST = ("ab_in", "ab_out")
GATHER_LATER = {"ffn0": ("w1_0", "w2_0"),
                "rest": ("sb_in", "sb_out", "w1_1", "w2_1")}


class _Exchange:
    def __init__(self, shards):
        x, y, c = _coords()
        self.where = jnp.stack([c, 2 * x + y]).astype(jnp.int32)
        block = (4 * x + 2 * y + c).astype(jnp.int32).reshape(1)
        self.full = {n: _place(s, KIND[n], block, f"place_{n}") for n, s in shards.items()}
        first = _all_gather([self.full[n] for n in GATHER_FIRST], [KIND[n] for n in GATHER_FIRST])
        self.full.update(zip(GATHER_FIRST, first))
        self.tokens = []
        self.gathers = {}
        self.scatters = {}
        self.landed = {}
        after = first[-1]
        for key, group in GATHER_LATER.items():
            to_chips, to_sibling = _gather_plans([KIND[n] for n in group])
            bufs = [self.full[n] for n in group]
            sems, bufs, after = _copies_start(f"gather_start_{key}", bufs, to_chips, 4 * len(group), after)
            self.tokens.append(after)
            self.gathers[key] = (group, sems, bufs, to_chips, to_sibling)

    def tie(self, small):
        for token in self.tokens:
            small = small + token[0:1, 0:1]
        self.tokens = []
        return small

    def started(self):
        return tuple(self.tokens)

    def weight(self, name):
        return self.full[name]

    def arrive(self, key, after):
        group, sems, bufs, to_chips, to_sibling = self.gathers[key]
        sems, bufs, token = _copies_wait_start(f"gather_pass_{key}", bufs, sems, after, to_chips, to_sibling,
                                               3 * len(group))
        self.tokens.append(token)
        self.gathers[key] = (group, sems, bufs, to_chips, to_sibling)

    def land(self, key, after):
        group, sems, bufs, _, to_sibling = self.gathers.pop(key)
        self.full.update(zip(group, _copies_wait(f"gather_done_{key}", bufs, sems, after, to_sibling)))

    def reduce(self, key, grads):
        names = list(grads)
        kinds = [KIND[n] for n in names]
        others = _exchange_siblings([grads[n] for n in names], kinds, f"grad_siblings_{key}")
        pairs = [_pair_sum(grads[n], o, k, self.where, f"pair_sum_{n}") for n, o, k in zip(names, others, kinds)]
        parts, lands = [p[0] for p in pairs], [p[1] for p in pairs]
        plan = _scatter_plan(len(names))
        sems, bufs, token = _copies_start(f"scatter_start_{key}", parts + lands, plan, 3 * len(names), parts[0])
        self.tokens.append(token)
        self.scatters[key] = (names, sems, bufs, plan)

    def finish(self, after):
        for key, (names, sems, bufs, plan) in self.scatters.items():
            bufs = _copies_wait(f"scatter_done_{key}", bufs, sems, after, plan)
            self.landed.update(zip(names, bufs[len(names):]))
        self.scatters = {}
        return self.landed


SMALL = ("norm_pre_mix", "norm_post_mix", "norm_pre_ffn", "norm_post_ffn", "sgu_ln_g", "sgu_ln_b", "sgu_w", "sgu_b")
ORDER = ("norm_pre_mix", "norm_post_mix", "norm_pre_ffn", "norm_post_ffn", "ab_w_in", "sgu_ln_g", "sgu_ln_b", "sgu_w",
         "sgu_b", "ab_w_out", "sb_w_in", "sb_w_out", "ffn_w1", "ffn_w2")


def kernel(x, norm_pre_mix, norm_post_mix, norm_pre_ffn, norm_post_ffn, ab_w_in, sgu_ln_g, sgu_ln_b, sgu_w, sgu_b, ab_w_out, sb_w_in, sb_w_out, ffn_w1, ffn_w2, loss_target, m_norm_pre_mix, m_norm_post_mix, m_norm_pre_ffn, m_norm_post_ffn, m_ab_w_in, m_sgu_ln_g, m_sgu_ln_b, m_sgu_w, m_sgu_b, m_ab_w_out, m_sb_w_in, m_sb_w_out, m_ffn_w1, m_ffn_w2, v_norm_pre_mix, v_norm_post_mix, v_norm_pre_ffn, v_norm_post_ffn, v_ab_w_in, v_sgu_ln_g, v_sgu_ln_b, v_sgu_w, v_sgu_b, v_ab_w_out, v_sb_w_in, v_sb_w_out, v_ffn_w1, v_ffn_w2):
    W = dict(norm_pre_mix=norm_pre_mix, norm_post_mix=norm_post_mix, norm_pre_ffn=norm_pre_ffn,
             norm_post_ffn=norm_post_ffn, ab_w_in=ab_w_in, sgu_ln_g=sgu_ln_g, sgu_ln_b=sgu_ln_b, sgu_w=sgu_w,
             sgu_b=sgu_b, ab_w_out=ab_w_out, sb_w_in=sb_w_in, sb_w_out=sb_w_out, ffn_w1=ffn_w1, ffn_w2=ffn_w2)
    M = dict(norm_pre_mix=m_norm_pre_mix, norm_post_mix=m_norm_post_mix, norm_pre_ffn=m_norm_pre_ffn,
             norm_post_ffn=m_norm_post_ffn, ab_w_in=m_ab_w_in, sgu_ln_g=m_sgu_ln_g, sgu_ln_b=m_sgu_ln_b,
             sgu_w=m_sgu_w, sgu_b=m_sgu_b, ab_w_out=m_ab_w_out, sb_w_in=m_sb_w_in, sb_w_out=m_sb_w_out,
             ffn_w1=m_ffn_w1, ffn_w2=m_ffn_w2)
    V = dict(norm_pre_mix=v_norm_pre_mix, norm_post_mix=v_norm_post_mix, norm_pre_ffn=v_norm_pre_ffn,
             norm_post_ffn=v_norm_post_ffn, ab_w_in=v_ab_w_in, sgu_ln_g=v_sgu_ln_g, sgu_ln_b=v_sgu_ln_b,
             sgu_w=v_sgu_w, sgu_b=v_sgu_b, ab_w_out=v_ab_w_out, sb_w_in=v_sb_w_in, sb_w_out=v_sb_w_out,
             ffn_w1=v_ffn_w1, ffn_w2=v_ffn_w2)

    shards = {"ab_in": ab_w_in[0], "ab_out": ab_w_out[0], "w1_0": ffn_w1[0], "w2_0": ffn_w2[0],
              "sb_in": sb_w_in[0], "sb_out": sb_w_out[0], "w1_1": ffn_w1[1], "w2_1": ffn_w2[1]}
    comm = _Exchange(shards)
    norms = (norm_pre_mix, norm_post_mix, norm_pre_ffn, norm_post_ffn)
    sgu = (sgu_ln_g, sgu_ln_b, sgu_w[0], sgu_b[0])
    loss, dx, small = _local_step(x[0], loss_target[0], norms, sgu, comm)
    loss = lax.psum(loss, MESH_AXES)
    landed = comm.finish(after=dx)

    out = {}
    for name, layers in (("ab_w_in", ["ab_in"]), ("ab_w_out", ["ab_out"]), ("sb_w_in", ["sb_in"]),
                         ("sb_w_out", ["sb_out"]), ("ffn_w1", ["w1_0", "w1_1"]), ("ffn_w2", ["w2_0", "w2_1"])):
        out[name] = _adamw(W[name], [landed[n] for n in layers], M[name], V[name], f"adamw_{name}")

    small_g = [small["pre_mix"], small["post_mix"], small["pre_ffn"], small["post_ffn"], small["ln_g"],
               small["ln_b"], small["w_s"][None], small["b_s"][None]]
    g_small = _all_reduce_small(_pack(small_g))
    res = _adamw(_pack([W[n] for n in SMALL])[None], [g_small[None]], _pack([M[n] for n in SMALL])[None],
                 _pack([V[n] for n in SMALL])[None], "adamw_small")
    like = [W[n] for n in SMALL]
    for n, *vals in zip(SMALL, *[_unpack(r[0], like) for r in res]):
        out[n] = vals

    return (loss, dx[None], *[out[n][0] for n in ORDER], *[out[n][1] for n in ORDER],
            *[out[n][2] for n in ORDER], *[out[n][3] for n in ORDER])
```

```python
import functools
import math

import jax
import jax.numpy as jnp
from jax import lax
from jax.experimental import pallas as pl
from jax.experimental.pallas import tpu as pltpu

F32 = jnp.float32
BF16 = jnp.bfloat16

HEAD_DIM = 128
CHUNK = 128
ATT_BLOCK = 128
DILATED_PAIRS = ((128, 1), (512, 4), (2048, 16))
RMS_EPS = 1e-6
LN_EPS = 1e-5
ADAM_LR = 0.001
ADAM_B1 = 0.9
ADAM_B2 = 0.999
ADAM_EPS = 1e-08
ADAM_WD = 0.01
ADAM_STEP = 10
N_DEV = 8
MESH_AXES = ("x", "y", "c")
MASKED = -1e30

V7X_VMEM_BYTES = 64 * 1024 * 1024
VMEM_LIMIT = V7X_VMEM_BYTES - 8 * 1024 * 1024

NN = (((1,), (0,)), ((), ()))
NT = (((1,), (1,)), ((), ()))
TN = (((0,), (0,)), ((), ()))


def _params(*sem):
    return pltpu.CompilerParams(dimension_semantics=sem, vmem_limit_bytes=VMEM_LIMIT)


def _dot(a, b, dims=NN):
    return lax.dot_general(a, b, dims, preferred_element_type=F32)


def _tile(n, preferred):
    if n <= preferred:
        return n
    t = preferred - preferred % 128
    while n % t:
        t -= 128
    assert t > 0, (n, preferred)
    return t


def _matmul(a, b, *, mode, name, out_dtype=F32, tm=1024, tn=1024, tk=2048, epi=None, extras=(), after=()):
    if mode == "nn":
        (M, K), N = a.shape, b.shape[1]
    elif mode == "nt":
        (M, K), N = a.shape, b.shape[0]
    else:
        (K, M), N = a.shape, b.shape[1]
    tm, tn, tk = _tile(M, tm), _tile(N, tn), _tile(K, tk)
    nk = K // tk
    if mode == "tn":
        a_spec = pl.BlockSpec((tk, tm), lambda i, j, k: (k, i))
    else:
        a_spec = pl.BlockSpec((tm, tk), lambda i, j, k: (i, k))
    if mode == "nt":
        b_spec = pl.BlockSpec((tn, tk), lambda i, j, k: (j, k))
    else:
        b_spec = pl.BlockSpec((tk, tn), lambda i, j, k: (k, j))
    o_spec = pl.BlockSpec((tm, tn), lambda i, j, k: (i, j))
    dims = {"nn": NN, "nt": NT, "tn": TN}[mode]
    n_extra = len(extras)
    n_in = n_extra + len(after)

    def finish(acc, refs):
        j = pl.program_id(1)
        if epi is None:
            return acc
        return epi(acc, j, *[r[...] for r in refs])

    if nk == 1:
        def body(a_ref, b_ref, *rest):
            o_ref = rest[n_in]
            acc = _dot(a_ref[...], b_ref[...], dims)
            o_ref[...] = finish(acc, rest[:n_extra]).astype(o_ref.dtype)
        scratch = []
    else:
        def body(a_ref, b_ref, *rest):
            o_ref, acc_ref = rest[n_in], rest[n_in + 1]
            k = pl.program_id(2)

            @pl.when(k == 0)
            def _():
                acc_ref[...] = jnp.zeros_like(acc_ref)

            acc_ref[...] += _dot(a_ref[...], b_ref[...], dims)

            @pl.when(k == nk - 1)
            def _():
                o_ref[...] = finish(acc_ref[...], rest[:n_extra]).astype(o_ref.dtype)
        scratch = [pltpu.VMEM((tm, tn), F32)]

    return pl.pallas_call(
        body,
        name=name,
        grid=(M // tm, N // tn, nk),
        in_specs=[a_spec, b_spec] + [o_spec] * n_extra + [ANY] * len(after),
        out_specs=o_spec,
        out_shape=jax.ShapeDtypeStruct((M, N), out_dtype),
        scratch_shapes=scratch,
        compiler_params=_params("parallel", "parallel", "arbitrary"),
    )(a, b, *extras, *after)


ROWS = 256


def _rms(x):
    return lax.rsqrt(jnp.mean(x * x, axis=-1, keepdims=True) + RMS_EPS)


def _prenorm(x, g, name):
    T, D = x.shape

    def body(x_ref, g_ref, h_ref):
        xv = x_ref[...]
        h_ref[...] = (xv * _rms(xv) * g_ref[...]).astype(BF16)

    row = pl.BlockSpec((ROWS, D), lambda i: (i, 0))
    vec = pl.BlockSpec((1, D), lambda i: (0, 0))
    return pl.pallas_call(
        body, name=name, grid=(T // ROWS,), in_specs=[row, vec], out_specs=row,
        out_shape=jax.ShapeDtypeStruct((T, D), BF16), compiler_params=_params("parallel"),
    )(x, g)


def _postnorm_prenorm(x, y, g_post, g_pre, name):
    T, D = x.shape

    def body(x_ref, y_ref, gp_ref, gn_ref, xo_ref, h_ref):
        yv = y_ref[...]
        xn = x_ref[...] + yv * _rms(yv) * gp_ref[...]
        xo_ref[...] = xn
        h_ref[...] = (xn * _rms(xn) * gn_ref[...]).astype(BF16)

    row = pl.BlockSpec((ROWS, D), lambda i: (i, 0))
    vec = pl.BlockSpec((1, D), lambda i: (0, 0))
    return pl.pallas_call(
        body, name=name, grid=(T // ROWS,), in_specs=[row, row, vec, vec], out_specs=[row, row],
        out_shape=[jax.ShapeDtypeStruct((T, D), F32), jax.ShapeDtypeStruct((T, D), BF16)],
        compiler_params=_params("parallel"),
    )(x, y, g_post, g_pre)


def _postnorm_loss(x, y, g_post, target, name):
    T, D = x.shape

    def body(x_ref, y_ref, gp_ref, t_ref, loss_ref, dx_ref):
        @pl.when(pl.program_id(0) == 0)
        def _():
            loss_ref[...] = jnp.zeros_like(loss_ref)

        yv = y_ref[...]
        err = x_ref[...] + yv * _rms(yv) * gp_ref[...] - t_ref[...]
        dx_ref[...] = err * (1.0 / D)
        loss_ref[...] += 0.5 * jnp.sum(jnp.sum(err * err, axis=-1, keepdims=True) * (1.0 / D))

    row = pl.BlockSpec((ROWS, D), lambda i: (i, 0))
    vec = pl.BlockSpec((1, D), lambda i: (0, 0))
    acc = pl.BlockSpec((8, 128), lambda i: (0, 0))
    return pl.pallas_call(
        body, name=name, grid=(T // ROWS,), in_specs=[row, row, vec, row], out_specs=[acc, row],
        out_shape=[jax.ShapeDtypeStruct((8, 128), F32), jax.ShapeDtypeStruct((T, D), F32)],
        compiler_params=_params("arbitrary"),
    )(x, y, g_post, target)


def _postnorm_bwd(dx, y, g_post, name):
    T, D = dx.shape

    def body(dx_ref, y_ref, g_ref, dy_ref, dg_ref):
        @pl.when(pl.program_id(0) == 0)
        def _():
            dg_ref[...] = jnp.zeros_like(dg_ref)

        yv, dn = y_ref[...], dx_ref[...]
        r = _rms(yv)
        yh = yv * r
        gd = dn * g_ref[...]
        dy_ref[...] = (r * (gd - yh * jnp.mean(yh * gd, axis=-1, keepdims=True))).astype(BF16)
        dg_ref[...] += jnp.sum(dn * yh, axis=0, keepdims=True)

    row = pl.BlockSpec((ROWS, D), lambda i: (i, 0))
    vec = pl.BlockSpec((1, D), lambda i: (0, 0))
    return pl.pallas_call(
        body, name=name, grid=(T // ROWS,), in_specs=[row, row, vec], out_specs=[row, vec],
        out_shape=[jax.ShapeDtypeStruct((T, D), BF16), jax.ShapeDtypeStruct((1, D), F32)],
        compiler_params=_params("arbitrary"),
    )(dx, y, g_post)


def _prenorm_bwd(dx_out, dh, x, g_pre, name):
    T, D = x.shape

    def body(dxo_ref, dh_ref, x_ref, g_ref, dx_ref, dg_ref):
        @pl.when(pl.program_id(0) == 0)
        def _():
            dg_ref[...] = jnp.zeros_like(dg_ref)

        xv, dhv = x_ref[...], dh_ref[...]
        r = _rms(xv)
        xh = xv * r
        gd = dhv * g_ref[...]
        dx_ref[...] = dxo_ref[...] + r * (gd - xh * jnp.mean(xh * gd, axis=-1, keepdims=True))
        dg_ref[...] += jnp.sum(dhv * xh, axis=0, keepdims=True)

    row = pl.BlockSpec((ROWS, D), lambda i: (i, 0))
    vec = pl.BlockSpec((1, D), lambda i: (0, 0))
    return pl.pallas_call(
        body, name=name, grid=(T // ROWS,), in_specs=[row, row, row, vec], out_specs=[row, vec],
        out_shape=[jax.ShapeDtypeStruct((T, D), F32), jax.ShapeDtypeStruct((1, D), F32)],
        compiler_params=_params("arbitrary"),
    )(dx_out, dh, x, g_pre)


_INV_SQRT2 = 1.0 / math.sqrt(2.0)
_INV_SQRT2PI = 1.0 / math.sqrt(2.0 * math.pi)


def _gelu(x):
    return 0.5 * x * (1.0 + lax.erf(x * _INV_SQRT2))


def _gelu_grad(x):
    return 0.5 * (1.0 + lax.erf(x * _INV_SQRT2)) + x * jnp.exp(-0.5 * x * x) * _INV_SQRT2PI


def _layernorm_stats(x):
    mu = jnp.mean(x, axis=-1, keepdims=True)
    xc = x - mu
    rstd = lax.rsqrt(jnp.mean(xc * xc, axis=-1, keepdims=True) + LN_EPS)
    return xc * rstd, rstd


def _tril_mask():
    i = lax.broadcasted_iota(jnp.int32, (CHUNK, CHUNK), 0)
    j = lax.broadcasted_iota(jnp.int32, (CHUNK, CHUNK), 1)
    return j <= i


SGU_ROWS = 512


def _sgu_fwd(z, ln_g, ln_b, w_s, b_t, name):
    T = z.shape[0]
    A = ln_g.shape[1]
    G = A // 128
    rows = min(SGU_ROWS, T)

    def body(u_ref, v_ref, g_ref, b_ref, w_ref, bt_ref, o_ref):
        mask = _tril_mask()
        for c in range(rows // CHUNK):
            rs = pl.ds(c * CHUNK, CHUNK)
            xh, _ = _layernorm_stats(_gelu(v_ref[rs, :]))
            vn = (xh * g_ref[...] + b_ref[...]).astype(BF16)
            for g in range(G):
                cs = pl.ds(g * 128, 128)
                w = jnp.where(mask, w_ref[g], 0.0).astype(BF16)
                mixed = _dot(w, vn[:, g * 128:(g + 1) * 128]) + bt_ref[:, g:g + 1]
                o_ref[rs, cs] = (_gelu(u_ref[rs, cs]) * mixed).astype(BF16)

    return pl.pallas_call(
        body, name=name, grid=(T // rows,),
        in_specs=[
            pl.BlockSpec((rows, A), lambda i: (i, 0)),
            pl.BlockSpec((rows, A), lambda i: (i, 1)),
            pl.BlockSpec((1, A), lambda i: (0, 0)),
            pl.BlockSpec((1, A), lambda i: (0, 0)),
            pl.BlockSpec((G, CHUNK, CHUNK), lambda i: (0, 0, 0)),
            pl.BlockSpec((CHUNK, G), lambda i: (0, 0)),
        ],
        out_specs=pl.BlockSpec((rows, A), lambda i: (i, 0)),
        out_shape=jax.ShapeDtypeStruct((T, A), BF16),
        compiler_params=_params("parallel"),
    )(z, z, ln_g, ln_b, w_s, b_t)


def _sgu_bwd(z, dcat, ln_g, ln_b, w_s, b_t, name):
    T = z.shape[0]
    A = ln_g.shape[1]
    G = A // 128
    rows = min(SGU_ROWS, T)

    def body(u_ref, v_ref, da_ref, g_ref, b_ref, w_ref, bt_ref, dz_ref, dg_ref, db_ref, dw_ref, dbt_ref, dvn_ref):
        @pl.when(pl.program_id(0) == 0)
        def _():
            dg_ref[...] = jnp.zeros_like(dg_ref)
            db_ref[...] = jnp.zeros_like(db_ref)
            dw_ref[...] = jnp.zeros_like(dw_ref)
            dbt_ref[...] = jnp.zeros_like(dbt_ref)

        mask = _tril_mask()
        for c in range(rows // CHUNK):
            rs = pl.ds(c * CHUNK, CHUNK)
            vv = v_ref[rs, :]
            gv = _gelu(vv)
            xh, rstd = _layernorm_stats(gv)
            vn = (xh * g_ref[...] + b_ref[...]).astype(BF16)
            for g in range(G):
                cs = pl.ds(g * 128, 128)
                w = jnp.where(mask, w_ref[g], 0.0).astype(BF16)
                vg = vn[:, g * 128:(g + 1) * 128]
                mixed = _dot(w, vg) + bt_ref[:, g:g + 1]
                uu = u_ref[rs, cs]
                da = da_ref[rs, cs]
                dz_ref[rs, cs] = (da * mixed * _gelu_grad(uu)).astype(BF16)
                dm = da * _gelu(uu)
                dmb = dm.astype(BF16)
                dbt_ref[:, g:g + 1] += jnp.sum(dm, axis=1, keepdims=True)
                dw_ref[g] += jnp.where(mask, _dot(dmb, vg, NT), 0.0)
                dvn_ref[:, cs] = _dot(w, dmb, TN)
            dvn = dvn_ref[...]
            dg_ref[...] += jnp.sum(dvn * xh, axis=0, keepdims=True)
            db_ref[...] += jnp.sum(dvn, axis=0, keepdims=True)
            dxh = dvn * g_ref[...]
            dgv = rstd * (dxh - jnp.mean(dxh, axis=-1, keepdims=True)
                          - xh * jnp.mean(dxh * xh, axis=-1, keepdims=True))
            dz_ref[rs, pl.ds(A, A)] = (dgv * _gelu_grad(vv)).astype(BF16)

    vec = pl.BlockSpec((1, A), lambda i: (0, 0))
    wsp = pl.BlockSpec((G, CHUNK, CHUNK), lambda i: (0, 0, 0))
    bsp = pl.BlockSpec((CHUNK, G), lambda i: (0, 0))
    return pl.pallas_call(
        body, name=name, grid=(T // rows,),
        in_specs=[
            pl.BlockSpec((rows, A), lambda i: (i, 0)),
            pl.BlockSpec((rows, A), lambda i: (i, 1)),
            pl.BlockSpec((rows, A), lambda i: (i, 0)),
            vec, vec, wsp, bsp,
        ],
        out_specs=[pl.BlockSpec((rows, 2 * A), lambda i: (i, 0)), vec, vec, wsp, bsp],
        out_shape=[
            jax.ShapeDtypeStruct((T, 2 * A), BF16),
            jax.ShapeDtypeStruct((1, A), F32),
            jax.ShapeDtypeStruct((1, A), F32),
            jax.ShapeDtypeStruct((G, CHUNK, CHUNK), F32),
            jax.ShapeDtypeStruct((CHUNK, G), F32),
        ],
        scratch_shapes=[pltpu.VMEM((CHUNK, A), F32)],
        compiler_params=_params("arbitrary"),
    )(z, z, dcat, ln_g, ln_b, w_s, b_t)


def _alibi_row(B, d):
    H = B // HEAD_DIM
    slopes = [d * 2.0 ** (-8.0 * (h + 1.0) / H) for h in range(H)]
    return jnp.repeat(jnp.asarray(slopes, F32), HEAD_DIM)[None, :]


def _dil_scores(q, k, slope_d, valid, dist):
    s = _dot(q, k, NT) - slope_d * dist
    return jnp.where(valid, s, MASKED)


def _dil_layout(T, B, d):
    H = B // HEAD_DIM
    hp = H if d == 1 else 1
    return hp, H // hp, T // (d * ATT_BLOCK)


def _dil_rows(ref, r, d, cs):
    return ref[pl.ds(r, ATT_BLOCK, stride=d), cs] if d > 1 else ref[:, cs]


def _dil_put(ref, r, d, cs, val):
    if d > 1:
        ref[pl.ds(r, ATT_BLOCK, stride=d), cs] = val
    else:
        ref[:, cs] = val


def _dilated_fwd(z, d, B, name):
    T = z.shape[0]
    H = B // HEAD_DIM
    hp, groups, nb = _dil_layout(T, B, d)
    cw = hp * HEAD_DIM
    scale = HEAD_DIM ** -0.5
    blk = ATT_BLOCK

    def body(q_ref, kp_ref, kc_ref, vp_ref, vc_ref, sl_ref, o_ref, l_ref):
        n = pl.program_id(1)
        qi = lax.broadcasted_iota(jnp.int32, (blk, 2 * blk), 0)
        kj = lax.broadcasted_iota(jnp.int32, (blk, 2 * blk), 1)
        dist = qi + blk - kj
        valid = (dist >= 0) & (dist <= blk) & ((kj >= blk) | (n > 0))
        distf = dist.astype(F32)
        for r in range(d):
            for hh in range(hp):
                cs = pl.ds(hh * HEAD_DIM, HEAD_DIM)
                q = (_dil_rows(q_ref, r, d, cs) * scale).astype(BF16)
                k = jnp.concatenate([_dil_rows(kp_ref, r, d, cs), _dil_rows(kc_ref, r, d, cs)], axis=0).astype(BF16)
                v = jnp.concatenate([_dil_rows(vp_ref, r, d, cs), _dil_rows(vc_ref, r, d, cs)], axis=0).astype(BF16)
                s = _dil_scores(q, k, sl_ref[:, cs][:, :1], valid, distf)
                m = jnp.max(s, axis=-1, keepdims=True)
                p = jnp.exp(s - m)
                den = jnp.sum(p, axis=-1, keepdims=True)
                _dil_put(o_ref, r, d, cs, _dot(p.astype(BF16), v) / den)
                _dil_put(l_ref, r, d, cs, jnp.broadcast_to(m + jnp.log(den), (blk, HEAD_DIM)))

    def col(unit):
        return lambda g, n: (n, unit * groups + g)

    def col_prev(unit):
        return lambda g, n: (jnp.maximum(n - 1, 0), unit * groups + g)

    bs = (d * blk, cw)
    out = pl.BlockSpec(bs, lambda g, n: (n, g))
    return pl.pallas_call(
        body, name=name, grid=(groups, nb),
        in_specs=[pl.BlockSpec(bs, col(2)), pl.BlockSpec(bs, col_prev(3)), pl.BlockSpec(bs, col(3)),
                  pl.BlockSpec(bs, col_prev(4)), pl.BlockSpec(bs, col(4)), pl.BlockSpec((1, cw), lambda g, n: (0, g))],
        out_specs=[out, out],
        out_shape=[jax.ShapeDtypeStruct((T, B), F32)] * 2,
        compiler_params=_params("parallel", "parallel"),
    )(z, z, z, z, z, _alibi_row(B, d))


def _dilated_merge(os_, ls_, name):
    T, B = os_[0].shape

    def body(o1, o2, o3, l1, l2, l3, ob_ref, of_ref, lt_ref):
        a, b, c = l1[...], l2[...], l3[...]
        m = jnp.maximum(jnp.maximum(a, b), c)
        ea, eb, ec = jnp.exp(a - m), jnp.exp(b - m), jnp.exp(c - m)
        tot = ea + eb + ec
        o = (ea * o1[...] + eb * o2[...] + ec * o3[...]) / tot
        of_ref[...] = o
        ob_ref[...] = o.astype(BF16)
        lt_ref[...] = m + jnp.log(tot)

    row = pl.BlockSpec((ROWS, B), lambda i: (i, 0))
    return pl.pallas_call(
        body, name=name, grid=(T // ROWS,), in_specs=[row] * 6, out_specs=[row] * 3,
        out_shape=[jax.ShapeDtypeStruct((T, B), BF16), jax.ShapeDtypeStruct((T, B), F32),
                   jax.ShapeDtypeStruct((T, B), F32)],
        compiler_params=_params("parallel"),
    )(*os_, *ls_)


def _dilated_delta(dcat, o, name):
    T, B = o.shape
    H = B // HEAD_DIM

    def body(do_ref, o_ref, d_ref):
        for h in range(H):
            cs = pl.ds(h * HEAD_DIM, HEAD_DIM)
            dsum = jnp.sum(do_ref[:, cs] * o_ref[:, cs], axis=-1, keepdims=True)
            d_ref[:, cs] = jnp.broadcast_to(dsum, (ROWS, HEAD_DIM))

    row = pl.BlockSpec((ROWS, B), lambda i: (i, 0))
    return pl.pallas_call(
        body, name=name, grid=(T // ROWS,),
        in_specs=[pl.BlockSpec((ROWS, B), lambda i: (i, 1)), row], out_specs=row,
        out_shape=jax.ShapeDtypeStruct((T, B), F32), compiler_params=_params("parallel"),
    )(dcat, o)


def _dilated_dq(z, dcat, lse, delta, d, B, name):
    T = z.shape[0]
    hp, groups, nb = _dil_layout(T, B, d)
    cw = hp * HEAD_DIM
    scale = HEAD_DIM ** -0.5
    blk = ATT_BLOCK

    def body(q_ref, kp_ref, kc_ref, vp_ref, vc_ref, do_ref, l_ref, dl_ref, sl_ref, dq_ref):
        n = pl.program_id(1)
        qi = lax.broadcasted_iota(jnp.int32, (blk, 2 * blk), 0)
        kj = lax.broadcasted_iota(jnp.int32, (blk, 2 * blk), 1)
        dist = qi + blk - kj
        valid = (dist >= 0) & (dist <= blk) & ((kj >= blk) | (n > 0))
        distf = dist.astype(F32)
        for r in range(d):
            for hh in range(hp):
                cs = pl.ds(hh * HEAD_DIM, HEAD_DIM)
                q = (_dil_rows(q_ref, r, d, cs) * scale).astype(BF16)
                k = jnp.concatenate([_dil_rows(kp_ref, r, d, cs), _dil_rows(kc_ref, r, d, cs)], axis=0).astype(BF16)
                v = jnp.concatenate([_dil_rows(vp_ref, r, d, cs), _dil_rows(vc_ref, r, d, cs)], axis=0).astype(BF16)
                s = _dil_scores(q, k, sl_ref[:, cs][:, :1], valid, distf)
                p = jnp.exp(s - _dil_rows(l_ref, r, d, cs)[:, :1])
                dp = _dot(_dil_rows(do_ref, r, d, cs).astype(BF16), v, NT)
                ds = p * (dp - _dil_rows(dl_ref, r, d, cs)[:, :1])
                _dil_put(dq_ref, r, d, cs, _dot(ds.astype(BF16), k))

    def col(unit):
        return lambda g, n: (n, unit * groups + g)

    def col_prev(unit):
        return lambda g, n: (jnp.maximum(n - 1, 0), unit * groups + g)

    bs = (d * blk, cw)
    out = pl.BlockSpec(bs, lambda g, n: (n, g))
    return pl.pallas_call(
        body, name=name, grid=(groups, nb),
        in_specs=[pl.BlockSpec(bs, col(2)), pl.BlockSpec(bs, col_prev(3)), pl.BlockSpec(bs, col(3)),
                  pl.BlockSpec(bs, col_prev(4)), pl.BlockSpec(bs, col(4)), pl.BlockSpec(bs, col(1)), out, out,
                  pl.BlockSpec((1, cw), lambda g, n: (0, g))],
        out_specs=out,
        out_shape=jax.ShapeDtypeStruct((T, B), F32),
        compiler_params=_params("parallel", "parallel"),
    )(z, z, z, z, z, dcat, lse, delta, _alibi_row(B, d))


def _dilated_dkv(z, dcat, lse, delta, d, B, name):
    T = z.shape[0]
    hp, groups, nb = _dil_layout(T, B, d)
    cw = hp * HEAD_DIM
    scale = HEAD_DIM ** -0.5
    blk = ATT_BLOCK

    def body(k_ref, v_ref, qa_ref, qb_ref, doa_ref, dob_ref, la_ref, lb_ref, da_ref, db_ref, sl_ref, dk_ref, dv_ref):
        m = pl.program_id(1)
        qi = lax.broadcasted_iota(jnp.int32, (2 * blk, blk), 0)
        kj = lax.broadcasted_iota(jnp.int32, (2 * blk, blk), 1)
        dist = qi - kj
        valid = (dist >= 0) & (dist <= blk) & ((qi < blk) | (m + 1 < nb))
        distf = dist.astype(F32)
        for r in range(d):
            for hh in range(hp):
                cs = pl.ds(hh * HEAD_DIM, HEAD_DIM)

                def both(a_ref, b_ref):
                    return jnp.concatenate([_dil_rows(a_ref, r, d, cs), _dil_rows(b_ref, r, d, cs)], axis=0)

                q = (both(qa_ref, qb_ref) * scale).astype(BF16)
                do = both(doa_ref, dob_ref).astype(BF16)
                k = _dil_rows(k_ref, r, d, cs).astype(BF16)
                v = _dil_rows(v_ref, r, d, cs).astype(BF16)
                s = _dil_scores(q, k, sl_ref[:, cs][:, :1], valid, distf)
                p = jnp.exp(jnp.where(valid, s - both(la_ref, lb_ref)[:, :1], MASKED))
                _dil_put(dv_ref, r, d, cs, _dot(p.astype(BF16), do, TN))
                ds = p * (_dot(do, v, NT) - both(da_ref, db_ref)[:, :1])
                _dil_put(dk_ref, r, d, cs, _dot(ds.astype(BF16), q, TN))

    def col(unit):
        return lambda g, m: (m, unit * groups + g)

    def nxt(unit):
        return lambda g, m: (jnp.minimum(m + 1, nb - 1), unit * groups + g)

    bs = (d * blk, cw)
    out = pl.BlockSpec(bs, lambda g, m: (m, g))
    return pl.pallas_call(
        body, name=name, grid=(groups, nb),
        in_specs=[pl.BlockSpec(bs, col(3)), pl.BlockSpec(bs, col(4)),
                  pl.BlockSpec(bs, col(2)), pl.BlockSpec(bs, nxt(2)),
                  pl.BlockSpec(bs, col(1)), pl.BlockSpec(bs, nxt(1)),
                  out, pl.BlockSpec(bs, nxt(0)), out, pl.BlockSpec(bs, nxt(0)),
                  pl.BlockSpec((1, cw), lambda g, m: (0, g))],
        out_specs=[out, out],
        out_shape=[jax.ShapeDtypeStruct((T, B), F32)] * 2,
        compiler_params=_params("parallel", "parallel"),
    )(z, z, z, z, dcat, dcat, lse, lse, delta, delta, _alibi_row(B, d))


def _dilated_combine(dqs, dks, dvs, name):
    T, B = dqs[0].shape
    scale = HEAD_DIM ** -0.5

    def body(q1, q2, q3, k1, k2, k3, v1, v2, v3, o_ref):
        o_ref[:, pl.ds(0, B)] = ((q1[...] + q2[...] + q3[...]) * scale).astype(BF16)
        o_ref[:, pl.ds(B, B)] = (k1[...] + k2[...] + k3[...]).astype(BF16)
        o_ref[:, pl.ds(2 * B, B)] = (v1[...] + v2[...] + v3[...]).astype(BF16)

    row = pl.BlockSpec((ROWS, B), lambda i: (i, 0))
    return pl.pallas_call(
        body, name=name, grid=(T // ROWS,), in_specs=[row] * 9,
        out_specs=pl.BlockSpec((ROWS, 3 * B), lambda i: (i, 0)),
        out_shape=jax.ShapeDtypeStruct((T, 3 * B), BF16), compiler_params=_params("parallel"),
    )(*dqs, *dks, *dvs)


SB_QUERY_ROWS = 512
SB_KEYS = 2 * ATT_BLOCK


def _tri_and_ones(pred):
    rows = lax.broadcasted_iota(jnp.int32, (2 * ATT_BLOCK, 2 * ATT_BLOCK), 0) % ATT_BLOCK
    cols = lax.broadcasted_iota(jnp.int32, (2 * ATT_BLOCK, 2 * ATT_BLOCK), 1)
    return ((cols >= ATT_BLOCK) | pred(rows, cols)).astype(BF16)


def _running(x, tri):
    hi = x.astype(BF16)
    lo = (x - hi.astype(F32)).astype(BF16)
    return _dot(jnp.concatenate([hi, lo], axis=1), tri)


def _sb_mask(query_rows, s):
    rows = lax.broadcasted_iota(jnp.int32, (query_rows, SB_KEYS), 0)
    cols = lax.broadcasted_iota(jnp.int32, (query_rows, SB_KEYS), 1)
    return cols + s * SB_KEYS < rows


def _log_sigmoids(z):
    e = jnp.exp(-jnp.abs(z))
    ls = jnp.minimum(z, 0.0) - jnp.log(1.0 + e)
    return ls, ls - z, e


def _sb_fwd(qkv, W, name):
    T = qkv.shape[0]
    H = W // HEAD_DIM
    blk = ATT_BLOCK
    qb = min(SB_QUERY_ROWS, T)
    per = qb // SB_KEYS

    def body(q_ref, k_ref, v_ref, o_ref, lt_ref, acc_ref):
        i = pl.program_id(1)
        q = q_ref[...]
        tri = _tri_and_ones(lambda r, c: r > c)
        lt_ref[...] = jnp.zeros_like(lt_ref)
        acc_ref[...] = jnp.zeros_like(acc_ref)

        def tile(j, mask):
            ks = pl.ds(pl.multiple_of(j * SB_KEYS, SB_KEYS), SB_KEYS)
            z = _dot(q, k_ref[ks, :], NT)
            ls, lm, _ = _log_sigmoids(z)
            if mask is not None:
                lm = jnp.where(mask, lm, 0.0)
            later = lt_ref[...]
            second = _running(lm[:, blk:], tri)
            first = _running(lm[:, :blk], tri)
            after_first = later + second[:, blk:]
            a = jnp.exp(ls + jnp.concatenate([first[:, :blk] + after_first, second[:, :blk] + later], axis=1))
            if mask is not None:
                a = jnp.where(mask, a, 0.0)
            acc_ref[...] += _dot(a.astype(BF16), v_ref[ks, :])
            lt_ref[...] = after_first + first[:, blk:]

        for s in reversed(range(per)):
            tile(i * per + s, _sb_mask(qb, s))

        def step(jj, _):
            for s in range(per):
                tile((i - jj) * per - 1 - s, None)
            return 0

        lax.fori_loop(0, i, step, 0)
        o_ref[...] = acc_ref[...].astype(BF16)

    qs = pl.BlockSpec((qb, HEAD_DIM), lambda h, i: (i, h))
    return pl.pallas_call(
        body, name=name, grid=(H, T // qb),
        in_specs=[qs, pl.BlockSpec((T, HEAD_DIM), lambda h, i: (0, H + h)),
                  pl.BlockSpec((T, HEAD_DIM), lambda h, i: (0, 2 * H + h))],
        out_specs=[qs, qs],
        out_shape=[jax.ShapeDtypeStruct((T, W), BF16), jax.ShapeDtypeStruct((T, W), F32)],
        scratch_shapes=[pltpu.VMEM((qb, HEAD_DIM), F32)],
        compiler_params=_params("parallel", "arbitrary"),
    )(qkv, qkv, qkv)


def _sb_bwd(qkv, do, ltot, W, name):
    T = qkv.shape[0]
    H = W // HEAD_DIM
    blk = ATT_BLOCK
    nkb = T // SB_KEYS
    qb = min(SB_QUERY_ROWS, T)
    per = qb // SB_KEYS

    def body(q_ref, k_ref, v_ref, do_ref, lt_ref, dq_ref, dkt_ref, dvt_ref, qt_ref, dot_ref, plm_ref, pg_ref):
        i = pl.program_id(1)

        @pl.when(i == 0)
        def _():
            dkt_ref[...] = jnp.zeros_like(dkt_ref)
            dvt_ref[...] = jnp.zeros_like(dvt_ref)

        q = q_ref[...]
        do = do_ref[...]
        qt_ref[...] = q.astype(F32).T.astype(BF16)
        dot_ref[...] = do.astype(F32).T.astype(BF16)
        upto = _tri_and_ones(lambda r, c: r <= c)
        before = _tri_and_ones(lambda r, c: r < c)
        plm_ref[...] = jnp.zeros_like(plm_ref)
        pg_ref[...] = jnp.zeros_like(pg_ref)
        dq_ref[...] = jnp.zeros_like(dq_ref)

        def tile(j, mask):
            ks = pl.ds(pl.multiple_of(j * SB_KEYS, SB_KEYS), SB_KEYS)
            k = k_ref[ks, :]
            v = v_ref[ks, :]
            z = _dot(q, k, NT)
            ls, lm, e = _log_sigmoids(z)
            if mask is not None:
                lm = jnp.where(mask, lm, 0.0)
            earlier = plm_ref[...]
            first = _running(lm[:, :blk], upto)
            second = _running(lm[:, blk:], upto)
            upto_first = earlier + first[:, blk:]
            seen = jnp.concatenate([first[:, :blk] + earlier, second[:, :blk] + upto_first], axis=1)
            ltot = lt_ref[...]
            a = jnp.exp(ls + (jnp.concatenate([ltot, ltot], axis=1) - seen))
            if mask is not None:
                a = jnp.where(mask, a, 0.0)
            g = a * _dot(do, v, NT)
            g_earlier = pg_ref[...]
            g_first = _running(g[:, :blk], before)
            g_second = _running(g[:, blk:], before)
            g_upto_first = g_earlier + g_first[:, blk:]
            gsum = jnp.concatenate([g_first[:, :blk] + g_earlier, g_second[:, :blk] + g_upto_first], axis=1)
            r = 1.0 / (1.0 + e)
            pos = z >= 0.0
            sig = jnp.where(pos, r, e * r)
            nsig = jnp.where(pos, e * r, r)
            dz = g * nsig - gsum * sig
            if mask is not None:
                dz = jnp.where(mask, dz, 0.0)
            dzb = dz.astype(BF16)
            dkt_ref[j] += _dot(qt_ref[...], dzb)
            dvt_ref[j] += _dot(dot_ref[...], a.astype(BF16))
            dq_ref[...] += _dot(dzb, k)
            plm_ref[...] = upto_first + second[:, blk:]
            pg_ref[...] = g_upto_first + g_second[:, blk:]

        def step(jj, _):
            for s in range(per):
                tile(jj * per + s, None)
            return 0

        lax.fori_loop(0, i, step, 0)
        for s in range(per):
            tile(i * per + s, _sb_mask(qb, s))

    qs = pl.BlockSpec((qb, HEAD_DIM), lambda h, i: (i, h))
    res = pl.BlockSpec((None, nkb, HEAD_DIM, SB_KEYS), lambda h, i: (h, 0, 0, 0))
    return pl.pallas_call(
        body, name=name, grid=(H, T // qb),
        in_specs=[qs, pl.BlockSpec((T, HEAD_DIM), lambda h, i: (0, H + h)),
                  pl.BlockSpec((T, HEAD_DIM), lambda h, i: (0, 2 * H + h)), qs, qs],
        out_specs=[qs, res, res],
        out_shape=[jax.ShapeDtypeStruct((T, W), F32)] + [jax.ShapeDtypeStruct((H, nkb, HEAD_DIM, SB_KEYS), F32)] * 2,
        scratch_shapes=[pltpu.VMEM((HEAD_DIM, qb), BF16), pltpu.VMEM((HEAD_DIM, qb), BF16),
                        pltpu.VMEM((qb, HEAD_DIM), F32), pltpu.VMEM((qb, HEAD_DIM), F32)],
        compiler_params=_params("parallel", "arbitrary"),
    )(qkv, qkv, qkv, do, ltot)


def _sb_pack(dq, dkt, dvt, name):
    T, W = dq.shape
    H = W // HEAD_DIM
    blk = SB_KEYS
    scale = HEAD_DIM ** -0.5

    def body(q_ref, kt_ref, vt_ref, o_ref):
        o_ref[:, pl.ds(0, W)] = (q_ref[...] * scale).astype(BF16)
        for h in range(H):
            o_ref[:, pl.ds(W + h * HEAD_DIM, HEAD_DIM)] = kt_ref[h].T.astype(BF16)
            o_ref[:, pl.ds(2 * W + h * HEAD_DIM, HEAD_DIM)] = vt_ref[h].T.astype(BF16)

    tr = pl.BlockSpec((H, None, HEAD_DIM, blk), lambda i: (0, i, 0, 0))
    return pl.pallas_call(
        body, name=name, grid=(T // blk,), in_specs=[pl.BlockSpec((blk, W), lambda i: (i, 0)), tr, tr],
        out_specs=pl.BlockSpec((blk, 3 * W), lambda i: (i, 0)),
        out_shape=jax.ShapeDtypeStruct((T, 3 * W), BF16), compiler_params=_params("parallel"),
    )(dq, dkt, dvt)


def _local_step(x, target, norms, sgu, comm):
    T, D = x.shape
    A = D // 2
    pre_mix, post_mix, pre_ffn, post_ffn = norms
    ln_g, ln_b, w_s, b_s = sgu
    b_t = b_s.T
    scale = HEAD_DIM ** -0.5

    def vec(p, layer):
        return comm.tie(p[layer:layer + 1])

    h0 = _prenorm(x, vec(pre_mix, 0), "prenorm0")
    z = _matmul(h0, comm.weight("ab_in"), mode="nn", name="ab_in_fwd")
    a_out = _sgu_fwd(z, ln_g, ln_b, w_s, b_t, "sgu_fwd")
    branch = [_dilated_fwd(z, d, A, f"dilated_fwd_{d}") for _, d in DILATED_PAIRS]
    b_out, o_dil, lse_dil = _dilated_merge([b[0] for b in branch], [b[1] for b in branch], "dilated_merge")
    cat = jnp.concatenate([a_out, b_out], axis=-1)
    y0 = _matmul(cat, comm.weight("ab_out"), mode="nn", name="ab_out_fwd")
    comm.arrive("ffn0", after=y0)
    x1, h1 = _postnorm_prenorm(x, y0, vec(post_mix, 0), vec(pre_ffn, 0), "norm_mix0")
    comm.land("ffn0", after=h1)

    def relu2(acc, j):
        r = jnp.maximum(acc, 0.0)
        return r * r

    f0 = _matmul(h1, comm.weight("w1_0"), mode="nn", name="ffn0_w1_fwd", out_dtype=BF16, epi=relu2)
    y1 = _matmul(f0, comm.weight("w2_0"), mode="nn", name="ffn0_w2_fwd")
    comm.arrive("rest", after=y1)
    x2, h2 = _postnorm_prenorm(x1, y1, vec(post_ffn, 0), vec(pre_mix, 1), "norm_ffn0")
    comm.land("rest", after=h2)

    tn_qkv = _tile(D, 1024)
    nq = D // tn_qkv

    def scale_q(acc, j):
        return jnp.where(j < nq, acc * scale, acc)

    qkv = _matmul(h2, comm.weight("sb_in"), mode="nn", name="sb_in_fwd", out_dtype=BF16, tn=tn_qkv, epi=scale_q)
    o_sb, ltot = _sb_fwd(qkv, D, "sb_fwd")
    y2 = _matmul(o_sb, comm.weight("sb_out"), mode="nn", name="sb_out_fwd")
    x3, h3 = _postnorm_prenorm(x2, y2, vec(post_mix, 1), vec(pre_ffn, 1), "norm_mix1")
    f1 = _matmul(h3, comm.weight("w1_1"), mode="nn", name="ffn1_w1_fwd", out_dtype=BF16, epi=relu2)
    y3 = _matmul(f1, comm.weight("w2_1"), mode="nn", name="ffn1_w2_fwd")
    loss_tile, dx4 = _postnorm_loss(x3, y3, vec(post_ffn, 1), target, "norm_loss")
    loss = loss_tile[0, 0]

    def relu2_bwd(acc, j, f):
        return acc * (2.0 * jnp.sqrt(f.astype(F32)))

    def ffn_bwd(dx_out, x_in, h, f, y, layer, tag):
        dy, dg_post = _postnorm_bwd(dx_out, y, vec(post_ffn, layer), f"ffn{tag}_postnorm_bwd")
        g_w2 = _matmul(f, dy, mode="tn", name=f"ffn{tag}_w2_wgrad", out_dtype=BF16)
        da = _matmul(dy, comm.weight(f"w2_{layer}"), mode="nt", name=f"ffn{tag}_w2_dgrad", out_dtype=BF16,
                     epi=relu2_bwd, extras=(f,))
        g_w1 = _matmul(h, da, mode="tn", name=f"ffn{tag}_w1_wgrad", out_dtype=BF16)
        comm.reduce(f"ffn{tag}", {f"w2_{layer}": g_w2, f"w1_{layer}": g_w1})
        dh = _matmul(da, comm.weight(f"w1_{layer}"), mode="nt", name=f"ffn{tag}_w1_dgrad", after=comm.started())
        dx_in, dg_pre = _prenorm_bwd(dx_out, dh, x_in, vec(pre_ffn, layer), f"ffn{tag}_prenorm_bwd")
        return dx_in, dg_pre, dg_post

    dx3, dg_pre_ffn1, dg_post_ffn1 = ffn_bwd(dx4, x3, h3, f1, y3, 1, "1")

    dy2, dg_post_mix1 = _postnorm_bwd(dx3, y2, vec(post_mix, 1), "sb_postnorm_bwd")
    g_sb_out = _matmul(o_sb, dy2, mode="tn", name="sb_out_wgrad", out_dtype=BF16)
    do_sb = _matmul(dy2, comm.weight("sb_out"), mode="nt", name="sb_out_dgrad", out_dtype=BF16)
    dq, dk, dv = _sb_bwd(qkv, do_sb, ltot, D, "sb_bwd")
    dqkv = _sb_pack(dq, dk, dv, "sb_pack")
    g_sb_in = _matmul(h2, dqkv, mode="tn", name="sb_in_wgrad", out_dtype=BF16)
    comm.reduce("sb", {"sb_out": g_sb_out, "sb_in": g_sb_in})
    dh2 = _matmul(dqkv, comm.weight("sb_in"), mode="nt", name="sb_in_dgrad", after=comm.started())
    dx2, dg_pre_mix1 = _prenorm_bwd(dx3, dh2, x2, vec(pre_mix, 1), "sb_prenorm_bwd")

    dx1, dg_pre_ffn0, dg_post_ffn0 = ffn_bwd(dx2, x1, h1, f0, y1, 0, "0")

    dy0, dg_post_mix0 = _postnorm_bwd(dx1, y0, vec(post_mix, 0), "ab_postnorm_bwd")
    g_ab_out = _matmul(cat, dy0, mode="tn", name="ab_out_wgrad", out_dtype=BF16)
    dcat = _matmul(dy0, comm.weight("ab_out"), mode="nt", name="ab_out_dgrad")
    duv, d_ln_g, d_ln_b, d_w_s, d_b_t = _sgu_bwd(z, dcat, ln_g, ln_b, w_s, b_t, "sgu_bwd")
    delta = _dilated_delta(dcat, o_dil, "dilated_delta")
    dqs, dks, dvs = [], [], []
    for _, d in DILATED_PAIRS:
        dqs.append(_dilated_dq(z, dcat, lse_dil, delta, d, A, f"dilated_dq_{d}"))
        dk_b, dv_b = _dilated_dkv(z, dcat, lse_dil, delta, d, A, f"dilated_dkv_{d}")
        dks.append(dk_b)
        dvs.append(dv_b)
    dqkv0 = _dilated_combine(dqs, dks, dvs, "dilated_combine")
    dz = jnp.concatenate([duv, dqkv0], axis=-1)
    g_ab_in = _matmul(h0, dz, mode="tn", name="ab_in_wgrad", out_dtype=BF16)
    comm.reduce("ab", {"ab_out": g_ab_out, "ab_in": g_ab_in})
    dh0 = _matmul(dz, comm.weight("ab_in"), mode="nt", name="ab_in_dgrad", after=comm.started())
    dx0, dg_pre_mix0 = _prenorm_bwd(dx1, dh0, x, vec(pre_mix, 0), "ab_prenorm_bwd")

    small = {
        "pre_mix": jnp.concatenate([dg_pre_mix0, dg_pre_mix1], axis=0),
        "post_mix": jnp.concatenate([dg_post_mix0, dg_post_mix1], axis=0),
        "pre_ffn": jnp.concatenate([dg_pre_ffn0, dg_pre_ffn1], axis=0),
        "post_ffn": jnp.concatenate([dg_post_ffn0, dg_post_ffn1], axis=0),
        "ln_g": d_ln_g, "ln_b": d_ln_b, "w_s": d_w_s, "b_s": d_b_t.T,
    }
    return loss, dx0, small


MESH_ID = pl.DeviceIdType.MESH
ANY = pl.BlockSpec(memory_space=pl.ANY)


def _coords():
    return lax.axis_index("x"), lax.axis_index("y"), lax.axis_index("c")


def _shard_of(ref, kind, p):
    if kind == "col":
        n = ref.shape[1] // N_DEV
        return ref.at[:, pl.ds(pl.multiple_of(p * n, 128), n)]
    r = ref.shape[0] // N_DEV
    return ref.at[pl.ds(pl.multiple_of(p * r, 16), r), :]


def _full_shape(shard, kind):
    if kind == "col":
        return (shard.shape[0], shard.shape[1] * N_DEV)
    return (shard.shape[0] * N_DEV, shard.shape[1])


def _place(shard, kind, block, name):
    rows, cols = shard.shape
    tr = _tile(rows, 512)

    def body(b_ref, s_ref, o_ref):
        o_ref[...] = s_ref[...].astype(BF16)

    if kind == "col":
        out = pl.BlockSpec((tr, cols), lambda i, b_ref: (i, b_ref[0]))
    else:
        out = pl.BlockSpec((tr, cols), lambda i, b_ref: (b_ref[0] * (rows // tr) + i, 0))
    return pl.pallas_call(
        body, name=name,
        grid_spec=pltpu.PrefetchScalarGridSpec(
            num_scalar_prefetch=1, grid=(rows // tr,),
            in_specs=[pl.BlockSpec((tr, cols), lambda i, b_ref: (i, 0))], out_specs=out),
        out_shape=jax.ShapeDtypeStruct(_full_shape(shard, kind), BF16),
        compiler_params=_params("parallel"),
    )(block, shard)


def _all_gather(fulls, kinds):
    nt = len(fulls)

    def body(*refs):
        ins = refs[:nt]
        send_sems, recv_sems = refs[2 * nt:]
        x, y, c = _coords()
        me, sibling = (x, y, c), (x, y, 1 - c)
        chips = [(1 - x, y), (x, 1 - y), (1 - x, 1 - y)]

        def copy(t, k, block, to):
            px, py, pc = block
            slot = _shard_of(ins[t], kinds[t], 4 * px + 2 * py + pc)
            return pltpu.make_async_remote_copy(
                src_ref=slot, dst_ref=slot, send_sem=send_sems.at[7 * t + k], recv_sem=recv_sems.at[7 * t + k],
                device_id=to, device_id_type=MESH_ID)

        sent = []
        for t in range(nt):
            first = [copy(t, 0, me, sibling)] + [copy(t, 1 + j, me, (*chip, c)) for j, chip in enumerate(chips)]
            for cp in first:
                cp.start()
            sent += first
        for t in range(nt):
            for j, chip in enumerate(chips):
                copy(t, 1 + j, (*chip, c), me).wait_recv()
                sent.append(copy(t, 4 + j, (*chip, c), sibling))
                sent[-1].start()
        for t in range(nt):
            copy(t, 0, sibling, me).wait_recv()
            for j, chip in enumerate(chips):
                copy(t, 4 + j, (*chip, 1 - c), me).wait_recv()
        for cp in sent:
            cp.wait_send()

    return pl.pallas_call(
        body, name="all_gather_weights",
        in_specs=[ANY] * nt, out_specs=[ANY] * nt,
        out_shape=[jax.ShapeDtypeStruct(f.shape, f.dtype) for f in fulls],
        input_output_aliases={t: t for t in range(nt)},
        scratch_shapes=[pltpu.SemaphoreType.DMA((7 * nt,)), pltpu.SemaphoreType.DMA((7 * nt,))],
    )(*fulls)


HBM = pl.BlockSpec(memory_space=pltpu.HBM)
SEM = pl.BlockSpec(memory_space=pltpu.SEMAPHORE)
FLOWS = pltpu.SideEffectType.DATAFLOW_SIDE_EFFECTING


def _in_hbm(a):
    return pltpu.with_memory_space_constraint(a, pltpu.HBM)


def _hbm_like(bufs):
    return [pltpu.HBM(b.shape, b.dtype) for b in bufs]


def _copies_start(name, bufs, plan, n, after):
    nb = len(bufs)

    def body(*refs):
        send_sems, recv_sems, token = refs[nb + 1], refs[nb + 2], refs[-1]
        for cp in plan(refs[:nb], send_sems, recv_sems):
            cp.start()
        token[...] = jnp.zeros_like(token)

    out = pl.pallas_call(
        body, name=name, in_specs=[HBM] * nb + [ANY],
        out_specs=[SEM, SEM] + [HBM] * nb + [pl.BlockSpec(memory_space=pltpu.VMEM)],
        out_shape=[pltpu.SemaphoreType.DMA((n,)), pltpu.SemaphoreType.DMA((n,))] + _hbm_like(bufs)
        + [jax.ShapeDtypeStruct((8, 128), F32)],
        input_output_aliases={i: 2 + i for i in range(nb)},
        compiler_params=pltpu.CompilerParams(has_side_effects=FLOWS),
    )(*[_in_hbm(b) for b in bufs], after)
    return (out[0], out[1]), list(out[2:2 + nb]), out[-1]


def _copies_wait(name, bufs, sems, after, plan):
    nb = len(bufs)

    def body(*refs):
        for cp in plan(refs[:nb], refs[nb], refs[nb + 1]):
            cp.wait_send()
            cp.wait_recv()

    out = pl.pallas_call(
        body, name=name, in_specs=[HBM] * nb + [SEM, SEM, ANY], out_specs=[HBM] * nb,
        out_shape=_hbm_like(bufs), input_output_aliases={i: i for i in range(nb)},
        compiler_params=pltpu.CompilerParams(has_side_effects=FLOWS),
    )(*bufs, *sems, after)
    return list(out)


def _copies_wait_start(name, bufs, sems, after, plan, next_plan, n_next):
    nb = len(bufs)

    def body(*refs):
        ins = refs[:nb]
        for cp in plan(ins, refs[nb], refs[nb + 1]):
            cp.wait_send()
            cp.wait_recv()
        send_sems, recv_sems, token = refs[nb + 3], refs[nb + 4], refs[-1]
        for cp in next_plan(ins, send_sems, recv_sems):
            cp.start()
        token[...] = jnp.zeros_like(token)

    out = pl.pallas_call(
        body, name=name, in_specs=[HBM] * nb + [SEM, SEM, ANY],
        out_specs=[SEM, SEM] + [HBM] * nb + [pl.BlockSpec(memory_space=pltpu.VMEM)],
        out_shape=[pltpu.SemaphoreType.DMA((n_next,)), pltpu.SemaphoreType.DMA((n_next,))] + _hbm_like(bufs)
        + [jax.ShapeDtypeStruct((8, 128), F32)],
        input_output_aliases={i: 2 + i for i in range(nb)},
        compiler_params=pltpu.CompilerParams(has_side_effects=FLOWS),
    )(*bufs, *sems, after)
    return (out[0], out[1]), list(out[2:2 + nb]), out[-1]


def _gather_plans(kinds):
    nt = len(kinds)

    def slot(refs, t, px, py, pc):
        return _shard_of(refs[t], kinds[t], 4 * px + 2 * py + pc)

    def to_chips(refs, send_sems, recv_sems):
        x, y, c = _coords()
        peers = [(x, y, 1 - c), (1 - x, y, c), (x, 1 - y, c), (1 - x, 1 - y, c)]
        return [pltpu.make_async_remote_copy(
            src_ref=slot(refs, t, x, y, c), dst_ref=slot(refs, t, x, y, c), send_sem=send_sems.at[4 * t + k],
            recv_sem=recv_sems.at[4 * t + k], device_id=peer, device_id_type=MESH_ID)
            for t in range(nt) for k, peer in enumerate(peers)]

    def to_sibling(refs, send_sems, recv_sems):
        x, y, c = _coords()
        chips = [(1 - x, y), (x, 1 - y), (1 - x, 1 - y)]
        return [pltpu.make_async_remote_copy(
            src_ref=slot(refs, t, *chip, c), dst_ref=slot(refs, t, *chip, c), send_sem=send_sems.at[3 * t + j],
            recv_sem=recv_sems.at[3 * t + j], device_id=(x, y, 1 - c), device_id_type=MESH_ID)
            for t in range(nt) for j, chip in enumerate(chips)]

    return to_chips, to_sibling


def _shard_shape(full, kind):
    if kind == "col":
        return (full.shape[0], full.shape[1] // N_DEV)
    return (full.shape[0] // N_DEV, full.shape[1])


def _quarter_of(ref, kind, q):
    if kind == "col":
        n = ref.shape[1] // 4
        return ref.at[:, pl.ds(pl.multiple_of(q * n, 128), n)]
    r = ref.shape[0] // 4
    return ref.at[pl.ds(pl.multiple_of(q * r, 16), r), :]


def _quarter_shape(full, kind):
    if kind == "col":
        return (full.shape[0], full.shape[1] // 4)
    return (full.shape[0] // 4, full.shape[1])


def _scatter_plan(kinds):
    nt = len(kinds)

    def plan(refs, send_sems, recv_sems):
        x, y, c = _coords()
        chips = [(1 - x, y), (x, 1 - y), (1 - x, 1 - y)]
        return [pltpu.make_async_remote_copy(
            src_ref=_quarter_of(refs[t], kinds[t], 2 * qx + qy), dst_ref=refs[nt + t].at[2 * x + y],
            send_sem=send_sems.at[3 * t + j], recv_sem=recv_sems.at[3 * t + j],
            device_id=(qx, qy, c), device_id_type=MESH_ID)
            for t in range(nt) for j, (qx, qy) in enumerate(chips)]
    return plan


def _partial_specs(full, kind, tr, half):
    rows, cols = _shard_shape(full, kind)
    steps = rows // tr
    if kind == "col":
        own = pl.BlockSpec((tr, cols), lambda i, w: (i, 2 * w[2] + w[half]))
        landed = [pl.BlockSpec((None, tr, cols), lambda i, w, k=k: (w[2 + k], i, w[half])) for k in (1, 2, 3)]
    else:
        own = pl.BlockSpec((tr, cols), lambda i, w: ((2 * w[2] + w[half]) * steps + i, 0))
        landed = [pl.BlockSpec((None, tr, cols), lambda i, w, k=k: (w[2 + k], w[half] * steps + i, 0))
                  for k in (1, 2, 3)]
    return [own] + landed


def _for_sibling(grad, land, kind, where, name):
    rows, cols = _shard_shape(grad, kind)
    tr = _tile(rows, 256)

    def body(w_ref, g_ref, a_ref, b_ref, c_ref, o_ref):
        o_ref[...] = ((g_ref[...].astype(F32) + a_ref[...].astype(F32)) + b_ref[...].astype(F32)) \
            + c_ref[...].astype(F32)

    return pl.pallas_call(
        body, name=name,
        grid_spec=pltpu.PrefetchScalarGridSpec(
            num_scalar_prefetch=1, grid=(rows // tr,), in_specs=_partial_specs(grad, kind, tr, 1),
            out_specs=pl.BlockSpec((tr, cols), lambda i, w: (i, 0))),
        out_shape=jax.ShapeDtypeStruct((rows, cols), F32),
        compiler_params=_params("parallel"),
    )(where, grad, land, land, land)


def _exchange_siblings(sums):
    nt = len(sums)

    def body(*refs):
        ins, outs = refs[:nt], refs[nt:2 * nt]
        send_sems, recv_sems = refs[2 * nt:]
        x, y, c = _coords()
        copies = [pltpu.make_async_remote_copy(
            src_ref=ins[t], dst_ref=outs[t], send_sem=send_sems.at[t], recv_sem=recv_sems.at[t],
            device_id=(x, y, 1 - c), device_id_type=MESH_ID) for t in range(nt)]
        for cp in copies:
            cp.start()
        for cp in copies:
            cp.wait()

    return pl.pallas_call(
        body, name="grad_exchange_siblings", in_specs=[ANY] * nt, out_specs=[ANY] * nt,
        out_shape=[jax.ShapeDtypeStruct(s.shape, s.dtype) for s in sums],
        scratch_shapes=[pltpu.SemaphoreType.DMA((nt,)), pltpu.SemaphoreType.DMA((nt,))],
    )(*sums)


def _all_reduce_small(vec):
    R = vec.shape[0]

    def body(v_ref, o_ref, recv_ref, send_sems, recv_sems):
        x, y, c = _coords()
        me = 4 * x + 2 * y + c
        recv_ref[me] = v_ref[...]
        copies = []
        for k in range(1, N_DEV):
            bx, by, bc = (k >> 2) & 1, (k >> 1) & 1, k & 1
            peer = (1 - x if bx else x, 1 - y if by else y, 1 - c if bc else c)
            copies.append(pltpu.make_async_remote_copy(
                src_ref=v_ref, dst_ref=recv_ref.at[me],
                send_sem=send_sems.at[k - 1], recv_sem=recv_sems.at[k - 1],
                device_id=peer, device_id_type=MESH_ID))
        for cp in copies:
            cp.start()
        for cp in copies:
            cp.wait()
        total = recv_ref[0]
        for p in range(1, N_DEV):
            total = total + recv_ref[p]
        o_ref[...] = total

    return pl.pallas_call(
        body, name="all_reduce_small",
        in_specs=[pl.BlockSpec(memory_space=pltpu.VMEM)], out_specs=pl.BlockSpec(memory_space=pltpu.VMEM),
        out_shape=jax.ShapeDtypeStruct((R, 128), F32),
        scratch_shapes=[pltpu.VMEM((N_DEV, R, 128), F32), pltpu.SemaphoreType.DMA((N_DEV - 1,)),
                        pltpu.SemaphoreType.DMA((N_DEV - 1,))],
        compiler_params=pltpu.CompilerParams(vmem_limit_bytes=VMEM_LIMIT),
    )(vec)


def _adamw_math(w, g, m, v):
    m = ADAM_B1 * m + (1.0 - ADAM_B1) * g
    v = ADAM_B2 * v + (1.0 - ADAM_B2) * (g * g)
    m_hat = m / (1.0 - ADAM_B1 ** ADAM_STEP)
    v_hat = v / (1.0 - ADAM_B2 ** ADAM_STEP)
    delta = -ADAM_LR * (m_hat / (jnp.sqrt(v_hat) + ADAM_EPS) + ADAM_WD * w)
    return delta, m, v


def _adamw(w, grads, kind, where, m, v, name):
    layers, rows, cols = w.shape
    tr = _tile(rows, 128)
    out = None
    for layer, (grad, land, sibling) in enumerate(grads):
        def body(w_ref, p0_ref, p1_ref, p2_ref, p3_ref, s_ref, x_ref, m_ref, v_ref, *rest):
            g_ref, d_ref, mo_ref, vo_ref = rest[-4:]
            g = ((p0_ref[...].astype(F32) + p1_ref[...].astype(F32)) + p2_ref[...].astype(F32)) \
                + p3_ref[...].astype(F32) + s_ref[...]
            g_ref[...] = g
            d_ref[...], mo_ref[...], vo_ref[...] = _adamw_math(x_ref[...], g, m_ref[...], v_ref[...])

        blk = pl.BlockSpec((None, tr, cols), lambda i, w_, layer=layer: (layer, i, 0))
        earlier = [] if out is None else list(out)
        out = pl.pallas_call(
            body, name=f"{name}_{layer}",
            grid_spec=pltpu.PrefetchScalarGridSpec(
                num_scalar_prefetch=1, grid=(rows // tr,),
                in_specs=_partial_specs(grad, kind, tr, 0) + [pl.BlockSpec((tr, cols), lambda i, w_: (i, 0))]
                + [blk] * 3 + [ANY] * len(earlier),
                out_specs=[blk] * 4),
            out_shape=[jax.ShapeDtypeStruct((layers, rows, cols), F32)] * 4,
            input_output_aliases={9 + k: k for k in range(len(earlier))},
            compiler_params=_params("parallel"),
        )(where, grad, land, land, land, sibling, w, m, v, *earlier)
    return out


def _adamw_small(w, g, m, v):
    def body(w_ref, g_ref, m_ref, v_ref, d_ref, mo_ref, vo_ref):
        d_ref[...], mo_ref[...], vo_ref[...] = _adamw_math(w_ref[...], g_ref[...], m_ref[...], v_ref[...])

    whole = pl.BlockSpec(memory_space=pltpu.VMEM)
    return pl.pallas_call(
        body, name="adamw_small", in_specs=[whole] * 4, out_specs=[whole] * 3,
        out_shape=[jax.ShapeDtypeStruct(w.shape, F32)] * 3,
        compiler_params=pltpu.CompilerParams(vmem_limit_bytes=VMEM_LIMIT),
    )(w, g, m, v)


def _pack(arrays):
    rows = []
    for a in arrays:
        flat = a.reshape(-1)
        pad = (-flat.shape[0]) % 1024
        rows.append(jnp.pad(flat, (0, pad)).reshape(-1, 128))
    return jnp.concatenate(rows, axis=0)


def _unpack(packed, like):
    out, r = [], 0
    for a in like:
        n = math.prod(a.shape)
        nr = (n + 1023) // 1024 * 8
        out.append(packed[r:r + nr].reshape(-1)[:n].reshape(a.shape))
        r += nr
    return out


KIND = {"ab_in": "col", "ab_out": "row", "sb_in": "col", "sb_out": "row",
        "w1_0": "col", "w1_1": "col", "w2_0": "row", "w2_1": "row"}
GATHER_FIRST = ("ab_in", "ab_out")
GATHER_LATER = {"ffn0": ("w1_0", "w2_0"),
                "rest": ("sb_in", "sb_out", "w1_1", "w2_1")}


class _Exchange:
    def __init__(self, shards):
        x, y, c = _coords()
        q = 2 * x + y
        self.where = jnp.stack([c, 1 - c, q, (q + 1) % 4, (q + 2) % 4, (q + 3) % 4]).astype(jnp.int32)
        block = (4 * x + 2 * y + c).astype(jnp.int32).reshape(1)
        self.full = {n: _place(s, KIND[n], block, f"place_{n}") for n, s in shards.items()}
        first = _all_gather([self.full[n] for n in GATHER_FIRST], [KIND[n] for n in GATHER_FIRST])
        self.full.update(zip(GATHER_FIRST, first))
        self.tokens = []
        self.gathers = {}
        self.scatters = {}
        after = first[-1]
        for key, group in GATHER_LATER.items():
            to_chips, to_sibling = _gather_plans([KIND[n] for n in group])
            bufs = [self.full[n] for n in group]
            sems, bufs, after = _copies_start(f"gather_start_{key}", bufs, to_chips, 4 * len(group), after)
            self.tokens.append(after)
            self.gathers[key] = (group, sems, bufs, to_chips, to_sibling)

    def tie(self, small):
        for token in self.tokens:
            small = small + token[0:1, 0:1]
        self.tokens = []
        return small

    def started(self):
        return tuple(self.tokens)

    def weight(self, name):
        return self.full[name]

    def arrive(self, key, after):
        group, sems, bufs, to_chips, to_sibling = self.gathers[key]
        sems, bufs, token = _copies_wait_start(f"gather_pass_{key}", bufs, sems, after, to_chips, to_sibling,
                                               3 * len(group))
        self.tokens.append(token)
        self.gathers[key] = (group, sems, bufs, to_chips, to_sibling)

    def land(self, key, after):
        group, sems, bufs, _, to_sibling = self.gathers.pop(key)
        self.full.update(zip(group, _copies_wait(f"gather_done_{key}", bufs, sems, after, to_sibling)))

    def reduce(self, key, grads):
        names = list(grads)
        kinds = [KIND[n] for n in names]
        full = [grads[n] for n in names]
        lands = [lax.empty((4,) + _quarter_shape(g, k), BF16) for g, k in zip(full, kinds)]
        plan = _scatter_plan(kinds)
        sems, bufs, token = _copies_start(f"scatter_start_{key}", full + lands, plan, 3 * len(names), full[-1])
        self.tokens.append(token)
        self.scatters[key] = (names, sems, bufs, plan)

    def finish(self, after):
        names, grads, lands = [], [], []
        for key, (group, sems, bufs, plan) in self.scatters.items():
            bufs = _copies_wait(f"scatter_done_{key}", bufs, sems, after, plan)
            names += group
            grads += bufs[:len(group)]
            lands += bufs[len(group):]
        self.scatters = {}
        sums = [_for_sibling(g, l, KIND[n], self.where, f"for_sibling_{n}") for n, g, l in zip(names, grads, lands)]
        return {n: t for n, *t in zip(names, grads, lands, _exchange_siblings(sums))}


SMALL = ("norm_pre_mix", "norm_post_mix", "norm_pre_ffn", "norm_post_ffn", "sgu_ln_g", "sgu_ln_b", "sgu_w", "sgu_b")
ORDER = ("norm_pre_mix", "norm_post_mix", "norm_pre_ffn", "norm_post_ffn", "ab_w_in", "sgu_ln_g", "sgu_ln_b", "sgu_w",
         "sgu_b", "ab_w_out", "sb_w_in", "sb_w_out", "ffn_w1", "ffn_w2")


def kernel(x, norm_pre_mix, norm_post_mix, norm_pre_ffn, norm_post_ffn, ab_w_in, sgu_ln_g, sgu_ln_b, sgu_w, sgu_b, ab_w_out, sb_w_in, sb_w_out, ffn_w1, ffn_w2, loss_target, m_norm_pre_mix, m_norm_post_mix, m_norm_pre_ffn, m_norm_post_ffn, m_ab_w_in, m_sgu_ln_g, m_sgu_ln_b, m_sgu_w, m_sgu_b, m_ab_w_out, m_sb_w_in, m_sb_w_out, m_ffn_w1, m_ffn_w2, v_norm_pre_mix, v_norm_post_mix, v_norm_pre_ffn, v_norm_post_ffn, v_ab_w_in, v_sgu_ln_g, v_sgu_ln_b, v_sgu_w, v_sgu_b, v_ab_w_out, v_sb_w_in, v_sb_w_out, v_ffn_w1, v_ffn_w2):
    W = dict(norm_pre_mix=norm_pre_mix, norm_post_mix=norm_post_mix, norm_pre_ffn=norm_pre_ffn,
             norm_post_ffn=norm_post_ffn, ab_w_in=ab_w_in, sgu_ln_g=sgu_ln_g, sgu_ln_b=sgu_ln_b, sgu_w=sgu_w,
             sgu_b=sgu_b, ab_w_out=ab_w_out, sb_w_in=sb_w_in, sb_w_out=sb_w_out, ffn_w1=ffn_w1, ffn_w2=ffn_w2)
    M = dict(norm_pre_mix=m_norm_pre_mix, norm_post_mix=m_norm_post_mix, norm_pre_ffn=m_norm_pre_ffn,
             norm_post_ffn=m_norm_post_ffn, ab_w_in=m_ab_w_in, sgu_ln_g=m_sgu_ln_g, sgu_ln_b=m_sgu_ln_b,
             sgu_w=m_sgu_w, sgu_b=m_sgu_b, ab_w_out=m_ab_w_out, sb_w_in=m_sb_w_in, sb_w_out=m_sb_w_out,
             ffn_w1=m_ffn_w1, ffn_w2=m_ffn_w2)
    V = dict(norm_pre_mix=v_norm_pre_mix, norm_post_mix=v_norm_post_mix, norm_pre_ffn=v_norm_pre_ffn,
             norm_post_ffn=v_norm_post_ffn, ab_w_in=v_ab_w_in, sgu_ln_g=v_sgu_ln_g, sgu_ln_b=v_sgu_ln_b,
             sgu_w=v_sgu_w, sgu_b=v_sgu_b, ab_w_out=v_ab_w_out, sb_w_in=v_sb_w_in, sb_w_out=v_sb_w_out,
             ffn_w1=v_ffn_w1, ffn_w2=v_ffn_w2)

    shards = {"ab_in": ab_w_in[0], "ab_out": ab_w_out[0], "w1_0": ffn_w1[0], "w2_0": ffn_w2[0],
              "sb_in": sb_w_in[0], "sb_out": sb_w_out[0], "w1_1": ffn_w1[1], "w2_1": ffn_w2[1]}
    comm = _Exchange(shards)
    norms = (norm_pre_mix, norm_post_mix, norm_pre_ffn, norm_post_ffn)
    sgu = (sgu_ln_g, sgu_ln_b, sgu_w[0], sgu_b[0])
    loss, dx, small = _local_step(x[0], loss_target[0], norms, sgu, comm)
    loss = lax.psum(loss, MESH_AXES)
    landed = comm.finish(after=dx)

    out = {}
    for name, layers in (("ab_w_in", ["ab_in"]), ("ab_w_out", ["ab_out"]), ("sb_w_in", ["sb_in"]),
                         ("sb_w_out", ["sb_out"]), ("ffn_w1", ["w1_0", "w1_1"]), ("ffn_w2", ["w2_0", "w2_1"])):
        out[name] = _adamw(W[name], [landed[n] for n in layers], KIND[layers[0]], comm.where, M[name], V[name],
                           f"adamw_{name}")

    small_g = [small["pre_mix"], small["post_mix"], small["pre_ffn"], small["post_ffn"], small["ln_g"],
               small["ln_b"], small["w_s"][None], small["b_s"][None]]
    g_small = _all_reduce_small(_pack(small_g))
    res = _adamw_small(_pack([W[n] for n in SMALL]), g_small, _pack([M[n] for n in SMALL]),
                       _pack([V[n] for n in SMALL]))
    like = [W[n] for n in SMALL]
    for n, *vals in zip(SMALL, *[_unpack(r, like) for r in [g_small] + list(res)]):
        out[n] = vals

    return (loss, dx[None], *[out[n][0] for n in ORDER], *[out[n][1] for n in ORDER],
            *[out[n][2] for n in ORDER], *[out[n][3] for n in ORDER])
```

```python
import functools
import math

import jax
import jax.numpy as jnp
from jax import lax
from jax.experimental import pallas as pl
from jax.experimental.pallas import tpu as pltpu

F32 = jnp.float32
BF16 = jnp.bfloat16

HEAD_DIM = 128
CHUNK = 128
ATT_BLOCK = 128
DILATED_PAIRS = ((128, 1), (512, 4), (2048, 16))
RMS_EPS = 1e-6
LN_EPS = 1e-5
ADAM_LR = 0.001
ADAM_B1 = 0.9
ADAM_B2 = 0.999
ADAM_EPS = 1e-08
ADAM_WD = 0.01
ADAM_STEP = 10
N_DEV = 8
MESH_AXES = ("x", "y", "c")
MASKED = -1e30

V7X_VMEM_BYTES = 64 * 1024 * 1024
VMEM_LIMIT = V7X_VMEM_BYTES - 8 * 1024 * 1024

NN = (((1,), (0,)), ((), ()))
NT = (((1,), (1,)), ((), ()))
TN = (((0,), (0,)), ((), ()))


def _params(*sem):
    return pltpu.CompilerParams(dimension_semantics=sem, vmem_limit_bytes=VMEM_LIMIT)


def _dot(a, b, dims=NN):
    return lax.dot_general(a, b, dims, preferred_element_type=F32)


def _tile(n, preferred):
    if n <= preferred:
        return n
    t = preferred - preferred % 128
    while n % t:
        t -= 128
    assert t > 0, (n, preferred)
    return t


def _matmul(a, b, *, mode, name, out_dtype=F32, tm=1024, tn=1024, tk=2048, epi=None, extras=(), after=()):
    if mode == "nn":
        (M, K), N = a.shape, b.shape[1]
    elif mode == "nt":
        (M, K), N = a.shape, b.shape[0]
    else:
        (K, M), N = a.shape, b.shape[1]
    tm, tn, tk = _tile(M, tm), _tile(N, tn), _tile(K, tk)
    nk = K // tk
    if mode == "tn":
        a_spec = pl.BlockSpec((tk, tm), lambda i, j, k: (k, i))
    else:
        a_spec = pl.BlockSpec((tm, tk), lambda i, j, k: (i, k))
    if mode == "nt":
        b_spec = pl.BlockSpec((tn, tk), lambda i, j, k: (j, k))
    else:
        b_spec = pl.BlockSpec((tk, tn), lambda i, j, k: (k, j))
    o_spec = pl.BlockSpec((tm, tn), lambda i, j, k: (i, j))
    dims = {"nn": NN, "nt": NT, "tn": TN}[mode]
    n_extra = len(extras)
    n_in = n_extra + len(after)

    def finish(acc, refs):
        j = pl.program_id(1)
        if epi is None:
            return acc
        return epi(acc, j, *[r[...] for r in refs])

    if nk == 1:
        def body(a_ref, b_ref, *rest):
            o_ref = rest[n_in]
            acc = _dot(a_ref[...], b_ref[...], dims)
            o_ref[...] = finish(acc, rest[:n_extra]).astype(o_ref.dtype)
        scratch = []
    else:
        def body(a_ref, b_ref, *rest):
            o_ref, acc_ref = rest[n_in], rest[n_in + 1]
            k = pl.program_id(2)

            @pl.when(k == 0)
            def _():
                acc_ref[...] = jnp.zeros_like(acc_ref)

            acc_ref[...] += _dot(a_ref[...], b_ref[...], dims)

            @pl.when(k == nk - 1)
            def _():
                o_ref[...] = finish(acc_ref[...], rest[:n_extra]).astype(o_ref.dtype)
        scratch = [pltpu.VMEM((tm, tn), F32)]

    return pl.pallas_call(
        body,
        name=name,
        grid=(M // tm, N // tn, nk),
        in_specs=[a_spec, b_spec] + [o_spec] * n_extra + [ANY] * len(after),
        out_specs=o_spec,
        out_shape=jax.ShapeDtypeStruct((M, N), out_dtype),
        scratch_shapes=scratch,
        compiler_params=_params("parallel", "parallel", "arbitrary"),
    )(a, b, *extras, *after)


ROWS = 256


def _rms(x):
    return lax.rsqrt(jnp.mean(x * x, axis=-1, keepdims=True) + RMS_EPS)


def _prenorm(x, g, name):
    T, D = x.shape

    def body(x_ref, g_ref, h_ref):
        xv = x_ref[...]
        h_ref[...] = (xv * _rms(xv) * g_ref[...]).astype(BF16)

    row = pl.BlockSpec((ROWS, D), lambda i: (i, 0))
    vec = pl.BlockSpec((1, D), lambda i: (0, 0))
    return pl.pallas_call(
        body, name=name, grid=(T // ROWS,), in_specs=[row, vec], out_specs=row,
        out_shape=jax.ShapeDtypeStruct((T, D), BF16), compiler_params=_params("parallel"),
    )(x, g)


def _postnorm_prenorm(x, y, g_post, g_pre, name):
    T, D = x.shape

    def body(x_ref, y_ref, gp_ref, gn_ref, xo_ref, h_ref):
        yv = y_ref[...]
        xn = x_ref[...] + yv * _rms(yv) * gp_ref[...]
        xo_ref[...] = xn
        h_ref[...] = (xn * _rms(xn) * gn_ref[...]).astype(BF16)

    row = pl.BlockSpec((ROWS, D), lambda i: (i, 0))
    vec = pl.BlockSpec((1, D), lambda i: (0, 0))
    return pl.pallas_call(
        body, name=name, grid=(T // ROWS,), in_specs=[row, row, vec, vec], out_specs=[row, row],
        out_shape=[jax.ShapeDtypeStruct((T, D), F32), jax.ShapeDtypeStruct((T, D), BF16)],
        compiler_params=_params("parallel"),
    )(x, y, g_post, g_pre)


def _postnorm_loss(x, y, g_post, target, name):
    T, D = x.shape

    def body(x_ref, y_ref, gp_ref, t_ref, loss_ref, dx_ref):
        @pl.when(pl.program_id(0) == 0)
        def _():
            loss_ref[...] = jnp.zeros_like(loss_ref)

        yv = y_ref[...]
        err = x_ref[...] + yv * _rms(yv) * gp_ref[...] - t_ref[...]
        dx_ref[...] = err * (1.0 / D)
        loss_ref[...] += 0.5 * jnp.sum(jnp.sum(err * err, axis=-1, keepdims=True) * (1.0 / D))

    row = pl.BlockSpec((ROWS, D), lambda i: (i, 0))
    vec = pl.BlockSpec((1, D), lambda i: (0, 0))
    acc = pl.BlockSpec((8, 128), lambda i: (0, 0))
    return pl.pallas_call(
        body, name=name, grid=(T // ROWS,), in_specs=[row, row, vec, row], out_specs=[acc, row],
        out_shape=[jax.ShapeDtypeStruct((8, 128), F32), jax.ShapeDtypeStruct((T, D), F32)],
        compiler_params=_params("arbitrary"),
    )(x, y, g_post, target)


def _postnorm_bwd(dx, y, g_post, name):
    T, D = dx.shape

    def body(dx_ref, y_ref, g_ref, dy_ref, dg_ref):
        @pl.when(pl.program_id(0) == 0)
        def _():
            dg_ref[...] = jnp.zeros_like(dg_ref)

        yv, dn = y_ref[...], dx_ref[...]
        r = _rms(yv)
        yh = yv * r
        gd = dn * g_ref[...]
        dy_ref[...] = (r * (gd - yh * jnp.mean(yh * gd, axis=-1, keepdims=True))).astype(BF16)
        dg_ref[...] += jnp.sum(dn * yh, axis=0, keepdims=True)

    row = pl.BlockSpec((ROWS, D), lambda i: (i, 0))
    vec = pl.BlockSpec((1, D), lambda i: (0, 0))
    return pl.pallas_call(
        body, name=name, grid=(T // ROWS,), in_specs=[row, row, vec], out_specs=[row, vec],
        out_shape=[jax.ShapeDtypeStruct((T, D), BF16), jax.ShapeDtypeStruct((1, D), F32)],
        compiler_params=_params("arbitrary"),
    )(dx, y, g_post)


def _prenorm_bwd(dx_out, dh, x, g_pre, name):
    T, D = x.shape

    def body(dxo_ref, dh_ref, x_ref, g_ref, dx_ref, dg_ref):
        @pl.when(pl.program_id(0) == 0)
        def _():
            dg_ref[...] = jnp.zeros_like(dg_ref)

        xv, dhv = x_ref[...], dh_ref[...]
        r = _rms(xv)
        xh = xv * r
        gd = dhv * g_ref[...]
        dx_ref[...] = dxo_ref[...] + r * (gd - xh * jnp.mean(xh * gd, axis=-1, keepdims=True))
        dg_ref[...] += jnp.sum(dhv * xh, axis=0, keepdims=True)

    row = pl.BlockSpec((ROWS, D), lambda i: (i, 0))
    vec = pl.BlockSpec((1, D), lambda i: (0, 0))
    return pl.pallas_call(
        body, name=name, grid=(T // ROWS,), in_specs=[row, row, row, vec], out_specs=[row, vec],
        out_shape=[jax.ShapeDtypeStruct((T, D), F32), jax.ShapeDtypeStruct((1, D), F32)],
        compiler_params=_params("arbitrary"),
    )(dx_out, dh, x, g_pre)


_INV_SQRT2 = 1.0 / math.sqrt(2.0)
_INV_SQRT2PI = 1.0 / math.sqrt(2.0 * math.pi)


def _gelu(x):
    return 0.5 * x * (1.0 + lax.erf(x * _INV_SQRT2))


def _gelu_grad(x):
    return 0.5 * (1.0 + lax.erf(x * _INV_SQRT2)) + x * jnp.exp(-0.5 * x * x) * _INV_SQRT2PI


def _layernorm_stats(x):
    mu = jnp.mean(x, axis=-1, keepdims=True)
    xc = x - mu
    rstd = lax.rsqrt(jnp.mean(xc * xc, axis=-1, keepdims=True) + LN_EPS)
    return xc * rstd, rstd


def _tril_mask():
    i = lax.broadcasted_iota(jnp.int32, (CHUNK, CHUNK), 0)
    j = lax.broadcasted_iota(jnp.int32, (CHUNK, CHUNK), 1)
    return j <= i


SGU_ROWS = 512


def _sgu_fwd(z, ln_g, ln_b, w_s, b_t, name):
    T = z.shape[0]
    A = ln_g.shape[1]
    G = A // 128
    rows = min(SGU_ROWS, T)

    def body(u_ref, v_ref, g_ref, b_ref, w_ref, bt_ref, o_ref):
        mask = _tril_mask()
        for c in range(rows // CHUNK):
            rs = pl.ds(c * CHUNK, CHUNK)
            xh, _ = _layernorm_stats(_gelu(v_ref[rs, :]))
            vn = (xh * g_ref[...] + b_ref[...]).astype(BF16)
            for g in range(G):
                cs = pl.ds(g * 128, 128)
                w = jnp.where(mask, w_ref[g], 0.0).astype(BF16)
                mixed = _dot(w, vn[:, g * 128:(g + 1) * 128]) + bt_ref[:, g:g + 1]
                o_ref[rs, cs] = (_gelu(u_ref[rs, cs]) * mixed).astype(BF16)

    return pl.pallas_call(
        body, name=name, grid=(T // rows,),
        in_specs=[
            pl.BlockSpec((rows, A), lambda i: (i, 0)),
            pl.BlockSpec((rows, A), lambda i: (i, 1)),
            pl.BlockSpec((1, A), lambda i: (0, 0)),
            pl.BlockSpec((1, A), lambda i: (0, 0)),
            pl.BlockSpec((G, CHUNK, CHUNK), lambda i: (0, 0, 0)),
            pl.BlockSpec((CHUNK, G), lambda i: (0, 0)),
        ],
        out_specs=pl.BlockSpec((rows, A), lambda i: (i, 0)),
        out_shape=jax.ShapeDtypeStruct((T, A), BF16),
        compiler_params=_params("parallel"),
    )(z, z, ln_g, ln_b, w_s, b_t)


def _sgu_bwd(z, dcat, ln_g, ln_b, w_s, b_t, name):
    T = z.shape[0]
    A = ln_g.shape[1]
    G = A // 128
    rows = min(SGU_ROWS, T)

    def body(u_ref, v_ref, da_ref, g_ref, b_ref, w_ref, bt_ref, dz_ref, dg_ref, db_ref, dw_ref, dbt_ref, dvn_ref):
        @pl.when(pl.program_id(0) == 0)
        def _():
            dg_ref[...] = jnp.zeros_like(dg_ref)
            db_ref[...] = jnp.zeros_like(db_ref)
            dw_ref[...] = jnp.zeros_like(dw_ref)
            dbt_ref[...] = jnp.zeros_like(dbt_ref)

        mask = _tril_mask()
        for c in range(rows // CHUNK):
            rs = pl.ds(c * CHUNK, CHUNK)
            vv = v_ref[rs, :]
            gv = _gelu(vv)
            xh, rstd = _layernorm_stats(gv)
            vn = (xh * g_ref[...] + b_ref[...]).astype(BF16)
            for g in range(G):
                cs = pl.ds(g * 128, 128)
                w = jnp.where(mask, w_ref[g], 0.0).astype(BF16)
                vg = vn[:, g * 128:(g + 1) * 128]
                mixed = _dot(w, vg) + bt_ref[:, g:g + 1]
                uu = u_ref[rs, cs]
                da = da_ref[rs, cs]
                dz_ref[rs, cs] = (da * mixed * _gelu_grad(uu)).astype(BF16)
                dm = da * _gelu(uu)
                dmb = dm.astype(BF16)
                dbt_ref[:, g:g + 1] += jnp.sum(dm, axis=1, keepdims=True)
                dw_ref[g] += jnp.where(mask, _dot(dmb, vg, NT), 0.0)
                dvn_ref[:, cs] = _dot(w, dmb, TN)
            dvn = dvn_ref[...]
            dg_ref[...] += jnp.sum(dvn * xh, axis=0, keepdims=True)
            db_ref[...] += jnp.sum(dvn, axis=0, keepdims=True)
            dxh = dvn * g_ref[...]
            dgv = rstd * (dxh - jnp.mean(dxh, axis=-1, keepdims=True)
                          - xh * jnp.mean(dxh * xh, axis=-1, keepdims=True))
            dz_ref[rs, pl.ds(A, A)] = (dgv * _gelu_grad(vv)).astype(BF16)

    vec = pl.BlockSpec((1, A), lambda i: (0, 0))
    wsp = pl.BlockSpec((G, CHUNK, CHUNK), lambda i: (0, 0, 0))
    bsp = pl.BlockSpec((CHUNK, G), lambda i: (0, 0))
    return pl.pallas_call(
        body, name=name, grid=(T // rows,),
        in_specs=[
            pl.BlockSpec((rows, A), lambda i: (i, 0)),
            pl.BlockSpec((rows, A), lambda i: (i, 1)),
            pl.BlockSpec((rows, A), lambda i: (i, 0)),
            vec, vec, wsp, bsp,
        ],
        out_specs=[pl.BlockSpec((rows, 2 * A), lambda i: (i, 0)), vec, vec, wsp, bsp],
        out_shape=[
            jax.ShapeDtypeStruct((T, 2 * A), BF16),
            jax.ShapeDtypeStruct((1, A), F32),
            jax.ShapeDtypeStruct((1, A), F32),
            jax.ShapeDtypeStruct((G, CHUNK, CHUNK), F32),
            jax.ShapeDtypeStruct((CHUNK, G), F32),
        ],
        scratch_shapes=[pltpu.VMEM((CHUNK, A), F32)],
        compiler_params=_params("arbitrary"),
    )(z, z, dcat, ln_g, ln_b, w_s, b_t)


def _alibi_row(B, d):
    H = B // HEAD_DIM
    slopes = [d * 2.0 ** (-8.0 * (h + 1.0) / H) for h in range(H)]
    return jnp.repeat(jnp.asarray(slopes, F32), HEAD_DIM)[None, :]


def _dil_scores(q, k, slope_d, valid, dist):
    s = _dot(q, k, NT) - slope_d * dist
    return jnp.where(valid, s, MASKED)


def _dil_layout(T, B, d):
    H = B // HEAD_DIM
    hp = H if d == 1 else 1
    return hp, H // hp, T // (d * ATT_BLOCK)


def _dil_rows(ref, r, d, cs):
    return ref[pl.ds(r, ATT_BLOCK, stride=d), cs] if d > 1 else ref[:, cs]


def _dil_put(ref, r, d, cs, val):
    if d > 1:
        ref[pl.ds(r, ATT_BLOCK, stride=d), cs] = val
    else:
        ref[:, cs] = val


def _dilated_fwd(z, d, B, name):
    T = z.shape[0]
    H = B // HEAD_DIM
    hp, groups, nb = _dil_layout(T, B, d)
    cw = hp * HEAD_DIM
    scale = HEAD_DIM ** -0.5
    blk = ATT_BLOCK

    def body(q_ref, kp_ref, kc_ref, vp_ref, vc_ref, sl_ref, o_ref, l_ref):
        n = pl.program_id(1)
        qi = lax.broadcasted_iota(jnp.int32, (blk, 2 * blk), 0)
        kj = lax.broadcasted_iota(jnp.int32, (blk, 2 * blk), 1)
        dist = qi + blk - kj
        valid = (dist >= 0) & (dist <= blk) & ((kj >= blk) | (n > 0))
        distf = dist.astype(F32)
        for r in range(d):
            for hh in range(hp):
                cs = pl.ds(hh * HEAD_DIM, HEAD_DIM)
                q = (_dil_rows(q_ref, r, d, cs) * scale).astype(BF16)
                k = jnp.concatenate([_dil_rows(kp_ref, r, d, cs), _dil_rows(kc_ref, r, d, cs)], axis=0).astype(BF16)
                v = jnp.concatenate([_dil_rows(vp_ref, r, d, cs), _dil_rows(vc_ref, r, d, cs)], axis=0).astype(BF16)
                s = _dil_scores(q, k, sl_ref[:, cs][:, :1], valid, distf)
                m = jnp.max(s, axis=-1, keepdims=True)
                p = jnp.exp(s - m)
                den = jnp.sum(p, axis=-1, keepdims=True)
                _dil_put(o_ref, r, d, cs, _dot(p.astype(BF16), v) / den)
                _dil_put(l_ref, r, d, cs, jnp.broadcast_to(m + jnp.log(den), (blk, HEAD_DIM)))

    def col(unit):
        return lambda g, n: (n, unit * groups + g)

    def col_prev(unit):
        return lambda g, n: (jnp.maximum(n - 1, 0), unit * groups + g)

    bs = (d * blk, cw)
    out = pl.BlockSpec(bs, lambda g, n: (n, g))
    return pl.pallas_call(
        body, name=name, grid=(groups, nb),
        in_specs=[pl.BlockSpec(bs, col(2)), pl.BlockSpec(bs, col_prev(3)), pl.BlockSpec(bs, col(3)),
                  pl.BlockSpec(bs, col_prev(4)), pl.BlockSpec(bs, col(4)), pl.BlockSpec((1, cw), lambda g, n: (0, g))],
        out_specs=[out, out],
        out_shape=[jax.ShapeDtypeStruct((T, B), F32)] * 2,
        compiler_params=_params("parallel", "parallel"),
    )(z, z, z, z, z, _alibi_row(B, d))


def _dilated_merge(os_, ls_, name):
    T, B = os_[0].shape

    def body(o1, o2, o3, l1, l2, l3, ob_ref, of_ref, lt_ref):
        a, b, c = l1[...], l2[...], l3[...]
        m = jnp.maximum(jnp.maximum(a, b), c)
        ea, eb, ec = jnp.exp(a - m), jnp.exp(b - m), jnp.exp(c - m)
        tot = ea + eb + ec
        o = (ea * o1[...] + eb * o2[...] + ec * o3[...]) / tot
        of_ref[...] = o
        ob_ref[...] = o.astype(BF16)
        lt_ref[...] = m + jnp.log(tot)

    row = pl.BlockSpec((ROWS, B), lambda i: (i, 0))
    return pl.pallas_call(
        body, name=name, grid=(T // ROWS,), in_specs=[row] * 6, out_specs=[row] * 3,
        out_shape=[jax.ShapeDtypeStruct((T, B), BF16), jax.ShapeDtypeStruct((T, B), F32),
                   jax.ShapeDtypeStruct((T, B), F32)],
        compiler_params=_params("parallel"),
    )(*os_, *ls_)


def _dilated_delta(dcat, o, name):
    T, B = o.shape
    H = B // HEAD_DIM

    def body(do_ref, o_ref, d_ref):
        for h in range(H):
            cs = pl.ds(h * HEAD_DIM, HEAD_DIM)
            dsum = jnp.sum(do_ref[:, cs] * o_ref[:, cs], axis=-1, keepdims=True)
            d_ref[:, cs] = jnp.broadcast_to(dsum, (ROWS, HEAD_DIM))

    row = pl.BlockSpec((ROWS, B), lambda i: (i, 0))
    return pl.pallas_call(
        body, name=name, grid=(T // ROWS,),
        in_specs=[pl.BlockSpec((ROWS, B), lambda i: (i, 1)), row], out_specs=row,
        out_shape=jax.ShapeDtypeStruct((T, B), F32), compiler_params=_params("parallel"),
    )(dcat, o)


def _dilated_dq(z, dcat, lse, delta, d, B, name):
    T = z.shape[0]
    hp, groups, nb = _dil_layout(T, B, d)
    cw = hp * HEAD_DIM
    scale = HEAD_DIM ** -0.5
    blk = ATT_BLOCK

    def body(q_ref, kp_ref, kc_ref, vp_ref, vc_ref, do_ref, l_ref, dl_ref, sl_ref, dq_ref):
        n = pl.program_id(1)
        qi = lax.broadcasted_iota(jnp.int32, (blk, 2 * blk), 0)
        kj = lax.broadcasted_iota(jnp.int32, (blk, 2 * blk), 1)
        dist = qi + blk - kj
        valid = (dist >= 0) & (dist <= blk) & ((kj >= blk) | (n > 0))
        distf = dist.astype(F32)
        for r in range(d):
            for hh in range(hp):
                cs = pl.ds(hh * HEAD_DIM, HEAD_DIM)
                q = (_dil_rows(q_ref, r, d, cs) * scale).astype(BF16)
                k = jnp.concatenate([_dil_rows(kp_ref, r, d, cs), _dil_rows(kc_ref, r, d, cs)], axis=0).astype(BF16)
                v = jnp.concatenate([_dil_rows(vp_ref, r, d, cs), _dil_rows(vc_ref, r, d, cs)], axis=0).astype(BF16)
                s = _dil_scores(q, k, sl_ref[:, cs][:, :1], valid, distf)
                p = jnp.exp(s - _dil_rows(l_ref, r, d, cs)[:, :1])
                dp = _dot(_dil_rows(do_ref, r, d, cs).astype(BF16), v, NT)
                ds = p * (dp - _dil_rows(dl_ref, r, d, cs)[:, :1])
                _dil_put(dq_ref, r, d, cs, _dot(ds.astype(BF16), k))

    def col(unit):
        return lambda g, n: (n, unit * groups + g)

    def col_prev(unit):
        return lambda g, n: (jnp.maximum(n - 1, 0), unit * groups + g)

    bs = (d * blk, cw)
    out = pl.BlockSpec(bs, lambda g, n: (n, g))
    return pl.pallas_call(
        body, name=name, grid=(groups, nb),
        in_specs=[pl.BlockSpec(bs, col(2)), pl.BlockSpec(bs, col_prev(3)), pl.BlockSpec(bs, col(3)),
                  pl.BlockSpec(bs, col_prev(4)), pl.BlockSpec(bs, col(4)), pl.BlockSpec(bs, col(1)), out, out,
                  pl.BlockSpec((1, cw), lambda g, n: (0, g))],
        out_specs=out,
        out_shape=jax.ShapeDtypeStruct((T, B), F32),
        compiler_params=_params("parallel", "parallel"),
    )(z, z, z, z, z, dcat, lse, delta, _alibi_row(B, d))


def _dilated_dkv(z, dcat, lse, delta, d, B, name):
    T = z.shape[0]
    hp, groups, nb = _dil_layout(T, B, d)
    cw = hp * HEAD_DIM
    scale = HEAD_DIM ** -0.5
    blk = ATT_BLOCK

    def body(k_ref, v_ref, qa_ref, qb_ref, doa_ref, dob_ref, la_ref, lb_ref, da_ref, db_ref, sl_ref, dk_ref, dv_ref):
        m = pl.program_id(1)
        qi = lax.broadcasted_iota(jnp.int32, (2 * blk, blk), 0)
        kj = lax.broadcasted_iota(jnp.int32, (2 * blk, blk), 1)
        dist = qi - kj
        valid = (dist >= 0) & (dist <= blk) & ((qi < blk) | (m + 1 < nb))
        distf = dist.astype(F32)
        for r in range(d):
            for hh in range(hp):
                cs = pl.ds(hh * HEAD_DIM, HEAD_DIM)

                def both(a_ref, b_ref):
                    return jnp.concatenate([_dil_rows(a_ref, r, d, cs), _dil_rows(b_ref, r, d, cs)], axis=0)

                q = (both(qa_ref, qb_ref) * scale).astype(BF16)
                do = both(doa_ref, dob_ref).astype(BF16)
                k = _dil_rows(k_ref, r, d, cs).astype(BF16)
                v = _dil_rows(v_ref, r, d, cs).astype(BF16)
                s = _dil_scores(q, k, sl_ref[:, cs][:, :1], valid, distf)
                p = jnp.exp(jnp.where(valid, s - both(la_ref, lb_ref)[:, :1], MASKED))
                _dil_put(dv_ref, r, d, cs, _dot(p.astype(BF16), do, TN))
                ds = p * (_dot(do, v, NT) - both(da_ref, db_ref)[:, :1])
                _dil_put(dk_ref, r, d, cs, _dot(ds.astype(BF16), q, TN))

    def col(unit):
        return lambda g, m: (m, unit * groups + g)

    def nxt(unit):
        return lambda g, m: (jnp.minimum(m + 1, nb - 1), unit * groups + g)

    bs = (d * blk, cw)
    out = pl.BlockSpec(bs, lambda g, m: (m, g))
    return pl.pallas_call(
        body, name=name, grid=(groups, nb),
        in_specs=[pl.BlockSpec(bs, col(3)), pl.BlockSpec(bs, col(4)),
                  pl.BlockSpec(bs, col(2)), pl.BlockSpec(bs, nxt(2)),
                  pl.BlockSpec(bs, col(1)), pl.BlockSpec(bs, nxt(1)),
                  out, pl.BlockSpec(bs, nxt(0)), out, pl.BlockSpec(bs, nxt(0)),
                  pl.BlockSpec((1, cw), lambda g, m: (0, g))],
        out_specs=[out, out],
        out_shape=[jax.ShapeDtypeStruct((T, B), F32)] * 2,
        compiler_params=_params("parallel", "parallel"),
    )(z, z, z, z, dcat, dcat, lse, lse, delta, delta, _alibi_row(B, d))


def _dilated_combine(dqs, dks, dvs, name):
    T, B = dqs[0].shape
    scale = HEAD_DIM ** -0.5

    def body(q1, q2, q3, k1, k2, k3, v1, v2, v3, o_ref):
        o_ref[:, pl.ds(0, B)] = ((q1[...] + q2[...] + q3[...]) * scale).astype(BF16)
        o_ref[:, pl.ds(B, B)] = (k1[...] + k2[...] + k3[...]).astype(BF16)
        o_ref[:, pl.ds(2 * B, B)] = (v1[...] + v2[...] + v3[...]).astype(BF16)

    row = pl.BlockSpec((ROWS, B), lambda i: (i, 0))
    return pl.pallas_call(
        body, name=name, grid=(T // ROWS,), in_specs=[row] * 9,
        out_specs=pl.BlockSpec((ROWS, 3 * B), lambda i: (i, 0)),
        out_shape=jax.ShapeDtypeStruct((T, 3 * B), BF16), compiler_params=_params("parallel"),
    )(*dqs, *dks, *dvs)


SB_QUERY_ROWS = 512
SB_KEYS = 2 * ATT_BLOCK


def _tri_and_ones(pred):
    rows = lax.broadcasted_iota(jnp.int32, (2 * ATT_BLOCK, 2 * ATT_BLOCK), 0) % ATT_BLOCK
    cols = lax.broadcasted_iota(jnp.int32, (2 * ATT_BLOCK, 2 * ATT_BLOCK), 1)
    return ((cols >= ATT_BLOCK) | pred(rows, cols)).astype(BF16)


def _running(x, tri):
    hi = x.astype(BF16)
    lo = (x - hi.astype(F32)).astype(BF16)
    return _dot(jnp.concatenate([hi, lo], axis=1), tri)


def _sb_mask(query_rows, s):
    rows = lax.broadcasted_iota(jnp.int32, (query_rows, SB_KEYS), 0)
    cols = lax.broadcasted_iota(jnp.int32, (query_rows, SB_KEYS), 1)
    return cols + s * SB_KEYS < rows


def _log_sigmoids(z):
    e = jnp.exp(-jnp.abs(z))
    ls = jnp.minimum(z, 0.0) - jnp.log(1.0 + e)
    return ls, ls - z, e


def _sb_fwd(qkv, W, name):
    T = qkv.shape[0]
    H = W // HEAD_DIM
    blk = ATT_BLOCK
    qb = min(SB_QUERY_ROWS, T)
    per = qb // SB_KEYS

    def body(q_ref, k_ref, v_ref, o_ref, lt_ref, acc_ref):
        i = pl.program_id(1)
        q = q_ref[...]
        tri = _tri_and_ones(lambda r, c: r > c)
        lt_ref[...] = jnp.zeros_like(lt_ref)
        acc_ref[...] = jnp.zeros_like(acc_ref)

        def tile(j, mask):
            ks = pl.ds(pl.multiple_of(j * SB_KEYS, SB_KEYS), SB_KEYS)
            z = _dot(q, k_ref[ks, :], NT)
            ls, lm, _ = _log_sigmoids(z)
            if mask is not None:
                lm = jnp.where(mask, lm, 0.0)
            later = lt_ref[...]
            second = _running(lm[:, blk:], tri)
            first = _running(lm[:, :blk], tri)
            after_first = later + second[:, blk:]
            a = jnp.exp(ls + jnp.concatenate([first[:, :blk] + after_first, second[:, :blk] + later], axis=1))
            if mask is not None:
                a = jnp.where(mask, a, 0.0)
            acc_ref[...] += _dot(a.astype(BF16), v_ref[ks, :])
            lt_ref[...] = after_first + first[:, blk:]

        for s in reversed(range(per)):
            tile(i * per + s, _sb_mask(qb, s))

        def step(jj, _):
            for s in range(per):
                tile((i - jj) * per - 1 - s, None)
            return 0

        lax.fori_loop(0, i, step, 0)
        o_ref[...] = acc_ref[...].astype(BF16)

    qs = pl.BlockSpec((qb, HEAD_DIM), lambda h, i: (i, h))
    return pl.pallas_call(
        body, name=name, grid=(H, T // qb),
        in_specs=[qs, pl.BlockSpec((T, HEAD_DIM), lambda h, i: (0, H + h)),
                  pl.BlockSpec((T, HEAD_DIM), lambda h, i: (0, 2 * H + h))],
        out_specs=[qs, qs],
        out_shape=[jax.ShapeDtypeStruct((T, W), BF16), jax.ShapeDtypeStruct((T, W), F32)],
        scratch_shapes=[pltpu.VMEM((qb, HEAD_DIM), F32)],
        compiler_params=_params("parallel", "arbitrary"),
    )(qkv, qkv, qkv)


def _sb_bwd(qkv, do, ltot, W, name):
    T = qkv.shape[0]
    H = W // HEAD_DIM
    blk = ATT_BLOCK
    nkb = T // SB_KEYS
    qb = min(SB_QUERY_ROWS, T)
    per = qb // SB_KEYS

    def body(q_ref, k_ref, v_ref, do_ref, lt_ref, dq_ref, dkt_ref, dvt_ref, qt_ref, dot_ref, plm_ref, pg_ref):
        i = pl.program_id(1)

        @pl.when(i == 0)
        def _():
            dkt_ref[...] = jnp.zeros_like(dkt_ref)
            dvt_ref[...] = jnp.zeros_like(dvt_ref)

        q = q_ref[...]
        do = do_ref[...]
        qt_ref[...] = q.astype(F32).T.astype(BF16)
        dot_ref[...] = do.astype(F32).T.astype(BF16)
        upto = _tri_and_ones(lambda r, c: r <= c)
        before = _tri_and_ones(lambda r, c: r < c)
        plm_ref[...] = jnp.zeros_like(plm_ref)
        pg_ref[...] = jnp.zeros_like(pg_ref)
        dq_ref[...] = jnp.zeros_like(dq_ref)

        def tile(j, mask):
            ks = pl.ds(pl.multiple_of(j * SB_KEYS, SB_KEYS), SB_KEYS)
            k = k_ref[ks, :]
            v = v_ref[ks, :]
            z = _dot(q, k, NT)
            ls, lm, e = _log_sigmoids(z)
            if mask is not None:
                lm = jnp.where(mask, lm, 0.0)
            earlier = plm_ref[...]
            first = _running(lm[:, :blk], upto)
            second = _running(lm[:, blk:], upto)
            upto_first = earlier + first[:, blk:]
            seen = jnp.concatenate([first[:, :blk] + earlier, second[:, :blk] + upto_first], axis=1)
            ltot = lt_ref[...]
            a = jnp.exp(ls + (jnp.concatenate([ltot, ltot], axis=1) - seen))
            if mask is not None:
                a = jnp.where(mask, a, 0.0)
            g = a * _dot(do, v, NT)
            g_earlier = pg_ref[...]
            g_first = _running(g[:, :blk], before)
            g_second = _running(g[:, blk:], before)
            g_upto_first = g_earlier + g_first[:, blk:]
            gsum = jnp.concatenate([g_first[:, :blk] + g_earlier, g_second[:, :blk] + g_upto_first], axis=1)
            r = 1.0 / (1.0 + e)
            pos = z >= 0.0
            sig = jnp.where(pos, r, e * r)
            nsig = jnp.where(pos, e * r, r)
            dz = g * nsig - gsum * sig
            if mask is not None:
                dz = jnp.where(mask, dz, 0.0)
            dzb = dz.astype(BF16)
            dkt_ref[j] += _dot(qt_ref[...], dzb)
            dvt_ref[j] += _dot(dot_ref[...], a.astype(BF16))
            dq_ref[...] += _dot(dzb, k)
            plm_ref[...] = upto_first + second[:, blk:]
            pg_ref[...] = g_upto_first + g_second[:, blk:]

        def step(jj, _):
            for s in range(per):
                tile(jj * per + s, None)
            return 0

        lax.fori_loop(0, i, step, 0)
        for s in range(per):
            tile(i * per + s, _sb_mask(qb, s))

    qs = pl.BlockSpec((qb, HEAD_DIM), lambda h, i: (i, h))
    res = pl.BlockSpec((None, nkb, HEAD_DIM, SB_KEYS), lambda h, i: (h, 0, 0, 0))
    return pl.pallas_call(
        body, name=name, grid=(H, T // qb),
        in_specs=[qs, pl.BlockSpec((T, HEAD_DIM), lambda h, i: (0, H + h)),
                  pl.BlockSpec((T, HEAD_DIM), lambda h, i: (0, 2 * H + h)), qs, qs],
        out_specs=[qs, res, res],
        out_shape=[jax.ShapeDtypeStruct((T, W), F32)] + [jax.ShapeDtypeStruct((H, nkb, HEAD_DIM, SB_KEYS), F32)] * 2,
        scratch_shapes=[pltpu.VMEM((HEAD_DIM, qb), BF16), pltpu.VMEM((HEAD_DIM, qb), BF16),
                        pltpu.VMEM((qb, HEAD_DIM), F32), pltpu.VMEM((qb, HEAD_DIM), F32)],
        compiler_params=_params("parallel", "arbitrary"),
    )(qkv, qkv, qkv, do, ltot)


def _sb_pack(dq, dkt, dvt, name):
    T, W = dq.shape
    H = W // HEAD_DIM
    blk = SB_KEYS
    scale = HEAD_DIM ** -0.5

    def body(q_ref, kt_ref, vt_ref, o_ref):
        o_ref[:, pl.ds(0, W)] = (q_ref[...] * scale).astype(BF16)
        for h in range(H):
            o_ref[:, pl.ds(W + h * HEAD_DIM, HEAD_DIM)] = kt_ref[h].T.astype(BF16)
            o_ref[:, pl.ds(2 * W + h * HEAD_DIM, HEAD_DIM)] = vt_ref[h].T.astype(BF16)

    tr = pl.BlockSpec((H, None, HEAD_DIM, blk), lambda i: (0, i, 0, 0))
    return pl.pallas_call(
        body, name=name, grid=(T // blk,), in_specs=[pl.BlockSpec((blk, W), lambda i: (i, 0)), tr, tr],
        out_specs=pl.BlockSpec((blk, 3 * W), lambda i: (i, 0)),
        out_shape=jax.ShapeDtypeStruct((T, 3 * W), BF16), compiler_params=_params("parallel"),
    )(dq, dkt, dvt)


def _local_step(x, target, norms, sgu, comm):
    T, D = x.shape
    A = D // 2
    pre_mix, post_mix, pre_ffn, post_ffn = norms
    ln_g, ln_b, w_s, b_s = sgu
    b_t = b_s.T
    scale = HEAD_DIM ** -0.5

    def vec(p, layer):
        return comm.tie(p[layer:layer + 1])

    h0 = _prenorm(x, vec(pre_mix, 0), "prenorm0")
    z = _matmul(h0, comm.weight("ab_in"), mode="nn", name="ab_in_fwd")
    a_out = _sgu_fwd(z, ln_g, ln_b, w_s, b_t, "sgu_fwd")
    branch = [_dilated_fwd(z, d, A, f"dilated_fwd_{d}") for _, d in DILATED_PAIRS]
    b_out, o_dil, lse_dil = _dilated_merge([b[0] for b in branch], [b[1] for b in branch], "dilated_merge")
    cat = jnp.concatenate([a_out, b_out], axis=-1)
    y0 = _matmul(cat, comm.weight("ab_out"), mode="nn", name="ab_out_fwd")
    comm.arrive("ffn0", after=y0)
    x1, h1 = _postnorm_prenorm(x, y0, vec(post_mix, 0), vec(pre_ffn, 0), "norm_mix0")
    comm.land("ffn0", after=h1)

    def relu2(acc, j):
        r = jnp.maximum(acc, 0.0)
        return r * r

    f0 = _matmul(h1, comm.weight("w1_0"), mode="nn", name="ffn0_w1_fwd", out_dtype=BF16, epi=relu2)
    y1 = _matmul(f0, comm.weight("w2_0"), mode="nn", name="ffn0_w2_fwd")
    comm.arrive("rest", after=y1)
    x2, h2 = _postnorm_prenorm(x1, y1, vec(post_ffn, 0), vec(pre_mix, 1), "norm_ffn0")
    comm.land("rest", after=h2)

    tn_qkv = _tile(D, 1024)
    nq = D // tn_qkv

    def scale_q(acc, j):
        return jnp.where(j < nq, acc * scale, acc)

    qkv = _matmul(h2, comm.weight("sb_in"), mode="nn", name="sb_in_fwd", out_dtype=BF16, tn=tn_qkv, epi=scale_q)
    o_sb, ltot = _sb_fwd(qkv, D, "sb_fwd")
    y2 = _matmul(o_sb, comm.weight("sb_out"), mode="nn", name="sb_out_fwd")
    x3, h3 = _postnorm_prenorm(x2, y2, vec(post_mix, 1), vec(pre_ffn, 1), "norm_mix1")
    f1 = _matmul(h3, comm.weight("w1_1"), mode="nn", name="ffn1_w1_fwd", out_dtype=BF16, epi=relu2)
    y3 = _matmul(f1, comm.weight("w2_1"), mode="nn", name="ffn1_w2_fwd")
    loss_tile, dx4 = _postnorm_loss(x3, y3, vec(post_ffn, 1), target, "norm_loss")
    loss = loss_tile[0, 0]

    def relu2_bwd(acc, j, f):
        return acc * (2.0 * jnp.sqrt(f.astype(F32)))

    def ffn_bwd(dx_out, x_in, h, f, y, layer, tag):
        dy, dg_post = _postnorm_bwd(dx_out, y, vec(post_ffn, layer), f"ffn{tag}_postnorm_bwd")
        g_w2 = _matmul(f, dy, mode="tn", name=f"ffn{tag}_w2_wgrad", out_dtype=BF16)
        da = _matmul(dy, comm.weight(f"w2_{layer}"), mode="nt", name=f"ffn{tag}_w2_dgrad", out_dtype=BF16,
                     epi=relu2_bwd, extras=(f,))
        g_w1 = _matmul(h, da, mode="tn", name=f"ffn{tag}_w1_wgrad", out_dtype=BF16)
        comm.reduce(f"ffn{tag}", {f"w2_{layer}": g_w2, f"w1_{layer}": g_w1})
        dh = _matmul(da, comm.weight(f"w1_{layer}"), mode="nt", name=f"ffn{tag}_w1_dgrad", after=comm.started())
        dx_in, dg_pre = _prenorm_bwd(dx_out, dh, x_in, vec(pre_ffn, layer), f"ffn{tag}_prenorm_bwd")
        return dx_in, dg_pre, dg_post

    dx3, dg_pre_ffn1, dg_post_ffn1 = ffn_bwd(dx4, x3, h3, f1, y3, 1, "1")

    dy2, dg_post_mix1 = _postnorm_bwd(dx3, y2, vec(post_mix, 1), "sb_postnorm_bwd")
    g_sb_out = _matmul(o_sb, dy2, mode="tn", name="sb_out_wgrad", out_dtype=BF16)
    do_sb = _matmul(dy2, comm.weight("sb_out"), mode="nt", name="sb_out_dgrad", out_dtype=BF16)
    dq, dk, dv = _sb_bwd(qkv, do_sb, ltot, D, "sb_bwd")
    comm.settle("ffn1", after=dq)
    dqkv = _sb_pack(dq, dk, dv, "sb_pack")
    g_sb_in = _matmul(h2, dqkv, mode="tn", name="sb_in_wgrad", out_dtype=BF16)
    comm.reduce("sb", {"sb_out": g_sb_out, "sb_in": g_sb_in})
    dh2 = _matmul(dqkv, comm.weight("sb_in"), mode="nt", name="sb_in_dgrad", after=comm.started())
    dx2, dg_pre_mix1 = _prenorm_bwd(dx3, dh2, x2, vec(pre_mix, 1), "sb_prenorm_bwd")

    dx1, dg_pre_ffn0, dg_post_ffn0 = ffn_bwd(dx2, x1, h1, f0, y1, 0, "0")
    comm.settle("sb", after=dx1)

    dy0, dg_post_mix0 = _postnorm_bwd(dx1, y0, vec(post_mix, 0), "ab_postnorm_bwd")
    g_ab_out = _matmul(cat, dy0, mode="tn", name="ab_out_wgrad", out_dtype=BF16)
    comm.reduce("ab_out", {"ab_out": g_ab_out})
    dcat = _matmul(dy0, comm.weight("ab_out"), mode="nt", name="ab_out_dgrad", after=comm.started())
    duv, d_ln_g, d_ln_b, d_w_s, d_b_t = _sgu_bwd(z, dcat, ln_g, ln_b, w_s, b_t, "sgu_bwd")
    delta = _dilated_delta(dcat, o_dil, "dilated_delta")
    dqs, dks, dvs = [], [], []
    for _, d in DILATED_PAIRS:
        dqs.append(_dilated_dq(z, dcat, lse_dil, delta, d, A, f"dilated_dq_{d}"))
        dk_b, dv_b = _dilated_dkv(z, dcat, lse_dil, delta, d, A, f"dilated_dkv_{d}")
        dks.append(dk_b)
        dvs.append(dv_b)
    dqkv0 = _dilated_combine(dqs, dks, dvs, "dilated_combine")
    dz = jnp.concatenate([duv, dqkv0], axis=-1)
    g_ab_in = _matmul(h0, dz, mode="tn", name="ab_in_wgrad", out_dtype=BF16)
    comm.reduce("ab_in", {"ab_in": g_ab_in})
    dh0 = _matmul(dz, comm.weight("ab_in"), mode="nt", name="ab_in_dgrad", after=comm.started())
    dx0, dg_pre_mix0 = _prenorm_bwd(dx1, dh0, x, vec(pre_mix, 0), "ab_prenorm_bwd")

    small = {
        "pre_mix": jnp.concatenate([dg_pre_mix0, dg_pre_mix1], axis=0),
        "post_mix": jnp.concatenate([dg_post_mix0, dg_post_mix1], axis=0),
        "pre_ffn": jnp.concatenate([dg_pre_ffn0, dg_pre_ffn1], axis=0),
        "post_ffn": jnp.concatenate([dg_post_ffn0, dg_post_ffn1], axis=0),
        "ln_g": d_ln_g, "ln_b": d_ln_b, "w_s": d_w_s, "b_s": d_b_t.T,
    }
    return loss, dx0, small


MESH_ID = pl.DeviceIdType.MESH
ANY = pl.BlockSpec(memory_space=pl.ANY)


def _coords():
    return lax.axis_index("x"), lax.axis_index("y"), lax.axis_index("c")


def _shard_of(ref, kind, p):
    if kind == "col":
        n = ref.shape[1] // N_DEV
        return ref.at[:, pl.ds(pl.multiple_of(p * n, 128), n)]
    r = ref.shape[0] // N_DEV
    return ref.at[pl.ds(pl.multiple_of(p * r, 16), r), :]


def _full_shape(shard, kind):
    if kind == "col":
        return (shard.shape[0], shard.shape[1] * N_DEV)
    return (shard.shape[0] * N_DEV, shard.shape[1])


def _place(shard, kind, block, name):
    rows, cols = shard.shape
    tr = _tile(rows, 512)

    def body(b_ref, s_ref, o_ref):
        o_ref[...] = s_ref[...].astype(BF16)

    if kind == "col":
        out = pl.BlockSpec((tr, cols), lambda i, b_ref: (i, b_ref[0]))
    else:
        out = pl.BlockSpec((tr, cols), lambda i, b_ref: (b_ref[0] * (rows // tr) + i, 0))
    return pl.pallas_call(
        body, name=name,
        grid_spec=pltpu.PrefetchScalarGridSpec(
            num_scalar_prefetch=1, grid=(rows // tr,),
            in_specs=[pl.BlockSpec((tr, cols), lambda i, b_ref: (i, 0))], out_specs=out),
        out_shape=jax.ShapeDtypeStruct(_full_shape(shard, kind), BF16),
        compiler_params=_params("parallel"),
    )(block, shard)


def _all_gather(fulls, kinds):
    nt = len(fulls)

    def body(*refs):
        ins = refs[:nt]
        send_sems, recv_sems = refs[2 * nt:]
        x, y, c = _coords()
        me, sibling = (x, y, c), (x, y, 1 - c)
        chips = [(1 - x, y), (x, 1 - y), (1 - x, 1 - y)]

        def copy(t, k, block, to):
            px, py, pc = block
            slot = _shard_of(ins[t], kinds[t], 4 * px + 2 * py + pc)
            return pltpu.make_async_remote_copy(
                src_ref=slot, dst_ref=slot, send_sem=send_sems.at[7 * t + k], recv_sem=recv_sems.at[7 * t + k],
                device_id=to, device_id_type=MESH_ID)

        sent = []
        for t in range(nt):
            first = [copy(t, 0, me, sibling)] + [copy(t, 1 + j, me, (*chip, c)) for j, chip in enumerate(chips)]
            for cp in first:
                cp.start()
            sent += first
        for t in range(nt):
            for j, chip in enumerate(chips):
                copy(t, 1 + j, (*chip, c), me).wait_recv()
                sent.append(copy(t, 4 + j, (*chip, c), sibling))
                sent[-1].start()
        for t in range(nt):
            copy(t, 0, sibling, me).wait_recv()
            for j, chip in enumerate(chips):
                copy(t, 4 + j, (*chip, 1 - c), me).wait_recv()
        for cp in sent:
            cp.wait_send()

    return pl.pallas_call(
        body, name="all_gather_weights",
        in_specs=[ANY] * nt, out_specs=[ANY] * nt,
        out_shape=[jax.ShapeDtypeStruct(f.shape, f.dtype) for f in fulls],
        input_output_aliases={t: t for t in range(nt)},
        scratch_shapes=[pltpu.SemaphoreType.DMA((7 * nt,)), pltpu.SemaphoreType.DMA((7 * nt,))],
    )(*fulls)


HBM = pl.BlockSpec(memory_space=pltpu.HBM)
SEM = pl.BlockSpec(memory_space=pltpu.SEMAPHORE)
FLOWS = pltpu.SideEffectType.DATAFLOW_SIDE_EFFECTING


def _in_hbm(a):
    return pltpu.with_memory_space_constraint(a, pltpu.HBM)


def _hbm_like(bufs):
    return [pltpu.HBM(b.shape, b.dtype) for b in bufs]


def _copies_start(name, bufs, plan, n, after):
    nb = len(bufs)

    def body(*refs):
        send_sems, recv_sems, token = refs[nb + 1], refs[nb + 2], refs[-1]
        for cp in plan(refs[:nb], send_sems, recv_sems):
            cp.start()
        token[...] = jnp.zeros_like(token)

    out = pl.pallas_call(
        body, name=name, in_specs=[HBM] * nb + [ANY],
        out_specs=[SEM, SEM] + [HBM] * nb + [pl.BlockSpec(memory_space=pltpu.VMEM)],
        out_shape=[pltpu.SemaphoreType.DMA((n,)), pltpu.SemaphoreType.DMA((n,))] + _hbm_like(bufs)
        + [jax.ShapeDtypeStruct((8, 128), F32)],
        input_output_aliases={i: 2 + i for i in range(nb)},
        compiler_params=pltpu.CompilerParams(has_side_effects=FLOWS),
    )(*[_in_hbm(b) for b in bufs], after)
    return (out[0], out[1]), list(out[2:2 + nb]), out[-1]


def _copies_wait(name, bufs, sems, after, plan):
    nb = len(bufs)

    def body(*refs):
        for cp in plan(refs[:nb], refs[nb], refs[nb + 1]):
            cp.wait_send()
            cp.wait_recv()

    out = pl.pallas_call(
        body, name=name, in_specs=[HBM] * nb + [SEM, SEM, ANY], out_specs=[HBM] * nb,
        out_shape=_hbm_like(bufs), input_output_aliases={i: i for i in range(nb)},
        compiler_params=pltpu.CompilerParams(has_side_effects=FLOWS),
    )(*bufs, *sems, after)
    return list(out)


def _copies_wait_start(name, bufs, sems, after, plan, next_plan, n_next):
    nb = len(bufs)

    def body(*refs):
        ins = refs[:nb]
        for cp in plan(ins, refs[nb], refs[nb + 1]):
            cp.wait_send()
            cp.wait_recv()
        send_sems, recv_sems, token = refs[nb + 3], refs[nb + 4], refs[-1]
        for cp in next_plan(ins, send_sems, recv_sems):
            cp.start()
        token[...] = jnp.zeros_like(token)

    out = pl.pallas_call(
        body, name=name, in_specs=[HBM] * nb + [SEM, SEM, ANY],
        out_specs=[SEM, SEM] + [HBM] * nb + [pl.BlockSpec(memory_space=pltpu.VMEM)],
        out_shape=[pltpu.SemaphoreType.DMA((n_next,)), pltpu.SemaphoreType.DMA((n_next,))] + _hbm_like(bufs)
        + [jax.ShapeDtypeStruct((8, 128), F32)],
        input_output_aliases={i: 2 + i for i in range(nb)},
        compiler_params=pltpu.CompilerParams(has_side_effects=FLOWS),
    )(*bufs, *sems, after)
    return (out[0], out[1]), list(out[2:2 + nb]), out[-1]


def _gather_plans(kinds):
    nt = len(kinds)

    def slot(refs, t, px, py, pc):
        return _shard_of(refs[t], kinds[t], 4 * px + 2 * py + pc)

    def to_chips(refs, send_sems, recv_sems):
        x, y, c = _coords()
        peers = [(x, y, 1 - c), (1 - x, y, c), (x, 1 - y, c), (1 - x, 1 - y, c)]
        return [pltpu.make_async_remote_copy(
            src_ref=slot(refs, t, x, y, c), dst_ref=slot(refs, t, x, y, c), send_sem=send_sems.at[4 * t + k],
            recv_sem=recv_sems.at[4 * t + k], device_id=peer, device_id_type=MESH_ID)
            for t in range(nt) for k, peer in enumerate(peers)]

    def to_sibling(refs, send_sems, recv_sems):
        x, y, c = _coords()
        chips = [(1 - x, y), (x, 1 - y), (1 - x, 1 - y)]
        return [pltpu.make_async_remote_copy(
            src_ref=slot(refs, t, *chip, c), dst_ref=slot(refs, t, *chip, c), send_sem=send_sems.at[3 * t + j],
            recv_sem=recv_sems.at[3 * t + j], device_id=(x, y, 1 - c), device_id_type=MESH_ID)
            for t in range(nt) for j, chip in enumerate(chips)]

    return to_chips, to_sibling


def _shard_shape(full, kind):
    if kind == "col":
        return (full.shape[0], full.shape[1] // N_DEV)
    return (full.shape[0] // N_DEV, full.shape[1])


def _quarter_of(ref, kind, q):
    if kind == "col":
        n = ref.shape[1] // 4
        return ref.at[:, pl.ds(pl.multiple_of(q * n, 128), n)]
    r = ref.shape[0] // 4
    return ref.at[pl.ds(pl.multiple_of(q * r, 16), r), :]


def _quarter_shape(full, kind):
    if kind == "col":
        return (full.shape[0], full.shape[1] // 4)
    return (full.shape[0] // 4, full.shape[1])


def _scatter_plan(kinds):
    nt = len(kinds)

    def plan(refs, send_sems, recv_sems):
        x, y, c = _coords()
        chips = [(1 - x, y), (x, 1 - y), (1 - x, 1 - y)]
        return [pltpu.make_async_remote_copy(
            src_ref=_quarter_of(refs[t], kinds[t], 2 * qx + qy), dst_ref=refs[nt + t].at[2 * x + y],
            send_sem=send_sems.at[3 * t + j], recv_sem=recv_sems.at[3 * t + j],
            device_id=(qx, qy, c), device_id_type=MESH_ID)
            for t in range(nt) for j, (qx, qy) in enumerate(chips)]
    return plan


def _partial_specs(full, kind, tr, half):
    rows, cols = _shard_shape(full, kind)
    steps = rows // tr
    if kind == "col":
        own = pl.BlockSpec((tr, cols), lambda i, w: (i, 2 * w[2] + w[half]))
        landed = [pl.BlockSpec((None, tr, cols), lambda i, w, k=k: (w[2 + k], i, w[half])) for k in (1, 2, 3)]
    else:
        own = pl.BlockSpec((tr, cols), lambda i, w: ((2 * w[2] + w[half]) * steps + i, 0))
        landed = [pl.BlockSpec((None, tr, cols), lambda i, w, k=k: (w[2 + k], w[half] * steps + i, 0))
                  for k in (1, 2, 3)]
    return [own] + landed


def _for_sibling(grad, land, kind, where, name):
    rows, cols = _shard_shape(grad, kind)
    tr = _tile(rows, 256)

    def body(w_ref, g_ref, a_ref, b_ref, c_ref, o_ref):
        o_ref[...] = ((g_ref[...].astype(F32) + a_ref[...].astype(F32)) + b_ref[...].astype(F32)) \
            + c_ref[...].astype(F32)

    return pl.pallas_call(
        body, name=name,
        grid_spec=pltpu.PrefetchScalarGridSpec(
            num_scalar_prefetch=1, grid=(rows // tr,), in_specs=_partial_specs(grad, kind, tr, 1),
            out_specs=pl.BlockSpec((tr, cols), lambda i, w: (i, 0))),
        out_shape=jax.ShapeDtypeStruct((rows, cols), F32),
        compiler_params=_params("parallel"),
    )(where, grad, land, land, land)


def _exchange_siblings(sums, name):
    nt = len(sums)

    def body(*refs):
        ins, outs = refs[:nt], refs[nt:2 * nt]
        send_sems, recv_sems = refs[2 * nt:]
        x, y, c = _coords()
        copies = [pltpu.make_async_remote_copy(
            src_ref=ins[t], dst_ref=outs[t], send_sem=send_sems.at[t], recv_sem=recv_sems.at[t],
            device_id=(x, y, 1 - c), device_id_type=MESH_ID) for t in range(nt)]
        for cp in copies:
            cp.start()
        for cp in copies:
            cp.wait()

    return pl.pallas_call(
        body, name=name, in_specs=[ANY] * nt, out_specs=[ANY] * nt,
        out_shape=[jax.ShapeDtypeStruct(s.shape, s.dtype) for s in sums],
        scratch_shapes=[pltpu.SemaphoreType.DMA((nt,)), pltpu.SemaphoreType.DMA((nt,))],
    )(*sums)


def _all_reduce_small(vec, after):
    R = vec.shape[0]

    def body(v_ref, after_ref, o_ref, recv_ref, send_sems, recv_sems):
        x, y, c = _coords()
        me = 4 * x + 2 * y + c
        recv_ref[me] = v_ref[...]
        copies = []
        for k in range(1, N_DEV):
            bx, by, bc = (k >> 2) & 1, (k >> 1) & 1, k & 1
            peer = (1 - x if bx else x, 1 - y if by else y, 1 - c if bc else c)
            copies.append(pltpu.make_async_remote_copy(
                src_ref=v_ref, dst_ref=recv_ref.at[me],
                send_sem=send_sems.at[k - 1], recv_sem=recv_sems.at[k - 1],
                device_id=peer, device_id_type=MESH_ID))
        for cp in copies:
            cp.start()
        for cp in copies:
            cp.wait()
        total = recv_ref[0]
        for p in range(1, N_DEV):
            total = total + recv_ref[p]
        o_ref[...] = total

    return pl.pallas_call(
        body, name="all_reduce_small",
        in_specs=[pl.BlockSpec(memory_space=pltpu.VMEM), ANY], out_specs=pl.BlockSpec(memory_space=pltpu.VMEM),
        out_shape=jax.ShapeDtypeStruct((R, 128), F32),
        scratch_shapes=[pltpu.VMEM((N_DEV, R, 128), F32), pltpu.SemaphoreType.DMA((N_DEV - 1,)),
                        pltpu.SemaphoreType.DMA((N_DEV - 1,))],
        compiler_params=pltpu.CompilerParams(vmem_limit_bytes=VMEM_LIMIT),
    )(vec, after)


def _adamw_math(w, g, m, v):
    m = ADAM_B1 * m + (1.0 - ADAM_B1) * g
    v = ADAM_B2 * v + (1.0 - ADAM_B2) * (g * g)
    m_hat = m / (1.0 - ADAM_B1 ** ADAM_STEP)
    v_hat = v / (1.0 - ADAM_B2 ** ADAM_STEP)
    delta = -ADAM_LR * (m_hat / (jnp.sqrt(v_hat) + ADAM_EPS) + ADAM_WD * w)
    return delta, m, v


def _adamw(w, grads, kind, where, m, v, name):
    layers, rows, cols = w.shape
    tr = _tile(rows, 128)
    out = None
    for layer, (grad, land, sibling) in enumerate(grads):
        def body(w_ref, p0_ref, p1_ref, p2_ref, p3_ref, s_ref, x_ref, m_ref, v_ref, *rest):
            g_ref, d_ref, mo_ref, vo_ref = rest[-4:]
            g = ((p0_ref[...].astype(F32) + p1_ref[...].astype(F32)) + p2_ref[...].astype(F32)) \
                + p3_ref[...].astype(F32) + s_ref[...]
            g_ref[...] = g
            d_ref[...], mo_ref[...], vo_ref[...] = _adamw_math(x_ref[...], g, m_ref[...], v_ref[...])

        blk = pl.BlockSpec((None, tr, cols), lambda i, w_, layer=layer: (layer, i, 0))
        earlier = [] if out is None else list(out)
        out = pl.pallas_call(
            body, name=f"{name}_{layer}",
            grid_spec=pltpu.PrefetchScalarGridSpec(
                num_scalar_prefetch=1, grid=(rows // tr,),
                in_specs=_partial_specs(grad, kind, tr, 0) + [pl.BlockSpec((tr, cols), lambda i, w_: (i, 0))]
                + [blk] * 3 + [ANY] * len(earlier),
                out_specs=[blk] * 4),
            out_shape=[jax.ShapeDtypeStruct((layers, rows, cols), F32)] * 4,
            input_output_aliases={9 + k: k for k in range(len(earlier))},
            compiler_params=_params("parallel"),
        )(where, grad, land, land, land, sibling, w, m, v, *earlier)
    return out


def _adamw_small(w, g, m, v):
    def body(w_ref, g_ref, m_ref, v_ref, d_ref, mo_ref, vo_ref):
        d_ref[...], mo_ref[...], vo_ref[...] = _adamw_math(w_ref[...], g_ref[...], m_ref[...], v_ref[...])

    whole = pl.BlockSpec(memory_space=pltpu.VMEM)
    return pl.pallas_call(
        body, name="adamw_small", in_specs=[whole] * 4, out_specs=[whole] * 3,
        out_shape=[jax.ShapeDtypeStruct(w.shape, F32)] * 3,
        compiler_params=pltpu.CompilerParams(vmem_limit_bytes=VMEM_LIMIT),
    )(w, g, m, v)


def _pack(arrays):
    rows = []
    for a in arrays:
        flat = a.reshape(-1)
        pad = (-flat.shape[0]) % 1024
        rows.append(jnp.pad(flat, (0, pad)).reshape(-1, 128))
    return jnp.concatenate(rows, axis=0)


def _unpack(packed, like):
    out, r = [], 0
    for a in like:
        n = math.prod(a.shape)
        nr = (n + 1023) // 1024 * 8
        out.append(packed[r:r + nr].reshape(-1)[:n].reshape(a.shape))
        r += nr
    return out


KIND = {"ab_in": "col", "ab_out": "row", "sb_in": "col", "sb_out": "row",
        "w1_0": "col", "w1_1": "col", "w2_0": "row", "w2_1": "row"}
GATHER_FIRST = ("ab_in", "ab_out")
GATHER_LATER = {"ffn0": ("w1_0", "w2_0"),
                "rest": ("sb_in", "sb_out", "w1_1", "w2_1")}


class _Exchange:
    def __init__(self, shards):
        x, y, c = _coords()
        q = 2 * x + y
        self.where = jnp.stack([c, 1 - c, q, (q + 1) % 4, (q + 2) % 4, (q + 3) % 4]).astype(jnp.int32)
        block = (4 * x + 2 * y + c).astype(jnp.int32).reshape(1)
        self.full = {n: _place(s, KIND[n], block, f"place_{n}") for n, s in shards.items()}
        first = _all_gather([self.full[n] for n in GATHER_FIRST], [KIND[n] for n in GATHER_FIRST])
        self.full.update(zip(GATHER_FIRST, first))
        self.tokens = []
        self.gathers = {}
        self.scatters = {}
        self.settled = {}
        after = first[-1]
        for key, group in GATHER_LATER.items():
            to_chips, to_sibling = _gather_plans([KIND[n] for n in group])
            bufs = [self.full[n] for n in group]
            sems, bufs, after = _copies_start(f"gather_start_{key}", bufs, to_chips, 4 * len(group), after)
            self.tokens.append(after)
            self.gathers[key] = (group, sems, bufs, to_chips, to_sibling)

    def tie(self, small):
        for token in self.tokens:
            small = small + token[0:1, 0:1]
        self.tokens = []
        return small

    def started(self):
        return tuple(self.tokens)

    def weight(self, name):
        return self.full[name]

    def arrive(self, key, after):
        group, sems, bufs, to_chips, to_sibling = self.gathers[key]
        sems, bufs, token = _copies_wait_start(f"gather_pass_{key}", bufs, sems, after, to_chips, to_sibling,
                                               3 * len(group))
        self.tokens.append(token)
        self.gathers[key] = (group, sems, bufs, to_chips, to_sibling)

    def land(self, key, after):
        group, sems, bufs, _, to_sibling = self.gathers.pop(key)
        self.full.update(zip(group, _copies_wait(f"gather_done_{key}", bufs, sems, after, to_sibling)))

    def reduce(self, key, grads):
        names = list(grads)
        kinds = [KIND[n] for n in names]
        full = [grads[n] for n in names]
        lands = [lax.empty((4,) + _quarter_shape(g, k), BF16) for g, k in zip(full, kinds)]
        plan = _scatter_plan(kinds)
        sems, bufs, token = _copies_start(f"scatter_start_{key}", full + lands, plan, 3 * len(names), full[-1])
        self.tokens.append(token)
        self.scatters[key] = (names, sems, bufs, plan)

    def settle(self, key, after):
        names, sems, bufs, plan = self.scatters.pop(key)
        bufs = _copies_wait(f"scatter_done_{key}", bufs, sems, after, plan)
        grads, lands = bufs[:len(names)], bufs[len(names):]
        sums = [_for_sibling(g, l, KIND[n], self.where, f"for_sibling_{n}") for n, g, l in zip(names, grads, lands)]
        got = _exchange_siblings(sums, f"grad_siblings_{key}")
        self.settled.update({n: t for n, *t in zip(names, grads, lands, got)})

    def finish(self, after):
        for key in list(self.scatters):
            self.settle(key, after)
        return self.settled


SMALL = ("norm_pre_mix", "norm_post_mix", "norm_pre_ffn", "norm_post_ffn", "sgu_ln_g", "sgu_ln_b", "sgu_w", "sgu_b")
ORDER = ("norm_pre_mix", "norm_post_mix", "norm_pre_ffn", "norm_post_ffn", "ab_w_in", "sgu_ln_g", "sgu_ln_b", "sgu_w",
         "sgu_b", "ab_w_out", "sb_w_in", "sb_w_out", "ffn_w1", "ffn_w2")


def kernel(x, norm_pre_mix, norm_post_mix, norm_pre_ffn, norm_post_ffn, ab_w_in, sgu_ln_g, sgu_ln_b, sgu_w, sgu_b, ab_w_out, sb_w_in, sb_w_out, ffn_w1, ffn_w2, loss_target, m_norm_pre_mix, m_norm_post_mix, m_norm_pre_ffn, m_norm_post_ffn, m_ab_w_in, m_sgu_ln_g, m_sgu_ln_b, m_sgu_w, m_sgu_b, m_ab_w_out, m_sb_w_in, m_sb_w_out, m_ffn_w1, m_ffn_w2, v_norm_pre_mix, v_norm_post_mix, v_norm_pre_ffn, v_norm_post_ffn, v_ab_w_in, v_sgu_ln_g, v_sgu_ln_b, v_sgu_w, v_sgu_b, v_ab_w_out, v_sb_w_in, v_sb_w_out, v_ffn_w1, v_ffn_w2):
    W = dict(norm_pre_mix=norm_pre_mix, norm_post_mix=norm_post_mix, norm_pre_ffn=norm_pre_ffn,
             norm_post_ffn=norm_post_ffn, ab_w_in=ab_w_in, sgu_ln_g=sgu_ln_g, sgu_ln_b=sgu_ln_b, sgu_w=sgu_w,
             sgu_b=sgu_b, ab_w_out=ab_w_out, sb_w_in=sb_w_in, sb_w_out=sb_w_out, ffn_w1=ffn_w1, ffn_w2=ffn_w2)
    M = dict(norm_pre_mix=m_norm_pre_mix, norm_post_mix=m_norm_post_mix, norm_pre_ffn=m_norm_pre_ffn,
             norm_post_ffn=m_norm_post_ffn, ab_w_in=m_ab_w_in, sgu_ln_g=m_sgu_ln_g, sgu_ln_b=m_sgu_ln_b,
             sgu_w=m_sgu_w, sgu_b=m_sgu_b, ab_w_out=m_ab_w_out, sb_w_in=m_sb_w_in, sb_w_out=m_sb_w_out,
             ffn_w1=m_ffn_w1, ffn_w2=m_ffn_w2)
    V = dict(norm_pre_mix=v_norm_pre_mix, norm_post_mix=v_norm_post_mix, norm_pre_ffn=v_norm_pre_ffn,
             norm_post_ffn=v_norm_post_ffn, ab_w_in=v_ab_w_in, sgu_ln_g=v_sgu_ln_g, sgu_ln_b=v_sgu_ln_b,
             sgu_w=v_sgu_w, sgu_b=v_sgu_b, ab_w_out=v_ab_w_out, sb_w_in=v_sb_w_in, sb_w_out=v_sb_w_out,
             ffn_w1=v_ffn_w1, ffn_w2=v_ffn_w2)

    shards = {"ab_in": ab_w_in[0], "ab_out": ab_w_out[0], "w1_0": ffn_w1[0], "w2_0": ffn_w2[0],
              "sb_in": sb_w_in[0], "sb_out": sb_w_out[0], "w1_1": ffn_w1[1], "w2_1": ffn_w2[1]}
    comm = _Exchange(shards)
    norms = (norm_pre_mix, norm_post_mix, norm_pre_ffn, norm_post_ffn)
    sgu = (sgu_ln_g, sgu_ln_b, sgu_w[0], sgu_b[0])
    loss, dx, small = _local_step(x[0], loss_target[0], norms, sgu, comm)
    loss = lax.psum(loss, MESH_AXES)

    out = {}

    def update(name, layers, landed):
        out[name] = _adamw(W[name], [landed[n] for n in layers], KIND[layers[0]], comm.where, M[name], V[name],
                           f"adamw_{name}")

    comm.settle("ffn0", after=dx)
    for name, layers in (("sb_w_in", ["sb_in"]), ("sb_w_out", ["sb_out"]), ("ffn_w1", ["w1_0", "w1_1"]),
                         ("ffn_w2", ["w2_0", "w2_1"])):
        update(name, layers, comm.settled)
    small_g = [small["pre_mix"], small["post_mix"], small["pre_ffn"], small["post_ffn"], small["ln_g"],
               small["ln_b"], small["w_s"][None], small["b_s"][None]]
    g_small = _all_reduce_small(_pack(small_g), out["ffn_w2"][3])
    landed = comm.finish(after=g_small)
    update("ab_w_out", ["ab_out"], landed)
    update("ab_w_in", ["ab_in"], landed)
    res = _adamw_small(_pack([W[n] for n in SMALL]), g_small, _pack([M[n] for n in SMALL]),
                       _pack([V[n] for n in SMALL]))
    like = [W[n] for n in SMALL]
    for n, *vals in zip(SMALL, *[_unpack(r, like) for r in [g_small] + list(res)]):
        out[n] = vals

    return (loss, dx[None], *[out[n][0] for n in ORDER], *[out[n][1] for n in ORDER],
            *[out[n][2] for n in ORDER], *[out[n][3] for n in ORDER])
```

```python
import functools
import math

import jax
import jax.numpy as jnp
from jax import lax
from jax.experimental import pallas as pl
from jax.experimental.pallas import tpu as pltpu

F32 = jnp.float32
BF16 = jnp.bfloat16

HEAD_DIM = 128
CHUNK = 128
ATT_BLOCK = 128
DILATED_PAIRS = ((128, 1), (512, 4), (2048, 16))
RMS_EPS = 1e-6
LN_EPS = 1e-5
ADAM_LR = 0.001
ADAM_B1 = 0.9
ADAM_B2 = 0.999
ADAM_EPS = 1e-08
ADAM_WD = 0.01
ADAM_STEP = 10
N_DEV = 8
MESH_AXES = ("x", "y", "c")
MASKED = -1e30

V7X_VMEM_BYTES = 64 * 1024 * 1024
VMEM_LIMIT = V7X_VMEM_BYTES - 8 * 1024 * 1024

NN = (((1,), (0,)), ((), ()))
NT = (((1,), (1,)), ((), ()))
TN = (((0,), (0,)), ((), ()))


def _params(*sem):
    return pltpu.CompilerParams(dimension_semantics=sem, vmem_limit_bytes=VMEM_LIMIT)


def _dot(a, b, dims=NN):
    return lax.dot_general(a, b, dims, preferred_element_type=F32)


def _tile(n, preferred):
    if n <= preferred:
        return n
    t = preferred - preferred % 128
    while n % t:
        t -= 128
    assert t > 0, (n, preferred)
    return t


def _matmul(a, b, *, mode, name, out_dtype=F32, tm=1024, tn=1024, tk=2048, epi=None, extras=(), after=()):
    if mode == "nn":
        (M, K), N = a.shape, b.shape[1]
    elif mode == "nt":
        (M, K), N = a.shape, b.shape[0]
    else:
        (K, M), N = a.shape, b.shape[1]
    tm, tn, tk = _tile(M, tm), _tile(N, tn), _tile(K, tk)
    nk = K // tk
    if mode == "tn":
        a_spec = pl.BlockSpec((tk, tm), lambda i, j, k: (k, i))
    else:
        a_spec = pl.BlockSpec((tm, tk), lambda i, j, k: (i, k))
    if mode == "nt":
        b_spec = pl.BlockSpec((tn, tk), lambda i, j, k: (j, k))
    else:
        b_spec = pl.BlockSpec((tk, tn), lambda i, j, k: (k, j))
    o_spec = pl.BlockSpec((tm, tn), lambda i, j, k: (i, j))
    dims = {"nn": NN, "nt": NT, "tn": TN}[mode]
    n_extra = len(extras)
    n_in = n_extra + len(after)

    def finish(acc, refs):
        j = pl.program_id(1)
        if epi is None:
            return acc
        return epi(acc, j, *[r[...] for r in refs])

    if nk == 1:
        def body(a_ref, b_ref, *rest):
            o_ref = rest[n_in]
            acc = _dot(a_ref[...], b_ref[...], dims)
            o_ref[...] = finish(acc, rest[:n_extra]).astype(o_ref.dtype)
        scratch = []
    else:
        def body(a_ref, b_ref, *rest):
            o_ref, acc_ref = rest[n_in], rest[n_in + 1]
            k = pl.program_id(2)

            @pl.when(k == 0)
            def _():
                acc_ref[...] = jnp.zeros_like(acc_ref)

            acc_ref[...] += _dot(a_ref[...], b_ref[...], dims)

            @pl.when(k == nk - 1)
            def _():
                o_ref[...] = finish(acc_ref[...], rest[:n_extra]).astype(o_ref.dtype)
        scratch = [pltpu.VMEM((tm, tn), F32)]

    return pl.pallas_call(
        body,
        name=name,
        grid=(M // tm, N // tn, nk),
        in_specs=[a_spec, b_spec] + [o_spec] * n_extra + [ANY] * len(after),
        out_specs=o_spec,
        out_shape=jax.ShapeDtypeStruct((M, N), out_dtype),
        scratch_shapes=scratch,
        compiler_params=_params("parallel", "parallel", "arbitrary"),
    )(a, b, *extras, *after)


ROWS = 256


def _rms(x):
    return lax.rsqrt(jnp.mean(x * x, axis=-1, keepdims=True) + RMS_EPS)


def _prenorm(x, g, name):
    T, D = x.shape

    def body(x_ref, g_ref, h_ref):
        xv = x_ref[...]
        h_ref[...] = (xv * _rms(xv) * g_ref[...]).astype(BF16)

    row = pl.BlockSpec((ROWS, D), lambda i: (i, 0))
    vec = pl.BlockSpec((1, D), lambda i: (0, 0))
    return pl.pallas_call(
        body, name=name, grid=(T // ROWS,), in_specs=[row, vec], out_specs=row,
        out_shape=jax.ShapeDtypeStruct((T, D), BF16), compiler_params=_params("parallel"),
    )(x, g)


def _postnorm_prenorm(x, y, g_post, g_pre, name):
    T, D = x.shape

    def body(x_ref, y_ref, gp_ref, gn_ref, xo_ref, h_ref):
        yv = y_ref[...]
        xn = x_ref[...] + yv * _rms(yv) * gp_ref[...]
        xo_ref[...] = xn
        h_ref[...] = (xn * _rms(xn) * gn_ref[...]).astype(BF16)

    row = pl.BlockSpec((ROWS, D), lambda i: (i, 0))
    vec = pl.BlockSpec((1, D), lambda i: (0, 0))
    return pl.pallas_call(
        body, name=name, grid=(T // ROWS,), in_specs=[row, row, vec, vec], out_specs=[row, row],
        out_shape=[jax.ShapeDtypeStruct((T, D), F32), jax.ShapeDtypeStruct((T, D), BF16)],
        compiler_params=_params("parallel"),
    )(x, y, g_post, g_pre)


def _postnorm_loss(x, y, g_post, target, name):
    T, D = x.shape

    def body(x_ref, y_ref, gp_ref, t_ref, loss_ref, dx_ref):
        @pl.when(pl.program_id(0) == 0)
        def _():
            loss_ref[...] = jnp.zeros_like(loss_ref)

        yv = y_ref[...]
        err = x_ref[...] + yv * _rms(yv) * gp_ref[...] - t_ref[...]
        dx_ref[...] = err * (1.0 / D)
        loss_ref[...] += 0.5 * jnp.sum(jnp.sum(err * err, axis=-1, keepdims=True) * (1.0 / D))

    row = pl.BlockSpec((ROWS, D), lambda i: (i, 0))
    vec = pl.BlockSpec((1, D), lambda i: (0, 0))
    acc = pl.BlockSpec((8, 128), lambda i: (0, 0))
    return pl.pallas_call(
        body, name=name, grid=(T // ROWS,), in_specs=[row, row, vec, row], out_specs=[acc, row],
        out_shape=[jax.ShapeDtypeStruct((8, 128), F32), jax.ShapeDtypeStruct((T, D), F32)],
        compiler_params=_params("arbitrary"),
    )(x, y, g_post, target)


def _postnorm_bwd(dx, y, g_post, name):
    T, D = dx.shape

    def body(dx_ref, y_ref, g_ref, dy_ref, dg_ref):
        @pl.when(pl.program_id(0) == 0)
        def _():
            dg_ref[...] = jnp.zeros_like(dg_ref)

        yv, dn = y_ref[...], dx_ref[...]
        r = _rms(yv)
        yh = yv * r
        gd = dn * g_ref[...]
        dy_ref[...] = (r * (gd - yh * jnp.mean(yh * gd, axis=-1, keepdims=True))).astype(BF16)
        dg_ref[...] += jnp.sum(dn * yh, axis=0, keepdims=True)

    row = pl.BlockSpec((ROWS, D), lambda i: (i, 0))
    vec = pl.BlockSpec((1, D), lambda i: (0, 0))
    return pl.pallas_call(
        body, name=name, grid=(T // ROWS,), in_specs=[row, row, vec], out_specs=[row, vec],
        out_shape=[jax.ShapeDtypeStruct((T, D), BF16), jax.ShapeDtypeStruct((1, D), F32)],
        compiler_params=_params("arbitrary"),
    )(dx, y, g_post)


def _prenorm_bwd(dx_out, dh, x, g_pre, name):
    T, D = x.shape

    def body(dxo_ref, dh_ref, x_ref, g_ref, dx_ref, dg_ref):
        @pl.when(pl.program_id(0) == 0)
        def _():
            dg_ref[...] = jnp.zeros_like(dg_ref)

        xv, dhv = x_ref[...], dh_ref[...]
        r = _rms(xv)
        xh = xv * r
        gd = dhv * g_ref[...]
        dx_ref[...] = dxo_ref[...] + r * (gd - xh * jnp.mean(xh * gd, axis=-1, keepdims=True))
        dg_ref[...] += jnp.sum(dhv * xh, axis=0, keepdims=True)

    row = pl.BlockSpec((ROWS, D), lambda i: (i, 0))
    vec = pl.BlockSpec((1, D), lambda i: (0, 0))
    return pl.pallas_call(
        body, name=name, grid=(T // ROWS,), in_specs=[row, row, row, vec], out_specs=[row, vec],
        out_shape=[jax.ShapeDtypeStruct((T, D), F32), jax.ShapeDtypeStruct((1, D), F32)],
        compiler_params=_params("arbitrary"),
    )(dx_out, dh, x, g_pre)


_INV_SQRT2 = 1.0 / math.sqrt(2.0)
_INV_SQRT2PI = 1.0 / math.sqrt(2.0 * math.pi)


def _gelu(x):
    return 0.5 * x * (1.0 + lax.erf(x * _INV_SQRT2))


def _gelu_grad(x):
    return 0.5 * (1.0 + lax.erf(x * _INV_SQRT2)) + x * jnp.exp(-0.5 * x * x) * _INV_SQRT2PI


def _layernorm_stats(x):
    mu = jnp.mean(x, axis=-1, keepdims=True)
    xc = x - mu
    rstd = lax.rsqrt(jnp.mean(xc * xc, axis=-1, keepdims=True) + LN_EPS)
    return xc * rstd, rstd


def _tril_mask():
    i = lax.broadcasted_iota(jnp.int32, (CHUNK, CHUNK), 0)
    j = lax.broadcasted_iota(jnp.int32, (CHUNK, CHUNK), 1)
    return j <= i


SGU_ROWS = 512


def _sgu_fwd(z, ln_g, ln_b, w_s, b_t, name):
    T = z.shape[0]
    A = ln_g.shape[1]
    G = A // 128
    rows = min(SGU_ROWS, T)

    def body(u_ref, v_ref, g_ref, b_ref, w_ref, bt_ref, o_ref):
        mask = _tril_mask()
        for c in range(rows // CHUNK):
            rs = pl.ds(c * CHUNK, CHUNK)
            xh, _ = _layernorm_stats(_gelu(v_ref[rs, :]))
            vn = (xh * g_ref[...] + b_ref[...]).astype(BF16)
            for g in range(G):
                cs = pl.ds(g * 128, 128)
                w = jnp.where(mask, w_ref[g], 0.0).astype(BF16)
                mixed = _dot(w, vn[:, g * 128:(g + 1) * 128]) + bt_ref[:, g:g + 1]
                o_ref[rs, cs] = (_gelu(u_ref[rs, cs]) * mixed).astype(BF16)

    return pl.pallas_call(
        body, name=name, grid=(T // rows,),
        in_specs=[
            pl.BlockSpec((rows, A), lambda i: (i, 0)),
            pl.BlockSpec((rows, A), lambda i: (i, 1)),
            pl.BlockSpec((1, A), lambda i: (0, 0)),
            pl.BlockSpec((1, A), lambda i: (0, 0)),
            pl.BlockSpec((G, CHUNK, CHUNK), lambda i: (0, 0, 0)),
            pl.BlockSpec((CHUNK, G), lambda i: (0, 0)),
        ],
        out_specs=pl.BlockSpec((rows, A), lambda i: (i, 0)),
        out_shape=jax.ShapeDtypeStruct((T, A), BF16),
        compiler_params=_params("parallel"),
    )(z, z, ln_g, ln_b, w_s, b_t)


def _sgu_bwd(z, dcat, ln_g, ln_b, w_s, b_t, name):
    T = z.shape[0]
    A = ln_g.shape[1]
    G = A // 128
    rows = min(SGU_ROWS, T)

    def body(u_ref, v_ref, da_ref, g_ref, b_ref, w_ref, bt_ref, dz_ref, dg_ref, db_ref, dw_ref, dbt_ref, dvn_ref):
        @pl.when(pl.program_id(0) == 0)
        def _():
            dg_ref[...] = jnp.zeros_like(dg_ref)
            db_ref[...] = jnp.zeros_like(db_ref)
            dw_ref[...] = jnp.zeros_like(dw_ref)
            dbt_ref[...] = jnp.zeros_like(dbt_ref)

        mask = _tril_mask()
        for c in range(rows // CHUNK):
            rs = pl.ds(c * CHUNK, CHUNK)
            vv = v_ref[rs, :]
            gv = _gelu(vv)
            xh, rstd = _layernorm_stats(gv)
            vn = (xh * g_ref[...] + b_ref[...]).astype(BF16)
            for g in range(G):
                cs = pl.ds(g * 128, 128)
                w = jnp.where(mask, w_ref[g], 0.0).astype(BF16)
                vg = vn[:, g * 128:(g + 1) * 128]
                mixed = _dot(w, vg) + bt_ref[:, g:g + 1]
                uu = u_ref[rs, cs]
                da = da_ref[rs, cs]
                dz_ref[rs, cs] = (da * mixed * _gelu_grad(uu)).astype(BF16)
                dm = da * _gelu(uu)
                dmb = dm.astype(BF16)
                dbt_ref[:, g:g + 1] += jnp.sum(dm, axis=1, keepdims=True)
                dw_ref[g] += jnp.where(mask, _dot(dmb, vg, NT), 0.0)
                dvn_ref[:, cs] = _dot(w, dmb, TN)
            dvn = dvn_ref[...]
            dg_ref[...] += jnp.sum(dvn * xh, axis=0, keepdims=True)
            db_ref[...] += jnp.sum(dvn, axis=0, keepdims=True)
            dxh = dvn * g_ref[...]
            dgv = rstd * (dxh - jnp.mean(dxh, axis=-1, keepdims=True)
                          - xh * jnp.mean(dxh * xh, axis=-1, keepdims=True))
            dz_ref[rs, pl.ds(A, A)] = (dgv * _gelu_grad(vv)).astype(BF16)

    vec = pl.BlockSpec((1, A), lambda i: (0, 0))
    wsp = pl.BlockSpec((G, CHUNK, CHUNK), lambda i: (0, 0, 0))
    bsp = pl.BlockSpec((CHUNK, G), lambda i: (0, 0))
    return pl.pallas_call(
        body, name=name, grid=(T // rows,),
        in_specs=[
            pl.BlockSpec((rows, A), lambda i: (i, 0)),
            pl.BlockSpec((rows, A), lambda i: (i, 1)),
            pl.BlockSpec((rows, A), lambda i: (i, 0)),
            vec, vec, wsp, bsp,
        ],
        out_specs=[pl.BlockSpec((rows, 2 * A), lambda i: (i, 0)), vec, vec, wsp, bsp],
        out_shape=[
            jax.ShapeDtypeStruct((T, 2 * A), BF16),
            jax.ShapeDtypeStruct((1, A), F32),
            jax.ShapeDtypeStruct((1, A), F32),
            jax.ShapeDtypeStruct((G, CHUNK, CHUNK), F32),
            jax.ShapeDtypeStruct((CHUNK, G), F32),
        ],
        scratch_shapes=[pltpu.VMEM((CHUNK, A), F32)],
        compiler_params=_params("arbitrary"),
    )(z, z, dcat, ln_g, ln_b, w_s, b_t)


def _alibi_row(B, d):
    H = B // HEAD_DIM
    slopes = [d * 2.0 ** (-8.0 * (h + 1.0) / H) for h in range(H)]
    return jnp.repeat(jnp.asarray(slopes, F32), HEAD_DIM)[None, :]


def _dil_scores(q, k, slope_d, valid, dist):
    s = _dot(q, k, NT) - slope_d * dist
    return jnp.where(valid, s, MASKED)


def _dil_layout(T, B, d):
    H = B // HEAD_DIM
    hp = H if d == 1 else 1
    return hp, H // hp, T // (d * ATT_BLOCK)


def _dil_rows(ref, r, d, cs):
    return ref[pl.ds(r, ATT_BLOCK, stride=d), cs] if d > 1 else ref[:, cs]


def _dil_put(ref, r, d, cs, val):
    if d > 1:
        ref[pl.ds(r, ATT_BLOCK, stride=d), cs] = val
    else:
        ref[:, cs] = val


def _dilated_fwd(z, d, B, name):
    T = z.shape[0]
    H = B // HEAD_DIM
    hp, groups, nb = _dil_layout(T, B, d)
    cw = hp * HEAD_DIM
    scale = HEAD_DIM ** -0.5
    blk = ATT_BLOCK

    def body(q_ref, kp_ref, kc_ref, vp_ref, vc_ref, sl_ref, o_ref, l_ref):
        n = pl.program_id(1)
        qi = lax.broadcasted_iota(jnp.int32, (blk, 2 * blk), 0)
        kj = lax.broadcasted_iota(jnp.int32, (blk, 2 * blk), 1)
        dist = qi + blk - kj
        valid = (dist >= 0) & (dist <= blk) & ((kj >= blk) | (n > 0))
        distf = dist.astype(F32)
        for r in range(d):
            for hh in range(hp):
                cs = pl.ds(hh * HEAD_DIM, HEAD_DIM)
                q = (_dil_rows(q_ref, r, d, cs) * scale).astype(BF16)
                k = jnp.concatenate([_dil_rows(kp_ref, r, d, cs), _dil_rows(kc_ref, r, d, cs)], axis=0).astype(BF16)
                v = jnp.concatenate([_dil_rows(vp_ref, r, d, cs), _dil_rows(vc_ref, r, d, cs)], axis=0).astype(BF16)
                s = _dil_scores(q, k, sl_ref[:, cs][:, :1], valid, distf)
                m = jnp.max(s, axis=-1, keepdims=True)
                p = jnp.exp(s - m)
                den = jnp.sum(p, axis=-1, keepdims=True)
                _dil_put(o_ref, r, d, cs, _dot(p.astype(BF16), v) / den)
                _dil_put(l_ref, r, d, cs, jnp.broadcast_to(m + jnp.log(den), (blk, HEAD_DIM)))

    def col(unit):
        return lambda g, n: (n, unit * groups + g)

    def col_prev(unit):
        return lambda g, n: (jnp.maximum(n - 1, 0), unit * groups + g)

    bs = (d * blk, cw)
    out = pl.BlockSpec(bs, lambda g, n: (n, g))
    return pl.pallas_call(
        body, name=name, grid=(groups, nb),
        in_specs=[pl.BlockSpec(bs, col(2)), pl.BlockSpec(bs, col_prev(3)), pl.BlockSpec(bs, col(3)),
                  pl.BlockSpec(bs, col_prev(4)), pl.BlockSpec(bs, col(4)), pl.BlockSpec((1, cw), lambda g, n: (0, g))],
        out_specs=[out, out],
        out_shape=[jax.ShapeDtypeStruct((T, B), F32)] * 2,
        compiler_params=_params("parallel", "parallel"),
    )(z, z, z, z, z, _alibi_row(B, d))


def _dilated_merge(os_, ls_, name):
    T, B = os_[0].shape

    def body(o1, o2, o3, l1, l2, l3, ob_ref, of_ref, lt_ref):
        a, b, c = l1[...], l2[...], l3[...]
        m = jnp.maximum(jnp.maximum(a, b), c)
        ea, eb, ec = jnp.exp(a - m), jnp.exp(b - m), jnp.exp(c - m)
        tot = ea + eb + ec
        o = (ea * o1[...] + eb * o2[...] + ec * o3[...]) / tot
        of_ref[...] = o
        ob_ref[...] = o.astype(BF16)
        lt_ref[...] = m + jnp.log(tot)

    row = pl.BlockSpec((ROWS, B), lambda i: (i, 0))
    return pl.pallas_call(
        body, name=name, grid=(T // ROWS,), in_specs=[row] * 6, out_specs=[row] * 3,
        out_shape=[jax.ShapeDtypeStruct((T, B), BF16), jax.ShapeDtypeStruct((T, B), F32),
                   jax.ShapeDtypeStruct((T, B), F32)],
        compiler_params=_params("parallel"),
    )(*os_, *ls_)


def _dilated_delta(dcat, o, name):
    T, B = o.shape
    H = B // HEAD_DIM

    def body(do_ref, o_ref, d_ref):
        for h in range(H):
            cs = pl.ds(h * HEAD_DIM, HEAD_DIM)
            dsum = jnp.sum(do_ref[:, cs] * o_ref[:, cs], axis=-1, keepdims=True)
            d_ref[:, cs] = jnp.broadcast_to(dsum, (ROWS, HEAD_DIM))

    row = pl.BlockSpec((ROWS, B), lambda i: (i, 0))
    return pl.pallas_call(
        body, name=name, grid=(T // ROWS,),
        in_specs=[pl.BlockSpec((ROWS, B), lambda i: (i, 1)), row], out_specs=row,
        out_shape=jax.ShapeDtypeStruct((T, B), F32), compiler_params=_params("parallel"),
    )(dcat, o)


def _dilated_dq(z, dcat, lse, delta, d, B, name):
    T = z.shape[0]
    hp, groups, nb = _dil_layout(T, B, d)
    cw = hp * HEAD_DIM
    scale = HEAD_DIM ** -0.5
    blk = ATT_BLOCK

    def body(q_ref, kp_ref, kc_ref, vp_ref, vc_ref, do_ref, l_ref, dl_ref, sl_ref, dq_ref):
        n = pl.program_id(1)
        qi = lax.broadcasted_iota(jnp.int32, (blk, 2 * blk), 0)
        kj = lax.broadcasted_iota(jnp.int32, (blk, 2 * blk), 1)
        dist = qi + blk - kj
        valid = (dist >= 0) & (dist <= blk) & ((kj >= blk) | (n > 0))
        distf = dist.astype(F32)
        for r in range(d):
            for hh in range(hp):
                cs = pl.ds(hh * HEAD_DIM, HEAD_DIM)
                q = (_dil_rows(q_ref, r, d, cs) * scale).astype(BF16)
                k = jnp.concatenate([_dil_rows(kp_ref, r, d, cs), _dil_rows(kc_ref, r, d, cs)], axis=0).astype(BF16)
                v = jnp.concatenate([_dil_rows(vp_ref, r, d, cs), _dil_rows(vc_ref, r, d, cs)], axis=0).astype(BF16)
                s = _dil_scores(q, k, sl_ref[:, cs][:, :1], valid, distf)
                p = jnp.exp(s - _dil_rows(l_ref, r, d, cs)[:, :1])
                dp = _dot(_dil_rows(do_ref, r, d, cs).astype(BF16), v, NT)
                ds = p * (dp - _dil_rows(dl_ref, r, d, cs)[:, :1])
                _dil_put(dq_ref, r, d, cs, _dot(ds.astype(BF16), k))

    def col(unit):
        return lambda g, n: (n, unit * groups + g)

    def col_prev(unit):
        return lambda g, n: (jnp.maximum(n - 1, 0), unit * groups + g)

    bs = (d * blk, cw)
    out = pl.BlockSpec(bs, lambda g, n: (n, g))
    return pl.pallas_call(
        body, name=name, grid=(groups, nb),
        in_specs=[pl.BlockSpec(bs, col(2)), pl.BlockSpec(bs, col_prev(3)), pl.BlockSpec(bs, col(3)),
                  pl.BlockSpec(bs, col_prev(4)), pl.BlockSpec(bs, col(4)), pl.BlockSpec(bs, col(1)), out, out,
                  pl.BlockSpec((1, cw), lambda g, n: (0, g))],
        out_specs=out,
        out_shape=jax.ShapeDtypeStruct((T, B), F32),
        compiler_params=_params("parallel", "parallel"),
    )(z, z, z, z, z, dcat, lse, delta, _alibi_row(B, d))


def _dilated_dkv(z, dcat, lse, delta, d, B, name):
    T = z.shape[0]
    hp, groups, nb = _dil_layout(T, B, d)
    cw = hp * HEAD_DIM
    scale = HEAD_DIM ** -0.5
    blk = ATT_BLOCK

    def body(k_ref, v_ref, qa_ref, qb_ref, doa_ref, dob_ref, la_ref, lb_ref, da_ref, db_ref, sl_ref, dk_ref, dv_ref):
        m = pl.program_id(1)
        qi = lax.broadcasted_iota(jnp.int32, (2 * blk, blk), 0)
        kj = lax.broadcasted_iota(jnp.int32, (2 * blk, blk), 1)
        dist = qi - kj
        valid = (dist >= 0) & (dist <= blk) & ((qi < blk) | (m + 1 < nb))
        distf = dist.astype(F32)
        for r in range(d):
            for hh in range(hp):
                cs = pl.ds(hh * HEAD_DIM, HEAD_DIM)

                def both(a_ref, b_ref):
                    return jnp.concatenate([_dil_rows(a_ref, r, d, cs), _dil_rows(b_ref, r, d, cs)], axis=0)

                q = (both(qa_ref, qb_ref) * scale).astype(BF16)
                do = both(doa_ref, dob_ref).astype(BF16)
                k = _dil_rows(k_ref, r, d, cs).astype(BF16)
                v = _dil_rows(v_ref, r, d, cs).astype(BF16)
                s = _dil_scores(q, k, sl_ref[:, cs][:, :1], valid, distf)
                p = jnp.exp(jnp.where(valid, s - both(la_ref, lb_ref)[:, :1], MASKED))
                _dil_put(dv_ref, r, d, cs, _dot(p.astype(BF16), do, TN))
                ds = p * (_dot(do, v, NT) - both(da_ref, db_ref)[:, :1])
                _dil_put(dk_ref, r, d, cs, _dot(ds.astype(BF16), q, TN))

    def col(unit):
        return lambda g, m: (m, unit * groups + g)

    def nxt(unit):
        return lambda g, m: (jnp.minimum(m + 1, nb - 1), unit * groups + g)

    bs = (d * blk, cw)
    out = pl.BlockSpec(bs, lambda g, m: (m, g))
    return pl.pallas_call(
        body, name=name, grid=(groups, nb),
        in_specs=[pl.BlockSpec(bs, col(3)), pl.BlockSpec(bs, col(4)),
                  pl.BlockSpec(bs, col(2)), pl.BlockSpec(bs, nxt(2)),
                  pl.BlockSpec(bs, col(1)), pl.BlockSpec(bs, nxt(1)),
                  out, pl.BlockSpec(bs, nxt(0)), out, pl.BlockSpec(bs, nxt(0)),
                  pl.BlockSpec((1, cw), lambda g, m: (0, g))],
        out_specs=[out, out],
        out_shape=[jax.ShapeDtypeStruct((T, B), F32)] * 2,
        compiler_params=_params("parallel", "parallel"),
    )(z, z, z, z, dcat, dcat, lse, lse, delta, delta, _alibi_row(B, d))


def _dilated_combine(dqs, dks, dvs, name):
    T, B = dqs[0].shape
    scale = HEAD_DIM ** -0.5

    def body(q1, q2, q3, k1, k2, k3, v1, v2, v3, o_ref):
        o_ref[:, pl.ds(0, B)] = ((q1[...] + q2[...] + q3[...]) * scale).astype(BF16)
        o_ref[:, pl.ds(B, B)] = (k1[...] + k2[...] + k3[...]).astype(BF16)
        o_ref[:, pl.ds(2 * B, B)] = (v1[...] + v2[...] + v3[...]).astype(BF16)

    row = pl.BlockSpec((ROWS, B), lambda i: (i, 0))
    return pl.pallas_call(
        body, name=name, grid=(T // ROWS,), in_specs=[row] * 9,
        out_specs=pl.BlockSpec((ROWS, 3 * B), lambda i: (i, 0)),
        out_shape=jax.ShapeDtypeStruct((T, 3 * B), BF16), compiler_params=_params("parallel"),
    )(*dqs, *dks, *dvs)


SB_QUERY_ROWS = 512
SB_KEYS = 2 * ATT_BLOCK


def _tri_and_ones(pred):
    rows = lax.broadcasted_iota(jnp.int32, (2 * ATT_BLOCK, 2 * ATT_BLOCK), 0) % ATT_BLOCK
    cols = lax.broadcasted_iota(jnp.int32, (2 * ATT_BLOCK, 2 * ATT_BLOCK), 1)
    return ((cols >= ATT_BLOCK) | pred(rows, cols)).astype(BF16)


def _running(x, tri):
    hi = x.astype(BF16)
    lo = (x - hi.astype(F32)).astype(BF16)
    return _dot(jnp.concatenate([hi, lo], axis=1), tri)


def _sb_mask(query_rows, s):
    rows = lax.broadcasted_iota(jnp.int32, (query_rows, SB_KEYS), 0)
    cols = lax.broadcasted_iota(jnp.int32, (query_rows, SB_KEYS), 1)
    return cols + s * SB_KEYS < rows


def _log_sigmoids(z):
    e = jnp.exp(-jnp.abs(z))
    ls = jnp.minimum(z, 0.0) - jnp.log(1.0 + e)
    return ls, ls - z, e


def _sb_fwd(qkv, W, name):
    T = qkv.shape[0]
    H = W // HEAD_DIM
    blk = ATT_BLOCK
    qb = min(SB_QUERY_ROWS, T)
    per = qb // SB_KEYS

    def body(q_ref, k_ref, v_ref, o_ref, lt_ref, acc_ref):
        i = pl.program_id(1)
        q = q_ref[...]
        tri = _tri_and_ones(lambda r, c: r > c)
        lt_ref[...] = jnp.zeros_like(lt_ref)
        acc_ref[...] = jnp.zeros_like(acc_ref)

        def tile(j, mask):
            ks = pl.ds(pl.multiple_of(j * SB_KEYS, SB_KEYS), SB_KEYS)
            z = _dot(q, k_ref[ks, :], NT)
            ls, lm, _ = _log_sigmoids(z)
            if mask is not None:
                lm = jnp.where(mask, lm, 0.0)
            later = lt_ref[...]
            second = _running(lm[:, blk:], tri)
            first = _running(lm[:, :blk], tri)
            after_first = later + second[:, blk:]
            a = jnp.exp(ls + jnp.concatenate([first[:, :blk] + after_first, second[:, :blk] + later], axis=1))
            if mask is not None:
                a = jnp.where(mask, a, 0.0)
            acc_ref[...] += _dot(a.astype(BF16), v_ref[ks, :])
            lt_ref[...] = after_first + first[:, blk:]

        for s in reversed(range(per)):
            tile(i * per + s, _sb_mask(qb, s))

        def step(jj, _):
            for s in range(per):
                tile((i - jj) * per - 1 - s, None)
            return 0

        lax.fori_loop(0, i, step, 0)
        o_ref[...] = acc_ref[...].astype(BF16)

    qs = pl.BlockSpec((qb, HEAD_DIM), lambda h, i: (i, h))
    return pl.pallas_call(
        body, name=name, grid=(H, T // qb),
        in_specs=[qs, pl.BlockSpec((T, HEAD_DIM), lambda h, i: (0, H + h)),
                  pl.BlockSpec((T, HEAD_DIM), lambda h, i: (0, 2 * H + h))],
        out_specs=[qs, qs],
        out_shape=[jax.ShapeDtypeStruct((T, W), BF16), jax.ShapeDtypeStruct((T, W), F32)],
        scratch_shapes=[pltpu.VMEM((qb, HEAD_DIM), F32)],
        compiler_params=_params("parallel", "arbitrary"),
    )(qkv, qkv, qkv)


def _sb_bwd(qkv, do, ltot, W, name):
    T = qkv.shape[0]
    H = W // HEAD_DIM
    blk = ATT_BLOCK
    nkb = T // SB_KEYS
    qb = min(SB_QUERY_ROWS, T)
    per = qb // SB_KEYS

    def body(q_ref, k_ref, v_ref, do_ref, lt_ref, dq_ref, dkt_ref, dvt_ref, qt_ref, dot_ref, plm_ref, pg_ref):
        i = pl.program_id(1)

        @pl.when(i == 0)
        def _():
            dkt_ref[...] = jnp.zeros_like(dkt_ref)
            dvt_ref[...] = jnp.zeros_like(dvt_ref)

        q = q_ref[...]
        do = do_ref[...]
        qt_ref[...] = q.astype(F32).T.astype(BF16)
        dot_ref[...] = do.astype(F32).T.astype(BF16)
        upto = _tri_and_ones(lambda r, c: r <= c)
        before = _tri_and_ones(lambda r, c: r < c)
        plm_ref[...] = jnp.zeros_like(plm_ref)
        pg_ref[...] = jnp.zeros_like(pg_ref)
        dq_ref[...] = jnp.zeros_like(dq_ref)

        def tile(j, mask):
            ks = pl.ds(pl.multiple_of(j * SB_KEYS, SB_KEYS), SB_KEYS)
            k = k_ref[ks, :]
            v = v_ref[ks, :]
            z = _dot(q, k, NT)
            ls, lm, e = _log_sigmoids(z)
            if mask is not None:
                lm = jnp.where(mask, lm, 0.0)
            earlier = plm_ref[...]
            first = _running(lm[:, :blk], upto)
            second = _running(lm[:, blk:], upto)
            upto_first = earlier + first[:, blk:]
            seen = jnp.concatenate([first[:, :blk] + earlier, second[:, :blk] + upto_first], axis=1)
            ltot = lt_ref[...]
            a = jnp.exp(ls + (jnp.concatenate([ltot, ltot], axis=1) - seen))
            if mask is not None:
                a = jnp.where(mask, a, 0.0)
            g = a * _dot(do, v, NT)
            g_earlier = pg_ref[...]
            g_first = _running(g[:, :blk], before)
            g_second = _running(g[:, blk:], before)
            g_upto_first = g_earlier + g_first[:, blk:]
            gsum = jnp.concatenate([g_first[:, :blk] + g_earlier, g_second[:, :blk] + g_upto_first], axis=1)
            r = 1.0 / (1.0 + e)
            pos = z >= 0.0
            sig = jnp.where(pos, r, e * r)
            nsig = jnp.where(pos, e * r, r)
            dz = g * nsig - gsum * sig
            if mask is not None:
                dz = jnp.where(mask, dz, 0.0)
            dzb = dz.astype(BF16)
            dkt_ref[j] += _dot(qt_ref[...], dzb)
            dvt_ref[j] += _dot(dot_ref[...], a.astype(BF16))
            dq_ref[...] += _dot(dzb, k)
            plm_ref[...] = upto_first + second[:, blk:]
            pg_ref[...] = g_upto_first + g_second[:, blk:]

        def step(jj, _):
            for s in range(per):
                tile(jj * per + s, None)
            return 0

        lax.fori_loop(0, i, step, 0)
        for s in range(per):
            tile(i * per + s, _sb_mask(qb, s))

    qs = pl.BlockSpec((qb, HEAD_DIM), lambda h, i: (i, h))
    res = pl.BlockSpec((None, nkb, HEAD_DIM, SB_KEYS), lambda h, i: (h, 0, 0, 0))
    return pl.pallas_call(
        body, name=name, grid=(H, T // qb),
        in_specs=[qs, pl.BlockSpec((T, HEAD_DIM), lambda h, i: (0, H + h)),
                  pl.BlockSpec((T, HEAD_DIM), lambda h, i: (0, 2 * H + h)), qs, qs],
        out_specs=[qs, res, res],
        out_shape=[jax.ShapeDtypeStruct((T, W), F32)] + [jax.ShapeDtypeStruct((H, nkb, HEAD_DIM, SB_KEYS), F32)] * 2,
        scratch_shapes=[pltpu.VMEM((HEAD_DIM, qb), BF16), pltpu.VMEM((HEAD_DIM, qb), BF16),
                        pltpu.VMEM((qb, HEAD_DIM), F32), pltpu.VMEM((qb, HEAD_DIM), F32)],
        compiler_params=_params("parallel", "arbitrary"),
    )(qkv, qkv, qkv, do, ltot)


def _sb_pack(dq, dkt, dvt, name):
    T, W = dq.shape
    H = W // HEAD_DIM
    blk = SB_KEYS
    scale = HEAD_DIM ** -0.5

    def body(q_ref, kt_ref, vt_ref, o_ref):
        o_ref[:, pl.ds(0, W)] = (q_ref[...] * scale).astype(BF16)
        for h in range(H):
            o_ref[:, pl.ds(W + h * HEAD_DIM, HEAD_DIM)] = kt_ref[h].T.astype(BF16)
            o_ref[:, pl.ds(2 * W + h * HEAD_DIM, HEAD_DIM)] = vt_ref[h].T.astype(BF16)

    tr = pl.BlockSpec((H, None, HEAD_DIM, blk), lambda i: (0, i, 0, 0))
    return pl.pallas_call(
        body, name=name, grid=(T // blk,), in_specs=[pl.BlockSpec((blk, W), lambda i: (i, 0)), tr, tr],
        out_specs=pl.BlockSpec((blk, 3 * W), lambda i: (i, 0)),
        out_shape=jax.ShapeDtypeStruct((T, 3 * W), BF16), compiler_params=_params("parallel"),
    )(dq, dkt, dvt)


def _local_step(x, target, norms, sgu, comm):
    T, D = x.shape
    A = D // 2
    pre_mix, post_mix, pre_ffn, post_ffn = norms
    ln_g, ln_b, w_s, b_s = sgu
    b_t = b_s.T
    scale = HEAD_DIM ** -0.5

    def vec(p, layer):
        return comm.tie(p[layer:layer + 1])

    h0 = _prenorm(x, vec(pre_mix, 0), "prenorm0")
    z = _matmul(h0, comm.weight("ab_in"), mode="nn", name="ab_in_fwd")
    a_out = _sgu_fwd(z, ln_g, ln_b, w_s, b_t, "sgu_fwd")
    branch = [_dilated_fwd(z, d, A, f"dilated_fwd_{d}") for _, d in DILATED_PAIRS]
    b_out, o_dil, lse_dil = _dilated_merge([b[0] for b in branch], [b[1] for b in branch], "dilated_merge")
    cat = jnp.concatenate([a_out, b_out], axis=-1)
    y0 = _matmul(cat, comm.weight("ab_out"), mode="nn", name="ab_out_fwd")
    comm.arrive("ffn0", after=y0)
    x1, h1 = _postnorm_prenorm(x, y0, vec(post_mix, 0), vec(pre_ffn, 0), "norm_mix0")
    comm.land("ffn0", after=h1)

    def relu2(acc, j):
        r = jnp.maximum(acc, 0.0)
        return r * r

    f0 = _matmul(h1, comm.weight("w1_0"), mode="nn", name="ffn0_w1_fwd", out_dtype=BF16, epi=relu2)
    y1 = _matmul(f0, comm.weight("w2_0"), mode="nn", name="ffn0_w2_fwd")
    comm.arrive("rest", after=y1)
    x2, h2 = _postnorm_prenorm(x1, y1, vec(post_ffn, 0), vec(pre_mix, 1), "norm_ffn0")
    comm.land("rest", after=h2)

    tn_qkv = _tile(D, 1024)
    nq = D // tn_qkv

    def scale_q(acc, j):
        return jnp.where(j < nq, acc * scale, acc)

    qkv = _matmul(h2, comm.weight("sb_in"), mode="nn", name="sb_in_fwd", out_dtype=BF16, tn=tn_qkv, epi=scale_q)
    o_sb, ltot = _sb_fwd(qkv, D, "sb_fwd")
    y2 = _matmul(o_sb, comm.weight("sb_out"), mode="nn", name="sb_out_fwd")
    x3, h3 = _postnorm_prenorm(x2, y2, vec(post_mix, 1), vec(pre_ffn, 1), "norm_mix1")
    f1 = _matmul(h3, comm.weight("w1_1"), mode="nn", name="ffn1_w1_fwd", out_dtype=BF16, epi=relu2)
    y3 = _matmul(f1, comm.weight("w2_1"), mode="nn", name="ffn1_w2_fwd")
    loss_tile, dx4 = _postnorm_loss(x3, y3, vec(post_ffn, 1), target, "norm_loss")
    loss = loss_tile[0, 0]

    def relu2_bwd(acc, j, f):
        return acc * (2.0 * jnp.sqrt(f.astype(F32)))

    def ffn_bwd(dx_out, x_in, h, f, y, layer, tag):
        dy, dg_post = _postnorm_bwd(dx_out, y, vec(post_ffn, layer), f"ffn{tag}_postnorm_bwd")
        g_w2 = _matmul(f, dy, mode="tn", name=f"ffn{tag}_w2_wgrad", out_dtype=BF16)
        da = _matmul(dy, comm.weight(f"w2_{layer}"), mode="nt", name=f"ffn{tag}_w2_dgrad", out_dtype=BF16,
                     epi=relu2_bwd, extras=(f,))
        g_w1 = _matmul(h, da, mode="tn", name=f"ffn{tag}_w1_wgrad", out_dtype=BF16)
        comm.reduce(f"ffn{tag}", {f"w2_{layer}": g_w2, f"w1_{layer}": g_w1})
        dh = _matmul(da, comm.weight(f"w1_{layer}"), mode="nt", name=f"ffn{tag}_w1_dgrad", after=comm.started())
        dx_in, dg_pre = _prenorm_bwd(dx_out, dh, x_in, vec(pre_ffn, layer), f"ffn{tag}_prenorm_bwd")
        return dx_in, dg_pre, dg_post

    dx3, dg_pre_ffn1, dg_post_ffn1 = ffn_bwd(dx4, x3, h3, f1, y3, 1, "1")

    dy2, dg_post_mix1 = _postnorm_bwd(dx3, y2, vec(post_mix, 1), "sb_postnorm_bwd")
    g_sb_out = _matmul(o_sb, dy2, mode="tn", name="sb_out_wgrad", out_dtype=BF16)
    do_sb = _matmul(dy2, comm.weight("sb_out"), mode="nt", name="sb_out_dgrad", out_dtype=BF16)
    dq, dk, dv = _sb_bwd(qkv, do_sb, ltot, D, "sb_bwd")
    dqkv = _sb_pack(dq, dk, dv, "sb_pack")
    g_sb_in = _matmul(h2, dqkv, mode="tn", name="sb_in_wgrad", out_dtype=BF16)
    comm.reduce("sb", {"sb_out": g_sb_out, "sb_in": g_sb_in})
    dh2 = _matmul(dqkv, comm.weight("sb_in"), mode="nt", name="sb_in_dgrad", after=comm.started())
    dx2, dg_pre_mix1 = _prenorm_bwd(dx3, dh2, x2, vec(pre_mix, 1), "sb_prenorm_bwd")

    dx1, dg_pre_ffn0, dg_post_ffn0 = ffn_bwd(dx2, x1, h1, f0, y1, 0, "0")

    dy0, dg_post_mix0 = _postnorm_bwd(dx1, y0, vec(post_mix, 0), "ab_postnorm_bwd")
    g_ab_out = _matmul(cat, dy0, mode="tn", name="ab_out_wgrad", out_dtype=BF16)
    comm.reduce("ab_out", {"ab_out": g_ab_out})
    dcat = _matmul(dy0, comm.weight("ab_out"), mode="nt", name="ab_out_dgrad", after=comm.started())
    duv, d_ln_g, d_ln_b, d_w_s, d_b_t = _sgu_bwd(z, dcat, ln_g, ln_b, w_s, b_t, "sgu_bwd")
    delta = _dilated_delta(dcat, o_dil, "dilated_delta")
    dqs, dks, dvs = [], [], []
    for _, d in DILATED_PAIRS:
        dqs.append(_dilated_dq(z, dcat, lse_dil, delta, d, A, f"dilated_dq_{d}"))
        dk_b, dv_b = _dilated_dkv(z, dcat, lse_dil, delta, d, A, f"dilated_dkv_{d}")
        dks.append(dk_b)
        dvs.append(dv_b)
    dqkv0 = _dilated_combine(dqs, dks, dvs, "dilated_combine")
    dz = jnp.concatenate([duv, dqkv0], axis=-1)
    g_ab_in = _matmul(h0, dz, mode="tn", name="ab_in_wgrad", out_dtype=BF16)
    comm.reduce("ab_in", {"ab_in": g_ab_in})
    dh0 = _matmul(dz, comm.weight("ab_in"), mode="nt", name="ab_in_dgrad", after=comm.started())
    dx0, dg_pre_mix0 = _prenorm_bwd(dx1, dh0, x, vec(pre_mix, 0), "ab_prenorm_bwd")

    small = {
        "pre_mix": jnp.concatenate([dg_pre_mix0, dg_pre_mix1], axis=0),
        "post_mix": jnp.concatenate([dg_post_mix0, dg_post_mix1], axis=0),
        "pre_ffn": jnp.concatenate([dg_pre_ffn0, dg_pre_ffn1], axis=0),
        "post_ffn": jnp.concatenate([dg_post_ffn0, dg_post_ffn1], axis=0),
        "ln_g": d_ln_g, "ln_b": d_ln_b, "w_s": d_w_s, "b_s": d_b_t.T,
    }
    return loss, dx0, small


MESH_ID = pl.DeviceIdType.MESH
ANY = pl.BlockSpec(memory_space=pl.ANY)


def _coords():
    return lax.axis_index("x"), lax.axis_index("y"), lax.axis_index("c")


def _shard_of(ref, kind, p):
    if kind == "col":
        n = ref.shape[1] // N_DEV
        return ref.at[:, pl.ds(pl.multiple_of(p * n, 128), n)]
    r = ref.shape[0] // N_DEV
    return ref.at[pl.ds(pl.multiple_of(p * r, 16), r), :]


def _full_shape(shard, kind):
    if kind == "col":
        return (shard.shape[0], shard.shape[1] * N_DEV)
    return (shard.shape[0] * N_DEV, shard.shape[1])


def _place(shard, kind, block, name):
    rows, cols = shard.shape
    tr = _tile(rows, 512)

    def body(b_ref, s_ref, o_ref):
        o_ref[...] = s_ref[...].astype(BF16)

    if kind == "col":
        out = pl.BlockSpec((tr, cols), lambda i, b_ref: (i, b_ref[0]))
    else:
        out = pl.BlockSpec((tr, cols), lambda i, b_ref: (b_ref[0] * (rows // tr) + i, 0))
    return pl.pallas_call(
        body, name=name,
        grid_spec=pltpu.PrefetchScalarGridSpec(
            num_scalar_prefetch=1, grid=(rows // tr,),
            in_specs=[pl.BlockSpec((tr, cols), lambda i, b_ref: (i, 0))], out_specs=out),
        out_shape=jax.ShapeDtypeStruct(_full_shape(shard, kind), BF16),
        compiler_params=_params("parallel"),
    )(block, shard)


def _all_gather(fulls, kinds):
    nt = len(fulls)

    def body(*refs):
        ins = refs[:nt]
        send_sems, recv_sems = refs[2 * nt:]
        x, y, c = _coords()
        me, sibling = (x, y, c), (x, y, 1 - c)
        chips = [(1 - x, y), (x, 1 - y), (1 - x, 1 - y)]

        def copy(t, k, block, to):
            px, py, pc = block
            slot = _shard_of(ins[t], kinds[t], 4 * px + 2 * py + pc)
            return pltpu.make_async_remote_copy(
                src_ref=slot, dst_ref=slot, send_sem=send_sems.at[7 * t + k], recv_sem=recv_sems.at[7 * t + k],
                device_id=to, device_id_type=MESH_ID)

        sent = []
        for t in range(nt):
            first = [copy(t, 0, me, sibling)] + [copy(t, 1 + j, me, (*chip, c)) for j, chip in enumerate(chips)]
            for cp in first:
                cp.start()
            sent += first
        for t in range(nt):
            for j, chip in enumerate(chips):
                copy(t, 1 + j, (*chip, c), me).wait_recv()
                sent.append(copy(t, 4 + j, (*chip, c), sibling))
                sent[-1].start()
        for t in range(nt):
            copy(t, 0, sibling, me).wait_recv()
            for j, chip in enumerate(chips):
                copy(t, 4 + j, (*chip, 1 - c), me).wait_recv()
        for cp in sent:
            cp.wait_send()

    return pl.pallas_call(
        body, name="all_gather_weights",
        in_specs=[ANY] * nt, out_specs=[ANY] * nt,
        out_shape=[jax.ShapeDtypeStruct(f.shape, f.dtype) for f in fulls],
        input_output_aliases={t: t for t in range(nt)},
        scratch_shapes=[pltpu.SemaphoreType.DMA((7 * nt,)), pltpu.SemaphoreType.DMA((7 * nt,))],
    )(*fulls)


HBM = pl.BlockSpec(memory_space=pltpu.HBM)
SEM = pl.BlockSpec(memory_space=pltpu.SEMAPHORE)
FLOWS = pltpu.SideEffectType.DATAFLOW_SIDE_EFFECTING


def _in_hbm(a):
    return pltpu.with_memory_space_constraint(a, pltpu.HBM)


def _hbm_like(bufs):
    return [pltpu.HBM(b.shape, b.dtype) for b in bufs]


def _copies_start(name, bufs, plan, n, after):
    nb = len(bufs)

    def body(*refs):
        send_sems, recv_sems, token = refs[nb + 1], refs[nb + 2], refs[-1]
        for cp in plan(refs[:nb], send_sems, recv_sems):
            cp.start()
        token[...] = jnp.zeros_like(token)

    out = pl.pallas_call(
        body, name=name, in_specs=[HBM] * nb + [ANY],
        out_specs=[SEM, SEM] + [HBM] * nb + [pl.BlockSpec(memory_space=pltpu.VMEM)],
        out_shape=[pltpu.SemaphoreType.DMA((n,)), pltpu.SemaphoreType.DMA((n,))] + _hbm_like(bufs)
        + [jax.ShapeDtypeStruct((8, 128), F32)],
        input_output_aliases={i: 2 + i for i in range(nb)},
        compiler_params=pltpu.CompilerParams(has_side_effects=FLOWS),
    )(*[_in_hbm(b) for b in bufs], after)
    return (out[0], out[1]), list(out[2:2 + nb]), out[-1]


def _copies_wait(name, bufs, sems, after, plan):
    nb = len(bufs)

    def body(*refs):
        for cp in plan(refs[:nb], refs[nb], refs[nb + 1]):
            cp.wait_send()
            cp.wait_recv()

    out = pl.pallas_call(
        body, name=name, in_specs=[HBM] * nb + [SEM, SEM, ANY], out_specs=[HBM] * nb,
        out_shape=_hbm_like(bufs), input_output_aliases={i: i for i in range(nb)},
        compiler_params=pltpu.CompilerParams(has_side_effects=FLOWS),
    )(*bufs, *sems, after)
    return list(out)


def _copies_wait_start(name, bufs, sems, after, plan, next_plan, n_next):
    nb = len(bufs)

    def body(*refs):
        ins = refs[:nb]
        for cp in plan(ins, refs[nb], refs[nb + 1]):
            cp.wait_send()
            cp.wait_recv()
        send_sems, recv_sems, token = refs[nb + 3], refs[nb + 4], refs[-1]
        for cp in next_plan(ins, send_sems, recv_sems):
            cp.start()
        token[...] = jnp.zeros_like(token)

    out = pl.pallas_call(
        body, name=name, in_specs=[HBM] * nb + [SEM, SEM, ANY],
        out_specs=[SEM, SEM] + [HBM] * nb + [pl.BlockSpec(memory_space=pltpu.VMEM)],
        out_shape=[pltpu.SemaphoreType.DMA((n_next,)), pltpu.SemaphoreType.DMA((n_next,))] + _hbm_like(bufs)
        + [jax.ShapeDtypeStruct((8, 128), F32)],
        input_output_aliases={i: 2 + i for i in range(nb)},
        compiler_params=pltpu.CompilerParams(has_side_effects=FLOWS),
    )(*bufs, *sems, after)
    return (out[0], out[1]), list(out[2:2 + nb]), out[-1]


def _gather_plans(kinds):
    nt = len(kinds)

    def slot(refs, t, px, py, pc):
        return _shard_of(refs[t], kinds[t], 4 * px + 2 * py + pc)

    def to_chips(refs, send_sems, recv_sems):
        x, y, c = _coords()
        peers = [(x, y, 1 - c), (1 - x, y, c), (x, 1 - y, c), (1 - x, 1 - y, c)]
        return [pltpu.make_async_remote_copy(
            src_ref=slot(refs, t, x, y, c), dst_ref=slot(refs, t, x, y, c), send_sem=send_sems.at[4 * t + k],
            recv_sem=recv_sems.at[4 * t + k], device_id=peer, device_id_type=MESH_ID)
            for t in range(nt) for k, peer in enumerate(peers)]

    def to_sibling(refs, send_sems, recv_sems):
        x, y, c = _coords()
        chips = [(1 - x, y), (x, 1 - y), (1 - x, 1 - y)]
        return [pltpu.make_async_remote_copy(
            src_ref=slot(refs, t, *chip, c), dst_ref=slot(refs, t, *chip, c), send_sem=send_sems.at[3 * t + j],
            recv_sem=recv_sems.at[3 * t + j], device_id=(x, y, 1 - c), device_id_type=MESH_ID)
            for t in range(nt) for j, chip in enumerate(chips)]

    return to_chips, to_sibling


def _shard_shape(full, kind):
    if kind == "col":
        return (full.shape[0], full.shape[1] // N_DEV)
    return (full.shape[0] // N_DEV, full.shape[1])


def _scatter_plan(kinds):
    nt = len(kinds)

    def plan(refs, send_sems, recv_sems):
        x, y, c = _coords()
        copies = []
        for t in range(nt):
            for k in range(1, N_DEV):
                px = 1 - x if (k >> 2) & 1 else x
                py = 1 - y if (k >> 1) & 1 else y
                pc = 1 - c if k & 1 else c
                copies.append(pltpu.make_async_remote_copy(
                    src_ref=_shard_of(refs[t], kinds[t], 4 * px + 2 * py + pc),
                    dst_ref=refs[nt + t].at[4 * x + 2 * y + c],
                    send_sem=send_sems.at[7 * t + k - 1], recv_sem=recv_sems.at[7 * t + k - 1],
                    device_id=(px, py, pc), device_id_type=MESH_ID))
        return copies
    return plan


def _partial_specs(full, kind, tr):
    rows, cols = _shard_shape(full, kind)
    steps = rows // tr
    if kind == "col":
        own = pl.BlockSpec((tr, cols), lambda i, w: (i, w[0]))
    else:
        own = pl.BlockSpec((tr, cols), lambda i, w: (w[0] * steps + i, 0))
    return [own] + [pl.BlockSpec((None, tr, cols), lambda i, w, k=k: (w[k], i, 0)) for k in range(1, N_DEV)]


def _all_reduce_small(vec, after):
    R = vec.shape[0]

    def body(v_ref, after_ref, o_ref, recv_ref, send_sems, recv_sems):
        x, y, c = _coords()
        me = 4 * x + 2 * y + c
        recv_ref[me] = v_ref[...]
        copies = []
        for k in range(1, N_DEV):
            bx, by, bc = (k >> 2) & 1, (k >> 1) & 1, k & 1
            peer = (1 - x if bx else x, 1 - y if by else y, 1 - c if bc else c)
            copies.append(pltpu.make_async_remote_copy(
                src_ref=v_ref, dst_ref=recv_ref.at[me],
                send_sem=send_sems.at[k - 1], recv_sem=recv_sems.at[k - 1],
                device_id=peer, device_id_type=MESH_ID))
        for cp in copies:
            cp.start()
        for cp in copies:
            cp.wait()
        total = recv_ref[0]
        for p in range(1, N_DEV):
            total = total + recv_ref[p]
        o_ref[...] = total

    return pl.pallas_call(
        body, name="all_reduce_small",
        in_specs=[pl.BlockSpec(memory_space=pltpu.VMEM), ANY], out_specs=pl.BlockSpec(memory_space=pltpu.VMEM),
        out_shape=jax.ShapeDtypeStruct((R, 128), F32),
        scratch_shapes=[pltpu.VMEM((N_DEV, R, 128), F32), pltpu.SemaphoreType.DMA((N_DEV - 1,)),
                        pltpu.SemaphoreType.DMA((N_DEV - 1,))],
        compiler_params=pltpu.CompilerParams(vmem_limit_bytes=VMEM_LIMIT),
    )(vec, after)


def _adamw_math(w, g, m, v):
    m = ADAM_B1 * m + (1.0 - ADAM_B1) * g
    v = ADAM_B2 * v + (1.0 - ADAM_B2) * (g * g)
    m_hat = m / (1.0 - ADAM_B1 ** ADAM_STEP)
    v_hat = v / (1.0 - ADAM_B2 ** ADAM_STEP)
    delta = -ADAM_LR * (m_hat / (jnp.sqrt(v_hat) + ADAM_EPS) + ADAM_WD * w)
    return delta, m, v


def _adamw(w, grads, kind, where, m, v, name):
    layers, rows, cols = w.shape
    tr = _tile(rows, 128)
    out = None
    for layer, (grad, land) in enumerate(grads):
        def body(w_ref, *refs):
            parts, (x_ref, m_ref, v_ref) = refs[:N_DEV], refs[N_DEV:N_DEV + 3]
            g_ref, d_ref, mo_ref, vo_ref = refs[-4:]
            g = parts[0][...].astype(F32)
            for p_ref in parts[1:]:
                g = g + p_ref[...].astype(F32)
            g_ref[...] = g
            d_ref[...], mo_ref[...], vo_ref[...] = _adamw_math(x_ref[...], g, m_ref[...], v_ref[...])

        blk = pl.BlockSpec((None, tr, cols), lambda i, w_, layer=layer: (layer, i, 0))
        earlier = [] if out is None else list(out)
        out = pl.pallas_call(
            body, name=f"{name}_{layer}",
            grid_spec=pltpu.PrefetchScalarGridSpec(
                num_scalar_prefetch=1, grid=(rows // tr,),
                in_specs=_partial_specs(grad, kind, tr) + [blk] * 3 + [ANY] * len(earlier),
                out_specs=[blk] * 4),
            out_shape=[jax.ShapeDtypeStruct((layers, rows, cols), F32)] * 4,
            input_output_aliases={N_DEV + 4 + k: k for k in range(len(earlier))},
            compiler_params=_params("parallel"),
        )(where, grad, *[land] * (N_DEV - 1), w, m, v, *earlier)
    return out


def _adamw_small(w, g, m, v):
    def body(w_ref, g_ref, m_ref, v_ref, d_ref, mo_ref, vo_ref):
        d_ref[...], mo_ref[...], vo_ref[...] = _adamw_math(w_ref[...], g_ref[...], m_ref[...], v_ref[...])

    whole = pl.BlockSpec(memory_space=pltpu.VMEM)
    return pl.pallas_call(
        body, name="adamw_small", in_specs=[whole] * 4, out_specs=[whole] * 3,
        out_shape=[jax.ShapeDtypeStruct(w.shape, F32)] * 3,
        compiler_params=pltpu.CompilerParams(vmem_limit_bytes=VMEM_LIMIT),
    )(w, g, m, v)


def _pack(arrays):
    rows = []
    for a in arrays:
        flat = a.reshape(-1)
        pad = (-flat.shape[0]) % 1024
        rows.append(jnp.pad(flat, (0, pad)).reshape(-1, 128))
    return jnp.concatenate(rows, axis=0)


def _unpack(packed, like):
    out, r = [], 0
    for a in like:
        n = math.prod(a.shape)
        nr = (n + 1023) // 1024 * 8
        out.append(packed[r:r + nr].reshape(-1)[:n].reshape(a.shape))
        r += nr
    return out


KIND = {"ab_in": "col", "ab_out": "row", "sb_in": "col", "sb_out": "row",
        "w1_0": "col", "w1_1": "col", "w2_0": "row", "w2_1": "row"}
GATHER_FIRST = ("ab_in", "ab_out")
GATHER_LATER = {"ffn0": ("w1_0", "w2_0"),
                "rest": ("sb_in", "sb_out", "w1_1", "w2_1")}


class _Exchange:
    def __init__(self, shards):
        x, y, c = _coords()
        me = (4 * x + 2 * y + c).astype(jnp.int32)
        self.where = jnp.stack([jnp.bitwise_xor(me, k) for k in range(N_DEV)])
        block = me.reshape(1)
        self.full = {n: _place(s, KIND[n], block, f"place_{n}") for n, s in shards.items()}
        first = _all_gather([self.full[n] for n in GATHER_FIRST], [KIND[n] for n in GATHER_FIRST])
        self.full.update(zip(GATHER_FIRST, first))
        self.tokens = []
        self.gathers = {}
        self.scatters = {}
        self.settled = {}
        after = first[-1]
        for key, group in GATHER_LATER.items():
            to_chips, to_sibling = _gather_plans([KIND[n] for n in group])
            bufs = [self.full[n] for n in group]
            sems, bufs, after = _copies_start(f"gather_start_{key}", bufs, to_chips, 4 * len(group), after)
            self.tokens.append(after)
            self.gathers[key] = (group, sems, bufs, to_chips, to_sibling)

    def tie(self, small):
        for token in self.tokens:
            small = small + token[0:1, 0:1]
        self.tokens = []
        return small

    def started(self):
        return tuple(self.tokens)

    def weight(self, name):
        return self.full[name]

    def arrive(self, key, after):
        group, sems, bufs, to_chips, to_sibling = self.gathers[key]
        sems, bufs, token = _copies_wait_start(f"gather_pass_{key}", bufs, sems, after, to_chips, to_sibling,
                                               3 * len(group))
        self.tokens.append(token)
        self.gathers[key] = (group, sems, bufs, to_chips, to_sibling)

    def land(self, key, after):
        group, sems, bufs, _, to_sibling = self.gathers.pop(key)
        self.full.update(zip(group, _copies_wait(f"gather_done_{key}", bufs, sems, after, to_sibling)))

    def reduce(self, key, grads):
        names = list(grads)
        kinds = [KIND[n] for n in names]
        full = [grads[n] for n in names]
        lands = [lax.empty((N_DEV,) + _shard_shape(g, k), BF16) for g, k in zip(full, kinds)]
        plan = _scatter_plan(kinds)
        sems, bufs, token = _copies_start(f"scatter_start_{key}", full + lands, plan, (N_DEV - 1) * len(names),
                                          full[-1])
        self.tokens.append(token)
        self.scatters[key] = (names, sems, bufs, plan)

    def settle(self, keys, after):
        for key in keys:
            names, sems, bufs, plan = self.scatters.pop(key)
            bufs = _copies_wait(f"scatter_done_{key}", bufs, sems, after, plan)
            self.settled.update({n: t for n, *t in zip(names, bufs[:len(names)], bufs[len(names):])})
        return self.settled


SMALL = ("norm_pre_mix", "norm_post_mix", "norm_pre_ffn", "norm_post_ffn", "sgu_ln_g", "sgu_ln_b", "sgu_w", "sgu_b")
ORDER = ("norm_pre_mix", "norm_post_mix", "norm_pre_ffn", "norm_post_ffn", "ab_w_in", "sgu_ln_g", "sgu_ln_b", "sgu_w",
         "sgu_b", "ab_w_out", "sb_w_in", "sb_w_out", "ffn_w1", "ffn_w2")


def kernel(x, norm_pre_mix, norm_post_mix, norm_pre_ffn, norm_post_ffn, ab_w_in, sgu_ln_g, sgu_ln_b, sgu_w, sgu_b, ab_w_out, sb_w_in, sb_w_out, ffn_w1, ffn_w2, loss_target, m_norm_pre_mix, m_norm_post_mix, m_norm_pre_ffn, m_norm_post_ffn, m_ab_w_in, m_sgu_ln_g, m_sgu_ln_b, m_sgu_w, m_sgu_b, m_ab_w_out, m_sb_w_in, m_sb_w_out, m_ffn_w1, m_ffn_w2, v_norm_pre_mix, v_norm_post_mix, v_norm_pre_ffn, v_norm_post_ffn, v_ab_w_in, v_sgu_ln_g, v_sgu_ln_b, v_sgu_w, v_sgu_b, v_ab_w_out, v_sb_w_in, v_sb_w_out, v_ffn_w1, v_ffn_w2):
    W = dict(norm_pre_mix=norm_pre_mix, norm_post_mix=norm_post_mix, norm_pre_ffn=norm_pre_ffn,
             norm_post_ffn=norm_post_ffn, ab_w_in=ab_w_in, sgu_ln_g=sgu_ln_g, sgu_ln_b=sgu_ln_b, sgu_w=sgu_w,
             sgu_b=sgu_b, ab_w_out=ab_w_out, sb_w_in=sb_w_in, sb_w_out=sb_w_out, ffn_w1=ffn_w1, ffn_w2=ffn_w2)
    M = dict(norm_pre_mix=m_norm_pre_mix, norm_post_mix=m_norm_post_mix, norm_pre_ffn=m_norm_pre_ffn,
             norm_post_ffn=m_norm_post_ffn, ab_w_in=m_ab_w_in, sgu_ln_g=m_sgu_ln_g, sgu_ln_b=m_sgu_ln_b,
             sgu_w=m_sgu_w, sgu_b=m_sgu_b, ab_w_out=m_ab_w_out, sb_w_in=m_sb_w_in, sb_w_out=m_sb_w_out,
             ffn_w1=m_ffn_w1, ffn_w2=m_ffn_w2)
    V = dict(norm_pre_mix=v_norm_pre_mix, norm_post_mix=v_norm_post_mix, norm_pre_ffn=v_norm_pre_ffn,
             norm_post_ffn=v_norm_post_ffn, ab_w_in=v_ab_w_in, sgu_ln_g=v_sgu_ln_g, sgu_ln_b=v_sgu_ln_b,
             sgu_w=v_sgu_w, sgu_b=v_sgu_b, ab_w_out=v_ab_w_out, sb_w_in=v_sb_w_in, sb_w_out=v_sb_w_out,
             ffn_w1=v_ffn_w1, ffn_w2=v_ffn_w2)

    shards = {"ab_in": ab_w_in[0], "ab_out": ab_w_out[0], "w1_0": ffn_w1[0], "w2_0": ffn_w2[0],
              "sb_in": sb_w_in[0], "sb_out": sb_w_out[0], "w1_1": ffn_w1[1], "w2_1": ffn_w2[1]}
    comm = _Exchange(shards)
    norms = (norm_pre_mix, norm_post_mix, norm_pre_ffn, norm_post_ffn)
    sgu = (sgu_ln_g, sgu_ln_b, sgu_w[0], sgu_b[0])
    loss, dx, small = _local_step(x[0], loss_target[0], norms, sgu, comm)
    loss = lax.psum(loss, MESH_AXES)

    out = {}

    def update(name, layers, landed):
        out[name] = _adamw(W[name], [landed[n] for n in layers], KIND[layers[0]], comm.where, M[name], V[name],
                           f"adamw_{name}")

    landed = comm.settle(("ffn1", "sb", "ffn0"), after=dx)
    for name, layers in (("sb_w_in", ["sb_in"]), ("sb_w_out", ["sb_out"]), ("ffn_w1", ["w1_0", "w1_1"]),
                         ("ffn_w2", ["w2_0", "w2_1"])):
        update(name, layers, landed)
    small_g = [small["pre_mix"], small["post_mix"], small["pre_ffn"], small["post_ffn"], small["ln_g"],
               small["ln_b"], small["w_s"][None], small["b_s"][None]]
    g_small = _all_reduce_small(_pack(small_g), out["ffn_w2"][3])
    landed = comm.settle(("ab_out", "ab_in"), after=g_small)
    update("ab_w_out", ["ab_out"], landed)
    update("ab_w_in", ["ab_in"], landed)
    res = _adamw_small(_pack([W[n] for n in SMALL]), g_small, _pack([M[n] for n in SMALL]),
                       _pack([V[n] for n in SMALL]))
    like = [W[n] for n in SMALL]
    for n, *vals in zip(SMALL, *[_unpack(r, like) for r in [g_small] + list(res)]):
        out[n] = vals

    return (loss, dx[None], *[out[n][0] for n in ORDER], *[out[n][1] for n in ORDER],
            *[out[n][2] for n in ORDER], *[out[n][3] for n in ORDER])
```

```python
import functools
import math

import jax
import jax.numpy as jnp
from jax import lax
from jax.experimental import pallas as pl
from jax.experimental.pallas import tpu as pltpu

F32 = jnp.float32
BF16 = jnp.bfloat16

HEAD_DIM = 128
CHUNK = 128
ATT_BLOCK = 128
DILATED_PAIRS = ((128, 1), (512, 4), (2048, 16))
RMS_EPS = 1e-6
LN_EPS = 1e-5
ADAM_LR = 0.001
ADAM_B1 = 0.9
ADAM_B2 = 0.999
ADAM_EPS = 1e-08
ADAM_WD = 0.01
ADAM_STEP = 10
N_DEV = 8
MESH_AXES = ("x", "y", "c")
MASKED = -1e30

V7X_VMEM_BYTES = 64 * 1024 * 1024
VMEM_LIMIT = V7X_VMEM_BYTES - 8 * 1024 * 1024

NN = (((1,), (0,)), ((), ()))
NT = (((1,), (1,)), ((), ()))
TN = (((0,), (0,)), ((), ()))


def _params(*sem):
    return pltpu.CompilerParams(dimension_semantics=sem, vmem_limit_bytes=VMEM_LIMIT)


def _dot(a, b, dims=NN):
    return lax.dot_general(a, b, dims, preferred_element_type=F32)


def _tile(n, preferred):
    if n <= preferred:
        return n
    t = preferred - preferred % 128
    while n % t:
        t -= 128
    assert t > 0, (n, preferred)
    return t


def _matmul(a, b, *, mode, name, out_dtype=F32, tm=1024, tn=1024, tk=2048, epi=None, extras=(), after=()):
    if mode == "nn":
        (M, K), N = a.shape, b.shape[1]
    elif mode == "nt":
        (M, K), N = a.shape, b.shape[0]
    else:
        (K, M), N = a.shape, b.shape[1]
    tm, tn, tk = _tile(M, tm), _tile(N, tn), _tile(K, tk)
    nk = K // tk
    if mode == "tn":
        a_spec = pl.BlockSpec((tk, tm), lambda i, j, k: (k, i))
    else:
        a_spec = pl.BlockSpec((tm, tk), lambda i, j, k: (i, k))
    if mode == "nt":
        b_spec = pl.BlockSpec((tn, tk), lambda i, j, k: (j, k))
    else:
        b_spec = pl.BlockSpec((tk, tn), lambda i, j, k: (k, j))
    o_spec = pl.BlockSpec((tm, tn), lambda i, j, k: (i, j))
    dims = {"nn": NN, "nt": NT, "tn": TN}[mode]
    n_extra = len(extras)
    n_in = n_extra + len(after)

    def finish(acc, refs):
        j = pl.program_id(1)
        if epi is None:
            return acc
        return epi(acc, j, *[r[...] for r in refs])

    if nk == 1:
        def body(a_ref, b_ref, *rest):
            o_ref = rest[n_in]
            acc = _dot(a_ref[...], b_ref[...], dims)
            o_ref[...] = finish(acc, rest[:n_extra]).astype(o_ref.dtype)
        scratch = []
    else:
        def body(a_ref, b_ref, *rest):
            o_ref, acc_ref = rest[n_in], rest[n_in + 1]
            k = pl.program_id(2)

            @pl.when(k == 0)
            def _():
                acc_ref[...] = jnp.zeros_like(acc_ref)

            acc_ref[...] += _dot(a_ref[...], b_ref[...], dims)

            @pl.when(k == nk - 1)
            def _():
                o_ref[...] = finish(acc_ref[...], rest[:n_extra]).astype(o_ref.dtype)
        scratch = [pltpu.VMEM((tm, tn), F32)]

    return pl.pallas_call(
        body,
        name=name,
        grid=(M // tm, N // tn, nk),
        in_specs=[a_spec, b_spec] + [o_spec] * n_extra + [ANY] * len(after),
        out_specs=o_spec,
        out_shape=jax.ShapeDtypeStruct((M, N), out_dtype),
        scratch_shapes=scratch,
        compiler_params=_params("parallel", "parallel", "arbitrary"),
    )(a, b, *extras, *after)


ROWS = 256


def _rms(x):
    return lax.rsqrt(jnp.mean(x * x, axis=-1, keepdims=True) + RMS_EPS)


def _prenorm(x, g, name):
    T, D = x.shape

    def body(x_ref, g_ref, h_ref):
        xv = x_ref[...]
        h_ref[...] = (xv * _rms(xv) * g_ref[...]).astype(BF16)

    row = pl.BlockSpec((ROWS, D), lambda i: (i, 0))
    vec = pl.BlockSpec((1, D), lambda i: (0, 0))
    return pl.pallas_call(
        body, name=name, grid=(T // ROWS,), in_specs=[row, vec], out_specs=row,
        out_shape=jax.ShapeDtypeStruct((T, D), BF16), compiler_params=_params("parallel"),
    )(x, g)


def _postnorm_prenorm(x, y, g_post, g_pre, name):
    T, D = x.shape

    def body(x_ref, y_ref, gp_ref, gn_ref, xo_ref, h_ref):
        yv = y_ref[...]
        xn = x_ref[...] + yv * _rms(yv) * gp_ref[...]
        xo_ref[...] = xn
        h_ref[...] = (xn * _rms(xn) * gn_ref[...]).astype(BF16)

    row = pl.BlockSpec((ROWS, D), lambda i: (i, 0))
    vec = pl.BlockSpec((1, D), lambda i: (0, 0))
    return pl.pallas_call(
        body, name=name, grid=(T // ROWS,), in_specs=[row, row, vec, vec], out_specs=[row, row],
        out_shape=[jax.ShapeDtypeStruct((T, D), F32), jax.ShapeDtypeStruct((T, D), BF16)],
        compiler_params=_params("parallel"),
    )(x, y, g_post, g_pre)


def _postnorm_grads(dn, yh, r, g):
    gd = dn * g
    return r * (gd - yh * jnp.mean(yh * gd, axis=-1, keepdims=True)), dn * yh


def _postnorm_loss(x, y, g_post, target, name):
    T, D = x.shape

    def body(x_ref, y_ref, gp_ref, t_ref, loss_ref, dx_ref, dy_ref, dg_ref):
        @pl.when(pl.program_id(0) == 0)
        def _():
            loss_ref[...] = jnp.zeros_like(loss_ref)
            dg_ref[...] = jnp.zeros_like(dg_ref)

        yv = y_ref[...]
        r = _rms(yv)
        yh = yv * r
        err = x_ref[...] + yh * gp_ref[...] - t_ref[...]
        dx = err * (1.0 / D)
        dx_ref[...] = dx
        loss_ref[...] += 0.5 * jnp.sum(jnp.sum(err * err, axis=-1, keepdims=True) * (1.0 / D))
        dy, dg = _postnorm_grads(dx, yh, r, gp_ref[...])
        dy_ref[...] = dy.astype(BF16)
        dg_ref[...] += jnp.sum(dg, axis=0, keepdims=True)

    row = pl.BlockSpec((ROWS, D), lambda i: (i, 0))
    vec = pl.BlockSpec((1, D), lambda i: (0, 0))
    acc = pl.BlockSpec((8, 128), lambda i: (0, 0))
    return pl.pallas_call(
        body, name=name, grid=(T // ROWS,), in_specs=[row, row, vec, row], out_specs=[acc, row, row, vec],
        out_shape=[jax.ShapeDtypeStruct((8, 128), F32), jax.ShapeDtypeStruct((T, D), F32),
                   jax.ShapeDtypeStruct((T, D), BF16), jax.ShapeDtypeStruct((1, D), F32)],
        compiler_params=_params("arbitrary"),
    )(x, y, g_post, target)


def _norm_bwd_pair(dx_out, dh, x, g_pre, y_prev, g_post_prev, name):
    T, D = x.shape

    def body(dxo_ref, dh_ref, x_ref, g_ref, y_ref, gp_ref, dx_ref, dg_ref, dy_ref, dgp_ref):
        @pl.when(pl.program_id(0) == 0)
        def _():
            dg_ref[...] = jnp.zeros_like(dg_ref)
            dgp_ref[...] = jnp.zeros_like(dgp_ref)

        xv, dhv = x_ref[...], dh_ref[...]
        r = _rms(xv)
        xh = xv * r
        gd = dhv * g_ref[...]
        dx = dxo_ref[...] + r * (gd - xh * jnp.mean(xh * gd, axis=-1, keepdims=True))
        dx_ref[...] = dx
        dg_ref[...] += jnp.sum(dhv * xh, axis=0, keepdims=True)
        yv = y_ref[...]
        ry = _rms(yv)
        dy, dgp = _postnorm_grads(dx, yv * ry, ry, gp_ref[...])
        dy_ref[...] = dy.astype(BF16)
        dgp_ref[...] += jnp.sum(dgp, axis=0, keepdims=True)

    row = pl.BlockSpec((ROWS, D), lambda i: (i, 0))
    vec = pl.BlockSpec((1, D), lambda i: (0, 0))
    return pl.pallas_call(
        body, name=name, grid=(T // ROWS,), in_specs=[row, row, row, vec, row, vec],
        out_specs=[row, vec, row, vec],
        out_shape=[jax.ShapeDtypeStruct((T, D), F32), jax.ShapeDtypeStruct((1, D), F32),
                   jax.ShapeDtypeStruct((T, D), BF16), jax.ShapeDtypeStruct((1, D), F32)],
        compiler_params=_params("arbitrary"),
    )(dx_out, dh, x, g_pre, y_prev, g_post_prev)


def _prenorm_bwd(dx_out, dh, x, g_pre, name):
    T, D = x.shape

    def body(dxo_ref, dh_ref, x_ref, g_ref, dx_ref, dg_ref):
        @pl.when(pl.program_id(0) == 0)
        def _():
            dg_ref[...] = jnp.zeros_like(dg_ref)

        xv, dhv = x_ref[...], dh_ref[...]
        r = _rms(xv)
        xh = xv * r
        gd = dhv * g_ref[...]
        dx_ref[...] = dxo_ref[...] + r * (gd - xh * jnp.mean(xh * gd, axis=-1, keepdims=True))
        dg_ref[...] += jnp.sum(dhv * xh, axis=0, keepdims=True)

    row = pl.BlockSpec((ROWS, D), lambda i: (i, 0))
    vec = pl.BlockSpec((1, D), lambda i: (0, 0))
    return pl.pallas_call(
        body, name=name, grid=(T // ROWS,), in_specs=[row, row, row, vec], out_specs=[row, vec],
        out_shape=[jax.ShapeDtypeStruct((T, D), F32), jax.ShapeDtypeStruct((1, D), F32)],
        compiler_params=_params("arbitrary"),
    )(dx_out, dh, x, g_pre)


_INV_SQRT2 = 1.0 / math.sqrt(2.0)
_INV_SQRT2PI = 1.0 / math.sqrt(2.0 * math.pi)


def _gelu(x):
    return 0.5 * x * (1.0 + lax.erf(x * _INV_SQRT2))


def _gelu_grad(x):
    return 0.5 * (1.0 + lax.erf(x * _INV_SQRT2)) + x * jnp.exp(-0.5 * x * x) * _INV_SQRT2PI


def _layernorm_stats(x):
    mu = jnp.mean(x, axis=-1, keepdims=True)
    xc = x - mu
    rstd = lax.rsqrt(jnp.mean(xc * xc, axis=-1, keepdims=True) + LN_EPS)
    return xc * rstd, rstd


def _tril_mask():
    i = lax.broadcasted_iota(jnp.int32, (CHUNK, CHUNK), 0)
    j = lax.broadcasted_iota(jnp.int32, (CHUNK, CHUNK), 1)
    return j <= i


SGU_ROWS = 512


def _sgu_fwd(z, ln_g, ln_b, w_s, b_t, name):
    T = z.shape[0]
    A = ln_g.shape[1]
    G = A // 128
    rows = min(SGU_ROWS, T)

    def body(u_ref, v_ref, g_ref, b_ref, w_ref, bt_ref, o_ref):
        mask = _tril_mask()
        for c in range(rows // CHUNK):
            rs = pl.ds(c * CHUNK, CHUNK)
            xh, _ = _layernorm_stats(_gelu(v_ref[rs, :]))
            vn = (xh * g_ref[...] + b_ref[...]).astype(BF16)
            for g in range(G):
                cs = pl.ds(g * 128, 128)
                w = jnp.where(mask, w_ref[g], 0.0).astype(BF16)
                mixed = _dot(w, vn[:, g * 128:(g + 1) * 128]) + bt_ref[:, g:g + 1]
                o_ref[rs, cs] = (_gelu(u_ref[rs, cs]) * mixed).astype(BF16)

    return pl.pallas_call(
        body, name=name, grid=(T // rows,),
        in_specs=[
            pl.BlockSpec((rows, A), lambda i: (i, 0)),
            pl.BlockSpec((rows, A), lambda i: (i, 1)),
            pl.BlockSpec((1, A), lambda i: (0, 0)),
            pl.BlockSpec((1, A), lambda i: (0, 0)),
            pl.BlockSpec((G, CHUNK, CHUNK), lambda i: (0, 0, 0)),
            pl.BlockSpec((CHUNK, G), lambda i: (0, 0)),
        ],
        out_specs=pl.BlockSpec((rows, A), lambda i: (i, 0)),
        out_shape=jax.ShapeDtypeStruct((T, A), BF16),
        compiler_params=_params("parallel"),
    )(z, z, ln_g, ln_b, w_s, b_t)


def _sgu_bwd(z, dcat, ln_g, ln_b, w_s, b_t, name):
    T = z.shape[0]
    A = ln_g.shape[1]
    G = A // 128
    rows = min(SGU_ROWS, T)

    def body(u_ref, v_ref, da_ref, g_ref, b_ref, w_ref, bt_ref, dz_ref, dg_ref, db_ref, dw_ref, dbt_ref, dvn_ref):
        @pl.when(pl.program_id(0) == 0)
        def _():
            dg_ref[...] = jnp.zeros_like(dg_ref)
            db_ref[...] = jnp.zeros_like(db_ref)
            dw_ref[...] = jnp.zeros_like(dw_ref)
            dbt_ref[...] = jnp.zeros_like(dbt_ref)

        mask = _tril_mask()
        for c in range(rows // CHUNK):
            rs = pl.ds(c * CHUNK, CHUNK)
            vv = v_ref[rs, :]
            gv = _gelu(vv)
            xh, rstd = _layernorm_stats(gv)
            vn = (xh * g_ref[...] + b_ref[...]).astype(BF16)
            for g in range(G):
                cs = pl.ds(g * 128, 128)
                w = jnp.where(mask, w_ref[g], 0.0).astype(BF16)
                vg = vn[:, g * 128:(g + 1) * 128]
                mixed = _dot(w, vg) + bt_ref[:, g:g + 1]
                uu = u_ref[rs, cs]
                da = da_ref[rs, cs]
                dz_ref[rs, cs] = (da * mixed * _gelu_grad(uu)).astype(BF16)
                dm = da * _gelu(uu)
                dmb = dm.astype(BF16)
                dbt_ref[:, g:g + 1] += jnp.sum(dm, axis=1, keepdims=True)
                dw_ref[g] += jnp.where(mask, _dot(dmb, vg, NT), 0.0)
                dvn_ref[:, cs] = _dot(w, dmb, TN)
            dvn = dvn_ref[...]
            dg_ref[...] += jnp.sum(dvn * xh, axis=0, keepdims=True)
            db_ref[...] += jnp.sum(dvn, axis=0, keepdims=True)
            dxh = dvn * g_ref[...]
            dgv = rstd * (dxh - jnp.mean(dxh, axis=-1, keepdims=True)
                          - xh * jnp.mean(dxh * xh, axis=-1, keepdims=True))
            dz_ref[rs, pl.ds(A, A)] = (dgv * _gelu_grad(vv)).astype(BF16)

    vec = pl.BlockSpec((1, A), lambda i: (0, 0))
    wsp = pl.BlockSpec((G, CHUNK, CHUNK), lambda i: (0, 0, 0))
    bsp = pl.BlockSpec((CHUNK, G), lambda i: (0, 0))
    return pl.pallas_call(
        body, name=name, grid=(T // rows,),
        in_specs=[
            pl.BlockSpec((rows, A), lambda i: (i, 0)),
            pl.BlockSpec((rows, A), lambda i: (i, 1)),
            pl.BlockSpec((rows, A), lambda i: (i, 0)),
            vec, vec, wsp, bsp,
        ],
        out_specs=[pl.BlockSpec((rows, 2 * A), lambda i: (i, 0)), vec, vec, wsp, bsp],
        out_shape=[
            jax.ShapeDtypeStruct((T, 2 * A), BF16),
            jax.ShapeDtypeStruct((1, A), F32),
            jax.ShapeDtypeStruct((1, A), F32),
            jax.ShapeDtypeStruct((G, CHUNK, CHUNK), F32),
            jax.ShapeDtypeStruct((CHUNK, G), F32),
        ],
        scratch_shapes=[pltpu.VMEM((CHUNK, A), F32)],
        compiler_params=_params("arbitrary"),
    )(z, z, dcat, ln_g, ln_b, w_s, b_t)


def _alibi_row(B, d):
    H = B // HEAD_DIM
    slopes = [d * 2.0 ** (-8.0 * (h + 1.0) / H) for h in range(H)]
    return jnp.repeat(jnp.asarray(slopes, F32), HEAD_DIM)[None, :]


def _dil_scores(q, k, slope_d, valid, dist):
    s = _dot(q, k, NT) - slope_d * dist
    return jnp.where(valid, s, MASKED)


def _dil_layout(T, B, d):
    H = B // HEAD_DIM
    hp = H if d == 1 else 1
    return hp, H // hp, T // (d * ATT_BLOCK)


def _dil_rows(ref, r, d, cs):
    return ref[pl.ds(r, ATT_BLOCK, stride=d), cs] if d > 1 else ref[:, cs]


def _dil_put(ref, r, d, cs, val):
    if d > 1:
        ref[pl.ds(r, ATT_BLOCK, stride=d), cs] = val
    else:
        ref[:, cs] = val


def _dilated_fwd(z, d, B, name):
    T = z.shape[0]
    H = B // HEAD_DIM
    hp, groups, nb = _dil_layout(T, B, d)
    cw = hp * HEAD_DIM
    scale = HEAD_DIM ** -0.5
    blk = ATT_BLOCK

    def body(q_ref, kp_ref, kc_ref, vp_ref, vc_ref, sl_ref, o_ref, l_ref):
        n = pl.program_id(1)
        qi = lax.broadcasted_iota(jnp.int32, (blk, 2 * blk), 0)
        kj = lax.broadcasted_iota(jnp.int32, (blk, 2 * blk), 1)
        dist = qi + blk - kj
        valid = (dist >= 0) & (dist <= blk) & ((kj >= blk) | (n > 0))
        distf = dist.astype(F32)
        for r in range(d):
            for hh in range(hp):
                cs = pl.ds(hh * HEAD_DIM, HEAD_DIM)
                q = (_dil_rows(q_ref, r, d, cs) * scale).astype(BF16)
                k = jnp.concatenate([_dil_rows(kp_ref, r, d, cs), _dil_rows(kc_ref, r, d, cs)], axis=0).astype(BF16)
                v = jnp.concatenate([_dil_rows(vp_ref, r, d, cs), _dil_rows(vc_ref, r, d, cs)], axis=0).astype(BF16)
                s = _dil_scores(q, k, sl_ref[:, cs][:, :1], valid, distf)
                m = jnp.max(s, axis=-1, keepdims=True)
                p = jnp.exp(s - m)
                den = jnp.sum(p, axis=-1, keepdims=True)
                _dil_put(o_ref, r, d, cs, _dot(p.astype(BF16), v) / den)
                _dil_put(l_ref, r, d, cs, jnp.broadcast_to(m + jnp.log(den), (blk, HEAD_DIM)))

    def col(unit):
        return lambda g, n: (n, unit * groups + g)

    def col_prev(unit):
        return lambda g, n: (jnp.maximum(n - 1, 0), unit * groups + g)

    bs = (d * blk, cw)
    out = pl.BlockSpec(bs, lambda g, n: (n, g))
    return pl.pallas_call(
        body, name=name, grid=(groups, nb),
        in_specs=[pl.BlockSpec(bs, col(2)), pl.BlockSpec(bs, col_prev(3)), pl.BlockSpec(bs, col(3)),
                  pl.BlockSpec(bs, col_prev(4)), pl.BlockSpec(bs, col(4)), pl.BlockSpec((1, cw), lambda g, n: (0, g))],
        out_specs=[out, out],
        out_shape=[jax.ShapeDtypeStruct((T, B), F32)] * 2,
        compiler_params=_params("parallel", "parallel"),
    )(z, z, z, z, z, _alibi_row(B, d))


def _dilated_merge(a_out, os_, ls_, name):
    T, B = os_[0].shape
    A = a_out.shape[1]

    def body(a_ref, o1, o2, o3, l1, l2, l3, cat_ref, of_ref, lt_ref):
        a, b, c = l1[...], l2[...], l3[...]
        m = jnp.maximum(jnp.maximum(a, b), c)
        ea, eb, ec = jnp.exp(a - m), jnp.exp(b - m), jnp.exp(c - m)
        tot = ea + eb + ec
        o = (ea * o1[...] + eb * o2[...] + ec * o3[...]) / tot
        of_ref[...] = o
        cat_ref[:, pl.ds(0, A)] = a_ref[...]
        cat_ref[:, pl.ds(A, B)] = o.astype(BF16)
        lt_ref[...] = m + jnp.log(tot)

    row = pl.BlockSpec((ROWS, B), lambda i: (i, 0))
    return pl.pallas_call(
        body, name=name, grid=(T // ROWS,), in_specs=[pl.BlockSpec((ROWS, A), lambda i: (i, 0))] + [row] * 6,
        out_specs=[pl.BlockSpec((ROWS, A + B), lambda i: (i, 0)), row, row],
        out_shape=[jax.ShapeDtypeStruct((T, A + B), BF16), jax.ShapeDtypeStruct((T, B), F32),
                   jax.ShapeDtypeStruct((T, B), F32)],
        compiler_params=_params("parallel"),
    )(a_out, *os_, *ls_)


def _dilated_delta(dcat, o, name):
    T, B = o.shape
    H = B // HEAD_DIM

    def body(do_ref, o_ref, d_ref):
        for h in range(H):
            cs = pl.ds(h * HEAD_DIM, HEAD_DIM)
            dsum = jnp.sum(do_ref[:, cs] * o_ref[:, cs], axis=-1, keepdims=True)
            d_ref[:, cs] = jnp.broadcast_to(dsum, (ROWS, HEAD_DIM))

    row = pl.BlockSpec((ROWS, B), lambda i: (i, 0))
    return pl.pallas_call(
        body, name=name, grid=(T // ROWS,),
        in_specs=[pl.BlockSpec((ROWS, B), lambda i: (i, 1)), row], out_specs=row,
        out_shape=jax.ShapeDtypeStruct((T, B), F32), compiler_params=_params("parallel"),
    )(dcat, o)


def _dilated_dq(z, dcat, lse, delta, d, B, name):
    T = z.shape[0]
    hp, groups, nb = _dil_layout(T, B, d)
    cw = hp * HEAD_DIM
    scale = HEAD_DIM ** -0.5
    blk = ATT_BLOCK

    def body(q_ref, kp_ref, kc_ref, vp_ref, vc_ref, do_ref, l_ref, dl_ref, sl_ref, dq_ref):
        n = pl.program_id(1)
        qi = lax.broadcasted_iota(jnp.int32, (blk, 2 * blk), 0)
        kj = lax.broadcasted_iota(jnp.int32, (blk, 2 * blk), 1)
        dist = qi + blk - kj
        valid = (dist >= 0) & (dist <= blk) & ((kj >= blk) | (n > 0))
        distf = dist.astype(F32)
        for r in range(d):
            for hh in range(hp):
                cs = pl.ds(hh * HEAD_DIM, HEAD_DIM)
                q = (_dil_rows(q_ref, r, d, cs) * scale).astype(BF16)
                k = jnp.concatenate([_dil_rows(kp_ref, r, d, cs), _dil_rows(kc_ref, r, d, cs)], axis=0).astype(BF16)
                v = jnp.concatenate([_dil_rows(vp_ref, r, d, cs), _dil_rows(vc_ref, r, d, cs)], axis=0).astype(BF16)
                s = _dil_scores(q, k, sl_ref[:, cs][:, :1], valid, distf)
                p = jnp.exp(s - _dil_rows(l_ref, r, d, cs)[:, :1])
                dp = _dot(_dil_rows(do_ref, r, d, cs).astype(BF16), v, NT)
                ds = p * (dp - _dil_rows(dl_ref, r, d, cs)[:, :1])
                _dil_put(dq_ref, r, d, cs, _dot(ds.astype(BF16), k))

    def col(unit):
        return lambda g, n: (n, unit * groups + g)

    def col_prev(unit):
        return lambda g, n: (jnp.maximum(n - 1, 0), unit * groups + g)

    bs = (d * blk, cw)
    out = pl.BlockSpec(bs, lambda g, n: (n, g))
    return pl.pallas_call(
        body, name=name, grid=(groups, nb),
        in_specs=[pl.BlockSpec(bs, col(2)), pl.BlockSpec(bs, col_prev(3)), pl.BlockSpec(bs, col(3)),
                  pl.BlockSpec(bs, col_prev(4)), pl.BlockSpec(bs, col(4)), pl.BlockSpec(bs, col(1)), out, out,
                  pl.BlockSpec((1, cw), lambda g, n: (0, g))],
        out_specs=out,
        out_shape=jax.ShapeDtypeStruct((T, B), F32),
        compiler_params=_params("parallel", "parallel"),
    )(z, z, z, z, z, dcat, lse, delta, _alibi_row(B, d))


def _dilated_dkv(z, dcat, lse, delta, d, B, name):
    T = z.shape[0]
    hp, groups, nb = _dil_layout(T, B, d)
    cw = hp * HEAD_DIM
    scale = HEAD_DIM ** -0.5
    blk = ATT_BLOCK

    def body(k_ref, v_ref, qa_ref, qb_ref, doa_ref, dob_ref, la_ref, lb_ref, da_ref, db_ref, sl_ref, dk_ref, dv_ref):
        m = pl.program_id(1)
        qi = lax.broadcasted_iota(jnp.int32, (2 * blk, blk), 0)
        kj = lax.broadcasted_iota(jnp.int32, (2 * blk, blk), 1)
        dist = qi - kj
        valid = (dist >= 0) & (dist <= blk) & ((qi < blk) | (m + 1 < nb))
        distf = dist.astype(F32)
        for r in range(d):
            for hh in range(hp):
                cs = pl.ds(hh * HEAD_DIM, HEAD_DIM)

                def both(a_ref, b_ref):
                    return jnp.concatenate([_dil_rows(a_ref, r, d, cs), _dil_rows(b_ref, r, d, cs)], axis=0)

                q = (both(qa_ref, qb_ref) * scale).astype(BF16)
                do = both(doa_ref, dob_ref).astype(BF16)
                k = _dil_rows(k_ref, r, d, cs).astype(BF16)
                v = _dil_rows(v_ref, r, d, cs).astype(BF16)
                s = _dil_scores(q, k, sl_ref[:, cs][:, :1], valid, distf)
                p = jnp.exp(jnp.where(valid, s - both(la_ref, lb_ref)[:, :1], MASKED))
                _dil_put(dv_ref, r, d, cs, _dot(p.astype(BF16), do, TN))
                ds = p * (_dot(do, v, NT) - both(da_ref, db_ref)[:, :1])
                _dil_put(dk_ref, r, d, cs, _dot(ds.astype(BF16), q, TN))

    def col(unit):
        return lambda g, m: (m, unit * groups + g)

    def nxt(unit):
        return lambda g, m: (jnp.minimum(m + 1, nb - 1), unit * groups + g)

    bs = (d * blk, cw)
    out = pl.BlockSpec(bs, lambda g, m: (m, g))
    return pl.pallas_call(
        body, name=name, grid=(groups, nb),
        in_specs=[pl.BlockSpec(bs, col(3)), pl.BlockSpec(bs, col(4)),
                  pl.BlockSpec(bs, col(2)), pl.BlockSpec(bs, nxt(2)),
                  pl.BlockSpec(bs, col(1)), pl.BlockSpec(bs, nxt(1)),
                  out, pl.BlockSpec(bs, nxt(0)), out, pl.BlockSpec(bs, nxt(0)),
                  pl.BlockSpec((1, cw), lambda g, m: (0, g))],
        out_specs=[out, out],
        out_shape=[jax.ShapeDtypeStruct((T, B), F32)] * 2,
        compiler_params=_params("parallel", "parallel"),
    )(z, z, z, z, dcat, dcat, lse, lse, delta, delta, _alibi_row(B, d))


def _dilated_combine(duv, dqs, dks, dvs, name):
    T, B = dqs[0].shape
    A2 = duv.shape[1]
    scale = HEAD_DIM ** -0.5

    def body(uv, q1, q2, q3, k1, k2, k3, v1, v2, v3, o_ref):
        o_ref[:, pl.ds(0, A2)] = uv[...]
        o_ref[:, pl.ds(A2, B)] = ((q1[...] + q2[...] + q3[...]) * scale).astype(BF16)
        o_ref[:, pl.ds(A2 + B, B)] = (k1[...] + k2[...] + k3[...]).astype(BF16)
        o_ref[:, pl.ds(A2 + 2 * B, B)] = (v1[...] + v2[...] + v3[...]).astype(BF16)

    row = pl.BlockSpec((ROWS, B), lambda i: (i, 0))
    return pl.pallas_call(
        body, name=name, grid=(T // ROWS,), in_specs=[pl.BlockSpec((ROWS, A2), lambda i: (i, 0))] + [row] * 9,
        out_specs=pl.BlockSpec((ROWS, A2 + 3 * B), lambda i: (i, 0)),
        out_shape=jax.ShapeDtypeStruct((T, A2 + 3 * B), BF16), compiler_params=_params("parallel"),
    )(duv, *dqs, *dks, *dvs)


SB_QUERY_ROWS = 512
SB_KEYS = 2 * ATT_BLOCK


def _tri_and_ones(pred):
    rows = lax.broadcasted_iota(jnp.int32, (2 * ATT_BLOCK, 2 * ATT_BLOCK), 0) % ATT_BLOCK
    cols = lax.broadcasted_iota(jnp.int32, (2 * ATT_BLOCK, 2 * ATT_BLOCK), 1)
    return ((cols >= ATT_BLOCK) | pred(rows, cols)).astype(BF16)


def _running(x, tri):
    hi = x.astype(BF16)
    lo = (x - hi.astype(F32)).astype(BF16)
    return _dot(jnp.concatenate([hi, lo], axis=1), tri)


def _sb_mask(query_rows, s):
    rows = lax.broadcasted_iota(jnp.int32, (query_rows, SB_KEYS), 0)
    cols = lax.broadcasted_iota(jnp.int32, (query_rows, SB_KEYS), 1)
    return cols + s * SB_KEYS < rows


def _log_sigmoids(z):
    ls = jnp.minimum(z, 0.0) - jnp.log(1.0 + jnp.exp(-jnp.abs(z)))
    return ls, ls - z


def _sb_fwd(qkv, W, name):
    T = qkv.shape[0]
    H = W // HEAD_DIM
    blk = ATT_BLOCK
    qb = min(SB_QUERY_ROWS, T)
    per = qb // SB_KEYS

    def body(q_ref, k_ref, v_ref, o_ref, lt_ref, acc_ref):
        i = pl.program_id(1)
        q = q_ref[...]
        tri = _tri_and_ones(lambda r, c: r > c)
        lt_ref[...] = jnp.zeros_like(lt_ref)
        acc_ref[...] = jnp.zeros_like(acc_ref)

        def tile(j, mask):
            ks = pl.ds(pl.multiple_of(j * SB_KEYS, SB_KEYS), SB_KEYS)
            z = _dot(q, k_ref[ks, :], NT)
            ls, lm = _log_sigmoids(z)
            if mask is not None:
                lm = jnp.where(mask, lm, 0.0)
            later = lt_ref[...]
            second = _running(lm[:, blk:], tri)
            first = _running(lm[:, :blk], tri)
            after_first = later + second[:, blk:]
            a = jnp.exp(ls + jnp.concatenate([first[:, :blk] + after_first, second[:, :blk] + later], axis=1))
            if mask is not None:
                a = jnp.where(mask, a, 0.0)
            acc_ref[...] += _dot(a.astype(BF16), v_ref[ks, :])
            lt_ref[...] = after_first + first[:, blk:]

        for s in reversed(range(per)):
            tile(i * per + s, _sb_mask(qb, s))

        def step(jj, _):
            for s in range(per):
                tile((i - jj) * per - 1 - s, None)
            return 0

        lax.fori_loop(0, i, step, 0)
        o_ref[...] = acc_ref[...].astype(BF16)

    qs = pl.BlockSpec((qb, HEAD_DIM), lambda h, i: (i, h))
    return pl.pallas_call(
        body, name=name, grid=(H, T // qb),
        in_specs=[qs, pl.BlockSpec((T, HEAD_DIM), lambda h, i: (0, H + h)),
                  pl.BlockSpec((T, HEAD_DIM), lambda h, i: (0, 2 * H + h))],
        out_specs=[qs, qs],
        out_shape=[jax.ShapeDtypeStruct((T, W), BF16), jax.ShapeDtypeStruct((T, W), F32)],
        scratch_shapes=[pltpu.VMEM((qb, HEAD_DIM), F32)],
        compiler_params=_params("parallel", "arbitrary"),
    )(qkv, qkv, qkv)


def _sb_bwd(qkv, do, ltot, W, name):
    T = qkv.shape[0]
    H = W // HEAD_DIM
    blk = ATT_BLOCK
    nkb = T // SB_KEYS
    qb = min(SB_QUERY_ROWS, T)
    per = qb // SB_KEYS

    def body(q_ref, k_ref, v_ref, do_ref, lt_ref, dq_ref, dkt_ref, dvt_ref, qt_ref, dot_ref, plm_ref, pg_ref):
        i = pl.program_id(1)

        @pl.when(i == 0)
        def _():
            dkt_ref[...] = jnp.zeros_like(dkt_ref)
            dvt_ref[...] = jnp.zeros_like(dvt_ref)

        q = q_ref[...]
        do = do_ref[...]
        qt_ref[...] = q.astype(F32).T.astype(BF16)
        dot_ref[...] = do.astype(F32).T.astype(BF16)
        upto = _tri_and_ones(lambda r, c: r <= c)
        before = _tri_and_ones(lambda r, c: r < c)
        plm_ref[...] = jnp.zeros_like(plm_ref)
        pg_ref[...] = jnp.zeros_like(pg_ref)
        dq_ref[...] = jnp.zeros_like(dq_ref)

        def tile(j, mask):
            ks = pl.ds(pl.multiple_of(j * SB_KEYS, SB_KEYS), SB_KEYS)
            k = k_ref[ks, :]
            v = v_ref[ks, :]
            z = _dot(q, k, NT)
            ls, lm = _log_sigmoids(z)
            nsig = jnp.exp(lm)
            if mask is not None:
                lm = jnp.where(mask, lm, 0.0)
            earlier = plm_ref[...]
            first = _running(lm[:, :blk], upto)
            second = _running(lm[:, blk:], upto)
            upto_first = earlier + first[:, blk:]
            seen = jnp.concatenate([first[:, :blk] + earlier, second[:, :blk] + upto_first], axis=1)
            ltot = lt_ref[...]
            a = jnp.exp(ls + (jnp.concatenate([ltot, ltot], axis=1) - seen))
            if mask is not None:
                a = jnp.where(mask, a, 0.0)
            g = a * _dot(do, v, NT)
            g_earlier = pg_ref[...]
            g_first = _running(g[:, :blk], before)
            g_second = _running(g[:, blk:], before)
            g_upto_first = g_earlier + g_first[:, blk:]
            gsum = jnp.concatenate([g_first[:, :blk] + g_earlier, g_second[:, :blk] + g_upto_first], axis=1)
            dz = g * nsig - gsum * jnp.exp(ls)
            if mask is not None:
                dz = jnp.where(mask, dz, 0.0)
            dzb = dz.astype(BF16)
            dkt_ref[j] += _dot(qt_ref[...], dzb)
            dvt_ref[j] += _dot(dot_ref[...], a.astype(BF16))
            dq_ref[...] += _dot(dzb, k)
            plm_ref[...] = upto_first + second[:, blk:]
            pg_ref[...] = g_upto_first + g_second[:, blk:]

        def step(jj, _):
            for s in range(per):
                tile(jj * per + s, None)
            return 0

        lax.fori_loop(0, i, step, 0)
        for s in range(per):
            tile(i * per + s, _sb_mask(qb, s))

    qs = pl.BlockSpec((qb, HEAD_DIM), lambda h, i: (i, h))
    res = pl.BlockSpec((None, nkb, HEAD_DIM, SB_KEYS), lambda h, i: (h, 0, 0, 0))
    return pl.pallas_call(
        body, name=name, grid=(H, T // qb),
        in_specs=[qs, pl.BlockSpec((T, HEAD_DIM), lambda h, i: (0, H + h)),
                  pl.BlockSpec((T, HEAD_DIM), lambda h, i: (0, 2 * H + h)), qs, qs],
        out_specs=[qs, res, res],
        out_shape=[jax.ShapeDtypeStruct((T, W), F32)] + [jax.ShapeDtypeStruct((H, nkb, HEAD_DIM, SB_KEYS), F32)] * 2,
        scratch_shapes=[pltpu.VMEM((HEAD_DIM, qb), BF16), pltpu.VMEM((HEAD_DIM, qb), BF16),
                        pltpu.VMEM((qb, HEAD_DIM), F32), pltpu.VMEM((qb, HEAD_DIM), F32)],
        compiler_params=_params("parallel", "arbitrary"),
    )(qkv, qkv, qkv, do, ltot)


def _sb_pack(dq, dkt, dvt, name):
    T, W = dq.shape
    H = W // HEAD_DIM
    blk = SB_KEYS
    scale = HEAD_DIM ** -0.5

    def body(q_ref, kt_ref, vt_ref, o_ref):
        o_ref[:, pl.ds(0, W)] = (q_ref[...] * scale).astype(BF16)
        for h in range(H):
            o_ref[:, pl.ds(W + h * HEAD_DIM, HEAD_DIM)] = kt_ref[h].T.astype(BF16)
            o_ref[:, pl.ds(2 * W + h * HEAD_DIM, HEAD_DIM)] = vt_ref[h].T.astype(BF16)

    tr = pl.BlockSpec((H, None, HEAD_DIM, blk), lambda i: (0, i, 0, 0))
    return pl.pallas_call(
        body, name=name, grid=(T // blk,), in_specs=[pl.BlockSpec((blk, W), lambda i: (i, 0)), tr, tr],
        out_specs=pl.BlockSpec((blk, 3 * W), lambda i: (i, 0)),
        out_shape=jax.ShapeDtypeStruct((T, 3 * W), BF16), compiler_params=_params("parallel"),
    )(dq, dkt, dvt)


def _local_step(x, target, norms, sgu, comm):
    T, D = x.shape
    A = D // 2
    pre_mix, post_mix, pre_ffn, post_ffn = norms
    ln_g, ln_b, w_s, b_s = sgu
    b_t = b_s.T
    scale = HEAD_DIM ** -0.5

    def vec(p, layer):
        return comm.tie(p[layer:layer + 1])

    h0 = _prenorm(x, vec(pre_mix, 0), "prenorm0")
    comm.arrive("ab_in", after=h0)
    comm.land("ab_in")
    z = _matmul(h0, comm.weight("ab_in"), mode="nn", name="ab_in_fwd")
    comm.arrive("ab_out", after=z)
    a_out = _sgu_fwd(z, ln_g, ln_b, w_s, b_t, "sgu_fwd")
    branch = [_dilated_fwd(z, d, A, f"dilated_fwd_{d}") for _, d in DILATED_PAIRS]
    cat, o_dil, lse_dil = _dilated_merge(a_out, [b[0] for b in branch], [b[1] for b in branch], "dilated_merge")
    comm.land("ab_out", after=o_dil)
    y0 = _matmul(cat, comm.weight("ab_out"), mode="nn", name="ab_out_fwd")
    comm.arrive("ffn0", after=y0)
    x1, h1 = _postnorm_prenorm(x, y0, vec(post_mix, 0), vec(pre_ffn, 0), "norm_mix0")
    comm.land("ffn0", after=h1)

    def relu2(acc, j):
        r = jnp.maximum(acc, 0.0)
        return r * r

    f0 = _matmul(h1, comm.weight("w1_0"), mode="nn", name="ffn0_w1_fwd", out_dtype=BF16, epi=relu2)
    y1 = _matmul(f0, comm.weight("w2_0"), mode="nn", name="ffn0_w2_fwd")
    comm.arrive("rest", after=y1)
    x2, h2 = _postnorm_prenorm(x1, y1, vec(post_ffn, 0), vec(pre_mix, 1), "norm_ffn0")
    comm.land("rest", after=h2)

    tn_qkv = _tile(D, 1024)
    nq = D // tn_qkv

    def scale_q(acc, j):
        return jnp.where(j < nq, acc * scale, acc)

    qkv = _matmul(h2, comm.weight("sb_in"), mode="nn", name="sb_in_fwd", out_dtype=BF16, tn=tn_qkv, epi=scale_q)
    o_sb, ltot = _sb_fwd(qkv, D, "sb_fwd")
    y2 = _matmul(o_sb, comm.weight("sb_out"), mode="nn", name="sb_out_fwd")
    x3, h3 = _postnorm_prenorm(x2, y2, vec(post_mix, 1), vec(pre_ffn, 1), "norm_mix1")
    f1 = _matmul(h3, comm.weight("w1_1"), mode="nn", name="ffn1_w1_fwd", out_dtype=BF16, epi=relu2)
    y3 = _matmul(f1, comm.weight("w2_1"), mode="nn", name="ffn1_w2_fwd")
    loss_tile, dx4, dy3, dg_post_ffn1 = _postnorm_loss(x3, y3, vec(post_ffn, 1), target, "norm_loss")
    loss = loss_tile[0, 0]

    def relu2_bwd(acc, j, f):
        return acc * (2.0 * jnp.sqrt(f.astype(F32)))

    def ffn_bwd(dy, h, f, layer):
        g_w2 = _matmul(f, dy, mode="tn", name=f"ffn{layer}_w2_wgrad", out_dtype=BF16)
        da = _matmul(dy, comm.weight(f"w2_{layer}"), mode="nt", name=f"ffn{layer}_w2_dgrad", out_dtype=BF16,
                     epi=relu2_bwd, extras=(f,))
        g_w1 = _matmul(h, da, mode="tn", name=f"ffn{layer}_w1_wgrad", out_dtype=BF16)
        comm.reduce(f"ffn{layer}", {f"w2_{layer}": g_w2, f"w1_{layer}": g_w1})
        return _matmul(da, comm.weight(f"w1_{layer}"), mode="nt", name=f"ffn{layer}_w1_dgrad", after=comm.started())

    dh3 = ffn_bwd(dy3, h3, f1, 1)
    dx3, dg_pre_ffn1, dy2, dg_post_mix1 = _norm_bwd_pair(dx4, dh3, x3, vec(pre_ffn, 1), y2, vec(post_mix, 1),
                                                         "ffn1_sb_norm_bwd")
    g_sb_out = _matmul(o_sb, dy2, mode="tn", name="sb_out_wgrad", out_dtype=BF16)
    do_sb = _matmul(dy2, comm.weight("sb_out"), mode="nt", name="sb_out_dgrad", out_dtype=BF16)
    dq, dk, dv = _sb_bwd(qkv, do_sb, ltot, D, "sb_bwd")
    dqkv = _sb_pack(dq, dk, dv, "sb_pack")
    g_sb_in = _matmul(h2, dqkv, mode="tn", name="sb_in_wgrad", out_dtype=BF16)
    comm.reduce("sb", {"sb_out": g_sb_out, "sb_in": g_sb_in})
    dh2 = _matmul(dqkv, comm.weight("sb_in"), mode="nt", name="sb_in_dgrad", after=comm.started())
    dx2, dg_pre_mix1, dy1, dg_post_ffn0 = _norm_bwd_pair(dx3, dh2, x2, vec(pre_mix, 1), y1, vec(post_ffn, 0),
                                                         "sb_ffn0_norm_bwd")
    dh1 = ffn_bwd(dy1, h1, f0, 0)
    dx1, dg_pre_ffn0, dy0, dg_post_mix0 = _norm_bwd_pair(dx2, dh1, x1, vec(pre_ffn, 0), y0, vec(post_mix, 0),
                                                         "ffn0_ab_norm_bwd")
    g_ab_out = _matmul(cat, dy0, mode="tn", name="ab_out_wgrad", out_dtype=BF16)
    comm.reduce("ab_out", {"ab_out": g_ab_out})
    dcat = _matmul(dy0, comm.weight("ab_out"), mode="nt", name="ab_out_dgrad", after=comm.started())
    duv, d_ln_g, d_ln_b, d_w_s, d_b_t = _sgu_bwd(z, dcat, ln_g, ln_b, w_s, b_t, "sgu_bwd")
    delta = _dilated_delta(dcat, o_dil, "dilated_delta")
    dqs, dks, dvs = [], [], []
    for _, d in DILATED_PAIRS:
        dqs.append(_dilated_dq(z, dcat, lse_dil, delta, d, A, f"dilated_dq_{d}"))
        dk_b, dv_b = _dilated_dkv(z, dcat, lse_dil, delta, d, A, f"dilated_dkv_{d}")
        dks.append(dk_b)
        dvs.append(dv_b)
    dz = _dilated_combine(duv, dqs, dks, dvs, "dilated_combine")
    g_ab_in = _matmul(h0, dz, mode="tn", name="ab_in_wgrad", out_dtype=BF16)
    comm.reduce("ab_in", {"ab_in": g_ab_in})
    dh0 = _matmul(dz, comm.weight("ab_in"), mode="nt", name="ab_in_dgrad", after=comm.started())
    dx0, dg_pre_mix0 = _prenorm_bwd(dx1, dh0, x, vec(pre_mix, 0), "ab_prenorm_bwd")

    small = {
        "pre_mix": jnp.concatenate([dg_pre_mix0, dg_pre_mix1], axis=0),
        "post_mix": jnp.concatenate([dg_post_mix0, dg_post_mix1], axis=0),
        "pre_ffn": jnp.concatenate([dg_pre_ffn0, dg_pre_ffn1], axis=0),
        "post_ffn": jnp.concatenate([dg_post_ffn0, dg_post_ffn1], axis=0),
        "ln_g": d_ln_g, "ln_b": d_ln_b, "w_s": d_w_s, "b_s": d_b_t.T,
    }
    return loss, dx0, small


MESH_ID = pl.DeviceIdType.MESH
ANY = pl.BlockSpec(memory_space=pl.ANY)


def _coords():
    return lax.axis_index("x"), lax.axis_index("y"), lax.axis_index("c")


def _shard_of(ref, kind, p):
    if kind == "col":
        n = ref.shape[1] // N_DEV
        return ref.at[:, pl.ds(pl.multiple_of(p * n, 128), n)]
    r = ref.shape[0] // N_DEV
    return ref.at[pl.ds(pl.multiple_of(p * r, 16), r), :]


def _full_shape(shard, kind):
    if kind == "col":
        return (shard.shape[0], shard.shape[1] * N_DEV)
    return (shard.shape[0] * N_DEV, shard.shape[1])


def _place(shard, kind, block, name):
    rows, cols = shard.shape
    tr = _tile(rows, 512)

    def body(b_ref, s_ref, o_ref):
        o_ref[...] = s_ref[...].astype(BF16)

    if kind == "col":
        out = pl.BlockSpec((tr, cols), lambda i, b_ref: (i, b_ref[0]))
    else:
        out = pl.BlockSpec((tr, cols), lambda i, b_ref: (b_ref[0] * (rows // tr) + i, 0))
    return pl.pallas_call(
        body, name=name,
        grid_spec=pltpu.PrefetchScalarGridSpec(
            num_scalar_prefetch=1, grid=(rows // tr,),
            in_specs=[pl.BlockSpec((tr, cols), lambda i, b_ref: (i, 0))], out_specs=out),
        out_shape=jax.ShapeDtypeStruct(_full_shape(shard, kind), BF16),
        compiler_params=_params("parallel"),
    )(block, shard)


HBM = pl.BlockSpec(memory_space=pltpu.HBM)
SEM = pl.BlockSpec(memory_space=pltpu.SEMAPHORE)
FLOWS = pltpu.SideEffectType.DATAFLOW_SIDE_EFFECTING


def _in_hbm(a):
    return pltpu.with_memory_space_constraint(a, pltpu.HBM)


def _hbm_like(bufs):
    return [pltpu.HBM(b.shape, b.dtype) for b in bufs]


def _copies_start(name, bufs, plan, n, after):
    nb = len(bufs)

    def body(*refs):
        send_sems, recv_sems, token = refs[nb + 1], refs[nb + 2], refs[-1]
        for cp in plan(refs[:nb], send_sems, recv_sems):
            cp.start()
        token[...] = jnp.zeros_like(token)

    out = pl.pallas_call(
        body, name=name, in_specs=[HBM] * nb + [ANY],
        out_specs=[SEM, SEM] + [HBM] * nb + [pl.BlockSpec(memory_space=pltpu.VMEM)],
        out_shape=[pltpu.SemaphoreType.DMA((n,)), pltpu.SemaphoreType.DMA((n,))] + _hbm_like(bufs)
        + [jax.ShapeDtypeStruct((8, 128), F32)],
        input_output_aliases={i: 2 + i for i in range(nb)},
        compiler_params=pltpu.CompilerParams(has_side_effects=FLOWS),
    )(*[_in_hbm(b) for b in bufs], after)
    return (out[0], out[1]), list(out[2:2 + nb]), out[-1]


def _copies_wait(name, bufs, sems, after, plan):
    nb = len(bufs)

    def body(*refs):
        for cp in plan(refs[:nb], refs[nb], refs[nb + 1]):
            cp.wait_send()
            cp.wait_recv()

    out = pl.pallas_call(
        body, name=name, in_specs=[HBM] * nb + [SEM, SEM, ANY], out_specs=[HBM] * nb,
        out_shape=_hbm_like(bufs), input_output_aliases={i: i for i in range(nb)},
        compiler_params=pltpu.CompilerParams(has_side_effects=FLOWS),
    )(*bufs, *sems, after)
    return list(out)


def _copies_wait_start(name, bufs, sems, after, plan, next_plan, n_next):
    nb = len(bufs)

    def body(*refs):
        ins = refs[:nb]
        for cp in plan(ins, refs[nb], refs[nb + 1]):
            cp.wait_send()
            cp.wait_recv()
        send_sems, recv_sems, token = refs[nb + 3], refs[nb + 4], refs[-1]
        for cp in next_plan(ins, send_sems, recv_sems):
            cp.start()
        token[...] = jnp.zeros_like(token)

    out = pl.pallas_call(
        body, name=name, in_specs=[HBM] * nb + [SEM, SEM, ANY],
        out_specs=[SEM, SEM] + [HBM] * nb + [pl.BlockSpec(memory_space=pltpu.VMEM)],
        out_shape=[pltpu.SemaphoreType.DMA((n_next,)), pltpu.SemaphoreType.DMA((n_next,))] + _hbm_like(bufs)
        + [jax.ShapeDtypeStruct((8, 128), F32)],
        input_output_aliases={i: 2 + i for i in range(nb)},
        compiler_params=pltpu.CompilerParams(has_side_effects=FLOWS),
    )(*bufs, *sems, after)
    return (out[0], out[1]), list(out[2:2 + nb]), out[-1]


def _gather_plans(kinds):
    nt = len(kinds)

    def slot(refs, t, px, py, pc):
        return _shard_of(refs[t], kinds[t], 4 * px + 2 * py + pc)

    def to_chips(refs, send_sems, recv_sems):
        x, y, c = _coords()
        peers = [(x, y, 1 - c), (1 - x, y, c), (x, 1 - y, c), (1 - x, 1 - y, c)]
        return [pltpu.make_async_remote_copy(
            src_ref=slot(refs, t, x, y, c), dst_ref=slot(refs, t, x, y, c), send_sem=send_sems.at[4 * t + k],
            recv_sem=recv_sems.at[4 * t + k], device_id=peer, device_id_type=MESH_ID)
            for t in range(nt) for k, peer in enumerate(peers)]

    def to_sibling(refs, send_sems, recv_sems):
        x, y, c = _coords()
        chips = [(1 - x, y), (x, 1 - y), (1 - x, 1 - y)]
        return [pltpu.make_async_remote_copy(
            src_ref=slot(refs, t, *chip, c), dst_ref=slot(refs, t, *chip, c), send_sem=send_sems.at[3 * t + j],
            recv_sem=recv_sems.at[3 * t + j], device_id=(x, y, 1 - c), device_id_type=MESH_ID)
            for t in range(nt) for j, chip in enumerate(chips)]

    return to_chips, to_sibling


def _shard_shape(full, kind):
    if kind == "col":
        return (full.shape[0], full.shape[1] // N_DEV)
    return (full.shape[0] // N_DEV, full.shape[1])


def _scatter_plan(kinds):
    nt = len(kinds)

    def plan(refs, send_sems, recv_sems):
        x, y, c = _coords()
        copies = []
        for t in range(nt):
            for k in range(1, N_DEV):
                px = 1 - x if (k >> 2) & 1 else x
                py = 1 - y if (k >> 1) & 1 else y
                pc = 1 - c if k & 1 else c
                copies.append(pltpu.make_async_remote_copy(
                    src_ref=_shard_of(refs[t], kinds[t], 4 * px + 2 * py + pc),
                    dst_ref=refs[nt + t].at[4 * x + 2 * y + c],
                    send_sem=send_sems.at[7 * t + k - 1], recv_sem=recv_sems.at[7 * t + k - 1],
                    device_id=(px, py, pc), device_id_type=MESH_ID))
        return copies
    return plan


def _partial_specs(full, kind, tr):
    rows, cols = _shard_shape(full, kind)
    steps = rows // tr
    if kind == "col":
        own = pl.BlockSpec((tr, cols), lambda i, w: (i, w[0]))
    else:
        own = pl.BlockSpec((tr, cols), lambda i, w: (w[0] * steps + i, 0))
    return [own] + [pl.BlockSpec((None, tr, cols), lambda i, w, k=k: (w[k], i, 0)) for k in range(1, N_DEV)]


def _all_reduce_small(vec, after):
    R = vec.shape[0]

    def body(v_ref, after_ref, o_ref, recv_ref, send_sems, recv_sems):
        x, y, c = _coords()
        me = 4 * x + 2 * y + c
        recv_ref[me] = v_ref[...]
        copies = []
        for k in range(1, N_DEV):
            bx, by, bc = (k >> 2) & 1, (k >> 1) & 1, k & 1
            peer = (1 - x if bx else x, 1 - y if by else y, 1 - c if bc else c)
            copies.append(pltpu.make_async_remote_copy(
                src_ref=v_ref, dst_ref=recv_ref.at[me],
                send_sem=send_sems.at[k - 1], recv_sem=recv_sems.at[k - 1],
                device_id=peer, device_id_type=MESH_ID))
        for cp in copies:
            cp.start()
        for cp in copies:
            cp.wait()
        total = recv_ref[0]
        for p in range(1, N_DEV):
            total = total + recv_ref[p]
        o_ref[...] = total

    return pl.pallas_call(
        body, name="all_reduce_small",
        in_specs=[pl.BlockSpec(memory_space=pltpu.VMEM), ANY], out_specs=pl.BlockSpec(memory_space=pltpu.VMEM),
        out_shape=jax.ShapeDtypeStruct((R, 128), F32),
        scratch_shapes=[pltpu.VMEM((N_DEV, R, 128), F32), pltpu.SemaphoreType.DMA((N_DEV - 1,)),
                        pltpu.SemaphoreType.DMA((N_DEV - 1,))],
        compiler_params=pltpu.CompilerParams(vmem_limit_bytes=VMEM_LIMIT),
    )(vec, after)


def _adamw_math(w, g, m, v):
    m = ADAM_B1 * m + (1.0 - ADAM_B1) * g
    v = ADAM_B2 * v + (1.0 - ADAM_B2) * (g * g)
    m_hat = m / (1.0 - ADAM_B1 ** ADAM_STEP)
    v_hat = v / (1.0 - ADAM_B2 ** ADAM_STEP)
    delta = -ADAM_LR * (m_hat / (jnp.sqrt(v_hat) + ADAM_EPS) + ADAM_WD * w)
    return delta, m, v


def _adamw(w, grads, kind, where, m, v, name):
    layers, rows, cols = w.shape
    tr = _tile(rows, 128)
    out = None
    for layer, (grad, land) in enumerate(grads):
        def body(w_ref, *refs):
            parts, (x_ref, m_ref, v_ref) = refs[:N_DEV], refs[N_DEV:N_DEV + 3]
            g_ref, d_ref, mo_ref, vo_ref = refs[-4:]
            g = parts[0][...].astype(F32)
            for p_ref in parts[1:]:
                g = g + p_ref[...].astype(F32)
            g_ref[...] = g
            d_ref[...], mo_ref[...], vo_ref[...] = _adamw_math(x_ref[...], g, m_ref[...], v_ref[...])

        blk = pl.BlockSpec((None, tr, cols), lambda i, w_, layer=layer: (layer, i, 0))
        earlier = [] if out is None else list(out)
        out = pl.pallas_call(
            body, name=f"{name}_{layer}",
            grid_spec=pltpu.PrefetchScalarGridSpec(
                num_scalar_prefetch=1, grid=(rows // tr,),
                in_specs=_partial_specs(grad, kind, tr) + [blk] * 3 + [ANY] * len(earlier),
                out_specs=[blk] * 4),
            out_shape=[jax.ShapeDtypeStruct((layers, rows, cols), F32)] * 4,
            input_output_aliases={N_DEV + 4 + k: k for k in range(len(earlier))},
            compiler_params=_params("parallel"),
        )(where, grad, *[land] * (N_DEV - 1), w, m, v, *earlier)
    return out


def _adamw_small(w, g, m, v):
    def body(w_ref, g_ref, m_ref, v_ref, d_ref, mo_ref, vo_ref):
        d_ref[...], mo_ref[...], vo_ref[...] = _adamw_math(w_ref[...], g_ref[...], m_ref[...], v_ref[...])

    whole = pl.BlockSpec(memory_space=pltpu.VMEM)
    return pl.pallas_call(
        body, name="adamw_small", in_specs=[whole] * 4, out_specs=[whole] * 3,
        out_shape=[jax.ShapeDtypeStruct(w.shape, F32)] * 3,
        compiler_params=pltpu.CompilerParams(vmem_limit_bytes=VMEM_LIMIT),
    )(w, g, m, v)


def _pack(arrays):
    rows = []
    for a in arrays:
        flat = a.reshape(-1)
        pad = (-flat.shape[0]) % 1024
        rows.append(jnp.pad(flat, (0, pad)).reshape(-1, 128))
    return jnp.concatenate(rows, axis=0)


def _unpack(packed, like):
    out, r = [], 0
    for a in like:
        n = math.prod(a.shape)
        nr = (n + 1023) // 1024 * 8
        out.append(packed[r:r + nr].reshape(-1)[:n].reshape(a.shape))
        r += nr
    return out


KIND = {"ab_in": "col", "ab_out": "row", "sb_in": "col", "sb_out": "row",
        "w1_0": "col", "w1_1": "col", "w2_0": "row", "w2_1": "row"}
GATHERS = {"ab_in": ("ab_in",), "ab_out": ("ab_out",), "ffn0": ("w1_0", "w2_0"),
           "rest": ("sb_in", "sb_out", "w1_1", "w2_1")}


class _Exchange:
    def __init__(self, shards):
        x, y, c = _coords()
        me = (4 * x + 2 * y + c).astype(jnp.int32)
        self.where = jnp.stack([jnp.bitwise_xor(me, k) for k in range(N_DEV)])
        block = me.reshape(1)
        self.full = {n: _place(s, KIND[n], block, f"place_{n}") for n, s in shards.items()}
        self.tokens = []
        self.gathers = {}
        self.scatters = {}
        self.settled = {}
        after = block
        for key, group in GATHERS.items():
            to_chips, to_sibling = _gather_plans([KIND[n] for n in group])
            bufs = [self.full[n] for n in group]
            sems, bufs, after = _copies_start(f"gather_start_{key}", bufs, to_chips, 4 * len(group), after)
            self.tokens.append(after)
            self.gathers[key] = (group, sems, bufs, to_chips, to_sibling)

    def tie(self, small):
        for token in self.tokens:
            small = small + token[0:1, 0:1]
        self.tokens = []
        return small

    def started(self):
        return tuple(self.tokens)

    def weight(self, name):
        return self.full[name]

    def arrive(self, key, after):
        group, sems, bufs, to_chips, to_sibling = self.gathers[key]
        sems, bufs, token = _copies_wait_start(f"gather_pass_{key}", bufs, sems, after, to_chips, to_sibling,
                                               3 * len(group))
        self.tokens.append(token)
        self.gathers[key] = (group, sems, bufs, token, to_sibling)

    def land(self, key, after=None):
        group, sems, bufs, token, to_sibling = self.gathers.pop(key)
        after = token if after is None else after
        self.full.update(zip(group, _copies_wait(f"gather_done_{key}", bufs, sems, after, to_sibling)))

    def reduce(self, key, grads):
        names = list(grads)
        kinds = [KIND[n] for n in names]
        full = [grads[n] for n in names]
        lands = [lax.empty((N_DEV,) + _shard_shape(g, k), BF16) for g, k in zip(full, kinds)]
        plan = _scatter_plan(kinds)
        sems, bufs, token = _copies_start(f"scatter_start_{key}", full + lands, plan, (N_DEV - 1) * len(names),
                                          full[-1])
        self.tokens.append(token)
        self.scatters[key] = (names, sems, bufs, plan)

    def settle(self, keys, after):
        for key in keys:
            names, sems, bufs, plan = self.scatters.pop(key)
            bufs = _copies_wait(f"scatter_done_{key}", bufs, sems, after, plan)
            self.settled.update({n: t for n, *t in zip(names, bufs[:len(names)], bufs[len(names):])})
        return self.settled


SMALL = ("norm_pre_mix", "norm_post_mix", "norm_pre_ffn", "norm_post_ffn", "sgu_ln_g", "sgu_ln_b", "sgu_w", "sgu_b")
ORDER = ("norm_pre_mix", "norm_post_mix", "norm_pre_ffn", "norm_post_ffn", "ab_w_in", "sgu_ln_g", "sgu_ln_b", "sgu_w",
         "sgu_b", "ab_w_out", "sb_w_in", "sb_w_out", "ffn_w1", "ffn_w2")


def kernel(x, norm_pre_mix, norm_post_mix, norm_pre_ffn, norm_post_ffn, ab_w_in, sgu_ln_g, sgu_ln_b, sgu_w, sgu_b, ab_w_out, sb_w_in, sb_w_out, ffn_w1, ffn_w2, loss_target, m_norm_pre_mix, m_norm_post_mix, m_norm_pre_ffn, m_norm_post_ffn, m_ab_w_in, m_sgu_ln_g, m_sgu_ln_b, m_sgu_w, m_sgu_b, m_ab_w_out, m_sb_w_in, m_sb_w_out, m_ffn_w1, m_ffn_w2, v_norm_pre_mix, v_norm_post_mix, v_norm_pre_ffn, v_norm_post_ffn, v_ab_w_in, v_sgu_ln_g, v_sgu_ln_b, v_sgu_w, v_sgu_b, v_ab_w_out, v_sb_w_in, v_sb_w_out, v_ffn_w1, v_ffn_w2):
    W = dict(norm_pre_mix=norm_pre_mix, norm_post_mix=norm_post_mix, norm_pre_ffn=norm_pre_ffn,
             norm_post_ffn=norm_post_ffn, ab_w_in=ab_w_in, sgu_ln_g=sgu_ln_g, sgu_ln_b=sgu_ln_b, sgu_w=sgu_w,
             sgu_b=sgu_b, ab_w_out=ab_w_out, sb_w_in=sb_w_in, sb_w_out=sb_w_out, ffn_w1=ffn_w1, ffn_w2=ffn_w2)
    M = dict(norm_pre_mix=m_norm_pre_mix, norm_post_mix=m_norm_post_mix, norm_pre_ffn=m_norm_pre_ffn,
             norm_post_ffn=m_norm_post_ffn, ab_w_in=m_ab_w_in, sgu_ln_g=m_sgu_ln_g, sgu_ln_b=m_sgu_ln_b,
             sgu_w=m_sgu_w, sgu_b=m_sgu_b, ab_w_out=m_ab_w_out, sb_w_in=m_sb_w_in, sb_w_out=m_sb_w_out,
             ffn_w1=m_ffn_w1, ffn_w2=m_ffn_w2)
    V = dict(norm_pre_mix=v_norm_pre_mix, norm_post_mix=v_norm_post_mix, norm_pre_ffn=v_norm_pre_ffn,
             norm_post_ffn=v_norm_post_ffn, ab_w_in=v_ab_w_in, sgu_ln_g=v_sgu_ln_g, sgu_ln_b=v_sgu_ln_b,
             sgu_w=v_sgu_w, sgu_b=v_sgu_b, ab_w_out=v_ab_w_out, sb_w_in=v_sb_w_in, sb_w_out=v_sb_w_out,
             ffn_w1=v_ffn_w1, ffn_w2=v_ffn_w2)

    shards = {"ab_in": ab_w_in[0], "ab_out": ab_w_out[0], "w1_0": ffn_w1[0], "w2_0": ffn_w2[0],
              "sb_in": sb_w_in[0], "sb_out": sb_w_out[0], "w1_1": ffn_w1[1], "w2_1": ffn_w2[1]}
    comm = _Exchange(shards)
    norms = (norm_pre_mix, norm_post_mix, norm_pre_ffn, norm_post_ffn)
    sgu = (sgu_ln_g, sgu_ln_b, sgu_w[0], sgu_b[0])
    loss, dx, small = _local_step(x[0], loss_target[0], norms, sgu, comm)
    loss = lax.psum(loss, MESH_AXES)

    out = {}

    def update(name, layers, landed):
        out[name] = _adamw(W[name], [landed[n] for n in layers], KIND[layers[0]], comm.where, M[name], V[name],
                           f"adamw_{name}")

    landed = comm.settle(("ffn1", "sb", "ffn0"), after=dx)
    for name, layers in (("sb_w_in", ["sb_in"]), ("sb_w_out", ["sb_out"]), ("ffn_w1", ["w1_0", "w1_1"]),
                         ("ffn_w2", ["w2_0", "w2_1"])):
        update(name, layers, landed)
    small_g = [small["pre_mix"], small["post_mix"], small["pre_ffn"], small["post_ffn"], small["ln_g"],
               small["ln_b"], small["w_s"][None], small["b_s"][None]]
    g_small = _all_reduce_small(_pack(small_g), out["ffn_w2"][3])
    landed = comm.settle(("ab_out", "ab_in"), after=g_small)
    update("ab_w_out", ["ab_out"], landed)
    update("ab_w_in", ["ab_in"], landed)
    res = _adamw_small(_pack([W[n] for n in SMALL]), g_small, _pack([M[n] for n in SMALL]),
                       _pack([V[n] for n in SMALL]))
    like = [W[n] for n in SMALL]
    for n, *vals in zip(SMALL, *[_unpack(r, like) for r in [g_small] + list(res)]):
        out[n] = vals

    return (loss, dx[None], *[out[n][0] for n in ORDER], *[out[n][1] for n in ORDER],
            *[out[n][2] for n in ORDER], *[out[n][3] for n in ORDER])
```

```python
import functools
import math

import jax
import jax.numpy as jnp
from jax import lax
from jax.experimental import pallas as pl
from jax.experimental.pallas import tpu as pltpu

F32 = jnp.float32
BF16 = jnp.bfloat16

HEAD_DIM = 128
CHUNK = 128
ATT_BLOCK = 128
DILATED_PAIRS = ((128, 1), (512, 4), (2048, 16))
RMS_EPS = 1e-6
LN_EPS = 1e-5
ADAM_LR = 0.001
ADAM_B1 = 0.9
ADAM_B2 = 0.999
ADAM_EPS = 1e-08
ADAM_WD = 0.01
ADAM_STEP = 10
N_DEV = 8
MESH_AXES = ("x", "y", "c")
MASKED = -1e30

V7X_VMEM_BYTES = 64 * 1024 * 1024
VMEM_LIMIT = V7X_VMEM_BYTES - 8 * 1024 * 1024

NN = (((1,), (0,)), ((), ()))
NT = (((1,), (1,)), ((), ()))
TN = (((0,), (0,)), ((), ()))


def _params(*sem):
    return pltpu.CompilerParams(dimension_semantics=sem, vmem_limit_bytes=VMEM_LIMIT)


def _dot(a, b, dims=NN):
    return lax.dot_general(a, b, dims, preferred_element_type=F32)


def _tile(n, preferred):
    if n <= preferred:
        return n
    t = preferred - preferred % 128
    while n % t:
        t -= 128
    assert t > 0, (n, preferred)
    return t


def _matmul(a, b, *, mode, name, out_dtype=F32, tm=1024, tn=1024, tk=2048, epi=None, extras=(), after=()):
    if mode == "nn":
        (M, K), N = a.shape, b.shape[1]
    elif mode == "nt":
        (M, K), N = a.shape, b.shape[0]
    else:
        (K, M), N = a.shape, b.shape[1]
    tm, tn, tk = _tile(M, tm), _tile(N, tn), _tile(K, tk)
    nk = K // tk
    if mode == "tn":
        a_spec = pl.BlockSpec((tk, tm), lambda i, j, k: (k, i))
    else:
        a_spec = pl.BlockSpec((tm, tk), lambda i, j, k: (i, k))
    if mode == "nt":
        b_spec = pl.BlockSpec((tn, tk), lambda i, j, k: (j, k))
    else:
        b_spec = pl.BlockSpec((tk, tn), lambda i, j, k: (k, j))
    o_spec = pl.BlockSpec((tm, tn), lambda i, j, k: (i, j))
    dims = {"nn": NN, "nt": NT, "tn": TN}[mode]
    n_extra = len(extras)
    n_in = n_extra + len(after)

    def finish(acc, refs):
        j = pl.program_id(1)
        if epi is None:
            return acc
        return epi(acc, j, *[r[...] for r in refs])

    if nk == 1:
        def body(a_ref, b_ref, *rest):
            o_ref = rest[n_in]
            acc = _dot(a_ref[...], b_ref[...], dims)
            o_ref[...] = finish(acc, rest[:n_extra]).astype(o_ref.dtype)
        scratch = []
    else:
        def body(a_ref, b_ref, *rest):
            o_ref, acc_ref = rest[n_in], rest[n_in + 1]
            k = pl.program_id(2)

            @pl.when(k == 0)
            def _():
                acc_ref[...] = jnp.zeros_like(acc_ref)

            acc_ref[...] += _dot(a_ref[...], b_ref[...], dims)

            @pl.when(k == nk - 1)
            def _():
                o_ref[...] = finish(acc_ref[...], rest[:n_extra]).astype(o_ref.dtype)
        scratch = [pltpu.VMEM((tm, tn), F32)]

    return pl.pallas_call(
        body,
        name=name,
        grid=(M // tm, N // tn, nk),
        in_specs=[a_spec, b_spec] + [o_spec] * n_extra + [ANY] * len(after),
        out_specs=o_spec,
        out_shape=jax.ShapeDtypeStruct((M, N), out_dtype),
        scratch_shapes=scratch,
        compiler_params=_params("parallel", "parallel", "arbitrary"),
    )(a, b, *extras, *after)


ROWS = 256


def _rms(x):
    return lax.rsqrt(jnp.mean(x * x, axis=-1, keepdims=True) + RMS_EPS)


def _prenorm(x, g, name):
    T, D = x.shape

    def body(x_ref, g_ref, h_ref):
        xv = x_ref[...]
        h_ref[...] = (xv * _rms(xv) * g_ref[...]).astype(BF16)

    row = pl.BlockSpec((ROWS, D), lambda i: (i, 0))
    vec = pl.BlockSpec((1, D), lambda i: (0, 0))
    return pl.pallas_call(
        body, name=name, grid=(T // ROWS,), in_specs=[row, vec], out_specs=row,
        out_shape=jax.ShapeDtypeStruct((T, D), BF16), compiler_params=_params("parallel"),
    )(x, g)


def _postnorm_prenorm(x, y, g_post, g_pre, name):
    T, D = x.shape

    def body(x_ref, y_ref, gp_ref, gn_ref, xo_ref, h_ref):
        yv = y_ref[...]
        xn = x_ref[...] + yv * _rms(yv) * gp_ref[...]
        xo_ref[...] = xn
        h_ref[...] = (xn * _rms(xn) * gn_ref[...]).astype(BF16)

    row = pl.BlockSpec((ROWS, D), lambda i: (i, 0))
    vec = pl.BlockSpec((1, D), lambda i: (0, 0))
    return pl.pallas_call(
        body, name=name, grid=(T // ROWS,), in_specs=[row, row, vec, vec], out_specs=[row, row],
        out_shape=[jax.ShapeDtypeStruct((T, D), F32), jax.ShapeDtypeStruct((T, D), BF16)],
        compiler_params=_params("parallel"),
    )(x, y, g_post, g_pre)


def _postnorm_grads(dn, yh, r, g):
    gd = dn * g
    return r * (gd - yh * jnp.mean(yh * gd, axis=-1, keepdims=True)), dn * yh


def _postnorm_loss(x, y, g_post, target, name):
    T, D = x.shape

    def body(x_ref, y_ref, gp_ref, t_ref, loss_ref, dx_ref, dy_ref, dg_ref):
        @pl.when(pl.program_id(0) == 0)
        def _():
            loss_ref[...] = jnp.zeros_like(loss_ref)
            dg_ref[...] = jnp.zeros_like(dg_ref)

        yv = y_ref[...]
        r = _rms(yv)
        yh = yv * r
        err = x_ref[...] + yh * gp_ref[...] - t_ref[...]
        dx = err * (1.0 / D)
        dx_ref[...] = dx
        loss_ref[...] += 0.5 * jnp.sum(jnp.sum(err * err, axis=-1, keepdims=True) * (1.0 / D))
        dy, dg = _postnorm_grads(dx, yh, r, gp_ref[...])
        dy_ref[...] = dy.astype(BF16)
        dg_ref[...] += jnp.sum(dg, axis=0, keepdims=True)

    row = pl.BlockSpec((ROWS, D), lambda i: (i, 0))
    vec = pl.BlockSpec((1, D), lambda i: (0, 0))
    acc = pl.BlockSpec((8, 128), lambda i: (0, 0))
    return pl.pallas_call(
        body, name=name, grid=(T // ROWS,), in_specs=[row, row, vec, row], out_specs=[acc, row, row, vec],
        out_shape=[jax.ShapeDtypeStruct((8, 128), F32), jax.ShapeDtypeStruct((T, D), F32),
                   jax.ShapeDtypeStruct((T, D), BF16), jax.ShapeDtypeStruct((1, D), F32)],
        compiler_params=_params("arbitrary"),
    )(x, y, g_post, target)


def _norm_bwd_pair(dx_out, dh, x, g_pre, y_prev, g_post_prev, name):
    T, D = x.shape

    def body(dxo_ref, dh_ref, x_ref, g_ref, y_ref, gp_ref, dx_ref, dg_ref, dy_ref, dgp_ref):
        @pl.when(pl.program_id(0) == 0)
        def _():
            dg_ref[...] = jnp.zeros_like(dg_ref)
            dgp_ref[...] = jnp.zeros_like(dgp_ref)

        xv, dhv = x_ref[...], dh_ref[...]
        r = _rms(xv)
        xh = xv * r
        gd = dhv * g_ref[...]
        dx = dxo_ref[...] + r * (gd - xh * jnp.mean(xh * gd, axis=-1, keepdims=True))
        dx_ref[...] = dx
        dg_ref[...] += jnp.sum(dhv * xh, axis=0, keepdims=True)
        yv = y_ref[...]
        ry = _rms(yv)
        dy, dgp = _postnorm_grads(dx, yv * ry, ry, gp_ref[...])
        dy_ref[...] = dy.astype(BF16)
        dgp_ref[...] += jnp.sum(dgp, axis=0, keepdims=True)

    row = pl.BlockSpec((ROWS, D), lambda i: (i, 0))
    vec = pl.BlockSpec((1, D), lambda i: (0, 0))
    return pl.pallas_call(
        body, name=name, grid=(T // ROWS,), in_specs=[row, row, row, vec, row, vec],
        out_specs=[row, vec, row, vec],
        out_shape=[jax.ShapeDtypeStruct((T, D), F32), jax.ShapeDtypeStruct((1, D), F32),
                   jax.ShapeDtypeStruct((T, D), BF16), jax.ShapeDtypeStruct((1, D), F32)],
        compiler_params=_params("arbitrary"),
    )(dx_out, dh, x, g_pre, y_prev, g_post_prev)


def _prenorm_bwd(dx_out, dh, x, g_pre, name):
    T, D = x.shape

    def body(dxo_ref, dh_ref, x_ref, g_ref, dx_ref, dg_ref):
        @pl.when(pl.program_id(0) == 0)
        def _():
            dg_ref[...] = jnp.zeros_like(dg_ref)

        xv, dhv = x_ref[...], dh_ref[...]
        r = _rms(xv)
        xh = xv * r
        gd = dhv * g_ref[...]
        dx_ref[...] = dxo_ref[...] + r * (gd - xh * jnp.mean(xh * gd, axis=-1, keepdims=True))
        dg_ref[...] += jnp.sum(dhv * xh, axis=0, keepdims=True)

    row = pl.BlockSpec((ROWS, D), lambda i: (i, 0))
    vec = pl.BlockSpec((1, D), lambda i: (0, 0))
    return pl.pallas_call(
        body, name=name, grid=(T // ROWS,), in_specs=[row, row, row, vec], out_specs=[row, vec],
        out_shape=[jax.ShapeDtypeStruct((T, D), F32), jax.ShapeDtypeStruct((1, D), F32)],
        compiler_params=_params("arbitrary"),
    )(dx_out, dh, x, g_pre)


_INV_SQRT2 = 1.0 / math.sqrt(2.0)
_INV_SQRT2PI = 1.0 / math.sqrt(2.0 * math.pi)


def _gelu(x):
    return 0.5 * x * (1.0 + lax.erf(x * _INV_SQRT2))


def _gelu_grad(x):
    return 0.5 * (1.0 + lax.erf(x * _INV_SQRT2)) + x * jnp.exp(-0.5 * x * x) * _INV_SQRT2PI


def _layernorm_stats(x):
    mu = jnp.mean(x, axis=-1, keepdims=True)
    xc = x - mu
    rstd = lax.rsqrt(jnp.mean(xc * xc, axis=-1, keepdims=True) + LN_EPS)
    return xc * rstd, rstd


def _tril_mask():
    i = lax.broadcasted_iota(jnp.int32, (CHUNK, CHUNK), 0)
    j = lax.broadcasted_iota(jnp.int32, (CHUNK, CHUNK), 1)
    return j <= i


SGU_ROWS = 512


def _sgu_fwd(z, ln_g, ln_b, w_s, b_t, name):
    T = z.shape[0]
    A = ln_g.shape[1]
    G = A // 128
    rows = min(SGU_ROWS, T)

    def body(u_ref, v_ref, g_ref, b_ref, w_ref, bt_ref, o_ref):
        mask = _tril_mask()
        for c in range(rows // CHUNK):
            rs = pl.ds(c * CHUNK, CHUNK)
            xh, _ = _layernorm_stats(_gelu(v_ref[rs, :]))
            vn = (xh * g_ref[...] + b_ref[...]).astype(BF16)
            for g in range(G):
                cs = pl.ds(g * 128, 128)
                w = jnp.where(mask, w_ref[g], 0.0).astype(BF16)
                mixed = _dot(w, vn[:, g * 128:(g + 1) * 128]) + bt_ref[:, g:g + 1]
                o_ref[rs, cs] = (_gelu(u_ref[rs, cs]) * mixed).astype(BF16)

    return pl.pallas_call(
        body, name=name, grid=(T // rows,),
        in_specs=[
            pl.BlockSpec((rows, A), lambda i: (i, 0)),
            pl.BlockSpec((rows, A), lambda i: (i, 1)),
            pl.BlockSpec((1, A), lambda i: (0, 0)),
            pl.BlockSpec((1, A), lambda i: (0, 0)),
            pl.BlockSpec((G, CHUNK, CHUNK), lambda i: (0, 0, 0)),
            pl.BlockSpec((CHUNK, G), lambda i: (0, 0)),
        ],
        out_specs=pl.BlockSpec((rows, A), lambda i: (i, 0)),
        out_shape=jax.ShapeDtypeStruct((T, A), BF16),
        compiler_params=_params("parallel"),
    )(z, z, ln_g, ln_b, w_s, b_t)


def _sgu_bwd(z, dcat, ln_g, ln_b, w_s, b_t, name):
    T = z.shape[0]
    A = ln_g.shape[1]
    G = A // 128
    rows = min(SGU_ROWS, T)

    def body(u_ref, v_ref, da_ref, g_ref, b_ref, w_ref, bt_ref, dz_ref, dg_ref, db_ref, dw_ref, dbt_ref, dvn_ref):
        @pl.when(pl.program_id(0) == 0)
        def _():
            dg_ref[...] = jnp.zeros_like(dg_ref)
            db_ref[...] = jnp.zeros_like(db_ref)
            dw_ref[...] = jnp.zeros_like(dw_ref)
            dbt_ref[...] = jnp.zeros_like(dbt_ref)

        mask = _tril_mask()
        for c in range(rows // CHUNK):
            rs = pl.ds(c * CHUNK, CHUNK)
            vv = v_ref[rs, :]
            gv = _gelu(vv)
            xh, rstd = _layernorm_stats(gv)
            vn = (xh * g_ref[...] + b_ref[...]).astype(BF16)
            for g in range(G):
                cs = pl.ds(g * 128, 128)
                w = jnp.where(mask, w_ref[g], 0.0).astype(BF16)
                vg = vn[:, g * 128:(g + 1) * 128]
                mixed = _dot(w, vg) + bt_ref[:, g:g + 1]
                uu = u_ref[rs, cs]
                da = da_ref[rs, cs]
                dz_ref[rs, cs] = (da * mixed * _gelu_grad(uu)).astype(BF16)
                dm = da * _gelu(uu)
                dmb = dm.astype(BF16)
                dbt_ref[:, g:g + 1] += jnp.sum(dm, axis=1, keepdims=True)
                dw_ref[g] += jnp.where(mask, _dot(dmb, vg, NT), 0.0)
                dvn_ref[:, cs] = _dot(w, dmb, TN)
            dvn = dvn_ref[...]
            dg_ref[...] += jnp.sum(dvn * xh, axis=0, keepdims=True)
            db_ref[...] += jnp.sum(dvn, axis=0, keepdims=True)
            dxh = dvn * g_ref[...]
            dgv = rstd * (dxh - jnp.mean(dxh, axis=-1, keepdims=True)
                          - xh * jnp.mean(dxh * xh, axis=-1, keepdims=True))
            dz_ref[rs, pl.ds(A, A)] = (dgv * _gelu_grad(vv)).astype(BF16)

    vec = pl.BlockSpec((1, A), lambda i: (0, 0))
    wsp = pl.BlockSpec((G, CHUNK, CHUNK), lambda i: (0, 0, 0))
    bsp = pl.BlockSpec((CHUNK, G), lambda i: (0, 0))
    return pl.pallas_call(
        body, name=name, grid=(T // rows,),
        in_specs=[
            pl.BlockSpec((rows, A), lambda i: (i, 0)),
            pl.BlockSpec((rows, A), lambda i: (i, 1)),
            pl.BlockSpec((rows, A), lambda i: (i, 0)),
            vec, vec, wsp, bsp,
        ],
        out_specs=[pl.BlockSpec((rows, 2 * A), lambda i: (i, 0)), vec, vec, wsp, bsp],
        out_shape=[
            jax.ShapeDtypeStruct((T, 2 * A), BF16),
            jax.ShapeDtypeStruct((1, A), F32),
            jax.ShapeDtypeStruct((1, A), F32),
            jax.ShapeDtypeStruct((G, CHUNK, CHUNK), F32),
            jax.ShapeDtypeStruct((CHUNK, G), F32),
        ],
        scratch_shapes=[pltpu.VMEM((CHUNK, A), F32)],
        compiler_params=_params("arbitrary"),
    )(z, z, dcat, ln_g, ln_b, w_s, b_t)


def _alibi_row(B, d):
    H = B // HEAD_DIM
    slopes = [d * 2.0 ** (-8.0 * (h + 1.0) / H) for h in range(H)]
    return jnp.repeat(jnp.asarray(slopes, F32), HEAD_DIM)[None, :]


def _dil_scores(q, k, slope_d, valid, dist):
    s = _dot(q, k, NT) - slope_d * dist
    return jnp.where(valid, s, MASKED)


def _dil_layout(T, B, d):
    H = B // HEAD_DIM
    hp = H if d == 1 else 1
    return hp, H // hp, T // (d * ATT_BLOCK)


def _dil_rows(ref, r, d, cs):
    return ref[pl.ds(r, ATT_BLOCK, stride=d), cs] if d > 1 else ref[:, cs]


def _dil_put(ref, r, d, cs, val):
    if d > 1:
        ref[pl.ds(r, ATT_BLOCK, stride=d), cs] = val
    else:
        ref[:, cs] = val


def _dilated_fwd(z, d, B, name):
    T = z.shape[0]
    H = B // HEAD_DIM
    hp, groups, nb = _dil_layout(T, B, d)
    cw = hp * HEAD_DIM
    scale = HEAD_DIM ** -0.5
    blk = ATT_BLOCK

    def body(q_ref, kp_ref, kc_ref, vp_ref, vc_ref, sl_ref, o_ref, l_ref):
        n = pl.program_id(1)
        qi = lax.broadcasted_iota(jnp.int32, (blk, 2 * blk), 0)
        kj = lax.broadcasted_iota(jnp.int32, (blk, 2 * blk), 1)
        dist = qi + blk - kj
        valid = (dist >= 0) & (dist <= blk) & ((kj >= blk) | (n > 0))
        distf = dist.astype(F32)
        for r in range(d):
            for hh in range(hp):
                cs = pl.ds(hh * HEAD_DIM, HEAD_DIM)
                q = (_dil_rows(q_ref, r, d, cs) * scale).astype(BF16)
                k = jnp.concatenate([_dil_rows(kp_ref, r, d, cs), _dil_rows(kc_ref, r, d, cs)], axis=0).astype(BF16)
                v = jnp.concatenate([_dil_rows(vp_ref, r, d, cs), _dil_rows(vc_ref, r, d, cs)], axis=0).astype(BF16)
                s = _dil_scores(q, k, sl_ref[:, cs][:, :1], valid, distf)
                m = jnp.max(s, axis=-1, keepdims=True)
                p = jnp.exp(s - m)
                den = jnp.sum(p, axis=-1, keepdims=True)
                _dil_put(o_ref, r, d, cs, _dot(p.astype(BF16), v) / den)
                _dil_put(l_ref, r, d, cs, jnp.broadcast_to(m + jnp.log(den), (blk, HEAD_DIM)))

    def col(unit):
        return lambda g, n: (n, unit * groups + g)

    def col_prev(unit):
        return lambda g, n: (jnp.maximum(n - 1, 0), unit * groups + g)

    bs = (d * blk, cw)
    out = pl.BlockSpec(bs, lambda g, n: (n, g))
    return pl.pallas_call(
        body, name=name, grid=(groups, nb),
        in_specs=[pl.BlockSpec(bs, col(2)), pl.BlockSpec(bs, col_prev(3)), pl.BlockSpec(bs, col(3)),
                  pl.BlockSpec(bs, col_prev(4)), pl.BlockSpec(bs, col(4)), pl.BlockSpec((1, cw), lambda g, n: (0, g))],
        out_specs=[out, out],
        out_shape=[jax.ShapeDtypeStruct((T, B), F32)] * 2,
        compiler_params=_params("parallel", "parallel"),
    )(z, z, z, z, z, _alibi_row(B, d))


def _dilated_merge(a_out, os_, ls_, name):
    T, B = os_[0].shape
    A = a_out.shape[1]

    def body(a_ref, o1, o2, o3, l1, l2, l3, cat_ref, of_ref, lt_ref):
        a, b, c = l1[...], l2[...], l3[...]
        m = jnp.maximum(jnp.maximum(a, b), c)
        ea, eb, ec = jnp.exp(a - m), jnp.exp(b - m), jnp.exp(c - m)
        tot = ea + eb + ec
        o = (ea * o1[...] + eb * o2[...] + ec * o3[...]) / tot
        of_ref[...] = o
        cat_ref[:, pl.ds(0, A)] = a_ref[...]
        cat_ref[:, pl.ds(A, B)] = o.astype(BF16)
        lt_ref[...] = m + jnp.log(tot)

    row = pl.BlockSpec((ROWS, B), lambda i: (i, 0))
    return pl.pallas_call(
        body, name=name, grid=(T // ROWS,), in_specs=[pl.BlockSpec((ROWS, A), lambda i: (i, 0))] + [row] * 6,
        out_specs=[pl.BlockSpec((ROWS, A + B), lambda i: (i, 0)), row, row],
        out_shape=[jax.ShapeDtypeStruct((T, A + B), BF16), jax.ShapeDtypeStruct((T, B), F32),
                   jax.ShapeDtypeStruct((T, B), F32)],
        compiler_params=_params("parallel"),
    )(a_out, *os_, *ls_)


def _dilated_delta(dcat, o, name):
    T, B = o.shape
    H = B // HEAD_DIM

    def body(do_ref, o_ref, d_ref):
        for h in range(H):
            cs = pl.ds(h * HEAD_DIM, HEAD_DIM)
            dsum = jnp.sum(do_ref[:, cs] * o_ref[:, cs], axis=-1, keepdims=True)
            d_ref[:, cs] = jnp.broadcast_to(dsum, (ROWS, HEAD_DIM))

    row = pl.BlockSpec((ROWS, B), lambda i: (i, 0))
    return pl.pallas_call(
        body, name=name, grid=(T // ROWS,),
        in_specs=[pl.BlockSpec((ROWS, B), lambda i: (i, 1)), row], out_specs=row,
        out_shape=jax.ShapeDtypeStruct((T, B), F32), compiler_params=_params("parallel"),
    )(dcat, o)


def _dilated_dq(z, dcat, lse, delta, d, B, name):
    T = z.shape[0]
    hp, groups, nb = _dil_layout(T, B, d)
    cw = hp * HEAD_DIM
    scale = HEAD_DIM ** -0.5
    blk = ATT_BLOCK

    def body(q_ref, kp_ref, kc_ref, vp_ref, vc_ref, do_ref, l_ref, dl_ref, sl_ref, dq_ref):
        n = pl.program_id(1)
        qi = lax.broadcasted_iota(jnp.int32, (blk, 2 * blk), 0)
        kj = lax.broadcasted_iota(jnp.int32, (blk, 2 * blk), 1)
        dist = qi + blk - kj
        valid = (dist >= 0) & (dist <= blk) & ((kj >= blk) | (n > 0))
        distf = dist.astype(F32)
        for r in range(d):
            for hh in range(hp):
                cs = pl.ds(hh * HEAD_DIM, HEAD_DIM)
                q = (_dil_rows(q_ref, r, d, cs) * scale).astype(BF16)
                k = jnp.concatenate([_dil_rows(kp_ref, r, d, cs), _dil_rows(kc_ref, r, d, cs)], axis=0).astype(BF16)
                v = jnp.concatenate([_dil_rows(vp_ref, r, d, cs), _dil_rows(vc_ref, r, d, cs)], axis=0).astype(BF16)
                s = _dil_scores(q, k, sl_ref[:, cs][:, :1], valid, distf)
                p = jnp.exp(s - _dil_rows(l_ref, r, d, cs)[:, :1])
                dp = _dot(_dil_rows(do_ref, r, d, cs).astype(BF16), v, NT)
                ds = p * (dp - _dil_rows(dl_ref, r, d, cs)[:, :1])
                _dil_put(dq_ref, r, d, cs, _dot(ds.astype(BF16), k))

    def col(unit):
        return lambda g, n: (n, unit * groups + g)

    def col_prev(unit):
        return lambda g, n: (jnp.maximum(n - 1, 0), unit * groups + g)

    bs = (d * blk, cw)
    out = pl.BlockSpec(bs, lambda g, n: (n, g))
    return pl.pallas_call(
        body, name=name, grid=(groups, nb),
        in_specs=[pl.BlockSpec(bs, col(2)), pl.BlockSpec(bs, col_prev(3)), pl.BlockSpec(bs, col(3)),
                  pl.BlockSpec(bs, col_prev(4)), pl.BlockSpec(bs, col(4)), pl.BlockSpec(bs, col(1)), out, out,
                  pl.BlockSpec((1, cw), lambda g, n: (0, g))],
        out_specs=out,
        out_shape=jax.ShapeDtypeStruct((T, B), F32),
        compiler_params=_params("parallel", "parallel"),
    )(z, z, z, z, z, dcat, lse, delta, _alibi_row(B, d))


def _dilated_dkv(z, dcat, lse, delta, d, B, name):
    T = z.shape[0]
    hp, groups, nb = _dil_layout(T, B, d)
    cw = hp * HEAD_DIM
    scale = HEAD_DIM ** -0.5
    blk = ATT_BLOCK

    def body(k_ref, v_ref, qa_ref, qb_ref, doa_ref, dob_ref, la_ref, lb_ref, da_ref, db_ref, sl_ref, dk_ref, dv_ref):
        m = pl.program_id(1)
        qi = lax.broadcasted_iota(jnp.int32, (2 * blk, blk), 0)
        kj = lax.broadcasted_iota(jnp.int32, (2 * blk, blk), 1)
        dist = qi - kj
        valid = (dist >= 0) & (dist <= blk) & ((qi < blk) | (m + 1 < nb))
        distf = dist.astype(F32)
        for r in range(d):
            for hh in range(hp):
                cs = pl.ds(hh * HEAD_DIM, HEAD_DIM)

                def both(a_ref, b_ref):
                    return jnp.concatenate([_dil_rows(a_ref, r, d, cs), _dil_rows(b_ref, r, d, cs)], axis=0)

                q = (both(qa_ref, qb_ref) * scale).astype(BF16)
                do = both(doa_ref, dob_ref).astype(BF16)
                k = _dil_rows(k_ref, r, d, cs).astype(BF16)
                v = _dil_rows(v_ref, r, d, cs).astype(BF16)
                s = _dil_scores(q, k, sl_ref[:, cs][:, :1], valid, distf)
                p = jnp.exp(jnp.where(valid, s - both(la_ref, lb_ref)[:, :1], MASKED))
                _dil_put(dv_ref, r, d, cs, _dot(p.astype(BF16), do, TN))
                ds = p * (_dot(do, v, NT) - both(da_ref, db_ref)[:, :1])
                _dil_put(dk_ref, r, d, cs, _dot(ds.astype(BF16), q, TN))

    def col(unit):
        return lambda g, m: (m, unit * groups + g)

    def nxt(unit):
        return lambda g, m: (jnp.minimum(m + 1, nb - 1), unit * groups + g)

    bs = (d * blk, cw)
    out = pl.BlockSpec(bs, lambda g, m: (m, g))
    return pl.pallas_call(
        body, name=name, grid=(groups, nb),
        in_specs=[pl.BlockSpec(bs, col(3)), pl.BlockSpec(bs, col(4)),
                  pl.BlockSpec(bs, col(2)), pl.BlockSpec(bs, nxt(2)),
                  pl.BlockSpec(bs, col(1)), pl.BlockSpec(bs, nxt(1)),
                  out, pl.BlockSpec(bs, nxt(0)), out, pl.BlockSpec(bs, nxt(0)),
                  pl.BlockSpec((1, cw), lambda g, m: (0, g))],
        out_specs=[out, out],
        out_shape=[jax.ShapeDtypeStruct((T, B), F32)] * 2,
        compiler_params=_params("parallel", "parallel"),
    )(z, z, z, z, dcat, dcat, lse, lse, delta, delta, _alibi_row(B, d))


def _dilated_combine(duv, dqs, dks, dvs, name):
    T, B = dqs[0].shape
    A2 = duv.shape[1]
    scale = HEAD_DIM ** -0.5

    def body(uv, q1, q2, q3, k1, k2, k3, v1, v2, v3, o_ref):
        o_ref[:, pl.ds(0, A2)] = uv[...]
        o_ref[:, pl.ds(A2, B)] = ((q1[...] + q2[...] + q3[...]) * scale).astype(BF16)
        o_ref[:, pl.ds(A2 + B, B)] = (k1[...] + k2[...] + k3[...]).astype(BF16)
        o_ref[:, pl.ds(A2 + 2 * B, B)] = (v1[...] + v2[...] + v3[...]).astype(BF16)

    row = pl.BlockSpec((ROWS, B), lambda i: (i, 0))
    return pl.pallas_call(
        body, name=name, grid=(T // ROWS,), in_specs=[pl.BlockSpec((ROWS, A2), lambda i: (i, 0))] + [row] * 9,
        out_specs=pl.BlockSpec((ROWS, A2 + 3 * B), lambda i: (i, 0)),
        out_shape=jax.ShapeDtypeStruct((T, A2 + 3 * B), BF16), compiler_params=_params("parallel"),
    )(duv, *dqs, *dks, *dvs)


SB_QUERY_ROWS = 512
SB_KEYS = 2 * ATT_BLOCK


def _tri_and_ones(pred):
    rows = lax.broadcasted_iota(jnp.int32, (2 * ATT_BLOCK, 2 * ATT_BLOCK), 0) % ATT_BLOCK
    cols = lax.broadcasted_iota(jnp.int32, (2 * ATT_BLOCK, 2 * ATT_BLOCK), 1)
    return ((cols >= ATT_BLOCK) | pred(rows, cols)).astype(BF16)


def _running(x, tri):
    hi = x.astype(BF16)
    lo = (x - hi.astype(F32)).astype(BF16)
    return _dot(jnp.concatenate([hi, lo], axis=1), tri)


def _sb_mask(query_rows, s):
    rows = lax.broadcasted_iota(jnp.int32, (query_rows, SB_KEYS), 0)
    cols = lax.broadcasted_iota(jnp.int32, (query_rows, SB_KEYS), 1)
    return cols + s * SB_KEYS < rows


def _log_sigmoids(z):
    ls = jnp.minimum(z, 0.0) - jnp.log(1.0 + jnp.exp(-jnp.abs(z)))
    return ls, ls - z


def _sb_fwd(qkv, W, name):
    T = qkv.shape[0]
    H = W // HEAD_DIM
    blk = ATT_BLOCK
    qb = min(SB_QUERY_ROWS, T)
    per = qb // SB_KEYS

    def body(q_ref, k_ref, v_ref, o_ref, lt_ref, acc_ref):
        i = pl.program_id(1)
        q = q_ref[...]
        tri = _tri_and_ones(lambda r, c: r > c)
        lt_ref[...] = jnp.zeros_like(lt_ref)
        acc_ref[...] = jnp.zeros_like(acc_ref)

        def tile(j, mask):
            ks = pl.ds(pl.multiple_of(j * SB_KEYS, SB_KEYS), SB_KEYS)
            z = _dot(q, k_ref[ks, :], NT)
            ls, lm = _log_sigmoids(z)
            if mask is not None:
                lm = jnp.where(mask, lm, 0.0)
            later = lt_ref[...]
            second = _running(lm[:, blk:], tri)
            first = _running(lm[:, :blk], tri)
            after_first = later + second[:, blk:]
            a = jnp.exp(ls + jnp.concatenate([first[:, :blk] + after_first, second[:, :blk] + later], axis=1))
            if mask is not None:
                a = jnp.where(mask, a, 0.0)
            acc_ref[...] += _dot(a.astype(BF16), v_ref[ks, :])
            lt_ref[...] = after_first + first[:, blk:]

        for s in reversed(range(per)):
            tile(i * per + s, _sb_mask(qb, s))

        def step(jj, _):
            for s in range(per):
                tile((i - jj) * per - 1 - s, None)
            return 0

        lax.fori_loop(0, i, step, 0)
        o_ref[...] = acc_ref[...].astype(BF16)

    qs = pl.BlockSpec((qb, HEAD_DIM), lambda h, i: (i, h))
    return pl.pallas_call(
        body, name=name, grid=(H, T // qb),
        in_specs=[qs, pl.BlockSpec((T, HEAD_DIM), lambda h, i: (0, H + h)),
                  pl.BlockSpec((T, HEAD_DIM), lambda h, i: (0, 2 * H + h))],
        out_specs=[qs, qs],
        out_shape=[jax.ShapeDtypeStruct((T, W), BF16), jax.ShapeDtypeStruct((T, W), F32)],
        scratch_shapes=[pltpu.VMEM((qb, HEAD_DIM), F32)],
        compiler_params=_params("parallel", "arbitrary"),
    )(qkv, qkv, qkv)


def _sb_bwd(qkv, do, ltot, W, name):
    T = qkv.shape[0]
    H = W // HEAD_DIM
    blk = ATT_BLOCK
    nkb = T // SB_KEYS
    qb = min(SB_QUERY_ROWS, T)
    per = qb // SB_KEYS

    def body(q_ref, k_ref, v_ref, do_ref, lt_ref, dq_ref, dkt_ref, dvt_ref, qt_ref, dot_ref, plm_ref, pg_ref):
        i = pl.program_id(1)

        @pl.when(i == 0)
        def _():
            dkt_ref[...] = jnp.zeros_like(dkt_ref)
            dvt_ref[...] = jnp.zeros_like(dvt_ref)

        q = q_ref[...]
        do = do_ref[...]
        qt_ref[...] = q.astype(F32).T.astype(BF16)
        dot_ref[...] = do.astype(F32).T.astype(BF16)
        upto = _tri_and_ones(lambda r, c: r <= c)
        before = _tri_and_ones(lambda r, c: r < c)
        plm_ref[...] = jnp.zeros_like(plm_ref)
        pg_ref[...] = jnp.zeros_like(pg_ref)
        dq_ref[...] = jnp.zeros_like(dq_ref)

        def tile(j, mask):
            ks = pl.ds(pl.multiple_of(j * SB_KEYS, SB_KEYS), SB_KEYS)
            k = k_ref[ks, :]
            v = v_ref[ks, :]
            z = _dot(q, k, NT)
            ls, lm = _log_sigmoids(z)
            nsig = jnp.exp(lm)
            if mask is not None:
                lm = jnp.where(mask, lm, 0.0)
            earlier = plm_ref[...]
            first = _running(lm[:, :blk], upto)
            second = _running(lm[:, blk:], upto)
            upto_first = earlier + first[:, blk:]
            seen = jnp.concatenate([first[:, :blk] + earlier, second[:, :blk] + upto_first], axis=1)
            ltot = lt_ref[...]
            a = jnp.exp(ls + (jnp.concatenate([ltot, ltot], axis=1) - seen))
            if mask is not None:
                a = jnp.where(mask, a, 0.0)
            g = a * _dot(do, v, NT)
            g_earlier = pg_ref[...]
            g_first = _running(g[:, :blk], before)
            g_second = _running(g[:, blk:], before)
            g_upto_first = g_earlier + g_first[:, blk:]
            gsum = jnp.concatenate([g_first[:, :blk] + g_earlier, g_second[:, :blk] + g_upto_first], axis=1)
            dz = g * nsig - gsum * jnp.exp(ls)
            if mask is not None:
                dz = jnp.where(mask, dz, 0.0)
            dzb = dz.astype(BF16)
            dkt_ref[j] += _dot(qt_ref[...], dzb)
            dvt_ref[j] += _dot(dot_ref[...], a.astype(BF16))
            dq_ref[...] += _dot(dzb, k)
            plm_ref[...] = upto_first + second[:, blk:]
            pg_ref[...] = g_upto_first + g_second[:, blk:]

        def step(jj, _):
            for s in range(per):
                tile(jj * per + s, None)
            return 0

        lax.fori_loop(0, i, step, 0)
        for s in range(per):
            tile(i * per + s, _sb_mask(qb, s))

    qs = pl.BlockSpec((qb, HEAD_DIM), lambda h, i: (i, h))
    res = pl.BlockSpec((None, nkb, HEAD_DIM, SB_KEYS), lambda h, i: (h, 0, 0, 0))
    return pl.pallas_call(
        body, name=name, grid=(H, T // qb),
        in_specs=[qs, pl.BlockSpec((T, HEAD_DIM), lambda h, i: (0, H + h)),
                  pl.BlockSpec((T, HEAD_DIM), lambda h, i: (0, 2 * H + h)), qs, qs],
        out_specs=[qs, res, res],
        out_shape=[jax.ShapeDtypeStruct((T, W), F32)] + [jax.ShapeDtypeStruct((H, nkb, HEAD_DIM, SB_KEYS), F32)] * 2,
        scratch_shapes=[pltpu.VMEM((HEAD_DIM, qb), BF16), pltpu.VMEM((HEAD_DIM, qb), BF16),
                        pltpu.VMEM((qb, HEAD_DIM), F32), pltpu.VMEM((qb, HEAD_DIM), F32)],
        compiler_params=_params("parallel", "arbitrary"),
    )(qkv, qkv, qkv, do, ltot)


def _sb_pack(dq, dkt, dvt, name):
    T, W = dq.shape
    H = W // HEAD_DIM
    blk = SB_KEYS
    scale = HEAD_DIM ** -0.5

    def body(q_ref, kt_ref, vt_ref, o_ref):
        o_ref[:, pl.ds(0, W)] = (q_ref[...] * scale).astype(BF16)
        for h in range(H):
            o_ref[:, pl.ds(W + h * HEAD_DIM, HEAD_DIM)] = kt_ref[h].T.astype(BF16)
            o_ref[:, pl.ds(2 * W + h * HEAD_DIM, HEAD_DIM)] = vt_ref[h].T.astype(BF16)

    tr = pl.BlockSpec((H, None, HEAD_DIM, blk), lambda i: (0, i, 0, 0))
    return pl.pallas_call(
        body, name=name, grid=(T // blk,), in_specs=[pl.BlockSpec((blk, W), lambda i: (i, 0)), tr, tr],
        out_specs=pl.BlockSpec((blk, 3 * W), lambda i: (i, 0)),
        out_shape=jax.ShapeDtypeStruct((T, 3 * W), BF16), compiler_params=_params("parallel"),
    )(dq, dkt, dvt)


def _local_step(x, target, norms, sgu, comm):
    T, D = x.shape
    A = D // 2
    pre_mix, post_mix, pre_ffn, post_ffn = norms
    ln_g, ln_b, w_s, b_s = sgu
    b_t = b_s.T
    scale = HEAD_DIM ** -0.5

    def vec(p, layer):
        return comm.tie(p[layer:layer + 1])

    h0 = _prenorm(x, vec(pre_mix, 0), "prenorm0")
    comm.arrive("ab_in", after=h0)
    comm.land("ab_in")
    z = _matmul(h0, comm.weight("ab_in"), mode="nn", name="ab_in_fwd")
    comm.arrive("ab_out", after=z)
    a_out = _sgu_fwd(z, ln_g, ln_b, w_s, b_t, "sgu_fwd")
    branch = [_dilated_fwd(z, d, A, f"dilated_fwd_{d}") for _, d in DILATED_PAIRS]
    cat, o_dil, lse_dil = _dilated_merge(a_out, [b[0] for b in branch], [b[1] for b in branch], "dilated_merge")
    comm.land("ab_out", after=o_dil)
    y0 = _matmul(cat, comm.weight("ab_out"), mode="nn", name="ab_out_fwd")
    comm.arrive("ffn0", after=y0)
    x1, h1 = _postnorm_prenorm(x, y0, vec(post_mix, 0), vec(pre_ffn, 0), "norm_mix0")
    comm.land("ffn0", after=h1)

    def relu2(acc, j):
        r = jnp.maximum(acc, 0.0)
        return r * r

    f0 = _matmul(h1, comm.weight("w1_0"), mode="nn", name="ffn0_w1_fwd", out_dtype=BF16, epi=relu2)
    y1 = _matmul(f0, comm.weight("w2_0"), mode="nn", name="ffn0_w2_fwd")
    comm.arrive("sb", after=y1)
    x2, h2 = _postnorm_prenorm(x1, y1, vec(post_ffn, 0), vec(pre_mix, 1), "norm_ffn0")
    comm.land("sb", after=h2)

    tn_qkv = _tile(D, 1024)
    nq = D // tn_qkv

    def scale_q(acc, j):
        return jnp.where(j < nq, acc * scale, acc)

    qkv = _matmul(h2, comm.weight("sb_in"), mode="nn", name="sb_in_fwd", out_dtype=BF16, tn=tn_qkv, epi=scale_q)
    comm.arrive("ffn1", after=qkv)
    o_sb, ltot = _sb_fwd(qkv, D, "sb_fwd")
    comm.land("ffn1", after=o_sb)
    y2 = _matmul(o_sb, comm.weight("sb_out"), mode="nn", name="sb_out_fwd")
    x3, h3 = _postnorm_prenorm(x2, y2, vec(post_mix, 1), vec(pre_ffn, 1), "norm_mix1")
    f1 = _matmul(h3, comm.weight("w1_1"), mode="nn", name="ffn1_w1_fwd", out_dtype=BF16, epi=relu2)
    y3 = _matmul(f1, comm.weight("w2_1"), mode="nn", name="ffn1_w2_fwd")
    loss_tile, dx4, dy3, dg_post_ffn1 = _postnorm_loss(x3, y3, vec(post_ffn, 1), target, "norm_loss")
    loss = loss_tile[0, 0]

    def relu2_bwd(acc, j, f):
        return acc * (2.0 * jnp.sqrt(f.astype(F32)))

    def ffn_bwd(dy, h, f, layer):
        g_w2 = _matmul(f, dy, mode="tn", name=f"ffn{layer}_w2_wgrad", out_dtype=BF16)
        da = _matmul(dy, comm.weight(f"w2_{layer}"), mode="nt", name=f"ffn{layer}_w2_dgrad", out_dtype=BF16,
                     epi=relu2_bwd, extras=(f,))
        g_w1 = _matmul(h, da, mode="tn", name=f"ffn{layer}_w1_wgrad", out_dtype=BF16)
        comm.reduce(f"ffn{layer}", {f"w2_{layer}": g_w2, f"w1_{layer}": g_w1})
        return _matmul(da, comm.weight(f"w1_{layer}"), mode="nt", name=f"ffn{layer}_w1_dgrad", after=comm.started())

    dh3 = ffn_bwd(dy3, h3, f1, 1)
    dx3, dg_pre_ffn1, dy2, dg_post_mix1 = _norm_bwd_pair(dx4, dh3, x3, vec(pre_ffn, 1), y2, vec(post_mix, 1),
                                                         "ffn1_sb_norm_bwd")
    g_sb_out = _matmul(o_sb, dy2, mode="tn", name="sb_out_wgrad", out_dtype=BF16)
    do_sb = _matmul(dy2, comm.weight("sb_out"), mode="nt", name="sb_out_dgrad", out_dtype=BF16)
    dq, dk, dv = _sb_bwd(qkv, do_sb, ltot, D, "sb_bwd")
    dqkv = _sb_pack(dq, dk, dv, "sb_pack")
    g_sb_in = _matmul(h2, dqkv, mode="tn", name="sb_in_wgrad", out_dtype=BF16)
    comm.reduce("sb", {"sb_out": g_sb_out, "sb_in": g_sb_in})
    dh2 = _matmul(dqkv, comm.weight("sb_in"), mode="nt", name="sb_in_dgrad", after=comm.started())
    dx2, dg_pre_mix1, dy1, dg_post_ffn0 = _norm_bwd_pair(dx3, dh2, x2, vec(pre_mix, 1), y1, vec(post_ffn, 0),
                                                         "sb_ffn0_norm_bwd")
    dh1 = ffn_bwd(dy1, h1, f0, 0)
    dx1, dg_pre_ffn0, dy0, dg_post_mix0 = _norm_bwd_pair(dx2, dh1, x1, vec(pre_ffn, 0), y0, vec(post_mix, 0),
                                                         "ffn0_ab_norm_bwd")
    g_ab_out = _matmul(cat, dy0, mode="tn", name="ab_out_wgrad", out_dtype=BF16)
    comm.reduce("ab_out", {"ab_out": g_ab_out})
    dcat = _matmul(dy0, comm.weight("ab_out"), mode="nt", name="ab_out_dgrad", after=comm.started())
    duv, d_ln_g, d_ln_b, d_w_s, d_b_t = _sgu_bwd(z, dcat, ln_g, ln_b, w_s, b_t, "sgu_bwd")
    delta = _dilated_delta(dcat, o_dil, "dilated_delta")
    dqs, dks, dvs = [], [], []
    for _, d in DILATED_PAIRS:
        dqs.append(_dilated_dq(z, dcat, lse_dil, delta, d, A, f"dilated_dq_{d}"))
        dk_b, dv_b = _dilated_dkv(z, dcat, lse_dil, delta, d, A, f"dilated_dkv_{d}")
        dks.append(dk_b)
        dvs.append(dv_b)
    dz = _dilated_combine(duv, dqs, dks, dvs, "dilated_combine")
    g_ab_in = _matmul(h0, dz, mode="tn", name="ab_in_wgrad", out_dtype=BF16)
    comm.reduce("ab_in", {"ab_in": g_ab_in})
    dh0 = _matmul(dz, comm.weight("ab_in"), mode="nt", name="ab_in_dgrad", after=comm.started())
    dx0, dg_pre_mix0 = _prenorm_bwd(dx1, dh0, x, vec(pre_mix, 0), "ab_prenorm_bwd")

    small = {
        "pre_mix": jnp.concatenate([dg_pre_mix0, dg_pre_mix1], axis=0),
        "post_mix": jnp.concatenate([dg_post_mix0, dg_post_mix1], axis=0),
        "pre_ffn": jnp.concatenate([dg_pre_ffn0, dg_pre_ffn1], axis=0),
        "post_ffn": jnp.concatenate([dg_post_ffn0, dg_post_ffn1], axis=0),
        "ln_g": d_ln_g, "ln_b": d_ln_b, "w_s": d_w_s, "b_s": d_b_t.T,
    }
    return loss, dx0, small


MESH_ID = pl.DeviceIdType.MESH
ANY = pl.BlockSpec(memory_space=pl.ANY)


def _coords():
    return lax.axis_index("x"), lax.axis_index("y"), lax.axis_index("c")


def _shard_of(ref, kind, p):
    if kind == "col":
        n = ref.shape[1] // N_DEV
        return ref.at[:, pl.ds(pl.multiple_of(p * n, 128), n)]
    r = ref.shape[0] // N_DEV
    return ref.at[pl.ds(pl.multiple_of(p * r, 16), r), :]


def _full_shape(shard, kind):
    if kind == "col":
        return (shard.shape[0], shard.shape[1] * N_DEV)
    return (shard.shape[0] * N_DEV, shard.shape[1])


def _place(shards, layer, kind, block, after, name):
    _, rows, cols = shards.shape
    tr = _tile(rows, 512)

    def body(b_ref, s_ref, after_ref, o_ref):
        o_ref[...] = s_ref[...].astype(BF16)

    if kind == "col":
        out = pl.BlockSpec((tr, cols), lambda i, b_ref: (i, b_ref[0]))
    else:
        out = pl.BlockSpec((tr, cols), lambda i, b_ref: (b_ref[0] * (rows // tr) + i, 0))
    return pl.pallas_call(
        body, name=name,
        grid_spec=pltpu.PrefetchScalarGridSpec(
            num_scalar_prefetch=1, grid=(rows // tr,),
            in_specs=[pl.BlockSpec((None, tr, cols), lambda i, b_ref: (layer, i, 0)), ANY], out_specs=out),
        out_shape=jax.ShapeDtypeStruct(_full_shape(shards[0], kind), BF16),
        compiler_params=_params("parallel"),
    )(block, shards, after)


HBM = pl.BlockSpec(memory_space=pltpu.HBM)
SEM = pl.BlockSpec(memory_space=pltpu.SEMAPHORE)
FLOWS = pltpu.SideEffectType.DATAFLOW_SIDE_EFFECTING


def _in_hbm(a):
    return pltpu.with_memory_space_constraint(a, pltpu.HBM)


def _hbm_like(bufs):
    return [pltpu.HBM(b.shape, b.dtype) for b in bufs]


def _copies_start(name, bufs, plan, n, after):
    nb = len(bufs)

    def body(*refs):
        send_sems, recv_sems, token = refs[nb + 1], refs[nb + 2], refs[-1]
        for cp in plan(refs[:nb], send_sems, recv_sems):
            cp.start()
        token[...] = jnp.zeros_like(token)

    out = pl.pallas_call(
        body, name=name, in_specs=[HBM] * nb + [ANY],
        out_specs=[SEM, SEM] + [HBM] * nb + [pl.BlockSpec(memory_space=pltpu.VMEM)],
        out_shape=[pltpu.SemaphoreType.DMA((n,)), pltpu.SemaphoreType.DMA((n,))] + _hbm_like(bufs)
        + [jax.ShapeDtypeStruct((8, 128), F32)],
        input_output_aliases={i: 2 + i for i in range(nb)},
        compiler_params=pltpu.CompilerParams(has_side_effects=FLOWS),
    )(*[_in_hbm(b) for b in bufs], after)
    return (out[0], out[1]), list(out[2:2 + nb]), out[-1]


def _copies_wait(name, bufs, sems, after, plan):
    nb = len(bufs)

    def body(*refs):
        for cp in plan(refs[:nb], refs[nb], refs[nb + 1]):
            cp.wait_send()
            cp.wait_recv()

    out = pl.pallas_call(
        body, name=name, in_specs=[HBM] * nb + [SEM, SEM, ANY], out_specs=[HBM] * nb,
        out_shape=_hbm_like(bufs), input_output_aliases={i: i for i in range(nb)},
        compiler_params=pltpu.CompilerParams(has_side_effects=FLOWS),
    )(*bufs, *sems, after)
    return list(out)


def _copies_wait_start(name, bufs, sems, after, plan, next_plan, n_next):
    nb = len(bufs)

    def body(*refs):
        ins = refs[:nb]
        for cp in plan(ins, refs[nb], refs[nb + 1]):
            cp.wait_send()
            cp.wait_recv()
        send_sems, recv_sems, token = refs[nb + 3], refs[nb + 4], refs[-1]
        for cp in next_plan(ins, send_sems, recv_sems):
            cp.start()
        token[...] = jnp.zeros_like(token)

    out = pl.pallas_call(
        body, name=name, in_specs=[HBM] * nb + [SEM, SEM, ANY],
        out_specs=[SEM, SEM] + [HBM] * nb + [pl.BlockSpec(memory_space=pltpu.VMEM)],
        out_shape=[pltpu.SemaphoreType.DMA((n_next,)), pltpu.SemaphoreType.DMA((n_next,))] + _hbm_like(bufs)
        + [jax.ShapeDtypeStruct((8, 128), F32)],
        input_output_aliases={i: 2 + i for i in range(nb)},
        compiler_params=pltpu.CompilerParams(has_side_effects=FLOWS),
    )(*bufs, *sems, after)
    return (out[0], out[1]), list(out[2:2 + nb]), out[-1]


def _gather_plans(kinds):
    nt = len(kinds)

    def slot(refs, t, px, py, pc):
        return _shard_of(refs[t], kinds[t], 4 * px + 2 * py + pc)

    def to_chips(refs, send_sems, recv_sems):
        x, y, c = _coords()
        peers = [(x, y, 1 - c), (1 - x, y, c), (x, 1 - y, c), (1 - x, 1 - y, c)]
        return [pltpu.make_async_remote_copy(
            src_ref=slot(refs, t, x, y, c), dst_ref=slot(refs, t, x, y, c), send_sem=send_sems.at[4 * t + k],
            recv_sem=recv_sems.at[4 * t + k], device_id=peer, device_id_type=MESH_ID)
            for t in range(nt) for k, peer in enumerate(peers)]

    def to_sibling(refs, send_sems, recv_sems):
        x, y, c = _coords()
        chips = [(1 - x, y), (x, 1 - y), (1 - x, 1 - y)]
        return [pltpu.make_async_remote_copy(
            src_ref=slot(refs, t, *chip, c), dst_ref=slot(refs, t, *chip, c), send_sem=send_sems.at[3 * t + j],
            recv_sem=recv_sems.at[3 * t + j], device_id=(x, y, 1 - c), device_id_type=MESH_ID)
            for t in range(nt) for j, chip in enumerate(chips)]

    return to_chips, to_sibling


def _shard_shape(full, kind):
    if kind == "col":
        return (full.shape[0], full.shape[1] // N_DEV)
    return (full.shape[0] // N_DEV, full.shape[1])


def _scatter_plan(kinds):
    nt = len(kinds)

    def plan(refs, send_sems, recv_sems):
        x, y, c = _coords()
        copies = []
        for t in range(nt):
            for k in range(1, N_DEV):
                px = 1 - x if (k >> 2) & 1 else x
                py = 1 - y if (k >> 1) & 1 else y
                pc = 1 - c if k & 1 else c
                copies.append(pltpu.make_async_remote_copy(
                    src_ref=_shard_of(refs[t], kinds[t], 4 * px + 2 * py + pc),
                    dst_ref=refs[nt + t].at[4 * x + 2 * y + c],
                    send_sem=send_sems.at[7 * t + k - 1], recv_sem=recv_sems.at[7 * t + k - 1],
                    device_id=(px, py, pc), device_id_type=MESH_ID))
        return copies
    return plan


def _partial_specs(full, kind, tr):
    rows, cols = _shard_shape(full, kind)
    steps = rows // tr
    if kind == "col":
        own = pl.BlockSpec((tr, cols), lambda i, w: (i, w[0]))
    else:
        own = pl.BlockSpec((tr, cols), lambda i, w: (w[0] * steps + i, 0))
    return [own] + [pl.BlockSpec((None, tr, cols), lambda i, w, k=k: (w[k], i, 0)) for k in range(1, N_DEV)]


def _all_reduce_small(vec, after):
    R = vec.shape[0]

    def body(v_ref, after_ref, o_ref, recv_ref, send_sems, recv_sems):
        x, y, c = _coords()
        me = 4 * x + 2 * y + c
        recv_ref[me] = v_ref[...]
        copies = []
        for k in range(1, N_DEV):
            bx, by, bc = (k >> 2) & 1, (k >> 1) & 1, k & 1
            peer = (1 - x if bx else x, 1 - y if by else y, 1 - c if bc else c)
            copies.append(pltpu.make_async_remote_copy(
                src_ref=v_ref, dst_ref=recv_ref.at[me],
                send_sem=send_sems.at[k - 1], recv_sem=recv_sems.at[k - 1],
                device_id=peer, device_id_type=MESH_ID))
        for cp in copies:
            cp.start()
        for cp in copies:
            cp.wait()
        total = recv_ref[0]
        for p in range(1, N_DEV):
            total = total + recv_ref[p]
        o_ref[...] = total

    return pl.pallas_call(
        body, name="all_reduce_small",
        in_specs=[pl.BlockSpec(memory_space=pltpu.VMEM), ANY], out_specs=pl.BlockSpec(memory_space=pltpu.VMEM),
        out_shape=jax.ShapeDtypeStruct((R, 128), F32),
        scratch_shapes=[pltpu.VMEM((N_DEV, R, 128), F32), pltpu.SemaphoreType.DMA((N_DEV - 1,)),
                        pltpu.SemaphoreType.DMA((N_DEV - 1,))],
        compiler_params=pltpu.CompilerParams(vmem_limit_bytes=VMEM_LIMIT),
    )(vec, after)


def _adamw_math(w, g, m, v):
    m = ADAM_B1 * m + (1.0 - ADAM_B1) * g
    v = ADAM_B2 * v + (1.0 - ADAM_B2) * (g * g)
    m_hat = m / (1.0 - ADAM_B1 ** ADAM_STEP)
    v_hat = v / (1.0 - ADAM_B2 ** ADAM_STEP)
    delta = -ADAM_LR * (m_hat / (jnp.sqrt(v_hat) + ADAM_EPS) + ADAM_WD * w)
    return delta, m, v


def _adamw(w, grads, kind, where, m, v, name):
    layers, rows, cols = w.shape
    tr = _tile(rows, 128)
    out = None
    for layer, (grad, land) in enumerate(grads):
        def body(w_ref, *refs):
            parts, (x_ref, m_ref, v_ref) = refs[:N_DEV], refs[N_DEV:N_DEV + 3]
            g_ref, d_ref, mo_ref, vo_ref = refs[-4:]
            g = parts[0][...].astype(F32)
            for p_ref in parts[1:]:
                g = g + p_ref[...].astype(F32)
            g_ref[...] = g
            d_ref[...], mo_ref[...], vo_ref[...] = _adamw_math(x_ref[...], g, m_ref[...], v_ref[...])

        blk = pl.BlockSpec((None, tr, cols), lambda i, w_, layer=layer: (layer, i, 0))
        earlier = [] if out is None else list(out)
        out = pl.pallas_call(
            body, name=f"{name}_{layer}",
            grid_spec=pltpu.PrefetchScalarGridSpec(
                num_scalar_prefetch=1, grid=(rows // tr,),
                in_specs=_partial_specs(grad, kind, tr) + [blk] * 3 + [ANY] * len(earlier),
                out_specs=[blk] * 4),
            out_shape=[jax.ShapeDtypeStruct((layers, rows, cols), F32)] * 4,
            input_output_aliases={N_DEV + 4 + k: k for k in range(len(earlier))},
            compiler_params=_params("parallel"),
        )(where, grad, *[land] * (N_DEV - 1), w, m, v, *earlier)
    return out


def _adamw_small(w, g, m, v):
    def body(w_ref, g_ref, m_ref, v_ref, d_ref, mo_ref, vo_ref):
        d_ref[...], mo_ref[...], vo_ref[...] = _adamw_math(w_ref[...], g_ref[...], m_ref[...], v_ref[...])

    whole = pl.BlockSpec(memory_space=pltpu.VMEM)
    return pl.pallas_call(
        body, name="adamw_small", in_specs=[whole] * 4, out_specs=[whole] * 3,
        out_shape=[jax.ShapeDtypeStruct(w.shape, F32)] * 3,
        compiler_params=pltpu.CompilerParams(vmem_limit_bytes=VMEM_LIMIT),
    )(w, g, m, v)


def _pack(arrays):
    rows = []
    for a in arrays:
        flat = a.reshape(-1)
        pad = (-flat.shape[0]) % 1024
        rows.append(jnp.pad(flat, (0, pad)).reshape(-1, 128))
    return jnp.concatenate(rows, axis=0)


def _unpack(packed, like):
    out, r = [], 0
    for a in like:
        n = math.prod(a.shape)
        nr = (n + 1023) // 1024 * 8
        out.append(packed[r:r + nr].reshape(-1)[:n].reshape(a.shape))
        r += nr
    return out


KIND = {"ab_in": "col", "ab_out": "row", "sb_in": "col", "sb_out": "row",
        "w1_0": "col", "w1_1": "col", "w2_0": "row", "w2_1": "row"}
GATHERS = {"ab_in": ("ab_in",), "ab_out": ("ab_out",), "ffn0": ("w1_0", "w2_0"),
           "sb": ("sb_in", "sb_out"), "ffn1": ("w1_1", "w2_1")}


class _Exchange:
    def __init__(self, shards):
        x, y, c = _coords()
        me = (4 * x + 2 * y + c).astype(jnp.int32)
        self.where = jnp.stack([jnp.bitwise_xor(me, k) for k in range(N_DEV)])
        self.full = {}
        self.tokens = []
        self.gathers = {}
        self.scatters = {}
        self.settled = {}
        block = me.reshape(1)
        after = block
        for key, group in GATHERS.items():
            for n in group:
                self.full[n] = _place(*shards[n], KIND[n], block, after, f"place_{n}")
            to_chips, to_sibling = _gather_plans([KIND[n] for n in group])
            bufs = [self.full[n] for n in group]
            sems, bufs, after = _copies_start(f"gather_start_{key}", bufs, to_chips, 4 * len(group), after)
            self.tokens.append(after)
            self.gathers[key] = (group, sems, bufs, to_chips, to_sibling)

    def tie(self, small):
        for token in self.tokens:
            small = small + token[0:1, 0:1]
        self.tokens = []
        return small

    def started(self):
        return tuple(self.tokens)

    def weight(self, name):
        return self.full[name]

    def arrive(self, key, after):
        group, sems, bufs, to_chips, to_sibling = self.gathers[key]
        sems, bufs, token = _copies_wait_start(f"gather_pass_{key}", bufs, sems, after, to_chips, to_sibling,
                                               3 * len(group))
        self.tokens.append(token)
        self.gathers[key] = (group, sems, bufs, token, to_sibling)

    def land(self, key, after=None):
        group, sems, bufs, token, to_sibling = self.gathers.pop(key)
        after = token if after is None else after
        self.full.update(zip(group, _copies_wait(f"gather_done_{key}", bufs, sems, after, to_sibling)))

    def reduce(self, key, grads):
        names = list(grads)
        kinds = [KIND[n] for n in names]
        full = [grads[n] for n in names]
        lands = [lax.empty((N_DEV,) + _shard_shape(g, k), BF16) for g, k in zip(full, kinds)]
        plan = _scatter_plan(kinds)
        sems, bufs, token = _copies_start(f"scatter_start_{key}", full + lands, plan, (N_DEV - 1) * len(names),
                                          full[-1])
        self.tokens.append(token)
        self.scatters[key] = (names, sems, bufs, plan)

    def settle(self, keys, after):
        for key in keys:
            names, sems, bufs, plan = self.scatters.pop(key)
            bufs = _copies_wait(f"scatter_done_{key}", bufs, sems, after, plan)
            self.settled.update({n: t for n, *t in zip(names, bufs[:len(names)], bufs[len(names):])})
        return self.settled


SMALL = ("norm_pre_mix", "norm_post_mix", "norm_pre_ffn", "norm_post_ffn", "sgu_ln_g", "sgu_ln_b", "sgu_w", "sgu_b")
ORDER = ("norm_pre_mix", "norm_post_mix", "norm_pre_ffn", "norm_post_ffn", "ab_w_in", "sgu_ln_g", "sgu_ln_b", "sgu_w",
         "sgu_b", "ab_w_out", "sb_w_in", "sb_w_out", "ffn_w1", "ffn_w2")


def kernel(x, norm_pre_mix, norm_post_mix, norm_pre_ffn, norm_post_ffn, ab_w_in, sgu_ln_g, sgu_ln_b, sgu_w, sgu_b, ab_w_out, sb_w_in, sb_w_out, ffn_w1, ffn_w2, loss_target, m_norm_pre_mix, m_norm_post_mix, m_norm_pre_ffn, m_norm_post_ffn, m_ab_w_in, m_sgu_ln_g, m_sgu_ln_b, m_sgu_w, m_sgu_b, m_ab_w_out, m_sb_w_in, m_sb_w_out, m_ffn_w1, m_ffn_w2, v_norm_pre_mix, v_norm_post_mix, v_norm_pre_ffn, v_norm_post_ffn, v_ab_w_in, v_sgu_ln_g, v_sgu_ln_b, v_sgu_w, v_sgu_b, v_ab_w_out, v_sb_w_in, v_sb_w_out, v_ffn_w1, v_ffn_w2):
    W = dict(norm_pre_mix=norm_pre_mix, norm_post_mix=norm_post_mix, norm_pre_ffn=norm_pre_ffn,
             norm_post_ffn=norm_post_ffn, ab_w_in=ab_w_in, sgu_ln_g=sgu_ln_g, sgu_ln_b=sgu_ln_b, sgu_w=sgu_w,
             sgu_b=sgu_b, ab_w_out=ab_w_out, sb_w_in=sb_w_in, sb_w_out=sb_w_out, ffn_w1=ffn_w1, ffn_w2=ffn_w2)
    M = dict(norm_pre_mix=m_norm_pre_mix, norm_post_mix=m_norm_post_mix, norm_pre_ffn=m_norm_pre_ffn,
             norm_post_ffn=m_norm_post_ffn, ab_w_in=m_ab_w_in, sgu_ln_g=m_sgu_ln_g, sgu_ln_b=m_sgu_ln_b,
             sgu_w=m_sgu_w, sgu_b=m_sgu_b, ab_w_out=m_ab_w_out, sb_w_in=m_sb_w_in, sb_w_out=m_sb_w_out,
             ffn_w1=m_ffn_w1, ffn_w2=m_ffn_w2)
    V = dict(norm_pre_mix=v_norm_pre_mix, norm_post_mix=v_norm_post_mix, norm_pre_ffn=v_norm_pre_ffn,
             norm_post_ffn=v_norm_post_ffn, ab_w_in=v_ab_w_in, sgu_ln_g=v_sgu_ln_g, sgu_ln_b=v_sgu_ln_b,
             sgu_w=v_sgu_w, sgu_b=v_sgu_b, ab_w_out=v_ab_w_out, sb_w_in=v_sb_w_in, sb_w_out=v_sb_w_out,
             ffn_w1=v_ffn_w1, ffn_w2=v_ffn_w2)

    shards = {"ab_in": (ab_w_in, 0), "ab_out": (ab_w_out, 0), "w1_0": (ffn_w1, 0), "w2_0": (ffn_w2, 0),
              "sb_in": (sb_w_in, 0), "sb_out": (sb_w_out, 0), "w1_1": (ffn_w1, 1), "w2_1": (ffn_w2, 1)}
    comm = _Exchange(shards)
    norms = (norm_pre_mix, norm_post_mix, norm_pre_ffn, norm_post_ffn)
    sgu = (sgu_ln_g, sgu_ln_b, sgu_w[0], sgu_b[0])
    loss, dx, small = _local_step(x[0], loss_target[0], norms, sgu, comm)
    loss = lax.psum(loss, MESH_AXES)

    out = {}

    def update(name, layers, landed):
        out[name] = _adamw(W[name], [landed[n] for n in layers], KIND[layers[0]], comm.where, M[name], V[name],
                           f"adamw_{name}")

    landed = comm.settle(("ffn1", "sb", "ffn0"), after=dx)
    for name, layers in (("sb_w_in", ["sb_in"]), ("sb_w_out", ["sb_out"]), ("ffn_w1", ["w1_0", "w1_1"]),
                         ("ffn_w2", ["w2_0", "w2_1"])):
        update(name, layers, landed)
    small_g = [small["pre_mix"], small["post_mix"], small["pre_ffn"], small["post_ffn"], small["ln_g"],
               small["ln_b"], small["w_s"][None], small["b_s"][None]]
    g_small = _all_reduce_small(_pack(small_g), out["ffn_w2"][3])
    landed = comm.settle(("ab_out", "ab_in"), after=g_small)
    update("ab_w_out", ["ab_out"], landed)
    update("ab_w_in", ["ab_in"], landed)
    res = _adamw_small(_pack([W[n] for n in SMALL]), g_small, _pack([M[n] for n in SMALL]),
                       _pack([V[n] for n in SMALL]))
    like = [W[n] for n in SMALL]
    for n, *vals in zip(SMALL, *[_unpack(r, like) for r in [g_small] + list(res)]):
        out[n] = vals

    return (loss, dx[None], *[out[n][0] for n in ORDER], *[out[n][1] for n in ORDER],
            *[out[n][2] for n in ORDER], *[out[n][3] for n in ORDER])
```

```python
import functools
import math

import jax
import jax.numpy as jnp
from jax import lax
from jax.experimental import pallas as pl
from jax.experimental.pallas import tpu as pltpu

F32 = jnp.float32
BF16 = jnp.bfloat16

HEAD_DIM = 128
CHUNK = 128
ATT_BLOCK = 128
DILATED_PAIRS = ((128, 1), (512, 4), (2048, 16))
RMS_EPS = 1e-6
LN_EPS = 1e-5
ADAM_LR = 0.001
ADAM_B1 = 0.9
ADAM_B2 = 0.999
ADAM_EPS = 1e-08
ADAM_WD = 0.01
ADAM_STEP = 10
N_DEV = 8
MESH_AXES = ("x", "y", "c")
MASKED = -1e30

V7X_VMEM_BYTES = 64 * 1024 * 1024
VMEM_LIMIT = V7X_VMEM_BYTES - 8 * 1024 * 1024

NN = (((1,), (0,)), ((), ()))
NT = (((1,), (1,)), ((), ()))
TN = (((0,), (0,)), ((), ()))


def _params(*sem):
    return pltpu.CompilerParams(dimension_semantics=sem, vmem_limit_bytes=VMEM_LIMIT)


def _dot(a, b, dims=NN):
    return lax.dot_general(a, b, dims, preferred_element_type=F32)


def _tile(n, preferred):
    if n <= preferred:
        return n
    t = preferred - preferred % 128
    while n % t:
        t -= 128
    assert t > 0, (n, preferred)
    return t


def _matmul(a, b, *, mode, name, out_dtype=F32, tm=1024, tn=1024, tk=2048, epi=None, extras=(), after=()):
    if mode == "nn":
        (M, K), N = a.shape, b.shape[1]
    elif mode == "nt":
        (M, K), N = a.shape, b.shape[0]
    else:
        (K, M), N = a.shape, b.shape[1]
    tm, tn, tk = _tile(M, tm), _tile(N, tn), _tile(K, tk)
    nk = K // tk
    if mode == "tn":
        a_spec = pl.BlockSpec((tk, tm), lambda i, j, k: (k, i))
    else:
        a_spec = pl.BlockSpec((tm, tk), lambda i, j, k: (i, k))
    if mode == "nt":
        b_spec = pl.BlockSpec((tn, tk), lambda i, j, k: (j, k))
    else:
        b_spec = pl.BlockSpec((tk, tn), lambda i, j, k: (k, j))
    o_spec = pl.BlockSpec((tm, tn), lambda i, j, k: (i, j))
    dims = {"nn": NN, "nt": NT, "tn": TN}[mode]
    n_extra = len(extras)
    n_in = n_extra + len(after)

    def finish(acc, refs):
        j = pl.program_id(1)
        if epi is None:
            return acc
        return epi(acc, j, *[r[...] for r in refs])

    if nk == 1:
        def body(a_ref, b_ref, *rest):
            o_ref = rest[n_in]
            acc = _dot(a_ref[...], b_ref[...], dims)
            o_ref[...] = finish(acc, rest[:n_extra]).astype(o_ref.dtype)
        scratch = []
    else:
        def body(a_ref, b_ref, *rest):
            o_ref, acc_ref = rest[n_in], rest[n_in + 1]
            k = pl.program_id(2)

            @pl.when(k == 0)
            def _():
                acc_ref[...] = jnp.zeros_like(acc_ref)

            acc_ref[...] += _dot(a_ref[...], b_ref[...], dims)

            @pl.when(k == nk - 1)
            def _():
                o_ref[...] = finish(acc_ref[...], rest[:n_extra]).astype(o_ref.dtype)
        scratch = [pltpu.VMEM((tm, tn), F32)]

    return pl.pallas_call(
        body,
        name=name,
        grid=(M // tm, N // tn, nk),
        in_specs=[a_spec, b_spec] + [o_spec] * n_extra + [ANY] * len(after),
        out_specs=o_spec,
        out_shape=jax.ShapeDtypeStruct((M, N), out_dtype),
        scratch_shapes=scratch,
        compiler_params=_params("parallel", "parallel", "arbitrary"),
    )(a, b, *extras, *after)


ROWS = 256


def _rms(x):
    return lax.rsqrt(jnp.mean(x * x, axis=-1, keepdims=True) + RMS_EPS)


def _prenorm(x, g, name):
    T, D = x.shape

    def body(x_ref, g_ref, h_ref):
        xv = x_ref[...]
        h_ref[...] = (xv * _rms(xv) * g_ref[...]).astype(BF16)

    row = pl.BlockSpec((ROWS, D), lambda i: (i, 0))
    vec = pl.BlockSpec((1, D), lambda i: (0, 0))
    return pl.pallas_call(
        body, name=name, grid=(T // ROWS,), in_specs=[row, vec], out_specs=row,
        out_shape=jax.ShapeDtypeStruct((T, D), BF16), compiler_params=_params("parallel"),
    )(x, g)


def _postnorm_prenorm(x, y, g_post, g_pre, name):
    T, D = x.shape

    def body(x_ref, y_ref, gp_ref, gn_ref, xo_ref, h_ref):
        yv = y_ref[...]
        xn = x_ref[...] + yv * _rms(yv) * gp_ref[...]
        xo_ref[...] = xn
        h_ref[...] = (xn * _rms(xn) * gn_ref[...]).astype(BF16)

    row = pl.BlockSpec((ROWS, D), lambda i: (i, 0))
    vec = pl.BlockSpec((1, D), lambda i: (0, 0))
    return pl.pallas_call(
        body, name=name, grid=(T // ROWS,), in_specs=[row, row, vec, vec], out_specs=[row, row],
        out_shape=[jax.ShapeDtypeStruct((T, D), F32), jax.ShapeDtypeStruct((T, D), BF16)],
        compiler_params=_params("parallel"),
    )(x, y, g_post, g_pre)


def _postnorm_grads(dn, yh, r, g):
    gd = dn * g
    return r * (gd - yh * jnp.mean(yh * gd, axis=-1, keepdims=True)), dn * yh


def _postnorm_loss(x, y, g_post, target, name):
    T, D = x.shape

    def body(x_ref, y_ref, gp_ref, t_ref, loss_ref, dx_ref, dy_ref, dg_ref):
        @pl.when(pl.program_id(0) == 0)
        def _():
            loss_ref[...] = jnp.zeros_like(loss_ref)
            dg_ref[...] = jnp.zeros_like(dg_ref)

        yv = y_ref[...]
        r = _rms(yv)
        yh = yv * r
        err = x_ref[...] + yh * gp_ref[...] - t_ref[...]
        dx = err * (1.0 / D)
        dx_ref[...] = dx
        loss_ref[...] += 0.5 * jnp.sum(jnp.sum(err * err, axis=-1, keepdims=True) * (1.0 / D))
        dy, dg = _postnorm_grads(dx, yh, r, gp_ref[...])
        dy_ref[...] = dy.astype(BF16)
        dg_ref[...] += jnp.sum(dg, axis=0, keepdims=True)

    row = pl.BlockSpec((ROWS, D), lambda i: (i, 0))
    vec = pl.BlockSpec((1, D), lambda i: (0, 0))
    acc = pl.BlockSpec((8, 128), lambda i: (0, 0))
    return pl.pallas_call(
        body, name=name, grid=(T // ROWS,), in_specs=[row, row, vec, row], out_specs=[acc, row, row, vec],
        out_shape=[jax.ShapeDtypeStruct((8, 128), F32), jax.ShapeDtypeStruct((T, D), F32),
                   jax.ShapeDtypeStruct((T, D), BF16), jax.ShapeDtypeStruct((1, D), F32)],
        compiler_params=_params("arbitrary"),
    )(x, y, g_post, target)


def _norm_bwd_pair(dx_out, dh, x, g_pre, y_prev, g_post_prev, name):
    T, D = x.shape

    def body(dxo_ref, dh_ref, x_ref, g_ref, y_ref, gp_ref, dx_ref, dg_ref, dy_ref, dgp_ref):
        @pl.when(pl.program_id(0) == 0)
        def _():
            dg_ref[...] = jnp.zeros_like(dg_ref)
            dgp_ref[...] = jnp.zeros_like(dgp_ref)

        xv, dhv = x_ref[...], dh_ref[...]
        r = _rms(xv)
        xh = xv * r
        gd = dhv * g_ref[...]
        dx = dxo_ref[...] + r * (gd - xh * jnp.mean(xh * gd, axis=-1, keepdims=True))
        dx_ref[...] = dx
        dg_ref[...] += jnp.sum(dhv * xh, axis=0, keepdims=True)
        yv = y_ref[...]
        ry = _rms(yv)
        dy, dgp = _postnorm_grads(dx, yv * ry, ry, gp_ref[...])
        dy_ref[...] = dy.astype(BF16)
        dgp_ref[...] += jnp.sum(dgp, axis=0, keepdims=True)

    row = pl.BlockSpec((ROWS, D), lambda i: (i, 0))
    vec = pl.BlockSpec((1, D), lambda i: (0, 0))
    return pl.pallas_call(
        body, name=name, grid=(T // ROWS,), in_specs=[row, row, row, vec, row, vec],
        out_specs=[row, vec, row, vec],
        out_shape=[jax.ShapeDtypeStruct((T, D), F32), jax.ShapeDtypeStruct((1, D), F32),
                   jax.ShapeDtypeStruct((T, D), BF16), jax.ShapeDtypeStruct((1, D), F32)],
        compiler_params=_params("arbitrary"),
    )(dx_out, dh, x, g_pre, y_prev, g_post_prev)


def _prenorm_bwd(dx_out, dh, x, g_pre, name):
    T, D = x.shape

    def body(dxo_ref, dh_ref, x_ref, g_ref, dx_ref, dg_ref):
        @pl.when(pl.program_id(0) == 0)
        def _():
            dg_ref[...] = jnp.zeros_like(dg_ref)

        xv, dhv = x_ref[...], dh_ref[...]
        r = _rms(xv)
        xh = xv * r
        gd = dhv * g_ref[...]
        dx_ref[...] = dxo_ref[...] + r * (gd - xh * jnp.mean(xh * gd, axis=-1, keepdims=True))
        dg_ref[...] += jnp.sum(dhv * xh, axis=0, keepdims=True)

    row = pl.BlockSpec((ROWS, D), lambda i: (i, 0))
    vec = pl.BlockSpec((1, D), lambda i: (0, 0))
    return pl.pallas_call(
        body, name=name, grid=(T // ROWS,), in_specs=[row, row, row, vec], out_specs=[row, vec],
        out_shape=[jax.ShapeDtypeStruct((T, D), F32), jax.ShapeDtypeStruct((1, D), F32)],
        compiler_params=_params("arbitrary"),
    )(dx_out, dh, x, g_pre)


_INV_SQRT2 = 1.0 / math.sqrt(2.0)
_INV_SQRT2PI = 1.0 / math.sqrt(2.0 * math.pi)


def _gelu(x):
    return 0.5 * x * (1.0 + lax.erf(x * _INV_SQRT2))


def _gelu_grad(x):
    return 0.5 * (1.0 + lax.erf(x * _INV_SQRT2)) + x * jnp.exp(-0.5 * x * x) * _INV_SQRT2PI


def _layernorm_stats(x):
    mu = jnp.mean(x, axis=-1, keepdims=True)
    xc = x - mu
    rstd = lax.rsqrt(jnp.mean(xc * xc, axis=-1, keepdims=True) + LN_EPS)
    return xc * rstd, rstd


def _tril_mask():
    i = lax.broadcasted_iota(jnp.int32, (CHUNK, CHUNK), 0)
    j = lax.broadcasted_iota(jnp.int32, (CHUNK, CHUNK), 1)
    return j <= i


SGU_ROWS = 512


def _sgu_fwd(z, ln_g, ln_b, w_s, b_t, name):
    T = z.shape[0]
    A = ln_g.shape[1]
    G = A // 128
    rows = min(SGU_ROWS, T)

    def body(u_ref, v_ref, g_ref, b_ref, w_ref, bt_ref, o_ref):
        mask = _tril_mask()
        for c in range(rows // CHUNK):
            rs = pl.ds(c * CHUNK, CHUNK)
            xh, _ = _layernorm_stats(_gelu(v_ref[rs, :]))
            vn = (xh * g_ref[...] + b_ref[...]).astype(BF16)
            for g in range(G):
                cs = pl.ds(g * 128, 128)
                w = jnp.where(mask, w_ref[g], 0.0).astype(BF16)
                mixed = _dot(w, vn[:, g * 128:(g + 1) * 128]) + bt_ref[:, g:g + 1]
                o_ref[rs, cs] = (_gelu(u_ref[rs, cs]) * mixed).astype(BF16)

    return pl.pallas_call(
        body, name=name, grid=(T // rows,),
        in_specs=[
            pl.BlockSpec((rows, A), lambda i: (i, 0)),
            pl.BlockSpec((rows, A), lambda i: (i, 1)),
            pl.BlockSpec((1, A), lambda i: (0, 0)),
            pl.BlockSpec((1, A), lambda i: (0, 0)),
            pl.BlockSpec((G, CHUNK, CHUNK), lambda i: (0, 0, 0)),
            pl.BlockSpec((CHUNK, G), lambda i: (0, 0)),
        ],
        out_specs=pl.BlockSpec((rows, A), lambda i: (i, 0)),
        out_shape=jax.ShapeDtypeStruct((T, 2 * A), BF16),
        compiler_params=_params("parallel"),
    )(z, z, ln_g, ln_b, w_s, b_t)


def _sgu_bwd(z, dcat, ln_g, ln_b, w_s, b_t, name):
    T = z.shape[0]
    A = ln_g.shape[1]
    G = A // 128
    rows = min(SGU_ROWS, T)

    def body(u_ref, v_ref, da_ref, g_ref, b_ref, w_ref, bt_ref, dz_ref, dg_ref, db_ref, dw_ref, dbt_ref, dvn_ref):
        @pl.when(pl.program_id(0) == 0)
        def _():
            dg_ref[...] = jnp.zeros_like(dg_ref)
            db_ref[...] = jnp.zeros_like(db_ref)
            dw_ref[...] = jnp.zeros_like(dw_ref)
            dbt_ref[...] = jnp.zeros_like(dbt_ref)

        mask = _tril_mask()
        for c in range(rows // CHUNK):
            rs = pl.ds(c * CHUNK, CHUNK)
            vv = v_ref[rs, :]
            gv = _gelu(vv)
            xh, rstd = _layernorm_stats(gv)
            vn = (xh * g_ref[...] + b_ref[...]).astype(BF16)
            for g in range(G):
                cs = pl.ds(g * 128, 128)
                w = jnp.where(mask, w_ref[g], 0.0).astype(BF16)
                vg = vn[:, g * 128:(g + 1) * 128]
                mixed = _dot(w, vg) + bt_ref[:, g:g + 1]
                uu = u_ref[rs, cs]
                da = da_ref[rs, cs]
                dz_ref[rs, cs] = (da * mixed * _gelu_grad(uu)).astype(BF16)
                dm = da * _gelu(uu)
                dmb = dm.astype(BF16)
                dbt_ref[:, g:g + 1] += jnp.sum(dm, axis=1, keepdims=True)
                dw_ref[g] += jnp.where(mask, _dot(dmb, vg, NT), 0.0)
                dvn_ref[:, cs] = _dot(w, dmb, TN)
            dvn = dvn_ref[...]
            dg_ref[...] += jnp.sum(dvn * xh, axis=0, keepdims=True)
            db_ref[...] += jnp.sum(dvn, axis=0, keepdims=True)
            dxh = dvn * g_ref[...]
            dgv = rstd * (dxh - jnp.mean(dxh, axis=-1, keepdims=True)
                          - xh * jnp.mean(dxh * xh, axis=-1, keepdims=True))
            dz_ref[rs, pl.ds(A, A)] = (dgv * _gelu_grad(vv)).astype(BF16)

    vec = pl.BlockSpec((1, A), lambda i: (0, 0))
    wsp = pl.BlockSpec((G, CHUNK, CHUNK), lambda i: (0, 0, 0))
    bsp = pl.BlockSpec((CHUNK, G), lambda i: (0, 0))
    return pl.pallas_call(
        body, name=name, grid=(T // rows,),
        in_specs=[
            pl.BlockSpec((rows, A), lambda i: (i, 0)),
            pl.BlockSpec((rows, A), lambda i: (i, 1)),
            pl.BlockSpec((rows, A), lambda i: (i, 0)),
            vec, vec, wsp, bsp,
        ],
        out_specs=[pl.BlockSpec((rows, 2 * A), lambda i: (i, 0)), vec, vec, wsp, bsp],
        out_shape=[
            jax.ShapeDtypeStruct((T, 2 * A), BF16),
            jax.ShapeDtypeStruct((1, A), F32),
            jax.ShapeDtypeStruct((1, A), F32),
            jax.ShapeDtypeStruct((G, CHUNK, CHUNK), F32),
            jax.ShapeDtypeStruct((CHUNK, G), F32),
        ],
        scratch_shapes=[pltpu.VMEM((CHUNK, A), F32)],
        compiler_params=_params("arbitrary"),
    )(z, z, dcat, ln_g, ln_b, w_s, b_t)


def _alibi_row(B, d):
    H = B // HEAD_DIM
    slopes = [d * 2.0 ** (-8.0 * (h + 1.0) / H) for h in range(H)]
    return jnp.repeat(jnp.asarray(slopes, F32), HEAD_DIM)[None, :]


def _dil_scores(q, k, slope_d, valid, dist):
    s = _dot(q, k, NT) - slope_d * dist
    return jnp.where(valid, s, MASKED)


def _dil_layout(T, B, d):
    H = B // HEAD_DIM
    hp = H if d == 1 else 1
    return hp, H // hp, T // (d * ATT_BLOCK)


def _dil_rows(ref, r, d, cs):
    return ref[pl.ds(r, ATT_BLOCK, stride=d), cs] if d > 1 else ref[:, cs]


def _dil_put(ref, r, d, cs, val):
    if d > 1:
        ref[pl.ds(r, ATT_BLOCK, stride=d), cs] = val
    else:
        ref[:, cs] = val


def _dil_block(n, r, d):
    if d == 1:
        return pl.ds(pl.multiple_of(n * ATT_BLOCK, ATT_BLOCK), ATT_BLOCK)
    return pl.ds(n * (d * ATT_BLOCK) + r, ATT_BLOCK, stride=d)


def _dilated_forward(z, cat, B, name):
    T = z.shape[0]
    H = B // HEAD_DIM
    A = cat.shape[1] - B
    scale = HEAD_DIM ** -0.5
    blk = ATT_BLOCK
    chunk = _tile(T, 512)

    def body(q_ref, k_ref, v_ref, sl_ref, cat_in, cat_ref, of_ref, lt_ref, *branch):
        o_refs, l_refs = branch[:3], branch[3:]
        slope = sl_ref[:, :1]
        qi = lax.broadcasted_iota(jnp.int32, (blk, 2 * blk), 0)
        kj = lax.broadcasted_iota(jnp.int32, (blk, 2 * blk), 1)
        dist = qi + blk - kj
        band = (dist >= 0) & (dist <= blk)
        distf = dist.astype(F32)

        for b, (_, d) in enumerate(DILATED_PAIRS):
            def one(n, r, b=b, d=d):
                rows, prev = _dil_block(n, r, d), _dil_block(jnp.maximum(n - 1, 0), r, d)
                q = (q_ref[rows, :] * scale).astype(BF16)
                k = jnp.concatenate([k_ref[prev, :], k_ref[rows, :]], axis=0).astype(BF16)
                v = jnp.concatenate([v_ref[prev, :], v_ref[rows, :]], axis=0).astype(BF16)
                s = _dil_scores(q, k, slope * float(d), band & ((kj >= blk) | (n > 0)), distf)
                m = jnp.max(s, axis=-1, keepdims=True)
                p = jnp.exp(s - m)
                den = jnp.sum(p, axis=-1, keepdims=True)
                o_refs[b][rows, :] = _dot(p.astype(BF16), v) / den
                l_refs[b][rows, :] = jnp.broadcast_to(m + jnp.log(den), (blk, HEAD_DIM))

            per = max(1, 4 // d)

            def step(i, _, d=d, per=per, one=one):
                for u in range(per):
                    for r in range(d):
                        one(i * per + u, r)
                return 0

            lax.fori_loop(0, T // (d * blk * per), step, 0)

        def merge(i, _):
            rs = pl.ds(pl.multiple_of(i * chunk, chunk), chunk)
            a, b, c = l_refs[0][rs, :], l_refs[1][rs, :], l_refs[2][rs, :]
            m = jnp.maximum(jnp.maximum(a, b), c)
            ea, eb, ec = jnp.exp(a - m), jnp.exp(b - m), jnp.exp(c - m)
            tot = ea + eb + ec
            o = (ea * o_refs[0][rs, :] + eb * o_refs[1][rs, :] + ec * o_refs[2][rs, :]) / tot
            of_ref[rs, :] = o
            cat_ref[rs, :] = o.astype(BF16)
            lt_ref[rs, :] = m + jnp.log(tot)
            return 0

        lax.fori_loop(0, T // chunk, merge, 0)

    def col(unit):
        return lambda h: (0, unit * H + h)

    seq = (T, HEAD_DIM)
    out = pl.BlockSpec(seq, lambda h: (0, h))
    return pl.pallas_call(
        body, name=name, grid=(H,),
        in_specs=[pl.BlockSpec(seq, col(2)), pl.BlockSpec(seq, col(3)), pl.BlockSpec(seq, col(4)),
                  pl.BlockSpec((1, HEAD_DIM), lambda h: (0, h)), ANY],
        out_specs=[pl.BlockSpec(seq, lambda h: (0, A // HEAD_DIM + h)), out, out],
        out_shape=[jax.ShapeDtypeStruct(cat.shape, BF16), jax.ShapeDtypeStruct((T, B), F32),
                   jax.ShapeDtypeStruct((T, B), F32)],
        input_output_aliases={4: 0},
        scratch_shapes=[pltpu.VMEM(seq, F32)] * 6,
        compiler_params=_params("parallel"),
    )(z, z, z, _alibi_row(B, 1), cat)


def _dilated_backward(z, dcat, o, lse, B, name):
    T = z.shape[0]
    H = B // HEAD_DIM
    scale = HEAD_DIM ** -0.5
    blk = ATT_BLOCK
    chunk = _tile(T, 512)

    def body(q_ref, k_ref, v_ref, do_ref, o_ref, l_ref, sl_ref, dq_ref, dk_ref, dv_ref, aq_ref, ak_ref, av_ref):
        slope = sl_ref[:, :1]
        qi = lax.broadcasted_iota(jnp.int32, (blk, 2 * blk), 0)
        kj = lax.broadcasted_iota(jnp.int32, (blk, 2 * blk), 1)
        dist = qi + blk - kj
        band = (dist >= 0) & (dist <= blk)
        distf = dist.astype(F32)
        for acc in (aq_ref, ak_ref, av_ref):
            acc[...] = jnp.zeros_like(acc)

        for _, d in DILATED_PAIRS:
            def one(n, r, d=d):
                rows, prev = _dil_block(n, r, d), _dil_block(jnp.maximum(n - 1, 0), r, d)
                q = (q_ref[rows, :] * scale).astype(BF16)
                k = jnp.concatenate([k_ref[prev, :], k_ref[rows, :]], axis=0).astype(BF16)
                v = jnp.concatenate([v_ref[prev, :], v_ref[rows, :]], axis=0).astype(BF16)
                do = do_ref[rows, :]
                delta = jnp.sum(do * o_ref[rows, :], axis=-1, keepdims=True)
                do = do.astype(BF16)
                s = _dil_scores(q, k, slope * float(d), band & ((kj >= blk) | (n > 0)), distf)
                p = jnp.exp(s - l_ref[rows, :][:, :1])
                ds = (p * (_dot(do, v, NT) - delta)).astype(BF16)
                aq_ref[rows, :] += _dot(ds, k)
                dk = _dot(ds, q, TN)
                dv = _dot(p.astype(BF16), do, TN)
                ak_ref[prev, :] += dk[:blk]
                av_ref[prev, :] += dv[:blk]
                ak_ref[rows, :] += dk[blk:]
                av_ref[rows, :] += dv[blk:]

            per = max(1, 4 // d)

            def step(i, _, d=d, per=per, one=one):
                for u in range(per):
                    for r in range(d):
                        one(i * per + u, r)
                return 0

            lax.fori_loop(0, T // (d * blk * per), step, 0)

        def emit(i, _):
            rs = pl.ds(pl.multiple_of(i * chunk, chunk), chunk)
            dq_ref[rs, :] = (aq_ref[rs, :] * scale).astype(BF16)
            dk_ref[rs, :] = ak_ref[rs, :].astype(BF16)
            dv_ref[rs, :] = av_ref[rs, :].astype(BF16)
            return 0

        lax.fori_loop(0, T // chunk, emit, 0)

    def col(unit):
        return lambda h: (0, unit * H + h)

    seq = (T, HEAD_DIM)
    own = pl.BlockSpec(seq, lambda h: (0, h))
    return pl.pallas_call(
        body, name=name, grid=(H,),
        in_specs=[pl.BlockSpec(seq, col(2)), pl.BlockSpec(seq, col(3)), pl.BlockSpec(seq, col(4)),
                  pl.BlockSpec(seq, col(1)), own, own, pl.BlockSpec((1, HEAD_DIM), lambda h: (0, h))],
        out_specs=[own] * 3,
        out_shape=[jax.ShapeDtypeStruct((T, B), BF16)] * 3,
        scratch_shapes=[pltpu.VMEM(seq, F32)] * 3,
        compiler_params=_params("parallel"),
    )(z, z, z, dcat, o, lse, _alibi_row(B, 1))


def _join_columns(parts, name):
    T = parts[0].shape[0]
    widths = [p.shape[1] for p in parts]

    def body(*refs):
        o_ref, at = refs[-1], 0
        for ref, w in zip(refs[:-1], widths):
            o_ref[:, pl.ds(at, w)] = ref[...]
            at += w

    return pl.pallas_call(
        body, name=name, grid=(T // ROWS,),
        in_specs=[pl.BlockSpec((ROWS, w), lambda i: (i, 0)) for w in widths],
        out_specs=pl.BlockSpec((ROWS, sum(widths)), lambda i: (i, 0)),
        out_shape=jax.ShapeDtypeStruct((T, sum(widths)), BF16), compiler_params=_params("parallel"),
    )(*parts)


def _dilated_fwd(z, d, B, name):
    T = z.shape[0]
    H = B // HEAD_DIM
    hp, groups, nb = _dil_layout(T, B, d)
    cw = hp * HEAD_DIM
    scale = HEAD_DIM ** -0.5
    blk = ATT_BLOCK

    def body(q_ref, kp_ref, kc_ref, vp_ref, vc_ref, sl_ref, o_ref, l_ref):
        n = pl.program_id(1)
        qi = lax.broadcasted_iota(jnp.int32, (blk, 2 * blk), 0)
        kj = lax.broadcasted_iota(jnp.int32, (blk, 2 * blk), 1)
        dist = qi + blk - kj
        valid = (dist >= 0) & (dist <= blk) & ((kj >= blk) | (n > 0))
        distf = dist.astype(F32)
        for r in range(d):
            for hh in range(hp):
                cs = pl.ds(hh * HEAD_DIM, HEAD_DIM)
                q = (_dil_rows(q_ref, r, d, cs) * scale).astype(BF16)
                k = jnp.concatenate([_dil_rows(kp_ref, r, d, cs), _dil_rows(kc_ref, r, d, cs)], axis=0).astype(BF16)
                v = jnp.concatenate([_dil_rows(vp_ref, r, d, cs), _dil_rows(vc_ref, r, d, cs)], axis=0).astype(BF16)
                s = _dil_scores(q, k, sl_ref[:, cs][:, :1], valid, distf)
                m = jnp.max(s, axis=-1, keepdims=True)
                p = jnp.exp(s - m)
                den = jnp.sum(p, axis=-1, keepdims=True)
                _dil_put(o_ref, r, d, cs, _dot(p.astype(BF16), v) / den)
                _dil_put(l_ref, r, d, cs, jnp.broadcast_to(m + jnp.log(den), (blk, HEAD_DIM)))

    def col(unit):
        return lambda g, n: (n, unit * groups + g)

    def col_prev(unit):
        return lambda g, n: (jnp.maximum(n - 1, 0), unit * groups + g)

    bs = (d * blk, cw)
    out = pl.BlockSpec(bs, lambda g, n: (n, g))
    return pl.pallas_call(
        body, name=name, grid=(groups, nb),
        in_specs=[pl.BlockSpec(bs, col(2)), pl.BlockSpec(bs, col_prev(3)), pl.BlockSpec(bs, col(3)),
                  pl.BlockSpec(bs, col_prev(4)), pl.BlockSpec(bs, col(4)), pl.BlockSpec((1, cw), lambda g, n: (0, g))],
        out_specs=[out, out],
        out_shape=[jax.ShapeDtypeStruct((T, B), F32)] * 2,
        compiler_params=_params("parallel", "parallel"),
    )(z, z, z, z, z, _alibi_row(B, d))


def _dilated_merge(a_out, os_, ls_, name):
    T, B = os_[0].shape
    A = a_out.shape[1]

    def body(a_ref, o1, o2, o3, l1, l2, l3, cat_ref, of_ref, lt_ref):
        a, b, c = l1[...], l2[...], l3[...]
        m = jnp.maximum(jnp.maximum(a, b), c)
        ea, eb, ec = jnp.exp(a - m), jnp.exp(b - m), jnp.exp(c - m)
        tot = ea + eb + ec
        o = (ea * o1[...] + eb * o2[...] + ec * o3[...]) / tot
        of_ref[...] = o
        cat_ref[:, pl.ds(0, A)] = a_ref[...]
        cat_ref[:, pl.ds(A, B)] = o.astype(BF16)
        lt_ref[...] = m + jnp.log(tot)

    row = pl.BlockSpec((ROWS, B), lambda i: (i, 0))
    return pl.pallas_call(
        body, name=name, grid=(T // ROWS,), in_specs=[pl.BlockSpec((ROWS, A), lambda i: (i, 0))] + [row] * 6,
        out_specs=[pl.BlockSpec((ROWS, A + B), lambda i: (i, 0)), row, row],
        out_shape=[jax.ShapeDtypeStruct((T, A + B), BF16), jax.ShapeDtypeStruct((T, B), F32),
                   jax.ShapeDtypeStruct((T, B), F32)],
        compiler_params=_params("parallel"),
    )(a_out, *os_, *ls_)


def _dilated_delta(dcat, o, name):
    T, B = o.shape
    H = B // HEAD_DIM

    def body(do_ref, o_ref, d_ref):
        for h in range(H):
            cs = pl.ds(h * HEAD_DIM, HEAD_DIM)
            dsum = jnp.sum(do_ref[:, cs] * o_ref[:, cs], axis=-1, keepdims=True)
            d_ref[:, cs] = jnp.broadcast_to(dsum, (ROWS, HEAD_DIM))

    row = pl.BlockSpec((ROWS, B), lambda i: (i, 0))
    return pl.pallas_call(
        body, name=name, grid=(T // ROWS,),
        in_specs=[pl.BlockSpec((ROWS, B), lambda i: (i, 1)), row], out_specs=row,
        out_shape=jax.ShapeDtypeStruct((T, B), F32), compiler_params=_params("parallel"),
    )(dcat, o)


def _dilated_dq(z, dcat, lse, delta, d, B, name):
    T = z.shape[0]
    hp, groups, nb = _dil_layout(T, B, d)
    cw = hp * HEAD_DIM
    scale = HEAD_DIM ** -0.5
    blk = ATT_BLOCK

    def body(q_ref, kp_ref, kc_ref, vp_ref, vc_ref, do_ref, l_ref, dl_ref, sl_ref, dq_ref):
        n = pl.program_id(1)
        qi = lax.broadcasted_iota(jnp.int32, (blk, 2 * blk), 0)
        kj = lax.broadcasted_iota(jnp.int32, (blk, 2 * blk), 1)
        dist = qi + blk - kj
        valid = (dist >= 0) & (dist <= blk) & ((kj >= blk) | (n > 0))
        distf = dist.astype(F32)
        for r in range(d):
            for hh in range(hp):
                cs = pl.ds(hh * HEAD_DIM, HEAD_DIM)
                q = (_dil_rows(q_ref, r, d, cs) * scale).astype(BF16)
                k = jnp.concatenate([_dil_rows(kp_ref, r, d, cs), _dil_rows(kc_ref, r, d, cs)], axis=0).astype(BF16)
                v = jnp.concatenate([_dil_rows(vp_ref, r, d, cs), _dil_rows(vc_ref, r, d, cs)], axis=0).astype(BF16)
                s = _dil_scores(q, k, sl_ref[:, cs][:, :1], valid, distf)
                p = jnp.exp(s - _dil_rows(l_ref, r, d, cs)[:, :1])
                dp = _dot(_dil_rows(do_ref, r, d, cs).astype(BF16), v, NT)
                ds = p * (dp - _dil_rows(dl_ref, r, d, cs)[:, :1])
                _dil_put(dq_ref, r, d, cs, _dot(ds.astype(BF16), k))

    def col(unit):
        return lambda g, n: (n, unit * groups + g)

    def col_prev(unit):
        return lambda g, n: (jnp.maximum(n - 1, 0), unit * groups + g)

    bs = (d * blk, cw)
    out = pl.BlockSpec(bs, lambda g, n: (n, g))
    return pl.pallas_call(
        body, name=name, grid=(groups, nb),
        in_specs=[pl.BlockSpec(bs, col(2)), pl.BlockSpec(bs, col_prev(3)), pl.BlockSpec(bs, col(3)),
                  pl.BlockSpec(bs, col_prev(4)), pl.BlockSpec(bs, col(4)), pl.BlockSpec(bs, col(1)), out, out,
                  pl.BlockSpec((1, cw), lambda g, n: (0, g))],
        out_specs=out,
        out_shape=jax.ShapeDtypeStruct((T, B), F32),
        compiler_params=_params("parallel", "parallel"),
    )(z, z, z, z, z, dcat, lse, delta, _alibi_row(B, d))


def _dilated_dkv(z, dcat, lse, delta, d, B, name):
    T = z.shape[0]
    hp, groups, nb = _dil_layout(T, B, d)
    cw = hp * HEAD_DIM
    scale = HEAD_DIM ** -0.5
    blk = ATT_BLOCK

    def body(k_ref, v_ref, qa_ref, qb_ref, doa_ref, dob_ref, la_ref, lb_ref, da_ref, db_ref, sl_ref, dk_ref, dv_ref):
        m = pl.program_id(1)
        qi = lax.broadcasted_iota(jnp.int32, (2 * blk, blk), 0)
        kj = lax.broadcasted_iota(jnp.int32, (2 * blk, blk), 1)
        dist = qi - kj
        valid = (dist >= 0) & (dist <= blk) & ((qi < blk) | (m + 1 < nb))
        distf = dist.astype(F32)
        for r in range(d):
            for hh in range(hp):
                cs = pl.ds(hh * HEAD_DIM, HEAD_DIM)

                def both(a_ref, b_ref):
                    return jnp.concatenate([_dil_rows(a_ref, r, d, cs), _dil_rows(b_ref, r, d, cs)], axis=0)

                q = (both(qa_ref, qb_ref) * scale).astype(BF16)
                do = both(doa_ref, dob_ref).astype(BF16)
                k = _dil_rows(k_ref, r, d, cs).astype(BF16)
                v = _dil_rows(v_ref, r, d, cs).astype(BF16)
                s = _dil_scores(q, k, sl_ref[:, cs][:, :1], valid, distf)
                p = jnp.exp(jnp.where(valid, s - both(la_ref, lb_ref)[:, :1], MASKED))
                _dil_put(dv_ref, r, d, cs, _dot(p.astype(BF16), do, TN))
                ds = p * (_dot(do, v, NT) - both(da_ref, db_ref)[:, :1])
                _dil_put(dk_ref, r, d, cs, _dot(ds.astype(BF16), q, TN))

    def col(unit):
        return lambda g, m: (m, unit * groups + g)

    def nxt(unit):
        return lambda g, m: (jnp.minimum(m + 1, nb - 1), unit * groups + g)

    bs = (d * blk, cw)
    out = pl.BlockSpec(bs, lambda g, m: (m, g))
    return pl.pallas_call(
        body, name=name, grid=(groups, nb),
        in_specs=[pl.BlockSpec(bs, col(3)), pl.BlockSpec(bs, col(4)),
                  pl.BlockSpec(bs, col(2)), pl.BlockSpec(bs, nxt(2)),
                  pl.BlockSpec(bs, col(1)), pl.BlockSpec(bs, nxt(1)),
                  out, pl.BlockSpec(bs, nxt(0)), out, pl.BlockSpec(bs, nxt(0)),
                  pl.BlockSpec((1, cw), lambda g, m: (0, g))],
        out_specs=[out, out],
        out_shape=[jax.ShapeDtypeStruct((T, B), F32)] * 2,
        compiler_params=_params("parallel", "parallel"),
    )(z, z, z, z, dcat, dcat, lse, lse, delta, delta, _alibi_row(B, d))


def _dilated_combine(duv, dqs, dks, dvs, name):
    T, B = dqs[0].shape
    A2 = duv.shape[1]
    scale = HEAD_DIM ** -0.5

    def body(uv, q1, q2, q3, k1, k2, k3, v1, v2, v3, o_ref):
        o_ref[:, pl.ds(0, A2)] = uv[...]
        o_ref[:, pl.ds(A2, B)] = ((q1[...] + q2[...] + q3[...]) * scale).astype(BF16)
        o_ref[:, pl.ds(A2 + B, B)] = (k1[...] + k2[...] + k3[...]).astype(BF16)
        o_ref[:, pl.ds(A2 + 2 * B, B)] = (v1[...] + v2[...] + v3[...]).astype(BF16)

    row = pl.BlockSpec((ROWS, B), lambda i: (i, 0))
    return pl.pallas_call(
        body, name=name, grid=(T // ROWS,), in_specs=[pl.BlockSpec((ROWS, A2), lambda i: (i, 0))] + [row] * 9,
        out_specs=pl.BlockSpec((ROWS, A2 + 3 * B), lambda i: (i, 0)),
        out_shape=jax.ShapeDtypeStruct((T, A2 + 3 * B), BF16), compiler_params=_params("parallel"),
    )(duv, *dqs, *dks, *dvs)


SB_QUERY_ROWS = 512
SB_KEYS = 2 * ATT_BLOCK


def _tri_and_ones(pred):
    rows = lax.broadcasted_iota(jnp.int32, (2 * ATT_BLOCK, 2 * ATT_BLOCK), 0) % ATT_BLOCK
    cols = lax.broadcasted_iota(jnp.int32, (2 * ATT_BLOCK, 2 * ATT_BLOCK), 1)
    return ((cols >= ATT_BLOCK) | pred(rows, cols)).astype(BF16)


def _running(x, tri):
    hi = x.astype(BF16)
    lo = (x - hi.astype(F32)).astype(BF16)
    return _dot(jnp.concatenate([hi, lo], axis=1), tri)


def _sb_mask(query_rows, s):
    rows = lax.broadcasted_iota(jnp.int32, (query_rows, SB_KEYS), 0)
    cols = lax.broadcasted_iota(jnp.int32, (query_rows, SB_KEYS), 1)
    return cols + s * SB_KEYS < rows


def _log_sigmoids(z):
    ls = jnp.minimum(z, 0.0) - jnp.log(1.0 + jnp.exp(-jnp.abs(z)))
    return ls, ls - z


def _sb_fwd(qkv, W, name):
    T = qkv.shape[0]
    H = W // HEAD_DIM
    blk = ATT_BLOCK
    qb = min(SB_QUERY_ROWS, T)
    per = qb // SB_KEYS

    def body(q_ref, k_ref, v_ref, o_ref, lt_ref, acc_ref):
        i = pl.program_id(1)
        q = q_ref[...]
        tri = _tri_and_ones(lambda r, c: r > c)
        lt_ref[...] = jnp.zeros_like(lt_ref)
        acc_ref[...] = jnp.zeros_like(acc_ref)

        def tile(j, mask):
            ks = pl.ds(pl.multiple_of(j * SB_KEYS, SB_KEYS), SB_KEYS)
            z = _dot(q, k_ref[ks, :], NT)
            ls, lm = _log_sigmoids(z)
            if mask is not None:
                lm = jnp.where(mask, lm, 0.0)
            later = lt_ref[...]
            second = _running(lm[:, blk:], tri)
            first = _running(lm[:, :blk], tri)
            after_first = later + second[:, blk:]
            a = jnp.exp(ls + jnp.concatenate([first[:, :blk] + after_first, second[:, :blk] + later], axis=1))
            if mask is not None:
                a = jnp.where(mask, a, 0.0)
            acc_ref[...] += _dot(a.astype(BF16), v_ref[ks, :])
            lt_ref[...] = after_first + first[:, blk:]

        for s in reversed(range(per)):
            tile(i * per + s, _sb_mask(qb, s))

        def step(jj, _):
            for s in range(per):
                tile((i - jj) * per - 1 - s, None)
            return 0

        lax.fori_loop(0, i, step, 0)
        o_ref[...] = acc_ref[...].astype(BF16)

    qs = pl.BlockSpec((qb, HEAD_DIM), lambda h, i: (i, h))
    return pl.pallas_call(
        body, name=name, grid=(H, T // qb),
        in_specs=[qs, pl.BlockSpec((T, HEAD_DIM), lambda h, i: (0, H + h)),
                  pl.BlockSpec((T, HEAD_DIM), lambda h, i: (0, 2 * H + h))],
        out_specs=[qs, qs],
        out_shape=[jax.ShapeDtypeStruct((T, W), BF16), jax.ShapeDtypeStruct((T, W), F32)],
        scratch_shapes=[pltpu.VMEM((qb, HEAD_DIM), F32)],
        compiler_params=_params("parallel", "arbitrary"),
    )(qkv, qkv, qkv)


def _sb_bwd(qkv, do, ltot, W, name):
    T = qkv.shape[0]
    H = W // HEAD_DIM
    blk = ATT_BLOCK
    nkb = T // SB_KEYS
    qb = min(SB_QUERY_ROWS, T)
    per = qb // SB_KEYS

    def body(q_ref, k_ref, v_ref, do_ref, lt_ref, dq_ref, dkt_ref, dvt_ref, qt_ref, dot_ref, plm_ref, pg_ref):
        i = pl.program_id(1)

        @pl.when(i == 0)
        def _():
            dkt_ref[...] = jnp.zeros_like(dkt_ref)
            dvt_ref[...] = jnp.zeros_like(dvt_ref)

        q = q_ref[...]
        do = do_ref[...]
        qt_ref[...] = q.astype(F32).T.astype(BF16)
        dot_ref[...] = do.astype(F32).T.astype(BF16)
        upto = _tri_and_ones(lambda r, c: r <= c)
        before = _tri_and_ones(lambda r, c: r < c)
        plm_ref[...] = jnp.zeros_like(plm_ref)
        pg_ref[...] = jnp.zeros_like(pg_ref)
        dq_ref[...] = jnp.zeros_like(dq_ref)

        def tile(j, mask):
            ks = pl.ds(pl.multiple_of(j * SB_KEYS, SB_KEYS), SB_KEYS)
            k = k_ref[ks, :]
            v = v_ref[ks, :]
            z = _dot(q, k, NT)
            ls, lm = _log_sigmoids(z)
            nsig = jnp.exp(lm)
            if mask is not None:
                lm = jnp.where(mask, lm, 0.0)
            earlier = plm_ref[...]
            first = _running(lm[:, :blk], upto)
            second = _running(lm[:, blk:], upto)
            upto_first = earlier + first[:, blk:]
            seen = jnp.concatenate([first[:, :blk] + earlier, second[:, :blk] + upto_first], axis=1)
            ltot = lt_ref[...]
            a = jnp.exp(ls + (jnp.concatenate([ltot, ltot], axis=1) - seen))
            if mask is not None:
                a = jnp.where(mask, a, 0.0)
            g = a * _dot(do, v, NT)
            g_earlier = pg_ref[...]
            g_first = _running(g[:, :blk], before)
            g_second = _running(g[:, blk:], before)
            g_upto_first = g_earlier + g_first[:, blk:]
            gsum = jnp.concatenate([g_first[:, :blk] + g_earlier, g_second[:, :blk] + g_upto_first], axis=1)
            dz = g * nsig - gsum * jnp.exp(ls)
            if mask is not None:
                dz = jnp.where(mask, dz, 0.0)
            dzb = dz.astype(BF16)
            dkt_ref[j] += _dot(qt_ref[...], dzb)
            dvt_ref[j] += _dot(dot_ref[...], a.astype(BF16))
            dq_ref[...] += _dot(dzb, k)
            plm_ref[...] = upto_first + second[:, blk:]
            pg_ref[...] = g_upto_first + g_second[:, blk:]

        def step(jj, _):
            for s in range(per):
                tile(jj * per + s, None)
            return 0

        lax.fori_loop(0, i, step, 0)
        for s in range(per):
            tile(i * per + s, _sb_mask(qb, s))

    qs = pl.BlockSpec((qb, HEAD_DIM), lambda h, i: (i, h))
    res = pl.BlockSpec((None, nkb, HEAD_DIM, SB_KEYS), lambda h, i: (h, 0, 0, 0))
    return pl.pallas_call(
        body, name=name, grid=(H, T // qb),
        in_specs=[qs, pl.BlockSpec((T, HEAD_DIM), lambda h, i: (0, H + h)),
                  pl.BlockSpec((T, HEAD_DIM), lambda h, i: (0, 2 * H + h)), qs, qs],
        out_specs=[qs, res, res],
        out_shape=[jax.ShapeDtypeStruct((T, W), F32)] + [jax.ShapeDtypeStruct((H, nkb, HEAD_DIM, SB_KEYS), F32)] * 2,
        scratch_shapes=[pltpu.VMEM((HEAD_DIM, qb), BF16), pltpu.VMEM((HEAD_DIM, qb), BF16),
                        pltpu.VMEM((qb, HEAD_DIM), F32), pltpu.VMEM((qb, HEAD_DIM), F32)],
        compiler_params=_params("parallel", "arbitrary"),
    )(qkv, qkv, qkv, do, ltot)


def _sb_pack(dq, dkt, dvt, name):
    T, W = dq.shape
    H = W // HEAD_DIM
    blk = SB_KEYS
    scale = HEAD_DIM ** -0.5

    def body(q_ref, kt_ref, vt_ref, o_ref):
        o_ref[:, pl.ds(0, W)] = (q_ref[...] * scale).astype(BF16)
        for h in range(H):
            o_ref[:, pl.ds(W + h * HEAD_DIM, HEAD_DIM)] = kt_ref[h].T.astype(BF16)
            o_ref[:, pl.ds(2 * W + h * HEAD_DIM, HEAD_DIM)] = vt_ref[h].T.astype(BF16)

    tr = pl.BlockSpec((H, None, HEAD_DIM, blk), lambda i: (0, i, 0, 0))
    return pl.pallas_call(
        body, name=name, grid=(T // blk,), in_specs=[pl.BlockSpec((blk, W), lambda i: (i, 0)), tr, tr],
        out_specs=pl.BlockSpec((blk, 3 * W), lambda i: (i, 0)),
        out_shape=jax.ShapeDtypeStruct((T, 3 * W), BF16), compiler_params=_params("parallel"),
    )(dq, dkt, dvt)


def _local_step(x, target, norms, sgu, comm):
    T, D = x.shape
    A = D // 2
    pre_mix, post_mix, pre_ffn, post_ffn = norms
    ln_g, ln_b, w_s, b_s = sgu
    b_t = b_s.T
    scale = HEAD_DIM ** -0.5

    def vec(p, layer):
        return comm.tie(p[layer:layer + 1])

    h0 = _prenorm(x, vec(pre_mix, 0), "prenorm0")
    comm.arrive("ab_in", after=h0)
    comm.land("ab_in")
    z = _matmul(h0, comm.weight("ab_in"), mode="nn", name="ab_in_fwd")
    comm.arrive("ab_out", after=z)
    cat = _sgu_fwd(z, ln_g, ln_b, w_s, b_t, "sgu_fwd")
    cat, o_dil, lse_dil = _dilated_forward(z, cat, A, "dilated_fwd")
    comm.land("ab_out", after=o_dil)
    y0 = _matmul(cat, comm.weight("ab_out"), mode="nn", name="ab_out_fwd")
    comm.arrive("ffn0", after=y0)
    x1, h1 = _postnorm_prenorm(x, y0, vec(post_mix, 0), vec(pre_ffn, 0), "norm_mix0")
    comm.land("ffn0", after=h1)

    def relu2(acc, j):
        r = jnp.maximum(acc, 0.0)
        return r * r

    f0 = _matmul(h1, comm.weight("w1_0"), mode="nn", name="ffn0_w1_fwd", out_dtype=BF16, epi=relu2)
    y1 = _matmul(f0, comm.weight("w2_0"), mode="nn", name="ffn0_w2_fwd")
    comm.arrive("sb", after=y1)
    x2, h2 = _postnorm_prenorm(x1, y1, vec(post_ffn, 0), vec(pre_mix, 1), "norm_ffn0")
    comm.land("sb", after=h2)

    tn_qkv = _tile(D, 1024)
    nq = D // tn_qkv

    def scale_q(acc, j):
        return jnp.where(j < nq, acc * scale, acc)

    qkv = _matmul(h2, comm.weight("sb_in"), mode="nn", name="sb_in_fwd", out_dtype=BF16, tn=tn_qkv, epi=scale_q)
    comm.arrive("ffn1", after=qkv)
    o_sb, ltot = _sb_fwd(qkv, D, "sb_fwd")
    comm.land("ffn1", after=o_sb)
    y2 = _matmul(o_sb, comm.weight("sb_out"), mode="nn", name="sb_out_fwd")
    x3, h3 = _postnorm_prenorm(x2, y2, vec(post_mix, 1), vec(pre_ffn, 1), "norm_mix1")
    f1 = _matmul(h3, comm.weight("w1_1"), mode="nn", name="ffn1_w1_fwd", out_dtype=BF16, epi=relu2)
    y3 = _matmul(f1, comm.weight("w2_1"), mode="nn", name="ffn1_w2_fwd")
    loss_tile, dx4, dy3, dg_post_ffn1 = _postnorm_loss(x3, y3, vec(post_ffn, 1), target, "norm_loss")
    loss = loss_tile[0, 0]

    def relu2_bwd(acc, j, f):
        return acc * (2.0 * jnp.sqrt(f.astype(F32)))

    def ffn_bwd(dy, h, f, layer):
        g_w2 = _matmul(f, dy, mode="tn", name=f"ffn{layer}_w2_wgrad", out_dtype=BF16)
        da = _matmul(dy, comm.weight(f"w2_{layer}"), mode="nt", name=f"ffn{layer}_w2_dgrad", out_dtype=BF16,
                     epi=relu2_bwd, extras=(f,))
        g_w1 = _matmul(h, da, mode="tn", name=f"ffn{layer}_w1_wgrad", out_dtype=BF16)
        comm.reduce(f"ffn{layer}", {f"w2_{layer}": g_w2, f"w1_{layer}": g_w1})
        return _matmul(da, comm.weight(f"w1_{layer}"), mode="nt", name=f"ffn{layer}_w1_dgrad", after=comm.started())

    dh3 = ffn_bwd(dy3, h3, f1, 1)
    dx3, dg_pre_ffn1, dy2, dg_post_mix1 = _norm_bwd_pair(dx4, dh3, x3, vec(pre_ffn, 1), y2, vec(post_mix, 1),
                                                         "ffn1_sb_norm_bwd")
    g_sb_out = _matmul(o_sb, dy2, mode="tn", name="sb_out_wgrad", out_dtype=BF16)
    do_sb = _matmul(dy2, comm.weight("sb_out"), mode="nt", name="sb_out_dgrad", out_dtype=BF16)
    dq, dk, dv = _sb_bwd(qkv, do_sb, ltot, D, "sb_bwd")
    dqkv = _sb_pack(dq, dk, dv, "sb_pack")
    g_sb_in = _matmul(h2, dqkv, mode="tn", name="sb_in_wgrad", out_dtype=BF16)
    comm.reduce("sb", {"sb_out": g_sb_out, "sb_in": g_sb_in})
    dh2 = _matmul(dqkv, comm.weight("sb_in"), mode="nt", name="sb_in_dgrad", after=comm.started())
    dx2, dg_pre_mix1, dy1, dg_post_ffn0 = _norm_bwd_pair(dx3, dh2, x2, vec(pre_mix, 1), y1, vec(post_ffn, 0),
                                                         "sb_ffn0_norm_bwd")
    dh1 = ffn_bwd(dy1, h1, f0, 0)
    dx1, dg_pre_ffn0, dy0, dg_post_mix0 = _norm_bwd_pair(dx2, dh1, x1, vec(pre_ffn, 0), y0, vec(post_mix, 0),
                                                         "ffn0_ab_norm_bwd")
    g_ab_out = _matmul(cat, dy0, mode="tn", name="ab_out_wgrad", out_dtype=BF16)
    comm.reduce("ab_out", {"ab_out": g_ab_out})
    dcat = _matmul(dy0, comm.weight("ab_out"), mode="nt", name="ab_out_dgrad", after=comm.started())
    duv, d_ln_g, d_ln_b, d_w_s, d_b_t = _sgu_bwd(z, dcat, ln_g, ln_b, w_s, b_t, "sgu_bwd")
    dz = _join_columns([duv, *_dilated_backward(z, dcat, o_dil, lse_dil, A, "dilated_bwd")], "join_dz")
    g_ab_in = _matmul(h0, dz, mode="tn", name="ab_in_wgrad", out_dtype=BF16)
    comm.reduce("ab_in", {"ab_in": g_ab_in})
    dh0 = _matmul(dz, comm.weight("ab_in"), mode="nt", name="ab_in_dgrad", after=comm.started())
    dx0, dg_pre_mix0 = _prenorm_bwd(dx1, dh0, x, vec(pre_mix, 0), "ab_prenorm_bwd")

    small = {
        "pre_mix": jnp.concatenate([dg_pre_mix0, dg_pre_mix1], axis=0),
        "post_mix": jnp.concatenate([dg_post_mix0, dg_post_mix1], axis=0),
        "pre_ffn": jnp.concatenate([dg_pre_ffn0, dg_pre_ffn1], axis=0),
        "post_ffn": jnp.concatenate([dg_post_ffn0, dg_post_ffn1], axis=0),
        "ln_g": d_ln_g, "ln_b": d_ln_b, "w_s": d_w_s, "b_s": d_b_t.T,
    }
    return loss, dx0, small


MESH_ID = pl.DeviceIdType.MESH
ANY = pl.BlockSpec(memory_space=pl.ANY)


def _coords():
    return lax.axis_index("x"), lax.axis_index("y"), lax.axis_index("c")


def _shard_of(ref, kind, p):
    if kind == "col":
        n = ref.shape[1] // N_DEV
        return ref.at[:, pl.ds(pl.multiple_of(p * n, 128), n)]
    r = ref.shape[0] // N_DEV
    return ref.at[pl.ds(pl.multiple_of(p * r, 16), r), :]


def _full_shape(shard, kind):
    if kind == "col":
        return (shard.shape[0], shard.shape[1] * N_DEV)
    return (shard.shape[0] * N_DEV, shard.shape[1])


def _place(shards, layer, kind, block, after, name):
    _, rows, cols = shards.shape
    tr = _tile(rows, 512)

    def body(b_ref, s_ref, after_ref, o_ref):
        o_ref[...] = s_ref[...].astype(BF16)

    if kind == "col":
        out = pl.BlockSpec((tr, cols), lambda i, b_ref: (i, b_ref[0]))
    else:
        out = pl.BlockSpec((tr, cols), lambda i, b_ref: (b_ref[0] * (rows // tr) + i, 0))
    return pl.pallas_call(
        body, name=name,
        grid_spec=pltpu.PrefetchScalarGridSpec(
            num_scalar_prefetch=1, grid=(rows // tr,),
            in_specs=[pl.BlockSpec((None, tr, cols), lambda i, b_ref: (layer, i, 0)), ANY], out_specs=out),
        out_shape=jax.ShapeDtypeStruct(_full_shape(shards[0], kind), BF16),
        compiler_params=_params("parallel"),
    )(block, shards, after)


HBM = pl.BlockSpec(memory_space=pltpu.HBM)
SEM = pl.BlockSpec(memory_space=pltpu.SEMAPHORE)
FLOWS = pltpu.SideEffectType.DATAFLOW_SIDE_EFFECTING


def _in_hbm(a):
    return pltpu.with_memory_space_constraint(a, pltpu.HBM)


def _hbm_like(bufs):
    return [pltpu.HBM(b.shape, b.dtype) for b in bufs]


def _copies_start(name, bufs, plan, n, after):
    nb = len(bufs)

    def body(*refs):
        send_sems, recv_sems, token = refs[nb + 1], refs[nb + 2], refs[-1]
        for cp in plan(refs[:nb], send_sems, recv_sems):
            cp.start()
        token[...] = jnp.zeros_like(token)

    out = pl.pallas_call(
        body, name=name, in_specs=[HBM] * nb + [ANY],
        out_specs=[SEM, SEM] + [HBM] * nb + [pl.BlockSpec(memory_space=pltpu.VMEM)],
        out_shape=[pltpu.SemaphoreType.DMA((n,)), pltpu.SemaphoreType.DMA((n,))] + _hbm_like(bufs)
        + [jax.ShapeDtypeStruct((8, 128), F32)],
        input_output_aliases={i: 2 + i for i in range(nb)},
        compiler_params=pltpu.CompilerParams(has_side_effects=FLOWS),
    )(*[_in_hbm(b) for b in bufs], after)
    return (out[0], out[1]), list(out[2:2 + nb]), out[-1]


def _copies_wait(name, bufs, sems, after, plan):
    nb = len(bufs)

    def body(*refs):
        for cp in plan(refs[:nb], refs[nb], refs[nb + 1]):
            cp.wait_send()
            cp.wait_recv()

    out = pl.pallas_call(
        body, name=name, in_specs=[HBM] * nb + [SEM, SEM, ANY], out_specs=[HBM] * nb,
        out_shape=_hbm_like(bufs), input_output_aliases={i: i for i in range(nb)},
        compiler_params=pltpu.CompilerParams(has_side_effects=FLOWS),
    )(*bufs, *sems, after)
    return list(out)


def _copies_wait_start(name, bufs, sems, after, plan, next_plan, n_next):
    nb = len(bufs)

    def body(*refs):
        ins = refs[:nb]
        for cp in plan(ins, refs[nb], refs[nb + 1]):
            cp.wait_send()
            cp.wait_recv()
        send_sems, recv_sems, token = refs[nb + 3], refs[nb + 4], refs[-1]
        for cp in next_plan(ins, send_sems, recv_sems):
            cp.start()
        token[...] = jnp.zeros_like(token)

    out = pl.pallas_call(
        body, name=name, in_specs=[HBM] * nb + [SEM, SEM, ANY],
        out_specs=[SEM, SEM] + [HBM] * nb + [pl.BlockSpec(memory_space=pltpu.VMEM)],
        out_shape=[pltpu.SemaphoreType.DMA((n_next,)), pltpu.SemaphoreType.DMA((n_next,))] + _hbm_like(bufs)
        + [jax.ShapeDtypeStruct((8, 128), F32)],
        input_output_aliases={i: 2 + i for i in range(nb)},
        compiler_params=pltpu.CompilerParams(has_side_effects=FLOWS),
    )(*bufs, *sems, after)
    return (out[0], out[1]), list(out[2:2 + nb]), out[-1]


def _gather_plans(kinds):
    nt = len(kinds)

    def slot(refs, t, px, py, pc):
        return _shard_of(refs[t], kinds[t], 4 * px + 2 * py + pc)

    def to_chips(refs, send_sems, recv_sems):
        x, y, c = _coords()
        peers = [(x, y, 1 - c), (1 - x, y, c), (x, 1 - y, c), (1 - x, 1 - y, c)]
        return [pltpu.make_async_remote_copy(
            src_ref=slot(refs, t, x, y, c), dst_ref=slot(refs, t, x, y, c), send_sem=send_sems.at[4 * t + k],
            recv_sem=recv_sems.at[4 * t + k], device_id=peer, device_id_type=MESH_ID)
            for t in range(nt) for k, peer in enumerate(peers)]

    def to_sibling(refs, send_sems, recv_sems):
        x, y, c = _coords()
        chips = [(1 - x, y), (x, 1 - y), (1 - x, 1 - y)]
        return [pltpu.make_async_remote_copy(
            src_ref=slot(refs, t, *chip, c), dst_ref=slot(refs, t, *chip, c), send_sem=send_sems.at[3 * t + j],
            recv_sem=recv_sems.at[3 * t + j], device_id=(x, y, 1 - c), device_id_type=MESH_ID)
            for t in range(nt) for j, chip in enumerate(chips)]

    return to_chips, to_sibling


def _shard_shape(full, kind):
    if kind == "col":
        return (full.shape[0], full.shape[1] // N_DEV)
    return (full.shape[0] // N_DEV, full.shape[1])


def _scatter_plan(kinds):
    nt = len(kinds)

    def plan(refs, send_sems, recv_sems):
        x, y, c = _coords()
        copies = []
        for t in range(nt):
            for k in range(1, N_DEV):
                px = 1 - x if (k >> 2) & 1 else x
                py = 1 - y if (k >> 1) & 1 else y
                pc = 1 - c if k & 1 else c
                copies.append(pltpu.make_async_remote_copy(
                    src_ref=_shard_of(refs[t], kinds[t], 4 * px + 2 * py + pc),
                    dst_ref=refs[nt + t].at[4 * x + 2 * y + c],
                    send_sem=send_sems.at[7 * t + k - 1], recv_sem=recv_sems.at[7 * t + k - 1],
                    device_id=(px, py, pc), device_id_type=MESH_ID))
        return copies
    return plan


def _partial_specs(full, kind, tr):
    rows, cols = _shard_shape(full, kind)
    steps = rows // tr
    if kind == "col":
        own = pl.BlockSpec((tr, cols), lambda i, w: (i, w[0]))
    else:
        own = pl.BlockSpec((tr, cols), lambda i, w: (w[0] * steps + i, 0))
    return [own] + [pl.BlockSpec((None, tr, cols), lambda i, w, k=k: (w[k], i, 0)) for k in range(1, N_DEV)]


def _all_reduce_small(vec, after):
    R = vec.shape[0]

    def body(v_ref, after_ref, o_ref, recv_ref, send_sems, recv_sems):
        x, y, c = _coords()
        me = 4 * x + 2 * y + c
        recv_ref[me] = v_ref[...]
        copies = []
        for k in range(1, N_DEV):
            bx, by, bc = (k >> 2) & 1, (k >> 1) & 1, k & 1
            peer = (1 - x if bx else x, 1 - y if by else y, 1 - c if bc else c)
            copies.append(pltpu.make_async_remote_copy(
                src_ref=v_ref, dst_ref=recv_ref.at[me],
                send_sem=send_sems.at[k - 1], recv_sem=recv_sems.at[k - 1],
                device_id=peer, device_id_type=MESH_ID))
        for cp in copies:
            cp.start()
        for cp in copies:
            cp.wait()
        total = recv_ref[0]
        for p in range(1, N_DEV):
            total = total + recv_ref[p]
        o_ref[...] = total

    return pl.pallas_call(
        body, name="all_reduce_small",
        in_specs=[pl.BlockSpec(memory_space=pltpu.VMEM), ANY], out_specs=pl.BlockSpec(memory_space=pltpu.VMEM),
        out_shape=jax.ShapeDtypeStruct((R, 128), F32),
        scratch_shapes=[pltpu.VMEM((N_DEV, R, 128), F32), pltpu.SemaphoreType.DMA((N_DEV - 1,)),
                        pltpu.SemaphoreType.DMA((N_DEV - 1,))],
        compiler_params=pltpu.CompilerParams(vmem_limit_bytes=VMEM_LIMIT),
    )(vec, after)


def _adamw_math(w, g, m, v):
    m = ADAM_B1 * m + (1.0 - ADAM_B1) * g
    v = ADAM_B2 * v + (1.0 - ADAM_B2) * (g * g)
    m_hat = m / (1.0 - ADAM_B1 ** ADAM_STEP)
    v_hat = v / (1.0 - ADAM_B2 ** ADAM_STEP)
    delta = -ADAM_LR * (m_hat / (jnp.sqrt(v_hat) + ADAM_EPS) + ADAM_WD * w)
    return delta, m, v


def _adamw(w, grads, kind, where, m, v, name):
    layers, rows, cols = w.shape
    tr = _tile(rows, 128)
    out = None
    for layer, (grad, land) in enumerate(grads):
        def body(w_ref, *refs):
            parts, (x_ref, m_ref, v_ref) = refs[:N_DEV], refs[N_DEV:N_DEV + 3]
            g_ref, d_ref, mo_ref, vo_ref = refs[-4:]
            g = parts[0][...].astype(F32)
            for p_ref in parts[1:]:
                g = g + p_ref[...].astype(F32)
            g_ref[...] = g
            d_ref[...], mo_ref[...], vo_ref[...] = _adamw_math(x_ref[...], g, m_ref[...], v_ref[...])

        blk = pl.BlockSpec((None, tr, cols), lambda i, w_, layer=layer: (layer, i, 0))
        earlier = [] if out is None else list(out)
        out = pl.pallas_call(
            body, name=f"{name}_{layer}",
            grid_spec=pltpu.PrefetchScalarGridSpec(
                num_scalar_prefetch=1, grid=(rows // tr,),
                in_specs=_partial_specs(grad, kind, tr) + [blk] * 3 + [ANY] * len(earlier),
                out_specs=[blk] * 4),
            out_shape=[jax.ShapeDtypeStruct((layers, rows, cols), F32)] * 4,
            input_output_aliases={N_DEV + 4 + k: k for k in range(len(earlier))},
            compiler_params=_params("parallel"),
        )(where, grad, *[land] * (N_DEV - 1), w, m, v, *earlier)
    return out


def _adamw_small(w, g, m, v):
    def body(w_ref, g_ref, m_ref, v_ref, d_ref, mo_ref, vo_ref):
        d_ref[...], mo_ref[...], vo_ref[...] = _adamw_math(w_ref[...], g_ref[...], m_ref[...], v_ref[...])

    whole = pl.BlockSpec(memory_space=pltpu.VMEM)
    return pl.pallas_call(
        body, name="adamw_small", in_specs=[whole] * 4, out_specs=[whole] * 3,
        out_shape=[jax.ShapeDtypeStruct(w.shape, F32)] * 3,
        compiler_params=pltpu.CompilerParams(vmem_limit_bytes=VMEM_LIMIT),
    )(w, g, m, v)


def _pack(arrays):
    rows = []
    for a in arrays:
        flat = a.reshape(-1)
        pad = (-flat.shape[0]) % 1024
        rows.append(jnp.pad(flat, (0, pad)).reshape(-1, 128))
    return jnp.concatenate(rows, axis=0)


def _unpack(packed, like):
    out, r = [], 0
    for a in like:
        n = math.prod(a.shape)
        nr = (n + 1023) // 1024 * 8
        out.append(packed[r:r + nr].reshape(-1)[:n].reshape(a.shape))
        r += nr
    return out


KIND = {"ab_in": "col", "ab_out": "row", "sb_in": "col", "sb_out": "row",
        "w1_0": "col", "w1_1": "col", "w2_0": "row", "w2_1": "row"}
GATHERS = {"ab_in": ("ab_in",), "ab_out": ("ab_out",), "ffn0": ("w1_0", "w2_0"),
           "sb": ("sb_in", "sb_out"), "ffn1": ("w1_1", "w2_1")}


class _Exchange:
    def __init__(self, shards):
        x, y, c = _coords()
        me = (4 * x + 2 * y + c).astype(jnp.int32)
        self.where = jnp.stack([jnp.bitwise_xor(me, k) for k in range(N_DEV)])
        self.full = {}
        self.tokens = []
        self.gathers = {}
        self.scatters = {}
        self.settled = {}
        block = me.reshape(1)
        after = block
        for key, group in GATHERS.items():
            for n in group:
                self.full[n] = _place(*shards[n], KIND[n], block, after, f"place_{n}")
            to_chips, to_sibling = _gather_plans([KIND[n] for n in group])
            bufs = [self.full[n] for n in group]
            sems, bufs, after = _copies_start(f"gather_start_{key}", bufs, to_chips, 4 * len(group), after)
            self.tokens.append(after)
            self.gathers[key] = (group, sems, bufs, to_chips, to_sibling)

    def tie(self, small):
        for token in self.tokens:
            small = small + token[0:1, 0:1]
        self.tokens = []
        return small

    def started(self):
        return tuple(self.tokens)

    def weight(self, name):
        return self.full[name]

    def arrive(self, key, after):
        group, sems, bufs, to_chips, to_sibling = self.gathers[key]
        sems, bufs, token = _copies_wait_start(f"gather_pass_{key}", bufs, sems, after, to_chips, to_sibling,
                                               3 * len(group))
        self.tokens.append(token)
        self.gathers[key] = (group, sems, bufs, token, to_sibling)

    def land(self, key, after=None):
        group, sems, bufs, token, to_sibling = self.gathers.pop(key)
        after = token if after is None else after
        self.full.update(zip(group, _copies_wait(f"gather_done_{key}", bufs, sems, after, to_sibling)))

    def reduce(self, key, grads):
        names = list(grads)
        kinds = [KIND[n] for n in names]
        full = [grads[n] for n in names]
        lands = [lax.empty((N_DEV,) + _shard_shape(g, k), BF16) for g, k in zip(full, kinds)]
        plan = _scatter_plan(kinds)
        sems, bufs, token = _copies_start(f"scatter_start_{key}", full + lands, plan, (N_DEV - 1) * len(names),
                                          full[-1])
        self.tokens.append(token)
        self.scatters[key] = (names, sems, bufs, plan)

    def settle(self, keys, after):
        for key in keys:
            names, sems, bufs, plan = self.scatters.pop(key)
            bufs = _copies_wait(f"scatter_done_{key}", bufs, sems, after, plan)
            self.settled.update({n: t for n, *t in zip(names, bufs[:len(names)], bufs[len(names):])})
        return self.settled


SMALL = ("norm_pre_mix", "norm_post_mix", "norm_pre_ffn", "norm_post_ffn", "sgu_ln_g", "sgu_ln_b", "sgu_w", "sgu_b")
ORDER = ("norm_pre_mix", "norm_post_mix", "norm_pre_ffn", "norm_post_ffn", "ab_w_in", "sgu_ln_g", "sgu_ln_b", "sgu_w",
         "sgu_b", "ab_w_out", "sb_w_in", "sb_w_out", "ffn_w1", "ffn_w2")


def kernel(x, norm_pre_mix, norm_post_mix, norm_pre_ffn, norm_post_ffn, ab_w_in, sgu_ln_g, sgu_ln_b, sgu_w, sgu_b, ab_w_out, sb_w_in, sb_w_out, ffn_w1, ffn_w2, loss_target, m_norm_pre_mix, m_norm_post_mix, m_norm_pre_ffn, m_norm_post_ffn, m_ab_w_in, m_sgu_ln_g, m_sgu_ln_b, m_sgu_w, m_sgu_b, m_ab_w_out, m_sb_w_in, m_sb_w_out, m_ffn_w1, m_ffn_w2, v_norm_pre_mix, v_norm_post_mix, v_norm_pre_ffn, v_norm_post_ffn, v_ab_w_in, v_sgu_ln_g, v_sgu_ln_b, v_sgu_w, v_sgu_b, v_ab_w_out, v_sb_w_in, v_sb_w_out, v_ffn_w1, v_ffn_w2):
    W = dict(norm_pre_mix=norm_pre_mix, norm_post_mix=norm_post_mix, norm_pre_ffn=norm_pre_ffn,
             norm_post_ffn=norm_post_ffn, ab_w_in=ab_w_in, sgu_ln_g=sgu_ln_g, sgu_ln_b=sgu_ln_b, sgu_w=sgu_w,
             sgu_b=sgu_b, ab_w_out=ab_w_out, sb_w_in=sb_w_in, sb_w_out=sb_w_out, ffn_w1=ffn_w1, ffn_w2=ffn_w2)
    M = dict(norm_pre_mix=m_norm_pre_mix, norm_post_mix=m_norm_post_mix, norm_pre_ffn=m_norm_pre_ffn,
             norm_post_ffn=m_norm_post_ffn, ab_w_in=m_ab_w_in, sgu_ln_g=m_sgu_ln_g, sgu_ln_b=m_sgu_ln_b,
             sgu_w=m_sgu_w, sgu_b=m_sgu_b, ab_w_out=m_ab_w_out, sb_w_in=m_sb_w_in, sb_w_out=m_sb_w_out,
             ffn_w1=m_ffn_w1, ffn_w2=m_ffn_w2)
    V = dict(norm_pre_mix=v_norm_pre_mix, norm_post_mix=v_norm_post_mix, norm_pre_ffn=v_norm_pre_ffn,
             norm_post_ffn=v_norm_post_ffn, ab_w_in=v_ab_w_in, sgu_ln_g=v_sgu_ln_g, sgu_ln_b=v_sgu_ln_b,
             sgu_w=v_sgu_w, sgu_b=v_sgu_b, ab_w_out=v_ab_w_out, sb_w_in=v_sb_w_in, sb_w_out=v_sb_w_out,
             ffn_w1=v_ffn_w1, ffn_w2=v_ffn_w2)

    shards = {"ab_in": (ab_w_in, 0), "ab_out": (ab_w_out, 0), "w1_0": (ffn_w1, 0), "w2_0": (ffn_w2, 0),
              "sb_in": (sb_w_in, 0), "sb_out": (sb_w_out, 0), "w1_1": (ffn_w1, 1), "w2_1": (ffn_w2, 1)}
    comm = _Exchange(shards)
    norms = (norm_pre_mix, norm_post_mix, norm_pre_ffn, norm_post_ffn)
    sgu = (sgu_ln_g, sgu_ln_b, sgu_w[0], sgu_b[0])
    loss, dx, small = _local_step(x[0], loss_target[0], norms, sgu, comm)
    loss = lax.psum(loss, MESH_AXES)

    out = {}

    def update(name, layers, landed):
        out[name] = _adamw(W[name], [landed[n] for n in layers], KIND[layers[0]], comm.where, M[name], V[name],
                           f"adamw_{name}")

    landed = comm.settle(("ffn1", "sb", "ffn0"), after=dx)
    for name, layers in (("sb_w_in", ["sb_in"]), ("sb_w_out", ["sb_out"]), ("ffn_w1", ["w1_0", "w1_1"]),
                         ("ffn_w2", ["w2_0", "w2_1"])):
        update(name, layers, landed)
    small_g = [small["pre_mix"], small["post_mix"], small["pre_ffn"], small["post_ffn"], small["ln_g"],
               small["ln_b"], small["w_s"][None], small["b_s"][None]]
    g_small = _all_reduce_small(_pack(small_g), out["ffn_w2"][3])
    landed = comm.settle(("ab_out", "ab_in"), after=g_small)
    update("ab_w_out", ["ab_out"], landed)
    update("ab_w_in", ["ab_in"], landed)
    res = _adamw_small(_pack([W[n] for n in SMALL]), g_small, _pack([M[n] for n in SMALL]),
                       _pack([V[n] for n in SMALL]))
    like = [W[n] for n in SMALL]
    for n, *vals in zip(SMALL, *[_unpack(r, like) for r in [g_small] + list(res)]):
        out[n] = vals

    return (loss, dx[None], *[out[n][0] for n in ORDER], *[out[n][1] for n in ORDER],
            *[out[n][2] for n in ORDER], *[out[n][3] for n in ORDER])
```

```python
import math

import jax
import jax.numpy as jnp
from jax import lax
from jax.experimental import pallas as pl
from jax.experimental.pallas import tpu as pltpu

F32 = jnp.float32
BF16 = jnp.bfloat16

HEAD_DIM = 128
CHUNK = 128
ATT_BLOCK = 128
DILATED_PAIRS = ((128, 1), (512, 4), (2048, 16))
RMS_EPS = 1e-6
LN_EPS = 1e-5
ADAM_LR = 0.001
ADAM_B1 = 0.9
ADAM_B2 = 0.999
ADAM_EPS = 1e-08
ADAM_WD = 0.01
ADAM_STEP = 10
N_DEV = 8
MESH_AXES = ("x", "y", "c")
MASKED = -1e30

V7X_VMEM_BYTES = 64 * 1024 * 1024
VMEM_LIMIT = V7X_VMEM_BYTES - 8 * 1024 * 1024

NN = (((1,), (0,)), ((), ()))
NT = (((1,), (1,)), ((), ()))
TN = (((0,), (0,)), ((), ()))


def _params(*sem):
    return pltpu.CompilerParams(dimension_semantics=sem, vmem_limit_bytes=VMEM_LIMIT)


def _dot(a, b, dims=NN):
    return lax.dot_general(a, b, dims, preferred_element_type=F32)


def _tile(n, preferred):
    if n <= preferred:
        return n
    t = preferred - preferred % 128
    while n % t:
        t -= 128
    assert t > 0, (n, preferred)
    return t


def _matmul(a, b, *, mode, name, out_dtype=F32, tm=1024, tn=1024, tk=2048, epi=None, extras=(), after=()):
    if mode == "nn":
        (M, K), N = a.shape, b.shape[1]
    elif mode == "nt":
        (M, K), N = a.shape, b.shape[0]
    else:
        (K, M), N = a.shape, b.shape[1]
    tm, tn, tk = _tile(M, tm), _tile(N, tn), _tile(K, tk)
    nk = K // tk
    if mode == "tn":
        a_spec = pl.BlockSpec((tk, tm), lambda i, j, k: (k, i))
    else:
        a_spec = pl.BlockSpec((tm, tk), lambda i, j, k: (i, k))
    if mode == "nt":
        b_spec = pl.BlockSpec((tn, tk), lambda i, j, k: (j, k))
    else:
        b_spec = pl.BlockSpec((tk, tn), lambda i, j, k: (k, j))
    o_spec = pl.BlockSpec((tm, tn), lambda i, j, k: (i, j))
    dims = {"nn": NN, "nt": NT, "tn": TN}[mode]
    n_extra = len(extras)
    n_in = n_extra + len(after)

    def finish(acc, refs):
        j = pl.program_id(1)
        if epi is None:
            return acc
        return epi(acc, j, *[r[...] for r in refs])

    if nk == 1:
        def body(a_ref, b_ref, *rest):
            o_ref = rest[n_in]
            acc = _dot(a_ref[...], b_ref[...], dims)
            o_ref[...] = finish(acc, rest[:n_extra]).astype(o_ref.dtype)
        scratch = []
    else:
        def body(a_ref, b_ref, *rest):
            o_ref, acc_ref = rest[n_in], rest[n_in + 1]
            k = pl.program_id(2)

            @pl.when(k == 0)
            def _():
                acc_ref[...] = jnp.zeros_like(acc_ref)

            acc_ref[...] += _dot(a_ref[...], b_ref[...], dims)

            @pl.when(k == nk - 1)
            def _():
                o_ref[...] = finish(acc_ref[...], rest[:n_extra]).astype(o_ref.dtype)
        scratch = [pltpu.VMEM((tm, tn), F32)]

    return pl.pallas_call(
        body,
        name=name,
        grid=(M // tm, N // tn, nk),
        in_specs=[a_spec, b_spec] + [o_spec] * n_extra + [ANY] * len(after),
        out_specs=o_spec,
        out_shape=jax.ShapeDtypeStruct((M, N), out_dtype),
        scratch_shapes=scratch,
        compiler_params=_params("parallel", "parallel", "arbitrary"),
    )(a, b, *extras, *after)


ROWS = 256


def _rms(x):
    return lax.rsqrt(jnp.mean(x * x, axis=-1, keepdims=True) + RMS_EPS)


def _prenorm(x, g, name):
    T, D = x.shape

    def body(x_ref, g_ref, h_ref):
        xv = x_ref[...]
        h_ref[...] = (xv * _rms(xv) * g_ref[...]).astype(BF16)

    row = pl.BlockSpec((ROWS, D), lambda i: (i, 0))
    vec = pl.BlockSpec((1, D), lambda i: (0, 0))
    return pl.pallas_call(
        body, name=name, grid=(T // ROWS,), in_specs=[row, vec], out_specs=row,
        out_shape=jax.ShapeDtypeStruct((T, D), BF16), compiler_params=_params("parallel"),
    )(x, g)


def _postnorm_prenorm(x, y, g_post, g_pre, name):
    T, D = x.shape

    def body(x_ref, y_ref, gp_ref, gn_ref, xo_ref, h_ref):
        yv = y_ref[...]
        xn = x_ref[...] + yv * _rms(yv) * gp_ref[...]
        xo_ref[...] = xn
        h_ref[...] = (xn * _rms(xn) * gn_ref[...]).astype(BF16)

    row = pl.BlockSpec((ROWS, D), lambda i: (i, 0))
    vec = pl.BlockSpec((1, D), lambda i: (0, 0))
    return pl.pallas_call(
        body, name=name, grid=(T // ROWS,), in_specs=[row, row, vec, vec], out_specs=[row, row],
        out_shape=[jax.ShapeDtypeStruct((T, D), F32), jax.ShapeDtypeStruct((T, D), BF16)],
        compiler_params=_params("parallel"),
    )(x, y, g_post, g_pre)


def _postnorm_grads(dn, yh, r, g):
    gd = dn * g
    return r * (gd - yh * jnp.mean(yh * gd, axis=-1, keepdims=True)), dn * yh


def _postnorm_loss(x, y, g_post, target, name):
    T, D = x.shape

    def body(x_ref, y_ref, gp_ref, t_ref, loss_ref, dx_ref, dy_ref, dg_ref):
        @pl.when(pl.program_id(0) == 0)
        def _():
            loss_ref[...] = jnp.zeros_like(loss_ref)
            dg_ref[...] = jnp.zeros_like(dg_ref)

        yv = y_ref[...]
        r = _rms(yv)
        yh = yv * r
        err = x_ref[...] + yh * gp_ref[...] - t_ref[...]
        dx = err * (1.0 / D)
        dx_ref[...] = dx
        loss_ref[...] += 0.5 * jnp.sum(jnp.sum(err * err, axis=-1, keepdims=True) * (1.0 / D))
        dy, dg = _postnorm_grads(dx, yh, r, gp_ref[...])
        dy_ref[...] = dy.astype(BF16)
        dg_ref[...] += jnp.sum(dg, axis=0, keepdims=True)

    row = pl.BlockSpec((ROWS, D), lambda i: (i, 0))
    vec = pl.BlockSpec((1, D), lambda i: (0, 0))
    acc = pl.BlockSpec((8, 128), lambda i: (0, 0))
    return pl.pallas_call(
        body, name=name, grid=(T // ROWS,), in_specs=[row, row, vec, row], out_specs=[acc, row, row, vec],
        out_shape=[jax.ShapeDtypeStruct((8, 128), F32), jax.ShapeDtypeStruct((T, D), F32),
                   jax.ShapeDtypeStruct((T, D), BF16), jax.ShapeDtypeStruct((1, D), F32)],
        compiler_params=_params("arbitrary"),
    )(x, y, g_post, target)


def _norm_bwd_pair(dx_out, dh, x, g_pre, y_prev, g_post_prev, name):
    T, D = x.shape

    def body(dxo_ref, dh_ref, x_ref, g_ref, y_ref, gp_ref, dx_ref, dg_ref, dy_ref, dgp_ref):
        @pl.when(pl.program_id(0) == 0)
        def _():
            dg_ref[...] = jnp.zeros_like(dg_ref)
            dgp_ref[...] = jnp.zeros_like(dgp_ref)

        xv, dhv = x_ref[...], dh_ref[...]
        r = _rms(xv)
        xh = xv * r
        gd = dhv * g_ref[...]
        dx = dxo_ref[...] + r * (gd - xh * jnp.mean(xh * gd, axis=-1, keepdims=True))
        dx_ref[...] = dx
        dg_ref[...] += jnp.sum(dhv * xh, axis=0, keepdims=True)
        yv = y_ref[...]
        ry = _rms(yv)
        dy, dgp = _postnorm_grads(dx, yv * ry, ry, gp_ref[...])
        dy_ref[...] = dy.astype(BF16)
        dgp_ref[...] += jnp.sum(dgp, axis=0, keepdims=True)

    row = pl.BlockSpec((ROWS, D), lambda i: (i, 0))
    vec = pl.BlockSpec((1, D), lambda i: (0, 0))
    return pl.pallas_call(
        body, name=name, grid=(T // ROWS,), in_specs=[row, row, row, vec, row, vec],
        out_specs=[row, vec, row, vec],
        out_shape=[jax.ShapeDtypeStruct((T, D), F32), jax.ShapeDtypeStruct((1, D), F32),
                   jax.ShapeDtypeStruct((T, D), BF16), jax.ShapeDtypeStruct((1, D), F32)],
        compiler_params=_params("arbitrary"),
    )(dx_out, dh, x, g_pre, y_prev, g_post_prev)


def _prenorm_bwd(dx_out, dh, x, g_pre, name):
    T, D = x.shape

    def body(dxo_ref, dh_ref, x_ref, g_ref, dx_ref, dg_ref):
        @pl.when(pl.program_id(0) == 0)
        def _():
            dg_ref[...] = jnp.zeros_like(dg_ref)

        xv, dhv = x_ref[...], dh_ref[...]
        r = _rms(xv)
        xh = xv * r
        gd = dhv * g_ref[...]
        dx_ref[...] = dxo_ref[...] + r * (gd - xh * jnp.mean(xh * gd, axis=-1, keepdims=True))
        dg_ref[...] += jnp.sum(dhv * xh, axis=0, keepdims=True)

    row = pl.BlockSpec((ROWS, D), lambda i: (i, 0))
    vec = pl.BlockSpec((1, D), lambda i: (0, 0))
    return pl.pallas_call(
        body, name=name, grid=(T // ROWS,), in_specs=[row, row, row, vec], out_specs=[row, vec],
        out_shape=[jax.ShapeDtypeStruct((T, D), F32), jax.ShapeDtypeStruct((1, D), F32)],
        compiler_params=_params("arbitrary"),
    )(dx_out, dh, x, g_pre)


_INV_SQRT2 = 1.0 / math.sqrt(2.0)
_INV_SQRT2PI = 1.0 / math.sqrt(2.0 * math.pi)


def _gelu(x):
    return 0.5 * x * (1.0 + lax.erf(x * _INV_SQRT2))


def _gelu_grad(x):
    return 0.5 * (1.0 + lax.erf(x * _INV_SQRT2)) + x * jnp.exp(-0.5 * x * x) * _INV_SQRT2PI


def _layernorm_stats(x):
    mu = jnp.mean(x, axis=-1, keepdims=True)
    xc = x - mu
    rstd = lax.rsqrt(jnp.mean(xc * xc, axis=-1, keepdims=True) + LN_EPS)
    return xc * rstd, rstd


def _tril_mask():
    i = lax.broadcasted_iota(jnp.int32, (CHUNK, CHUNK), 0)
    j = lax.broadcasted_iota(jnp.int32, (CHUNK, CHUNK), 1)
    return j <= i


SGU_ROWS = 512


def _sgu_fwd(z, ln_g, ln_b, w_s, b_t, name):
    T = z.shape[0]
    A = ln_g.shape[1]
    G = A // 128
    rows = min(SGU_ROWS, T)

    def body(u_ref, v_ref, g_ref, b_ref, w_ref, bt_ref, o_ref):
        mask = _tril_mask()
        for c in range(rows // CHUNK):
            rs = pl.ds(c * CHUNK, CHUNK)
            xh, _ = _layernorm_stats(_gelu(v_ref[rs, :]))
            vn = (xh * g_ref[...] + b_ref[...]).astype(BF16)
            for g in range(G):
                cs = pl.ds(g * 128, 128)
                w = jnp.where(mask, w_ref[g], 0.0).astype(BF16)
                mixed = _dot(w, vn[:, g * 128:(g + 1) * 128]) + bt_ref[:, g:g + 1]
                o_ref[rs, cs] = (_gelu(u_ref[rs, cs]) * mixed).astype(BF16)

    return pl.pallas_call(
        body, name=name, grid=(T // rows,),
        in_specs=[
            pl.BlockSpec((rows, A), lambda i: (i, 0)),
            pl.BlockSpec((rows, A), lambda i: (i, 1)),
            pl.BlockSpec((1, A), lambda i: (0, 0)),
            pl.BlockSpec((1, A), lambda i: (0, 0)),
            pl.BlockSpec((G, CHUNK, CHUNK), lambda i: (0, 0, 0)),
            pl.BlockSpec((CHUNK, G), lambda i: (0, 0)),
        ],
        out_specs=pl.BlockSpec((rows, A), lambda i: (i, 0)),
        out_shape=jax.ShapeDtypeStruct((T, 2 * A), BF16),
        compiler_params=_params("parallel"),
    )(z, z, ln_g, ln_b, w_s, b_t)


def _sgu_bwd(z, dcat, ln_g, ln_b, w_s, b_t, name):
    T = z.shape[0]
    A = ln_g.shape[1]
    G = A // 128
    rows = min(SGU_ROWS, T)

    def body(u_ref, v_ref, da_ref, g_ref, b_ref, w_ref, bt_ref, dz_ref, dg_ref, db_ref, dw_ref, dbt_ref, dvn_ref):
        @pl.when(pl.program_id(0) == 0)
        def _():
            dg_ref[...] = jnp.zeros_like(dg_ref)
            db_ref[...] = jnp.zeros_like(db_ref)
            dw_ref[...] = jnp.zeros_like(dw_ref)
            dbt_ref[...] = jnp.zeros_like(dbt_ref)

        mask = _tril_mask()
        for c in range(rows // CHUNK):
            rs = pl.ds(c * CHUNK, CHUNK)
            vv = v_ref[rs, :]
            gv = _gelu(vv)
            xh, rstd = _layernorm_stats(gv)
            vn = (xh * g_ref[...] + b_ref[...]).astype(BF16)
            for g in range(G):
                cs = pl.ds(g * 128, 128)
                w = jnp.where(mask, w_ref[g], 0.0).astype(BF16)
                vg = vn[:, g * 128:(g + 1) * 128]
                mixed = _dot(w, vg) + bt_ref[:, g:g + 1]
                uu = u_ref[rs, cs]
                da = da_ref[rs, cs]
                dz_ref[rs, cs] = (da * mixed * _gelu_grad(uu)).astype(BF16)
                dm = da * _gelu(uu)
                dmb = dm.astype(BF16)
                dbt_ref[:, g:g + 1] += jnp.sum(dm, axis=1, keepdims=True)
                dw_ref[g] += jnp.where(mask, _dot(dmb, vg, NT), 0.0)
                dvn_ref[:, cs] = _dot(w, dmb, TN)
            dvn = dvn_ref[...]
            dg_ref[...] += jnp.sum(dvn * xh, axis=0, keepdims=True)
            db_ref[...] += jnp.sum(dvn, axis=0, keepdims=True)
            dxh = dvn * g_ref[...]
            dgv = rstd * (dxh - jnp.mean(dxh, axis=-1, keepdims=True)
                          - xh * jnp.mean(dxh * xh, axis=-1, keepdims=True))
            dz_ref[rs, pl.ds(A, A)] = (dgv * _gelu_grad(vv)).astype(BF16)

    vec = pl.BlockSpec((1, A), lambda i: (0, 0))
    wsp = pl.BlockSpec((G, CHUNK, CHUNK), lambda i: (0, 0, 0))
    bsp = pl.BlockSpec((CHUNK, G), lambda i: (0, 0))
    return pl.pallas_call(
        body, name=name, grid=(T // rows,),
        in_specs=[
            pl.BlockSpec((rows, A), lambda i: (i, 0)),
            pl.BlockSpec((rows, A), lambda i: (i, 1)),
            pl.BlockSpec((rows, A), lambda i: (i, 0)),
            vec, vec, wsp, bsp,
        ],
        out_specs=[pl.BlockSpec((rows, 2 * A), lambda i: (i, 0)), vec, vec, wsp, bsp],
        out_shape=[
            jax.ShapeDtypeStruct((T, 2 * A), BF16),
            jax.ShapeDtypeStruct((1, A), F32),
            jax.ShapeDtypeStruct((1, A), F32),
            jax.ShapeDtypeStruct((G, CHUNK, CHUNK), F32),
            jax.ShapeDtypeStruct((CHUNK, G), F32),
        ],
        scratch_shapes=[pltpu.VMEM((CHUNK, A), F32)],
        compiler_params=_params("arbitrary"),
    )(z, z, dcat, ln_g, ln_b, w_s, b_t)


def _alibi_row(B, d):
    H = B // HEAD_DIM
    slopes = [d * 2.0 ** (-8.0 * (h + 1.0) / H) for h in range(H)]
    return jnp.repeat(jnp.asarray(slopes, F32), HEAD_DIM)[None, :]


def _dil_scores(q, k, slope_d, valid, dist):
    s = _dot(q, k, NT) - slope_d * dist
    return jnp.where(valid, s, MASKED)


def _dil_block(n, r, d):
    if d == 1:
        return pl.ds(pl.multiple_of(n * ATT_BLOCK, ATT_BLOCK), ATT_BLOCK)
    return pl.ds(n * (d * ATT_BLOCK) + r, ATT_BLOCK, stride=d)


def _dilated_forward(z, cat, B, name):
    T = z.shape[0]
    H = B // HEAD_DIM
    A = cat.shape[1] - B
    scale = HEAD_DIM ** -0.5
    blk = ATT_BLOCK
    chunk = _tile(T, 512)

    def body(q_ref, k_ref, v_ref, sl_ref, cat_in, cat_ref, of_ref, lt_ref, *branch):
        o_refs, l_refs = branch[:3], branch[3:]
        slope = sl_ref[:, :1]
        qi = lax.broadcasted_iota(jnp.int32, (blk, 2 * blk), 0)
        kj = lax.broadcasted_iota(jnp.int32, (blk, 2 * blk), 1)
        dist = qi + blk - kj
        band = (dist >= 0) & (dist <= blk)
        distf = dist.astype(F32)

        for b, (_, d) in enumerate(DILATED_PAIRS):
            def one(n, r, b=b, d=d):
                rows, prev = _dil_block(n, r, d), _dil_block(jnp.maximum(n - 1, 0), r, d)
                q = (q_ref[rows, :] * scale).astype(BF16)
                k = jnp.concatenate([k_ref[prev, :], k_ref[rows, :]], axis=0).astype(BF16)
                v = jnp.concatenate([v_ref[prev, :], v_ref[rows, :]], axis=0).astype(BF16)
                s = _dil_scores(q, k, slope * float(d), band & ((kj >= blk) | (n > 0)), distf)
                m = jnp.max(s, axis=-1, keepdims=True)
                p = jnp.exp(s - m)
                den = jnp.sum(p, axis=-1, keepdims=True)
                o_refs[b][rows, :] = _dot(p.astype(BF16), v) / den
                l_refs[b][rows, :] = jnp.broadcast_to(m + jnp.log(den), (blk, HEAD_DIM))

            per = max(1, 4 // d)

            def step(i, _, d=d, per=per, one=one):
                for u in range(per):
                    for r in range(d):
                        one(i * per + u, r)
                return 0

            lax.fori_loop(0, T // (d * blk * per), step, 0)

        def merge(i, _):
            rs = pl.ds(pl.multiple_of(i * chunk, chunk), chunk)
            a, b, c = l_refs[0][rs, :], l_refs[1][rs, :], l_refs[2][rs, :]
            m = jnp.maximum(jnp.maximum(a, b), c)
            ea, eb, ec = jnp.exp(a - m), jnp.exp(b - m), jnp.exp(c - m)
            tot = ea + eb + ec
            o = (ea * o_refs[0][rs, :] + eb * o_refs[1][rs, :] + ec * o_refs[2][rs, :]) / tot
            of_ref[rs, :] = o
            cat_ref[rs, :] = o.astype(BF16)
            lt_ref[rs, :] = m + jnp.log(tot)
            return 0

        lax.fori_loop(0, T // chunk, merge, 0)

    def col(unit):
        return lambda h: (0, unit * H + h)

    seq = (T, HEAD_DIM)
    out = pl.BlockSpec(seq, lambda h: (0, h))
    return pl.pallas_call(
        body, name=name, grid=(H,),
        in_specs=[pl.BlockSpec(seq, col(2)), pl.BlockSpec(seq, col(3)), pl.BlockSpec(seq, col(4)),
                  pl.BlockSpec((1, HEAD_DIM), lambda h: (0, h)), ANY],
        out_specs=[pl.BlockSpec(seq, lambda h: (0, A // HEAD_DIM + h)), out, out],
        out_shape=[jax.ShapeDtypeStruct(cat.shape, BF16), jax.ShapeDtypeStruct((T, B), F32),
                   jax.ShapeDtypeStruct((T, B), F32)],
        input_output_aliases={4: 0},
        scratch_shapes=[pltpu.VMEM(seq, F32)] * 6,
        compiler_params=_params("parallel"),
    )(z, z, z, _alibi_row(B, 1), cat)


def _dilated_backward(z, dcat, o, lse, B, name):
    T = z.shape[0]
    H = B // HEAD_DIM
    scale = HEAD_DIM ** -0.5
    blk = ATT_BLOCK
    chunk = _tile(T, 512)

    def body(q_ref, k_ref, v_ref, do_ref, o_ref, l_ref, sl_ref, dq_ref, dk_ref, dv_ref, aq_ref, ak_ref, av_ref):
        slope = sl_ref[:, :1]
        qi = lax.broadcasted_iota(jnp.int32, (blk, 2 * blk), 0)
        kj = lax.broadcasted_iota(jnp.int32, (blk, 2 * blk), 1)
        dist = qi + blk - kj
        band = (dist >= 0) & (dist <= blk)
        distf = dist.astype(F32)
        for acc in (aq_ref, ak_ref, av_ref):
            acc[...] = jnp.zeros_like(acc)

        for _, d in DILATED_PAIRS:
            def one(n, r, d=d):
                rows, prev = _dil_block(n, r, d), _dil_block(jnp.maximum(n - 1, 0), r, d)
                q = (q_ref[rows, :] * scale).astype(BF16)
                k = jnp.concatenate([k_ref[prev, :], k_ref[rows, :]], axis=0).astype(BF16)
                v = jnp.concatenate([v_ref[prev, :], v_ref[rows, :]], axis=0).astype(BF16)
                do = do_ref[rows, :]
                delta = jnp.sum(do * o_ref[rows, :], axis=-1, keepdims=True)
                do = do.astype(BF16)
                s = _dil_scores(q, k, slope * float(d), band & ((kj >= blk) | (n > 0)), distf)
                p = jnp.exp(s - l_ref[rows, :][:, :1])
                ds = (p * (_dot(do, v, NT) - delta)).astype(BF16)
                aq_ref[rows, :] += _dot(ds, k)
                dk = _dot(ds, q, TN)
                dv = _dot(p.astype(BF16), do, TN)
                ak_ref[prev, :] += dk[:blk]
                av_ref[prev, :] += dv[:blk]
                ak_ref[rows, :] += dk[blk:]
                av_ref[rows, :] += dv[blk:]

            per = max(1, 4 // d)

            def step(i, _, d=d, per=per, one=one):
                for u in range(per):
                    for r in range(d):
                        one(i * per + u, r)
                return 0

            lax.fori_loop(0, T // (d * blk * per), step, 0)

        def emit(i, _):
            rs = pl.ds(pl.multiple_of(i * chunk, chunk), chunk)
            dq_ref[rs, :] = (aq_ref[rs, :] * scale).astype(BF16)
            dk_ref[rs, :] = ak_ref[rs, :].astype(BF16)
            dv_ref[rs, :] = av_ref[rs, :].astype(BF16)
            return 0

        lax.fori_loop(0, T // chunk, emit, 0)

    def col(unit):
        return lambda h: (0, unit * H + h)

    seq = (T, HEAD_DIM)
    own = pl.BlockSpec(seq, lambda h: (0, h))
    return pl.pallas_call(
        body, name=name, grid=(H,),
        in_specs=[pl.BlockSpec(seq, col(2)), pl.BlockSpec(seq, col(3)), pl.BlockSpec(seq, col(4)),
                  pl.BlockSpec(seq, col(1)), own, own, pl.BlockSpec((1, HEAD_DIM), lambda h: (0, h))],
        out_specs=[own] * 3,
        out_shape=[jax.ShapeDtypeStruct((T, B), BF16)] * 3,
        scratch_shapes=[pltpu.VMEM(seq, F32)] * 3,
        compiler_params=_params("parallel"),
    )(z, z, z, dcat, o, lse, _alibi_row(B, 1))


def _join_columns(parts, name):
    T = parts[0].shape[0]
    widths = [p.shape[1] for p in parts]

    def body(*refs):
        o_ref, at = refs[-1], 0
        for ref, w in zip(refs[:-1], widths):
            o_ref[:, pl.ds(at, w)] = ref[...]
            at += w

    return pl.pallas_call(
        body, name=name, grid=(T // ROWS,),
        in_specs=[pl.BlockSpec((ROWS, w), lambda i: (i, 0)) for w in widths],
        out_specs=pl.BlockSpec((ROWS, sum(widths)), lambda i: (i, 0)),
        out_shape=jax.ShapeDtypeStruct((T, sum(widths)), BF16), compiler_params=_params("parallel"),
    )(*parts)


SB_QUERY_ROWS = 1024
SB_KEYS = 2 * ATT_BLOCK


def _tri_and_ones(pred):
    rows = lax.broadcasted_iota(jnp.int32, (2 * ATT_BLOCK, 2 * ATT_BLOCK), 0) % ATT_BLOCK
    cols = lax.broadcasted_iota(jnp.int32, (2 * ATT_BLOCK, 2 * ATT_BLOCK), 1)
    return ((cols >= ATT_BLOCK) | pred(rows, cols)).astype(BF16)


def _running(x, tri):
    hi = x.astype(BF16)
    lo = (x - hi.astype(F32)).astype(BF16)
    return _dot(jnp.concatenate([hi, lo], axis=1), tri)


def _sb_mask(query_rows, s):
    rows = lax.broadcasted_iota(jnp.int32, (query_rows - s * SB_KEYS, SB_KEYS), 0)
    cols = lax.broadcasted_iota(jnp.int32, (query_rows - s * SB_KEYS, SB_KEYS), 1)
    return cols < rows


def _log_sigmoids(z):
    ls = jnp.minimum(z, 0.0) - jnp.log(1.0 + jnp.exp(-jnp.abs(z)))
    return ls, ls - z


def _sb_fwd(qkv, W, name):
    T = qkv.shape[0]
    H = W // HEAD_DIM
    blk = ATT_BLOCK
    qb = min(SB_QUERY_ROWS, T)
    per = qb // SB_KEYS

    def body(q_ref, k_ref, v_ref, o_ref, lt_ref, acc_ref):
        i = pl.program_id(1)
        tri = _tri_and_ones(lambda r, c: r > c)
        lt_ref[...] = jnp.zeros_like(lt_ref)
        acc_ref[...] = jnp.zeros_like(acc_ref)

        def tile(j, mask, r0=0):
            ks = pl.ds(pl.multiple_of(j * SB_KEYS, SB_KEYS), SB_KEYS)
            qs = pl.ds(r0, qb - r0)
            z = _dot(q_ref[qs, :], k_ref[ks, :], NT)
            ls, lm = _log_sigmoids(z)
            if mask is not None:
                lm = jnp.where(mask, lm, 0.0)
            later = lt_ref[qs, :]
            second = _running(lm[:, blk:], tri)
            first = _running(lm[:, :blk], tri)
            after_first = later + second[:, blk:]
            a = jnp.exp(ls + jnp.concatenate([first[:, :blk] + after_first, second[:, :blk] + later], axis=1))
            if mask is not None:
                a = jnp.where(mask, a, 0.0)
            acc_ref[qs, :] += _dot(a.astype(BF16), v_ref[ks, :])
            lt_ref[qs, :] = after_first + first[:, blk:]

        for s in reversed(range(per)):
            tile(i * per + s, _sb_mask(qb, s), s * SB_KEYS)

        def step(jj, _):
            for s in range(per):
                tile((i - jj) * per - 1 - s, None)
            return 0

        lax.fori_loop(0, i, step, 0)
        o_ref[...] = acc_ref[...].astype(BF16)

    qs = pl.BlockSpec((qb, HEAD_DIM), lambda h, i: (i, h))
    return pl.pallas_call(
        body, name=name, grid=(H, T // qb),
        in_specs=[qs, pl.BlockSpec((T, HEAD_DIM), lambda h, i: (0, H + h)),
                  pl.BlockSpec((T, HEAD_DIM), lambda h, i: (0, 2 * H + h))],
        out_specs=[qs, qs],
        out_shape=[jax.ShapeDtypeStruct((T, W), BF16), jax.ShapeDtypeStruct((T, W), F32)],
        scratch_shapes=[pltpu.VMEM((qb, HEAD_DIM), F32)],
        compiler_params=_params("parallel", "arbitrary"),
    )(qkv, qkv, qkv)


def _sb_bwd(qkv, do, ltot, W, name):
    T = qkv.shape[0]
    H = W // HEAD_DIM
    blk = ATT_BLOCK
    nkb = T // SB_KEYS
    qb = min(SB_QUERY_ROWS, T)
    per = qb // SB_KEYS

    def body(q_ref, k_ref, v_ref, do_ref, lt_ref, dq_ref, dkt_ref, dvt_ref, qt_ref, dot_ref, plm_ref, pg_ref):
        i = pl.program_id(1)

        @pl.when(i == 0)
        def _():
            dkt_ref[...] = jnp.zeros_like(dkt_ref)
            dvt_ref[...] = jnp.zeros_like(dvt_ref)

        qt_ref[...] = q_ref[...].astype(F32).T.astype(BF16)
        dot_ref[...] = do_ref[...].astype(F32).T.astype(BF16)
        upto = _tri_and_ones(lambda r, c: r <= c)
        before = _tri_and_ones(lambda r, c: r < c)
        plm_ref[...] = jnp.zeros_like(plm_ref)
        pg_ref[...] = jnp.zeros_like(pg_ref)
        dq_ref[...] = jnp.zeros_like(dq_ref)

        def tile(j, mask, r0=0):
            ks = pl.ds(pl.multiple_of(j * SB_KEYS, SB_KEYS), SB_KEYS)
            qs = pl.ds(r0, qb - r0)
            k = k_ref[ks, :]
            v = v_ref[ks, :]
            z = _dot(q_ref[qs, :], k, NT)
            ls, lm = _log_sigmoids(z)
            nsig = jnp.exp(lm)
            if mask is not None:
                lm = jnp.where(mask, lm, 0.0)
            earlier = plm_ref[qs, :]
            first = _running(lm[:, :blk], upto)
            second = _running(lm[:, blk:], upto)
            upto_first = earlier + first[:, blk:]
            seen = jnp.concatenate([first[:, :blk] + earlier, second[:, :blk] + upto_first], axis=1)
            ltot = lt_ref[qs, :]
            a = jnp.exp(ls + (jnp.concatenate([ltot, ltot], axis=1) - seen))
            if mask is not None:
                a = jnp.where(mask, a, 0.0)
            g = a * _dot(do_ref[qs, :], v, NT)
            g_earlier = pg_ref[qs, :]
            g_first = _running(g[:, :blk], before)
            g_second = _running(g[:, blk:], before)
            g_upto_first = g_earlier + g_first[:, blk:]
            gsum = jnp.concatenate([g_first[:, :blk] + g_earlier, g_second[:, :blk] + g_upto_first], axis=1)
            dz = g * nsig - gsum * jnp.exp(ls)
            if mask is not None:
                dz = jnp.where(mask, dz, 0.0)
            dzb = dz.astype(BF16)
            dkt_ref[j] += _dot(qt_ref[:, qs], dzb)
            dvt_ref[j] += _dot(dot_ref[:, qs], a.astype(BF16))
            dq_ref[qs, :] += _dot(dzb, k)
            plm_ref[qs, :] = upto_first + second[:, blk:]
            pg_ref[qs, :] = g_upto_first + g_second[:, blk:]

        def step(jj, _):
            for s in range(per):
                tile(jj * per + s, None)
            return 0

        lax.fori_loop(0, i, step, 0)
        for s in range(per):
            tile(i * per + s, _sb_mask(qb, s), s * SB_KEYS)

    qs = pl.BlockSpec((qb, HEAD_DIM), lambda h, i: (i, h))
    res = pl.BlockSpec((None, nkb, HEAD_DIM, SB_KEYS), lambda h, i: (h, 0, 0, 0))
    return pl.pallas_call(
        body, name=name, grid=(H, T // qb),
        in_specs=[qs, pl.BlockSpec((T, HEAD_DIM), lambda h, i: (0, H + h)),
                  pl.BlockSpec((T, HEAD_DIM), lambda h, i: (0, 2 * H + h)), qs, qs],
        out_specs=[qs, res, res],
        out_shape=[jax.ShapeDtypeStruct((T, W), F32)] + [jax.ShapeDtypeStruct((H, nkb, HEAD_DIM, SB_KEYS), F32)] * 2,
        scratch_shapes=[pltpu.VMEM((HEAD_DIM, qb), BF16), pltpu.VMEM((HEAD_DIM, qb), BF16),
                        pltpu.VMEM((qb, HEAD_DIM), F32), pltpu.VMEM((qb, HEAD_DIM), F32)],
        compiler_params=_params("parallel", "arbitrary"),
    )(qkv, qkv, qkv, do, ltot)


def _sb_pack(dq, dkt, dvt, name):
    T, W = dq.shape
    H = W // HEAD_DIM
    blk = SB_KEYS
    scale = HEAD_DIM ** -0.5

    def body(q_ref, kt_ref, vt_ref, o_ref):
        o_ref[:, pl.ds(0, W)] = (q_ref[...] * scale).astype(BF16)
        for h in range(H):
            o_ref[:, pl.ds(W + h * HEAD_DIM, HEAD_DIM)] = kt_ref[h].T.astype(BF16)
            o_ref[:, pl.ds(2 * W + h * HEAD_DIM, HEAD_DIM)] = vt_ref[h].T.astype(BF16)

    tr = pl.BlockSpec((H, None, HEAD_DIM, blk), lambda i: (0, i, 0, 0))
    return pl.pallas_call(
        body, name=name, grid=(T // blk,), in_specs=[pl.BlockSpec((blk, W), lambda i: (i, 0)), tr, tr],
        out_specs=pl.BlockSpec((blk, 3 * W), lambda i: (i, 0)),
        out_shape=jax.ShapeDtypeStruct((T, 3 * W), BF16), compiler_params=_params("parallel"),
    )(dq, dkt, dvt)


def _local_step(x, target, norms, sgu, comm):
    T, D = x.shape
    A = D // 2
    pre_mix, post_mix, pre_ffn, post_ffn = norms
    ln_g, ln_b, w_s, b_s = sgu
    b_t = b_s.T
    scale = HEAD_DIM ** -0.5

    def vec(p, layer):
        return comm.tie(p[layer:layer + 1])

    h0 = _prenorm(x, vec(pre_mix, 0), "prenorm0")
    comm.arrive("ab_in", after=h0)
    comm.land("ab_in")
    z = _matmul(h0, comm.weight("ab_in"), mode="nn", name="ab_in_fwd")
    comm.arrive("ab_out", after=z)
    cat = _sgu_fwd(z, ln_g, ln_b, w_s, b_t, "sgu_fwd")
    cat, o_dil, lse_dil = _dilated_forward(z, cat, A, "dilated_fwd")
    comm.land("ab_out", after=o_dil)
    y0 = _matmul(cat, comm.weight("ab_out"), mode="nn", name="ab_out_fwd")
    comm.arrive("ffn0", after=y0)
    x1, h1 = _postnorm_prenorm(x, y0, vec(post_mix, 0), vec(pre_ffn, 0), "norm_mix0")
    comm.land("ffn0", after=h1)

    def relu2(acc, j):
        r = jnp.maximum(acc, 0.0)
        return r * r

    f0 = _matmul(h1, comm.weight("w1_0"), mode="nn", name="ffn0_w1_fwd", out_dtype=BF16, epi=relu2)
    y1 = _matmul(f0, comm.weight("w2_0"), mode="nn", name="ffn0_w2_fwd")
    comm.arrive("sb", after=y1)
    x2, h2 = _postnorm_prenorm(x1, y1, vec(post_ffn, 0), vec(pre_mix, 1), "norm_ffn0")
    comm.land("sb", after=h2)

    tn_qkv = _tile(D, 1024)
    nq = D // tn_qkv

    def scale_q(acc, j):
        return jnp.where(j < nq, acc * scale, acc)

    qkv = _matmul(h2, comm.weight("sb_in"), mode="nn", name="sb_in_fwd", out_dtype=BF16, tn=tn_qkv, epi=scale_q)
    comm.arrive("ffn1", after=qkv)
    o_sb, ltot = _sb_fwd(qkv, D, "sb_fwd")
    comm.land("ffn1", after=o_sb)
    y2 = _matmul(o_sb, comm.weight("sb_out"), mode="nn", name="sb_out_fwd")
    x3, h3 = _postnorm_prenorm(x2, y2, vec(post_mix, 1), vec(pre_ffn, 1), "norm_mix1")
    f1 = _matmul(h3, comm.weight("w1_1"), mode="nn", name="ffn1_w1_fwd", out_dtype=BF16, epi=relu2)
    y3 = _matmul(f1, comm.weight("w2_1"), mode="nn", name="ffn1_w2_fwd")
    loss_tile, dx4, dy3, dg_post_ffn1 = _postnorm_loss(x3, y3, vec(post_ffn, 1), target, "norm_loss")
    loss = loss_tile[0, 0]

    def relu2_bwd(acc, j, f):
        return acc * (2.0 * jnp.sqrt(f.astype(F32)))

    def ffn_bwd(dy, h, f, layer):
        g_w2 = _matmul(f, dy, mode="tn", name=f"ffn{layer}_w2_wgrad", out_dtype=BF16)
        da = _matmul(dy, comm.weight(f"w2_{layer}"), mode="nt", name=f"ffn{layer}_w2_dgrad", out_dtype=BF16,
                     epi=relu2_bwd, extras=(f,))
        g_w1 = _matmul(h, da, mode="tn", name=f"ffn{layer}_w1_wgrad", out_dtype=BF16)
        comm.reduce(f"ffn{layer}", {f"w2_{layer}": g_w2, f"w1_{layer}": g_w1})
        return _matmul(da, comm.weight(f"w1_{layer}"), mode="nt", name=f"ffn{layer}_w1_dgrad", after=comm.started())

    dh3 = ffn_bwd(dy3, h3, f1, 1)
    dx3, dg_pre_ffn1, dy2, dg_post_mix1 = _norm_bwd_pair(dx4, dh3, x3, vec(pre_ffn, 1), y2, vec(post_mix, 1),
                                                         "ffn1_sb_norm_bwd")
    g_sb_out = _matmul(o_sb, dy2, mode="tn", name="sb_out_wgrad", out_dtype=BF16)
    do_sb = _matmul(dy2, comm.weight("sb_out"), mode="nt", name="sb_out_dgrad", out_dtype=BF16)
    dq, dk, dv = _sb_bwd(qkv, do_sb, ltot, D, "sb_bwd")
    dqkv = _sb_pack(dq, dk, dv, "sb_pack")
    g_sb_in = _matmul(h2, dqkv, mode="tn", name="sb_in_wgrad", out_dtype=BF16)
    comm.reduce("sb", {"sb_out": g_sb_out, "sb_in": g_sb_in})
    dh2 = _matmul(dqkv, comm.weight("sb_in"), mode="nt", name="sb_in_dgrad", after=comm.started())
    dx2, dg_pre_mix1, dy1, dg_post_ffn0 = _norm_bwd_pair(dx3, dh2, x2, vec(pre_mix, 1), y1, vec(post_ffn, 0),
                                                         "sb_ffn0_norm_bwd")
    dh1 = ffn_bwd(dy1, h1, f0, 0)
    dx1, dg_pre_ffn0, dy0, dg_post_mix0 = _norm_bwd_pair(dx2, dh1, x1, vec(pre_ffn, 0), y0, vec(post_mix, 0),
                                                         "ffn0_ab_norm_bwd")
    g_ab_out = _matmul(cat, dy0, mode="tn", name="ab_out_wgrad", out_dtype=BF16)
    comm.reduce("ab_out", {"ab_out": g_ab_out})
    dcat = _matmul(dy0, comm.weight("ab_out"), mode="nt", name="ab_out_dgrad", after=comm.started())
    duv, d_ln_g, d_ln_b, d_w_s, d_b_t = _sgu_bwd(z, dcat, ln_g, ln_b, w_s, b_t, "sgu_bwd")
    dz = _join_columns([duv, *_dilated_backward(z, dcat, o_dil, lse_dil, A, "dilated_bwd")], "join_dz")
    g_ab_in = _matmul(h0, dz, mode="tn", name="ab_in_wgrad", out_dtype=BF16)
    comm.reduce("ab_in", {"ab_in": g_ab_in})
    dh0 = _matmul(dz, comm.weight("ab_in"), mode="nt", name="ab_in_dgrad", after=comm.started())
    dx0, dg_pre_mix0 = _prenorm_bwd(dx1, dh0, x, vec(pre_mix, 0), "ab_prenorm_bwd")

    small = {
        "pre_mix": jnp.concatenate([dg_pre_mix0, dg_pre_mix1], axis=0),
        "post_mix": jnp.concatenate([dg_post_mix0, dg_post_mix1], axis=0),
        "pre_ffn": jnp.concatenate([dg_pre_ffn0, dg_pre_ffn1], axis=0),
        "post_ffn": jnp.concatenate([dg_post_ffn0, dg_post_ffn1], axis=0),
        "ln_g": d_ln_g, "ln_b": d_ln_b, "w_s": d_w_s, "b_s": d_b_t.T,
    }
    return loss, dx0, small


MESH_ID = pl.DeviceIdType.MESH
ANY = pl.BlockSpec(memory_space=pl.ANY)


def _coords():
    return lax.axis_index("x"), lax.axis_index("y"), lax.axis_index("c")


def _shard_of(ref, kind, p):
    if kind == "col":
        n = ref.shape[1] // N_DEV
        return ref.at[:, pl.ds(pl.multiple_of(p * n, 128), n)]
    r = ref.shape[0] // N_DEV
    return ref.at[pl.ds(pl.multiple_of(p * r, 16), r), :]


def _full_shape(shard, kind):
    if kind == "col":
        return (shard.shape[0], shard.shape[1] * N_DEV)
    return (shard.shape[0] * N_DEV, shard.shape[1])


def _place(shards, layer, kind, block, after, name):
    _, rows, cols = shards.shape
    tr = _tile(rows, 512)

    def body(b_ref, s_ref, after_ref, o_ref):
        o_ref[...] = s_ref[...].astype(BF16)

    if kind == "col":
        out = pl.BlockSpec((tr, cols), lambda i, b_ref: (i, b_ref[0]))
    else:
        out = pl.BlockSpec((tr, cols), lambda i, b_ref: (b_ref[0] * (rows // tr) + i, 0))
    return pl.pallas_call(
        body, name=name,
        grid_spec=pltpu.PrefetchScalarGridSpec(
            num_scalar_prefetch=1, grid=(rows // tr,),
            in_specs=[pl.BlockSpec((None, tr, cols), lambda i, b_ref: (layer, i, 0)), ANY], out_specs=out),
        out_shape=jax.ShapeDtypeStruct(_full_shape(shards[0], kind), BF16),
        compiler_params=_params("parallel"),
    )(block, shards, after)


HBM = pl.BlockSpec(memory_space=pltpu.HBM)
SEM = pl.BlockSpec(memory_space=pltpu.SEMAPHORE)
FLOWS = pltpu.SideEffectType.DATAFLOW_SIDE_EFFECTING


def _in_hbm(a):
    return pltpu.with_memory_space_constraint(a, pltpu.HBM)


def _hbm_like(bufs):
    return [pltpu.HBM(b.shape, b.dtype) for b in bufs]


def _copies_start(name, bufs, plan, n, after):
    nb = len(bufs)

    def body(*refs):
        send_sems, recv_sems, token = refs[nb + 1], refs[nb + 2], refs[-1]
        for cp in plan(refs[:nb], send_sems, recv_sems):
            cp.start()
        token[...] = jnp.zeros_like(token)

    out = pl.pallas_call(
        body, name=name, in_specs=[HBM] * nb + [ANY],
        out_specs=[SEM, SEM] + [HBM] * nb + [pl.BlockSpec(memory_space=pltpu.VMEM)],
        out_shape=[pltpu.SemaphoreType.DMA((n,)), pltpu.SemaphoreType.DMA((n,))] + _hbm_like(bufs)
        + [jax.ShapeDtypeStruct((8, 128), F32)],
        input_output_aliases={i: 2 + i for i in range(nb)},
        compiler_params=pltpu.CompilerParams(has_side_effects=FLOWS),
    )(*[_in_hbm(b) for b in bufs], after)
    return (out[0], out[1]), list(out[2:2 + nb]), out[-1]


def _copies_wait(name, bufs, sems, after, plan):
    nb = len(bufs)

    def body(*refs):
        for cp in plan(refs[:nb], refs[nb], refs[nb + 1]):
            cp.wait_send()
            cp.wait_recv()

    out = pl.pallas_call(
        body, name=name, in_specs=[HBM] * nb + [SEM, SEM, ANY], out_specs=[HBM] * nb,
        out_shape=_hbm_like(bufs), input_output_aliases={i: i for i in range(nb)},
        compiler_params=pltpu.CompilerParams(has_side_effects=FLOWS),
    )(*bufs, *sems, after)
    return list(out)


def _copies_wait_start(name, bufs, sems, after, plan, next_plan, n_next):
    nb = len(bufs)

    def body(*refs):
        ins = refs[:nb]
        for cp in plan(ins, refs[nb], refs[nb + 1]):
            cp.wait_send()
            cp.wait_recv()
        send_sems, recv_sems, token = refs[nb + 3], refs[nb + 4], refs[-1]
        for cp in next_plan(ins, send_sems, recv_sems):
            cp.start()
        token[...] = jnp.zeros_like(token)

    out = pl.pallas_call(
        body, name=name, in_specs=[HBM] * nb + [SEM, SEM, ANY],
        out_specs=[SEM, SEM] + [HBM] * nb + [pl.BlockSpec(memory_space=pltpu.VMEM)],
        out_shape=[pltpu.SemaphoreType.DMA((n_next,)), pltpu.SemaphoreType.DMA((n_next,))] + _hbm_like(bufs)
        + [jax.ShapeDtypeStruct((8, 128), F32)],
        input_output_aliases={i: 2 + i for i in range(nb)},
        compiler_params=pltpu.CompilerParams(has_side_effects=FLOWS),
    )(*bufs, *sems, after)
    return (out[0], out[1]), list(out[2:2 + nb]), out[-1]


def _gather_plans(kinds):
    nt = len(kinds)

    def slot(refs, t, px, py, pc):
        return _shard_of(refs[t], kinds[t], 4 * px + 2 * py + pc)

    def to_chips(refs, send_sems, recv_sems):
        x, y, c = _coords()
        peers = [(x, y, 1 - c), (1 - x, y, c), (x, 1 - y, c), (1 - x, 1 - y, c)]
        return [pltpu.make_async_remote_copy(
            src_ref=slot(refs, t, x, y, c), dst_ref=slot(refs, t, x, y, c), send_sem=send_sems.at[4 * t + k],
            recv_sem=recv_sems.at[4 * t + k], device_id=peer, device_id_type=MESH_ID)
            for t in range(nt) for k, peer in enumerate(peers)]

    def to_sibling(refs, send_sems, recv_sems):
        x, y, c = _coords()
        chips = [(1 - x, y), (x, 1 - y), (1 - x, 1 - y)]
        return [pltpu.make_async_remote_copy(
            src_ref=slot(refs, t, *chip, c), dst_ref=slot(refs, t, *chip, c), send_sem=send_sems.at[3 * t + j],
            recv_sem=recv_sems.at[3 * t + j], device_id=(x, y, 1 - c), device_id_type=MESH_ID)
            for t in range(nt) for j, chip in enumerate(chips)]

    return to_chips, to_sibling


def _shard_shape(full, kind):
    if kind == "col":
        return (full.shape[0], full.shape[1] // N_DEV)
    return (full.shape[0] // N_DEV, full.shape[1])


def _scatter_plan(kinds):
    nt = len(kinds)

    def plan(refs, send_sems, recv_sems):
        x, y, c = _coords()
        copies = []
        for t in range(nt):
            for k in range(1, N_DEV):
                px = 1 - x if (k >> 2) & 1 else x
                py = 1 - y if (k >> 1) & 1 else y
                pc = 1 - c if k & 1 else c
                copies.append(pltpu.make_async_remote_copy(
                    src_ref=_shard_of(refs[t], kinds[t], 4 * px + 2 * py + pc),
                    dst_ref=refs[nt + t].at[4 * x + 2 * y + c],
                    send_sem=send_sems.at[7 * t + k - 1], recv_sem=recv_sems.at[7 * t + k - 1],
                    device_id=(px, py, pc), device_id_type=MESH_ID))
        return copies
    return plan


def _partial_specs(full, kind, tr):
    rows, cols = _shard_shape(full, kind)
    steps = rows // tr
    if kind == "col":
        own = pl.BlockSpec((tr, cols), lambda i, w: (i, w[0]))
    else:
        own = pl.BlockSpec((tr, cols), lambda i, w: (w[0] * steps + i, 0))
    return [own] + [pl.BlockSpec((None, tr, cols), lambda i, w, k=k: (w[k], i, 0)) for k in range(1, N_DEV)]


def _all_reduce_small(vec, after):
    R = vec.shape[0]

    def body(v_ref, after_ref, o_ref, recv_ref, send_sems, recv_sems):
        x, y, c = _coords()
        me = 4 * x + 2 * y + c
        recv_ref[me] = v_ref[...]
        copies = []
        for k in range(1, N_DEV):
            bx, by, bc = (k >> 2) & 1, (k >> 1) & 1, k & 1
            peer = (1 - x if bx else x, 1 - y if by else y, 1 - c if bc else c)
            copies.append(pltpu.make_async_remote_copy(
                src_ref=v_ref, dst_ref=recv_ref.at[me],
                send_sem=send_sems.at[k - 1], recv_sem=recv_sems.at[k - 1],
                device_id=peer, device_id_type=MESH_ID))
        for cp in copies:
            cp.start()
        for cp in copies:
            cp.wait()
        total = recv_ref[0]
        for p in range(1, N_DEV):
            total = total + recv_ref[p]
        o_ref[...] = total

    return pl.pallas_call(
        body, name="all_reduce_small",
        in_specs=[pl.BlockSpec(memory_space=pltpu.VMEM), ANY], out_specs=pl.BlockSpec(memory_space=pltpu.VMEM),
        out_shape=jax.ShapeDtypeStruct((R, 128), F32),
        scratch_shapes=[pltpu.VMEM((N_DEV, R, 128), F32), pltpu.SemaphoreType.DMA((N_DEV - 1,)),
                        pltpu.SemaphoreType.DMA((N_DEV - 1,))],
        compiler_params=pltpu.CompilerParams(vmem_limit_bytes=VMEM_LIMIT),
    )(vec, after)


def _adamw_math(w, g, m, v):
    m = ADAM_B1 * m + (1.0 - ADAM_B1) * g
    v = ADAM_B2 * v + (1.0 - ADAM_B2) * (g * g)
    m_hat = m / (1.0 - ADAM_B1 ** ADAM_STEP)
    v_hat = v / (1.0 - ADAM_B2 ** ADAM_STEP)
    delta = -ADAM_LR * (m_hat / (jnp.sqrt(v_hat) + ADAM_EPS) + ADAM_WD * w)
    return delta, m, v


def _adamw(w, grads, kind, where, m, v, name):
    layers, rows, cols = w.shape
    tr = _tile(rows, 128)
    out = None
    for layer, (grad, land) in enumerate(grads):
        def body(w_ref, *refs):
            parts, (x_ref, m_ref, v_ref) = refs[:N_DEV], refs[N_DEV:N_DEV + 3]
            g_ref, d_ref, mo_ref, vo_ref = refs[-4:]
            g = parts[0][...].astype(F32)
            for p_ref in parts[1:]:
                g = g + p_ref[...].astype(F32)
            g_ref[...] = g
            d_ref[...], mo_ref[...], vo_ref[...] = _adamw_math(x_ref[...], g, m_ref[...], v_ref[...])

        blk = pl.BlockSpec((None, tr, cols), lambda i, w_, layer=layer: (layer, i, 0))
        earlier = [] if out is None else list(out)
        out = pl.pallas_call(
            body, name=f"{name}_{layer}",
            grid_spec=pltpu.PrefetchScalarGridSpec(
                num_scalar_prefetch=1, grid=(rows // tr,),
                in_specs=_partial_specs(grad, kind, tr) + [blk] * 3 + [ANY] * len(earlier),
                out_specs=[blk] * 4),
            out_shape=[jax.ShapeDtypeStruct((layers, rows, cols), F32)] * 4,
            input_output_aliases={N_DEV + 4 + k: k for k in range(len(earlier))},
            compiler_params=_params("parallel"),
        )(where, grad, *[land] * (N_DEV - 1), w, m, v, *earlier)
    return out


def _adamw_small(w, g, m, v):
    def body(w_ref, g_ref, m_ref, v_ref, d_ref, mo_ref, vo_ref):
        d_ref[...], mo_ref[...], vo_ref[...] = _adamw_math(w_ref[...], g_ref[...], m_ref[...], v_ref[...])

    whole = pl.BlockSpec(memory_space=pltpu.VMEM)
    return pl.pallas_call(
        body, name="adamw_small", in_specs=[whole] * 4, out_specs=[whole] * 3,
        out_shape=[jax.ShapeDtypeStruct(w.shape, F32)] * 3,
        compiler_params=pltpu.CompilerParams(vmem_limit_bytes=VMEM_LIMIT),
    )(w, g, m, v)


def _pack(arrays):
    rows = []
    for a in arrays:
        flat = a.reshape(-1)
        pad = (-flat.shape[0]) % 1024
        rows.append(jnp.pad(flat, (0, pad)).reshape(-1, 128))
    return jnp.concatenate(rows, axis=0)


def _unpack(packed, like):
    out, r = [], 0
    for a in like:
        n = math.prod(a.shape)
        nr = (n + 1023) // 1024 * 8
        out.append(packed[r:r + nr].reshape(-1)[:n].reshape(a.shape))
        r += nr
    return out


KIND = {"ab_in": "col", "ab_out": "row", "sb_in": "col", "sb_out": "row",
        "w1_0": "col", "w1_1": "col", "w2_0": "row", "w2_1": "row"}
GATHERS = {"ab_in": ("ab_in",), "ab_out": ("ab_out",), "ffn0": ("w1_0", "w2_0"),
           "sb": ("sb_in", "sb_out"), "ffn1": ("w1_1", "w2_1")}


class _Exchange:
    def __init__(self, shards):
        x, y, c = _coords()
        me = (4 * x + 2 * y + c).astype(jnp.int32)
        self.where = jnp.stack([jnp.bitwise_xor(me, k) for k in range(N_DEV)])
        self.full = {}
        self.tokens = []
        self.gathers = {}
        self.scatters = {}
        self.settled = {}
        block = me.reshape(1)
        after = block
        for key, group in GATHERS.items():
            for n in group:
                self.full[n] = _place(*shards[n], KIND[n], block, after, f"place_{n}")
            to_chips, to_sibling = _gather_plans([KIND[n] for n in group])
            bufs = [self.full[n] for n in group]
            sems, bufs, after = _copies_start(f"gather_start_{key}", bufs, to_chips, 4 * len(group), after)
            self.tokens.append(after)
            self.gathers[key] = (group, sems, bufs, to_chips, to_sibling)

    def tie(self, small):
        for token in self.tokens:
            small = small + token[0:1, 0:1]
        self.tokens = []
        return small

    def started(self):
        return tuple(self.tokens)

    def weight(self, name):
        return self.full[name]

    def arrive(self, key, after):
        group, sems, bufs, to_chips, to_sibling = self.gathers[key]
        sems, bufs, token = _copies_wait_start(f"gather_pass_{key}", bufs, sems, after, to_chips, to_sibling,
                                               3 * len(group))
        self.tokens.append(token)
        self.gathers[key] = (group, sems, bufs, token, to_sibling)

    def land(self, key, after=None):
        group, sems, bufs, token, to_sibling = self.gathers.pop(key)
        after = token if after is None else after
        self.full.update(zip(group, _copies_wait(f"gather_done_{key}", bufs, sems, after, to_sibling)))

    def reduce(self, key, grads):
        names = list(grads)
        kinds = [KIND[n] for n in names]
        full = [grads[n] for n in names]
        lands = [lax.empty((N_DEV,) + _shard_shape(g, k), BF16) for g, k in zip(full, kinds)]
        plan = _scatter_plan(kinds)
        sems, bufs, token = _copies_start(f"scatter_start_{key}", full + lands, plan, (N_DEV - 1) * len(names),
                                          full[-1])
        self.tokens.append(token)
        self.scatters[key] = (names, sems, bufs, plan)

    def settle(self, keys, after):
        for key in keys:
            names, sems, bufs, plan = self.scatters.pop(key)
            bufs = _copies_wait(f"scatter_done_{key}", bufs, sems, after, plan)
            self.settled.update({n: t for n, *t in zip(names, bufs[:len(names)], bufs[len(names):])})
        return self.settled


SMALL = ("norm_pre_mix", "norm_post_mix", "norm_pre_ffn", "norm_post_ffn", "sgu_ln_g", "sgu_ln_b", "sgu_w", "sgu_b")
ORDER = ("norm_pre_mix", "norm_post_mix", "norm_pre_ffn", "norm_post_ffn", "ab_w_in", "sgu_ln_g", "sgu_ln_b", "sgu_w",
         "sgu_b", "ab_w_out", "sb_w_in", "sb_w_out", "ffn_w1", "ffn_w2")


def kernel(x, norm_pre_mix, norm_post_mix, norm_pre_ffn, norm_post_ffn, ab_w_in, sgu_ln_g, sgu_ln_b, sgu_w, sgu_b, ab_w_out, sb_w_in, sb_w_out, ffn_w1, ffn_w2, loss_target, m_norm_pre_mix, m_norm_post_mix, m_norm_pre_ffn, m_norm_post_ffn, m_ab_w_in, m_sgu_ln_g, m_sgu_ln_b, m_sgu_w, m_sgu_b, m_ab_w_out, m_sb_w_in, m_sb_w_out, m_ffn_w1, m_ffn_w2, v_norm_pre_mix, v_norm_post_mix, v_norm_pre_ffn, v_norm_post_ffn, v_ab_w_in, v_sgu_ln_g, v_sgu_ln_b, v_sgu_w, v_sgu_b, v_ab_w_out, v_sb_w_in, v_sb_w_out, v_ffn_w1, v_ffn_w2):
    W = dict(norm_pre_mix=norm_pre_mix, norm_post_mix=norm_post_mix, norm_pre_ffn=norm_pre_ffn,
             norm_post_ffn=norm_post_ffn, ab_w_in=ab_w_in, sgu_ln_g=sgu_ln_g, sgu_ln_b=sgu_ln_b, sgu_w=sgu_w,
             sgu_b=sgu_b, ab_w_out=ab_w_out, sb_w_in=sb_w_in, sb_w_out=sb_w_out, ffn_w1=ffn_w1, ffn_w2=ffn_w2)
    M = dict(norm_pre_mix=m_norm_pre_mix, norm_post_mix=m_norm_post_mix, norm_pre_ffn=m_norm_pre_ffn,
             norm_post_ffn=m_norm_post_ffn, ab_w_in=m_ab_w_in, sgu_ln_g=m_sgu_ln_g, sgu_ln_b=m_sgu_ln_b,
             sgu_w=m_sgu_w, sgu_b=m_sgu_b, ab_w_out=m_ab_w_out, sb_w_in=m_sb_w_in, sb_w_out=m_sb_w_out,
             ffn_w1=m_ffn_w1, ffn_w2=m_ffn_w2)
    V = dict(norm_pre_mix=v_norm_pre_mix, norm_post_mix=v_norm_post_mix, norm_pre_ffn=v_norm_pre_ffn,
             norm_post_ffn=v_norm_post_ffn, ab_w_in=v_ab_w_in, sgu_ln_g=v_sgu_ln_g, sgu_ln_b=v_sgu_ln_b,
             sgu_w=v_sgu_w, sgu_b=v_sgu_b, ab_w_out=v_ab_w_out, sb_w_in=v_sb_w_in, sb_w_out=v_sb_w_out,
             ffn_w1=v_ffn_w1, ffn_w2=v_ffn_w2)

    shards = {"ab_in": (ab_w_in, 0), "ab_out": (ab_w_out, 0), "w1_0": (ffn_w1, 0), "w2_0": (ffn_w2, 0),
              "sb_in": (sb_w_in, 0), "sb_out": (sb_w_out, 0), "w1_1": (ffn_w1, 1), "w2_1": (ffn_w2, 1)}
    comm = _Exchange(shards)
    norms = (norm_pre_mix, norm_post_mix, norm_pre_ffn, norm_post_ffn)
    sgu = (sgu_ln_g, sgu_ln_b, sgu_w[0], sgu_b[0])
    loss, dx, small = _local_step(x[0], loss_target[0], norms, sgu, comm)
    loss = lax.psum(loss, MESH_AXES)

    out = {}

    def update(name, layers, landed):
        out[name] = _adamw(W[name], [landed[n] for n in layers], KIND[layers[0]], comm.where, M[name], V[name],
                           f"adamw_{name}")

    landed = comm.settle(("ffn1", "sb", "ffn0"), after=dx)
    for name, layers in (("sb_w_in", ["sb_in"]), ("sb_w_out", ["sb_out"]), ("ffn_w1", ["w1_0", "w1_1"]),
                         ("ffn_w2", ["w2_0", "w2_1"])):
        update(name, layers, landed)
    small_g = [small["pre_mix"], small["post_mix"], small["pre_ffn"], small["post_ffn"], small["ln_g"],
               small["ln_b"], small["w_s"][None], small["b_s"][None]]
    g_small = _all_reduce_small(_pack(small_g), out["ffn_w2"][3])
    landed = comm.settle(("ab_out", "ab_in"), after=g_small)
    update("ab_w_out", ["ab_out"], landed)
    update("ab_w_in", ["ab_in"], landed)
    res = _adamw_small(_pack([W[n] for n in SMALL]), g_small, _pack([M[n] for n in SMALL]),
                       _pack([V[n] for n in SMALL]))
    like = [W[n] for n in SMALL]
    for n, *vals in zip(SMALL, *[_unpack(r, like) for r in [g_small] + list(res)]):
        out[n] = vals

    return (loss, dx[None], *[out[n][0] for n in ORDER], *[out[n][1] for n in ORDER],
            *[out[n][2] for n in ORDER], *[out[n][3] for n in ORDER])
```

```python
import math

import jax
import jax.numpy as jnp
from jax import lax
from jax.experimental import pallas as pl
from jax.experimental.pallas import tpu as pltpu

F32 = jnp.float32
BF16 = jnp.bfloat16

HEAD_DIM = 128
CHUNK = 128
ATT_BLOCK = 128
DILATED_PAIRS = ((128, 1), (512, 4), (2048, 16))
RMS_EPS = 1e-6
LN_EPS = 1e-5
ADAM_LR = 0.001
ADAM_B1 = 0.9
ADAM_B2 = 0.999
ADAM_EPS = 1e-08
ADAM_WD = 0.01
ADAM_STEP = 10
N_DEV = 8
MASKED = -1e30

V7X_VMEM_BYTES = 64 * 1024 * 1024
VMEM_LIMIT = V7X_VMEM_BYTES - 8 * 1024 * 1024

NN = (((1,), (0,)), ((), ()))
NT = (((1,), (1,)), ((), ()))
TN = (((0,), (0,)), ((), ()))


def _params(*sem):
    return pltpu.CompilerParams(dimension_semantics=sem, vmem_limit_bytes=VMEM_LIMIT)


def _dot(a, b, dims=NN):
    return lax.dot_general(a, b, dims, preferred_element_type=F32)


def _tile(n, preferred):
    if n <= preferred:
        return n
    t = preferred - preferred % 128
    while n % t:
        t -= 128
    assert t > 0, (n, preferred)
    return t


def _matmul(a, b, *, mode, name, out_dtype=F32, tm=1024, tn=1024, tk=2048, epi=None, extras=(), after=()):
    if mode == "nn":
        (M, K), N = a.shape, b.shape[1]
    elif mode == "nt":
        (M, K), N = a.shape, b.shape[0]
    else:
        (K, M), N = a.shape, b.shape[1]
    tm, tn, tk = _tile(M, tm), _tile(N, tn), _tile(K, tk)
    nk = K // tk
    if mode == "tn":
        a_spec = pl.BlockSpec((tk, tm), lambda i, j, k: (k, i))
    else:
        a_spec = pl.BlockSpec((tm, tk), lambda i, j, k: (i, k))
    if mode == "nt":
        b_spec = pl.BlockSpec((tn, tk), lambda i, j, k: (j, k))
    else:
        b_spec = pl.BlockSpec((tk, tn), lambda i, j, k: (k, j))
    o_spec = pl.BlockSpec((tm, tn), lambda i, j, k: (i, j))
    dims = {"nn": NN, "nt": NT, "tn": TN}[mode]
    n_extra = len(extras)
    n_in = n_extra + len(after)

    def finish(acc, refs):
        j = pl.program_id(1)
        if epi is None:
            return acc
        return epi(acc, j, *[r[...] for r in refs])

    if nk == 1:
        def body(a_ref, b_ref, *rest):
            o_ref = rest[n_in]
            acc = _dot(a_ref[...], b_ref[...], dims)
            o_ref[...] = finish(acc, rest[:n_extra]).astype(o_ref.dtype)
        scratch = []
    else:
        def body(a_ref, b_ref, *rest):
            o_ref, acc_ref = rest[n_in], rest[n_in + 1]
            k = pl.program_id(2)

            @pl.when(k == 0)
            def _():
                acc_ref[...] = jnp.zeros_like(acc_ref)

            acc_ref[...] += _dot(a_ref[...], b_ref[...], dims)

            @pl.when(k == nk - 1)
            def _():
                o_ref[...] = finish(acc_ref[...], rest[:n_extra]).astype(o_ref.dtype)
        scratch = [pltpu.VMEM((tm, tn), F32)]

    return pl.pallas_call(
        body,
        name=name,
        grid=(M // tm, N // tn, nk),
        in_specs=[a_spec, b_spec] + [o_spec] * n_extra + [ANY] * len(after),
        out_specs=o_spec,
        out_shape=jax.ShapeDtypeStruct((M, N), out_dtype),
        scratch_shapes=scratch,
        compiler_params=_params("parallel", "parallel", "arbitrary"),
    )(a, b, *extras, *after)


ROWS = 512


def _rms(x):
    return lax.rsqrt(jnp.mean(x * x, axis=-1, keepdims=True) + RMS_EPS)


def _prenorm(x, g, name):
    T, D = x.shape

    def body(x_ref, g_ref, h_ref):
        xv = x_ref[...]
        h_ref[...] = (xv * _rms(xv) * g_ref[...]).astype(BF16)

    row = pl.BlockSpec((ROWS, D), lambda i: (i, 0))
    vec = pl.BlockSpec((1, D), lambda i: (0, 0))
    return pl.pallas_call(
        body, name=name, grid=(T // ROWS,), in_specs=[row, vec], out_specs=row,
        out_shape=jax.ShapeDtypeStruct((T, D), BF16), compiler_params=_params("parallel"),
    )(x, g)


def _postnorm_prenorm(x, y, g_post, g_pre, name):
    T, D = x.shape

    def body(x_ref, y_ref, gp_ref, gn_ref, xo_ref, h_ref):
        yv = y_ref[...]
        xn = x_ref[...] + yv * _rms(yv) * gp_ref[...]
        xo_ref[...] = xn
        h_ref[...] = (xn * _rms(xn) * gn_ref[...]).astype(BF16)

    row = pl.BlockSpec((ROWS, D), lambda i: (i, 0))
    vec = pl.BlockSpec((1, D), lambda i: (0, 0))
    return pl.pallas_call(
        body, name=name, grid=(T // ROWS,), in_specs=[row, row, vec, vec], out_specs=[row, row],
        out_shape=[jax.ShapeDtypeStruct((T, D), F32), jax.ShapeDtypeStruct((T, D), BF16)],
        compiler_params=_params("parallel"),
    )(x, y, g_post, g_pre)


def _postnorm_grads(dn, yh, r, g):
    gd = dn * g
    return r * (gd - yh * jnp.mean(yh * gd, axis=-1, keepdims=True)), dn * yh


def _postnorm_loss(x, y, g_post, target, name):
    T, D = x.shape

    def body(x_ref, y_ref, gp_ref, t_ref, loss_ref, dx_ref, dy_ref, dg_ref):
        @pl.when(pl.program_id(0) == 0)
        def _():
            loss_ref[...] = jnp.zeros_like(loss_ref)
            dg_ref[...] = jnp.zeros_like(dg_ref)

        yv = y_ref[...]
        r = _rms(yv)
        yh = yv * r
        err = x_ref[...] + yh * gp_ref[...] - t_ref[...]
        dx = err * (1.0 / D)
        dx_ref[...] = dx
        loss_ref[...] += 0.5 * jnp.sum(jnp.sum(err * err, axis=-1, keepdims=True) * (1.0 / D))
        dy, dg = _postnorm_grads(dx, yh, r, gp_ref[...])
        dy_ref[...] = dy.astype(BF16)
        dg_ref[...] += jnp.sum(dg, axis=0, keepdims=True)

    row = pl.BlockSpec((ROWS, D), lambda i: (i, 0))
    vec = pl.BlockSpec((1, D), lambda i: (0, 0))
    acc = pl.BlockSpec((8, 128), lambda i: (0, 0))
    return pl.pallas_call(
        body, name=name, grid=(T // ROWS,), in_specs=[row, row, vec, row], out_specs=[acc, row, row, vec],
        out_shape=[jax.ShapeDtypeStruct((8, 128), F32), jax.ShapeDtypeStruct((T, D), F32),
                   jax.ShapeDtypeStruct((T, D), BF16), jax.ShapeDtypeStruct((1, D), F32)],
        compiler_params=_params("arbitrary"),
    )(x, y, g_post, target)


def _norm_bwd_pair(dx_out, dh, x, g_pre, y_prev, g_post_prev, name):
    T, D = x.shape

    def body(dxo_ref, dh_ref, x_ref, g_ref, y_ref, gp_ref, dx_ref, dg_ref, dy_ref, dgp_ref):
        @pl.when(pl.program_id(0) == 0)
        def _():
            dg_ref[...] = jnp.zeros_like(dg_ref)
            dgp_ref[...] = jnp.zeros_like(dgp_ref)

        xv, dhv = x_ref[...], dh_ref[...]
        r = _rms(xv)
        xh = xv * r
        gd = dhv * g_ref[...]
        dx = dxo_ref[...] + r * (gd - xh * jnp.mean(xh * gd, axis=-1, keepdims=True))
        dx_ref[...] = dx
        dg_ref[...] += jnp.sum(dhv * xh, axis=0, keepdims=True)
        yv = y_ref[...]
        ry = _rms(yv)
        dy, dgp = _postnorm_grads(dx, yv * ry, ry, gp_ref[...])
        dy_ref[...] = dy.astype(BF16)
        dgp_ref[...] += jnp.sum(dgp, axis=0, keepdims=True)

    row = pl.BlockSpec((ROWS, D), lambda i: (i, 0))
    vec = pl.BlockSpec((1, D), lambda i: (0, 0))
    return pl.pallas_call(
        body, name=name, grid=(T // ROWS,), in_specs=[row, row, row, vec, row, vec],
        out_specs=[row, vec, row, vec],
        out_shape=[jax.ShapeDtypeStruct((T, D), F32), jax.ShapeDtypeStruct((1, D), F32),
                   jax.ShapeDtypeStruct((T, D), BF16), jax.ShapeDtypeStruct((1, D), F32)],
        compiler_params=_params("arbitrary"),
    )(dx_out, dh, x, g_pre, y_prev, g_post_prev)


def _prenorm_bwd(dx_out, dh, x, g_pre, name):
    T, D = x.shape

    def body(dxo_ref, dh_ref, x_ref, g_ref, dx_ref, dg_ref):
        @pl.when(pl.program_id(0) == 0)
        def _():
            dg_ref[...] = jnp.zeros_like(dg_ref)

        xv, dhv = x_ref[...], dh_ref[...]
        r = _rms(xv)
        xh = xv * r
        gd = dhv * g_ref[...]
        dx_ref[...] = dxo_ref[...] + r * (gd - xh * jnp.mean(xh * gd, axis=-1, keepdims=True))
        dg_ref[...] += jnp.sum(dhv * xh, axis=0, keepdims=True)

    row = pl.BlockSpec((ROWS, D), lambda i: (i, 0))
    vec = pl.BlockSpec((1, D), lambda i: (0, 0))
    return pl.pallas_call(
        body, name=name, grid=(T // ROWS,), in_specs=[row, row, row, vec], out_specs=[row, vec],
        out_shape=[jax.ShapeDtypeStruct((T, D), F32), jax.ShapeDtypeStruct((1, D), F32)],
        compiler_params=_params("arbitrary"),
    )(dx_out, dh, x, g_pre)


_INV_SQRT2 = 1.0 / math.sqrt(2.0)
_INV_SQRT2PI = 1.0 / math.sqrt(2.0 * math.pi)


def _gelu(x):
    return 0.5 * x * (1.0 + lax.erf(x * _INV_SQRT2))


def _gelu_grad(x):
    return 0.5 * (1.0 + lax.erf(x * _INV_SQRT2)) + x * jnp.exp(-0.5 * x * x) * _INV_SQRT2PI


def _layernorm_stats(x):
    mu = jnp.mean(x, axis=-1, keepdims=True)
    xc = x - mu
    rstd = lax.rsqrt(jnp.mean(xc * xc, axis=-1, keepdims=True) + LN_EPS)
    return xc * rstd, rstd


def _tril_mask():
    i = lax.broadcasted_iota(jnp.int32, (CHUNK, CHUNK), 0)
    j = lax.broadcasted_iota(jnp.int32, (CHUNK, CHUNK), 1)
    return j <= i


SGU_ROWS = 512


def _sgu_fwd(z, ln_g, ln_b, w_s, b_t, name):
    T = z.shape[0]
    A = ln_g.shape[1]
    G = A // 128
    rows = min(SGU_ROWS, T)

    def body(u_ref, v_ref, g_ref, b_ref, w_ref, bt_ref, o_ref):
        mask = _tril_mask()
        for c in range(rows // CHUNK):
            rs = pl.ds(c * CHUNK, CHUNK)
            xh, _ = _layernorm_stats(_gelu(v_ref[rs, :]))
            vn = (xh * g_ref[...] + b_ref[...]).astype(BF16)
            for g in range(G):
                cs = pl.ds(g * 128, 128)
                w = jnp.where(mask, w_ref[g], 0.0).astype(BF16)
                mixed = _dot(w, vn[:, g * 128:(g + 1) * 128]) + bt_ref[:, g:g + 1]
                o_ref[rs, cs] = (_gelu(u_ref[rs, cs]) * mixed).astype(BF16)

    return pl.pallas_call(
        body, name=name, grid=(T // rows,),
        in_specs=[
            pl.BlockSpec((rows, A), lambda i: (i, 0)),
            pl.BlockSpec((rows, A), lambda i: (i, 1)),
            pl.BlockSpec((1, A), lambda i: (0, 0)),
            pl.BlockSpec((1, A), lambda i: (0, 0)),
            pl.BlockSpec((G, CHUNK, CHUNK), lambda i: (0, 0, 0)),
            pl.BlockSpec((CHUNK, G), lambda i: (0, 0)),
        ],
        out_specs=pl.BlockSpec((rows, A), lambda i: (i, 0)),
        out_shape=jax.ShapeDtypeStruct((T, 2 * A), BF16),
        compiler_params=_params("parallel"),
    )(z, z, ln_g, ln_b, w_s, b_t)


def _sgu_bwd(z, dcat, ln_g, ln_b, w_s, b_t, name):
    T = z.shape[0]
    A = ln_g.shape[1]
    G = A // 128
    rows = min(SGU_ROWS, T)

    def body(u_ref, v_ref, da_ref, g_ref, b_ref, w_ref, bt_ref, dz_ref, dg_ref, db_ref, dw_ref, dbt_ref, dvn_ref):
        @pl.when(pl.program_id(0) == 0)
        def _():
            dg_ref[...] = jnp.zeros_like(dg_ref)
            db_ref[...] = jnp.zeros_like(db_ref)
            dw_ref[...] = jnp.zeros_like(dw_ref)
            dbt_ref[...] = jnp.zeros_like(dbt_ref)

        mask = _tril_mask()
        for c in range(rows // CHUNK):
            rs = pl.ds(c * CHUNK, CHUNK)
            vv = v_ref[rs, :]
            gv = _gelu(vv)
            xh, rstd = _layernorm_stats(gv)
            vn = (xh * g_ref[...] + b_ref[...]).astype(BF16)
            for g in range(G):
                cs = pl.ds(g * 128, 128)
                w = jnp.where(mask, w_ref[g], 0.0).astype(BF16)
                vg = vn[:, g * 128:(g + 1) * 128]
                mixed = _dot(w, vg) + bt_ref[:, g:g + 1]
                uu = u_ref[rs, cs]
                da = da_ref[rs, cs]
                dz_ref[rs, cs] = (da * mixed * _gelu_grad(uu)).astype(BF16)
                dm = da * _gelu(uu)
                dmb = dm.astype(BF16)
                dbt_ref[:, g:g + 1] += jnp.sum(dm, axis=1, keepdims=True)
                dw_ref[g] += jnp.where(mask, _dot(dmb, vg, NT), 0.0)
                dvn_ref[:, cs] = _dot(w, dmb, TN)
            dvn = dvn_ref[...]
            dg_ref[...] += jnp.sum(dvn * xh, axis=0, keepdims=True)
            db_ref[...] += jnp.sum(dvn, axis=0, keepdims=True)
            dxh = dvn * g_ref[...]
            dgv = rstd * (dxh - jnp.mean(dxh, axis=-1, keepdims=True)
                          - xh * jnp.mean(dxh * xh, axis=-1, keepdims=True))
            dz_ref[rs, pl.ds(A, A)] = (dgv * _gelu_grad(vv)).astype(BF16)

    vec = pl.BlockSpec((1, A), lambda i: (0, 0))
    wsp = pl.BlockSpec((G, CHUNK, CHUNK), lambda i: (0, 0, 0))
    bsp = pl.BlockSpec((CHUNK, G), lambda i: (0, 0))
    return pl.pallas_call(
        body, name=name, grid=(T // rows,),
        in_specs=[
            pl.BlockSpec((rows, A), lambda i: (i, 0)),
            pl.BlockSpec((rows, A), lambda i: (i, 1)),
            pl.BlockSpec((rows, A), lambda i: (i, 0)),
            vec, vec, wsp, bsp,
        ],
        out_specs=[pl.BlockSpec((rows, 2 * A), lambda i: (i, 0)), vec, vec, wsp, bsp],
        out_shape=[
            jax.ShapeDtypeStruct((T, 2 * A), BF16),
            jax.ShapeDtypeStruct((1, A), F32),
            jax.ShapeDtypeStruct((1, A), F32),
            jax.ShapeDtypeStruct((G, CHUNK, CHUNK), F32),
            jax.ShapeDtypeStruct((CHUNK, G), F32),
        ],
        scratch_shapes=[pltpu.VMEM((CHUNK, A), F32)],
        compiler_params=_params("arbitrary"),
    )(z, z, dcat, ln_g, ln_b, w_s, b_t)


def _alibi_row(B, d):
    H = B // HEAD_DIM
    slopes = [d * 2.0 ** (-8.0 * (h + 1.0) / H) for h in range(H)]
    return jnp.repeat(jnp.asarray(slopes, F32), HEAD_DIM)[None, :]


def _dil_scores(q, k, slope_d, valid, dist):
    s = _dot(q, k, NT) - slope_d * dist
    return jnp.where(valid, s, MASKED)


def _dil_block(n, r, d):
    if d == 1:
        return pl.ds(pl.multiple_of(n * ATT_BLOCK, ATT_BLOCK), ATT_BLOCK)
    return pl.ds(n * (d * ATT_BLOCK) + r, ATT_BLOCK, stride=d)


def _dilated_forward(z, cat, B, name):
    T = z.shape[0]
    H = B // HEAD_DIM
    A = cat.shape[1] - B
    scale = HEAD_DIM ** -0.5
    blk = ATT_BLOCK
    chunk = _tile(T, 512)

    def body(q_ref, k_ref, v_ref, sl_ref, cat_in, cat_ref, of_ref, lt_ref, *branch):
        o_refs, l_refs = branch[:3], branch[3:]
        slope = sl_ref[:, :1]
        qi = lax.broadcasted_iota(jnp.int32, (blk, 2 * blk), 0)
        kj = lax.broadcasted_iota(jnp.int32, (blk, 2 * blk), 1)
        dist = qi + blk - kj
        band = (dist >= 0) & (dist <= blk)
        distf = dist.astype(F32)

        for b, (_, d) in enumerate(DILATED_PAIRS):
            def one(n, r, b=b, d=d):
                rows, prev = _dil_block(n, r, d), _dil_block(jnp.maximum(n - 1, 0), r, d)
                q = (q_ref[rows, :] * scale).astype(BF16)
                k = jnp.concatenate([k_ref[prev, :], k_ref[rows, :]], axis=0).astype(BF16)
                v = jnp.concatenate([v_ref[prev, :], v_ref[rows, :]], axis=0).astype(BF16)
                s = _dil_scores(q, k, slope * float(d), band & ((kj >= blk) | (n > 0)), distf)
                m = jnp.max(s, axis=-1, keepdims=True)
                p = jnp.exp(s - m)
                den = jnp.sum(p, axis=-1, keepdims=True)
                o_refs[b][rows, :] = _dot(p.astype(BF16), v) / den
                l_refs[b][rows, :] = jnp.broadcast_to(m + jnp.log(den), (blk, HEAD_DIM))

            per = max(1, 4 // d)

            def step(i, _, d=d, per=per, one=one):
                for u in range(per):
                    for r in range(d):
                        one(i * per + u, r)
                return 0

            lax.fori_loop(0, T // (d * blk * per), step, 0)

        def merge(i, _):
            rs = pl.ds(pl.multiple_of(i * chunk, chunk), chunk)
            a, b, c = l_refs[0][rs, :], l_refs[1][rs, :], l_refs[2][rs, :]
            m = jnp.maximum(jnp.maximum(a, b), c)
            ea, eb, ec = jnp.exp(a - m), jnp.exp(b - m), jnp.exp(c - m)
            tot = ea + eb + ec
            o = (ea * o_refs[0][rs, :] + eb * o_refs[1][rs, :] + ec * o_refs[2][rs, :]) / tot
            of_ref[rs, :] = o
            cat_ref[rs, :] = o.astype(BF16)
            lt_ref[rs, :] = m + jnp.log(tot)
            return 0

        lax.fori_loop(0, T // chunk, merge, 0)

    def col(unit):
        return lambda h: (0, unit * H + h)

    seq = (T, HEAD_DIM)
    out = pl.BlockSpec(seq, lambda h: (0, h))
    return pl.pallas_call(
        body, name=name, grid=(H,),
        in_specs=[pl.BlockSpec(seq, col(2)), pl.BlockSpec(seq, col(3)), pl.BlockSpec(seq, col(4)),
                  pl.BlockSpec((1, HEAD_DIM), lambda h: (0, h)), ANY],
        out_specs=[pl.BlockSpec(seq, lambda h: (0, A // HEAD_DIM + h)), out, out],
        out_shape=[jax.ShapeDtypeStruct(cat.shape, BF16), jax.ShapeDtypeStruct((T, B), F32),
                   jax.ShapeDtypeStruct((T, B), F32)],
        input_output_aliases={4: 0},
        scratch_shapes=[pltpu.VMEM(seq, F32)] * 6,
        compiler_params=_params("parallel"),
    )(z, z, z, _alibi_row(B, 1), cat)


def _dilated_backward(z, dcat, o, lse, B, name):
    T = z.shape[0]
    H = B // HEAD_DIM
    scale = HEAD_DIM ** -0.5
    blk = ATT_BLOCK
    chunk = _tile(T, 512)

    def body(q_ref, k_ref, v_ref, do_ref, o_ref, l_ref, sl_ref, dq_ref, dk_ref, dv_ref, aq_ref, ak_ref, av_ref):
        slope = sl_ref[:, :1]
        qi = lax.broadcasted_iota(jnp.int32, (blk, 2 * blk), 0)
        kj = lax.broadcasted_iota(jnp.int32, (blk, 2 * blk), 1)
        dist = qi + blk - kj
        band = (dist >= 0) & (dist <= blk)
        distf = dist.astype(F32)
        for acc in (aq_ref, ak_ref, av_ref):
            acc[...] = jnp.zeros_like(acc)

        for _, d in DILATED_PAIRS:
            def one(n, r, d=d):
                rows, prev = _dil_block(n, r, d), _dil_block(jnp.maximum(n - 1, 0), r, d)
                q = (q_ref[rows, :] * scale).astype(BF16)
                k = jnp.concatenate([k_ref[prev, :], k_ref[rows, :]], axis=0).astype(BF16)
                v = jnp.concatenate([v_ref[prev, :], v_ref[rows, :]], axis=0).astype(BF16)
                do = do_ref[rows, :]
                delta = jnp.sum(do * o_ref[rows, :], axis=-1, keepdims=True)
                do = do.astype(BF16)
                s = _dil_scores(q, k, slope * float(d), band & ((kj >= blk) | (n > 0)), distf)
                p = jnp.exp(s - l_ref[rows, :][:, :1])
                ds = (p * (_dot(do, v, NT) - delta)).astype(BF16)
                aq_ref[rows, :] += _dot(ds, k)
                dk = _dot(ds, q, TN)
                dv = _dot(p.astype(BF16), do, TN)
                ak_ref[prev, :] += dk[:blk]
                av_ref[prev, :] += dv[:blk]
                ak_ref[rows, :] += dk[blk:]
                av_ref[rows, :] += dv[blk:]

            per = max(1, 4 // d)

            def step(i, _, d=d, per=per, one=one):
                for u in range(per):
                    for r in range(d):
                        one(i * per + u, r)
                return 0

            lax.fori_loop(0, T // (d * blk * per), step, 0)

        def emit(i, _):
            rs = pl.ds(pl.multiple_of(i * chunk, chunk), chunk)
            dq_ref[rs, :] = (aq_ref[rs, :] * scale).astype(BF16)
            dk_ref[rs, :] = ak_ref[rs, :].astype(BF16)
            dv_ref[rs, :] = av_ref[rs, :].astype(BF16)
            return 0

        lax.fori_loop(0, T // chunk, emit, 0)

    def col(unit):
        return lambda h: (0, unit * H + h)

    seq = (T, HEAD_DIM)
    own = pl.BlockSpec(seq, lambda h: (0, h))
    return pl.pallas_call(
        body, name=name, grid=(H,),
        in_specs=[pl.BlockSpec(seq, col(2)), pl.BlockSpec(seq, col(3)), pl.BlockSpec(seq, col(4)),
                  pl.BlockSpec(seq, col(1)), own, own, pl.BlockSpec((1, HEAD_DIM), lambda h: (0, h))],
        out_specs=[own] * 3,
        out_shape=[jax.ShapeDtypeStruct((T, B), BF16)] * 3,
        scratch_shapes=[pltpu.VMEM(seq, F32)] * 3,
        compiler_params=_params("parallel"),
    )(z, z, z, dcat, o, lse, _alibi_row(B, 1))


def _join_columns(parts, name):
    T = parts[0].shape[0]
    widths = [p.shape[1] for p in parts]

    def body(*refs):
        o_ref, at = refs[-1], 0
        for ref, w in zip(refs[:-1], widths):
            o_ref[:, pl.ds(at, w)] = ref[...]
            at += w

    return pl.pallas_call(
        body, name=name, grid=(T // ROWS,),
        in_specs=[pl.BlockSpec((ROWS, w), lambda i: (i, 0)) for w in widths],
        out_specs=pl.BlockSpec((ROWS, sum(widths)), lambda i: (i, 0)),
        out_shape=jax.ShapeDtypeStruct((T, sum(widths)), BF16), compiler_params=_params("parallel"),
    )(*parts)


SB_QUERY_ROWS = 1024
SB_KEYS = 2 * ATT_BLOCK


def _tri_and_ones(pred):
    rows = lax.broadcasted_iota(jnp.int32, (2 * ATT_BLOCK, 2 * ATT_BLOCK), 0) % ATT_BLOCK
    cols = lax.broadcasted_iota(jnp.int32, (2 * ATT_BLOCK, 2 * ATT_BLOCK), 1)
    return ((cols >= ATT_BLOCK) | pred(rows, cols)).astype(BF16)


def _running(x, tri):
    hi = x.astype(BF16)
    lo = (x - hi.astype(F32)).astype(BF16)
    return _dot(jnp.concatenate([hi, lo], axis=1), tri)


def _sb_mask(query_rows, s):
    rows = lax.broadcasted_iota(jnp.int32, (query_rows - s * SB_KEYS, SB_KEYS), 0)
    cols = lax.broadcasted_iota(jnp.int32, (query_rows - s * SB_KEYS, SB_KEYS), 1)
    return cols < rows


def _log_sigmoids(z):
    ls = jnp.minimum(z, 0.0) - jnp.log(1.0 + jnp.exp(-jnp.abs(z)))
    return ls, ls - z


def _sb_fwd(qkv, W, name):
    T = qkv.shape[0]
    H = W // HEAD_DIM
    blk = ATT_BLOCK
    qb = min(SB_QUERY_ROWS, T)
    per = qb // SB_KEYS

    def body(q_ref, k_ref, v_ref, o_ref, lt_ref, acc_ref):
        i = pl.program_id(1)
        tri = _tri_and_ones(lambda r, c: r > c)
        lt_ref[...] = jnp.zeros_like(lt_ref)
        acc_ref[...] = jnp.zeros_like(acc_ref)

        def tile(j, mask, r0=0):
            ks = pl.ds(pl.multiple_of(j * SB_KEYS, SB_KEYS), SB_KEYS)
            qs = pl.ds(r0, qb - r0)
            z = _dot(q_ref[qs, :], k_ref[ks, :], NT)
            ls, lm = _log_sigmoids(z)
            if mask is not None:
                lm = jnp.where(mask, lm, 0.0)
            later = lt_ref[qs, :]
            second = _running(lm[:, blk:], tri)
            first = _running(lm[:, :blk], tri)
            after_first = later + second[:, blk:]
            a = jnp.exp(ls + jnp.concatenate([first[:, :blk] + after_first, second[:, :blk] + later], axis=1))
            if mask is not None:
                a = jnp.where(mask, a, 0.0)
            acc_ref[qs, :] += _dot(a.astype(BF16), v_ref[ks, :])
            lt_ref[qs, :] = after_first + first[:, blk:]

        for s in reversed(range(per)):
            tile(i * per + s, _sb_mask(qb, s), s * SB_KEYS)

        def step(jj, _):
            for s in range(per):
                tile((i - jj) * per - 1 - s, None)
            return 0

        lax.fori_loop(0, i, step, 0)
        o_ref[...] = acc_ref[...].astype(BF16)

    qs = pl.BlockSpec((qb, HEAD_DIM), lambda h, i: (i, h))
    return pl.pallas_call(
        body, name=name, grid=(H, T // qb),
        in_specs=[qs, pl.BlockSpec((T, HEAD_DIM), lambda h, i: (0, H + h)),
                  pl.BlockSpec((T, HEAD_DIM), lambda h, i: (0, 2 * H + h))],
        out_specs=[qs, qs],
        out_shape=[jax.ShapeDtypeStruct((T, W), BF16), jax.ShapeDtypeStruct((T, W), F32)],
        scratch_shapes=[pltpu.VMEM((qb, HEAD_DIM), F32)],
        compiler_params=_params("parallel", "arbitrary"),
    )(qkv, qkv, qkv)


def _sb_bwd(qkv, do, ltot, W, name):
    T = qkv.shape[0]
    H = W // HEAD_DIM
    blk = ATT_BLOCK
    nkb = T // SB_KEYS
    qb = min(SB_QUERY_ROWS, T)
    per = qb // SB_KEYS

    def body(q_ref, k_ref, v_ref, do_ref, lt_ref, dq_ref, dkt_ref, dvt_ref, qt_ref, dot_ref, plm_ref, pg_ref):
        i = pl.program_id(1)

        @pl.when(i == 0)
        def _():
            dkt_ref[...] = jnp.zeros_like(dkt_ref)
            dvt_ref[...] = jnp.zeros_like(dvt_ref)

        qt_ref[...] = q_ref[...].astype(F32).T.astype(BF16)
        dot_ref[...] = do_ref[...].astype(F32).T.astype(BF16)
        upto = _tri_and_ones(lambda r, c: r <= c)
        before = _tri_and_ones(lambda r, c: r < c)
        plm_ref[...] = jnp.zeros_like(plm_ref)
        pg_ref[...] = jnp.zeros_like(pg_ref)
        dq_ref[...] = jnp.zeros_like(dq_ref)

        def tile(j, mask, r0=0):
            ks = pl.ds(pl.multiple_of(j * SB_KEYS, SB_KEYS), SB_KEYS)
            qs = pl.ds(r0, qb - r0)
            k = k_ref[ks, :]
            v = v_ref[ks, :]
            z = _dot(q_ref[qs, :], k, NT)
            ls, lm = _log_sigmoids(z)
            nsig = jnp.exp(lm)
            if mask is not None:
                lm = jnp.where(mask, lm, 0.0)
            earlier = plm_ref[qs, :]
            first = _running(lm[:, :blk], upto)
            second = _running(lm[:, blk:], upto)
            upto_first = earlier + first[:, blk:]
            seen = jnp.concatenate([first[:, :blk] + earlier, second[:, :blk] + upto_first], axis=1)
            ltot = lt_ref[qs, :]
            a = jnp.exp(ls + (jnp.concatenate([ltot, ltot], axis=1) - seen))
            if mask is not None:
                a = jnp.where(mask, a, 0.0)
            g = a * _dot(do_ref[qs, :], v, NT)
            g_earlier = pg_ref[qs, :]
            g_first = _running(g[:, :blk], before)
            g_second = _running(g[:, blk:], before)
            g_upto_first = g_earlier + g_first[:, blk:]
            gsum = jnp.concatenate([g_first[:, :blk] + g_earlier, g_second[:, :blk] + g_upto_first], axis=1)
            dz = g * nsig - gsum * jnp.exp(ls)
            if mask is not None:
                dz = jnp.where(mask, dz, 0.0)
            dzb = dz.astype(BF16)
            dkt_ref[j] += _dot(qt_ref[:, qs], dzb)
            dvt_ref[j] += _dot(dot_ref[:, qs], a.astype(BF16))
            dq_ref[qs, :] += _dot(dzb, k)
            plm_ref[qs, :] = upto_first + second[:, blk:]
            pg_ref[qs, :] = g_upto_first + g_second[:, blk:]

        def step(jj, _):
            for s in range(per):
                tile(jj * per + s, None)
            return 0

        lax.fori_loop(0, i, step, 0)
        for s in range(per):
            tile(i * per + s, _sb_mask(qb, s), s * SB_KEYS)

    qs = pl.BlockSpec((qb, HEAD_DIM), lambda h, i: (i, h))
    res = pl.BlockSpec((None, nkb, HEAD_DIM, SB_KEYS), lambda h, i: (h, 0, 0, 0))
    return pl.pallas_call(
        body, name=name, grid=(H, T // qb),
        in_specs=[qs, pl.BlockSpec((T, HEAD_DIM), lambda h, i: (0, H + h)),
                  pl.BlockSpec((T, HEAD_DIM), lambda h, i: (0, 2 * H + h)), qs, qs],
        out_specs=[qs, res, res],
        out_shape=[jax.ShapeDtypeStruct((T, W), F32)] + [jax.ShapeDtypeStruct((H, nkb, HEAD_DIM, SB_KEYS), F32)] * 2,
        scratch_shapes=[pltpu.VMEM((HEAD_DIM, qb), BF16), pltpu.VMEM((HEAD_DIM, qb), BF16),
                        pltpu.VMEM((qb, HEAD_DIM), F32), pltpu.VMEM((qb, HEAD_DIM), F32)],
        compiler_params=_params("parallel", "arbitrary"),
    )(qkv, qkv, qkv, do, ltot)


def _sb_pack(dq, dkt, dvt, name):
    T, W = dq.shape
    H = W // HEAD_DIM
    blk = SB_KEYS
    scale = HEAD_DIM ** -0.5

    def body(q_ref, kt_ref, vt_ref, o_ref):
        o_ref[:, pl.ds(0, W)] = (q_ref[...] * scale).astype(BF16)
        for h in range(H):
            o_ref[:, pl.ds(W + h * HEAD_DIM, HEAD_DIM)] = kt_ref[h].T.astype(BF16)
            o_ref[:, pl.ds(2 * W + h * HEAD_DIM, HEAD_DIM)] = vt_ref[h].T.astype(BF16)

    tr = pl.BlockSpec((H, None, HEAD_DIM, blk), lambda i: (0, i, 0, 0))
    return pl.pallas_call(
        body, name=name, grid=(T // blk,), in_specs=[pl.BlockSpec((blk, W), lambda i: (i, 0)), tr, tr],
        out_specs=pl.BlockSpec((blk, 3 * W), lambda i: (i, 0)),
        out_shape=jax.ShapeDtypeStruct((T, 3 * W), BF16), compiler_params=_params("parallel"),
    )(dq, dkt, dvt)


def _local_step(x, target, norms, sgu, comm):
    T, D = x.shape
    A = D // 2
    pre_mix, post_mix, pre_ffn, post_ffn = norms
    ln_g, ln_b, w_s, b_s = sgu
    b_t = b_s.T
    scale = HEAD_DIM ** -0.5

    def vec(p, layer):
        return comm.tie(p[layer:layer + 1])

    h0 = _prenorm(x, vec(pre_mix, 0), "prenorm0")
    comm.arrive("ab_in", after=h0)
    comm.land("ab_in")
    z = _matmul(h0, comm.weight("ab_in"), mode="nn", name="ab_in_fwd")
    comm.arrive("ab_out", after=z)
    cat = _sgu_fwd(z, ln_g, ln_b, w_s, b_t, "sgu_fwd")
    cat, o_dil, lse_dil = _dilated_forward(z, cat, A, "dilated_fwd")
    comm.land("ab_out", after=o_dil)
    y0 = _matmul(cat, comm.weight("ab_out"), mode="nn", name="ab_out_fwd")
    comm.arrive("ffn0", after=y0)
    x1, h1 = _postnorm_prenorm(x, y0, vec(post_mix, 0), vec(pre_ffn, 0), "norm_mix0")
    comm.land("ffn0", after=h1)

    def relu2(acc, j):
        r = jnp.maximum(acc, 0.0)
        return r * r

    f0 = _matmul(h1, comm.weight("w1_0"), mode="nn", name="ffn0_w1_fwd", out_dtype=BF16, epi=relu2)
    y1 = _matmul(f0, comm.weight("w2_0"), mode="nn", name="ffn0_w2_fwd")
    comm.arrive("sb", after=y1)
    x2, h2 = _postnorm_prenorm(x1, y1, vec(post_ffn, 0), vec(pre_mix, 1), "norm_ffn0")
    comm.land("sb", after=h2)

    tn_qkv = _tile(D, 1024)
    nq = D // tn_qkv

    def scale_q(acc, j):
        return jnp.where(j < nq, acc * scale, acc)

    qkv = _matmul(h2, comm.weight("sb_in"), mode="nn", name="sb_in_fwd", out_dtype=BF16, tn=tn_qkv, epi=scale_q)
    comm.arrive("ffn1", after=qkv)
    o_sb, ltot = _sb_fwd(qkv, D, "sb_fwd")
    comm.land("ffn1", after=o_sb)
    y2 = _matmul(o_sb, comm.weight("sb_out"), mode="nn", name="sb_out_fwd")
    x3, h3 = _postnorm_prenorm(x2, y2, vec(post_mix, 1), vec(pre_ffn, 1), "norm_mix1")
    f1 = _matmul(h3, comm.weight("w1_1"), mode="nn", name="ffn1_w1_fwd", out_dtype=BF16, epi=relu2)
    y3 = _matmul(f1, comm.weight("w2_1"), mode="nn", name="ffn1_w2_fwd")
    loss_tile, dx4, dy3, dg_post_ffn1 = _postnorm_loss(x3, y3, vec(post_ffn, 1), target, "norm_loss")
    loss = loss_tile[0, 0]

    def relu2_bwd(acc, j, f):
        return acc * (2.0 * jnp.sqrt(f.astype(F32)))

    def ffn_bwd(dy, h, f, layer):
        g_w2 = _matmul(f, dy, mode="tn", name=f"ffn{layer}_w2_wgrad", out_dtype=BF16)
        da = _matmul(dy, comm.weight(f"w2_{layer}"), mode="nt", name=f"ffn{layer}_w2_dgrad", out_dtype=BF16,
                     epi=relu2_bwd, extras=(f,))
        g_w1 = _matmul(h, da, mode="tn", name=f"ffn{layer}_w1_wgrad", out_dtype=BF16)
        comm.reduce(f"ffn{layer}", {f"w2_{layer}": g_w2, f"w1_{layer}": g_w1})
        return _matmul(da, comm.weight(f"w1_{layer}"), mode="nt", name=f"ffn{layer}_w1_dgrad", after=comm.started())

    dh3 = ffn_bwd(dy3, h3, f1, 1)
    dx3, dg_pre_ffn1, dy2, dg_post_mix1 = _norm_bwd_pair(dx4, dh3, x3, vec(pre_ffn, 1), y2, vec(post_mix, 1),
                                                         "ffn1_sb_norm_bwd")
    g_sb_out = _matmul(o_sb, dy2, mode="tn", name="sb_out_wgrad", out_dtype=BF16)
    do_sb = _matmul(dy2, comm.weight("sb_out"), mode="nt", name="sb_out_dgrad", out_dtype=BF16)
    dq, dk, dv = _sb_bwd(qkv, do_sb, ltot, D, "sb_bwd")
    dqkv = _sb_pack(dq, dk, dv, "sb_pack")
    g_sb_in = _matmul(h2, dqkv, mode="tn", name="sb_in_wgrad", out_dtype=BF16)
    comm.reduce("sb", {"sb_out": g_sb_out, "sb_in": g_sb_in})
    dh2 = _matmul(dqkv, comm.weight("sb_in"), mode="nt", name="sb_in_dgrad", after=comm.started())
    dx2, dg_pre_mix1, dy1, dg_post_ffn0 = _norm_bwd_pair(dx3, dh2, x2, vec(pre_mix, 1), y1, vec(post_ffn, 0),
                                                         "sb_ffn0_norm_bwd")
    dh1 = ffn_bwd(dy1, h1, f0, 0)
    dx1, dg_pre_ffn0, dy0, dg_post_mix0 = _norm_bwd_pair(dx2, dh1, x1, vec(pre_ffn, 0), y0, vec(post_mix, 0),
                                                         "ffn0_ab_norm_bwd")
    g_ab_out = _matmul(cat, dy0, mode="tn", name="ab_out_wgrad", out_dtype=BF16)
    comm.reduce("ab_out", {"ab_out": g_ab_out})
    dcat = _matmul(dy0, comm.weight("ab_out"), mode="nt", name="ab_out_dgrad", after=comm.started())
    duv, d_ln_g, d_ln_b, d_w_s, d_b_t = _sgu_bwd(z, dcat, ln_g, ln_b, w_s, b_t, "sgu_bwd")
    dz = _join_columns([duv, *_dilated_backward(z, dcat, o_dil, lse_dil, A, "dilated_bwd")], "join_dz")
    g_ab_in = _matmul(h0, dz, mode="tn", name="ab_in_wgrad", out_dtype=BF16)
    comm.reduce("ab_in", {"ab_in": g_ab_in})
    dh0 = _matmul(dz, comm.weight("ab_in"), mode="nt", name="ab_in_dgrad", after=comm.started())
    dx0, dg_pre_mix0 = _prenorm_bwd(dx1, dh0, x, vec(pre_mix, 0), "ab_prenorm_bwd")

    small = {
        "pre_mix": jnp.concatenate([dg_pre_mix0, dg_pre_mix1], axis=0),
        "post_mix": jnp.concatenate([dg_post_mix0, dg_post_mix1], axis=0),
        "pre_ffn": jnp.concatenate([dg_pre_ffn0, dg_pre_ffn1], axis=0),
        "post_ffn": jnp.concatenate([dg_post_ffn0, dg_post_ffn1], axis=0),
        "ln_g": d_ln_g, "ln_b": d_ln_b, "w_s": d_w_s, "b_s": d_b_t.T,
    }
    return loss, dx0, small


MESH_ID = pl.DeviceIdType.MESH
ANY = pl.BlockSpec(memory_space=pl.ANY)


def _coords():
    return lax.axis_index("x"), lax.axis_index("y"), lax.axis_index("c")


def _shard_of(ref, kind, p):
    if kind == "col":
        n = ref.shape[1] // N_DEV
        return ref.at[:, pl.ds(pl.multiple_of(p * n, 128), n)]
    r = ref.shape[0] // N_DEV
    return ref.at[pl.ds(pl.multiple_of(p * r, 16), r), :]


def _full_shape(shard, kind):
    if kind == "col":
        return (shard.shape[0], shard.shape[1] * N_DEV)
    return (shard.shape[0] * N_DEV, shard.shape[1])


def _place(shards, layer, kind, block, after, name):
    _, rows, cols = shards.shape
    tr = _tile(rows, 512)

    def body(b_ref, s_ref, after_ref, o_ref):
        o_ref[...] = s_ref[...].astype(BF16)

    if kind == "col":
        out = pl.BlockSpec((tr, cols), lambda i, b_ref: (i, b_ref[0]))
    else:
        out = pl.BlockSpec((tr, cols), lambda i, b_ref: (b_ref[0] * (rows // tr) + i, 0))
    return pl.pallas_call(
        body, name=name,
        grid_spec=pltpu.PrefetchScalarGridSpec(
            num_scalar_prefetch=1, grid=(rows // tr,),
            in_specs=[pl.BlockSpec((None, tr, cols), lambda i, b_ref: (layer, i, 0)), ANY], out_specs=out),
        out_shape=jax.ShapeDtypeStruct(_full_shape(shards[0], kind), BF16),
        compiler_params=_params("parallel"),
    )(block, shards, after)


HBM = pl.BlockSpec(memory_space=pltpu.HBM)
SEM = pl.BlockSpec(memory_space=pltpu.SEMAPHORE)
FLOWS = pltpu.SideEffectType.DATAFLOW_SIDE_EFFECTING


def _in_hbm(a):
    return pltpu.with_memory_space_constraint(a, pltpu.HBM)


def _hbm_like(bufs):
    return [pltpu.HBM(b.shape, b.dtype) for b in bufs]


def _copies_start(name, bufs, plan, n, after):
    nb = len(bufs)

    def body(*refs):
        send_sems, recv_sems, token = refs[nb + 1], refs[nb + 2], refs[-1]
        for cp in plan(refs[:nb], send_sems, recv_sems):
            cp.start()
        token[...] = jnp.zeros_like(token)

    out = pl.pallas_call(
        body, name=name, in_specs=[HBM] * nb + [ANY],
        out_specs=[SEM, SEM] + [HBM] * nb + [pl.BlockSpec(memory_space=pltpu.VMEM)],
        out_shape=[pltpu.SemaphoreType.DMA((n,)), pltpu.SemaphoreType.DMA((n,))] + _hbm_like(bufs)
        + [jax.ShapeDtypeStruct((8, 128), F32)],
        input_output_aliases={i: 2 + i for i in range(nb)},
        compiler_params=pltpu.CompilerParams(has_side_effects=FLOWS),
    )(*[_in_hbm(b) for b in bufs], after)
    return (out[0], out[1]), list(out[2:2 + nb]), out[-1]


def _copies_wait(name, bufs, sems, after, plan):
    nb = len(bufs)

    def body(*refs):
        for cp in plan(refs[:nb], refs[nb], refs[nb + 1]):
            cp.wait_send()
            cp.wait_recv()

    out = pl.pallas_call(
        body, name=name, in_specs=[HBM] * nb + [SEM, SEM, ANY], out_specs=[HBM] * nb,
        out_shape=_hbm_like(bufs), input_output_aliases={i: i for i in range(nb)},
        compiler_params=pltpu.CompilerParams(has_side_effects=FLOWS),
    )(*bufs, *sems, after)
    return list(out)


def _copies_wait_start(name, bufs, sems, after, plan, next_plan, n_next):
    nb = len(bufs)

    def body(*refs):
        ins = refs[:nb]
        for cp in plan(ins, refs[nb], refs[nb + 1]):
            cp.wait_send()
            cp.wait_recv()
        send_sems, recv_sems, token = refs[nb + 3], refs[nb + 4], refs[-1]
        for cp in next_plan(ins, send_sems, recv_sems):
            cp.start()
        token[...] = jnp.zeros_like(token)

    out = pl.pallas_call(
        body, name=name, in_specs=[HBM] * nb + [SEM, SEM, ANY],
        out_specs=[SEM, SEM] + [HBM] * nb + [pl.BlockSpec(memory_space=pltpu.VMEM)],
        out_shape=[pltpu.SemaphoreType.DMA((n_next,)), pltpu.SemaphoreType.DMA((n_next,))] + _hbm_like(bufs)
        + [jax.ShapeDtypeStruct((8, 128), F32)],
        input_output_aliases={i: 2 + i for i in range(nb)},
        compiler_params=pltpu.CompilerParams(has_side_effects=FLOWS),
    )(*bufs, *sems, after)
    return (out[0], out[1]), list(out[2:2 + nb]), out[-1]


def _gather_plans(kinds):
    nt = len(kinds)

    def slot(refs, t, px, py, pc):
        return _shard_of(refs[t], kinds[t], 4 * px + 2 * py + pc)

    def to_chips(refs, send_sems, recv_sems):
        x, y, c = _coords()
        peers = [(x, y, 1 - c), (1 - x, y, c), (x, 1 - y, c), (1 - x, 1 - y, c)]
        return [pltpu.make_async_remote_copy(
            src_ref=slot(refs, t, x, y, c), dst_ref=slot(refs, t, x, y, c), send_sem=send_sems.at[4 * t + k],
            recv_sem=recv_sems.at[4 * t + k], device_id=peer, device_id_type=MESH_ID)
            for t in range(nt) for k, peer in enumerate(peers)]

    def to_sibling(refs, send_sems, recv_sems):
        x, y, c = _coords()
        chips = [(1 - x, y), (x, 1 - y), (1 - x, 1 - y)]
        return [pltpu.make_async_remote_copy(
            src_ref=slot(refs, t, *chip, c), dst_ref=slot(refs, t, *chip, c), send_sem=send_sems.at[3 * t + j],
            recv_sem=recv_sems.at[3 * t + j], device_id=(x, y, 1 - c), device_id_type=MESH_ID)
            for t in range(nt) for j, chip in enumerate(chips)]

    return to_chips, to_sibling


def _shard_shape(full, kind):
    if kind == "col":
        return (full.shape[0], full.shape[1] // N_DEV)
    return (full.shape[0] // N_DEV, full.shape[1])


def _scatter_plan(kinds):
    nt = len(kinds)

    def plan(refs, send_sems, recv_sems):
        x, y, c = _coords()
        copies = []
        for t in range(nt):
            for k in range(1, N_DEV):
                px = 1 - x if (k >> 2) & 1 else x
                py = 1 - y if (k >> 1) & 1 else y
                pc = 1 - c if k & 1 else c
                copies.append(pltpu.make_async_remote_copy(
                    src_ref=_shard_of(refs[t], kinds[t], 4 * px + 2 * py + pc),
                    dst_ref=refs[nt + t].at[4 * x + 2 * y + c],
                    send_sem=send_sems.at[7 * t + k - 1], recv_sem=recv_sems.at[7 * t + k - 1],
                    device_id=(px, py, pc), device_id_type=MESH_ID))
        return copies
    return plan


def _partial_specs(full, kind, tr):
    rows, cols = _shard_shape(full, kind)
    steps = rows // tr
    if kind == "col":
        own = pl.BlockSpec((tr, cols), lambda i, w: (i, w[0]))
    else:
        own = pl.BlockSpec((tr, cols), lambda i, w: (w[0] * steps + i, 0))
    return [own] + [pl.BlockSpec((None, tr, cols), lambda i, w, k=k: (w[k], i, 0)) for k in range(1, N_DEV)]


def _all_reduce_small(vec, after):
    R = vec.shape[0]

    def body(v_ref, after_ref, o_ref, recv_ref, send_sems, recv_sems):
        x, y, c = _coords()
        me = 4 * x + 2 * y + c
        recv_ref[me] = v_ref[...]
        copies = []
        for k in range(1, N_DEV):
            bx, by, bc = (k >> 2) & 1, (k >> 1) & 1, k & 1
            peer = (1 - x if bx else x, 1 - y if by else y, 1 - c if bc else c)
            copies.append(pltpu.make_async_remote_copy(
                src_ref=v_ref, dst_ref=recv_ref.at[me],
                send_sem=send_sems.at[k - 1], recv_sem=recv_sems.at[k - 1],
                device_id=peer, device_id_type=MESH_ID))
        for cp in copies:
            cp.start()
        for cp in copies:
            cp.wait()
        total = recv_ref[0]
        for p in range(1, N_DEV):
            total = total + recv_ref[p]
        o_ref[...] = total

    return pl.pallas_call(
        body, name="all_reduce_small",
        in_specs=[pl.BlockSpec(memory_space=pltpu.VMEM), ANY], out_specs=pl.BlockSpec(memory_space=pltpu.VMEM),
        out_shape=jax.ShapeDtypeStruct((R, 128), F32),
        scratch_shapes=[pltpu.VMEM((N_DEV, R, 128), F32), pltpu.SemaphoreType.DMA((N_DEV - 1,)),
                        pltpu.SemaphoreType.DMA((N_DEV - 1,))],
        compiler_params=pltpu.CompilerParams(vmem_limit_bytes=VMEM_LIMIT),
    )(vec, after)


def _adamw_math(w, g, m, v):
    m = ADAM_B1 * m + (1.0 - ADAM_B1) * g
    v = ADAM_B2 * v + (1.0 - ADAM_B2) * (g * g)
    m_hat = m / (1.0 - ADAM_B1 ** ADAM_STEP)
    v_hat = v / (1.0 - ADAM_B2 ** ADAM_STEP)
    delta = -ADAM_LR * (m_hat / (jnp.sqrt(v_hat) + ADAM_EPS) + ADAM_WD * w)
    return delta, m, v


def _adamw(w, grads, kind, where, m, v, name):
    layers, rows, cols = w.shape
    tr = _tile(rows, 128)
    out = None
    for layer, (grad, land) in enumerate(grads):
        def body(w_ref, *refs):
            parts, (x_ref, m_ref, v_ref) = refs[:N_DEV], refs[N_DEV:N_DEV + 3]
            g_ref, d_ref, mo_ref, vo_ref = refs[-4:]
            g = parts[0][...].astype(F32)
            for p_ref in parts[1:]:
                g = g + p_ref[...].astype(F32)
            g_ref[...] = g
            d_ref[...], mo_ref[...], vo_ref[...] = _adamw_math(x_ref[...], g, m_ref[...], v_ref[...])

        blk = pl.BlockSpec((None, tr, cols), lambda i, w_, layer=layer: (layer, i, 0))
        earlier = [] if out is None else list(out)
        out = pl.pallas_call(
            body, name=f"{name}_{layer}",
            grid_spec=pltpu.PrefetchScalarGridSpec(
                num_scalar_prefetch=1, grid=(rows // tr,),
                in_specs=_partial_specs(grad, kind, tr) + [blk] * 3 + [ANY] * len(earlier),
                out_specs=[blk] * 4),
            out_shape=[jax.ShapeDtypeStruct((layers, rows, cols), F32)] * 4,
            input_output_aliases={N_DEV + 4 + k: k for k in range(len(earlier))},
            compiler_params=_params("parallel"),
        )(where, grad, *[land] * (N_DEV - 1), w, m, v, *earlier)
    return out


def _adamw_small(w, g, m, v):
    def body(w_ref, g_ref, m_ref, v_ref, d_ref, mo_ref, vo_ref):
        d_ref[...], mo_ref[...], vo_ref[...] = _adamw_math(w_ref[...], g_ref[...], m_ref[...], v_ref[...])

    whole = pl.BlockSpec(memory_space=pltpu.VMEM)
    return pl.pallas_call(
        body, name="adamw_small", in_specs=[whole] * 4, out_specs=[whole] * 3,
        out_shape=[jax.ShapeDtypeStruct(w.shape, F32)] * 3,
        compiler_params=pltpu.CompilerParams(vmem_limit_bytes=VMEM_LIMIT),
    )(w, g, m, v)


def _pack(arrays):
    rows = []
    for a in arrays:
        flat = a.reshape(-1)
        pad = (-flat.shape[0]) % 1024
        rows.append(jnp.pad(flat, (0, pad)).reshape(-1, 128))
    return jnp.concatenate(rows, axis=0)


def _unpack(packed, like):
    out, r = [], 0
    for a in like:
        n = math.prod(a.shape)
        nr = (n + 1023) // 1024 * 8
        out.append(packed[r:r + nr].reshape(-1)[:n].reshape(a.shape))
        r += nr
    return out


KIND = {"ab_in": "col", "ab_out": "row", "sb_in": "col", "sb_out": "row",
        "w1_0": "col", "w1_1": "col", "w2_0": "row", "w2_1": "row"}
GATHERS = {"ab_in": ("ab_in",), "ab_out": ("ab_out",), "ffn0": ("w1_0", "w2_0"),
           "sb": ("sb_in", "sb_out"), "ffn1": ("w1_1", "w2_1")}


class _Exchange:
    def __init__(self, shards):
        x, y, c = _coords()
        me = (4 * x + 2 * y + c).astype(jnp.int32)
        self.where = jnp.stack([jnp.bitwise_xor(me, k) for k in range(N_DEV)])
        self.full = {}
        self.tokens = []
        self.gathers = {}
        self.scatters = {}
        self.settled = {}
        block = me.reshape(1)
        after = block
        for key, group in GATHERS.items():
            for n in group:
                self.full[n] = _place(*shards[n], KIND[n], block, after, f"place_{n}")
            to_chips, to_sibling = _gather_plans([KIND[n] for n in group])
            bufs = [self.full[n] for n in group]
            sems, bufs, after = _copies_start(f"gather_start_{key}", bufs, to_chips, 4 * len(group), after)
            self.tokens.append(after)
            self.gathers[key] = (group, sems, bufs, to_chips, to_sibling)

    def tie(self, small):
        for token in self.tokens:
            small = small + token[0:1, 0:1]
        self.tokens = []
        return small

    def started(self):
        return tuple(self.tokens)

    def weight(self, name):
        return self.full[name]

    def arrive(self, key, after):
        group, sems, bufs, to_chips, to_sibling = self.gathers[key]
        sems, bufs, token = _copies_wait_start(f"gather_pass_{key}", bufs, sems, after, to_chips, to_sibling,
                                               3 * len(group))
        self.tokens.append(token)
        self.gathers[key] = (group, sems, bufs, token, to_sibling)

    def land(self, key, after=None):
        group, sems, bufs, token, to_sibling = self.gathers.pop(key)
        after = token if after is None else after
        self.full.update(zip(group, _copies_wait(f"gather_done_{key}", bufs, sems, after, to_sibling)))

    def reduce(self, key, grads):
        names = list(grads)
        kinds = [KIND[n] for n in names]
        full = [grads[n] for n in names]
        lands = [lax.empty((N_DEV,) + _shard_shape(g, k), BF16) for g, k in zip(full, kinds)]
        plan = _scatter_plan(kinds)
        sems, bufs, token = _copies_start(f"scatter_start_{key}", full + lands, plan, (N_DEV - 1) * len(names),
                                          full[-1])
        self.tokens.append(token)
        self.scatters[key] = (names, sems, bufs, plan)

    def settle(self, keys, after):
        for key in keys:
            names, sems, bufs, plan = self.scatters.pop(key)
            bufs = _copies_wait(f"scatter_done_{key}", bufs, sems, after, plan)
            self.settled.update({n: t for n, *t in zip(names, bufs[:len(names)], bufs[len(names):])})
        return self.settled


SMALL = ("norm_pre_mix", "norm_post_mix", "norm_pre_ffn", "norm_post_ffn", "sgu_ln_g", "sgu_ln_b", "sgu_w", "sgu_b")
ORDER = ("norm_pre_mix", "norm_post_mix", "norm_pre_ffn", "norm_post_ffn", "ab_w_in", "sgu_ln_g", "sgu_ln_b", "sgu_w",
         "sgu_b", "ab_w_out", "sb_w_in", "sb_w_out", "ffn_w1", "ffn_w2")


def kernel(x, norm_pre_mix, norm_post_mix, norm_pre_ffn, norm_post_ffn, ab_w_in, sgu_ln_g, sgu_ln_b, sgu_w, sgu_b, ab_w_out, sb_w_in, sb_w_out, ffn_w1, ffn_w2, loss_target, m_norm_pre_mix, m_norm_post_mix, m_norm_pre_ffn, m_norm_post_ffn, m_ab_w_in, m_sgu_ln_g, m_sgu_ln_b, m_sgu_w, m_sgu_b, m_ab_w_out, m_sb_w_in, m_sb_w_out, m_ffn_w1, m_ffn_w2, v_norm_pre_mix, v_norm_post_mix, v_norm_pre_ffn, v_norm_post_ffn, v_ab_w_in, v_sgu_ln_g, v_sgu_ln_b, v_sgu_w, v_sgu_b, v_ab_w_out, v_sb_w_in, v_sb_w_out, v_ffn_w1, v_ffn_w2):
    W = dict(norm_pre_mix=norm_pre_mix, norm_post_mix=norm_post_mix, norm_pre_ffn=norm_pre_ffn,
             norm_post_ffn=norm_post_ffn, ab_w_in=ab_w_in, sgu_ln_g=sgu_ln_g, sgu_ln_b=sgu_ln_b, sgu_w=sgu_w,
             sgu_b=sgu_b, ab_w_out=ab_w_out, sb_w_in=sb_w_in, sb_w_out=sb_w_out, ffn_w1=ffn_w1, ffn_w2=ffn_w2)
    M = dict(norm_pre_mix=m_norm_pre_mix, norm_post_mix=m_norm_post_mix, norm_pre_ffn=m_norm_pre_ffn,
             norm_post_ffn=m_norm_post_ffn, ab_w_in=m_ab_w_in, sgu_ln_g=m_sgu_ln_g, sgu_ln_b=m_sgu_ln_b,
             sgu_w=m_sgu_w, sgu_b=m_sgu_b, ab_w_out=m_ab_w_out, sb_w_in=m_sb_w_in, sb_w_out=m_sb_w_out,
             ffn_w1=m_ffn_w1, ffn_w2=m_ffn_w2)
    V = dict(norm_pre_mix=v_norm_pre_mix, norm_post_mix=v_norm_post_mix, norm_pre_ffn=v_norm_pre_ffn,
             norm_post_ffn=v_norm_post_ffn, ab_w_in=v_ab_w_in, sgu_ln_g=v_sgu_ln_g, sgu_ln_b=v_sgu_ln_b,
             sgu_w=v_sgu_w, sgu_b=v_sgu_b, ab_w_out=v_ab_w_out, sb_w_in=v_sb_w_in, sb_w_out=v_sb_w_out,
             ffn_w1=v_ffn_w1, ffn_w2=v_ffn_w2)

    shards = {"ab_in": (ab_w_in, 0), "ab_out": (ab_w_out, 0), "w1_0": (ffn_w1, 0), "w2_0": (ffn_w2, 0),
              "sb_in": (sb_w_in, 0), "sb_out": (sb_w_out, 0), "w1_1": (ffn_w1, 1), "w2_1": (ffn_w2, 1)}
    comm = _Exchange(shards)
    norms = (norm_pre_mix, norm_post_mix, norm_pre_ffn, norm_post_ffn)
    sgu = (sgu_ln_g, sgu_ln_b, sgu_w[0], sgu_b[0])
    loss, dx, small = _local_step(x[0], loss_target[0], norms, sgu, comm)

    out = {}

    def update(name, layers, landed):
        out[name] = _adamw(W[name], [landed[n] for n in layers], KIND[layers[0]], comm.where, M[name], V[name],
                           f"adamw_{name}")

    landed = comm.settle(("ffn1", "sb", "ffn0"), after=dx)
    for name, layers in (("sb_w_in", ["sb_in"]), ("sb_w_out", ["sb_out"]), ("ffn_w1", ["w1_0", "w1_1"]),
                         ("ffn_w2", ["w2_0", "w2_1"])):
        update(name, layers, landed)
    small_g = [small["pre_mix"], small["post_mix"], small["pre_ffn"], small["post_ffn"], small["ln_g"],
               small["ln_b"], small["w_s"][None], small["b_s"][None]]
    summed = _all_reduce_small(_pack(small_g + [loss.reshape(1)]), out["ffn_w2"][3])
    g_small, loss = summed[:-8], summed[-8, 0]
    landed = comm.settle(("ab_out", "ab_in"), after=g_small)
    update("ab_w_out", ["ab_out"], landed)
    update("ab_w_in", ["ab_in"], landed)
    res = _adamw_small(_pack([W[n] for n in SMALL]), g_small, _pack([M[n] for n in SMALL]),
                       _pack([V[n] for n in SMALL]))
    like = [W[n] for n in SMALL]
    for n, *vals in zip(SMALL, *[_unpack(r, like) for r in [g_small] + list(res)]):
        out[n] = vals

    return (loss, dx[None], *[out[n][0] for n in ORDER], *[out[n][1] for n in ORDER],
            *[out[n][2] for n in ORDER], *[out[n][3] for n in ORDER])
```

```python
import math

import jax
import jax.numpy as jnp
from jax import lax
from jax.experimental import pallas as pl
from jax.experimental.pallas import tpu as pltpu

F32 = jnp.float32
BF16 = jnp.bfloat16

HEAD_DIM = 128
CHUNK = 128
ATT_BLOCK = 128
DILATED_PAIRS = ((128, 1), (512, 4), (2048, 16))
RMS_EPS = 1e-6
LN_EPS = 1e-5
ADAM_LR = 0.001
ADAM_B1 = 0.9
ADAM_B2 = 0.999
ADAM_EPS = 1e-08
ADAM_WD = 0.01
ADAM_STEP = 10
N_DEV = 8
MASKED = -1e30

V7X_VMEM_BYTES = 64 * 1024 * 1024
VMEM_LIMIT = V7X_VMEM_BYTES - 8 * 1024 * 1024

NN = (((1,), (0,)), ((), ()))
NT = (((1,), (1,)), ((), ()))
TN = (((0,), (0,)), ((), ()))


def _params(*sem):
    return pltpu.CompilerParams(dimension_semantics=sem, vmem_limit_bytes=VMEM_LIMIT)


def _dot(a, b, dims=NN):
    return lax.dot_general(a, b, dims, preferred_element_type=F32)


def _tile(n, preferred):
    if n <= preferred:
        return n
    t = preferred - preferred % 128
    while n % t:
        t -= 128
    assert t > 0, (n, preferred)
    return t


def _matmul(a, b, *, mode, name, out_dtype=F32, tm=1024, tn=1024, tk=2048, epi=None, extras=(), after=()):
    if mode == "nn":
        (M, K), N = a.shape, b.shape[1]
    elif mode == "nt":
        (M, K), N = a.shape, b.shape[0]
    else:
        (K, M), N = a.shape, b.shape[1]
    tm, tn, tk = _tile(M, tm), _tile(N, tn), _tile(K, tk)
    nk = K // tk
    if mode == "tn":
        a_spec = pl.BlockSpec((tk, tm), lambda i, j, k: (k, i))
    else:
        a_spec = pl.BlockSpec((tm, tk), lambda i, j, k: (i, k))
    if mode == "nt":
        b_spec = pl.BlockSpec((tn, tk), lambda i, j, k: (j, k))
    else:
        b_spec = pl.BlockSpec((tk, tn), lambda i, j, k: (k, j))
    o_spec = pl.BlockSpec((tm, tn), lambda i, j, k: (i, j))
    dims = {"nn": NN, "nt": NT, "tn": TN}[mode]
    n_extra = len(extras)
    n_in = n_extra + len(after)
    several = isinstance(out_dtype, tuple)
    n_out = len(out_dtype) if several else 1

    def finish(acc, rest):
        outs = acc if epi is None else epi(acc, pl.program_id(1), *[r[...] for r in rest[:n_extra]])
        for o_ref, o in zip(rest[n_in:n_in + n_out], outs if several else (outs,)):
            o_ref[...] = o.astype(o_ref.dtype)

    if nk == 1:
        def body(a_ref, b_ref, *rest):
            finish(_dot(a_ref[...], b_ref[...], dims), rest)
        scratch = []
    else:
        def body(a_ref, b_ref, *rest):
            acc_ref = rest[n_in + n_out]
            k = pl.program_id(2)

            @pl.when(k == 0)
            def _():
                acc_ref[...] = jnp.zeros_like(acc_ref)

            acc_ref[...] += _dot(a_ref[...], b_ref[...], dims)

            @pl.when(k == nk - 1)
            def _():
                finish(acc_ref[...], rest)
        scratch = [pltpu.VMEM((tm, tn), F32)]

    shapes = [jax.ShapeDtypeStruct((M, N), d) for d in (out_dtype if several else (out_dtype,))]
    return pl.pallas_call(
        body,
        name=name,
        grid=(M // tm, N // tn, nk),
        in_specs=[a_spec, b_spec] + [o_spec] * n_extra + [ANY] * len(after),
        out_specs=[o_spec] * n_out if several else o_spec,
        out_shape=shapes if several else shapes[0],
        scratch_shapes=scratch,
        compiler_params=_params("parallel", "parallel", "arbitrary"),
    )(a, b, *extras, *after)


ROWS = 512


def _rms(x):
    return lax.rsqrt(jnp.mean(x * x, axis=-1, keepdims=True) + RMS_EPS)


def _prenorm(x, g, name):
    T, D = x.shape

    def body(x_ref, g_ref, h_ref):
        xv = x_ref[...]
        h_ref[...] = (xv * _rms(xv) * g_ref[...]).astype(BF16)

    row = pl.BlockSpec((ROWS, D), lambda i: (i, 0))
    vec = pl.BlockSpec((1, D), lambda i: (0, 0))
    return pl.pallas_call(
        body, name=name, grid=(T // ROWS,), in_specs=[row, vec], out_specs=row,
        out_shape=jax.ShapeDtypeStruct((T, D), BF16), compiler_params=_params("parallel"),
    )(x, g)


def _postnorm_prenorm(x, y, g_post, g_pre, name):
    T, D = x.shape

    def body(x_ref, y_ref, gp_ref, gn_ref, xo_ref, h_ref):
        yv = y_ref[...]
        xn = x_ref[...] + yv * _rms(yv) * gp_ref[...]
        xo_ref[...] = xn
        h_ref[...] = (xn * _rms(xn) * gn_ref[...]).astype(BF16)

    row = pl.BlockSpec((ROWS, D), lambda i: (i, 0))
    vec = pl.BlockSpec((1, D), lambda i: (0, 0))
    return pl.pallas_call(
        body, name=name, grid=(T // ROWS,), in_specs=[row, row, vec, vec], out_specs=[row, row],
        out_shape=[jax.ShapeDtypeStruct((T, D), F32), jax.ShapeDtypeStruct((T, D), BF16)],
        compiler_params=_params("parallel"),
    )(x, y, g_post, g_pre)


def _postnorm_grads(dn, yh, r, g):
    gd = dn * g
    return r * (gd - yh * jnp.mean(yh * gd, axis=-1, keepdims=True)), dn * yh


def _postnorm_loss(x, y, g_post, target, name):
    T, D = x.shape

    def body(x_ref, y_ref, gp_ref, t_ref, loss_ref, dx_ref, dy_ref, dg_ref):
        @pl.when(pl.program_id(0) == 0)
        def _():
            loss_ref[...] = jnp.zeros_like(loss_ref)
            dg_ref[...] = jnp.zeros_like(dg_ref)

        yv = y_ref[...]
        r = _rms(yv)
        yh = yv * r
        err = x_ref[...] + yh * gp_ref[...] - t_ref[...]
        dx = err * (1.0 / D)
        dx_ref[...] = dx
        loss_ref[...] += 0.5 * jnp.sum(jnp.sum(err * err, axis=-1, keepdims=True) * (1.0 / D))
        dy, dg = _postnorm_grads(dx, yh, r, gp_ref[...])
        dy_ref[...] = dy.astype(BF16)
        dg_ref[...] += jnp.sum(dg, axis=0, keepdims=True)

    row = pl.BlockSpec((ROWS, D), lambda i: (i, 0))
    vec = pl.BlockSpec((1, D), lambda i: (0, 0))
    acc = pl.BlockSpec((8, 128), lambda i: (0, 0))
    return pl.pallas_call(
        body, name=name, grid=(T // ROWS,), in_specs=[row, row, vec, row], out_specs=[acc, row, row, vec],
        out_shape=[jax.ShapeDtypeStruct((8, 128), F32), jax.ShapeDtypeStruct((T, D), F32),
                   jax.ShapeDtypeStruct((T, D), BF16), jax.ShapeDtypeStruct((1, D), F32)],
        compiler_params=_params("arbitrary"),
    )(x, y, g_post, target)


def _norm_bwd_pair(dx_out, dh, x, g_pre, y_prev, g_post_prev, name):
    T, D = x.shape

    def body(dxo_ref, dh_ref, x_ref, g_ref, y_ref, gp_ref, dx_ref, dg_ref, dy_ref, dgp_ref):
        @pl.when(pl.program_id(0) == 0)
        def _():
            dg_ref[...] = jnp.zeros_like(dg_ref)
            dgp_ref[...] = jnp.zeros_like(dgp_ref)

        xv, dhv = x_ref[...], dh_ref[...]
        r = _rms(xv)
        xh = xv * r
        gd = dhv * g_ref[...]
        dx = dxo_ref[...] + r * (gd - xh * jnp.mean(xh * gd, axis=-1, keepdims=True))
        dx_ref[...] = dx
        dg_ref[...] += jnp.sum(dhv * xh, axis=0, keepdims=True)
        yv = y_ref[...]
        ry = _rms(yv)
        dy, dgp = _postnorm_grads(dx, yv * ry, ry, gp_ref[...])
        dy_ref[...] = dy.astype(BF16)
        dgp_ref[...] += jnp.sum(dgp, axis=0, keepdims=True)

    row = pl.BlockSpec((ROWS, D), lambda i: (i, 0))
    vec = pl.BlockSpec((1, D), lambda i: (0, 0))
    return pl.pallas_call(
        body, name=name, grid=(T // ROWS,), in_specs=[row, row, row, vec, row, vec],
        out_specs=[row, vec, row, vec],
        out_shape=[jax.ShapeDtypeStruct((T, D), F32), jax.ShapeDtypeStruct((1, D), F32),
                   jax.ShapeDtypeStruct((T, D), BF16), jax.ShapeDtypeStruct((1, D), F32)],
        compiler_params=_params("arbitrary"),
    )(dx_out, dh, x, g_pre, y_prev, g_post_prev)


def _prenorm_bwd(dx_out, dh, x, g_pre, name):
    T, D = x.shape

    def body(dxo_ref, dh_ref, x_ref, g_ref, dx_ref, dg_ref):
        @pl.when(pl.program_id(0) == 0)
        def _():
            dg_ref[...] = jnp.zeros_like(dg_ref)

        xv, dhv = x_ref[...], dh_ref[...]
        r = _rms(xv)
        xh = xv * r
        gd = dhv * g_ref[...]
        dx_ref[...] = dxo_ref[...] + r * (gd - xh * jnp.mean(xh * gd, axis=-1, keepdims=True))
        dg_ref[...] += jnp.sum(dhv * xh, axis=0, keepdims=True)

    row = pl.BlockSpec((ROWS, D), lambda i: (i, 0))
    vec = pl.BlockSpec((1, D), lambda i: (0, 0))
    return pl.pallas_call(
        body, name=name, grid=(T // ROWS,), in_specs=[row, row, row, vec], out_specs=[row, vec],
        out_shape=[jax.ShapeDtypeStruct((T, D), F32), jax.ShapeDtypeStruct((1, D), F32)],
        compiler_params=_params("arbitrary"),
    )(dx_out, dh, x, g_pre)


_INV_SQRT2 = 1.0 / math.sqrt(2.0)
_INV_SQRT2PI = 1.0 / math.sqrt(2.0 * math.pi)


def _gelu(x):
    return 0.5 * x * (1.0 + lax.erf(x * _INV_SQRT2))


def _gelu_grad(x):
    return 0.5 * (1.0 + lax.erf(x * _INV_SQRT2)) + x * jnp.exp(-0.5 * x * x) * _INV_SQRT2PI


def _layernorm_stats(x):
    mu = jnp.mean(x, axis=-1, keepdims=True)
    xc = x - mu
    rstd = lax.rsqrt(jnp.mean(xc * xc, axis=-1, keepdims=True) + LN_EPS)
    return xc * rstd, rstd


def _tril_mask():
    i = lax.broadcasted_iota(jnp.int32, (CHUNK, CHUNK), 0)
    j = lax.broadcasted_iota(jnp.int32, (CHUNK, CHUNK), 1)
    return j <= i


SGU_ROWS = 512


def _sgu_fwd(z, ln_g, ln_b, w_s, b_t, name):
    T = z.shape[0]
    A = ln_g.shape[1]
    G = A // 128
    rows = min(SGU_ROWS, T)

    def body(u_ref, v_ref, g_ref, b_ref, w_ref, bt_ref, o_ref):
        mask = _tril_mask()
        for c in range(rows // CHUNK):
            rs = pl.ds(c * CHUNK, CHUNK)
            xh, _ = _layernorm_stats(_gelu(v_ref[rs, :]))
            vn = (xh * g_ref[...] + b_ref[...]).astype(BF16)
            for g in range(G):
                cs = pl.ds(g * 128, 128)
                w = jnp.where(mask, w_ref[g], 0.0).astype(BF16)
                mixed = _dot(w, vn[:, g * 128:(g + 1) * 128]) + bt_ref[:, g:g + 1]
                o_ref[rs, cs] = (_gelu(u_ref[rs, cs]) * mixed).astype(BF16)

    return pl.pallas_call(
        body, name=name, grid=(T // rows,),
        in_specs=[
            pl.BlockSpec((rows, A), lambda i: (i, 0)),
            pl.BlockSpec((rows, A), lambda i: (i, 1)),
            pl.BlockSpec((1, A), lambda i: (0, 0)),
            pl.BlockSpec((1, A), lambda i: (0, 0)),
            pl.BlockSpec((G, CHUNK, CHUNK), lambda i: (0, 0, 0)),
            pl.BlockSpec((CHUNK, G), lambda i: (0, 0)),
        ],
        out_specs=pl.BlockSpec((rows, A), lambda i: (i, 0)),
        out_shape=jax.ShapeDtypeStruct((T, 2 * A), BF16),
        compiler_params=_params("parallel"),
    )(z, z, ln_g, ln_b, w_s, b_t)


def _sgu_bwd(z, dcat, ln_g, ln_b, w_s, b_t, name):
    T = z.shape[0]
    A = ln_g.shape[1]
    G = A // 128
    rows = min(SGU_ROWS, T)

    def body(u_ref, v_ref, da_ref, g_ref, b_ref, w_ref, bt_ref, dz_ref, dg_ref, db_ref, dw_ref, dbt_ref, dvn_ref):
        @pl.when(pl.program_id(0) == 0)
        def _():
            dg_ref[...] = jnp.zeros_like(dg_ref)
            db_ref[...] = jnp.zeros_like(db_ref)
            dw_ref[...] = jnp.zeros_like(dw_ref)
            dbt_ref[...] = jnp.zeros_like(dbt_ref)

        mask = _tril_mask()
        for c in range(rows // CHUNK):
            rs = pl.ds(c * CHUNK, CHUNK)
            vv = v_ref[rs, :]
            gv = _gelu(vv)
            xh, rstd = _layernorm_stats(gv)
            vn = (xh * g_ref[...] + b_ref[...]).astype(BF16)
            for g in range(G):
                cs = pl.ds(g * 128, 128)
                w = jnp.where(mask, w_ref[g], 0.0).astype(BF16)
                vg = vn[:, g * 128:(g + 1) * 128]
                mixed = _dot(w, vg) + bt_ref[:, g:g + 1]
                uu = u_ref[rs, cs]
                da = da_ref[rs, cs]
                dz_ref[rs, cs] = (da * mixed * _gelu_grad(uu)).astype(BF16)
                dm = da * _gelu(uu)
                dmb = dm.astype(BF16)
                dbt_ref[:, g:g + 1] += jnp.sum(dm, axis=1, keepdims=True)
                dw_ref[g] += jnp.where(mask, _dot(dmb, vg, NT), 0.0)
                dvn_ref[:, cs] = _dot(w, dmb, TN)
            dvn = dvn_ref[...]
            dg_ref[...] += jnp.sum(dvn * xh, axis=0, keepdims=True)
            db_ref[...] += jnp.sum(dvn, axis=0, keepdims=True)
            dxh = dvn * g_ref[...]
            dgv = rstd * (dxh - jnp.mean(dxh, axis=-1, keepdims=True)
                          - xh * jnp.mean(dxh * xh, axis=-1, keepdims=True))
            dz_ref[rs, pl.ds(A, A)] = (dgv * _gelu_grad(vv)).astype(BF16)

    vec = pl.BlockSpec((1, A), lambda i: (0, 0))
    wsp = pl.BlockSpec((G, CHUNK, CHUNK), lambda i: (0, 0, 0))
    bsp = pl.BlockSpec((CHUNK, G), lambda i: (0, 0))
    return pl.pallas_call(
        body, name=name, grid=(T // rows,),
        in_specs=[
            pl.BlockSpec((rows, A), lambda i: (i, 0)),
            pl.BlockSpec((rows, A), lambda i: (i, 1)),
            pl.BlockSpec((rows, A), lambda i: (i, 0)),
            vec, vec, wsp, bsp,
        ],
        out_specs=[pl.BlockSpec((rows, 2 * A), lambda i: (i, 0)), vec, vec, wsp, bsp],
        out_shape=[
            jax.ShapeDtypeStruct((T, 2 * A), BF16),
            jax.ShapeDtypeStruct((1, A), F32),
            jax.ShapeDtypeStruct((1, A), F32),
            jax.ShapeDtypeStruct((G, CHUNK, CHUNK), F32),
            jax.ShapeDtypeStruct((CHUNK, G), F32),
        ],
        scratch_shapes=[pltpu.VMEM((CHUNK, A), F32)],
        compiler_params=_params("arbitrary"),
    )(z, z, dcat, ln_g, ln_b, w_s, b_t)


def _alibi_row(B, d):
    H = B // HEAD_DIM
    slopes = [d * 2.0 ** (-8.0 * (h + 1.0) / H) for h in range(H)]
    return jnp.repeat(jnp.asarray(slopes, F32), HEAD_DIM)[None, :]


def _dil_scores(q, k, slope_d, valid, dist):
    s = _dot(q, k, NT) - slope_d * dist
    return jnp.where(valid, s, MASKED)


def _dil_block(n, r, d):
    if d == 1:
        return pl.ds(pl.multiple_of(n * ATT_BLOCK, ATT_BLOCK), ATT_BLOCK)
    return pl.ds(n * (d * ATT_BLOCK) + r, ATT_BLOCK, stride=d)


def _dilated_forward(z, cat, B, name):
    T = z.shape[0]
    H = B // HEAD_DIM
    A = cat.shape[1] - B
    scale = HEAD_DIM ** -0.5
    blk = ATT_BLOCK
    chunk = _tile(T, 512)

    def body(q_ref, k_ref, v_ref, sl_ref, cat_in, cat_ref, of_ref, lt_ref, *branch):
        o_refs, l_refs = branch[:3], branch[3:]
        slope = sl_ref[:, :1]
        qi = lax.broadcasted_iota(jnp.int32, (blk, 2 * blk), 0)
        kj = lax.broadcasted_iota(jnp.int32, (blk, 2 * blk), 1)
        dist = qi + blk - kj
        band = (dist >= 0) & (dist <= blk)
        distf = dist.astype(F32)

        for b, (_, d) in enumerate(DILATED_PAIRS):
            def one(n, r, b=b, d=d):
                rows, prev = _dil_block(n, r, d), _dil_block(jnp.maximum(n - 1, 0), r, d)
                q = (q_ref[rows, :] * scale).astype(BF16)
                k = jnp.concatenate([k_ref[prev, :], k_ref[rows, :]], axis=0).astype(BF16)
                v = jnp.concatenate([v_ref[prev, :], v_ref[rows, :]], axis=0).astype(BF16)
                s = _dil_scores(q, k, slope * float(d), band & ((kj >= blk) | (n > 0)), distf)
                m = jnp.max(s, axis=-1, keepdims=True)
                p = jnp.exp(s - m)
                den = jnp.sum(p, axis=-1, keepdims=True)
                o_refs[b][rows, :] = _dot(p.astype(BF16), v) / den
                l_refs[b][rows, :] = jnp.broadcast_to(m + jnp.log(den), (blk, HEAD_DIM))

            per = max(1, 4 // d)

            def step(i, _, d=d, per=per, one=one):
                for u in range(per):
                    for r in range(d):
                        one(i * per + u, r)
                return 0

            lax.fori_loop(0, T // (d * blk * per), step, 0)

        def merge(i, _):
            rs = pl.ds(pl.multiple_of(i * chunk, chunk), chunk)
            a, b, c = l_refs[0][rs, :], l_refs[1][rs, :], l_refs[2][rs, :]
            m = jnp.maximum(jnp.maximum(a, b), c)
            ea, eb, ec = jnp.exp(a - m), jnp.exp(b - m), jnp.exp(c - m)
            tot = ea + eb + ec
            o = (ea * o_refs[0][rs, :] + eb * o_refs[1][rs, :] + ec * o_refs[2][rs, :]) / tot
            of_ref[rs, :] = o
            cat_ref[rs, :] = o.astype(BF16)
            lt_ref[rs, :] = m + jnp.log(tot)
            return 0

        lax.fori_loop(0, T // chunk, merge, 0)

    def col(unit):
        return lambda h: (0, unit * H + h)

    seq = (T, HEAD_DIM)
    out = pl.BlockSpec(seq, lambda h: (0, h))
    return pl.pallas_call(
        body, name=name, grid=(H,),
        in_specs=[pl.BlockSpec(seq, col(2)), pl.BlockSpec(seq, col(3)), pl.BlockSpec(seq, col(4)),
                  pl.BlockSpec((1, HEAD_DIM), lambda h: (0, h)), ANY],
        out_specs=[pl.BlockSpec(seq, lambda h: (0, A // HEAD_DIM + h)), out, out],
        out_shape=[jax.ShapeDtypeStruct(cat.shape, BF16), jax.ShapeDtypeStruct((T, B), F32),
                   jax.ShapeDtypeStruct((T, B), F32)],
        input_output_aliases={4: 0},
        scratch_shapes=[pltpu.VMEM(seq, F32)] * 6,
        compiler_params=_params("parallel"),
    )(z, z, z, _alibi_row(B, 1), cat)


def _dilated_backward(z, dcat, o, lse, B, name):
    T = z.shape[0]
    H = B // HEAD_DIM
    scale = HEAD_DIM ** -0.5
    blk = ATT_BLOCK
    chunk = _tile(T, 512)

    def body(q_ref, k_ref, v_ref, do_ref, o_ref, l_ref, sl_ref, dq_ref, dk_ref, dv_ref, aq_ref, ak_ref, av_ref):
        slope = sl_ref[:, :1]
        qi = lax.broadcasted_iota(jnp.int32, (blk, 2 * blk), 0)
        kj = lax.broadcasted_iota(jnp.int32, (blk, 2 * blk), 1)
        dist = qi + blk - kj
        band = (dist >= 0) & (dist <= blk)
        distf = dist.astype(F32)
        for acc in (aq_ref, ak_ref, av_ref):
            acc[...] = jnp.zeros_like(acc)

        for _, d in DILATED_PAIRS:
            def one(n, r, d=d):
                rows, prev = _dil_block(n, r, d), _dil_block(jnp.maximum(n - 1, 0), r, d)
                q = (q_ref[rows, :] * scale).astype(BF16)
                k = jnp.concatenate([k_ref[prev, :], k_ref[rows, :]], axis=0).astype(BF16)
                v = jnp.concatenate([v_ref[prev, :], v_ref[rows, :]], axis=0).astype(BF16)
                do = do_ref[rows, :]
                delta = jnp.sum(do * o_ref[rows, :], axis=-1, keepdims=True)
                do = do.astype(BF16)
                s = _dil_scores(q, k, slope * float(d), band & ((kj >= blk) | (n > 0)), distf)
                p = jnp.exp(s - l_ref[rows, :][:, :1])
                ds = (p * (_dot(do, v, NT) - delta)).astype(BF16)
                aq_ref[rows, :] += _dot(ds, k)
                dk = _dot(ds, q, TN)
                dv = _dot(p.astype(BF16), do, TN)
                ak_ref[prev, :] += dk[:blk]
                av_ref[prev, :] += dv[:blk]
                ak_ref[rows, :] += dk[blk:]
                av_ref[rows, :] += dv[blk:]

            per = max(1, 4 // d)

            def step(i, _, d=d, per=per, one=one):
                for u in range(per):
                    for r in range(d):
                        one(i * per + u, r)
                return 0

            lax.fori_loop(0, T // (d * blk * per), step, 0)

        def emit(i, _):
            rs = pl.ds(pl.multiple_of(i * chunk, chunk), chunk)
            dq_ref[rs, :] = (aq_ref[rs, :] * scale).astype(BF16)
            dk_ref[rs, :] = ak_ref[rs, :].astype(BF16)
            dv_ref[rs, :] = av_ref[rs, :].astype(BF16)
            return 0

        lax.fori_loop(0, T // chunk, emit, 0)

    def col(unit):
        return lambda h: (0, unit * H + h)

    seq = (T, HEAD_DIM)
    own = pl.BlockSpec(seq, lambda h: (0, h))
    return pl.pallas_call(
        body, name=name, grid=(H,),
        in_specs=[pl.BlockSpec(seq, col(2)), pl.BlockSpec(seq, col(3)), pl.BlockSpec(seq, col(4)),
                  pl.BlockSpec(seq, col(1)), own, own, pl.BlockSpec((1, HEAD_DIM), lambda h: (0, h))],
        out_specs=[own] * 3,
        out_shape=[jax.ShapeDtypeStruct((T, B), BF16)] * 3,
        scratch_shapes=[pltpu.VMEM(seq, F32)] * 3,
        compiler_params=_params("parallel"),
    )(z, z, z, dcat, o, lse, _alibi_row(B, 1))


def _join_columns(parts, name):
    T = parts[0].shape[0]
    widths = [p.shape[1] for p in parts]

    def body(*refs):
        o_ref, at = refs[-1], 0
        for ref, w in zip(refs[:-1], widths):
            o_ref[:, pl.ds(at, w)] = ref[...]
            at += w

    return pl.pallas_call(
        body, name=name, grid=(T // ROWS,),
        in_specs=[pl.BlockSpec((ROWS, w), lambda i: (i, 0)) for w in widths],
        out_specs=pl.BlockSpec((ROWS, sum(widths)), lambda i: (i, 0)),
        out_shape=jax.ShapeDtypeStruct((T, sum(widths)), BF16), compiler_params=_params("parallel"),
    )(*parts)


SB_QUERY_ROWS = 1024
SB_KEYS = 2 * ATT_BLOCK


def _tri_and_ones(pred):
    rows = lax.broadcasted_iota(jnp.int32, (2 * ATT_BLOCK, 2 * ATT_BLOCK), 0) % ATT_BLOCK
    cols = lax.broadcasted_iota(jnp.int32, (2 * ATT_BLOCK, 2 * ATT_BLOCK), 1)
    return ((cols >= ATT_BLOCK) | pred(rows, cols)).astype(BF16)


def _running(x, tri):
    hi = x.astype(BF16)
    lo = (x - hi.astype(F32)).astype(BF16)
    return _dot(jnp.concatenate([hi, lo], axis=1), tri)


def _sb_mask(query_rows, s):
    rows = lax.broadcasted_iota(jnp.int32, (query_rows - s * SB_KEYS, SB_KEYS), 0)
    cols = lax.broadcasted_iota(jnp.int32, (query_rows - s * SB_KEYS, SB_KEYS), 1)
    return cols < rows


def _log_sigmoids(z):
    ls = jnp.minimum(z, 0.0) - jnp.log(1.0 + jnp.exp(-jnp.abs(z)))
    return ls, ls - z


def _sb_fwd(qkv, W, name):
    T = qkv.shape[0]
    H = W // HEAD_DIM
    blk = ATT_BLOCK
    qb = min(SB_QUERY_ROWS, T)
    per = qb // SB_KEYS

    def body(q_ref, k_ref, v_ref, o_ref, lt_ref, acc_ref):
        i = pl.program_id(1)
        tri = _tri_and_ones(lambda r, c: r > c)
        lt_ref[...] = jnp.zeros_like(lt_ref)
        acc_ref[...] = jnp.zeros_like(acc_ref)

        def tile(j, mask, r0=0):
            ks = pl.ds(pl.multiple_of(j * SB_KEYS, SB_KEYS), SB_KEYS)
            qs = pl.ds(r0, qb - r0)
            z = _dot(q_ref[qs, :], k_ref[ks, :], NT)
            ls, lm = _log_sigmoids(z)
            if mask is not None:
                lm = jnp.where(mask, lm, 0.0)
            later = lt_ref[qs, :]
            second = _running(lm[:, blk:], tri)
            first = _running(lm[:, :blk], tri)
            after_first = later + second[:, blk:]
            a = jnp.exp(ls + jnp.concatenate([first[:, :blk] + after_first, second[:, :blk] + later], axis=1))
            if mask is not None:
                a = jnp.where(mask, a, 0.0)
            acc_ref[qs, :] += _dot(a.astype(BF16), v_ref[ks, :])
            lt_ref[qs, :] = after_first + first[:, blk:]

        for s in reversed(range(per)):
            tile(i * per + s, _sb_mask(qb, s), s * SB_KEYS)

        def step(jj, _):
            for s in range(per):
                tile((i - jj) * per - 1 - s, None)
            return 0

        lax.fori_loop(0, i, step, 0)
        o_ref[...] = acc_ref[...].astype(BF16)

    qs = pl.BlockSpec((qb, HEAD_DIM), lambda h, i: (i, h))
    return pl.pallas_call(
        body, name=name, grid=(H, T // qb),
        in_specs=[qs, pl.BlockSpec((T, HEAD_DIM), lambda h, i: (0, H + h)),
                  pl.BlockSpec((T, HEAD_DIM), lambda h, i: (0, 2 * H + h))],
        out_specs=[qs, qs],
        out_shape=[jax.ShapeDtypeStruct((T, W), BF16), jax.ShapeDtypeStruct((T, W), F32)],
        scratch_shapes=[pltpu.VMEM((qb, HEAD_DIM), F32)],
        compiler_params=_params("parallel", "arbitrary"),
    )(qkv, qkv, qkv)


def _sb_bwd(qkv, do, ltot, W, name):
    T = qkv.shape[0]
    H = W // HEAD_DIM
    blk = ATT_BLOCK
    nkb = T // SB_KEYS
    qb = min(SB_QUERY_ROWS, T)
    per = qb // SB_KEYS

    def body(q_ref, k_ref, v_ref, do_ref, lt_ref, dq_ref, dkt_ref, dvt_ref, qt_ref, dot_ref, plm_ref, pg_ref):
        i = pl.program_id(1)

        @pl.when(i == 0)
        def _():
            dkt_ref[...] = jnp.zeros_like(dkt_ref)
            dvt_ref[...] = jnp.zeros_like(dvt_ref)

        qt_ref[...] = q_ref[...].astype(F32).T.astype(BF16)
        dot_ref[...] = do_ref[...].astype(F32).T.astype(BF16)
        upto = _tri_and_ones(lambda r, c: r <= c)
        before = _tri_and_ones(lambda r, c: r < c)
        plm_ref[...] = jnp.zeros_like(plm_ref)
        pg_ref[...] = jnp.zeros_like(pg_ref)
        dq_ref[...] = jnp.zeros_like(dq_ref)

        def tile(j, mask, r0=0):
            ks = pl.ds(pl.multiple_of(j * SB_KEYS, SB_KEYS), SB_KEYS)
            qs = pl.ds(r0, qb - r0)
            k = k_ref[ks, :]
            v = v_ref[ks, :]
            z = _dot(q_ref[qs, :], k, NT)
            ls, lm = _log_sigmoids(z)
            nsig = jnp.exp(lm)
            if mask is not None:
                lm = jnp.where(mask, lm, 0.0)
            earlier = plm_ref[qs, :]
            first = _running(lm[:, :blk], upto)
            second = _running(lm[:, blk:], upto)
            upto_first = earlier + first[:, blk:]
            seen = jnp.concatenate([first[:, :blk] + earlier, second[:, :blk] + upto_first], axis=1)
            ltot = lt_ref[qs, :]
            a = jnp.exp(ls + (jnp.concatenate([ltot, ltot], axis=1) - seen))
            if mask is not None:
                a = jnp.where(mask, a, 0.0)
            g = a * _dot(do_ref[qs, :], v, NT)
            g_earlier = pg_ref[qs, :]
            g_first = _running(g[:, :blk], before)
            g_second = _running(g[:, blk:], before)
            g_upto_first = g_earlier + g_first[:, blk:]
            gsum = jnp.concatenate([g_first[:, :blk] + g_earlier, g_second[:, :blk] + g_upto_first], axis=1)
            dz = g * nsig - gsum * jnp.exp(ls)
            if mask is not None:
                dz = jnp.where(mask, dz, 0.0)
            dzb = dz.astype(BF16)
            dkt_ref[j] += _dot(qt_ref[:, qs], dzb)
            dvt_ref[j] += _dot(dot_ref[:, qs], a.astype(BF16))
            dq_ref[qs, :] += _dot(dzb, k)
            plm_ref[qs, :] = upto_first + second[:, blk:]
            pg_ref[qs, :] = g_upto_first + g_second[:, blk:]

        def step(jj, _):
            for s in range(per):
                tile(jj * per + s, None)
            return 0

        lax.fori_loop(0, i, step, 0)
        for s in range(per):
            tile(i * per + s, _sb_mask(qb, s), s * SB_KEYS)

    qs = pl.BlockSpec((qb, HEAD_DIM), lambda h, i: (i, h))
    res = pl.BlockSpec((None, nkb, HEAD_DIM, SB_KEYS), lambda h, i: (h, 0, 0, 0))
    return pl.pallas_call(
        body, name=name, grid=(H, T // qb),
        in_specs=[qs, pl.BlockSpec((T, HEAD_DIM), lambda h, i: (0, H + h)),
                  pl.BlockSpec((T, HEAD_DIM), lambda h, i: (0, 2 * H + h)), qs, qs],
        out_specs=[qs, res, res],
        out_shape=[jax.ShapeDtypeStruct((T, W), F32)] + [jax.ShapeDtypeStruct((H, nkb, HEAD_DIM, SB_KEYS), F32)] * 2,
        scratch_shapes=[pltpu.VMEM((HEAD_DIM, qb), BF16), pltpu.VMEM((HEAD_DIM, qb), BF16),
                        pltpu.VMEM((qb, HEAD_DIM), F32), pltpu.VMEM((qb, HEAD_DIM), F32)],
        compiler_params=_params("parallel", "arbitrary"),
    )(qkv, qkv, qkv, do, ltot)


def _sb_pack(dq, dkt, dvt, name):
    T, W = dq.shape
    H = W // HEAD_DIM
    blk = SB_KEYS
    scale = HEAD_DIM ** -0.5

    def body(q_ref, kt_ref, vt_ref, o_ref):
        o_ref[:, pl.ds(0, W)] = (q_ref[...] * scale).astype(BF16)
        for h in range(H):
            o_ref[:, pl.ds(W + h * HEAD_DIM, HEAD_DIM)] = kt_ref[h].T.astype(BF16)
            o_ref[:, pl.ds(2 * W + h * HEAD_DIM, HEAD_DIM)] = vt_ref[h].T.astype(BF16)

    tr = pl.BlockSpec((H, None, HEAD_DIM, blk), lambda i: (0, i, 0, 0))
    return pl.pallas_call(
        body, name=name, grid=(T // blk,), in_specs=[pl.BlockSpec((blk, W), lambda i: (i, 0)), tr, tr],
        out_specs=pl.BlockSpec((blk, 3 * W), lambda i: (i, 0)),
        out_shape=jax.ShapeDtypeStruct((T, 3 * W), BF16), compiler_params=_params("parallel"),
    )(dq, dkt, dvt)


def _local_step(x, target, norms, sgu, comm):
    T, D = x.shape
    A = D // 2
    pre_mix, post_mix, pre_ffn, post_ffn = norms
    ln_g, ln_b, w_s, b_s = sgu
    b_t = b_s.T
    scale = HEAD_DIM ** -0.5

    def vec(p, layer):
        return comm.tie(p[layer:layer + 1])

    h0 = _prenorm(x, vec(pre_mix, 0), "prenorm0")
    comm.arrive("ab_in", after=h0)
    comm.land("ab_in")
    z = _matmul(h0, comm.weight("ab_in"), mode="nn", name="ab_in_fwd")
    comm.arrive("ab_out", after=z)
    cat = _sgu_fwd(z, ln_g, ln_b, w_s, b_t, "sgu_fwd")
    cat, o_dil, lse_dil = _dilated_forward(z, cat, A, "dilated_fwd")
    comm.land("ab_out", after=o_dil)
    y0 = _matmul(cat, comm.weight("ab_out"), mode="nn", name="ab_out_fwd")
    comm.arrive("ffn0", after=y0)
    x1, h1 = _postnorm_prenorm(x, y0, vec(post_mix, 0), vec(pre_ffn, 0), "norm_mix0")
    comm.land("ffn0", after=h1)

    def relu2(acc, j):
        r = jnp.maximum(acc, 0.0)
        return r * r, 2.0 * r

    f0, r0 = _matmul(h1, comm.weight("w1_0"), mode="nn", name="ffn0_w1_fwd", out_dtype=(BF16, BF16), epi=relu2)
    y1 = _matmul(f0, comm.weight("w2_0"), mode="nn", name="ffn0_w2_fwd")
    comm.arrive("sb", after=y1)
    x2, h2 = _postnorm_prenorm(x1, y1, vec(post_ffn, 0), vec(pre_mix, 1), "norm_ffn0")
    comm.land("sb", after=h2)

    tn_qkv = _tile(D, 1024)
    nq = D // tn_qkv

    def scale_q(acc, j):
        return jnp.where(j < nq, acc * scale, acc)

    qkv = _matmul(h2, comm.weight("sb_in"), mode="nn", name="sb_in_fwd", out_dtype=BF16, tn=tn_qkv, epi=scale_q)
    comm.arrive("ffn1", after=qkv)
    o_sb, ltot = _sb_fwd(qkv, D, "sb_fwd")
    comm.land("ffn1", after=o_sb)
    y2 = _matmul(o_sb, comm.weight("sb_out"), mode="nn", name="sb_out_fwd")
    x3, h3 = _postnorm_prenorm(x2, y2, vec(post_mix, 1), vec(pre_ffn, 1), "norm_mix1")
    f1, r1 = _matmul(h3, comm.weight("w1_1"), mode="nn", name="ffn1_w1_fwd", out_dtype=(BF16, BF16), epi=relu2)
    y3 = _matmul(f1, comm.weight("w2_1"), mode="nn", name="ffn1_w2_fwd")
    loss_tile, dx4, dy3, dg_post_ffn1 = _postnorm_loss(x3, y3, vec(post_ffn, 1), target, "norm_loss")
    loss = loss_tile[0, 0]

    def relu2_bwd(acc, j, r):
        return acc * r.astype(F32)

    def ffn_bwd(dy, h, f, r, layer):
        g_w2 = _matmul(f, dy, mode="tn", name=f"ffn{layer}_w2_wgrad", out_dtype=BF16)
        da = _matmul(dy, comm.weight(f"w2_{layer}"), mode="nt", name=f"ffn{layer}_w2_dgrad", out_dtype=BF16,
                     epi=relu2_bwd, extras=(r,))
        g_w1 = _matmul(h, da, mode="tn", name=f"ffn{layer}_w1_wgrad", out_dtype=BF16)
        comm.reduce(f"ffn{layer}", {f"w2_{layer}": g_w2, f"w1_{layer}": g_w1})
        return _matmul(da, comm.weight(f"w1_{layer}"), mode="nt", name=f"ffn{layer}_w1_dgrad", after=comm.started())

    dh3 = ffn_bwd(dy3, h3, f1, r1, 1)
    dx3, dg_pre_ffn1, dy2, dg_post_mix1 = _norm_bwd_pair(dx4, dh3, x3, vec(pre_ffn, 1), y2, vec(post_mix, 1),
                                                         "ffn1_sb_norm_bwd")
    g_sb_out = _matmul(o_sb, dy2, mode="tn", name="sb_out_wgrad", out_dtype=BF16)
    do_sb = _matmul(dy2, comm.weight("sb_out"), mode="nt", name="sb_out_dgrad", out_dtype=BF16)
    dq, dk, dv = _sb_bwd(qkv, do_sb, ltot, D, "sb_bwd")
    dqkv = _sb_pack(dq, dk, dv, "sb_pack")
    g_sb_in = _matmul(h2, dqkv, mode="tn", name="sb_in_wgrad", out_dtype=BF16)
    comm.reduce("sb", {"sb_out": g_sb_out, "sb_in": g_sb_in})
    dh2 = _matmul(dqkv, comm.weight("sb_in"), mode="nt", name="sb_in_dgrad", after=comm.started())
    dx2, dg_pre_mix1, dy1, dg_post_ffn0 = _norm_bwd_pair(dx3, dh2, x2, vec(pre_mix, 1), y1, vec(post_ffn, 0),
                                                         "sb_ffn0_norm_bwd")
    dh1 = ffn_bwd(dy1, h1, f0, r0, 0)
    dx1, dg_pre_ffn0, dy0, dg_post_mix0 = _norm_bwd_pair(dx2, dh1, x1, vec(pre_ffn, 0), y0, vec(post_mix, 0),
                                                         "ffn0_ab_norm_bwd")
    g_ab_out = _matmul(cat, dy0, mode="tn", name="ab_out_wgrad", out_dtype=BF16)
    comm.reduce("ab_out", {"ab_out": g_ab_out})
    dcat = _matmul(dy0, comm.weight("ab_out"), mode="nt", name="ab_out_dgrad", after=comm.started())
    duv, d_ln_g, d_ln_b, d_w_s, d_b_t = _sgu_bwd(z, dcat, ln_g, ln_b, w_s, b_t, "sgu_bwd")
    dz = _join_columns([duv, *_dilated_backward(z, dcat, o_dil, lse_dil, A, "dilated_bwd")], "join_dz")
    g_ab_in = _matmul(h0, dz, mode="tn", name="ab_in_wgrad", out_dtype=BF16)
    comm.reduce("ab_in", {"ab_in": g_ab_in})
    dh0 = _matmul(dz, comm.weight("ab_in"), mode="nt", name="ab_in_dgrad", after=comm.started())
    dx0, dg_pre_mix0 = _prenorm_bwd(dx1, dh0, x, vec(pre_mix, 0), "ab_prenorm_bwd")

    small = {
        "pre_mix": jnp.concatenate([dg_pre_mix0, dg_pre_mix1], axis=0),
        "post_mix": jnp.concatenate([dg_post_mix0, dg_post_mix1], axis=0),
        "pre_ffn": jnp.concatenate([dg_pre_ffn0, dg_pre_ffn1], axis=0),
        "post_ffn": jnp.concatenate([dg_post_ffn0, dg_post_ffn1], axis=0),
        "ln_g": d_ln_g, "ln_b": d_ln_b, "w_s": d_w_s, "b_s": d_b_t.T,
    }
    return loss, dx0, small


MESH_ID = pl.DeviceIdType.MESH
ANY = pl.BlockSpec(memory_space=pl.ANY)


def _coords():
    return lax.axis_index("x"), lax.axis_index("y"), lax.axis_index("c")


def _shard_of(ref, kind, p):
    if kind == "col":
        n = ref.shape[1] // N_DEV
        return ref.at[:, pl.ds(pl.multiple_of(p * n, 128), n)]
    r = ref.shape[0] // N_DEV
    return ref.at[pl.ds(pl.multiple_of(p * r, 16), r), :]


def _full_shape(shard, kind):
    if kind == "col":
        return (shard.shape[0], shard.shape[1] * N_DEV)
    return (shard.shape[0] * N_DEV, shard.shape[1])


def _place(shards, layer, kind, block, after, name):
    _, rows, cols = shards.shape
    tr = _tile(rows, 512)

    def body(b_ref, s_ref, after_ref, o_ref):
        o_ref[...] = s_ref[...].astype(BF16)

    if kind == "col":
        out = pl.BlockSpec((tr, cols), lambda i, b_ref: (i, b_ref[0]))
    else:
        out = pl.BlockSpec((tr, cols), lambda i, b_ref: (b_ref[0] * (rows // tr) + i, 0))
    return pl.pallas_call(
        body, name=name,
        grid_spec=pltpu.PrefetchScalarGridSpec(
            num_scalar_prefetch=1, grid=(rows // tr,),
            in_specs=[pl.BlockSpec((None, tr, cols), lambda i, b_ref: (layer, i, 0)), ANY], out_specs=out),
        out_shape=jax.ShapeDtypeStruct(_full_shape(shards[0], kind), BF16),
        compiler_params=_params("parallel"),
    )(block, shards, after)


HBM = pl.BlockSpec(memory_space=pltpu.HBM)
SEM = pl.BlockSpec(memory_space=pltpu.SEMAPHORE)
FLOWS = pltpu.SideEffectType.DATAFLOW_SIDE_EFFECTING


def _in_hbm(a):
    return pltpu.with_memory_space_constraint(a, pltpu.HBM)


def _hbm_like(bufs):
    return [pltpu.HBM(b.shape, b.dtype) for b in bufs]


def _copies_start(name, bufs, plan, n, after):
    nb = len(bufs)

    def body(*refs):
        send_sems, recv_sems, token = refs[nb + 1], refs[nb + 2], refs[-1]
        for cp in plan(refs[:nb], send_sems, recv_sems):
            cp.start()
        token[...] = jnp.zeros_like(token)

    out = pl.pallas_call(
        body, name=name, in_specs=[HBM] * nb + [ANY],
        out_specs=[SEM, SEM] + [HBM] * nb + [pl.BlockSpec(memory_space=pltpu.VMEM)],
        out_shape=[pltpu.SemaphoreType.DMA((n,)), pltpu.SemaphoreType.DMA((n,))] + _hbm_like(bufs)
        + [jax.ShapeDtypeStruct((8, 128), F32)],
        input_output_aliases={i: 2 + i for i in range(nb)},
        compiler_params=pltpu.CompilerParams(has_side_effects=FLOWS),
    )(*[_in_hbm(b) for b in bufs], after)
    return (out[0], out[1]), list(out[2:2 + nb]), out[-1]


def _copies_wait(name, bufs, sems, after, plan):
    nb = len(bufs)

    def body(*refs):
        for cp in plan(refs[:nb], refs[nb], refs[nb + 1]):
            cp.wait_send()
            cp.wait_recv()

    out = pl.pallas_call(
        body, name=name, in_specs=[HBM] * nb + [SEM, SEM, ANY], out_specs=[HBM] * nb,
        out_shape=_hbm_like(bufs), input_output_aliases={i: i for i in range(nb)},
        compiler_params=pltpu.CompilerParams(has_side_effects=FLOWS),
    )(*bufs, *sems, after)
    return list(out)


def _copies_wait_start(name, bufs, sems, after, plan, next_plan, n_next):
    nb = len(bufs)

    def body(*refs):
        ins = refs[:nb]
        for cp in plan(ins, refs[nb], refs[nb + 1]):
            cp.wait_send()
            cp.wait_recv()
        send_sems, recv_sems, token = refs[nb + 3], refs[nb + 4], refs[-1]
        for cp in next_plan(ins, send_sems, recv_sems):
            cp.start()
        token[...] = jnp.zeros_like(token)

    out = pl.pallas_call(
        body, name=name, in_specs=[HBM] * nb + [SEM, SEM, ANY],
        out_specs=[SEM, SEM] + [HBM] * nb + [pl.BlockSpec(memory_space=pltpu.VMEM)],
        out_shape=[pltpu.SemaphoreType.DMA((n_next,)), pltpu.SemaphoreType.DMA((n_next,))] + _hbm_like(bufs)
        + [jax.ShapeDtypeStruct((8, 128), F32)],
        input_output_aliases={i: 2 + i for i in range(nb)},
        compiler_params=pltpu.CompilerParams(has_side_effects=FLOWS),
    )(*bufs, *sems, after)
    return (out[0], out[1]), list(out[2:2 + nb]), out[-1]


def _gather_plans(kinds):
    nt = len(kinds)

    def slot(refs, t, px, py, pc):
        return _shard_of(refs[t], kinds[t], 4 * px + 2 * py + pc)

    def to_chips(refs, send_sems, recv_sems):
        x, y, c = _coords()
        peers = [(x, y, 1 - c), (1 - x, y, c), (x, 1 - y, c), (1 - x, 1 - y, c)]
        return [pltpu.make_async_remote_copy(
            src_ref=slot(refs, t, x, y, c), dst_ref=slot(refs, t, x, y, c), send_sem=send_sems.at[4 * t + k],
            recv_sem=recv_sems.at[4 * t + k], device_id=peer, device_id_type=MESH_ID)
            for t in range(nt) for k, peer in enumerate(peers)]

    def to_sibling(refs, send_sems, recv_sems):
        x, y, c = _coords()
        chips = [(1 - x, y), (x, 1 - y), (1 - x, 1 - y)]
        return [pltpu.make_async_remote_copy(
            src_ref=slot(refs, t, *chip, c), dst_ref=slot(refs, t, *chip, c), send_sem=send_sems.at[3 * t + j],
            recv_sem=recv_sems.at[3 * t + j], device_id=(x, y, 1 - c), device_id_type=MESH_ID)
            for t in range(nt) for j, chip in enumerate(chips)]

    return to_chips, to_sibling


def _shard_shape(full, kind):
    if kind == "col":
        return (full.shape[0], full.shape[1] // N_DEV)
    return (full.shape[0] // N_DEV, full.shape[1])


def _scatter_plan(kinds):
    nt = len(kinds)

    def plan(refs, send_sems, recv_sems):
        x, y, c = _coords()
        copies = []
        for t in range(nt):
            for k in range(1, N_DEV):
                px = 1 - x if (k >> 2) & 1 else x
                py = 1 - y if (k >> 1) & 1 else y
                pc = 1 - c if k & 1 else c
                copies.append(pltpu.make_async_remote_copy(
                    src_ref=_shard_of(refs[t], kinds[t], 4 * px + 2 * py + pc),
                    dst_ref=refs[nt + t].at[4 * x + 2 * y + c],
                    send_sem=send_sems.at[7 * t + k - 1], recv_sem=recv_sems.at[7 * t + k - 1],
                    device_id=(px, py, pc), device_id_type=MESH_ID))
        return copies
    return plan


def _partial_specs(full, kind, tr):
    rows, cols = _shard_shape(full, kind)
    steps = rows // tr
    if kind == "col":
        own = pl.BlockSpec((tr, cols), lambda i, w: (i, w[0]))
    else:
        own = pl.BlockSpec((tr, cols), lambda i, w: (w[0] * steps + i, 0))
    return [own] + [pl.BlockSpec((None, tr, cols), lambda i, w, k=k: (w[k], i, 0)) for k in range(1, N_DEV)]


def _all_reduce_small(vec, after):
    R = vec.shape[0]

    def body(v_ref, after_ref, o_ref, recv_ref, send_sems, recv_sems):
        x, y, c = _coords()
        me = 4 * x + 2 * y + c
        recv_ref[me] = v_ref[...]
        copies = []
        for k in range(1, N_DEV):
            bx, by, bc = (k >> 2) & 1, (k >> 1) & 1, k & 1
            peer = (1 - x if bx else x, 1 - y if by else y, 1 - c if bc else c)
            copies.append(pltpu.make_async_remote_copy(
                src_ref=v_ref, dst_ref=recv_ref.at[me],
                send_sem=send_sems.at[k - 1], recv_sem=recv_sems.at[k - 1],
                device_id=peer, device_id_type=MESH_ID))
        for cp in copies:
            cp.start()
        for cp in copies:
            cp.wait()
        total = recv_ref[0]
        for p in range(1, N_DEV):
            total = total + recv_ref[p]
        o_ref[...] = total

    return pl.pallas_call(
        body, name="all_reduce_small",
        in_specs=[pl.BlockSpec(memory_space=pltpu.VMEM), ANY], out_specs=pl.BlockSpec(memory_space=pltpu.VMEM),
        out_shape=jax.ShapeDtypeStruct((R, 128), F32),
        scratch_shapes=[pltpu.VMEM((N_DEV, R, 128), F32), pltpu.SemaphoreType.DMA((N_DEV - 1,)),
                        pltpu.SemaphoreType.DMA((N_DEV - 1,))],
        compiler_params=pltpu.CompilerParams(vmem_limit_bytes=VMEM_LIMIT),
    )(vec, after)


def _adamw_math(w, g, m, v):
    m = ADAM_B1 * m + (1.0 - ADAM_B1) * g
    v = ADAM_B2 * v + (1.0 - ADAM_B2) * (g * g)
    m_hat = m / (1.0 - ADAM_B1 ** ADAM_STEP)
    v_hat = v / (1.0 - ADAM_B2 ** ADAM_STEP)
    delta = -ADAM_LR * (m_hat / (jnp.sqrt(v_hat) + ADAM_EPS) + ADAM_WD * w)
    return delta, m, v


def _adamw(w, grads, kind, where, m, v, name):
    layers, rows, cols = w.shape
    tr = _tile(rows, 128)
    out = None
    for layer, (grad, land) in enumerate(grads):
        def body(w_ref, *refs):
            parts, (x_ref, m_ref, v_ref) = refs[:N_DEV], refs[N_DEV:N_DEV + 3]
            g_ref, d_ref, mo_ref, vo_ref = refs[-4:]
            g = parts[0][...].astype(F32)
            for p_ref in parts[1:]:
                g = g + p_ref[...].astype(F32)
            g_ref[...] = g
            d_ref[...], mo_ref[...], vo_ref[...] = _adamw_math(x_ref[...], g, m_ref[...], v_ref[...])

        blk = pl.BlockSpec((None, tr, cols), lambda i, w_, layer=layer: (layer, i, 0))
        earlier = [] if out is None else list(out)
        out = pl.pallas_call(
            body, name=f"{name}_{layer}",
            grid_spec=pltpu.PrefetchScalarGridSpec(
                num_scalar_prefetch=1, grid=(rows // tr,),
                in_specs=_partial_specs(grad, kind, tr) + [blk] * 3 + [ANY] * len(earlier),
                out_specs=[blk] * 4),
            out_shape=[jax.ShapeDtypeStruct((layers, rows, cols), F32)] * 4,
            input_output_aliases={N_DEV + 4 + k: k for k in range(len(earlier))},
            compiler_params=_params("parallel"),
        )(where, grad, *[land] * (N_DEV - 1), w, m, v, *earlier)
    return out


def _adamw_small(w, g, m, v):
    def body(w_ref, g_ref, m_ref, v_ref, d_ref, mo_ref, vo_ref):
        d_ref[...], mo_ref[...], vo_ref[...] = _adamw_math(w_ref[...], g_ref[...], m_ref[...], v_ref[...])

    whole = pl.BlockSpec(memory_space=pltpu.VMEM)
    return pl.pallas_call(
        body, name="adamw_small", in_specs=[whole] * 4, out_specs=[whole] * 3,
        out_shape=[jax.ShapeDtypeStruct(w.shape, F32)] * 3,
        compiler_params=pltpu.CompilerParams(vmem_limit_bytes=VMEM_LIMIT),
    )(w, g, m, v)


def _pack(arrays):
    rows = []
    for a in arrays:
        flat = a.reshape(-1)
        pad = (-flat.shape[0]) % 1024
        rows.append(jnp.pad(flat, (0, pad)).reshape(-1, 128))
    return jnp.concatenate(rows, axis=0)


def _unpack(packed, like):
    out, r = [], 0
    for a in like:
        n = math.prod(a.shape)
        nr = (n + 1023) // 1024 * 8
        out.append(packed[r:r + nr].reshape(-1)[:n].reshape(a.shape))
        r += nr
    return out


KIND = {"ab_in": "col", "ab_out": "row", "sb_in": "col", "sb_out": "row",
        "w1_0": "col", "w1_1": "col", "w2_0": "row", "w2_1": "row"}
GATHERS = {"ab_in": ("ab_in",), "ab_out": ("ab_out",), "ffn0": ("w1_0", "w2_0"),
           "sb": ("sb_in", "sb_out"), "ffn1": ("w1_1", "w2_1")}


class _Exchange:
    def __init__(self, shards):
        x, y, c = _coords()
        me = (4 * x + 2 * y + c).astype(jnp.int32)
        self.where = jnp.stack([jnp.bitwise_xor(me, k) for k in range(N_DEV)])
        self.full = {}
        self.tokens = []
        self.gathers = {}
        self.scatters = {}
        self.settled = {}
        block = me.reshape(1)
        after = block
        for key, group in GATHERS.items():
            for n in group:
                self.full[n] = _place(*shards[n], KIND[n], block, after, f"place_{n}")
            to_chips, to_sibling = _gather_plans([KIND[n] for n in group])
            bufs = [self.full[n] for n in group]
            sems, bufs, after = _copies_start(f"gather_start_{key}", bufs, to_chips, 4 * len(group), after)
            self.tokens.append(after)
            self.gathers[key] = (group, sems, bufs, to_chips, to_sibling)

    def tie(self, small):
        for token in self.tokens:
            small = small + token[0:1, 0:1]
        self.tokens = []
        return small

    def started(self):
        return tuple(self.tokens)

    def weight(self, name):
        return self.full[name]

    def arrive(self, key, after):
        group, sems, bufs, to_chips, to_sibling = self.gathers[key]
        sems, bufs, token = _copies_wait_start(f"gather_pass_{key}", bufs, sems, after, to_chips, to_sibling,
                                               3 * len(group))
        self.tokens.append(token)
        self.gathers[key] = (group, sems, bufs, token, to_sibling)

    def land(self, key, after=None):
        group, sems, bufs, token, to_sibling = self.gathers.pop(key)
        after = token if after is None else after
        self.full.update(zip(group, _copies_wait(f"gather_done_{key}", bufs, sems, after, to_sibling)))

    def reduce(self, key, grads):
        names = list(grads)
        kinds = [KIND[n] for n in names]
        full = [grads[n] for n in names]
        lands = [lax.empty((N_DEV,) + _shard_shape(g, k), BF16) for g, k in zip(full, kinds)]
        plan = _scatter_plan(kinds)
        sems, bufs, token = _copies_start(f"scatter_start_{key}", full + lands, plan, (N_DEV - 1) * len(names),
                                          full[-1])
        self.tokens.append(token)
        self.scatters[key] = (names, sems, bufs, plan)

    def settle(self, keys, after):
        for key in keys:
            names, sems, bufs, plan = self.scatters.pop(key)
            bufs = _copies_wait(f"scatter_done_{key}", bufs, sems, after, plan)
            self.settled.update({n: t for n, *t in zip(names, bufs[:len(names)], bufs[len(names):])})
        return self.settled


SMALL = ("norm_pre_mix", "norm_post_mix", "norm_pre_ffn", "norm_post_ffn", "sgu_ln_g", "sgu_ln_b", "sgu_w", "sgu_b")
ORDER = ("norm_pre_mix", "norm_post_mix", "norm_pre_ffn", "norm_post_ffn", "ab_w_in", "sgu_ln_g", "sgu_ln_b", "sgu_w",
         "sgu_b", "ab_w_out", "sb_w_in", "sb_w_out", "ffn_w1", "ffn_w2")


def kernel(x, norm_pre_mix, norm_post_mix, norm_pre_ffn, norm_post_ffn, ab_w_in, sgu_ln_g, sgu_ln_b, sgu_w, sgu_b, ab_w_out, sb_w_in, sb_w_out, ffn_w1, ffn_w2, loss_target, m_norm_pre_mix, m_norm_post_mix, m_norm_pre_ffn, m_norm_post_ffn, m_ab_w_in, m_sgu_ln_g, m_sgu_ln_b, m_sgu_w, m_sgu_b, m_ab_w_out, m_sb_w_in, m_sb_w_out, m_ffn_w1, m_ffn_w2, v_norm_pre_mix, v_norm_post_mix, v_norm_pre_ffn, v_norm_post_ffn, v_ab_w_in, v_sgu_ln_g, v_sgu_ln_b, v_sgu_w, v_sgu_b, v_ab_w_out, v_sb_w_in, v_sb_w_out, v_ffn_w1, v_ffn_w2):
    W = dict(norm_pre_mix=norm_pre_mix, norm_post_mix=norm_post_mix, norm_pre_ffn=norm_pre_ffn,
             norm_post_ffn=norm_post_ffn, ab_w_in=ab_w_in, sgu_ln_g=sgu_ln_g, sgu_ln_b=sgu_ln_b, sgu_w=sgu_w,
             sgu_b=sgu_b, ab_w_out=ab_w_out, sb_w_in=sb_w_in, sb_w_out=sb_w_out, ffn_w1=ffn_w1, ffn_w2=ffn_w2)
    M = dict(norm_pre_mix=m_norm_pre_mix, norm_post_mix=m_norm_post_mix, norm_pre_ffn=m_norm_pre_ffn,
             norm_post_ffn=m_norm_post_ffn, ab_w_in=m_ab_w_in, sgu_ln_g=m_sgu_ln_g, sgu_ln_b=m_sgu_ln_b,
             sgu_w=m_sgu_w, sgu_b=m_sgu_b, ab_w_out=m_ab_w_out, sb_w_in=m_sb_w_in, sb_w_out=m_sb_w_out,
             ffn_w1=m_ffn_w1, ffn_w2=m_ffn_w2)
    V = dict(norm_pre_mix=v_norm_pre_mix, norm_post_mix=v_norm_post_mix, norm_pre_ffn=v_norm_pre_ffn,
             norm_post_ffn=v_norm_post_ffn, ab_w_in=v_ab_w_in, sgu_ln_g=v_sgu_ln_g, sgu_ln_b=v_sgu_ln_b,
             sgu_w=v_sgu_w, sgu_b=v_sgu_b, ab_w_out=v_ab_w_out, sb_w_in=v_sb_w_in, sb_w_out=v_sb_w_out,
             ffn_w1=v_ffn_w1, ffn_w2=v_ffn_w2)

    shards = {"ab_in": (ab_w_in, 0), "ab_out": (ab_w_out, 0), "w1_0": (ffn_w1, 0), "w2_0": (ffn_w2, 0),
              "sb_in": (sb_w_in, 0), "sb_out": (sb_w_out, 0), "w1_1": (ffn_w1, 1), "w2_1": (ffn_w2, 1)}
    comm = _Exchange(shards)
    norms = (norm_pre_mix, norm_post_mix, norm_pre_ffn, norm_post_ffn)
    sgu = (sgu_ln_g, sgu_ln_b, sgu_w[0], sgu_b[0])
    loss, dx, small = _local_step(x[0], loss_target[0], norms, sgu, comm)

    out = {}

    def update(name, layers, landed):
        out[name] = _adamw(W[name], [landed[n] for n in layers], KIND[layers[0]], comm.where, M[name], V[name],
                           f"adamw_{name}")

    landed = comm.settle(("ffn1", "sb", "ffn0"), after=dx)
    for name, layers in (("sb_w_in", ["sb_in"]), ("sb_w_out", ["sb_out"]), ("ffn_w1", ["w1_0", "w1_1"]),
                         ("ffn_w2", ["w2_0", "w2_1"])):
        update(name, layers, landed)
    small_g = [small["pre_mix"], small["post_mix"], small["pre_ffn"], small["post_ffn"], small["ln_g"],
               small["ln_b"], small["w_s"][None], small["b_s"][None]]
    summed = _all_reduce_small(_pack(small_g + [loss.reshape(1)]), out["ffn_w2"][3])
    g_small, loss = summed[:-8], summed[-8, 0]
    landed = comm.settle(("ab_out", "ab_in"), after=g_small)
    update("ab_w_out", ["ab_out"], landed)
    update("ab_w_in", ["ab_in"], landed)
    res = _adamw_small(_pack([W[n] for n in SMALL]), g_small, _pack([M[n] for n in SMALL]),
                       _pack([V[n] for n in SMALL]))
    like = [W[n] for n in SMALL]
    for n, *vals in zip(SMALL, *[_unpack(r, like) for r in [g_small] + list(res)]):
        out[n] = vals

    return (loss, dx[None], *[out[n][0] for n in ORDER], *[out[n][1] for n in ORDER],
            *[out[n][2] for n in ORDER], *[out[n][3] for n in ORDER])
```

```python
import math

import jax
import jax.numpy as jnp
from jax import lax
from jax.experimental import pallas as pl
from jax.experimental.pallas import tpu as pltpu

F32 = jnp.float32
BF16 = jnp.bfloat16

HEAD_DIM = 128
CHUNK = 128
ATT_BLOCK = 128
DILATED_PAIRS = ((128, 1), (512, 4), (2048, 16))
RMS_EPS = 1e-6
LN_EPS = 1e-5
ADAM_LR = 0.001
ADAM_B1 = 0.9
ADAM_B2 = 0.999
ADAM_EPS = 1e-08
ADAM_WD = 0.01
ADAM_STEP = 10
N_DEV = 8
MASKED = -1e30

V7X_VMEM_BYTES = 64 * 1024 * 1024
VMEM_LIMIT = V7X_VMEM_BYTES - 8 * 1024 * 1024

NN = (((1,), (0,)), ((), ()))
NT = (((1,), (1,)), ((), ()))
TN = (((0,), (0,)), ((), ()))


def _params(*sem):
    return pltpu.CompilerParams(dimension_semantics=sem, vmem_limit_bytes=VMEM_LIMIT)


def _dot(a, b, dims=NN):
    return lax.dot_general(a, b, dims, preferred_element_type=F32)


def _tile(n, preferred):
    if n <= preferred:
        return n
    t = preferred - preferred % 128
    while n % t:
        t -= 128
    assert t > 0, (n, preferred)
    return t


def _matmul(a, b, *, mode, name, out_dtype=F32, tm=1024, tn=1024, tk=2048, epi=None, extras=(), after=()):
    if mode == "nn":
        (M, K), N = a.shape, b.shape[1]
    elif mode == "nt":
        (M, K), N = a.shape, b.shape[0]
    else:
        (K, M), N = a.shape, b.shape[1]
    tm, tn, tk = _tile(M, tm), _tile(N, tn), _tile(K, tk)
    nk = K // tk
    if mode == "tn":
        a_spec = pl.BlockSpec((tk, tm), lambda i, j, k: (k, i))
    else:
        a_spec = pl.BlockSpec((tm, tk), lambda i, j, k: (i, k))
    if mode == "nt":
        b_spec = pl.BlockSpec((tn, tk), lambda i, j, k: (j, k))
    else:
        b_spec = pl.BlockSpec((tk, tn), lambda i, j, k: (k, j))
    o_spec = pl.BlockSpec((tm, tn), lambda i, j, k: (i, j))
    dims = {"nn": NN, "nt": NT, "tn": TN}[mode]
    n_extra = len(extras)
    n_in = n_extra + len(after)
    several = isinstance(out_dtype, tuple)
    n_out = len(out_dtype) if several else 1

    def finish(acc, rest):
        outs = acc if epi is None else epi(acc, pl.program_id(1), *[r[...] for r in rest[:n_extra]])
        for o_ref, o in zip(rest[n_in:n_in + n_out], outs if several else (outs,)):
            o_ref[...] = o.astype(o_ref.dtype)

    if nk == 1:
        def body(a_ref, b_ref, *rest):
            finish(_dot(a_ref[...], b_ref[...], dims), rest)
        scratch = []
    else:
        def body(a_ref, b_ref, *rest):
            acc_ref = rest[n_in + n_out]
            k = pl.program_id(2)

            @pl.when(k == 0)
            def _():
                acc_ref[...] = jnp.zeros_like(acc_ref)

            acc_ref[...] += _dot(a_ref[...], b_ref[...], dims)

            @pl.when(k == nk - 1)
            def _():
                finish(acc_ref[...], rest)
        scratch = [pltpu.VMEM((tm, tn), F32)]

    shapes = [jax.ShapeDtypeStruct((M, N), d) for d in (out_dtype if several else (out_dtype,))]
    return pl.pallas_call(
        body,
        name=name,
        grid=(M // tm, N // tn, nk),
        in_specs=[a_spec, b_spec] + [o_spec] * n_extra + [ANY] * len(after),
        out_specs=[o_spec] * n_out if several else o_spec,
        out_shape=shapes if several else shapes[0],
        scratch_shapes=scratch,
        compiler_params=_params("parallel", "parallel", "arbitrary"),
    )(a, b, *extras, *after)


ROWS = 512


def _rms(x):
    return lax.rsqrt(jnp.mean(x * x, axis=-1, keepdims=True) + RMS_EPS)


def _prenorm(x, g, name):
    T, D = x.shape

    def body(x_ref, g_ref, h_ref):
        xv = x_ref[...]
        h_ref[...] = (xv * _rms(xv) * g_ref[...]).astype(BF16)

    row = pl.BlockSpec((ROWS, D), lambda i: (i, 0))
    vec = pl.BlockSpec((1, D), lambda i: (0, 0))
    return pl.pallas_call(
        body, name=name, grid=(T // ROWS,), in_specs=[row, vec], out_specs=row,
        out_shape=jax.ShapeDtypeStruct((T, D), BF16), compiler_params=_params("parallel"),
    )(x, g)


def _postnorm_prenorm(x, y, g_post, g_pre, name):
    T, D = x.shape

    def body(x_ref, y_ref, gp_ref, gn_ref, xo_ref, h_ref):
        yv = y_ref[...]
        xn = x_ref[...] + yv * _rms(yv) * gp_ref[...]
        xo_ref[...] = xn
        h_ref[...] = (xn * _rms(xn) * gn_ref[...]).astype(BF16)

    row = pl.BlockSpec((ROWS, D), lambda i: (i, 0))
    vec = pl.BlockSpec((1, D), lambda i: (0, 0))
    return pl.pallas_call(
        body, name=name, grid=(T // ROWS,), in_specs=[row, row, vec, vec], out_specs=[row, row],
        out_shape=[jax.ShapeDtypeStruct((T, D), F32), jax.ShapeDtypeStruct((T, D), BF16)],
        compiler_params=_params("parallel"),
    )(x, y, g_post, g_pre)


def _postnorm_grads(dn, yh, r, g):
    gd = dn * g
    return r * (gd - yh * jnp.mean(yh * gd, axis=-1, keepdims=True)), dn * yh


def _postnorm_loss(x, y, g_post, target, name):
    T, D = x.shape

    def body(x_ref, y_ref, gp_ref, t_ref, loss_ref, dx_ref, dy_ref, dg_ref):
        @pl.when(pl.program_id(0) == 0)
        def _():
            loss_ref[...] = jnp.zeros_like(loss_ref)
            dg_ref[...] = jnp.zeros_like(dg_ref)

        yv = y_ref[...]
        r = _rms(yv)
        yh = yv * r
        err = x_ref[...] + yh * gp_ref[...] - t_ref[...]
        dx = err * (1.0 / D)
        dx_ref[...] = dx
        loss_ref[...] += 0.5 * jnp.sum(jnp.sum(err * err, axis=-1, keepdims=True) * (1.0 / D))
        dy, dg = _postnorm_grads(dx, yh, r, gp_ref[...])
        dy_ref[...] = dy.astype(BF16)
        dg_ref[...] += jnp.sum(dg, axis=0, keepdims=True)

    row = pl.BlockSpec((ROWS, D), lambda i: (i, 0))
    vec = pl.BlockSpec((1, D), lambda i: (0, 0))
    acc = pl.BlockSpec((8, 128), lambda i: (0, 0))
    return pl.pallas_call(
        body, name=name, grid=(T // ROWS,), in_specs=[row, row, vec, row], out_specs=[acc, row, row, vec],
        out_shape=[jax.ShapeDtypeStruct((8, 128), F32), jax.ShapeDtypeStruct((T, D), F32),
                   jax.ShapeDtypeStruct((T, D), BF16), jax.ShapeDtypeStruct((1, D), F32)],
        compiler_params=_params("arbitrary"),
    )(x, y, g_post, target)


def _norm_bwd_pair(dx_out, dh, x, g_pre, y_prev, g_post_prev, name):
    T, D = x.shape

    def body(dxo_ref, dh_ref, x_ref, g_ref, y_ref, gp_ref, dx_ref, dg_ref, dy_ref, dgp_ref):
        @pl.when(pl.program_id(0) == 0)
        def _():
            dg_ref[...] = jnp.zeros_like(dg_ref)
            dgp_ref[...] = jnp.zeros_like(dgp_ref)

        xv, dhv = x_ref[...], dh_ref[...]
        r = _rms(xv)
        xh = xv * r
        gd = dhv * g_ref[...]
        dx = dxo_ref[...] + r * (gd - xh * jnp.mean(xh * gd, axis=-1, keepdims=True))
        dx_ref[...] = dx
        dg_ref[...] += jnp.sum(dhv * xh, axis=0, keepdims=True)
        yv = y_ref[...]
        ry = _rms(yv)
        dy, dgp = _postnorm_grads(dx, yv * ry, ry, gp_ref[...])
        dy_ref[...] = dy.astype(BF16)
        dgp_ref[...] += jnp.sum(dgp, axis=0, keepdims=True)

    row = pl.BlockSpec((ROWS, D), lambda i: (i, 0))
    vec = pl.BlockSpec((1, D), lambda i: (0, 0))
    return pl.pallas_call(
        body, name=name, grid=(T // ROWS,), in_specs=[row, row, row, vec, row, vec],
        out_specs=[row, vec, row, vec],
        out_shape=[jax.ShapeDtypeStruct((T, D), F32), jax.ShapeDtypeStruct((1, D), F32),
                   jax.ShapeDtypeStruct((T, D), BF16), jax.ShapeDtypeStruct((1, D), F32)],
        compiler_params=_params("arbitrary"),
    )(dx_out, dh, x, g_pre, y_prev, g_post_prev)


def _prenorm_bwd(dx_out, dh, x, g_pre, name):
    T, D = x.shape

    def body(dxo_ref, dh_ref, x_ref, g_ref, dx_ref, dg_ref):
        @pl.when(pl.program_id(0) == 0)
        def _():
            dg_ref[...] = jnp.zeros_like(dg_ref)

        xv, dhv = x_ref[...], dh_ref[...]
        r = _rms(xv)
        xh = xv * r
        gd = dhv * g_ref[...]
        dx_ref[...] = dxo_ref[...] + r * (gd - xh * jnp.mean(xh * gd, axis=-1, keepdims=True))
        dg_ref[...] += jnp.sum(dhv * xh, axis=0, keepdims=True)

    row = pl.BlockSpec((ROWS, D), lambda i: (i, 0))
    vec = pl.BlockSpec((1, D), lambda i: (0, 0))
    return pl.pallas_call(
        body, name=name, grid=(T // ROWS,), in_specs=[row, row, row, vec], out_specs=[row, vec],
        out_shape=[jax.ShapeDtypeStruct((T, D), F32), jax.ShapeDtypeStruct((1, D), F32)],
        compiler_params=_params("arbitrary"),
    )(dx_out, dh, x, g_pre)


_INV_SQRT2 = 1.0 / math.sqrt(2.0)
_INV_SQRT2PI = 1.0 / math.sqrt(2.0 * math.pi)


def _gelu(x):
    return 0.5 * x * (1.0 + lax.erf(x * _INV_SQRT2))


def _gelu_grad(x):
    return 0.5 * (1.0 + lax.erf(x * _INV_SQRT2)) + x * jnp.exp(-0.5 * x * x) * _INV_SQRT2PI


def _layernorm_stats(x):
    mu = jnp.mean(x, axis=-1, keepdims=True)
    xc = x - mu
    rstd = lax.rsqrt(jnp.mean(xc * xc, axis=-1, keepdims=True) + LN_EPS)
    return xc * rstd, rstd


def _tril_mask():
    i = lax.broadcasted_iota(jnp.int32, (CHUNK, CHUNK), 0)
    j = lax.broadcasted_iota(jnp.int32, (CHUNK, CHUNK), 1)
    return j <= i


SGU_ROWS = 512


def _sgu_fwd(z, ln_g, ln_b, w_s, b_t, name):
    T = z.shape[0]
    A = ln_g.shape[1]
    G = A // 128
    rows = min(SGU_ROWS, T)

    def body(u_ref, v_ref, g_ref, b_ref, w_ref, bt_ref, o_ref):
        mask = _tril_mask()
        for c in range(rows // CHUNK):
            rs = pl.ds(c * CHUNK, CHUNK)
            xh, _ = _layernorm_stats(_gelu(v_ref[rs, :]))
            vn = (xh * g_ref[...] + b_ref[...]).astype(BF16)
            for g in range(G):
                cs = pl.ds(g * 128, 128)
                w = jnp.where(mask, w_ref[g], 0.0).astype(BF16)
                mixed = _dot(w, vn[:, g * 128:(g + 1) * 128]) + bt_ref[:, g:g + 1]
                o_ref[rs, cs] = (_gelu(u_ref[rs, cs]) * mixed).astype(BF16)

    return pl.pallas_call(
        body, name=name, grid=(T // rows,),
        in_specs=[
            pl.BlockSpec((rows, A), lambda i: (i, 0)),
            pl.BlockSpec((rows, A), lambda i: (i, 1)),
            pl.BlockSpec((1, A), lambda i: (0, 0)),
            pl.BlockSpec((1, A), lambda i: (0, 0)),
            pl.BlockSpec((G, CHUNK, CHUNK), lambda i: (0, 0, 0)),
            pl.BlockSpec((CHUNK, G), lambda i: (0, 0)),
        ],
        out_specs=pl.BlockSpec((rows, A), lambda i: (i, 0)),
        out_shape=jax.ShapeDtypeStruct((T, 2 * A), BF16),
        compiler_params=_params("parallel"),
    )(z, z, ln_g, ln_b, w_s, b_t)


def _sgu_bwd(z, dcat, ln_g, ln_b, w_s, b_t, name):
    T = z.shape[0]
    A = ln_g.shape[1]
    G = A // 128
    rows = min(SGU_ROWS, T)

    def body(u_ref, v_ref, da_ref, g_ref, b_ref, w_ref, bt_ref, dz_ref, dg_ref, db_ref, dw_ref, dbt_ref, dvn_ref):
        @pl.when(pl.program_id(0) == 0)
        def _():
            dg_ref[...] = jnp.zeros_like(dg_ref)
            db_ref[...] = jnp.zeros_like(db_ref)
            dw_ref[...] = jnp.zeros_like(dw_ref)
            dbt_ref[...] = jnp.zeros_like(dbt_ref)

        mask = _tril_mask()
        for c in range(rows // CHUNK):
            rs = pl.ds(c * CHUNK, CHUNK)
            vv = v_ref[rs, :]
            gv = _gelu(vv)
            xh, rstd = _layernorm_stats(gv)
            vn = (xh * g_ref[...] + b_ref[...]).astype(BF16)
            for g in range(G):
                cs = pl.ds(g * 128, 128)
                w = jnp.where(mask, w_ref[g], 0.0).astype(BF16)
                vg = vn[:, g * 128:(g + 1) * 128]
                mixed = _dot(w, vg) + bt_ref[:, g:g + 1]
                uu = u_ref[rs, cs]
                da = da_ref[rs, cs]
                dz_ref[rs, cs] = (da * mixed * _gelu_grad(uu)).astype(BF16)
                dm = da * _gelu(uu)
                dmb = dm.astype(BF16)
                dbt_ref[:, g:g + 1] += jnp.sum(dm, axis=1, keepdims=True)
                dw_ref[g] += jnp.where(mask, _dot(dmb, vg, NT), 0.0)
                dvn_ref[:, cs] = _dot(w, dmb, TN)
            dvn = dvn_ref[...]
            dg_ref[...] += jnp.sum(dvn * xh, axis=0, keepdims=True)
            db_ref[...] += jnp.sum(dvn, axis=0, keepdims=True)
            dxh = dvn * g_ref[...]
            dgv = rstd * (dxh - jnp.mean(dxh, axis=-1, keepdims=True)
                          - xh * jnp.mean(dxh * xh, axis=-1, keepdims=True))
            dz_ref[rs, pl.ds(A, A)] = (dgv * _gelu_grad(vv)).astype(BF16)

    vec = pl.BlockSpec((1, A), lambda i: (0, 0))
    wsp = pl.BlockSpec((G, CHUNK, CHUNK), lambda i: (0, 0, 0))
    bsp = pl.BlockSpec((CHUNK, G), lambda i: (0, 0))
    return pl.pallas_call(
        body, name=name, grid=(T // rows,),
        in_specs=[
            pl.BlockSpec((rows, A), lambda i: (i, 0)),
            pl.BlockSpec((rows, A), lambda i: (i, 1)),
            pl.BlockSpec((rows, A), lambda i: (i, 0)),
            vec, vec, wsp, bsp,
        ],
        out_specs=[pl.BlockSpec((rows, 2 * A), lambda i: (i, 0)), vec, vec, wsp, bsp],
        out_shape=[
            jax.ShapeDtypeStruct((T, 2 * A), BF16),
            jax.ShapeDtypeStruct((1, A), F32),
            jax.ShapeDtypeStruct((1, A), F32),
            jax.ShapeDtypeStruct((G, CHUNK, CHUNK), F32),
            jax.ShapeDtypeStruct((CHUNK, G), F32),
        ],
        scratch_shapes=[pltpu.VMEM((CHUNK, A), F32)],
        compiler_params=_params("arbitrary"),
    )(z, z, dcat, ln_g, ln_b, w_s, b_t)


def _alibi_row(B, d):
    H = B // HEAD_DIM
    slopes = [d * 2.0 ** (-8.0 * (h + 1.0) / H) for h in range(H)]
    return jnp.repeat(jnp.asarray(slopes, F32), HEAD_DIM)[None, :]


def _dil_scores(q, k, slope_d, valid, dist):
    s = _dot(q, k, NT) - slope_d * dist
    return jnp.where(valid, s, MASKED)


def _dil_block(n, r, d):
    if d == 1:
        return pl.ds(pl.multiple_of(n * ATT_BLOCK, ATT_BLOCK), ATT_BLOCK)
    return pl.ds(n * (d * ATT_BLOCK) + r, ATT_BLOCK, stride=d)


def _dilated_forward(z, cat, B, name):
    T = z.shape[0]
    H = B // HEAD_DIM
    A = cat.shape[1] - B
    scale = HEAD_DIM ** -0.5
    blk = ATT_BLOCK
    chunk = _tile(T, 512)

    def body(q_ref, k_ref, v_ref, sl_ref, cat_in, cat_ref, of_ref, lt_ref, *branch):
        o_refs, l_refs = branch[:3], branch[3:]
        slope = sl_ref[:, :1]
        qi = lax.broadcasted_iota(jnp.int32, (blk, 2 * blk), 0)
        kj = lax.broadcasted_iota(jnp.int32, (blk, 2 * blk), 1)
        dist = qi + blk - kj
        band = (dist >= 0) & (dist <= blk)
        distf = dist.astype(F32)

        for b, (_, d) in enumerate(DILATED_PAIRS):
            def one(n, r, b=b, d=d):
                rows, prev = _dil_block(n, r, d), _dil_block(jnp.maximum(n - 1, 0), r, d)
                q = (q_ref[rows, :] * scale).astype(BF16)
                k = jnp.concatenate([k_ref[prev, :], k_ref[rows, :]], axis=0).astype(BF16)
                v = jnp.concatenate([v_ref[prev, :], v_ref[rows, :]], axis=0).astype(BF16)
                s = _dil_scores(q, k, slope * float(d), band & ((kj >= blk) | (n > 0)), distf)
                m = jnp.max(s, axis=-1, keepdims=True)
                p = jnp.exp(s - m)
                den = jnp.sum(p, axis=-1, keepdims=True)
                o_refs[b][rows, :] = _dot(p.astype(BF16), v) / den
                l_refs[b][rows, :] = jnp.broadcast_to(m + jnp.log(den), (blk, HEAD_DIM))

            per = max(1, 4 // d)

            def step(i, _, d=d, per=per, one=one):
                for u in range(per):
                    for r in range(d):
                        one(i * per + u, r)
                return 0

            lax.fori_loop(0, T // (d * blk * per), step, 0)

        def merge(i, _):
            rs = pl.ds(pl.multiple_of(i * chunk, chunk), chunk)
            a, b, c = l_refs[0][rs, :], l_refs[1][rs, :], l_refs[2][rs, :]
            m = jnp.maximum(jnp.maximum(a, b), c)
            ea, eb, ec = jnp.exp(a - m), jnp.exp(b - m), jnp.exp(c - m)
            tot = ea + eb + ec
            o = (ea * o_refs[0][rs, :] + eb * o_refs[1][rs, :] + ec * o_refs[2][rs, :]) / tot
            of_ref[rs, :] = o
            cat_ref[rs, :] = o.astype(BF16)
            lt_ref[rs, :] = m + jnp.log(tot)
            return 0

        lax.fori_loop(0, T // chunk, merge, 0)

    def col(unit):
        return lambda h: (0, unit * H + h)

    seq = (T, HEAD_DIM)
    out = pl.BlockSpec(seq, lambda h: (0, h))
    return pl.pallas_call(
        body, name=name, grid=(H,),
        in_specs=[pl.BlockSpec(seq, col(2)), pl.BlockSpec(seq, col(3)), pl.BlockSpec(seq, col(4)),
                  pl.BlockSpec((1, HEAD_DIM), lambda h: (0, h)), ANY],
        out_specs=[pl.BlockSpec(seq, lambda h: (0, A // HEAD_DIM + h)), out, out],
        out_shape=[jax.ShapeDtypeStruct(cat.shape, BF16), jax.ShapeDtypeStruct((T, B), F32),
                   jax.ShapeDtypeStruct((T, B), F32)],
        input_output_aliases={4: 0},
        scratch_shapes=[pltpu.VMEM(seq, F32)] * 6,
        compiler_params=_params("parallel"),
    )(z, z, z, _alibi_row(B, 1), cat)


def _dilated_backward(z, dcat, o, lse, B, name):
    T = z.shape[0]
    H = B // HEAD_DIM
    scale = HEAD_DIM ** -0.5
    blk = ATT_BLOCK
    chunk = _tile(T, 512)

    def body(q_ref, k_ref, v_ref, do_ref, o_ref, l_ref, sl_ref, dq_ref, dk_ref, dv_ref, aq_ref, ak_ref, av_ref):
        slope = sl_ref[:, :1]
        qi = lax.broadcasted_iota(jnp.int32, (blk, 2 * blk), 0)
        kj = lax.broadcasted_iota(jnp.int32, (blk, 2 * blk), 1)
        dist = qi + blk - kj
        band = (dist >= 0) & (dist <= blk)
        distf = dist.astype(F32)
        for acc in (aq_ref, ak_ref, av_ref):
            acc[...] = jnp.zeros_like(acc)

        for _, d in DILATED_PAIRS:
            def one(n, r, d=d):
                rows, prev = _dil_block(n, r, d), _dil_block(jnp.maximum(n - 1, 0), r, d)
                q = (q_ref[rows, :] * scale).astype(BF16)
                k = jnp.concatenate([k_ref[prev, :], k_ref[rows, :]], axis=0).astype(BF16)
                v = jnp.concatenate([v_ref[prev, :], v_ref[rows, :]], axis=0).astype(BF16)
                do = do_ref[rows, :]
                delta = jnp.sum(do * o_ref[rows, :], axis=-1, keepdims=True)
                do = do.astype(BF16)
                s = _dil_scores(q, k, slope * float(d), band & ((kj >= blk) | (n > 0)), distf)
                p = jnp.exp(s - l_ref[rows, :][:, :1])
                ds = (p * (_dot(do, v, NT) - delta)).astype(BF16)
                aq_ref[rows, :] += _dot(ds, k)
                dk = _dot(ds, q, TN)
                dv = _dot(p.astype(BF16), do, TN)
                ak_ref[prev, :] += dk[:blk]
                av_ref[prev, :] += dv[:blk]
                ak_ref[rows, :] += dk[blk:]
                av_ref[rows, :] += dv[blk:]

            per = max(1, 4 // d)

            def step(i, _, d=d, per=per, one=one):
                for u in range(per):
                    for r in range(d):
                        one(i * per + u, r)
                return 0

            lax.fori_loop(0, T // (d * blk * per), step, 0)

        def emit(i, _):
            rs = pl.ds(pl.multiple_of(i * chunk, chunk), chunk)
            dq_ref[rs, :] = (aq_ref[rs, :] * scale).astype(BF16)
            dk_ref[rs, :] = ak_ref[rs, :].astype(BF16)
            dv_ref[rs, :] = av_ref[rs, :].astype(BF16)
            return 0

        lax.fori_loop(0, T // chunk, emit, 0)

    def col(unit):
        return lambda h: (0, unit * H + h)

    seq = (T, HEAD_DIM)
    own = pl.BlockSpec(seq, lambda h: (0, h))
    return pl.pallas_call(
        body, name=name, grid=(H,),
        in_specs=[pl.BlockSpec(seq, col(2)), pl.BlockSpec(seq, col(3)), pl.BlockSpec(seq, col(4)),
                  pl.BlockSpec(seq, col(1)), own, own, pl.BlockSpec((1, HEAD_DIM), lambda h: (0, h))],
        out_specs=[own] * 3,
        out_shape=[jax.ShapeDtypeStruct((T, B), BF16)] * 3,
        scratch_shapes=[pltpu.VMEM(seq, F32)] * 3,
        compiler_params=_params("parallel"),
    )(z, z, z, dcat, o, lse, _alibi_row(B, 1))


def _join_columns(parts, name):
    T = parts[0].shape[0]
    widths = [p.shape[1] for p in parts]

    def body(*refs):
        o_ref, at = refs[-1], 0
        for ref, w in zip(refs[:-1], widths):
            o_ref[:, pl.ds(at, w)] = ref[...]
            at += w

    return pl.pallas_call(
        body, name=name, grid=(T // ROWS,),
        in_specs=[pl.BlockSpec((ROWS, w), lambda i: (i, 0)) for w in widths],
        out_specs=pl.BlockSpec((ROWS, sum(widths)), lambda i: (i, 0)),
        out_shape=jax.ShapeDtypeStruct((T, sum(widths)), BF16), compiler_params=_params("parallel"),
    )(*parts)


SB_QUERY_ROWS = 1024
SB_KEYS = 2 * ATT_BLOCK


def _tri_and_ones(pred):
    rows = lax.broadcasted_iota(jnp.int32, (2 * ATT_BLOCK, 2 * ATT_BLOCK), 0) % ATT_BLOCK
    cols = lax.broadcasted_iota(jnp.int32, (2 * ATT_BLOCK, 2 * ATT_BLOCK), 1)
    return ((cols >= ATT_BLOCK) | pred(rows, cols)).astype(BF16)


def _running(x, tri):
    hi = x.astype(BF16)
    lo = (x - hi.astype(F32)).astype(BF16)
    return _dot(jnp.concatenate([hi, lo], axis=1), tri)


def _sb_mask(query_rows, s):
    rows = lax.broadcasted_iota(jnp.int32, (query_rows - s * SB_KEYS, SB_KEYS), 0)
    cols = lax.broadcasted_iota(jnp.int32, (query_rows - s * SB_KEYS, SB_KEYS), 1)
    return cols < rows


def _log_sigmoids(z):
    ls = jnp.minimum(z, 0.0) - jnp.log(1.0 + jnp.exp(-jnp.abs(z)))
    return ls, ls - z


def _sb_fwd(qkv, W, name):
    T = qkv.shape[0]
    H = W // HEAD_DIM
    blk = ATT_BLOCK
    qb = min(SB_QUERY_ROWS, T)
    per = qb // SB_KEYS

    def body(q_ref, k_ref, v_ref, o_ref, lt_ref, acc_ref):
        i = pl.program_id(1)
        tri = _tri_and_ones(lambda r, c: r > c)
        lt_ref[...] = jnp.zeros_like(lt_ref)
        acc_ref[...] = jnp.zeros_like(acc_ref)

        def tile(j, mask, r0=0):
            ks = pl.ds(pl.multiple_of(j * SB_KEYS, SB_KEYS), SB_KEYS)
            qs = pl.ds(r0, qb - r0)
            z = _dot(q_ref[qs, :], k_ref[ks, :], NT)
            ls, lm = _log_sigmoids(z)
            if mask is not None:
                lm = jnp.where(mask, lm, 0.0)
            later = lt_ref[qs, :]
            second = _running(lm[:, blk:], tri)
            first = _running(lm[:, :blk], tri)
            after_first = later + second[:, blk:]
            a = jnp.exp(ls + jnp.concatenate([first[:, :blk] + after_first, second[:, :blk] + later], axis=1))
            if mask is not None:
                a = jnp.where(mask, a, 0.0)
            acc_ref[qs, :] += _dot(a.astype(BF16), v_ref[ks, :])
            lt_ref[qs, :] = after_first + first[:, blk:]

        for s in reversed(range(per)):
            tile(i * per + s, _sb_mask(qb, s), s * SB_KEYS)

        def step(jj, _):
            for s in range(per):
                tile((i - jj) * per - 1 - s, None)
            return 0

        lax.fori_loop(0, i, step, 0)
        o_ref[...] = acc_ref[...].astype(BF16)

    qs = pl.BlockSpec((qb, HEAD_DIM), lambda h, i: (i, h))
    return pl.pallas_call(
        body, name=name, grid=(H, T // qb),
        in_specs=[qs, pl.BlockSpec((T, HEAD_DIM), lambda h, i: (0, H + h)),
                  pl.BlockSpec((T, HEAD_DIM), lambda h, i: (0, 2 * H + h))],
        out_specs=[qs, qs],
        out_shape=[jax.ShapeDtypeStruct((T, W), BF16), jax.ShapeDtypeStruct((T, W), F32)],
        scratch_shapes=[pltpu.VMEM((qb, HEAD_DIM), F32)],
        compiler_params=_params("parallel", "arbitrary"),
    )(qkv, qkv, qkv)


def _sb_bwd(qkv, do, ltot, W, name):
    T = qkv.shape[0]
    H = W // HEAD_DIM
    blk = ATT_BLOCK
    nkb = T // SB_KEYS
    qb = min(SB_QUERY_ROWS, T)
    per = qb // SB_KEYS

    def body(q_ref, k_ref, v_ref, do_ref, lt_ref, dq_ref, dkt_ref, dvt_ref, qt_ref, dot_ref, plm_ref, pg_ref):
        i = pl.program_id(1)

        @pl.when(i == 0)
        def _():
            dkt_ref[...] = jnp.zeros_like(dkt_ref)
            dvt_ref[...] = jnp.zeros_like(dvt_ref)

        qt_ref[...] = q_ref[...].astype(F32).T.astype(BF16)
        dot_ref[...] = do_ref[...].astype(F32).T.astype(BF16)
        upto = _tri_and_ones(lambda r, c: r <= c)
        before = _tri_and_ones(lambda r, c: r < c)
        plm_ref[...] = jnp.zeros_like(plm_ref)
        pg_ref[...] = jnp.zeros_like(pg_ref)
        dq_ref[...] = jnp.zeros_like(dq_ref)

        def tile(j, mask, r0=0):
            ks = pl.ds(pl.multiple_of(j * SB_KEYS, SB_KEYS), SB_KEYS)
            qs = pl.ds(r0, qb - r0)
            k = k_ref[ks, :]
            v = v_ref[ks, :]
            z = _dot(q_ref[qs, :], k, NT)
            ls, lm = _log_sigmoids(z)
            nsig = jnp.exp(lm)
            if mask is not None:
                lm = jnp.where(mask, lm, 0.0)
            earlier = plm_ref[qs, :]
            first = _running(lm[:, :blk], upto)
            second = _running(lm[:, blk:], upto)
            upto_first = earlier + first[:, blk:]
            seen = jnp.concatenate([first[:, :blk] + earlier, second[:, :blk] + upto_first], axis=1)
            ltot = lt_ref[qs, :]
            a = jnp.exp(ls + (jnp.concatenate([ltot, ltot], axis=1) - seen))
            if mask is not None:
                a = jnp.where(mask, a, 0.0)
            g = a * _dot(do_ref[qs, :], v, NT)
            g_earlier = pg_ref[qs, :]
            g_first = _running(g[:, :blk], before)
            g_second = _running(g[:, blk:], before)
            g_upto_first = g_earlier + g_first[:, blk:]
            gsum = jnp.concatenate([g_first[:, :blk] + g_earlier, g_second[:, :blk] + g_upto_first], axis=1)
            dz = g * nsig - gsum * jnp.exp(ls)
            if mask is not None:
                dz = jnp.where(mask, dz, 0.0)
            dzb = dz.astype(BF16)
            dkt_ref[j] += _dot(qt_ref[:, qs], dzb)
            dvt_ref[j] += _dot(dot_ref[:, qs], a.astype(BF16))
            dq_ref[qs, :] += _dot(dzb, k)
            plm_ref[qs, :] = upto_first + second[:, blk:]
            pg_ref[qs, :] = g_upto_first + g_second[:, blk:]

        def step(jj, _):
            for s in range(per):
                tile(jj * per + s, None)
            return 0

        lax.fori_loop(0, i, step, 0)
        for s in range(per):
            tile(i * per + s, _sb_mask(qb, s), s * SB_KEYS)

    qs = pl.BlockSpec((qb, HEAD_DIM), lambda h, i: (i, h))
    res = pl.BlockSpec((None, nkb, HEAD_DIM, SB_KEYS), lambda h, i: (h, 0, 0, 0))
    return pl.pallas_call(
        body, name=name, grid=(H, T // qb),
        in_specs=[qs, pl.BlockSpec((T, HEAD_DIM), lambda h, i: (0, H + h)),
                  pl.BlockSpec((T, HEAD_DIM), lambda h, i: (0, 2 * H + h)), qs, qs],
        out_specs=[qs, res, res],
        out_shape=[jax.ShapeDtypeStruct((T, W), F32)] + [jax.ShapeDtypeStruct((H, nkb, HEAD_DIM, SB_KEYS), F32)] * 2,
        scratch_shapes=[pltpu.VMEM((HEAD_DIM, qb), BF16), pltpu.VMEM((HEAD_DIM, qb), BF16),
                        pltpu.VMEM((qb, HEAD_DIM), F32), pltpu.VMEM((qb, HEAD_DIM), F32)],
        compiler_params=_params("parallel", "arbitrary"),
    )(qkv, qkv, qkv, do, ltot)


def _sb_pack(dq, dkt, dvt, name):
    T, W = dq.shape
    H = W // HEAD_DIM
    blk = SB_KEYS
    scale = HEAD_DIM ** -0.5

    def body(q_ref, kt_ref, vt_ref, o_ref):
        o_ref[:, pl.ds(0, W)] = (q_ref[...] * scale).astype(BF16)
        for h in range(H):
            o_ref[:, pl.ds(W + h * HEAD_DIM, HEAD_DIM)] = kt_ref[h].T.astype(BF16)
            o_ref[:, pl.ds(2 * W + h * HEAD_DIM, HEAD_DIM)] = vt_ref[h].T.astype(BF16)

    tr = pl.BlockSpec((H, None, HEAD_DIM, blk), lambda i: (0, i, 0, 0))
    return pl.pallas_call(
        body, name=name, grid=(T // blk,), in_specs=[pl.BlockSpec((blk, W), lambda i: (i, 0)), tr, tr],
        out_specs=pl.BlockSpec((blk, 3 * W), lambda i: (i, 0)),
        out_shape=jax.ShapeDtypeStruct((T, 3 * W), BF16), compiler_params=_params("parallel"),
    )(dq, dkt, dvt)


def _local_step(x, target, norms, sgu, comm):
    T, D = x.shape
    A = D // 2
    pre_mix, post_mix, pre_ffn, post_ffn = norms
    ln_g, ln_b, w_s, b_s = sgu
    b_t = b_s.T
    scale = HEAD_DIM ** -0.5

    def vec(p, layer):
        return comm.tie(p[layer:layer + 1])

    h0 = _prenorm(x, vec(pre_mix, 0), "prenorm0")
    comm.arrive("ab_in", after=h0)
    comm.land("ab_in")
    z = _matmul(h0, comm.weight("ab_in"), mode="nn", name="ab_in_fwd")
    comm.arrive("ab_out", after=z)
    cat = _sgu_fwd(z, ln_g, ln_b, w_s, b_t, "sgu_fwd")
    cat, o_dil, lse_dil = _dilated_forward(z, cat, A, "dilated_fwd")
    comm.land("ab_out", after=o_dil)
    y0 = _matmul(cat, comm.weight("ab_out"), mode="nn", name="ab_out_fwd")
    comm.arrive("w1_0", after=y0)
    x1, h1 = _postnorm_prenorm(x, y0, vec(post_mix, 0), vec(pre_ffn, 0), "norm_mix0")
    comm.land("w1_0", after=h1)
    comm.arrive("w2_0", after=h1)

    def relu2(acc, j):
        r = jnp.maximum(acc, 0.0)
        return r * r, 2.0 * r

    f0, r0 = _matmul(h1, comm.weight("w1_0"), mode="nn", name="ffn0_w1_fwd", out_dtype=(BF16, BF16), epi=relu2)
    comm.land("w2_0", after=f0)
    y1 = _matmul(f0, comm.weight("w2_0"), mode="nn", name="ffn0_w2_fwd")
    comm.arrive("sb", after=y1)
    x2, h2 = _postnorm_prenorm(x1, y1, vec(post_ffn, 0), vec(pre_mix, 1), "norm_ffn0")
    comm.land("sb", after=h2)

    tn_qkv = _tile(D, 1024)
    nq = D // tn_qkv

    def scale_q(acc, j):
        return jnp.where(j < nq, acc * scale, acc)

    qkv = _matmul(h2, comm.weight("sb_in"), mode="nn", name="sb_in_fwd", out_dtype=BF16, tn=tn_qkv, epi=scale_q)
    comm.arrive("ffn1", after=qkv)
    o_sb, ltot = _sb_fwd(qkv, D, "sb_fwd")
    comm.land("ffn1", after=o_sb)
    y2 = _matmul(o_sb, comm.weight("sb_out"), mode="nn", name="sb_out_fwd")
    x3, h3 = _postnorm_prenorm(x2, y2, vec(post_mix, 1), vec(pre_ffn, 1), "norm_mix1")
    f1, r1 = _matmul(h3, comm.weight("w1_1"), mode="nn", name="ffn1_w1_fwd", out_dtype=(BF16, BF16), epi=relu2)
    y3 = _matmul(f1, comm.weight("w2_1"), mode="nn", name="ffn1_w2_fwd")
    loss_tile, dx4, dy3, dg_post_ffn1 = _postnorm_loss(x3, y3, vec(post_ffn, 1), target, "norm_loss")
    loss = loss_tile[0, 0]

    def relu2_bwd(acc, j, r):
        return acc * r.astype(F32)

    def ffn_bwd(dy, h, f, r, layer):
        g_w2 = _matmul(f, dy, mode="tn", name=f"ffn{layer}_w2_wgrad", out_dtype=BF16)
        da = _matmul(dy, comm.weight(f"w2_{layer}"), mode="nt", name=f"ffn{layer}_w2_dgrad", out_dtype=BF16,
                     epi=relu2_bwd, extras=(r,))
        g_w1 = _matmul(h, da, mode="tn", name=f"ffn{layer}_w1_wgrad", out_dtype=BF16)
        comm.reduce(f"ffn{layer}", {f"w2_{layer}": g_w2, f"w1_{layer}": g_w1})
        return _matmul(da, comm.weight(f"w1_{layer}"), mode="nt", name=f"ffn{layer}_w1_dgrad", after=comm.started())

    dh3 = ffn_bwd(dy3, h3, f1, r1, 1)
    dx3, dg_pre_ffn1, dy2, dg_post_mix1 = _norm_bwd_pair(dx4, dh3, x3, vec(pre_ffn, 1), y2, vec(post_mix, 1),
                                                         "ffn1_sb_norm_bwd")
    g_sb_out = _matmul(o_sb, dy2, mode="tn", name="sb_out_wgrad", out_dtype=BF16)
    do_sb = _matmul(dy2, comm.weight("sb_out"), mode="nt", name="sb_out_dgrad", out_dtype=BF16)
    dq, dk, dv = _sb_bwd(qkv, do_sb, ltot, D, "sb_bwd")
    dqkv = _sb_pack(dq, dk, dv, "sb_pack")
    g_sb_in = _matmul(h2, dqkv, mode="tn", name="sb_in_wgrad", out_dtype=BF16)
    comm.reduce("sb", {"sb_out": g_sb_out, "sb_in": g_sb_in})
    dh2 = _matmul(dqkv, comm.weight("sb_in"), mode="nt", name="sb_in_dgrad", after=comm.started())
    dx2, dg_pre_mix1, dy1, dg_post_ffn0 = _norm_bwd_pair(dx3, dh2, x2, vec(pre_mix, 1), y1, vec(post_ffn, 0),
                                                         "sb_ffn0_norm_bwd")
    dh1 = ffn_bwd(dy1, h1, f0, r0, 0)
    dx1, dg_pre_ffn0, dy0, dg_post_mix0 = _norm_bwd_pair(dx2, dh1, x1, vec(pre_ffn, 0), y0, vec(post_mix, 0),
                                                         "ffn0_ab_norm_bwd")
    g_ab_out = _matmul(cat, dy0, mode="tn", name="ab_out_wgrad", out_dtype=BF16)
    comm.reduce("ab_out", {"ab_out": g_ab_out})
    dcat = _matmul(dy0, comm.weight("ab_out"), mode="nt", name="ab_out_dgrad", after=comm.started())
    duv, d_ln_g, d_ln_b, d_w_s, d_b_t = _sgu_bwd(z, dcat, ln_g, ln_b, w_s, b_t, "sgu_bwd")
    dz = _join_columns([duv, *_dilated_backward(z, dcat, o_dil, lse_dil, A, "dilated_bwd")], "join_dz")
    g_ab_in = _matmul(h0, dz, mode="tn", name="ab_in_wgrad", out_dtype=BF16)
    comm.reduce("ab_in", {"ab_in": g_ab_in})
    dh0 = _matmul(dz, comm.weight("ab_in"), mode="nt", name="ab_in_dgrad", after=comm.started())
    dx0, dg_pre_mix0 = _prenorm_bwd(dx1, dh0, x, vec(pre_mix, 0), "ab_prenorm_bwd")

    small = {
        "pre_mix": jnp.concatenate([dg_pre_mix0, dg_pre_mix1], axis=0),
        "post_mix": jnp.concatenate([dg_post_mix0, dg_post_mix1], axis=0),
        "pre_ffn": jnp.concatenate([dg_pre_ffn0, dg_pre_ffn1], axis=0),
        "post_ffn": jnp.concatenate([dg_post_ffn0, dg_post_ffn1], axis=0),
        "ln_g": d_ln_g, "ln_b": d_ln_b, "w_s": d_w_s, "b_s": d_b_t.T,
    }
    return loss, dx0, small


MESH_ID = pl.DeviceIdType.MESH
ANY = pl.BlockSpec(memory_space=pl.ANY)


def _coords():
    return lax.axis_index("x"), lax.axis_index("y"), lax.axis_index("c")


def _shard_of(ref, kind, p):
    if kind == "col":
        n = ref.shape[1] // N_DEV
        return ref.at[:, pl.ds(pl.multiple_of(p * n, 128), n)]
    r = ref.shape[0] // N_DEV
    return ref.at[pl.ds(pl.multiple_of(p * r, 16), r), :]


def _full_shape(shard, kind):
    if kind == "col":
        return (shard.shape[0], shard.shape[1] * N_DEV)
    return (shard.shape[0] * N_DEV, shard.shape[1])


def _place(shards, layer, kind, block, after, name):
    _, rows, cols = shards.shape
    tr = _tile(rows, 512)

    def body(b_ref, s_ref, after_ref, o_ref):
        o_ref[...] = s_ref[...].astype(BF16)

    if kind == "col":
        out = pl.BlockSpec((tr, cols), lambda i, b_ref: (i, b_ref[0]))
    else:
        out = pl.BlockSpec((tr, cols), lambda i, b_ref: (b_ref[0] * (rows // tr) + i, 0))
    return pl.pallas_call(
        body, name=name,
        grid_spec=pltpu.PrefetchScalarGridSpec(
            num_scalar_prefetch=1, grid=(rows // tr,),
            in_specs=[pl.BlockSpec((None, tr, cols), lambda i, b_ref: (layer, i, 0)), ANY], out_specs=out),
        out_shape=jax.ShapeDtypeStruct(_full_shape(shards[0], kind), BF16),
        compiler_params=_params("parallel"),
    )(block, shards, after)


HBM = pl.BlockSpec(memory_space=pltpu.HBM)
SEM = pl.BlockSpec(memory_space=pltpu.SEMAPHORE)
FLOWS = pltpu.SideEffectType.DATAFLOW_SIDE_EFFECTING


def _in_hbm(a):
    return pltpu.with_memory_space_constraint(a, pltpu.HBM)


def _hbm_like(bufs):
    return [pltpu.HBM(b.shape, b.dtype) for b in bufs]


def _copies_start(name, bufs, plan, n, after):
    nb = len(bufs)

    def body(*refs):
        send_sems, recv_sems, token = refs[nb + 1], refs[nb + 2], refs[-1]
        for cp in plan(refs[:nb], send_sems, recv_sems):
            cp.start()
        token[...] = jnp.zeros_like(token)

    out = pl.pallas_call(
        body, name=name, in_specs=[HBM] * nb + [ANY],
        out_specs=[SEM, SEM] + [HBM] * nb + [pl.BlockSpec(memory_space=pltpu.VMEM)],
        out_shape=[pltpu.SemaphoreType.DMA((n,)), pltpu.SemaphoreType.DMA((n,))] + _hbm_like(bufs)
        + [jax.ShapeDtypeStruct((8, 128), F32)],
        input_output_aliases={i: 2 + i for i in range(nb)},
        compiler_params=pltpu.CompilerParams(has_side_effects=FLOWS),
    )(*[_in_hbm(b) for b in bufs], after)
    return (out[0], out[1]), list(out[2:2 + nb]), out[-1]


def _copies_wait(name, bufs, sems, after, plan):
    nb = len(bufs)

    def body(*refs):
        for cp in plan(refs[:nb], refs[nb], refs[nb + 1]):
            cp.wait_send()
            cp.wait_recv()

    out = pl.pallas_call(
        body, name=name, in_specs=[HBM] * nb + [SEM, SEM, ANY], out_specs=[HBM] * nb,
        out_shape=_hbm_like(bufs), input_output_aliases={i: i for i in range(nb)},
        compiler_params=pltpu.CompilerParams(has_side_effects=FLOWS),
    )(*bufs, *sems, after)
    return list(out)


def _copies_wait_start(name, bufs, sems, after, plan, next_plan, n_next):
    nb = len(bufs)

    def body(*refs):
        ins = refs[:nb]
        for cp in plan(ins, refs[nb], refs[nb + 1]):
            cp.wait_send()
            cp.wait_recv()
        send_sems, recv_sems, token = refs[nb + 3], refs[nb + 4], refs[-1]
        for cp in next_plan(ins, send_sems, recv_sems):
            cp.start()
        token[...] = jnp.zeros_like(token)

    out = pl.pallas_call(
        body, name=name, in_specs=[HBM] * nb + [SEM, SEM, ANY],
        out_specs=[SEM, SEM] + [HBM] * nb + [pl.BlockSpec(memory_space=pltpu.VMEM)],
        out_shape=[pltpu.SemaphoreType.DMA((n_next,)), pltpu.SemaphoreType.DMA((n_next,))] + _hbm_like(bufs)
        + [jax.ShapeDtypeStruct((8, 128), F32)],
        input_output_aliases={i: 2 + i for i in range(nb)},
        compiler_params=pltpu.CompilerParams(has_side_effects=FLOWS),
    )(*bufs, *sems, after)
    return (out[0], out[1]), list(out[2:2 + nb]), out[-1]


def _gather_plans(kinds):
    nt = len(kinds)

    def slot(refs, t, px, py, pc):
        return _shard_of(refs[t], kinds[t], 4 * px + 2 * py + pc)

    def to_chips(refs, send_sems, recv_sems):
        x, y, c = _coords()
        peers = [(x, y, 1 - c), (1 - x, y, c), (x, 1 - y, c), (1 - x, 1 - y, c)]
        return [pltpu.make_async_remote_copy(
            src_ref=slot(refs, t, x, y, c), dst_ref=slot(refs, t, x, y, c), send_sem=send_sems.at[4 * t + k],
            recv_sem=recv_sems.at[4 * t + k], device_id=peer, device_id_type=MESH_ID)
            for t in range(nt) for k, peer in enumerate(peers)]

    def to_sibling(refs, send_sems, recv_sems):
        x, y, c = _coords()
        chips = [(1 - x, y), (x, 1 - y), (1 - x, 1 - y)]
        return [pltpu.make_async_remote_copy(
            src_ref=slot(refs, t, *chip, c), dst_ref=slot(refs, t, *chip, c), send_sem=send_sems.at[3 * t + j],
            recv_sem=recv_sems.at[3 * t + j], device_id=(x, y, 1 - c), device_id_type=MESH_ID)
            for t in range(nt) for j, chip in enumerate(chips)]

    return to_chips, to_sibling


def _shard_shape(full, kind):
    if kind == "col":
        return (full.shape[0], full.shape[1] // N_DEV)
    return (full.shape[0] // N_DEV, full.shape[1])


def _scatter_plan(kinds):
    nt = len(kinds)

    def plan(refs, send_sems, recv_sems):
        x, y, c = _coords()
        copies = []
        for t in range(nt):
            for k in range(1, N_DEV):
                px = 1 - x if (k >> 2) & 1 else x
                py = 1 - y if (k >> 1) & 1 else y
                pc = 1 - c if k & 1 else c
                copies.append(pltpu.make_async_remote_copy(
                    src_ref=_shard_of(refs[t], kinds[t], 4 * px + 2 * py + pc),
                    dst_ref=refs[nt + t].at[4 * x + 2 * y + c],
                    send_sem=send_sems.at[7 * t + k - 1], recv_sem=recv_sems.at[7 * t + k - 1],
                    device_id=(px, py, pc), device_id_type=MESH_ID))
        return copies
    return plan


def _partial_specs(full, kind, tr):
    rows, cols = _shard_shape(full, kind)
    steps = rows // tr
    if kind == "col":
        own = pl.BlockSpec((tr, cols), lambda i, w: (i, w[0]))
    else:
        own = pl.BlockSpec((tr, cols), lambda i, w: (w[0] * steps + i, 0))
    return [own] + [pl.BlockSpec((None, tr, cols), lambda i, w, k=k: (w[k], i, 0)) for k in range(1, N_DEV)]


def _all_reduce_small(vec, after):
    R = vec.shape[0]

    def body(v_ref, after_ref, o_ref, recv_ref, send_sems, recv_sems):
        x, y, c = _coords()
        me = 4 * x + 2 * y + c
        recv_ref[me] = v_ref[...]
        copies = []
        for k in range(1, N_DEV):
            bx, by, bc = (k >> 2) & 1, (k >> 1) & 1, k & 1
            peer = (1 - x if bx else x, 1 - y if by else y, 1 - c if bc else c)
            copies.append(pltpu.make_async_remote_copy(
                src_ref=v_ref, dst_ref=recv_ref.at[me],
                send_sem=send_sems.at[k - 1], recv_sem=recv_sems.at[k - 1],
                device_id=peer, device_id_type=MESH_ID))
        for cp in copies:
            cp.start()
        for cp in copies:
            cp.wait()
        total = recv_ref[0]
        for p in range(1, N_DEV):
            total = total + recv_ref[p]
        o_ref[...] = total

    return pl.pallas_call(
        body, name="all_reduce_small",
        in_specs=[pl.BlockSpec(memory_space=pltpu.VMEM), ANY], out_specs=pl.BlockSpec(memory_space=pltpu.VMEM),
        out_shape=jax.ShapeDtypeStruct((R, 128), F32),
        scratch_shapes=[pltpu.VMEM((N_DEV, R, 128), F32), pltpu.SemaphoreType.DMA((N_DEV - 1,)),
                        pltpu.SemaphoreType.DMA((N_DEV - 1,))],
        compiler_params=pltpu.CompilerParams(vmem_limit_bytes=VMEM_LIMIT),
    )(vec, after)


def _adamw_math(w, g, m, v):
    m = ADAM_B1 * m + (1.0 - ADAM_B1) * g
    v = ADAM_B2 * v + (1.0 - ADAM_B2) * (g * g)
    m_hat = m / (1.0 - ADAM_B1 ** ADAM_STEP)
    v_hat = v / (1.0 - ADAM_B2 ** ADAM_STEP)
    delta = -ADAM_LR * (m_hat / (jnp.sqrt(v_hat) + ADAM_EPS) + ADAM_WD * w)
    return delta, m, v


def _adamw(w, grads, kind, where, m, v, name):
    layers, rows, cols = w.shape
    tr = _tile(rows, 128)
    out = None
    for layer, (grad, land) in enumerate(grads):
        def body(w_ref, *refs):
            parts, (x_ref, m_ref, v_ref) = refs[:N_DEV], refs[N_DEV:N_DEV + 3]
            g_ref, d_ref, mo_ref, vo_ref = refs[-4:]
            g = parts[0][...].astype(F32)
            for p_ref in parts[1:]:
                g = g + p_ref[...].astype(F32)
            g_ref[...] = g
            d_ref[...], mo_ref[...], vo_ref[...] = _adamw_math(x_ref[...], g, m_ref[...], v_ref[...])

        blk = pl.BlockSpec((None, tr, cols), lambda i, w_, layer=layer: (layer, i, 0))
        earlier = [] if out is None else list(out)
        out = pl.pallas_call(
            body, name=f"{name}_{layer}",
            grid_spec=pltpu.PrefetchScalarGridSpec(
                num_scalar_prefetch=1, grid=(rows // tr,),
                in_specs=_partial_specs(grad, kind, tr) + [blk] * 3 + [ANY] * len(earlier),
                out_specs=[blk] * 4),
            out_shape=[jax.ShapeDtypeStruct((layers, rows, cols), F32)] * 4,
            input_output_aliases={N_DEV + 4 + k: k for k in range(len(earlier))},
            compiler_params=_params("parallel"),
        )(where, grad, *[land] * (N_DEV - 1), w, m, v, *earlier)
    return out


def _adamw_small(w, g, m, v):
    def body(w_ref, g_ref, m_ref, v_ref, d_ref, mo_ref, vo_ref):
        d_ref[...], mo_ref[...], vo_ref[...] = _adamw_math(w_ref[...], g_ref[...], m_ref[...], v_ref[...])

    whole = pl.BlockSpec(memory_space=pltpu.VMEM)
    return pl.pallas_call(
        body, name="adamw_small", in_specs=[whole] * 4, out_specs=[whole] * 3,
        out_shape=[jax.ShapeDtypeStruct(w.shape, F32)] * 3,
        compiler_params=pltpu.CompilerParams(vmem_limit_bytes=VMEM_LIMIT),
    )(w, g, m, v)


def _pack(arrays):
    rows = []
    for a in arrays:
        flat = a.reshape(-1)
        pad = (-flat.shape[0]) % 1024
        rows.append(jnp.pad(flat, (0, pad)).reshape(-1, 128))
    return jnp.concatenate(rows, axis=0)


def _unpack(packed, like):
    out, r = [], 0
    for a in like:
        n = math.prod(a.shape)
        nr = (n + 1023) // 1024 * 8
        out.append(packed[r:r + nr].reshape(-1)[:n].reshape(a.shape))
        r += nr
    return out


KIND = {"ab_in": "col", "ab_out": "row", "sb_in": "col", "sb_out": "row",
        "w1_0": "col", "w1_1": "col", "w2_0": "row", "w2_1": "row"}
GATHERS = {"ab_in": ("ab_in",), "ab_out": ("ab_out",), "w1_0": ("w1_0",), "w2_0": ("w2_0",),
           "sb": ("sb_in", "sb_out"), "ffn1": ("w1_1", "w2_1")}


class _Exchange:
    def __init__(self, shards):
        x, y, c = _coords()
        me = (4 * x + 2 * y + c).astype(jnp.int32)
        self.where = jnp.stack([jnp.bitwise_xor(me, k) for k in range(N_DEV)])
        self.full = {}
        self.tokens = []
        self.gathers = {}
        self.scatters = {}
        self.settled = {}
        block = me.reshape(1)
        after = block
        for key, group in GATHERS.items():
            for n in group:
                self.full[n] = _place(*shards[n], KIND[n], block, after, f"place_{n}")
            to_chips, to_sibling = _gather_plans([KIND[n] for n in group])
            bufs = [self.full[n] for n in group]
            sems, bufs, after = _copies_start(f"gather_start_{key}", bufs, to_chips, 4 * len(group), after)
            self.tokens.append(after)
            self.gathers[key] = (group, sems, bufs, to_chips, to_sibling)

    def tie(self, small):
        for token in self.tokens:
            small = small + token[0:1, 0:1]
        self.tokens = []
        return small

    def started(self):
        return tuple(self.tokens)

    def weight(self, name):
        return self.full[name]

    def arrive(self, key, after):
        group, sems, bufs, to_chips, to_sibling = self.gathers[key]
        sems, bufs, token = _copies_wait_start(f"gather_pass_{key}", bufs, sems, after, to_chips, to_sibling,
                                               3 * len(group))
        self.tokens.append(token)
        self.gathers[key] = (group, sems, bufs, token, to_sibling)

    def land(self, key, after=None):
        group, sems, bufs, token, to_sibling = self.gathers.pop(key)
        after = token if after is None else after
        self.full.update(zip(group, _copies_wait(f"gather_done_{key}", bufs, sems, after, to_sibling)))

    def reduce(self, key, grads):
        names = list(grads)
        kinds = [KIND[n] for n in names]
        full = [grads[n] for n in names]
        lands = [lax.empty((N_DEV,) + _shard_shape(g, k), BF16) for g, k in zip(full, kinds)]
        plan = _scatter_plan(kinds)
        sems, bufs, token = _copies_start(f"scatter_start_{key}", full + lands, plan, (N_DEV - 1) * len(names),
                                          full[-1])
        self.tokens.append(token)
        self.scatters[key] = (names, sems, bufs, plan)

    def settle(self, keys, after):
        for key in keys:
            names, sems, bufs, plan = self.scatters.pop(key)
            bufs = _copies_wait(f"scatter_done_{key}", bufs, sems, after, plan)
            self.settled.update({n: t for n, *t in zip(names, bufs[:len(names)], bufs[len(names):])})
        return self.settled


SMALL = ("norm_pre_mix", "norm_post_mix", "norm_pre_ffn", "norm_post_ffn", "sgu_ln_g", "sgu_ln_b", "sgu_w", "sgu_b")
ORDER = ("norm_pre_mix", "norm_post_mix", "norm_pre_ffn", "norm_post_ffn", "ab_w_in", "sgu_ln_g", "sgu_ln_b", "sgu_w",
         "sgu_b", "ab_w_out", "sb_w_in", "sb_w_out", "ffn_w1", "ffn_w2")


def kernel(x, norm_pre_mix, norm_post_mix, norm_pre_ffn, norm_post_ffn, ab_w_in, sgu_ln_g, sgu_ln_b, sgu_w, sgu_b, ab_w_out, sb_w_in, sb_w_out, ffn_w1, ffn_w2, loss_target, m_norm_pre_mix, m_norm_post_mix, m_norm_pre_ffn, m_norm_post_ffn, m_ab_w_in, m_sgu_ln_g, m_sgu_ln_b, m_sgu_w, m_sgu_b, m_ab_w_out, m_sb_w_in, m_sb_w_out, m_ffn_w1, m_ffn_w2, v_norm_pre_mix, v_norm_post_mix, v_norm_pre_ffn, v_norm_post_ffn, v_ab_w_in, v_sgu_ln_g, v_sgu_ln_b, v_sgu_w, v_sgu_b, v_ab_w_out, v_sb_w_in, v_sb_w_out, v_ffn_w1, v_ffn_w2):
    W = dict(norm_pre_mix=norm_pre_mix, norm_post_mix=norm_post_mix, norm_pre_ffn=norm_pre_ffn,
             norm_post_ffn=norm_post_ffn, ab_w_in=ab_w_in, sgu_ln_g=sgu_ln_g, sgu_ln_b=sgu_ln_b, sgu_w=sgu_w,
             sgu_b=sgu_b, ab_w_out=ab_w_out, sb_w_in=sb_w_in, sb_w_out=sb_w_out, ffn_w1=ffn_w1, ffn_w2=ffn_w2)
    M = dict(norm_pre_mix=m_norm_pre_mix, norm_post_mix=m_norm_post_mix, norm_pre_ffn=m_norm_pre_ffn,
             norm_post_ffn=m_norm_post_ffn, ab_w_in=m_ab_w_in, sgu_ln_g=m_sgu_ln_g, sgu_ln_b=m_sgu_ln_b,
             sgu_w=m_sgu_w, sgu_b=m_sgu_b, ab_w_out=m_ab_w_out, sb_w_in=m_sb_w_in, sb_w_out=m_sb_w_out,
             ffn_w1=m_ffn_w1, ffn_w2=m_ffn_w2)
    V = dict(norm_pre_mix=v_norm_pre_mix, norm_post_mix=v_norm_post_mix, norm_pre_ffn=v_norm_pre_ffn,
             norm_post_ffn=v_norm_post_ffn, ab_w_in=v_ab_w_in, sgu_ln_g=v_sgu_ln_g, sgu_ln_b=v_sgu_ln_b,
             sgu_w=v_sgu_w, sgu_b=v_sgu_b, ab_w_out=v_ab_w_out, sb_w_in=v_sb_w_in, sb_w_out=v_sb_w_out,
             ffn_w1=v_ffn_w1, ffn_w2=v_ffn_w2)

    shards = {"ab_in": (ab_w_in, 0), "ab_out": (ab_w_out, 0), "w1_0": (ffn_w1, 0), "w2_0": (ffn_w2, 0),
              "sb_in": (sb_w_in, 0), "sb_out": (sb_w_out, 0), "w1_1": (ffn_w1, 1), "w2_1": (ffn_w2, 1)}
    comm = _Exchange(shards)
    norms = (norm_pre_mix, norm_post_mix, norm_pre_ffn, norm_post_ffn)
    sgu = (sgu_ln_g, sgu_ln_b, sgu_w[0], sgu_b[0])
    loss, dx, small = _local_step(x[0], loss_target[0], norms, sgu, comm)

    out = {}

    def update(name, layers, landed):
        out[name] = _adamw(W[name], [landed[n] for n in layers], KIND[layers[0]], comm.where, M[name], V[name],
                           f"adamw_{name}")

    landed = comm.settle(("ffn1", "sb", "ffn0"), after=dx)
    for name, layers in (("sb_w_in", ["sb_in"]), ("sb_w_out", ["sb_out"]), ("ffn_w1", ["w1_0", "w1_1"]),
                         ("ffn_w2", ["w2_0", "w2_1"])):
        update(name, layers, landed)
    small_g = [small["pre_mix"], small["post_mix"], small["pre_ffn"], small["post_ffn"], small["ln_g"],
               small["ln_b"], small["w_s"][None], small["b_s"][None]]
    summed = _all_reduce_small(_pack(small_g + [loss.reshape(1)]), out["ffn_w2"][3])
    g_small, loss = summed[:-8], summed[-8, 0]
    landed = comm.settle(("ab_out", "ab_in"), after=g_small)
    update("ab_w_out", ["ab_out"], landed)
    update("ab_w_in", ["ab_in"], landed)
    res = _adamw_small(_pack([W[n] for n in SMALL]), g_small, _pack([M[n] for n in SMALL]),
                       _pack([V[n] for n in SMALL]))
    like = [W[n] for n in SMALL]
    for n, *vals in zip(SMALL, *[_unpack(r, like) for r in [g_small] + list(res)]):
        out[n] = vals

    return (loss, dx[None], *[out[n][0] for n in ORDER], *[out[n][1] for n in ORDER],
            *[out[n][2] for n in ORDER], *[out[n][3] for n in ORDER])
```

```python
import math

import jax
import jax.numpy as jnp
from jax import lax
from jax.experimental import pallas as pl
from jax.experimental.pallas import tpu as pltpu

F32 = jnp.float32
BF16 = jnp.bfloat16

HEAD_DIM = 128
CHUNK = 128
ATT_BLOCK = 128
DILATED_PAIRS = ((128, 1), (512, 4), (2048, 16))
RMS_EPS = 1e-6
LN_EPS = 1e-5
ADAM_LR = 0.001
ADAM_B1 = 0.9
ADAM_B2 = 0.999
ADAM_EPS = 1e-08
ADAM_WD = 0.01
ADAM_STEP = 10
N_DEV = 8
MASKED = -1e30

V7X_VMEM_BYTES = 64 * 1024 * 1024
VMEM_LIMIT = V7X_VMEM_BYTES - 8 * 1024 * 1024

NN = (((1,), (0,)), ((), ()))
NT = (((1,), (1,)), ((), ()))
TN = (((0,), (0,)), ((), ()))


def _params(*sem):
    return pltpu.CompilerParams(dimension_semantics=sem, vmem_limit_bytes=VMEM_LIMIT)


def _dot(a, b, dims=NN):
    return lax.dot_general(a, b, dims, preferred_element_type=F32)


def _tile(n, preferred):
    if n <= preferred:
        return n
    t = preferred - preferred % 128
    while n % t:
        t -= 128
    assert t > 0, (n, preferred)
    return t


def _matmul(a, b, *, mode, name, out_dtype=F32, tm=1024, tn=1024, tk=2048, epi=None, extras=(), after=()):
    if mode == "nn":
        (M, K), N = a.shape, b.shape[1]
    elif mode == "nt":
        (M, K), N = a.shape, b.shape[0]
    else:
        (K, M), N = a.shape, b.shape[1]
    tm, tn, tk = _tile(M, tm), _tile(N, tn), _tile(K, tk)
    nk = K // tk
    if mode == "tn":
        a_spec = pl.BlockSpec((tk, tm), lambda i, j, k: (k, i))
    else:
        a_spec = pl.BlockSpec((tm, tk), lambda i, j, k: (i, k))
    if mode == "nt":
        b_spec = pl.BlockSpec((tn, tk), lambda i, j, k: (j, k))
    else:
        b_spec = pl.BlockSpec((tk, tn), lambda i, j, k: (k, j))
    o_spec = pl.BlockSpec((tm, tn), lambda i, j, k: (i, j))
    dims = {"nn": NN, "nt": NT, "tn": TN}[mode]
    n_extra = len(extras)
    n_in = n_extra + len(after)
    several = isinstance(out_dtype, tuple)
    n_out = len(out_dtype) if several else 1

    def finish(acc, rest):
        outs = acc if epi is None else epi(acc, pl.program_id(1), *[r[...] for r in rest[:n_extra]])
        for o_ref, o in zip(rest[n_in:n_in + n_out], outs if several else (outs,)):
            o_ref[...] = o.astype(o_ref.dtype)

    if nk == 1:
        def body(a_ref, b_ref, *rest):
            finish(_dot(a_ref[...], b_ref[...], dims), rest)
        scratch = []
    else:
        def body(a_ref, b_ref, *rest):
            acc_ref = rest[n_in + n_out]
            k = pl.program_id(2)

            @pl.when(k == 0)
            def _():
                acc_ref[...] = jnp.zeros_like(acc_ref)

            acc_ref[...] += _dot(a_ref[...], b_ref[...], dims)

            @pl.when(k == nk - 1)
            def _():
                finish(acc_ref[...], rest)
        scratch = [pltpu.VMEM((tm, tn), F32)]

    shapes = [jax.ShapeDtypeStruct((M, N), d) for d in (out_dtype if several else (out_dtype,))]
    return pl.pallas_call(
        body,
        name=name,
        grid=(M // tm, N // tn, nk),
        in_specs=[a_spec, b_spec] + [o_spec] * n_extra + [ANY] * len(after),
        out_specs=[o_spec] * n_out if several else o_spec,
        out_shape=shapes if several else shapes[0],
        scratch_shapes=scratch,
        compiler_params=_params("parallel", "parallel", "arbitrary"),
    )(a, b, *extras, *after)


ROWS = 512


def _rms(x):
    return lax.rsqrt(jnp.mean(x * x, axis=-1, keepdims=True) + RMS_EPS)


def _prenorm(x, g, name):
    T, D = x.shape

    def body(x_ref, g_ref, h_ref):
        xv = x_ref[...]
        h_ref[...] = (xv * _rms(xv) * g_ref[...]).astype(BF16)

    row = pl.BlockSpec((ROWS, D), lambda i: (i, 0))
    vec = pl.BlockSpec((1, D), lambda i: (0, 0))
    return pl.pallas_call(
        body, name=name, grid=(T // ROWS,), in_specs=[row, vec], out_specs=row,
        out_shape=jax.ShapeDtypeStruct((T, D), BF16), compiler_params=_params("parallel"),
    )(x, g)


def _postnorm_prenorm(x, y, g_post, g_pre, name):
    T, D = x.shape

    def body(x_ref, y_ref, gp_ref, gn_ref, xo_ref, h_ref):
        yv = y_ref[...]
        xn = x_ref[...] + yv * _rms(yv) * gp_ref[...]
        xo_ref[...] = xn
        h_ref[...] = (xn * _rms(xn) * gn_ref[...]).astype(BF16)

    row = pl.BlockSpec((ROWS, D), lambda i: (i, 0))
    vec = pl.BlockSpec((1, D), lambda i: (0, 0))
    return pl.pallas_call(
        body, name=name, grid=(T // ROWS,), in_specs=[row, row, vec, vec], out_specs=[row, row],
        out_shape=[jax.ShapeDtypeStruct((T, D), F32), jax.ShapeDtypeStruct((T, D), BF16)],
        compiler_params=_params("parallel"),
    )(x, y, g_post, g_pre)


def _postnorm_grads(dn, yh, r, g):
    gd = dn * g
    return r * (gd - yh * jnp.mean(yh * gd, axis=-1, keepdims=True)), dn * yh


def _postnorm_loss(x, y, g_post, target, name):
    T, D = x.shape

    def body(x_ref, y_ref, gp_ref, t_ref, loss_ref, dx_ref, dy_ref, dg_ref):
        @pl.when(pl.program_id(0) == 0)
        def _():
            loss_ref[...] = jnp.zeros_like(loss_ref)
            dg_ref[...] = jnp.zeros_like(dg_ref)

        yv = y_ref[...]
        r = _rms(yv)
        yh = yv * r
        err = x_ref[...] + yh * gp_ref[...] - t_ref[...]
        dx = err * (1.0 / D)
        dx_ref[...] = dx
        loss_ref[...] += 0.5 * jnp.sum(jnp.sum(err * err, axis=-1, keepdims=True) * (1.0 / D))
        dy, dg = _postnorm_grads(dx, yh, r, gp_ref[...])
        dy_ref[...] = dy.astype(BF16)
        dg_ref[...] += jnp.sum(dg, axis=0, keepdims=True)

    row = pl.BlockSpec((ROWS, D), lambda i: (i, 0))
    vec = pl.BlockSpec((1, D), lambda i: (0, 0))
    acc = pl.BlockSpec((8, 128), lambda i: (0, 0))
    return pl.pallas_call(
        body, name=name, grid=(T // ROWS,), in_specs=[row, row, vec, row], out_specs=[acc, row, row, vec],
        out_shape=[jax.ShapeDtypeStruct((8, 128), F32), jax.ShapeDtypeStruct((T, D), F32),
                   jax.ShapeDtypeStruct((T, D), BF16), jax.ShapeDtypeStruct((1, D), F32)],
        compiler_params=_params("arbitrary"),
    )(x, y, g_post, target)


def _norm_bwd_pair(dx_out, dh, x, g_pre, y_prev, g_post_prev, name):
    T, D = x.shape

    def body(dxo_ref, dh_ref, x_ref, g_ref, y_ref, gp_ref, dx_ref, dg_ref, dy_ref, dgp_ref):
        @pl.when(pl.program_id(0) == 0)
        def _():
            dg_ref[...] = jnp.zeros_like(dg_ref)
            dgp_ref[...] = jnp.zeros_like(dgp_ref)

        xv, dhv = x_ref[...], dh_ref[...]
        r = _rms(xv)
        xh = xv * r
        gd = dhv * g_ref[...]
        dx = dxo_ref[...] + r * (gd - xh * jnp.mean(xh * gd, axis=-1, keepdims=True))
        dx_ref[...] = dx
        dg_ref[...] += jnp.sum(dhv * xh, axis=0, keepdims=True)
        yv = y_ref[...]
        ry = _rms(yv)
        dy, dgp = _postnorm_grads(dx, yv * ry, ry, gp_ref[...])
        dy_ref[...] = dy.astype(BF16)
        dgp_ref[...] += jnp.sum(dgp, axis=0, keepdims=True)

    row = pl.BlockSpec((ROWS, D), lambda i: (i, 0))
    vec = pl.BlockSpec((1, D), lambda i: (0, 0))
    return pl.pallas_call(
        body, name=name, grid=(T // ROWS,), in_specs=[row, row, row, vec, row, vec],
        out_specs=[row, vec, row, vec],
        out_shape=[jax.ShapeDtypeStruct((T, D), F32), jax.ShapeDtypeStruct((1, D), F32),
                   jax.ShapeDtypeStruct((T, D), BF16), jax.ShapeDtypeStruct((1, D), F32)],
        compiler_params=_params("arbitrary"),
    )(dx_out, dh, x, g_pre, y_prev, g_post_prev)


def _prenorm_bwd(dx_out, dh, x, g_pre, name):
    T, D = x.shape

    def body(dxo_ref, dh_ref, x_ref, g_ref, dx_ref, dg_ref):
        @pl.when(pl.program_id(0) == 0)
        def _():
            dg_ref[...] = jnp.zeros_like(dg_ref)

        xv, dhv = x_ref[...], dh_ref[...]
        r = _rms(xv)
        xh = xv * r
        gd = dhv * g_ref[...]
        dx_ref[...] = dxo_ref[...] + r * (gd - xh * jnp.mean(xh * gd, axis=-1, keepdims=True))
        dg_ref[...] += jnp.sum(dhv * xh, axis=0, keepdims=True)

    row = pl.BlockSpec((ROWS, D), lambda i: (i, 0))
    vec = pl.BlockSpec((1, D), lambda i: (0, 0))
    return pl.pallas_call(
        body, name=name, grid=(T // ROWS,), in_specs=[row, row, row, vec], out_specs=[row, vec],
        out_shape=[jax.ShapeDtypeStruct((T, D), F32), jax.ShapeDtypeStruct((1, D), F32)],
        compiler_params=_params("arbitrary"),
    )(dx_out, dh, x, g_pre)


_INV_SQRT2 = 1.0 / math.sqrt(2.0)
_INV_SQRT2PI = 1.0 / math.sqrt(2.0 * math.pi)


def _gelu(x):
    return 0.5 * x * (1.0 + lax.erf(x * _INV_SQRT2))


def _gelu_grad(x):
    return 0.5 * (1.0 + lax.erf(x * _INV_SQRT2)) + x * jnp.exp(-0.5 * x * x) * _INV_SQRT2PI


def _layernorm_stats(x):
    mu = jnp.mean(x, axis=-1, keepdims=True)
    xc = x - mu
    rstd = lax.rsqrt(jnp.mean(xc * xc, axis=-1, keepdims=True) + LN_EPS)
    return xc * rstd, rstd


def _tril_mask():
    i = lax.broadcasted_iota(jnp.int32, (CHUNK, CHUNK), 0)
    j = lax.broadcasted_iota(jnp.int32, (CHUNK, CHUNK), 1)
    return j <= i


SGU_ROWS = 512


def _sgu_fwd(z, ln_g, ln_b, w_s, b_t, name):
    T = z.shape[0]
    A = ln_g.shape[1]
    G = A // 128
    rows = min(SGU_ROWS, T)

    def body(u_ref, v_ref, g_ref, b_ref, w_ref, bt_ref, o_ref):
        mask = _tril_mask()
        for c in range(rows // CHUNK):
            rs = pl.ds(c * CHUNK, CHUNK)
            xh, _ = _layernorm_stats(_gelu(v_ref[rs, :]))
            vn = (xh * g_ref[...] + b_ref[...]).astype(BF16)
            for g in range(G):
                cs = pl.ds(g * 128, 128)
                w = jnp.where(mask, w_ref[g], 0.0).astype(BF16)
                mixed = _dot(w, vn[:, g * 128:(g + 1) * 128]) + bt_ref[:, g:g + 1]
                o_ref[rs, cs] = (_gelu(u_ref[rs, cs]) * mixed).astype(BF16)

    return pl.pallas_call(
        body, name=name, grid=(T // rows,),
        in_specs=[
            pl.BlockSpec((rows, A), lambda i: (i, 0)),
            pl.BlockSpec((rows, A), lambda i: (i, 1)),
            pl.BlockSpec((1, A), lambda i: (0, 0)),
            pl.BlockSpec((1, A), lambda i: (0, 0)),
            pl.BlockSpec((G, CHUNK, CHUNK), lambda i: (0, 0, 0)),
            pl.BlockSpec((CHUNK, G), lambda i: (0, 0)),
        ],
        out_specs=pl.BlockSpec((rows, A), lambda i: (i, 0)),
        out_shape=jax.ShapeDtypeStruct((T, 2 * A), BF16),
        compiler_params=_params("parallel"),
    )(z, z, ln_g, ln_b, w_s, b_t)


def _sgu_bwd(z, dcat, ln_g, ln_b, w_s, b_t, name):
    T = z.shape[0]
    A = ln_g.shape[1]
    G = A // 128
    rows = min(SGU_ROWS, T)

    def body(u_ref, v_ref, da_ref, g_ref, b_ref, w_ref, bt_ref, dz_ref, dg_ref, db_ref, dw_ref, dbt_ref, dvn_ref):
        @pl.when(pl.program_id(0) == 0)
        def _():
            dg_ref[...] = jnp.zeros_like(dg_ref)
            db_ref[...] = jnp.zeros_like(db_ref)
            dw_ref[...] = jnp.zeros_like(dw_ref)
            dbt_ref[...] = jnp.zeros_like(dbt_ref)

        mask = _tril_mask()
        for c in range(rows // CHUNK):
            rs = pl.ds(c * CHUNK, CHUNK)
            vv = v_ref[rs, :]
            gv = _gelu(vv)
            xh, rstd = _layernorm_stats(gv)
            vn = (xh * g_ref[...] + b_ref[...]).astype(BF16)
            for g in range(G):
                cs = pl.ds(g * 128, 128)
                w = jnp.where(mask, w_ref[g], 0.0).astype(BF16)
                vg = vn[:, g * 128:(g + 1) * 128]
                mixed = _dot(w, vg) + bt_ref[:, g:g + 1]
                uu = u_ref[rs, cs]
                da = da_ref[rs, cs]
                dz_ref[rs, cs] = (da * mixed * _gelu_grad(uu)).astype(BF16)
                dm = da * _gelu(uu)
                dmb = dm.astype(BF16)
                dbt_ref[:, g:g + 1] += jnp.sum(dm, axis=1, keepdims=True)
                dw_ref[g] += jnp.where(mask, _dot(dmb, vg, NT), 0.0)
                dvn_ref[:, cs] = _dot(w, dmb, TN)
            dvn = dvn_ref[...]
            dg_ref[...] += jnp.sum(dvn * xh, axis=0, keepdims=True)
            db_ref[...] += jnp.sum(dvn, axis=0, keepdims=True)
            dxh = dvn * g_ref[...]
            dgv = rstd * (dxh - jnp.mean(dxh, axis=-1, keepdims=True)
                          - xh * jnp.mean(dxh * xh, axis=-1, keepdims=True))
            dz_ref[rs, pl.ds(A, A)] = (dgv * _gelu_grad(vv)).astype(BF16)

    vec = pl.BlockSpec((1, A), lambda i: (0, 0))
    wsp = pl.BlockSpec((G, CHUNK, CHUNK), lambda i: (0, 0, 0))
    bsp = pl.BlockSpec((CHUNK, G), lambda i: (0, 0))
    return pl.pallas_call(
        body, name=name, grid=(T // rows,),
        in_specs=[
            pl.BlockSpec((rows, A), lambda i: (i, 0)),
            pl.BlockSpec((rows, A), lambda i: (i, 1)),
            pl.BlockSpec((rows, A), lambda i: (i, 0)),
            vec, vec, wsp, bsp,
        ],
        out_specs=[pl.BlockSpec((rows, 2 * A), lambda i: (i, 0)), vec, vec, wsp, bsp],
        out_shape=[
            jax.ShapeDtypeStruct((T, 2 * A), BF16),
            jax.ShapeDtypeStruct((1, A), F32),
            jax.ShapeDtypeStruct((1, A), F32),
            jax.ShapeDtypeStruct((G, CHUNK, CHUNK), F32),
            jax.ShapeDtypeStruct((CHUNK, G), F32),
        ],
        scratch_shapes=[pltpu.VMEM((CHUNK, A), F32)],
        compiler_params=_params("arbitrary"),
    )(z, z, dcat, ln_g, ln_b, w_s, b_t)


def _alibi_row(B, d):
    H = B // HEAD_DIM
    slopes = [d * 2.0 ** (-8.0 * (h + 1.0) / H) for h in range(H)]
    return jnp.repeat(jnp.asarray(slopes, F32), HEAD_DIM)[None, :]


def _dil_scores(q, k, slope_d, valid, dist):
    s = _dot(q, k, NT) - slope_d * dist
    return jnp.where(valid, s, MASKED)


def _dil_block(n, r, d):
    if d == 1:
        return pl.ds(pl.multiple_of(n * ATT_BLOCK, ATT_BLOCK), ATT_BLOCK)
    return pl.ds(n * (d * ATT_BLOCK) + r, ATT_BLOCK, stride=d)


def _dilated_forward(z, cat, B, name):
    T = z.shape[0]
    H = B // HEAD_DIM
    A = cat.shape[1] - B
    scale = HEAD_DIM ** -0.5
    blk = ATT_BLOCK
    chunk = _tile(T, 512)

    def body(q_ref, k_ref, v_ref, sl_ref, cat_in, cat_ref, of_ref, lt_ref, *branch):
        o_refs, l_refs = branch[:3], branch[3:]
        slope = sl_ref[:, :1]
        qi = lax.broadcasted_iota(jnp.int32, (blk, 2 * blk), 0)
        kj = lax.broadcasted_iota(jnp.int32, (blk, 2 * blk), 1)
        dist = qi + blk - kj
        band = (dist >= 0) & (dist <= blk)
        distf = dist.astype(F32)

        for b, (_, d) in enumerate(DILATED_PAIRS):
            def one(n, r, b=b, d=d):
                rows, prev = _dil_block(n, r, d), _dil_block(jnp.maximum(n - 1, 0), r, d)
                q = (q_ref[rows, :] * scale).astype(BF16)
                k = jnp.concatenate([k_ref[prev, :], k_ref[rows, :]], axis=0).astype(BF16)
                v = jnp.concatenate([v_ref[prev, :], v_ref[rows, :]], axis=0).astype(BF16)
                s = _dil_scores(q, k, slope * float(d), band & ((kj >= blk) | (n > 0)), distf)
                m = jnp.max(s, axis=-1, keepdims=True)
                p = jnp.exp(s - m)
                den = jnp.sum(p, axis=-1, keepdims=True)
                o_refs[b][rows, :] = _dot(p.astype(BF16), v) / den
                l_refs[b][rows, :] = jnp.broadcast_to(m + jnp.log(den), (blk, HEAD_DIM))

            per = max(1, 4 // d)

            def step(i, _, d=d, per=per, one=one):
                for u in range(per):
                    for r in range(d):
                        one(i * per + u, r)
                return 0

            lax.fori_loop(0, T // (d * blk * per), step, 0)

        def merge(i, _):
            rs = pl.ds(pl.multiple_of(i * chunk, chunk), chunk)
            a, b, c = l_refs[0][rs, :], l_refs[1][rs, :], l_refs[2][rs, :]
            m = jnp.maximum(jnp.maximum(a, b), c)
            ea, eb, ec = jnp.exp(a - m), jnp.exp(b - m), jnp.exp(c - m)
            tot = ea + eb + ec
            o = (ea * o_refs[0][rs, :] + eb * o_refs[1][rs, :] + ec * o_refs[2][rs, :]) / tot
            of_ref[rs, :] = o
            cat_ref[rs, :] = o.astype(BF16)
            lt_ref[rs, :] = m + jnp.log(tot)
            return 0

        lax.fori_loop(0, T // chunk, merge, 0)

    def col(unit):
        return lambda h: (0, unit * H + h)

    seq = (T, HEAD_DIM)
    out = pl.BlockSpec(seq, lambda h: (0, h))
    return pl.pallas_call(
        body, name=name, grid=(H,),
        in_specs=[pl.BlockSpec(seq, col(2)), pl.BlockSpec(seq, col(3)), pl.BlockSpec(seq, col(4)),
                  pl.BlockSpec((1, HEAD_DIM), lambda h: (0, h)), ANY],
        out_specs=[pl.BlockSpec(seq, lambda h: (0, A // HEAD_DIM + h)), out, out],
        out_shape=[jax.ShapeDtypeStruct(cat.shape, BF16), jax.ShapeDtypeStruct((T, B), F32),
                   jax.ShapeDtypeStruct((T, B), F32)],
        input_output_aliases={4: 0},
        scratch_shapes=[pltpu.VMEM(seq, F32)] * 6,
        compiler_params=_params("parallel"),
    )(z, z, z, _alibi_row(B, 1), cat)


def _dilated_backward(z, dcat, o, lse, B, name):
    T = z.shape[0]
    H = B // HEAD_DIM
    scale = HEAD_DIM ** -0.5
    blk = ATT_BLOCK
    chunk = _tile(T, 512)

    def body(q_ref, k_ref, v_ref, do_ref, o_ref, l_ref, sl_ref, dq_ref, dk_ref, dv_ref, aq_ref, ak_ref, av_ref):
        slope = sl_ref[:, :1]
        qi = lax.broadcasted_iota(jnp.int32, (blk, 2 * blk), 0)
        kj = lax.broadcasted_iota(jnp.int32, (blk, 2 * blk), 1)
        dist = qi + blk - kj
        band = (dist >= 0) & (dist <= blk)
        distf = dist.astype(F32)
        for acc in (aq_ref, ak_ref, av_ref):
            acc[...] = jnp.zeros_like(acc)

        for _, d in DILATED_PAIRS:
            def one(n, r, d=d):
                rows, prev = _dil_block(n, r, d), _dil_block(jnp.maximum(n - 1, 0), r, d)
                q = (q_ref[rows, :] * scale).astype(BF16)
                k = jnp.concatenate([k_ref[prev, :], k_ref[rows, :]], axis=0).astype(BF16)
                v = jnp.concatenate([v_ref[prev, :], v_ref[rows, :]], axis=0).astype(BF16)
                do = do_ref[rows, :]
                delta = jnp.sum(do * o_ref[rows, :], axis=-1, keepdims=True)
                do = do.astype(BF16)
                s = _dil_scores(q, k, slope * float(d), band & ((kj >= blk) | (n > 0)), distf)
                p = jnp.exp(s - l_ref[rows, :][:, :1])
                ds = (p * (_dot(do, v, NT) - delta)).astype(BF16)
                aq_ref[rows, :] += _dot(ds, k)
                dk = _dot(ds, q, TN)
                dv = _dot(p.astype(BF16), do, TN)
                ak_ref[prev, :] += dk[:blk]
                av_ref[prev, :] += dv[:blk]
                ak_ref[rows, :] += dk[blk:]
                av_ref[rows, :] += dv[blk:]

            per = max(1, 4 // d)

            def step(i, _, d=d, per=per, one=one):
                for u in range(per):
                    for r in range(d):
                        one(i * per + u, r)
                return 0

            lax.fori_loop(0, T // (d * blk * per), step, 0)

        def emit(i, _):
            rs = pl.ds(pl.multiple_of(i * chunk, chunk), chunk)
            dq_ref[rs, :] = (aq_ref[rs, :] * scale).astype(BF16)
            dk_ref[rs, :] = ak_ref[rs, :].astype(BF16)
            dv_ref[rs, :] = av_ref[rs, :].astype(BF16)
            return 0

        lax.fori_loop(0, T // chunk, emit, 0)

    def col(unit):
        return lambda h: (0, unit * H + h)

    seq = (T, HEAD_DIM)
    own = pl.BlockSpec(seq, lambda h: (0, h))
    return pl.pallas_call(
        body, name=name, grid=(H,),
        in_specs=[pl.BlockSpec(seq, col(2)), pl.BlockSpec(seq, col(3)), pl.BlockSpec(seq, col(4)),
                  pl.BlockSpec(seq, col(1)), own, own, pl.BlockSpec((1, HEAD_DIM), lambda h: (0, h))],
        out_specs=[own] * 3,
        out_shape=[jax.ShapeDtypeStruct((T, B), BF16)] * 3,
        scratch_shapes=[pltpu.VMEM(seq, F32)] * 3,
        compiler_params=_params("parallel"),
    )(z, z, z, dcat, o, lse, _alibi_row(B, 1))


def _join_columns(parts, name):
    T = parts[0].shape[0]
    widths = [p.shape[1] for p in parts]

    def body(*refs):
        o_ref, at = refs[-1], 0
        for ref, w in zip(refs[:-1], widths):
            o_ref[:, pl.ds(at, w)] = ref[...]
            at += w

    return pl.pallas_call(
        body, name=name, grid=(T // ROWS,),
        in_specs=[pl.BlockSpec((ROWS, w), lambda i: (i, 0)) for w in widths],
        out_specs=pl.BlockSpec((ROWS, sum(widths)), lambda i: (i, 0)),
        out_shape=jax.ShapeDtypeStruct((T, sum(widths)), BF16), compiler_params=_params("parallel"),
    )(*parts)


SB_QUERY_ROWS = 1024
SB_KEYS = 2 * ATT_BLOCK


def _tri_and_ones(pred):
    rows = lax.broadcasted_iota(jnp.int32, (2 * ATT_BLOCK, 2 * ATT_BLOCK), 0) % ATT_BLOCK
    cols = lax.broadcasted_iota(jnp.int32, (2 * ATT_BLOCK, 2 * ATT_BLOCK), 1)
    return ((cols >= ATT_BLOCK) | pred(rows, cols)).astype(BF16)


def _running(x, tri):
    hi = x.astype(BF16)
    lo = (x - hi.astype(F32)).astype(BF16)
    return _dot(jnp.concatenate([hi, lo], axis=1), tri)


def _sb_mask(query_rows, s):
    rows = lax.broadcasted_iota(jnp.int32, (query_rows - s * SB_KEYS, SB_KEYS), 0)
    cols = lax.broadcasted_iota(jnp.int32, (query_rows - s * SB_KEYS, SB_KEYS), 1)
    return cols < rows


def _log_sigmoids(z):
    ls = jnp.minimum(z, 0.0) - jnp.log(1.0 + jnp.exp(-jnp.abs(z)))
    return ls, ls - z


def _sb_fwd(qkv, W, name):
    T = qkv.shape[0]
    H = W // HEAD_DIM
    blk = ATT_BLOCK
    qb = min(SB_QUERY_ROWS, T)
    per = qb // SB_KEYS

    def body(q_ref, k_ref, v_ref, o_ref, lt_ref, acc_ref):
        i = pl.program_id(1)
        tri = _tri_and_ones(lambda r, c: r > c)
        lt_ref[...] = jnp.zeros_like(lt_ref)
        acc_ref[...] = jnp.zeros_like(acc_ref)

        def tile(j, mask, r0=0):
            ks = pl.ds(pl.multiple_of(j * SB_KEYS, SB_KEYS), SB_KEYS)
            qs = pl.ds(r0, qb - r0)
            z = _dot(q_ref[qs, :], k_ref[ks, :], NT)
            ls, lm = _log_sigmoids(z)
            if mask is not None:
                lm = jnp.where(mask, lm, 0.0)
            later = lt_ref[qs, :]
            second = _running(lm[:, blk:], tri)
            first = _running(lm[:, :blk], tri)
            after_first = later + second[:, blk:]
            a = jnp.exp(ls + jnp.concatenate([first[:, :blk] + after_first, second[:, :blk] + later], axis=1))
            if mask is not None:
                a = jnp.where(mask, a, 0.0)
            acc_ref[qs, :] += _dot(a.astype(BF16), v_ref[ks, :])
            lt_ref[qs, :] = after_first + first[:, blk:]

        for s in reversed(range(per)):
            tile(i * per + s, _sb_mask(qb, s), s * SB_KEYS)

        def step(jj, _):
            for s in range(per):
                tile((i - jj) * per - 1 - s, None)
            return 0

        lax.fori_loop(0, i, step, 0)
        o_ref[...] = acc_ref[...].astype(BF16)

    qs = pl.BlockSpec((qb, HEAD_DIM), lambda h, i: (i, h))
    return pl.pallas_call(
        body, name=name, grid=(H, T // qb),
        in_specs=[qs, pl.BlockSpec((T, HEAD_DIM), lambda h, i: (0, H + h)),
                  pl.BlockSpec((T, HEAD_DIM), lambda h, i: (0, 2 * H + h))],
        out_specs=[qs, qs],
        out_shape=[jax.ShapeDtypeStruct((T, W), BF16), jax.ShapeDtypeStruct((T, W), F32)],
        scratch_shapes=[pltpu.VMEM((qb, HEAD_DIM), F32)],
        compiler_params=_params("parallel", "arbitrary"),
    )(qkv, qkv, qkv)


def _sb_bwd(qkv, do, ltot, W, name):
    T = qkv.shape[0]
    H = W // HEAD_DIM
    blk = ATT_BLOCK
    nkb = T // SB_KEYS
    qb = min(SB_QUERY_ROWS, T)
    per = qb // SB_KEYS

    def body(q_ref, k_ref, v_ref, do_ref, lt_ref, dq_ref, dkt_ref, dvt_ref, qt_ref, dot_ref, plm_ref, pg_ref):
        i = pl.program_id(1)

        @pl.when(i == 0)
        def _():
            dkt_ref[...] = jnp.zeros_like(dkt_ref)
            dvt_ref[...] = jnp.zeros_like(dvt_ref)

        qt_ref[...] = q_ref[...].astype(F32).T.astype(BF16)
        dot_ref[...] = do_ref[...].astype(F32).T.astype(BF16)
        upto = _tri_and_ones(lambda r, c: r <= c)
        before = _tri_and_ones(lambda r, c: r < c)
        plm_ref[...] = jnp.zeros_like(plm_ref)
        pg_ref[...] = jnp.zeros_like(pg_ref)
        dq_ref[...] = jnp.zeros_like(dq_ref)

        def tile(j, mask, r0=0):
            ks = pl.ds(pl.multiple_of(j * SB_KEYS, SB_KEYS), SB_KEYS)
            qs = pl.ds(r0, qb - r0)
            k = k_ref[ks, :]
            v = v_ref[ks, :]
            z = _dot(q_ref[qs, :], k, NT)
            ls, lm = _log_sigmoids(z)
            nsig = jnp.exp(lm)
            if mask is not None:
                lm = jnp.where(mask, lm, 0.0)
            earlier = plm_ref[qs, :]
            first = _running(lm[:, :blk], upto)
            second = _running(lm[:, blk:], upto)
            upto_first = earlier + first[:, blk:]
            seen = jnp.concatenate([first[:, :blk] + earlier, second[:, :blk] + upto_first], axis=1)
            ltot = lt_ref[qs, :]
            a = jnp.exp(ls + (jnp.concatenate([ltot, ltot], axis=1) - seen))
            if mask is not None:
                a = jnp.where(mask, a, 0.0)
            g = a * _dot(do_ref[qs, :], v, NT)
            g_earlier = pg_ref[qs, :]
            g_first = _running(g[:, :blk], before)
            g_second = _running(g[:, blk:], before)
            g_upto_first = g_earlier + g_first[:, blk:]
            gsum = jnp.concatenate([g_first[:, :blk] + g_earlier, g_second[:, :blk] + g_upto_first], axis=1)
            dz = g * nsig - gsum * jnp.exp(ls)
            if mask is not None:
                dz = jnp.where(mask, dz, 0.0)
            dzb = dz.astype(BF16)
            dkt_ref[j] += _dot(qt_ref[:, qs], dzb)
            dvt_ref[j] += _dot(dot_ref[:, qs], a.astype(BF16))
            dq_ref[qs, :] += _dot(dzb, k)
            plm_ref[qs, :] = upto_first + second[:, blk:]
            pg_ref[qs, :] = g_upto_first + g_second[:, blk:]

        def step(jj, _):
            for s in range(per):
                tile(jj * per + s, None)
            return 0

        lax.fori_loop(0, i, step, 0)
        for s in range(per):
            tile(i * per + s, _sb_mask(qb, s), s * SB_KEYS)

    qs = pl.BlockSpec((qb, HEAD_DIM), lambda h, i: (i, h))
    res = pl.BlockSpec((None, nkb, HEAD_DIM, SB_KEYS), lambda h, i: (h, 0, 0, 0))
    return pl.pallas_call(
        body, name=name, grid=(H, T // qb),
        in_specs=[qs, pl.BlockSpec((T, HEAD_DIM), lambda h, i: (0, H + h)),
                  pl.BlockSpec((T, HEAD_DIM), lambda h, i: (0, 2 * H + h)), qs, qs],
        out_specs=[qs, res, res],
        out_shape=[jax.ShapeDtypeStruct((T, W), F32)] + [jax.ShapeDtypeStruct((H, nkb, HEAD_DIM, SB_KEYS), F32)] * 2,
        scratch_shapes=[pltpu.VMEM((HEAD_DIM, qb), BF16), pltpu.VMEM((HEAD_DIM, qb), BF16),
                        pltpu.VMEM((qb, HEAD_DIM), F32), pltpu.VMEM((qb, HEAD_DIM), F32)],
        compiler_params=_params("parallel", "arbitrary"),
    )(qkv, qkv, qkv, do, ltot)


def _sb_pack(dq, dkt, dvt, name):
    T, W = dq.shape
    H = W // HEAD_DIM
    blk = SB_KEYS
    scale = HEAD_DIM ** -0.5

    def body(q_ref, kt_ref, vt_ref, o_ref):
        o_ref[:, pl.ds(0, W)] = (q_ref[...] * scale).astype(BF16)
        for h in range(H):
            o_ref[:, pl.ds(W + h * HEAD_DIM, HEAD_DIM)] = kt_ref[h].T.astype(BF16)
            o_ref[:, pl.ds(2 * W + h * HEAD_DIM, HEAD_DIM)] = vt_ref[h].T.astype(BF16)

    tr = pl.BlockSpec((H, None, HEAD_DIM, blk), lambda i: (0, i, 0, 0))
    return pl.pallas_call(
        body, name=name, grid=(T // blk,), in_specs=[pl.BlockSpec((blk, W), lambda i: (i, 0)), tr, tr],
        out_specs=pl.BlockSpec((blk, 3 * W), lambda i: (i, 0)),
        out_shape=jax.ShapeDtypeStruct((T, 3 * W), BF16), compiler_params=_params("parallel"),
    )(dq, dkt, dvt)


def _local_step(x, target, norms, sgu, comm):
    T, D = x.shape
    A = D // 2
    pre_mix, post_mix, pre_ffn, post_ffn = norms
    ln_g, ln_b, w_s, b_s = sgu
    b_t = b_s.T
    scale = HEAD_DIM ** -0.5

    def vec(p, layer):
        return comm.tie(p[layer:layer + 1])

    h0 = _prenorm(x, vec(pre_mix, 0), "prenorm0")
    comm.arrive("ab_in", after=h0)
    comm.land("ab_in")
    z = _matmul(h0, comm.weight("ab_in"), mode="nn", name="ab_in_fwd")
    comm.arrive("ab_out", after=z)
    cat = _sgu_fwd(z, ln_g, ln_b, w_s, b_t, "sgu_fwd")
    cat, o_dil, lse_dil = _dilated_forward(z, cat, A, "dilated_fwd")
    comm.land("ab_out", after=o_dil)
    y0 = _matmul(cat, comm.weight("ab_out"), mode="nn", name="ab_out_fwd")
    comm.arrive("w1_0", after=y0)
    x1, h1 = _postnorm_prenorm(x, y0, vec(post_mix, 0), vec(pre_ffn, 0), "norm_mix0")
    comm.land("w1_0", after=h1)
    comm.arrive("w2_0", after=h1)

    def relu2(acc, j):
        r = jnp.maximum(acc, 0.0)
        return r * r, 2.0 * r

    f0, r0 = _matmul(h1, comm.weight("w1_0"), mode="nn", name="ffn0_w1_fwd", out_dtype=(BF16, BF16), epi=relu2)
    comm.land("w2_0", after=f0)
    y1 = _matmul(f0, comm.weight("w2_0"), mode="nn", name="ffn0_w2_fwd")
    comm.arrive("sb", after=y1)
    x2, h2 = _postnorm_prenorm(x1, y1, vec(post_ffn, 0), vec(pre_mix, 1), "norm_ffn0")
    comm.land("sb", after=h2)

    tn_qkv = _tile(D, 1024)
    nq = D // tn_qkv

    def scale_q(acc, j):
        return jnp.where(j < nq, acc * scale, acc)

    qkv = _matmul(h2, comm.weight("sb_in"), mode="nn", name="sb_in_fwd", out_dtype=BF16, tn=tn_qkv, epi=scale_q)
    comm.arrive("ffn1", after=qkv)
    o_sb, ltot = _sb_fwd(qkv, D, "sb_fwd")
    comm.land("ffn1", after=o_sb)
    y2 = _matmul(o_sb, comm.weight("sb_out"), mode="nn", name="sb_out_fwd")
    x3, h3 = _postnorm_prenorm(x2, y2, vec(post_mix, 1), vec(pre_ffn, 1), "norm_mix1")
    f1, r1 = _matmul(h3, comm.weight("w1_1"), mode="nn", name="ffn1_w1_fwd", out_dtype=(BF16, BF16), epi=relu2)
    y3 = _matmul(f1, comm.weight("w2_1"), mode="nn", name="ffn1_w2_fwd")
    loss_tile, dx4, dy3, dg_post_ffn1 = _postnorm_loss(x3, y3, vec(post_ffn, 1), target, "norm_loss")
    loss = loss_tile[0, 0]

    def relu2_bwd(acc, j, r):
        return acc * r.astype(F32)

    def ffn_bwd(dy, h, f, r, layer):
        g_w2 = _matmul(f, dy, mode="tn", name=f"ffn{layer}_w2_wgrad", out_dtype=BF16)
        da = _matmul(dy, comm.weight(f"w2_{layer}"), mode="nt", name=f"ffn{layer}_w2_dgrad", out_dtype=BF16,
                     epi=relu2_bwd, extras=(r,))
        g_w1 = _matmul(h, da, mode="tn", name=f"ffn{layer}_w1_wgrad", out_dtype=BF16)
        comm.reduce(f"ffn{layer}", {f"w2_{layer}": g_w2, f"w1_{layer}": g_w1})
        return _matmul(da, comm.weight(f"w1_{layer}"), mode="nt", name=f"ffn{layer}_w1_dgrad", after=comm.started())

    dh3 = ffn_bwd(dy3, h3, f1, r1, 1)
    dx3, dg_pre_ffn1, dy2, dg_post_mix1 = _norm_bwd_pair(dx4, dh3, x3, vec(pre_ffn, 1), y2, vec(post_mix, 1),
                                                         "ffn1_sb_norm_bwd")
    g_sb_out = _matmul(o_sb, dy2, mode="tn", name="sb_out_wgrad", out_dtype=BF16)
    do_sb = _matmul(dy2, comm.weight("sb_out"), mode="nt", name="sb_out_dgrad", out_dtype=BF16)
    dq, dk, dv = _sb_bwd(qkv, do_sb, ltot, D, "sb_bwd")
    dqkv = _sb_pack(dq, dk, dv, "sb_pack")
    g_sb_in = _matmul(h2, dqkv, mode="tn", name="sb_in_wgrad", out_dtype=BF16)
    comm.reduce("sb", {"sb_out": g_sb_out, "sb_in": g_sb_in})
    dh2 = _matmul(dqkv, comm.weight("sb_in"), mode="nt", name="sb_in_dgrad", after=comm.started())
    dx2, dg_pre_mix1, dy1, dg_post_ffn0 = _norm_bwd_pair(dx3, dh2, x2, vec(pre_mix, 1), y1, vec(post_ffn, 0),
                                                         "sb_ffn0_norm_bwd")
    dh1 = ffn_bwd(dy1, h1, f0, r0, 0)
    dx1, dg_pre_ffn0, dy0, dg_post_mix0 = _norm_bwd_pair(dx2, dh1, x1, vec(pre_ffn, 0), y0, vec(post_mix, 0),
                                                         "ffn0_ab_norm_bwd")
    g_ab_out = _matmul(cat, dy0, mode="tn", name="ab_out_wgrad", out_dtype=BF16)
    comm.reduce("ab_out", {"ab_out": g_ab_out})
    dcat = _matmul(dy0, comm.weight("ab_out"), mode="nt", name="ab_out_dgrad", after=comm.started())
    duv, d_ln_g, d_ln_b, d_w_s, d_b_t = _sgu_bwd(z, dcat, ln_g, ln_b, w_s, b_t, "sgu_bwd")
    dz = _join_columns([duv, *_dilated_backward(z, dcat, o_dil, lse_dil, A, "dilated_bwd")], "join_dz")
    g_ab_in = _matmul(h0, dz, mode="tn", name="ab_in_wgrad", out_dtype=BF16)
    comm.reduce("ab_in", {"ab_in": g_ab_in})
    dh0 = _matmul(dz, comm.weight("ab_in"), mode="nt", name="ab_in_dgrad", after=comm.started())
    dx0, dg_pre_mix0 = _prenorm_bwd(dx1, dh0, x, vec(pre_mix, 0), "ab_prenorm_bwd")

    small = {
        "pre_mix": jnp.concatenate([dg_pre_mix0, dg_pre_mix1], axis=0),
        "post_mix": jnp.concatenate([dg_post_mix0, dg_post_mix1], axis=0),
        "pre_ffn": jnp.concatenate([dg_pre_ffn0, dg_pre_ffn1], axis=0),
        "post_ffn": jnp.concatenate([dg_post_ffn0, dg_post_ffn1], axis=0),
        "ln_g": d_ln_g, "ln_b": d_ln_b, "w_s": d_w_s, "b_s": d_b_t.T,
    }
    return loss, dx0, small


MESH_ID = pl.DeviceIdType.MESH
ANY = pl.BlockSpec(memory_space=pl.ANY)


def _coords():
    return lax.axis_index("x"), lax.axis_index("y"), lax.axis_index("c")


def _shard_of(ref, kind, p):
    if kind == "col":
        n = ref.shape[1] // N_DEV
        return ref.at[:, pl.ds(pl.multiple_of(p * n, 128), n)]
    r = ref.shape[0] // N_DEV
    return ref.at[pl.ds(pl.multiple_of(p * r, 16), r), :]


def _full_shape(shard, kind):
    if kind == "col":
        return (shard.shape[0], shard.shape[1] * N_DEV)
    return (shard.shape[0] * N_DEV, shard.shape[1])


def _place(shards, layer, kind, block, after, name):
    _, rows, cols = shards.shape
    tr = _tile(rows, 512)

    def body(b_ref, s_ref, after_ref, o_ref):
        o_ref[...] = s_ref[...].astype(BF16)

    if kind == "col":
        out = pl.BlockSpec((tr, cols), lambda i, b_ref: (i, b_ref[0]))
    else:
        out = pl.BlockSpec((tr, cols), lambda i, b_ref: (b_ref[0] * (rows // tr) + i, 0))
    return pl.pallas_call(
        body, name=name,
        grid_spec=pltpu.PrefetchScalarGridSpec(
            num_scalar_prefetch=1, grid=(rows // tr,),
            in_specs=[pl.BlockSpec((None, tr, cols), lambda i, b_ref: (layer, i, 0)), ANY], out_specs=out),
        out_shape=jax.ShapeDtypeStruct(_full_shape(shards[0], kind), BF16),
        compiler_params=_params("parallel"),
    )(block, shards, after)


HBM = pl.BlockSpec(memory_space=pltpu.HBM)
SEM = pl.BlockSpec(memory_space=pltpu.SEMAPHORE)
FLOWS = pltpu.SideEffectType.DATAFLOW_SIDE_EFFECTING


def _in_hbm(a):
    return pltpu.with_memory_space_constraint(a, pltpu.HBM)


def _hbm_like(bufs):
    return [pltpu.HBM(b.shape, b.dtype) for b in bufs]


def _copies_start(name, bufs, plan, n, after):
    nb = len(bufs)

    def body(*refs):
        send_sems, recv_sems, token = refs[nb + 1], refs[nb + 2], refs[-1]
        for cp in plan(refs[:nb], send_sems, recv_sems):
            cp.start()
        token[...] = jnp.zeros_like(token)

    out = pl.pallas_call(
        body, name=name, in_specs=[HBM] * nb + [ANY],
        out_specs=[SEM, SEM] + [HBM] * nb + [pl.BlockSpec(memory_space=pltpu.VMEM)],
        out_shape=[pltpu.SemaphoreType.DMA((n,)), pltpu.SemaphoreType.DMA((n,))] + _hbm_like(bufs)
        + [jax.ShapeDtypeStruct((8, 128), F32)],
        input_output_aliases={i: 2 + i for i in range(nb)},
        compiler_params=pltpu.CompilerParams(has_side_effects=FLOWS),
    )(*[_in_hbm(b) for b in bufs], after)
    return (out[0], out[1]), list(out[2:2 + nb]), out[-1]


def _copies_wait(name, bufs, sems, after, plan):
    nb = len(bufs)

    def body(*refs):
        for cp in plan(refs[:nb], refs[nb], refs[nb + 1]):
            cp.wait_send()
            cp.wait_recv()

    out = pl.pallas_call(
        body, name=name, in_specs=[HBM] * nb + [SEM, SEM, ANY], out_specs=[HBM] * nb,
        out_shape=_hbm_like(bufs), input_output_aliases={i: i for i in range(nb)},
        compiler_params=pltpu.CompilerParams(has_side_effects=FLOWS),
    )(*bufs, *sems, after)
    return list(out)


def _copies_wait_start(name, bufs, sems, after, plan, next_plan, n_next):
    nb = len(bufs)

    def body(*refs):
        ins = refs[:nb]
        for cp in plan(ins, refs[nb], refs[nb + 1]):
            cp.wait_send()
            cp.wait_recv()
        send_sems, recv_sems, token = refs[nb + 3], refs[nb + 4], refs[-1]
        for cp in next_plan(ins, send_sems, recv_sems):
            cp.start()
        token[...] = jnp.zeros_like(token)

    out = pl.pallas_call(
        body, name=name, in_specs=[HBM] * nb + [SEM, SEM, ANY],
        out_specs=[SEM, SEM] + [HBM] * nb + [pl.BlockSpec(memory_space=pltpu.VMEM)],
        out_shape=[pltpu.SemaphoreType.DMA((n_next,)), pltpu.SemaphoreType.DMA((n_next,))] + _hbm_like(bufs)
        + [jax.ShapeDtypeStruct((8, 128), F32)],
        input_output_aliases={i: 2 + i for i in range(nb)},
        compiler_params=pltpu.CompilerParams(has_side_effects=FLOWS),
    )(*bufs, *sems, after)
    return (out[0], out[1]), list(out[2:2 + nb]), out[-1]


def _gather_plans(kinds):
    nt = len(kinds)

    def slot(refs, t, px, py, pc):
        return _shard_of(refs[t], kinds[t], 4 * px + 2 * py + pc)

    def to_chips(refs, send_sems, recv_sems):
        x, y, c = _coords()
        peers = [(x, y, 1 - c), (1 - x, y, c), (x, 1 - y, c), (1 - x, 1 - y, c)]
        return [pltpu.make_async_remote_copy(
            src_ref=slot(refs, t, x, y, c), dst_ref=slot(refs, t, x, y, c), send_sem=send_sems.at[4 * t + k],
            recv_sem=recv_sems.at[4 * t + k], device_id=peer, device_id_type=MESH_ID)
            for t in range(nt) for k, peer in enumerate(peers)]

    def to_sibling(refs, send_sems, recv_sems):
        x, y, c = _coords()
        chips = [(1 - x, y), (x, 1 - y), (1 - x, 1 - y)]
        return [pltpu.make_async_remote_copy(
            src_ref=slot(refs, t, *chip, c), dst_ref=slot(refs, t, *chip, c), send_sem=send_sems.at[3 * t + j],
            recv_sem=recv_sems.at[3 * t + j], device_id=(x, y, 1 - c), device_id_type=MESH_ID)
            for t in range(nt) for j, chip in enumerate(chips)]

    return to_chips, to_sibling


def _shard_shape(full, kind):
    if kind == "col":
        return (full.shape[0], full.shape[1] // N_DEV)
    return (full.shape[0] // N_DEV, full.shape[1])


def _scatter_plan(kinds):
    nt = len(kinds)

    def plan(refs, send_sems, recv_sems):
        x, y, c = _coords()
        copies = []
        for t in range(nt):
            for k in range(1, N_DEV):
                px = 1 - x if (k >> 2) & 1 else x
                py = 1 - y if (k >> 1) & 1 else y
                pc = 1 - c if k & 1 else c
                copies.append(pltpu.make_async_remote_copy(
                    src_ref=_shard_of(refs[t], kinds[t], 4 * px + 2 * py + pc),
                    dst_ref=refs[nt + t].at[4 * x + 2 * y + c],
                    send_sem=send_sems.at[7 * t + k - 1], recv_sem=recv_sems.at[7 * t + k - 1],
                    device_id=(px, py, pc), device_id_type=MESH_ID))
        return copies
    return plan


def _partial_specs(full, kind, tr):
    rows, cols = _shard_shape(full, kind)
    steps = rows // tr
    if kind == "col":
        own = pl.BlockSpec((tr, cols), lambda i, w: (i, w[0]))
    else:
        own = pl.BlockSpec((tr, cols), lambda i, w: (w[0] * steps + i, 0))
    return [own] + [pl.BlockSpec((None, tr, cols), lambda i, w, k=k: (w[k], i, 0)) for k in range(1, N_DEV)]


def _small_plan(refs, send_sems, recv_sems):
    x, y, c = _coords()
    copies = []
    for k in range(1, N_DEV):
        peer = (1 - x if (k >> 2) & 1 else x, 1 - y if (k >> 1) & 1 else y, 1 - c if k & 1 else c)
        copies.append(pltpu.make_async_remote_copy(
            src_ref=refs[0], dst_ref=refs[1].at[4 * x + 2 * y + c], send_sem=send_sems.at[k - 1],
            recv_sem=recv_sems.at[k - 1], device_id=peer, device_id_type=MESH_ID))
    return copies


def _sum_small(vec, land):
    def body(v_ref, l_ref, o_ref):
        x, y, c = _coords()
        me = 4 * x + 2 * y + c
        total = jnp.where(me == 0, v_ref[...], l_ref[0])
        for p in range(1, N_DEV):
            total = total + jnp.where(me == p, v_ref[...], l_ref[p])
        o_ref[...] = total

    whole = pl.BlockSpec(memory_space=pltpu.VMEM)
    return pl.pallas_call(
        body, name="sum_small", in_specs=[whole, whole], out_specs=whole,
        out_shape=jax.ShapeDtypeStruct(vec.shape, F32),
        compiler_params=pltpu.CompilerParams(vmem_limit_bytes=VMEM_LIMIT),
    )(vec, land)


def _adamw_math(w, g, m, v):
    m = ADAM_B1 * m + (1.0 - ADAM_B1) * g
    v = ADAM_B2 * v + (1.0 - ADAM_B2) * (g * g)
    m_hat = m / (1.0 - ADAM_B1 ** ADAM_STEP)
    v_hat = v / (1.0 - ADAM_B2 ** ADAM_STEP)
    delta = -ADAM_LR * (m_hat / (jnp.sqrt(v_hat) + ADAM_EPS) + ADAM_WD * w)
    return delta, m, v


def _adamw(w, grads, kind, where, m, v, after, name):
    layers, rows, cols = w.shape
    tr = _tile(rows, 128)
    out = None
    for layer, (grad, land) in enumerate(grads):
        def body(w_ref, *refs):
            parts, (x_ref, m_ref, v_ref) = refs[:N_DEV], refs[N_DEV:N_DEV + 3]
            g_ref, d_ref, mo_ref, vo_ref = refs[-4:]
            g = parts[0][...].astype(F32)
            for p_ref in parts[1:]:
                g = g + p_ref[...].astype(F32)
            g_ref[...] = g
            d_ref[...], mo_ref[...], vo_ref[...] = _adamw_math(x_ref[...], g, m_ref[...], v_ref[...])

        blk = pl.BlockSpec((None, tr, cols), lambda i, w_, layer=layer: (layer, i, 0))
        earlier = [] if out is None else list(out)
        out = pl.pallas_call(
            body, name=f"{name}_{layer}",
            grid_spec=pltpu.PrefetchScalarGridSpec(
                num_scalar_prefetch=1, grid=(rows // tr,),
                in_specs=_partial_specs(grad, kind, tr) + [blk] * 3 + [ANY] * (len(earlier) + len(after)),
                out_specs=[blk] * 4),
            out_shape=[jax.ShapeDtypeStruct((layers, rows, cols), F32)] * 4,
            input_output_aliases={N_DEV + 4 + k: k for k in range(len(earlier))},
            compiler_params=_params("parallel"),
        )(where, grad, *[land] * (N_DEV - 1), w, m, v, *earlier, *after)
    return out


def _adamw_small(w, g, m, v):
    def body(w_ref, g_ref, m_ref, v_ref, d_ref, mo_ref, vo_ref):
        d_ref[...], mo_ref[...], vo_ref[...] = _adamw_math(w_ref[...], g_ref[...], m_ref[...], v_ref[...])

    whole = pl.BlockSpec(memory_space=pltpu.VMEM)
    return pl.pallas_call(
        body, name="adamw_small", in_specs=[whole] * 4, out_specs=[whole] * 3,
        out_shape=[jax.ShapeDtypeStruct(w.shape, F32)] * 3,
        compiler_params=pltpu.CompilerParams(vmem_limit_bytes=VMEM_LIMIT),
    )(w, g, m, v)


def _pack(arrays):
    rows = []
    for a in arrays:
        flat = a.reshape(-1)
        pad = (-flat.shape[0]) % 1024
        rows.append(jnp.pad(flat, (0, pad)).reshape(-1, 128))
    return jnp.concatenate(rows, axis=0)


def _unpack(packed, like):
    out, r = [], 0
    for a in like:
        n = math.prod(a.shape)
        nr = (n + 1023) // 1024 * 8
        out.append(packed[r:r + nr].reshape(-1)[:n].reshape(a.shape))
        r += nr
    return out


KIND = {"ab_in": "col", "ab_out": "row", "sb_in": "col", "sb_out": "row",
        "w1_0": "col", "w1_1": "col", "w2_0": "row", "w2_1": "row"}
GATHERS = {"ab_in": ("ab_in",), "ab_out": ("ab_out",), "w1_0": ("w1_0",), "w2_0": ("w2_0",),
           "sb": ("sb_in", "sb_out"), "ffn1": ("w1_1", "w2_1")}


class _Exchange:
    def __init__(self, shards):
        x, y, c = _coords()
        me = (4 * x + 2 * y + c).astype(jnp.int32)
        self.where = jnp.stack([jnp.bitwise_xor(me, k) for k in range(N_DEV)])
        self.full = {}
        self.tokens = []
        self.gathers = {}
        self.scatters = {}
        self.settled = {}
        block = me.reshape(1)
        after = block
        for key, group in GATHERS.items():
            for n in group:
                self.full[n] = _place(*shards[n], KIND[n], block, after, f"place_{n}")
            to_chips, to_sibling = _gather_plans([KIND[n] for n in group])
            bufs = [self.full[n] for n in group]
            sems, bufs, after = _copies_start(f"gather_start_{key}", bufs, to_chips, 4 * len(group), after)
            self.tokens.append(after)
            self.gathers[key] = (group, sems, bufs, to_chips, to_sibling)

    def tie(self, small):
        for token in self.tokens:
            small = small + token[0:1, 0:1]
        self.tokens = []
        return small

    def started(self):
        return tuple(self.tokens)

    def weight(self, name):
        return self.full[name]

    def arrive(self, key, after):
        group, sems, bufs, to_chips, to_sibling = self.gathers[key]
        sems, bufs, token = _copies_wait_start(f"gather_pass_{key}", bufs, sems, after, to_chips, to_sibling,
                                               3 * len(group))
        self.tokens.append(token)
        self.gathers[key] = (group, sems, bufs, token, to_sibling)

    def land(self, key, after=None):
        group, sems, bufs, token, to_sibling = self.gathers.pop(key)
        after = token if after is None else after
        self.full.update(zip(group, _copies_wait(f"gather_done_{key}", bufs, sems, after, to_sibling)))

    def reduce(self, key, grads):
        names = list(grads)
        kinds = [KIND[n] for n in names]
        full = [grads[n] for n in names]
        lands = [lax.empty((N_DEV,) + _shard_shape(g, k), BF16) for g, k in zip(full, kinds)]
        plan = _scatter_plan(kinds)
        sems, bufs, token = _copies_start(f"scatter_start_{key}", full + lands, plan, (N_DEV - 1) * len(names),
                                          full[-1])
        self.tokens.append(token)
        self.scatters[key] = (names, sems, bufs, plan)

    def settle(self, keys, after):
        for key in keys:
            names, sems, bufs, plan = self.scatters.pop(key)
            bufs = _copies_wait(f"scatter_done_{key}", bufs, sems, after, plan)
            self.settled.update({n: t for n, *t in zip(names, bufs[:len(names)], bufs[len(names):])})
        return self.settled


SMALL = ("norm_pre_mix", "norm_post_mix", "norm_pre_ffn", "norm_post_ffn", "sgu_ln_g", "sgu_ln_b", "sgu_w", "sgu_b")
ORDER = ("norm_pre_mix", "norm_post_mix", "norm_pre_ffn", "norm_post_ffn", "ab_w_in", "sgu_ln_g", "sgu_ln_b", "sgu_w",
         "sgu_b", "ab_w_out", "sb_w_in", "sb_w_out", "ffn_w1", "ffn_w2")


def kernel(x, norm_pre_mix, norm_post_mix, norm_pre_ffn, norm_post_ffn, ab_w_in, sgu_ln_g, sgu_ln_b, sgu_w, sgu_b, ab_w_out, sb_w_in, sb_w_out, ffn_w1, ffn_w2, loss_target, m_norm_pre_mix, m_norm_post_mix, m_norm_pre_ffn, m_norm_post_ffn, m_ab_w_in, m_sgu_ln_g, m_sgu_ln_b, m_sgu_w, m_sgu_b, m_ab_w_out, m_sb_w_in, m_sb_w_out, m_ffn_w1, m_ffn_w2, v_norm_pre_mix, v_norm_post_mix, v_norm_pre_ffn, v_norm_post_ffn, v_ab_w_in, v_sgu_ln_g, v_sgu_ln_b, v_sgu_w, v_sgu_b, v_ab_w_out, v_sb_w_in, v_sb_w_out, v_ffn_w1, v_ffn_w2):
    W = dict(norm_pre_mix=norm_pre_mix, norm_post_mix=norm_post_mix, norm_pre_ffn=norm_pre_ffn,
             norm_post_ffn=norm_post_ffn, ab_w_in=ab_w_in, sgu_ln_g=sgu_ln_g, sgu_ln_b=sgu_ln_b, sgu_w=sgu_w,
             sgu_b=sgu_b, ab_w_out=ab_w_out, sb_w_in=sb_w_in, sb_w_out=sb_w_out, ffn_w1=ffn_w1, ffn_w2=ffn_w2)
    M = dict(norm_pre_mix=m_norm_pre_mix, norm_post_mix=m_norm_post_mix, norm_pre_ffn=m_norm_pre_ffn,
             norm_post_ffn=m_norm_post_ffn, ab_w_in=m_ab_w_in, sgu_ln_g=m_sgu_ln_g, sgu_ln_b=m_sgu_ln_b,
             sgu_w=m_sgu_w, sgu_b=m_sgu_b, ab_w_out=m_ab_w_out, sb_w_in=m_sb_w_in, sb_w_out=m_sb_w_out,
             ffn_w1=m_ffn_w1, ffn_w2=m_ffn_w2)
    V = dict(norm_pre_mix=v_norm_pre_mix, norm_post_mix=v_norm_post_mix, norm_pre_ffn=v_norm_pre_ffn,
             norm_post_ffn=v_norm_post_ffn, ab_w_in=v_ab_w_in, sgu_ln_g=v_sgu_ln_g, sgu_ln_b=v_sgu_ln_b,
             sgu_w=v_sgu_w, sgu_b=v_sgu_b, ab_w_out=v_ab_w_out, sb_w_in=v_sb_w_in, sb_w_out=v_sb_w_out,
             ffn_w1=v_ffn_w1, ffn_w2=v_ffn_w2)

    shards = {"ab_in": (ab_w_in, 0), "ab_out": (ab_w_out, 0), "w1_0": (ffn_w1, 0), "w2_0": (ffn_w2, 0),
              "sb_in": (sb_w_in, 0), "sb_out": (sb_w_out, 0), "w1_1": (ffn_w1, 1), "w2_1": (ffn_w2, 1)}
    comm = _Exchange(shards)
    norms = (norm_pre_mix, norm_post_mix, norm_pre_ffn, norm_post_ffn)
    sgu = (sgu_ln_g, sgu_ln_b, sgu_w[0], sgu_b[0])
    loss, dx, small = _local_step(x[0], loss_target[0], norms, sgu, comm)

    out = {}

    def update(name, layers, landed, after=()):
        out[name] = _adamw(W[name], [landed[n] for n in layers], KIND[layers[0]], comm.where, M[name], V[name],
                           after, f"adamw_{name}")

    landed = comm.settle(("ffn1", "sb", "ffn0"), after=dx)
    small_g = [small["pre_mix"], small["post_mix"], small["pre_ffn"], small["post_ffn"], small["ln_g"],
               small["ln_b"], small["w_s"][None], small["b_s"][None]]
    mine = _pack(small_g + [loss.reshape(1)])
    sems, bufs, token = _copies_start("small_start", [mine, jnp.zeros((N_DEV,) + mine.shape, F32)], _small_plan,
                                      N_DEV - 1, dx)
    for name, layers in (("sb_w_in", ["sb_in"]), ("sb_w_out", ["sb_out"]), ("ffn_w1", ["w1_0", "w1_1"]),
                         ("ffn_w2", ["w2_0", "w2_1"])):
        update(name, layers, landed, after=(token,))
    summed = _sum_small(*_copies_wait("small_done", bufs, sems, out["ffn_w2"][3], _small_plan))
    g_small, loss = summed[:-8], summed[-8, 0]
    landed = comm.settle(("ab_out", "ab_in"), after=g_small)
    update("ab_w_out", ["ab_out"], landed)
    update("ab_w_in", ["ab_in"], landed)
    res = _adamw_small(_pack([W[n] for n in SMALL]), g_small, _pack([M[n] for n in SMALL]),
                       _pack([V[n] for n in SMALL]))
    like = [W[n] for n in SMALL]
    for n, *vals in zip(SMALL, *[_unpack(r, like) for r in [g_small] + list(res)]):
        out[n] = vals

    return (loss, dx[None], *[out[n][0] for n in ORDER], *[out[n][1] for n in ORDER],
            *[out[n][2] for n in ORDER], *[out[n][3] for n in ORDER])
```

```python
import math

import jax
import jax.numpy as jnp
from jax import lax
from jax.experimental import pallas as pl
from jax.experimental.pallas import tpu as pltpu

F32 = jnp.float32
BF16 = jnp.bfloat16

HEAD_DIM = 128
CHUNK = 128
ATT_BLOCK = 128
DILATED_PAIRS = ((128, 1), (512, 4), (2048, 16))
RMS_EPS = 1e-6
LN_EPS = 1e-5
ADAM_LR = 0.001
ADAM_B1 = 0.9
ADAM_B2 = 0.999
ADAM_EPS = 1e-08
ADAM_WD = 0.01
ADAM_STEP = 10
N_DEV = 8
MASKED = -1e30

V7X_VMEM_BYTES = 64 * 1024 * 1024
VMEM_LIMIT = V7X_VMEM_BYTES - 8 * 1024 * 1024

NN = (((1,), (0,)), ((), ()))
NT = (((1,), (1,)), ((), ()))
TN = (((0,), (0,)), ((), ()))


def _params(*sem):
    return pltpu.CompilerParams(dimension_semantics=sem, vmem_limit_bytes=VMEM_LIMIT)


def _dot(a, b, dims=NN):
    return lax.dot_general(a, b, dims, preferred_element_type=F32)


def _tile(n, preferred):
    if n <= preferred:
        return n
    t = preferred - preferred % 128
    while n % t:
        t -= 128
    assert t > 0, (n, preferred)
    return t


def _matmul(a, b, *, mode, name, out_dtype=F32, tm=1024, tn=1024, tk=2048, epi=None, extras=(), after=()):
    if mode == "nn":
        (M, K), N = a.shape, b.shape[1]
    elif mode == "nt":
        (M, K), N = a.shape, b.shape[0]
    else:
        (K, M), N = a.shape, b.shape[1]
    tm, tn, tk = _tile(M, tm), _tile(N, tn), _tile(K, tk)
    nk = K // tk
    if mode == "tn":
        a_spec = pl.BlockSpec((tk, tm), lambda i, j, k: (k, i))
    else:
        a_spec = pl.BlockSpec((tm, tk), lambda i, j, k: (i, k))
    if mode == "nt":
        b_spec = pl.BlockSpec((tn, tk), lambda i, j, k: (j, k))
    else:
        b_spec = pl.BlockSpec((tk, tn), lambda i, j, k: (k, j))
    o_spec = pl.BlockSpec((tm, tn), lambda i, j, k: (i, j))
    dims = {"nn": NN, "nt": NT, "tn": TN}[mode]
    n_extra = len(extras)
    n_in = n_extra + len(after)
    several = isinstance(out_dtype, tuple)
    n_out = len(out_dtype) if several else 1

    def finish(acc, rest):
        outs = acc if epi is None else epi(acc, pl.program_id(1), *[r[...] for r in rest[:n_extra]])
        for o_ref, o in zip(rest[n_in:n_in + n_out], outs if several else (outs,)):
            o_ref[...] = o.astype(o_ref.dtype)

    if nk == 1:
        def body(a_ref, b_ref, *rest):
            finish(_dot(a_ref[...], b_ref[...], dims), rest)
        scratch = []
    else:
        def body(a_ref, b_ref, *rest):
            acc_ref = rest[n_in + n_out]
            k = pl.program_id(2)

            @pl.when(k == 0)
            def _():
                acc_ref[...] = jnp.zeros_like(acc_ref)

            acc_ref[...] += _dot(a_ref[...], b_ref[...], dims)

            @pl.when(k == nk - 1)
            def _():
                finish(acc_ref[...], rest)
        scratch = [pltpu.VMEM((tm, tn), F32)]

    shapes = [jax.ShapeDtypeStruct((M, N), d) for d in (out_dtype if several else (out_dtype,))]
    return pl.pallas_call(
        body,
        name=name,
        grid=(M // tm, N // tn, nk),
        in_specs=[a_spec, b_spec] + [o_spec] * n_extra + [ANY] * len(after),
        out_specs=[o_spec] * n_out if several else o_spec,
        out_shape=shapes if several else shapes[0],
        scratch_shapes=scratch,
        compiler_params=_params("parallel", "parallel", "arbitrary"),
    )(a, b, *extras, *after)


ROWS = 512


def _rms(x):
    return lax.rsqrt(jnp.mean(x * x, axis=-1, keepdims=True) + RMS_EPS)


def _prenorm(x, g, name):
    T, D = x.shape

    def body(x_ref, g_ref, h_ref):
        xv = x_ref[...]
        h_ref[...] = (xv * _rms(xv) * g_ref[...]).astype(BF16)

    row = pl.BlockSpec((ROWS, D), lambda i: (i, 0))
    vec = pl.BlockSpec((1, D), lambda i: (0, 0))
    return pl.pallas_call(
        body, name=name, grid=(T // ROWS,), in_specs=[row, vec], out_specs=row,
        out_shape=jax.ShapeDtypeStruct((T, D), BF16), compiler_params=_params("parallel"),
    )(x, g)


def _postnorm_prenorm(x, y, g_post, g_pre, name):
    T, D = x.shape

    def body(x_ref, y_ref, gp_ref, gn_ref, xo_ref, h_ref):
        yv = y_ref[...]
        xn = x_ref[...] + yv * _rms(yv) * gp_ref[...]
        xo_ref[...] = xn
        h_ref[...] = (xn * _rms(xn) * gn_ref[...]).astype(BF16)

    row = pl.BlockSpec((ROWS, D), lambda i: (i, 0))
    vec = pl.BlockSpec((1, D), lambda i: (0, 0))
    return pl.pallas_call(
        body, name=name, grid=(T // ROWS,), in_specs=[row, row, vec, vec], out_specs=[row, row],
        out_shape=[jax.ShapeDtypeStruct((T, D), F32), jax.ShapeDtypeStruct((T, D), BF16)],
        compiler_params=_params("parallel"),
    )(x, y, g_post, g_pre)


def _postnorm_grads(dn, yh, r, g):
    gd = dn * g
    return r * (gd - yh * jnp.mean(yh * gd, axis=-1, keepdims=True)), dn * yh


def _postnorm_loss(x, y, g_post, target, name):
    T, D = x.shape

    def body(x_ref, y_ref, gp_ref, t_ref, loss_ref, dx_ref, dy_ref, dg_ref):
        @pl.when(pl.program_id(0) == 0)
        def _():
            loss_ref[...] = jnp.zeros_like(loss_ref)
            dg_ref[...] = jnp.zeros_like(dg_ref)

        yv = y_ref[...]
        r = _rms(yv)
        yh = yv * r
        err = x_ref[...] + yh * gp_ref[...] - t_ref[...]
        dx = err * (1.0 / D)
        dx_ref[...] = dx
        loss_ref[...] += 0.5 * jnp.sum(jnp.sum(err * err, axis=-1, keepdims=True) * (1.0 / D))
        dy, dg = _postnorm_grads(dx, yh, r, gp_ref[...])
        dy_ref[...] = dy.astype(BF16)
        dg_ref[...] += jnp.sum(dg, axis=0, keepdims=True)

    row = pl.BlockSpec((ROWS, D), lambda i: (i, 0))
    vec = pl.BlockSpec((1, D), lambda i: (0, 0))
    acc = pl.BlockSpec((8, 128), lambda i: (0, 0))
    return pl.pallas_call(
        body, name=name, grid=(T // ROWS,), in_specs=[row, row, vec, row], out_specs=[acc, row, row, vec],
        out_shape=[jax.ShapeDtypeStruct((8, 128), F32), jax.ShapeDtypeStruct((T, D), F32),
                   jax.ShapeDtypeStruct((T, D), BF16), jax.ShapeDtypeStruct((1, D), F32)],
        compiler_params=_params("arbitrary"),
    )(x, y, g_post, target)


def _norm_bwd_pair(dx_out, dh, x, g_pre, y_prev, g_post_prev, name):
    T, D = x.shape

    def body(dxo_ref, dh_ref, x_ref, g_ref, y_ref, gp_ref, dx_ref, dg_ref, dy_ref, dgp_ref):
        @pl.when(pl.program_id(0) == 0)
        def _():
            dg_ref[...] = jnp.zeros_like(dg_ref)
            dgp_ref[...] = jnp.zeros_like(dgp_ref)

        xv, dhv = x_ref[...], dh_ref[...]
        r = _rms(xv)
        xh = xv * r
        gd = dhv * g_ref[...]
        dx = dxo_ref[...] + r * (gd - xh * jnp.mean(xh * gd, axis=-1, keepdims=True))
        dx_ref[...] = dx
        dg_ref[...] += jnp.sum(dhv * xh, axis=0, keepdims=True)
        yv = y_ref[...]
        ry = _rms(yv)
        dy, dgp = _postnorm_grads(dx, yv * ry, ry, gp_ref[...])
        dy_ref[...] = dy.astype(BF16)
        dgp_ref[...] += jnp.sum(dgp, axis=0, keepdims=True)

    row = pl.BlockSpec((ROWS, D), lambda i: (i, 0))
    vec = pl.BlockSpec((1, D), lambda i: (0, 0))
    return pl.pallas_call(
        body, name=name, grid=(T // ROWS,), in_specs=[row, row, row, vec, row, vec],
        out_specs=[row, vec, row, vec],
        out_shape=[jax.ShapeDtypeStruct((T, D), F32), jax.ShapeDtypeStruct((1, D), F32),
                   jax.ShapeDtypeStruct((T, D), BF16), jax.ShapeDtypeStruct((1, D), F32)],
        compiler_params=_params("arbitrary"),
    )(dx_out, dh, x, g_pre, y_prev, g_post_prev)


def _prenorm_bwd(dx_out, dh, x, g_pre, name):
    T, D = x.shape

    def body(dxo_ref, dh_ref, x_ref, g_ref, dx_ref, dg_ref):
        @pl.when(pl.program_id(0) == 0)
        def _():
            dg_ref[...] = jnp.zeros_like(dg_ref)

        xv, dhv = x_ref[...], dh_ref[...]
        r = _rms(xv)
        xh = xv * r
        gd = dhv * g_ref[...]
        dx_ref[...] = dxo_ref[...] + r * (gd - xh * jnp.mean(xh * gd, axis=-1, keepdims=True))
        dg_ref[...] += jnp.sum(dhv * xh, axis=0, keepdims=True)

    row = pl.BlockSpec((ROWS, D), lambda i: (i, 0))
    vec = pl.BlockSpec((1, D), lambda i: (0, 0))
    return pl.pallas_call(
        body, name=name, grid=(T // ROWS,), in_specs=[row, row, row, vec], out_specs=[row, vec],
        out_shape=[jax.ShapeDtypeStruct((T, D), F32), jax.ShapeDtypeStruct((1, D), F32)],
        compiler_params=_params("arbitrary"),
    )(dx_out, dh, x, g_pre)


_INV_SQRT2 = 1.0 / math.sqrt(2.0)
_INV_SQRT2PI = 1.0 / math.sqrt(2.0 * math.pi)


def _gelu(x):
    return 0.5 * x * (1.0 + lax.erf(x * _INV_SQRT2))


def _gelu_grad(x):
    return 0.5 * (1.0 + lax.erf(x * _INV_SQRT2)) + x * jnp.exp(-0.5 * x * x) * _INV_SQRT2PI


def _layernorm_stats(x):
    mu = jnp.mean(x, axis=-1, keepdims=True)
    xc = x - mu
    rstd = lax.rsqrt(jnp.mean(xc * xc, axis=-1, keepdims=True) + LN_EPS)
    return xc * rstd, rstd


def _tril_mask():
    i = lax.broadcasted_iota(jnp.int32, (CHUNK, CHUNK), 0)
    j = lax.broadcasted_iota(jnp.int32, (CHUNK, CHUNK), 1)
    return j <= i


SGU_ROWS = 512


def _sgu_fwd(z, ln_g, ln_b, w_s, b_t, name):
    T = z.shape[0]
    A = ln_g.shape[1]
    G = A // 128
    rows = min(SGU_ROWS, T)

    def body(u_ref, v_ref, g_ref, b_ref, w_ref, bt_ref, o_ref):
        mask = _tril_mask()
        for c in range(rows // CHUNK):
            rs = pl.ds(c * CHUNK, CHUNK)
            xh, _ = _layernorm_stats(_gelu(v_ref[rs, :]))
            vn = (xh * g_ref[...] + b_ref[...]).astype(BF16)
            for g in range(G):
                cs = pl.ds(g * 128, 128)
                w = jnp.where(mask, w_ref[g], 0.0).astype(BF16)
                mixed = _dot(w, vn[:, g * 128:(g + 1) * 128]) + bt_ref[:, g:g + 1]
                o_ref[rs, cs] = (_gelu(u_ref[rs, cs]) * mixed).astype(BF16)

    return pl.pallas_call(
        body, name=name, grid=(T // rows,),
        in_specs=[
            pl.BlockSpec((rows, A), lambda i: (i, 0)),
            pl.BlockSpec((rows, A), lambda i: (i, 1)),
            pl.BlockSpec((1, A), lambda i: (0, 0)),
            pl.BlockSpec((1, A), lambda i: (0, 0)),
            pl.BlockSpec((G, CHUNK, CHUNK), lambda i: (0, 0, 0)),
            pl.BlockSpec((CHUNK, G), lambda i: (0, 0)),
        ],
        out_specs=pl.BlockSpec((rows, A), lambda i: (i, 0)),
        out_shape=jax.ShapeDtypeStruct((T, 2 * A), BF16),
        compiler_params=_params("parallel"),
    )(z, z, ln_g, ln_b, w_s, b_t)


def _sgu_bwd(z, dcat, ln_g, ln_b, w_s, b_t, name):
    T = z.shape[0]
    A = ln_g.shape[1]
    G = A // 128
    rows = min(SGU_ROWS, T)

    def body(u_ref, v_ref, da_ref, g_ref, b_ref, w_ref, bt_ref, dz_ref, dg_ref, db_ref, dw_ref, dbt_ref, dvn_ref):
        @pl.when(pl.program_id(0) == 0)
        def _():
            dg_ref[...] = jnp.zeros_like(dg_ref)
            db_ref[...] = jnp.zeros_like(db_ref)
            dw_ref[...] = jnp.zeros_like(dw_ref)
            dbt_ref[...] = jnp.zeros_like(dbt_ref)

        mask = _tril_mask()
        for c in range(rows // CHUNK):
            rs = pl.ds(c * CHUNK, CHUNK)
            vv = v_ref[rs, :]
            gv = _gelu(vv)
            xh, rstd = _layernorm_stats(gv)
            vn = (xh * g_ref[...] + b_ref[...]).astype(BF16)
            for g in range(G):
                cs = pl.ds(g * 128, 128)
                w = jnp.where(mask, w_ref[g], 0.0).astype(BF16)
                vg = vn[:, g * 128:(g + 1) * 128]
                mixed = _dot(w, vg) + bt_ref[:, g:g + 1]
                uu = u_ref[rs, cs]
                da = da_ref[rs, cs]
                dz_ref[rs, cs] = (da * mixed * _gelu_grad(uu)).astype(BF16)
                dm = da * _gelu(uu)
                dmb = dm.astype(BF16)
                dbt_ref[:, g:g + 1] += jnp.sum(dm, axis=1, keepdims=True)
                dw_ref[g] += jnp.where(mask, _dot(dmb, vg, NT), 0.0)
                dvn_ref[:, cs] = _dot(w, dmb, TN)
            dvn = dvn_ref[...]
            dg_ref[...] += jnp.sum(dvn * xh, axis=0, keepdims=True)
            db_ref[...] += jnp.sum(dvn, axis=0, keepdims=True)
            dxh = dvn * g_ref[...]
            dgv = rstd * (dxh - jnp.mean(dxh, axis=-1, keepdims=True)
                          - xh * jnp.mean(dxh * xh, axis=-1, keepdims=True))
            dz_ref[rs, pl.ds(A, A)] = (dgv * _gelu_grad(vv)).astype(BF16)

    vec = pl.BlockSpec((1, A), lambda i: (0, 0))
    wsp = pl.BlockSpec((G, CHUNK, CHUNK), lambda i: (0, 0, 0))
    bsp = pl.BlockSpec((CHUNK, G), lambda i: (0, 0))
    return pl.pallas_call(
        body, name=name, grid=(T // rows,),
        in_specs=[
            pl.BlockSpec((rows, A), lambda i: (i, 0)),
            pl.BlockSpec((rows, A), lambda i: (i, 1)),
            pl.BlockSpec((rows, A), lambda i: (i, 0)),
            vec, vec, wsp, bsp,
        ],
        out_specs=[pl.BlockSpec((rows, 2 * A), lambda i: (i, 0)), vec, vec, wsp, bsp],
        out_shape=[
            jax.ShapeDtypeStruct((T, 2 * A), BF16),
            jax.ShapeDtypeStruct((1, A), F32),
            jax.ShapeDtypeStruct((1, A), F32),
            jax.ShapeDtypeStruct((G, CHUNK, CHUNK), F32),
            jax.ShapeDtypeStruct((CHUNK, G), F32),
        ],
        scratch_shapes=[pltpu.VMEM((CHUNK, A), F32)],
        compiler_params=_params("arbitrary"),
    )(z, z, dcat, ln_g, ln_b, w_s, b_t)


def _alibi_row(B, d):
    H = B // HEAD_DIM
    slopes = [d * 2.0 ** (-8.0 * (h + 1.0) / H) for h in range(H)]
    return jnp.repeat(jnp.asarray(slopes, F32), HEAD_DIM)[None, :]


def _dil_scores(q, k, slope_d, valid, dist):
    s = _dot(q, k, NT) - slope_d * dist
    return jnp.where(valid, s, MASKED)


def _dil_block(n, r, d):
    if d == 1:
        return pl.ds(pl.multiple_of(n * ATT_BLOCK, ATT_BLOCK), ATT_BLOCK)
    return pl.ds(n * (d * ATT_BLOCK) + r, ATT_BLOCK, stride=d)


def _dilated_forward(z, cat, B, name):
    T = z.shape[0]
    H = B // HEAD_DIM
    A = cat.shape[1] - B
    scale = HEAD_DIM ** -0.5
    blk = ATT_BLOCK
    chunk = _tile(T, 512)

    def body(q_ref, k_ref, v_ref, sl_ref, cat_in, cat_ref, of_ref, lt_ref, *branch):
        o_refs, l_refs = branch[:3], branch[3:]
        slope = sl_ref[:, :1]
        qi = lax.broadcasted_iota(jnp.int32, (blk, 2 * blk), 0)
        kj = lax.broadcasted_iota(jnp.int32, (blk, 2 * blk), 1)
        dist = qi + blk - kj
        band = (dist >= 0) & (dist <= blk)
        distf = dist.astype(F32)

        for b, (_, d) in enumerate(DILATED_PAIRS):
            def one(n, r, b=b, d=d):
                rows, prev = _dil_block(n, r, d), _dil_block(jnp.maximum(n - 1, 0), r, d)
                q = (q_ref[rows, :] * scale).astype(BF16)
                k = jnp.concatenate([k_ref[prev, :], k_ref[rows, :]], axis=0).astype(BF16)
                v = jnp.concatenate([v_ref[prev, :], v_ref[rows, :]], axis=0).astype(BF16)
                s = _dil_scores(q, k, slope * float(d), band & ((kj >= blk) | (n > 0)), distf)
                m = jnp.max(s, axis=-1, keepdims=True)
                p = jnp.exp(s - m)
                den = jnp.sum(p, axis=-1, keepdims=True)
                o_refs[b][rows, :] = _dot(p.astype(BF16), v) / den
                l_refs[b][rows, :] = jnp.broadcast_to(m + jnp.log(den), (blk, HEAD_DIM))

            per = max(1, 4 // d)

            def step(i, _, d=d, per=per, one=one):
                for u in range(per):
                    for r in range(d):
                        one(i * per + u, r)
                return 0

            lax.fori_loop(0, T // (d * blk * per), step, 0)

        def merge(i, _):
            rs = pl.ds(pl.multiple_of(i * chunk, chunk), chunk)
            a, b, c = l_refs[0][rs, :], l_refs[1][rs, :], l_refs[2][rs, :]
            m = jnp.maximum(jnp.maximum(a, b), c)
            ea, eb, ec = jnp.exp(a - m), jnp.exp(b - m), jnp.exp(c - m)
            tot = ea + eb + ec
            o = (ea * o_refs[0][rs, :] + eb * o_refs[1][rs, :] + ec * o_refs[2][rs, :]) / tot
            of_ref[rs, :] = o
            cat_ref[rs, :] = o.astype(BF16)
            lt_ref[rs, :] = m + jnp.log(tot)
            return 0

        lax.fori_loop(0, T // chunk, merge, 0)

    def col(unit):
        return lambda h: (0, unit * H + h)

    seq = (T, HEAD_DIM)
    out = pl.BlockSpec(seq, lambda h: (0, h))
    return pl.pallas_call(
        body, name=name, grid=(H,),
        in_specs=[pl.BlockSpec(seq, col(2)), pl.BlockSpec(seq, col(3)), pl.BlockSpec(seq, col(4)),
                  pl.BlockSpec((1, HEAD_DIM), lambda h: (0, h)), ANY],
        out_specs=[pl.BlockSpec(seq, lambda h: (0, A // HEAD_DIM + h)), out, out],
        out_shape=[jax.ShapeDtypeStruct(cat.shape, BF16), jax.ShapeDtypeStruct((T, B), F32),
                   jax.ShapeDtypeStruct((T, B), F32)],
        input_output_aliases={4: 0},
        scratch_shapes=[pltpu.VMEM(seq, F32)] * 6,
        compiler_params=_params("parallel"),
    )(z, z, z, _alibi_row(B, 1), cat)


def _dilated_backward(z, dcat, o, lse, B, name):
    T = z.shape[0]
    H = B // HEAD_DIM
    scale = HEAD_DIM ** -0.5
    blk = ATT_BLOCK
    chunk = _tile(T, 512)

    def body(q_ref, k_ref, v_ref, do_ref, o_ref, l_ref, sl_ref, dq_ref, dk_ref, dv_ref, aq_ref, ak_ref, av_ref):
        slope = sl_ref[:, :1]
        qi = lax.broadcasted_iota(jnp.int32, (blk, 2 * blk), 0)
        kj = lax.broadcasted_iota(jnp.int32, (blk, 2 * blk), 1)
        dist = qi + blk - kj
        band = (dist >= 0) & (dist <= blk)
        distf = dist.astype(F32)
        for acc in (aq_ref, ak_ref, av_ref):
            acc[...] = jnp.zeros_like(acc)

        for _, d in DILATED_PAIRS:
            def one(n, r, d=d):
                rows, prev = _dil_block(n, r, d), _dil_block(jnp.maximum(n - 1, 0), r, d)
                q = (q_ref[rows, :] * scale).astype(BF16)
                k = jnp.concatenate([k_ref[prev, :], k_ref[rows, :]], axis=0).astype(BF16)
                v = jnp.concatenate([v_ref[prev, :], v_ref[rows, :]], axis=0).astype(BF16)
                do = do_ref[rows, :]
                delta = jnp.sum(do * o_ref[rows, :], axis=-1, keepdims=True)
                do = do.astype(BF16)
                s = _dil_scores(q, k, slope * float(d), band & ((kj >= blk) | (n > 0)), distf)
                p = jnp.exp(s - l_ref[rows, :][:, :1])
                ds = (p * (_dot(do, v, NT) - delta)).astype(BF16)
                aq_ref[rows, :] += _dot(ds, k)
                dk = _dot(ds, q, TN)
                dv = _dot(p.astype(BF16), do, TN)
                ak_ref[prev, :] += dk[:blk]
                av_ref[prev, :] += dv[:blk]
                ak_ref[rows, :] += dk[blk:]
                av_ref[rows, :] += dv[blk:]

            per = max(1, 4 // d)

            def step(i, _, d=d, per=per, one=one):
                for u in range(per):
                    for r in range(d):
                        one(i * per + u, r)
                return 0

            lax.fori_loop(0, T // (d * blk * per), step, 0)

        def emit(i, _):
            rs = pl.ds(pl.multiple_of(i * chunk, chunk), chunk)
            dq_ref[rs, :] = (aq_ref[rs, :] * scale).astype(BF16)
            dk_ref[rs, :] = ak_ref[rs, :].astype(BF16)
            dv_ref[rs, :] = av_ref[rs, :].astype(BF16)
            return 0

        lax.fori_loop(0, T // chunk, emit, 0)

    def col(unit):
        return lambda h: (0, unit * H + h)

    seq = (T, HEAD_DIM)
    own = pl.BlockSpec(seq, lambda h: (0, h))
    return pl.pallas_call(
        body, name=name, grid=(H,),
        in_specs=[pl.BlockSpec(seq, col(2)), pl.BlockSpec(seq, col(3)), pl.BlockSpec(seq, col(4)),
                  pl.BlockSpec(seq, col(1)), own, own, pl.BlockSpec((1, HEAD_DIM), lambda h: (0, h))],
        out_specs=[own] * 3,
        out_shape=[jax.ShapeDtypeStruct((T, B), BF16)] * 3,
        scratch_shapes=[pltpu.VMEM(seq, F32)] * 3,
        compiler_params=_params("parallel"),
    )(z, z, z, dcat, o, lse, _alibi_row(B, 1))


def _join_columns(parts, name):
    T = parts[0].shape[0]
    widths = [p.shape[1] for p in parts]

    def body(*refs):
        o_ref, at = refs[-1], 0
        for ref, w in zip(refs[:-1], widths):
            o_ref[:, pl.ds(at, w)] = ref[...]
            at += w

    return pl.pallas_call(
        body, name=name, grid=(T // ROWS,),
        in_specs=[pl.BlockSpec((ROWS, w), lambda i: (i, 0)) for w in widths],
        out_specs=pl.BlockSpec((ROWS, sum(widths)), lambda i: (i, 0)),
        out_shape=jax.ShapeDtypeStruct((T, sum(widths)), BF16), compiler_params=_params("parallel"),
    )(*parts)


SB_QUERY_ROWS = 1024
SB_KEYS = 2 * ATT_BLOCK


def _tri_and_ones(pred):
    rows = lax.broadcasted_iota(jnp.int32, (2 * ATT_BLOCK, 2 * ATT_BLOCK), 0) % ATT_BLOCK
    cols = lax.broadcasted_iota(jnp.int32, (2 * ATT_BLOCK, 2 * ATT_BLOCK), 1)
    return ((cols >= ATT_BLOCK) | pred(rows, cols)).astype(BF16)


def _running(x, tri):
    hi = x.astype(BF16)
    lo = (x - hi.astype(F32)).astype(BF16)
    return _dot(jnp.concatenate([hi, lo], axis=1), tri)


def _sb_mask(query_rows, s):
    rows = lax.broadcasted_iota(jnp.int32, (query_rows - s * SB_KEYS, SB_KEYS), 0)
    cols = lax.broadcasted_iota(jnp.int32, (query_rows - s * SB_KEYS, SB_KEYS), 1)
    return cols < rows


def _log_sigmoids(z):
    ls = jnp.minimum(z, 0.0) - jnp.log(1.0 + jnp.exp(-jnp.abs(z)))
    return ls, ls - z


def _sb_fwd(qkv, W, name):
    T = qkv.shape[0]
    H = W // HEAD_DIM
    blk = ATT_BLOCK
    qb = min(SB_QUERY_ROWS, T)
    per = qb // SB_KEYS

    def body(q_ref, k_ref, v_ref, o_ref, lt_ref, acc_ref):
        i = pl.program_id(1)
        tri = _tri_and_ones(lambda r, c: r > c)
        lt_ref[...] = jnp.zeros_like(lt_ref)
        acc_ref[...] = jnp.zeros_like(acc_ref)

        def tile(j, mask, r0=0):
            ks = pl.ds(pl.multiple_of(j * SB_KEYS, SB_KEYS), SB_KEYS)
            qs = pl.ds(r0, qb - r0)
            z = _dot(q_ref[qs, :], k_ref[ks, :], NT)
            ls, lm = _log_sigmoids(z)
            if mask is not None:
                lm = jnp.where(mask, lm, 0.0)
            later = lt_ref[qs, :]
            second = _running(lm[:, blk:], tri)
            first = _running(lm[:, :blk], tri)
            after_first = later + second[:, blk:]
            a = jnp.exp(ls + jnp.concatenate([first[:, :blk] + after_first, second[:, :blk] + later], axis=1))
            if mask is not None:
                a = jnp.where(mask, a, 0.0)
            acc_ref[qs, :] += _dot(a.astype(BF16), v_ref[ks, :])
            lt_ref[qs, :] = after_first + first[:, blk:]

        for s in reversed(range(per)):
            tile(i * per + s, _sb_mask(qb, s), s * SB_KEYS)

        def step(jj, _):
            for s in range(per):
                tile((i - jj) * per - 1 - s, None)
            return 0

        lax.fori_loop(0, i, step, 0)
        o_ref[...] = acc_ref[...].astype(BF16)

    qs = pl.BlockSpec((qb, HEAD_DIM), lambda h, i: (i, h))
    return pl.pallas_call(
        body, name=name, grid=(H, T // qb),
        in_specs=[qs, pl.BlockSpec((T, HEAD_DIM), lambda h, i: (0, H + h)),
                  pl.BlockSpec((T, HEAD_DIM), lambda h, i: (0, 2 * H + h))],
        out_specs=[qs, qs],
        out_shape=[jax.ShapeDtypeStruct((T, W), BF16), jax.ShapeDtypeStruct((T, W), F32)],
        scratch_shapes=[pltpu.VMEM((qb, HEAD_DIM), F32)],
        compiler_params=_params("parallel", "arbitrary"),
    )(qkv, qkv, qkv)


def _sb_bwd(qkv, do, ltot, W, name):
    T = qkv.shape[0]
    H = W // HEAD_DIM
    blk = ATT_BLOCK
    nkb = T // SB_KEYS
    qb = min(SB_QUERY_ROWS, T)
    per = qb // SB_KEYS

    def body(q_ref, k_ref, v_ref, do_ref, lt_ref, dq_ref, dkt_ref, dvt_ref, qt_ref, dot_ref, plm_ref, pg_ref):
        i = pl.program_id(1)

        @pl.when(i == 0)
        def _():
            dkt_ref[...] = jnp.zeros_like(dkt_ref)
            dvt_ref[...] = jnp.zeros_like(dvt_ref)

        qt_ref[...] = q_ref[...].astype(F32).T.astype(BF16)
        dot_ref[...] = do_ref[...].astype(F32).T.astype(BF16)
        upto = _tri_and_ones(lambda r, c: r <= c)
        before = _tri_and_ones(lambda r, c: r < c)
        plm_ref[...] = jnp.zeros_like(plm_ref)
        pg_ref[...] = jnp.zeros_like(pg_ref)
        dq_ref[...] = jnp.zeros_like(dq_ref)

        def tile(j, mask, r0=0):
            ks = pl.ds(pl.multiple_of(j * SB_KEYS, SB_KEYS), SB_KEYS)
            qs = pl.ds(r0, qb - r0)
            k = k_ref[ks, :]
            v = v_ref[ks, :]
            z = _dot(q_ref[qs, :], k, NT)
            ls, lm = _log_sigmoids(z)
            nsig = jnp.exp(lm)
            if mask is not None:
                lm = jnp.where(mask, lm, 0.0)
            earlier = plm_ref[qs, :]
            first = _running(lm[:, :blk], upto)
            second = _running(lm[:, blk:], upto)
            upto_first = earlier + first[:, blk:]
            seen = jnp.concatenate([first[:, :blk] + earlier, second[:, :blk] + upto_first], axis=1)
            ltot = lt_ref[qs, :]
            a = jnp.exp(ls + (jnp.concatenate([ltot, ltot], axis=1) - seen))
            if mask is not None:
                a = jnp.where(mask, a, 0.0)
            g = a * _dot(do_ref[qs, :], v, NT)
            g_earlier = pg_ref[qs, :]
            g_first = _running(g[:, :blk], before)
            g_second = _running(g[:, blk:], before)
            g_upto_first = g_earlier + g_first[:, blk:]
            gsum = jnp.concatenate([g_first[:, :blk] + g_earlier, g_second[:, :blk] + g_upto_first], axis=1)
            dz = g * nsig - gsum * jnp.exp(ls)
            if mask is not None:
                dz = jnp.where(mask, dz, 0.0)
            dzb = dz.astype(BF16)
            dkt_ref[j] += _dot(qt_ref[:, qs], dzb)
            dvt_ref[j] += _dot(dot_ref[:, qs], a.astype(BF16))
            dq_ref[qs, :] += _dot(dzb, k)
            plm_ref[qs, :] = upto_first + second[:, blk:]
            pg_ref[qs, :] = g_upto_first + g_second[:, blk:]

        def step(jj, _):
            for s in range(per):
                tile(jj * per + s, None)
            return 0

        lax.fori_loop(0, i, step, 0)
        for s in range(per):
            tile(i * per + s, _sb_mask(qb, s), s * SB_KEYS)

    qs = pl.BlockSpec((qb, HEAD_DIM), lambda h, i: (i, h))
    res = pl.BlockSpec((None, nkb, HEAD_DIM, SB_KEYS), lambda h, i: (h, 0, 0, 0))
    return pl.pallas_call(
        body, name=name, grid=(H, T // qb),
        in_specs=[qs, pl.BlockSpec((T, HEAD_DIM), lambda h, i: (0, H + h)),
                  pl.BlockSpec((T, HEAD_DIM), lambda h, i: (0, 2 * H + h)), qs, qs],
        out_specs=[qs, res, res],
        out_shape=[jax.ShapeDtypeStruct((T, W), F32)] + [jax.ShapeDtypeStruct((H, nkb, HEAD_DIM, SB_KEYS), F32)] * 2,
        scratch_shapes=[pltpu.VMEM((HEAD_DIM, qb), BF16), pltpu.VMEM((HEAD_DIM, qb), BF16),
                        pltpu.VMEM((qb, HEAD_DIM), F32), pltpu.VMEM((qb, HEAD_DIM), F32)],
        compiler_params=_params("parallel", "arbitrary"),
    )(qkv, qkv, qkv, do, ltot)


def _sb_pack(dq, dkt, dvt, name):
    T, W = dq.shape
    H = W // HEAD_DIM
    blk = SB_KEYS
    scale = HEAD_DIM ** -0.5

    def body(q_ref, kt_ref, vt_ref, o_ref):
        o_ref[:, pl.ds(0, W)] = (q_ref[...] * scale).astype(BF16)
        for h in range(H):
            o_ref[:, pl.ds(W + h * HEAD_DIM, HEAD_DIM)] = kt_ref[h].T.astype(BF16)
            o_ref[:, pl.ds(2 * W + h * HEAD_DIM, HEAD_DIM)] = vt_ref[h].T.astype(BF16)

    tr = pl.BlockSpec((H, None, HEAD_DIM, blk), lambda i: (0, i, 0, 0))
    return pl.pallas_call(
        body, name=name, grid=(T // blk,), in_specs=[pl.BlockSpec((blk, W), lambda i: (i, 0)), tr, tr],
        out_specs=pl.BlockSpec((blk, 3 * W), lambda i: (i, 0)),
        out_shape=jax.ShapeDtypeStruct((T, 3 * W), BF16), compiler_params=_params("parallel"),
    )(dq, dkt, dvt)


def _local_step(x, target, norms, sgu, comm):
    T, D = x.shape
    A = D // 2
    pre_mix, post_mix, pre_ffn, post_ffn = norms
    ln_g, ln_b, w_s, b_s = sgu
    b_t = b_s.T
    scale = HEAD_DIM ** -0.5

    def vec(p, layer):
        return comm.tie(p[layer:layer + 1])

    h0 = _prenorm(x, vec(pre_mix, 0), "prenorm0")
    comm.arrive("ab_in", after=h0)
    comm.land("ab_in")
    z = _matmul(h0, comm.weight("ab_in"), mode="nn", name="ab_in_fwd")
    comm.arrive("ab_out", after=z)
    cat = _sgu_fwd(z, ln_g, ln_b, w_s, b_t, "sgu_fwd")
    cat, o_dil, lse_dil = _dilated_forward(z, cat, A, "dilated_fwd")
    comm.land("ab_out", after=o_dil)
    y0 = _matmul(cat, comm.weight("ab_out"), mode="nn", name="ab_out_fwd")
    comm.arrive("w1_0", after=y0)
    x1, h1 = _postnorm_prenorm(x, y0, vec(post_mix, 0), vec(pre_ffn, 0), "norm_mix0")
    comm.land("w1_0", after=h1)
    comm.arrive("w2_0", after=h1)

    def relu2(acc, j):
        r = jnp.maximum(acc, 0.0)
        return r * r, 2.0 * r

    f0, r0 = _matmul(h1, comm.weight("w1_0"), mode="nn", name="ffn0_w1_fwd", out_dtype=(BF16, BF16), epi=relu2)
    comm.land("w2_0", after=f0)
    y1 = _matmul(f0, comm.weight("w2_0"), mode="nn", name="ffn0_w2_fwd")
    comm.arrive("sb", after=y1)
    x2, h2 = _postnorm_prenorm(x1, y1, vec(post_ffn, 0), vec(pre_mix, 1), "norm_ffn0")
    comm.land("sb", after=h2)

    tn_qkv = _tile(D, 1024)
    nq = D // tn_qkv

    def scale_q(acc, j):
        return jnp.where(j < nq, acc * scale, acc)

    qkv = _matmul(h2, comm.weight("sb_in"), mode="nn", name="sb_in_fwd", out_dtype=BF16, tn=tn_qkv, epi=scale_q)
    comm.arrive("ffn1", after=qkv)
    o_sb, ltot = _sb_fwd(qkv, D, "sb_fwd")
    comm.land("ffn1", after=o_sb)
    y2 = _matmul(o_sb, comm.weight("sb_out"), mode="nn", name="sb_out_fwd")
    x3, h3 = _postnorm_prenorm(x2, y2, vec(post_mix, 1), vec(pre_ffn, 1), "norm_mix1")
    f1, r1 = _matmul(h3, comm.weight("w1_1"), mode="nn", name="ffn1_w1_fwd", out_dtype=(BF16, BF16), epi=relu2)
    y3 = _matmul(f1, comm.weight("w2_1"), mode="nn", name="ffn1_w2_fwd")
    loss_tile, dx4, dy3, dg_post_ffn1 = _postnorm_loss(x3, y3, vec(post_ffn, 1), target, "norm_loss")
    loss = loss_tile[0, 0]

    def relu2_bwd(acc, j, r):
        return acc * r.astype(F32)

    def ffn_bwd(dy, h, f, r, layer):
        g_w2 = _matmul(f, dy, mode="tn", name=f"ffn{layer}_w2_wgrad", out_dtype=BF16)
        da = _matmul(dy, comm.weight(f"w2_{layer}"), mode="nt", name=f"ffn{layer}_w2_dgrad", out_dtype=BF16,
                     epi=relu2_bwd, extras=(r,))
        g_w1 = _matmul(h, da, mode="tn", name=f"ffn{layer}_w1_wgrad", out_dtype=BF16)
        comm.reduce(f"ffn{layer}", {f"w2_{layer}": g_w2, f"w1_{layer}": g_w1})
        return _matmul(da, comm.weight(f"w1_{layer}"), mode="nt", name=f"ffn{layer}_w1_dgrad", after=comm.started())

    dh3 = ffn_bwd(dy3, h3, f1, r1, 1)
    dx3, dg_pre_ffn1, dy2, dg_post_mix1 = _norm_bwd_pair(dx4, dh3, x3, vec(pre_ffn, 1), y2, vec(post_mix, 1),
                                                         "ffn1_sb_norm_bwd")
    g_sb_out = _matmul(o_sb, dy2, mode="tn", name="sb_out_wgrad", out_dtype=BF16)
    do_sb = _matmul(dy2, comm.weight("sb_out"), mode="nt", name="sb_out_dgrad", out_dtype=BF16)
    dq, dk, dv = _sb_bwd(qkv, do_sb, ltot, D, "sb_bwd")
    dqkv = _sb_pack(dq, dk, dv, "sb_pack")
    g_sb_in = _matmul(h2, dqkv, mode="tn", name="sb_in_wgrad", out_dtype=BF16)
    comm.reduce("sb", {"sb_out": g_sb_out, "sb_in": g_sb_in})
    dh2 = _matmul(dqkv, comm.weight("sb_in"), mode="nt", name="sb_in_dgrad", after=comm.started())
    dx2, dg_pre_mix1, dy1, dg_post_ffn0 = _norm_bwd_pair(dx3, dh2, x2, vec(pre_mix, 1), y1, vec(post_ffn, 0),
                                                         "sb_ffn0_norm_bwd")
    dh1 = ffn_bwd(dy1, h1, f0, r0, 0)
    dx1, dg_pre_ffn0, dy0, dg_post_mix0 = _norm_bwd_pair(dx2, dh1, x1, vec(pre_ffn, 0), y0, vec(post_mix, 0),
                                                         "ffn0_ab_norm_bwd")
    g_ab_out = _matmul(cat, dy0, mode="tn", name="ab_out_wgrad", out_dtype=BF16)
    comm.reduce("ab_out", {"ab_out": g_ab_out})
    dcat = _matmul(dy0, comm.weight("ab_out"), mode="nt", name="ab_out_dgrad", after=comm.started())
    duv, d_ln_g, d_ln_b, d_w_s, d_b_t = _sgu_bwd(z, dcat, ln_g, ln_b, w_s, b_t, "sgu_bwd")
    dz = _join_columns([duv, *_dilated_backward(z, dcat, o_dil, lse_dil, A, "dilated_bwd")], "join_dz")
    g_ab_in = _matmul(h0, dz, mode="tn", name="ab_in_wgrad", out_dtype=BF16)
    comm.reduce("ab_in", {"ab_in": g_ab_in})
    dh0 = _matmul(dz, comm.weight("ab_in"), mode="nt", name="ab_in_dgrad", after=comm.started())
    dx0, dg_pre_mix0 = _prenorm_bwd(dx1, dh0, x, vec(pre_mix, 0), "ab_prenorm_bwd")

    small = {
        "pre_mix": jnp.concatenate([dg_pre_mix0, dg_pre_mix1], axis=0),
        "post_mix": jnp.concatenate([dg_post_mix0, dg_post_mix1], axis=0),
        "pre_ffn": jnp.concatenate([dg_pre_ffn0, dg_pre_ffn1], axis=0),
        "post_ffn": jnp.concatenate([dg_post_ffn0, dg_post_ffn1], axis=0),
        "ln_g": d_ln_g, "ln_b": d_ln_b, "w_s": d_w_s, "b_s": d_b_t.T,
    }
    return loss, dx0, small


MESH_ID = pl.DeviceIdType.MESH
ANY = pl.BlockSpec(memory_space=pl.ANY)


def _coords():
    return lax.axis_index("x"), lax.axis_index("y"), lax.axis_index("c")


def _shard_of(ref, kind, p):
    if kind == "col":
        n = ref.shape[1] // N_DEV
        return ref.at[:, pl.ds(pl.multiple_of(p * n, 128), n)]
    r = ref.shape[0] // N_DEV
    return ref.at[pl.ds(pl.multiple_of(p * r, 16), r), :]


def _full_shape(shard, kind):
    if kind == "col":
        return (shard.shape[0], shard.shape[1] * N_DEV)
    return (shard.shape[0] * N_DEV, shard.shape[1])


def _place(shards, layer, kind, block, after, name):
    _, rows, cols = shards.shape
    tr = _tile(rows, 512)

    def body(b_ref, s_ref, after_ref, o_ref):
        o_ref[...] = s_ref[...].astype(BF16)

    if kind == "col":
        out = pl.BlockSpec((tr, cols), lambda i, b_ref: (i, b_ref[0]))
    else:
        out = pl.BlockSpec((tr, cols), lambda i, b_ref: (b_ref[0] * (rows // tr) + i, 0))
    return pl.pallas_call(
        body, name=name,
        grid_spec=pltpu.PrefetchScalarGridSpec(
            num_scalar_prefetch=1, grid=(rows // tr,),
            in_specs=[pl.BlockSpec((None, tr, cols), lambda i, b_ref: (layer, i, 0)), ANY], out_specs=out),
        out_shape=jax.ShapeDtypeStruct(_full_shape(shards[0], kind), BF16),
        compiler_params=_params("parallel"),
    )(block, shards, after)


HBM = pl.BlockSpec(memory_space=pltpu.HBM)
SEM = pl.BlockSpec(memory_space=pltpu.SEMAPHORE)
FLOWS = pltpu.SideEffectType.DATAFLOW_SIDE_EFFECTING


def _in_hbm(a):
    return pltpu.with_memory_space_constraint(a, pltpu.HBM)


def _hbm_like(bufs):
    return [pltpu.HBM(b.shape, b.dtype) for b in bufs]


def _copies_start(name, bufs, plan, n, after):
    nb = len(bufs)

    def body(*refs):
        send_sems, recv_sems, token = refs[nb + 1], refs[nb + 2], refs[-1]
        for cp in plan(refs[:nb], send_sems, recv_sems):
            cp.start()
        token[...] = jnp.zeros_like(token)

    out = pl.pallas_call(
        body, name=name, in_specs=[HBM] * nb + [ANY],
        out_specs=[SEM, SEM] + [HBM] * nb + [pl.BlockSpec(memory_space=pltpu.VMEM)],
        out_shape=[pltpu.SemaphoreType.DMA((n,)), pltpu.SemaphoreType.DMA((n,))] + _hbm_like(bufs)
        + [jax.ShapeDtypeStruct((8, 128), F32)],
        input_output_aliases={i: 2 + i for i in range(nb)},
        compiler_params=pltpu.CompilerParams(has_side_effects=FLOWS),
    )(*[_in_hbm(b) for b in bufs], after)
    return (out[0], out[1]), list(out[2:2 + nb]), out[-1]


def _copies_wait(name, bufs, sems, after, plan):
    nb = len(bufs)

    def body(*refs):
        for cp in plan(refs[:nb], refs[nb], refs[nb + 1]):
            cp.wait_send()
            cp.wait_recv()

    out = pl.pallas_call(
        body, name=name, in_specs=[HBM] * nb + [SEM, SEM, ANY], out_specs=[HBM] * nb,
        out_shape=_hbm_like(bufs), input_output_aliases={i: i for i in range(nb)},
        compiler_params=pltpu.CompilerParams(has_side_effects=FLOWS),
    )(*bufs, *sems, after)
    return list(out)


def _copies_wait_start(name, bufs, sems, after, plan, next_plan, n_next):
    nb = len(bufs)

    def body(*refs):
        ins = refs[:nb]
        for cp in plan(ins, refs[nb], refs[nb + 1]):
            cp.wait_send()
            cp.wait_recv()
        send_sems, recv_sems, token = refs[nb + 3], refs[nb + 4], refs[-1]
        for cp in next_plan(ins, send_sems, recv_sems):
            cp.start()
        token[...] = jnp.zeros_like(token)

    out = pl.pallas_call(
        body, name=name, in_specs=[HBM] * nb + [SEM, SEM, ANY],
        out_specs=[SEM, SEM] + [HBM] * nb + [pl.BlockSpec(memory_space=pltpu.VMEM)],
        out_shape=[pltpu.SemaphoreType.DMA((n_next,)), pltpu.SemaphoreType.DMA((n_next,))] + _hbm_like(bufs)
        + [jax.ShapeDtypeStruct((8, 128), F32)],
        input_output_aliases={i: 2 + i for i in range(nb)},
        compiler_params=pltpu.CompilerParams(has_side_effects=FLOWS),
    )(*bufs, *sems, after)
    return (out[0], out[1]), list(out[2:2 + nb]), out[-1]


def _gather_plans(kinds):
    nt = len(kinds)

    def slot(refs, t, px, py, pc):
        return _shard_of(refs[t], kinds[t], 4 * px + 2 * py + pc)

    def to_chips(refs, send_sems, recv_sems):
        x, y, c = _coords()
        peers = [(x, y, 1 - c), (1 - x, y, c), (x, 1 - y, c), (1 - x, 1 - y, c)]
        return [pltpu.make_async_remote_copy(
            src_ref=slot(refs, t, x, y, c), dst_ref=slot(refs, t, x, y, c), send_sem=send_sems.at[4 * t + k],
            recv_sem=recv_sems.at[4 * t + k], device_id=peer, device_id_type=MESH_ID)
            for t in range(nt) for k, peer in enumerate(peers)]

    def to_sibling(refs, send_sems, recv_sems):
        x, y, c = _coords()
        chips = [(1 - x, y), (x, 1 - y), (1 - x, 1 - y)]
        return [pltpu.make_async_remote_copy(
            src_ref=slot(refs, t, *chip, c), dst_ref=slot(refs, t, *chip, c), send_sem=send_sems.at[3 * t + j],
            recv_sem=recv_sems.at[3 * t + j], device_id=(x, y, 1 - c), device_id_type=MESH_ID)
            for t in range(nt) for j, chip in enumerate(chips)]

    return to_chips, to_sibling


def _shard_shape(full, kind):
    if kind == "col":
        return (full.shape[0], full.shape[1] // N_DEV)
    return (full.shape[0] // N_DEV, full.shape[1])


def _scatter_plan(kinds):
    nt = len(kinds)

    def plan(refs, send_sems, recv_sems):
        x, y, c = _coords()
        copies = []
        for t in range(nt):
            for k in range(1, N_DEV):
                px = 1 - x if (k >> 2) & 1 else x
                py = 1 - y if (k >> 1) & 1 else y
                pc = 1 - c if k & 1 else c
                copies.append(pltpu.make_async_remote_copy(
                    src_ref=_shard_of(refs[t], kinds[t], 4 * px + 2 * py + pc),
                    dst_ref=refs[nt + t].at[4 * x + 2 * y + c],
                    send_sem=send_sems.at[7 * t + k - 1], recv_sem=recv_sems.at[7 * t + k - 1],
                    device_id=(px, py, pc), device_id_type=MESH_ID))
        return copies
    return plan


def _partial_specs(full, kind, tr):
    rows, cols = _shard_shape(full, kind)
    steps = rows // tr
    if kind == "col":
        own = pl.BlockSpec((tr, cols), lambda i, w: (i, w[0]))
    else:
        own = pl.BlockSpec((tr, cols), lambda i, w: (w[0] * steps + i, 0))
    return [own] + [pl.BlockSpec((None, tr, cols), lambda i, w, k=k: (w[k], i, 0)) for k in range(1, N_DEV)]


def _small_plan(refs, send_sems, recv_sems):
    x, y, c = _coords()
    copies = []
    for k in range(1, N_DEV):
        peer = (1 - x if (k >> 2) & 1 else x, 1 - y if (k >> 1) & 1 else y, 1 - c if k & 1 else c)
        copies.append(pltpu.make_async_remote_copy(
            src_ref=refs[0], dst_ref=refs[1].at[4 * x + 2 * y + c], send_sem=send_sems.at[k - 1],
            recv_sem=recv_sems.at[k - 1], device_id=peer, device_id_type=MESH_ID))
    return copies


def _sum_small(vec, land):
    def body(v_ref, l_ref, o_ref):
        x, y, c = _coords()
        me = 4 * x + 2 * y + c
        total = jnp.where(me == 0, v_ref[...], l_ref[0])
        for p in range(1, N_DEV):
            total = total + jnp.where(me == p, v_ref[...], l_ref[p])
        o_ref[...] = total

    whole = pl.BlockSpec(memory_space=pltpu.VMEM)
    return pl.pallas_call(
        body, name="sum_small", in_specs=[whole, whole], out_specs=whole,
        out_shape=jax.ShapeDtypeStruct(vec.shape, F32),
        compiler_params=pltpu.CompilerParams(vmem_limit_bytes=VMEM_LIMIT),
    )(vec, land)


def _adamw_math(w, g, m, v):
    m = ADAM_B1 * m + (1.0 - ADAM_B1) * g
    v = ADAM_B2 * v + (1.0 - ADAM_B2) * (g * g)
    m_hat = m / (1.0 - ADAM_B1 ** ADAM_STEP)
    v_hat = v / (1.0 - ADAM_B2 ** ADAM_STEP)
    delta = -ADAM_LR * (m_hat / (jnp.sqrt(v_hat) + ADAM_EPS) + ADAM_WD * w)
    return delta, m, v


def _adamw(w, grads, kind, where, m, v, after, name):
    layers, rows, cols = w.shape
    tr = _tile(rows, 128)
    out = None
    for layer, (grad, land) in enumerate(grads):
        def body(w_ref, *refs):
            parts, (x_ref, m_ref, v_ref) = refs[:N_DEV], refs[N_DEV:N_DEV + 3]
            g_ref, d_ref, mo_ref, vo_ref = refs[-4:]
            g = parts[0][...].astype(F32)
            for p_ref in parts[1:]:
                g = g + p_ref[...].astype(F32)
            g_ref[...] = g
            d_ref[...], mo_ref[...], vo_ref[...] = _adamw_math(x_ref[...], g, m_ref[...], v_ref[...])

        blk = pl.BlockSpec((None, tr, cols), lambda i, w_, layer=layer: (layer, i, 0))
        earlier = [] if out is None else list(out)
        out = pl.pallas_call(
            body, name=f"{name}_{layer}",
            grid_spec=pltpu.PrefetchScalarGridSpec(
                num_scalar_prefetch=1, grid=(rows // tr,),
                in_specs=_partial_specs(grad, kind, tr) + [blk] * 3 + [ANY] * (len(earlier) + len(after)),
                out_specs=[blk] * 4),
            out_shape=[jax.ShapeDtypeStruct((layers, rows, cols), F32)] * 4,
            input_output_aliases={N_DEV + 4 + k: k for k in range(len(earlier))},
            compiler_params=_params("parallel"),
        )(where, grad, *[land] * (N_DEV - 1), w, m, v, *earlier, *after)
    return out


def _adamw_small(w, g, m, v):
    def body(w_ref, g_ref, m_ref, v_ref, d_ref, mo_ref, vo_ref):
        d_ref[...], mo_ref[...], vo_ref[...] = _adamw_math(w_ref[...], g_ref[...], m_ref[...], v_ref[...])

    whole = pl.BlockSpec(memory_space=pltpu.VMEM)
    return pl.pallas_call(
        body, name="adamw_small", in_specs=[whole] * 4, out_specs=[whole] * 3,
        out_shape=[jax.ShapeDtypeStruct(w.shape, F32)] * 3,
        compiler_params=pltpu.CompilerParams(vmem_limit_bytes=VMEM_LIMIT),
    )(w, g, m, v)


def _pack(arrays):
    rows = []
    for a in arrays:
        flat = a.reshape(-1)
        pad = (-flat.shape[0]) % 1024
        rows.append(jnp.pad(flat, (0, pad)).reshape(-1, 128))
    return jnp.concatenate(rows, axis=0)


def _unpack(packed, like):
    out, r = [], 0
    for a in like:
        n = math.prod(a.shape)
        nr = (n + 1023) // 1024 * 8
        out.append(packed[r:r + nr].reshape(-1)[:n].reshape(a.shape))
        r += nr
    return out


KIND = {"ab_in": "col", "ab_out": "row", "sb_in": "col", "sb_out": "row",
        "w1_0": "col", "w1_1": "col", "w2_0": "row", "w2_1": "row"}
GATHERS = {"ab_in": ("ab_in",), "ab_out": ("ab_out",), "w1_0": ("w1_0",), "w2_0": ("w2_0",),
           "sb": ("sb_in", "sb_out"), "ffn1": ("w1_1", "w2_1")}


class _Exchange:
    def __init__(self, shards):
        x, y, c = _coords()
        me = (4 * x + 2 * y + c).astype(jnp.int32)
        self.where = jnp.stack([jnp.bitwise_xor(me, k) for k in range(N_DEV)])
        self.full = {}
        self.tokens = []
        self.gathers = {}
        self.scatters = {}
        self.settled = {}
        block = me.reshape(1)
        after = block
        for key, group in GATHERS.items():
            for n in group:
                self.full[n] = _place(*shards[n], KIND[n], block, after, f"place_{n}")
            to_chips, to_sibling = _gather_plans([KIND[n] for n in group])
            bufs = [self.full[n] for n in group]
            sems, bufs, after = _copies_start(f"gather_start_{key}", bufs, to_chips, 4 * len(group), after)
            self.tokens.append(after)
            self.gathers[key] = (group, sems, bufs, to_chips, to_sibling)

    def tie(self, small):
        for token in self.tokens:
            small = small + token[0:1, 0:1]
        self.tokens = []
        return small

    def started(self):
        return tuple(self.tokens)

    def weight(self, name):
        return self.full[name]

    def arrive(self, key, after):
        group, sems, bufs, to_chips, to_sibling = self.gathers[key]
        sems, bufs, token = _copies_wait_start(f"gather_pass_{key}", bufs, sems, after, to_chips, to_sibling,
                                               3 * len(group))
        self.tokens.append(token)
        self.gathers[key] = (group, sems, bufs, token, to_sibling)

    def land(self, key, after=None):
        group, sems, bufs, token, to_sibling = self.gathers.pop(key)
        after = token if after is None else after
        self.full.update(zip(group, _copies_wait(f"gather_done_{key}", bufs, sems, after, to_sibling)))

    def reduce(self, key, grads):
        names = list(grads)
        kinds = [KIND[n] for n in names]
        full = [grads[n] for n in names]
        lands = [lax.empty((N_DEV,) + _shard_shape(g, k), BF16) for g, k in zip(full, kinds)]
        plan = _scatter_plan(kinds)
        sems, bufs, token = _copies_start(f"scatter_start_{key}", full + lands, plan, (N_DEV - 1) * len(names),
                                          self.where)
        self.tokens.append(token)
        self.scatters[key] = (names, sems, bufs, plan)

    def settle(self, keys, after):
        for key in keys:
            names, sems, bufs, plan = self.scatters.pop(key)
            bufs = _copies_wait(f"scatter_done_{key}", bufs, sems, after, plan)
            self.settled.update({n: t for n, *t in zip(names, bufs[:len(names)], bufs[len(names):])})
        return self.settled


SMALL = ("norm_pre_mix", "norm_post_mix", "norm_pre_ffn", "norm_post_ffn", "sgu_ln_g", "sgu_ln_b", "sgu_w", "sgu_b")
ORDER = ("norm_pre_mix", "norm_post_mix", "norm_pre_ffn", "norm_post_ffn", "ab_w_in", "sgu_ln_g", "sgu_ln_b", "sgu_w",
         "sgu_b", "ab_w_out", "sb_w_in", "sb_w_out", "ffn_w1", "ffn_w2")


def kernel(x, norm_pre_mix, norm_post_mix, norm_pre_ffn, norm_post_ffn, ab_w_in, sgu_ln_g, sgu_ln_b, sgu_w, sgu_b, ab_w_out, sb_w_in, sb_w_out, ffn_w1, ffn_w2, loss_target, m_norm_pre_mix, m_norm_post_mix, m_norm_pre_ffn, m_norm_post_ffn, m_ab_w_in, m_sgu_ln_g, m_sgu_ln_b, m_sgu_w, m_sgu_b, m_ab_w_out, m_sb_w_in, m_sb_w_out, m_ffn_w1, m_ffn_w2, v_norm_pre_mix, v_norm_post_mix, v_norm_pre_ffn, v_norm_post_ffn, v_ab_w_in, v_sgu_ln_g, v_sgu_ln_b, v_sgu_w, v_sgu_b, v_ab_w_out, v_sb_w_in, v_sb_w_out, v_ffn_w1, v_ffn_w2):
    W = dict(norm_pre_mix=norm_pre_mix, norm_post_mix=norm_post_mix, norm_pre_ffn=norm_pre_ffn,
             norm_post_ffn=norm_post_ffn, ab_w_in=ab_w_in, sgu_ln_g=sgu_ln_g, sgu_ln_b=sgu_ln_b, sgu_w=sgu_w,
             sgu_b=sgu_b, ab_w_out=ab_w_out, sb_w_in=sb_w_in, sb_w_out=sb_w_out, ffn_w1=ffn_w1, ffn_w2=ffn_w2)
    M = dict(norm_pre_mix=m_norm_pre_mix, norm_post_mix=m_norm_post_mix, norm_pre_ffn=m_norm_pre_ffn,
             norm_post_ffn=m_norm_post_ffn, ab_w_in=m_ab_w_in, sgu_ln_g=m_sgu_ln_g, sgu_ln_b=m_sgu_ln_b,
             sgu_w=m_sgu_w, sgu_b=m_sgu_b, ab_w_out=m_ab_w_out, sb_w_in=m_sb_w_in, sb_w_out=m_sb_w_out,
             ffn_w1=m_ffn_w1, ffn_w2=m_ffn_w2)
    V = dict(norm_pre_mix=v_norm_pre_mix, norm_post_mix=v_norm_post_mix, norm_pre_ffn=v_norm_pre_ffn,
             norm_post_ffn=v_norm_post_ffn, ab_w_in=v_ab_w_in, sgu_ln_g=v_sgu_ln_g, sgu_ln_b=v_sgu_ln_b,
             sgu_w=v_sgu_w, sgu_b=v_sgu_b, ab_w_out=v_ab_w_out, sb_w_in=v_sb_w_in, sb_w_out=v_sb_w_out,
             ffn_w1=v_ffn_w1, ffn_w2=v_ffn_w2)

    shards = {"ab_in": (ab_w_in, 0), "ab_out": (ab_w_out, 0), "w1_0": (ffn_w1, 0), "w2_0": (ffn_w2, 0),
              "sb_in": (sb_w_in, 0), "sb_out": (sb_w_out, 0), "w1_1": (ffn_w1, 1), "w2_1": (ffn_w2, 1)}
    comm = _Exchange(shards)
    norms = (norm_pre_mix, norm_post_mix, norm_pre_ffn, norm_post_ffn)
    sgu = (sgu_ln_g, sgu_ln_b, sgu_w[0], sgu_b[0])
    loss, dx, small = _local_step(x[0], loss_target[0], norms, sgu, comm)

    out = {}

    def update(name, layers, landed, after=()):
        out[name] = _adamw(W[name], [landed[n] for n in layers], KIND[layers[0]], comm.where, M[name], V[name],
                           after, f"adamw_{name}")

    landed = comm.settle(("ffn1", "sb", "ffn0"), after=dx)
    small_g = [small["pre_mix"], small["post_mix"], small["pre_ffn"], small["post_ffn"], small["ln_g"],
               small["ln_b"], small["w_s"][None], small["b_s"][None]]
    mine = _pack(small_g + [loss.reshape(1)])
    sems, bufs, token = _copies_start("small_start", [mine, jnp.zeros((N_DEV,) + mine.shape, F32)], _small_plan,
                                      N_DEV - 1, dx)
    for name, layers in (("sb_w_in", ["sb_in"]), ("sb_w_out", ["sb_out"]), ("ffn_w1", ["w1_0", "w1_1"]),
                         ("ffn_w2", ["w2_0", "w2_1"])):
        update(name, layers, landed, after=(token,))
    summed = _sum_small(*_copies_wait("small_done", bufs, sems, out["ffn_w2"][3], _small_plan))
    g_small, loss = summed[:-8], summed[-8, 0]
    landed = comm.settle(("ab_out", "ab_in"), after=g_small)
    update("ab_w_out", ["ab_out"], landed)
    update("ab_w_in", ["ab_in"], landed)
    res = _adamw_small(_pack([W[n] for n in SMALL]), g_small, _pack([M[n] for n in SMALL]),
                       _pack([V[n] for n in SMALL]))
    like = [W[n] for n in SMALL]
    for n, *vals in zip(SMALL, *[_unpack(r, like) for r in [g_small] + list(res)]):
        out[n] = vals

    return (loss, dx[None], *[out[n][0] for n in ORDER], *[out[n][1] for n in ORDER],
            *[out[n][2] for n in ORDER], *[out[n][3] for n in ORDER])
```

```python
import math

import jax
import jax.numpy as jnp
from jax import lax
from jax.experimental import pallas as pl
from jax.experimental.pallas import tpu as pltpu

F32 = jnp.float32
BF16 = jnp.bfloat16

HEAD_DIM = 128
CHUNK = 128
ATT_BLOCK = 128
DILATED_PAIRS = ((128, 1), (512, 4), (2048, 16))
RMS_EPS = 1e-6
LN_EPS = 1e-5
ADAM_LR = 0.001
ADAM_B1 = 0.9
ADAM_B2 = 0.999
ADAM_EPS = 1e-08
ADAM_WD = 0.01
ADAM_STEP = 10
N_DEV = 8
MASKED = -1e30

V7X_VMEM_BYTES = 64 * 1024 * 1024
VMEM_LIMIT = V7X_VMEM_BYTES - 8 * 1024 * 1024

NN = (((1,), (0,)), ((), ()))
NT = (((1,), (1,)), ((), ()))
TN = (((0,), (0,)), ((), ()))


def _params(*sem):
    return pltpu.CompilerParams(dimension_semantics=sem, vmem_limit_bytes=VMEM_LIMIT)


def _dot(a, b, dims=NN):
    return lax.dot_general(a, b, dims, preferred_element_type=F32)


def _tile(n, preferred):
    if n <= preferred:
        return n
    t = preferred - preferred % 128
    while n % t:
        t -= 128
    assert t > 0, (n, preferred)
    return t


def _matmul(a, b, *, mode, name, out_dtype=F32, tm=1024, tn=1024, tk=2048, epi=None, extras=(), after=()):
    if mode == "nn":
        (M, K), N = a.shape, b.shape[1]
    elif mode == "nt":
        (M, K), N = a.shape, b.shape[0]
    else:
        (K, M), N = a.shape, b.shape[1]
    tm, tn, tk = _tile(M, tm), _tile(N, tn), _tile(K, tk)
    nk = K // tk
    if mode == "tn":
        a_spec = pl.BlockSpec((tk, tm), lambda i, j, k: (k, i))
    else:
        a_spec = pl.BlockSpec((tm, tk), lambda i, j, k: (i, k))
    if mode == "nt":
        b_spec = pl.BlockSpec((tn, tk), lambda i, j, k: (j, k))
    else:
        b_spec = pl.BlockSpec((tk, tn), lambda i, j, k: (k, j))
    o_spec = pl.BlockSpec((tm, tn), lambda i, j, k: (i, j))
    dims = {"nn": NN, "nt": NT, "tn": TN}[mode]
    n_extra = len(extras)
    n_in = n_extra + len(after)
    several = isinstance(out_dtype, tuple)
    n_out = len(out_dtype) if several else 1

    def finish(acc, rest):
        outs = acc if epi is None else epi(acc, pl.program_id(1), *[r[...] for r in rest[:n_extra]])
        for o_ref, o in zip(rest[n_in:n_in + n_out], outs if several else (outs,)):
            o_ref[...] = o.astype(o_ref.dtype)

    if nk == 1:
        def body(a_ref, b_ref, *rest):
            finish(_dot(a_ref[...], b_ref[...], dims), rest)
        scratch = []
    else:
        def body(a_ref, b_ref, *rest):
            acc_ref = rest[n_in + n_out]
            k = pl.program_id(2)

            @pl.when(k == 0)
            def _():
                acc_ref[...] = jnp.zeros_like(acc_ref)

            acc_ref[...] += _dot(a_ref[...], b_ref[...], dims)

            @pl.when(k == nk - 1)
            def _():
                finish(acc_ref[...], rest)
        scratch = [pltpu.VMEM((tm, tn), F32)]

    shapes = [jax.ShapeDtypeStruct((M, N), d) for d in (out_dtype if several else (out_dtype,))]
    return pl.pallas_call(
        body,
        name=name,
        grid=(M // tm, N // tn, nk),
        in_specs=[a_spec, b_spec] + [o_spec] * n_extra + [ANY] * len(after),
        out_specs=[o_spec] * n_out if several else o_spec,
        out_shape=shapes if several else shapes[0],
        scratch_shapes=scratch,
        compiler_params=_params("parallel", "parallel", "arbitrary"),
    )(a, b, *extras, *after)


ROWS = 512


def _rms(x):
    return lax.rsqrt(jnp.mean(x * x, axis=-1, keepdims=True) + RMS_EPS)


def _prenorm(x, g, name):
    T, D = x.shape

    def body(x_ref, g_ref, h_ref):
        xv = x_ref[...]
        h_ref[...] = (xv * _rms(xv) * g_ref[...]).astype(BF16)

    row = pl.BlockSpec((ROWS, D), lambda i: (i, 0))
    vec = pl.BlockSpec((1, D), lambda i: (0, 0))
    return pl.pallas_call(
        body, name=name, grid=(T // ROWS,), in_specs=[row, vec], out_specs=row,
        out_shape=jax.ShapeDtypeStruct((T, D), BF16), compiler_params=_params("parallel"),
    )(x, g)


def _postnorm_prenorm(x, y, g_post, g_pre, name):
    T, D = x.shape

    def body(x_ref, y_ref, gp_ref, gn_ref, xo_ref, h_ref):
        yv = y_ref[...]
        xn = x_ref[...] + yv * _rms(yv) * gp_ref[...]
        xo_ref[...] = xn
        h_ref[...] = (xn * _rms(xn) * gn_ref[...]).astype(BF16)

    row = pl.BlockSpec((ROWS, D), lambda i: (i, 0))
    vec = pl.BlockSpec((1, D), lambda i: (0, 0))
    return pl.pallas_call(
        body, name=name, grid=(T // ROWS,), in_specs=[row, row, vec, vec], out_specs=[row, row],
        out_shape=[jax.ShapeDtypeStruct((T, D), F32), jax.ShapeDtypeStruct((T, D), BF16)],
        compiler_params=_params("parallel"),
    )(x, y, g_post, g_pre)


def _postnorm_grads(dn, yh, r, g):
    gd = dn * g
    return r * (gd - yh * jnp.mean(yh * gd, axis=-1, keepdims=True)), dn * yh


def _postnorm_loss(x, y, g_post, target, name):
    T, D = x.shape

    def body(x_ref, y_ref, gp_ref, t_ref, loss_ref, dx_ref, dy_ref, dg_ref):
        @pl.when(pl.program_id(0) == 0)
        def _():
            loss_ref[...] = jnp.zeros_like(loss_ref)
            dg_ref[...] = jnp.zeros_like(dg_ref)

        yv = y_ref[...]
        r = _rms(yv)
        yh = yv * r
        err = x_ref[...] + yh * gp_ref[...] - t_ref[...]
        dx = err * (1.0 / D)
        dx_ref[...] = dx
        loss_ref[...] += 0.5 * jnp.sum(jnp.sum(err * err, axis=-1, keepdims=True) * (1.0 / D))
        dy, dg = _postnorm_grads(dx, yh, r, gp_ref[...])
        dy_ref[...] = dy.astype(BF16)
        dg_ref[...] += jnp.sum(dg, axis=0, keepdims=True)

    row = pl.BlockSpec((ROWS, D), lambda i: (i, 0))
    vec = pl.BlockSpec((1, D), lambda i: (0, 0))
    acc = pl.BlockSpec((8, 128), lambda i: (0, 0))
    return pl.pallas_call(
        body, name=name, grid=(T // ROWS,), in_specs=[row, row, vec, row], out_specs=[acc, row, row, vec],
        out_shape=[jax.ShapeDtypeStruct((8, 128), F32), jax.ShapeDtypeStruct((T, D), F32),
                   jax.ShapeDtypeStruct((T, D), BF16), jax.ShapeDtypeStruct((1, D), F32)],
        compiler_params=_params("arbitrary"),
    )(x, y, g_post, target)


def _norm_bwd_pair(dx_out, dh, x, g_pre, y_prev, g_post_prev, name):
    T, D = x.shape

    def body(dxo_ref, dh_ref, x_ref, g_ref, y_ref, gp_ref, dx_ref, dg_ref, dy_ref, dgp_ref):
        @pl.when(pl.program_id(0) == 0)
        def _():
            dg_ref[...] = jnp.zeros_like(dg_ref)
            dgp_ref[...] = jnp.zeros_like(dgp_ref)

        xv, dhv = x_ref[...], dh_ref[...]
        r = _rms(xv)
        xh = xv * r
        gd = dhv * g_ref[...]
        dx = dxo_ref[...] + r * (gd - xh * jnp.mean(xh * gd, axis=-1, keepdims=True))
        dx_ref[...] = dx
        dg_ref[...] += jnp.sum(dhv * xh, axis=0, keepdims=True)
        yv = y_ref[...]
        ry = _rms(yv)
        dy, dgp = _postnorm_grads(dx, yv * ry, ry, gp_ref[...])
        dy_ref[...] = dy.astype(BF16)
        dgp_ref[...] += jnp.sum(dgp, axis=0, keepdims=True)

    row = pl.BlockSpec((ROWS, D), lambda i: (i, 0))
    vec = pl.BlockSpec((1, D), lambda i: (0, 0))
    return pl.pallas_call(
        body, name=name, grid=(T // ROWS,), in_specs=[row, row, row, vec, row, vec],
        out_specs=[row, vec, row, vec],
        out_shape=[jax.ShapeDtypeStruct((T, D), F32), jax.ShapeDtypeStruct((1, D), F32),
                   jax.ShapeDtypeStruct((T, D), BF16), jax.ShapeDtypeStruct((1, D), F32)],
        compiler_params=_params("arbitrary"),
    )(dx_out, dh, x, g_pre, y_prev, g_post_prev)


def _prenorm_bwd(dx_out, dh, x, g_pre, name):
    T, D = x.shape

    def body(dxo_ref, dh_ref, x_ref, g_ref, dx_ref, dg_ref):
        @pl.when(pl.program_id(0) == 0)
        def _():
            dg_ref[...] = jnp.zeros_like(dg_ref)

        xv, dhv = x_ref[...], dh_ref[...]
        r = _rms(xv)
        xh = xv * r
        gd = dhv * g_ref[...]
        dx_ref[...] = dxo_ref[...] + r * (gd - xh * jnp.mean(xh * gd, axis=-1, keepdims=True))
        dg_ref[...] += jnp.sum(dhv * xh, axis=0, keepdims=True)

    row = pl.BlockSpec((ROWS, D), lambda i: (i, 0))
    vec = pl.BlockSpec((1, D), lambda i: (0, 0))
    return pl.pallas_call(
        body, name=name, grid=(T // ROWS,), in_specs=[row, row, row, vec], out_specs=[row, vec],
        out_shape=[jax.ShapeDtypeStruct((T, D), F32), jax.ShapeDtypeStruct((1, D), F32)],
        compiler_params=_params("arbitrary"),
    )(dx_out, dh, x, g_pre)


_INV_SQRT2 = 1.0 / math.sqrt(2.0)
_INV_SQRT2PI = 1.0 / math.sqrt(2.0 * math.pi)


def _gelu(x):
    return 0.5 * x * (1.0 + lax.erf(x * _INV_SQRT2))


def _gelu_grad(x):
    return 0.5 * (1.0 + lax.erf(x * _INV_SQRT2)) + x * jnp.exp(-0.5 * x * x) * _INV_SQRT2PI


def _layernorm_stats(x):
    mu = jnp.mean(x, axis=-1, keepdims=True)
    xc = x - mu
    rstd = lax.rsqrt(jnp.mean(xc * xc, axis=-1, keepdims=True) + LN_EPS)
    return xc * rstd, rstd


def _tril_mask():
    i = lax.broadcasted_iota(jnp.int32, (CHUNK, CHUNK), 0)
    j = lax.broadcasted_iota(jnp.int32, (CHUNK, CHUNK), 1)
    return j <= i


SGU_ROWS = 512


def _sgu_fwd(z, ln_g, ln_b, w_s, b_t, name):
    T = z.shape[0]
    A = ln_g.shape[1]
    G = A // 128
    rows = min(SGU_ROWS, T)

    def body(u_ref, v_ref, g_ref, b_ref, w_ref, bt_ref, o_ref):
        mask = _tril_mask()
        for c in range(rows // CHUNK):
            rs = pl.ds(c * CHUNK, CHUNK)
            xh, _ = _layernorm_stats(_gelu(v_ref[rs, :]))
            vn = (xh * g_ref[...] + b_ref[...]).astype(BF16)
            for g in range(G):
                cs = pl.ds(g * 128, 128)
                w = jnp.where(mask, w_ref[g], 0.0).astype(BF16)
                mixed = _dot(w, vn[:, g * 128:(g + 1) * 128]) + bt_ref[:, g:g + 1]
                o_ref[rs, cs] = (_gelu(u_ref[rs, cs]) * mixed).astype(BF16)

    return pl.pallas_call(
        body, name=name, grid=(T // rows,),
        in_specs=[
            pl.BlockSpec((rows, A), lambda i: (i, 0)),
            pl.BlockSpec((rows, A), lambda i: (i, 1)),
            pl.BlockSpec((1, A), lambda i: (0, 0)),
            pl.BlockSpec((1, A), lambda i: (0, 0)),
            pl.BlockSpec((G, CHUNK, CHUNK), lambda i: (0, 0, 0)),
            pl.BlockSpec((CHUNK, G), lambda i: (0, 0)),
        ],
        out_specs=pl.BlockSpec((rows, A), lambda i: (i, 0)),
        out_shape=jax.ShapeDtypeStruct((T, 2 * A), BF16),
        compiler_params=_params("parallel"),
    )(z, z, ln_g, ln_b, w_s, b_t)


def _sgu_bwd(z, dcat, ln_g, ln_b, w_s, b_t, name):
    T = z.shape[0]
    A = ln_g.shape[1]
    G = A // 128
    rows = min(SGU_ROWS, T)

    def body(u_ref, v_ref, da_ref, g_ref, b_ref, w_ref, bt_ref, dz_ref, dg_ref, db_ref, dw_ref, dbt_ref, dvn_ref):
        @pl.when(pl.program_id(0) == 0)
        def _():
            dg_ref[...] = jnp.zeros_like(dg_ref)
            db_ref[...] = jnp.zeros_like(db_ref)
            dw_ref[...] = jnp.zeros_like(dw_ref)
            dbt_ref[...] = jnp.zeros_like(dbt_ref)

        mask = _tril_mask()
        for c in range(rows // CHUNK):
            rs = pl.ds(c * CHUNK, CHUNK)
            vv = v_ref[rs, :]
            gv = _gelu(vv)
            xh, rstd = _layernorm_stats(gv)
            vn = (xh * g_ref[...] + b_ref[...]).astype(BF16)
            for g in range(G):
                cs = pl.ds(g * 128, 128)
                w = jnp.where(mask, w_ref[g], 0.0).astype(BF16)
                vg = vn[:, g * 128:(g + 1) * 128]
                mixed = _dot(w, vg) + bt_ref[:, g:g + 1]
                uu = u_ref[rs, cs]
                da = da_ref[rs, cs]
                dz_ref[rs, cs] = (da * mixed * _gelu_grad(uu)).astype(BF16)
                dm = da * _gelu(uu)
                dmb = dm.astype(BF16)
                dbt_ref[:, g:g + 1] += jnp.sum(dm, axis=1, keepdims=True)
                dw_ref[g] += jnp.where(mask, _dot(dmb, vg, NT), 0.0)
                dvn_ref[:, cs] = _dot(w, dmb, TN)
            dvn = dvn_ref[...]
            dg_ref[...] += jnp.sum(dvn * xh, axis=0, keepdims=True)
            db_ref[...] += jnp.sum(dvn, axis=0, keepdims=True)
            dxh = dvn * g_ref[...]
            dgv = rstd * (dxh - jnp.mean(dxh, axis=-1, keepdims=True)
                          - xh * jnp.mean(dxh * xh, axis=-1, keepdims=True))
            dz_ref[rs, pl.ds(A, A)] = (dgv * _gelu_grad(vv)).astype(BF16)

    vec = pl.BlockSpec((1, A), lambda i: (0, 0))
    wsp = pl.BlockSpec((G, CHUNK, CHUNK), lambda i: (0, 0, 0))
    bsp = pl.BlockSpec((CHUNK, G), lambda i: (0, 0))
    return pl.pallas_call(
        body, name=name, grid=(T // rows,),
        in_specs=[
            pl.BlockSpec((rows, A), lambda i: (i, 0)),
            pl.BlockSpec((rows, A), lambda i: (i, 1)),
            pl.BlockSpec((rows, A), lambda i: (i, 0)),
            vec, vec, wsp, bsp,
        ],
        out_specs=[pl.BlockSpec((rows, 2 * A), lambda i: (i, 0)), vec, vec, wsp, bsp],
        out_shape=[
            jax.ShapeDtypeStruct((T, 2 * A), BF16),
            jax.ShapeDtypeStruct((1, A), F32),
            jax.ShapeDtypeStruct((1, A), F32),
            jax.ShapeDtypeStruct((G, CHUNK, CHUNK), F32),
            jax.ShapeDtypeStruct((CHUNK, G), F32),
        ],
        scratch_shapes=[pltpu.VMEM((CHUNK, A), F32)],
        compiler_params=_params("arbitrary"),
    )(z, z, dcat, ln_g, ln_b, w_s, b_t)


def _alibi_row(B, d):
    H = B // HEAD_DIM
    slopes = [d * 2.0 ** (-8.0 * (h + 1.0) / H) for h in range(H)]
    return jnp.repeat(jnp.asarray(slopes, F32), HEAD_DIM)[None, :]


def _dil_scores(q, k, slope_d, valid, dist):
    s = _dot(q, k, NT) - slope_d * dist
    return jnp.where(valid, s, MASKED)


def _dil_block(n, r, d):
    if d == 1:
        return pl.ds(pl.multiple_of(n * ATT_BLOCK, ATT_BLOCK), ATT_BLOCK)
    return pl.ds(n * (d * ATT_BLOCK) + r, ATT_BLOCK, stride=d)


def _dilated_forward(z, cat, B, name):
    T = z.shape[0]
    H = B // HEAD_DIM
    A = cat.shape[1] - B
    scale = HEAD_DIM ** -0.5
    blk = ATT_BLOCK
    chunk = _tile(T, 512)

    def body(q_ref, k_ref, v_ref, sl_ref, cat_in, cat_ref, of_ref, lt_ref, *branch):
        o_refs, l_refs = branch[:3], branch[3:]
        slope = sl_ref[:, :1]
        qi = lax.broadcasted_iota(jnp.int32, (blk, 2 * blk), 0)
        kj = lax.broadcasted_iota(jnp.int32, (blk, 2 * blk), 1)
        dist = qi + blk - kj
        band = (dist >= 0) & (dist <= blk)
        distf = dist.astype(F32)

        for b, (_, d) in enumerate(DILATED_PAIRS):
            def one(n, r, b=b, d=d):
                rows, prev = _dil_block(n, r, d), _dil_block(jnp.maximum(n - 1, 0), r, d)
                q = (q_ref[rows, :] * scale).astype(BF16)
                k = jnp.concatenate([k_ref[prev, :], k_ref[rows, :]], axis=0).astype(BF16)
                v = jnp.concatenate([v_ref[prev, :], v_ref[rows, :]], axis=0).astype(BF16)
                s = _dil_scores(q, k, slope * float(d), band & ((kj >= blk) | (n > 0)), distf)
                m = jnp.max(s, axis=-1, keepdims=True)
                p = jnp.exp(s - m)
                den = jnp.sum(p, axis=-1, keepdims=True)
                o_refs[b][rows, :] = _dot(p.astype(BF16), v) / den
                l_refs[b][rows, :] = jnp.broadcast_to(m + jnp.log(den), (blk, HEAD_DIM))

            per = max(1, 4 // d)

            def step(i, _, d=d, per=per, one=one):
                for u in range(per):
                    for r in range(d):
                        one(i * per + u, r)
                return 0

            lax.fori_loop(0, T // (d * blk * per), step, 0)

        def merge(i, _):
            rs = pl.ds(pl.multiple_of(i * chunk, chunk), chunk)
            a, b, c = l_refs[0][rs, :], l_refs[1][rs, :], l_refs[2][rs, :]
            m = jnp.maximum(jnp.maximum(a, b), c)
            ea, eb, ec = jnp.exp(a - m), jnp.exp(b - m), jnp.exp(c - m)
            tot = ea + eb + ec
            o = (ea * o_refs[0][rs, :] + eb * o_refs[1][rs, :] + ec * o_refs[2][rs, :]) / tot
            of_ref[rs, :] = o
            cat_ref[rs, :] = o.astype(BF16)
            lt_ref[rs, :] = m + jnp.log(tot)
            return 0

        lax.fori_loop(0, T // chunk, merge, 0)

    def col(unit):
        return lambda h: (0, unit * H + h)

    seq = (T, HEAD_DIM)
    out = pl.BlockSpec(seq, lambda h: (0, h))
    return pl.pallas_call(
        body, name=name, grid=(H,),
        in_specs=[pl.BlockSpec(seq, col(2)), pl.BlockSpec(seq, col(3)), pl.BlockSpec(seq, col(4)),
                  pl.BlockSpec((1, HEAD_DIM), lambda h: (0, h)), ANY],
        out_specs=[pl.BlockSpec(seq, lambda h: (0, A // HEAD_DIM + h)), out, out],
        out_shape=[jax.ShapeDtypeStruct(cat.shape, BF16), jax.ShapeDtypeStruct((T, B), F32),
                   jax.ShapeDtypeStruct((T, B), F32)],
        input_output_aliases={4: 0},
        scratch_shapes=[pltpu.VMEM(seq, F32)] * 6,
        compiler_params=_params("parallel"),
    )(z, z, z, _alibi_row(B, 1), cat)


def _dilated_backward(z, dcat, o, lse, B, name):
    T = z.shape[0]
    H = B // HEAD_DIM
    scale = HEAD_DIM ** -0.5
    blk = ATT_BLOCK
    chunk = _tile(T, 512)

    def body(q_ref, k_ref, v_ref, do_ref, o_ref, l_ref, sl_ref, dq_ref, dk_ref, dv_ref, aq_ref, ak_ref, av_ref):
        slope = sl_ref[:, :1]
        qi = lax.broadcasted_iota(jnp.int32, (blk, 2 * blk), 0)
        kj = lax.broadcasted_iota(jnp.int32, (blk, 2 * blk), 1)
        dist = qi + blk - kj
        band = (dist >= 0) & (dist <= blk)
        distf = dist.astype(F32)
        for acc in (aq_ref, ak_ref, av_ref):
            acc[...] = jnp.zeros_like(acc)

        for _, d in DILATED_PAIRS:
            def one(n, r, d=d):
                rows, prev = _dil_block(n, r, d), _dil_block(jnp.maximum(n - 1, 0), r, d)
                q = (q_ref[rows, :] * scale).astype(BF16)
                k = jnp.concatenate([k_ref[prev, :], k_ref[rows, :]], axis=0).astype(BF16)
                v = jnp.concatenate([v_ref[prev, :], v_ref[rows, :]], axis=0).astype(BF16)
                do = do_ref[rows, :]
                delta = jnp.sum(do * o_ref[rows, :], axis=-1, keepdims=True)
                do = do.astype(BF16)
                s = _dil_scores(q, k, slope * float(d), band & ((kj >= blk) | (n > 0)), distf)
                p = jnp.exp(s - l_ref[rows, :][:, :1])
                ds = (p * (_dot(do, v, NT) - delta)).astype(BF16)
                aq_ref[rows, :] += _dot(ds, k)
                dk = _dot(ds, q, TN)
                dv = _dot(p.astype(BF16), do, TN)
                ak_ref[prev, :] += dk[:blk]
                av_ref[prev, :] += dv[:blk]
                ak_ref[rows, :] += dk[blk:]
                av_ref[rows, :] += dv[blk:]

            per = max(1, 4 // d)

            def step(i, _, d=d, per=per, one=one):
                for u in range(per):
                    for r in range(d):
                        one(i * per + u, r)
                return 0

            lax.fori_loop(0, T // (d * blk * per), step, 0)

        def emit(i, _):
            rs = pl.ds(pl.multiple_of(i * chunk, chunk), chunk)
            dq_ref[rs, :] = (aq_ref[rs, :] * scale).astype(BF16)
            dk_ref[rs, :] = ak_ref[rs, :].astype(BF16)
            dv_ref[rs, :] = av_ref[rs, :].astype(BF16)
            return 0

        lax.fori_loop(0, T // chunk, emit, 0)

    def col(unit):
        return lambda h: (0, unit * H + h)

    seq = (T, HEAD_DIM)
    own = pl.BlockSpec(seq, lambda h: (0, h))
    return pl.pallas_call(
        body, name=name, grid=(H,),
        in_specs=[pl.BlockSpec(seq, col(2)), pl.BlockSpec(seq, col(3)), pl.BlockSpec(seq, col(4)),
                  pl.BlockSpec(seq, col(1)), own, own, pl.BlockSpec((1, HEAD_DIM), lambda h: (0, h))],
        out_specs=[own] * 3,
        out_shape=[jax.ShapeDtypeStruct((T, B), BF16)] * 3,
        scratch_shapes=[pltpu.VMEM(seq, F32)] * 3,
        compiler_params=_params("parallel"),
    )(z, z, z, dcat, o, lse, _alibi_row(B, 1))


def _join_columns(parts, name):
    T = parts[0].shape[0]
    widths = [p.shape[1] for p in parts]

    def body(*refs):
        o_ref, at = refs[-1], 0
        for ref, w in zip(refs[:-1], widths):
            o_ref[:, pl.ds(at, w)] = ref[...]
            at += w

    return pl.pallas_call(
        body, name=name, grid=(T // ROWS,),
        in_specs=[pl.BlockSpec((ROWS, w), lambda i: (i, 0)) for w in widths],
        out_specs=pl.BlockSpec((ROWS, sum(widths)), lambda i: (i, 0)),
        out_shape=jax.ShapeDtypeStruct((T, sum(widths)), BF16), compiler_params=_params("parallel"),
    )(*parts)


SB_QUERY_ROWS = 2048
SB_KEYS = 2 * ATT_BLOCK


def _tri_and_ones(pred):
    rows = lax.broadcasted_iota(jnp.int32, (2 * ATT_BLOCK, 2 * ATT_BLOCK), 0) % ATT_BLOCK
    cols = lax.broadcasted_iota(jnp.int32, (2 * ATT_BLOCK, 2 * ATT_BLOCK), 1)
    return ((cols >= ATT_BLOCK) | pred(rows, cols)).astype(BF16)


def _running(x, tri):
    hi = x.astype(BF16)
    lo = (x - hi.astype(F32)).astype(BF16)
    return _dot(jnp.concatenate([hi, lo], axis=1), tri)


def _sb_mask(query_rows, s):
    rows = lax.broadcasted_iota(jnp.int32, (query_rows - s * SB_KEYS, SB_KEYS), 0)
    cols = lax.broadcasted_iota(jnp.int32, (query_rows - s * SB_KEYS, SB_KEYS), 1)
    return cols < rows


def _log_sigmoids(z):
    ls = jnp.minimum(z, 0.0) - jnp.log(1.0 + jnp.exp(-jnp.abs(z)))
    return ls, ls - z


def _sb_fwd(qkv, W, name):
    T = qkv.shape[0]
    H = W // HEAD_DIM
    blk = ATT_BLOCK
    qb = min(SB_QUERY_ROWS, T)
    per = qb // SB_KEYS

    def body(q_ref, k_ref, v_ref, o_ref, lt_ref, acc_ref):
        i = pl.program_id(1)
        tri = _tri_and_ones(lambda r, c: r > c)
        lt_ref[...] = jnp.zeros_like(lt_ref)
        acc_ref[...] = jnp.zeros_like(acc_ref)

        def tile(j, mask, r0=0):
            ks = pl.ds(pl.multiple_of(j * SB_KEYS, SB_KEYS), SB_KEYS)
            qs = pl.ds(r0, qb - r0)
            z = _dot(q_ref[qs, :], k_ref[ks, :], NT)
            ls, lm = _log_sigmoids(z)
            if mask is not None:
                lm = jnp.where(mask, lm, 0.0)
            later = lt_ref[qs, :]
            second = _running(lm[:, blk:], tri)
            first = _running(lm[:, :blk], tri)
            after_first = later + second[:, blk:]
            a = jnp.exp(ls + jnp.concatenate([first[:, :blk] + after_first, second[:, :blk] + later], axis=1))
            if mask is not None:
                a = jnp.where(mask, a, 0.0)
            acc_ref[qs, :] += _dot(a.astype(BF16), v_ref[ks, :])
            lt_ref[qs, :] = after_first + first[:, blk:]

        for s in reversed(range(per)):
            tile(i * per + s, _sb_mask(qb, s), s * SB_KEYS)

        def step(jj, _):
            for s in range(per):
                tile((i - jj) * per - 1 - s, None)
            return 0

        lax.fori_loop(0, i, step, 0)
        o_ref[...] = acc_ref[...].astype(BF16)

    qs = pl.BlockSpec((qb, HEAD_DIM), lambda h, i: (i, h))
    return pl.pallas_call(
        body, name=name, grid=(H, T // qb),
        in_specs=[qs, pl.BlockSpec((T, HEAD_DIM), lambda h, i: (0, H + h)),
                  pl.BlockSpec((T, HEAD_DIM), lambda h, i: (0, 2 * H + h))],
        out_specs=[qs, qs],
        out_shape=[jax.ShapeDtypeStruct((T, W), BF16), jax.ShapeDtypeStruct((T, W), F32)],
        scratch_shapes=[pltpu.VMEM((qb, HEAD_DIM), F32)],
        compiler_params=_params("parallel", "arbitrary"),
    )(qkv, qkv, qkv)


def _sb_bwd(qkv, do, ltot, W, name):
    T = qkv.shape[0]
    H = W // HEAD_DIM
    blk = ATT_BLOCK
    nkb = T // SB_KEYS
    qb = min(SB_QUERY_ROWS, T)
    per = qb // SB_KEYS

    def body(q_ref, k_ref, v_ref, do_ref, lt_ref, dq_ref, dkt_ref, dvt_ref, qt_ref, dot_ref, plm_ref, pg_ref):
        i = pl.program_id(1)

        @pl.when(i == 0)
        def _():
            dkt_ref[...] = jnp.zeros_like(dkt_ref)
            dvt_ref[...] = jnp.zeros_like(dvt_ref)

        qt_ref[...] = q_ref[...].astype(F32).T.astype(BF16)
        dot_ref[...] = do_ref[...].astype(F32).T.astype(BF16)
        upto = _tri_and_ones(lambda r, c: r <= c)
        before = _tri_and_ones(lambda r, c: r < c)
        plm_ref[...] = jnp.zeros_like(plm_ref)
        pg_ref[...] = jnp.zeros_like(pg_ref)
        dq_ref[...] = jnp.zeros_like(dq_ref)

        def tile(j, mask, r0=0):
            ks = pl.ds(pl.multiple_of(j * SB_KEYS, SB_KEYS), SB_KEYS)
            qs = pl.ds(r0, qb - r0)
            k = k_ref[ks, :]
            v = v_ref[ks, :]
            z = _dot(q_ref[qs, :], k, NT)
            ls, lm = _log_sigmoids(z)
            nsig = jnp.exp(lm)
            if mask is not None:
                lm = jnp.where(mask, lm, 0.0)
            earlier = plm_ref[qs, :]
            first = _running(lm[:, :blk], upto)
            second = _running(lm[:, blk:], upto)
            upto_first = earlier + first[:, blk:]
            seen = jnp.concatenate([first[:, :blk] + earlier, second[:, :blk] + upto_first], axis=1)
            ltot = lt_ref[qs, :]
            a = jnp.exp(ls + (jnp.concatenate([ltot, ltot], axis=1) - seen))
            if mask is not None:
                a = jnp.where(mask, a, 0.0)
            g = a * _dot(do_ref[qs, :], v, NT)
            g_earlier = pg_ref[qs, :]
            g_first = _running(g[:, :blk], before)
            g_second = _running(g[:, blk:], before)
            g_upto_first = g_earlier + g_first[:, blk:]
            gsum = jnp.concatenate([g_first[:, :blk] + g_earlier, g_second[:, :blk] + g_upto_first], axis=1)
            dz = g * nsig - gsum * jnp.exp(ls)
            if mask is not None:
                dz = jnp.where(mask, dz, 0.0)
            dzb = dz.astype(BF16)
            dkt_ref[j] += _dot(qt_ref[:, qs], dzb)
            dvt_ref[j] += _dot(dot_ref[:, qs], a.astype(BF16))
            dq_ref[qs, :] += _dot(dzb, k)
            plm_ref[qs, :] = upto_first + second[:, blk:]
            pg_ref[qs, :] = g_upto_first + g_second[:, blk:]

        def step(jj, _):
            for s in range(per):
                tile(jj * per + s, None)
            return 0

        lax.fori_loop(0, i, step, 0)
        for s in range(per):
            tile(i * per + s, _sb_mask(qb, s), s * SB_KEYS)

    qs = pl.BlockSpec((qb, HEAD_DIM), lambda h, i: (i, h))
    res = pl.BlockSpec((None, nkb, HEAD_DIM, SB_KEYS), lambda h, i: (h, 0, 0, 0))
    return pl.pallas_call(
        body, name=name, grid=(H, T // qb),
        in_specs=[qs, pl.BlockSpec((T, HEAD_DIM), lambda h, i: (0, H + h)),
                  pl.BlockSpec((T, HEAD_DIM), lambda h, i: (0, 2 * H + h)), qs, qs],
        out_specs=[qs, res, res],
        out_shape=[jax.ShapeDtypeStruct((T, W), F32)] + [jax.ShapeDtypeStruct((H, nkb, HEAD_DIM, SB_KEYS), F32)] * 2,
        scratch_shapes=[pltpu.VMEM((HEAD_DIM, qb), BF16), pltpu.VMEM((HEAD_DIM, qb), BF16),
                        pltpu.VMEM((qb, HEAD_DIM), F32), pltpu.VMEM((qb, HEAD_DIM), F32)],
        compiler_params=_params("parallel", "arbitrary"),
    )(qkv, qkv, qkv, do, ltot)


def _sb_pack(dq, dkt, dvt, name):
    T, W = dq.shape
    H = W // HEAD_DIM
    blk = SB_KEYS
    scale = HEAD_DIM ** -0.5

    def body(q_ref, kt_ref, vt_ref, o_ref):
        o_ref[:, pl.ds(0, W)] = (q_ref[...] * scale).astype(BF16)
        for h in range(H):
            o_ref[:, pl.ds(W + h * HEAD_DIM, HEAD_DIM)] = kt_ref[h].T.astype(BF16)
            o_ref[:, pl.ds(2 * W + h * HEAD_DIM, HEAD_DIM)] = vt_ref[h].T.astype(BF16)

    tr = pl.BlockSpec((H, None, HEAD_DIM, blk), lambda i: (0, i, 0, 0))
    return pl.pallas_call(
        body, name=name, grid=(T // blk,), in_specs=[pl.BlockSpec((blk, W), lambda i: (i, 0)), tr, tr],
        out_specs=pl.BlockSpec((blk, 3 * W), lambda i: (i, 0)),
        out_shape=jax.ShapeDtypeStruct((T, 3 * W), BF16), compiler_params=_params("parallel"),
    )(dq, dkt, dvt)


def _local_step(x, target, norms, sgu, comm):
    T, D = x.shape
    A = D // 2
    pre_mix, post_mix, pre_ffn, post_ffn = norms
    ln_g, ln_b, w_s, b_s = sgu
    b_t = b_s.T
    scale = HEAD_DIM ** -0.5

    def vec(p, layer):
        return comm.tie(p[layer:layer + 1])

    h0 = _prenorm(x, vec(pre_mix, 0), "prenorm0")
    comm.arrive("ab_in", after=h0)
    comm.land("ab_in")
    z = _matmul(h0, comm.weight("ab_in"), mode="nn", name="ab_in_fwd")
    comm.arrive("ab_out", after=z)
    cat = _sgu_fwd(z, ln_g, ln_b, w_s, b_t, "sgu_fwd")
    cat, o_dil, lse_dil = _dilated_forward(z, cat, A, "dilated_fwd")
    comm.land("ab_out", after=o_dil)
    y0 = _matmul(cat, comm.weight("ab_out"), mode="nn", name="ab_out_fwd")
    comm.arrive("w1_0", after=y0)
    x1, h1 = _postnorm_prenorm(x, y0, vec(post_mix, 0), vec(pre_ffn, 0), "norm_mix0")
    comm.land("w1_0", after=h1)
    comm.arrive("w2_0", after=h1)

    def relu2(acc, j):
        r = jnp.maximum(acc, 0.0)
        return r * r, 2.0 * r

    f0, r0 = _matmul(h1, comm.weight("w1_0"), mode="nn", name="ffn0_w1_fwd", out_dtype=(BF16, BF16), epi=relu2)
    comm.land("w2_0", after=f0)
    y1 = _matmul(f0, comm.weight("w2_0"), mode="nn", name="ffn0_w2_fwd")
    comm.arrive("sb", after=y1)
    x2, h2 = _postnorm_prenorm(x1, y1, vec(post_ffn, 0), vec(pre_mix, 1), "norm_ffn0")
    comm.land("sb", after=h2)

    tn_qkv = _tile(D, 1024)
    nq = D // tn_qkv

    def scale_q(acc, j):
        return jnp.where(j < nq, acc * scale, acc)

    qkv = _matmul(h2, comm.weight("sb_in"), mode="nn", name="sb_in_fwd", out_dtype=BF16, tn=tn_qkv, epi=scale_q)
    comm.arrive("ffn1", after=qkv)
    o_sb, ltot = _sb_fwd(qkv, D, "sb_fwd")
    comm.land("ffn1", after=o_sb)
    y2 = _matmul(o_sb, comm.weight("sb_out"), mode="nn", name="sb_out_fwd")
    x3, h3 = _postnorm_prenorm(x2, y2, vec(post_mix, 1), vec(pre_ffn, 1), "norm_mix1")
    f1, r1 = _matmul(h3, comm.weight("w1_1"), mode="nn", name="ffn1_w1_fwd", out_dtype=(BF16, BF16), epi=relu2)
    y3 = _matmul(f1, comm.weight("w2_1"), mode="nn", name="ffn1_w2_fwd")
    loss_tile, dx4, dy3, dg_post_ffn1 = _postnorm_loss(x3, y3, vec(post_ffn, 1), target, "norm_loss")
    loss = loss_tile[0, 0]

    def relu2_bwd(acc, j, r):
        return acc * r.astype(F32)

    def ffn_bwd(dy, h, f, r, layer):
        g_w2 = _matmul(f, dy, mode="tn", name=f"ffn{layer}_w2_wgrad", out_dtype=BF16)
        da = _matmul(dy, comm.weight(f"w2_{layer}"), mode="nt", name=f"ffn{layer}_w2_dgrad", out_dtype=BF16,
                     epi=relu2_bwd, extras=(r,))
        g_w1 = _matmul(h, da, mode="tn", name=f"ffn{layer}_w1_wgrad", out_dtype=BF16)
        comm.reduce(f"ffn{layer}", {f"w2_{layer}": g_w2, f"w1_{layer}": g_w1})
        return _matmul(da, comm.weight(f"w1_{layer}"), mode="nt", name=f"ffn{layer}_w1_dgrad", after=comm.started())

    dh3 = ffn_bwd(dy3, h3, f1, r1, 1)
    dx3, dg_pre_ffn1, dy2, dg_post_mix1 = _norm_bwd_pair(dx4, dh3, x3, vec(pre_ffn, 1), y2, vec(post_mix, 1),
                                                         "ffn1_sb_norm_bwd")
    g_sb_out = _matmul(o_sb, dy2, mode="tn", name="sb_out_wgrad", out_dtype=BF16)
    do_sb = _matmul(dy2, comm.weight("sb_out"), mode="nt", name="sb_out_dgrad", out_dtype=BF16)
    dq, dk, dv = _sb_bwd(qkv, do_sb, ltot, D, "sb_bwd")
    dqkv = _sb_pack(dq, dk, dv, "sb_pack")
    g_sb_in = _matmul(h2, dqkv, mode="tn", name="sb_in_wgrad", out_dtype=BF16)
    comm.reduce("sb", {"sb_out": g_sb_out, "sb_in": g_sb_in})
    dh2 = _matmul(dqkv, comm.weight("sb_in"), mode="nt", name="sb_in_dgrad", after=comm.started())
    dx2, dg_pre_mix1, dy1, dg_post_ffn0 = _norm_bwd_pair(dx3, dh2, x2, vec(pre_mix, 1), y1, vec(post_ffn, 0),
                                                         "sb_ffn0_norm_bwd")
    dh1 = ffn_bwd(dy1, h1, f0, r0, 0)
    dx1, dg_pre_ffn0, dy0, dg_post_mix0 = _norm_bwd_pair(dx2, dh1, x1, vec(pre_ffn, 0), y0, vec(post_mix, 0),
                                                         "ffn0_ab_norm_bwd")
    g_ab_out = _matmul(cat, dy0, mode="tn", name="ab_out_wgrad", out_dtype=BF16)
    comm.reduce("ab_out", {"ab_out": g_ab_out})
    dcat = _matmul(dy0, comm.weight("ab_out"), mode="nt", name="ab_out_dgrad", after=comm.started())
    duv, d_ln_g, d_ln_b, d_w_s, d_b_t = _sgu_bwd(z, dcat, ln_g, ln_b, w_s, b_t, "sgu_bwd")
    dz = _join_columns([duv, *_dilated_backward(z, dcat, o_dil, lse_dil, A, "dilated_bwd")], "join_dz")
    g_ab_in = _matmul(h0, dz, mode="tn", name="ab_in_wgrad", out_dtype=BF16)
    comm.reduce("ab_in", {"ab_in": g_ab_in})
    dh0 = _matmul(dz, comm.weight("ab_in"), mode="nt", name="ab_in_dgrad", after=comm.started())
    dx0, dg_pre_mix0 = _prenorm_bwd(dx1, dh0, x, vec(pre_mix, 0), "ab_prenorm_bwd")

    small = {
        "pre_mix": jnp.concatenate([dg_pre_mix0, dg_pre_mix1], axis=0),
        "post_mix": jnp.concatenate([dg_post_mix0, dg_post_mix1], axis=0),
        "pre_ffn": jnp.concatenate([dg_pre_ffn0, dg_pre_ffn1], axis=0),
        "post_ffn": jnp.concatenate([dg_post_ffn0, dg_post_ffn1], axis=0),
        "ln_g": d_ln_g, "ln_b": d_ln_b, "w_s": d_w_s, "b_s": d_b_t.T,
    }
    return loss, dx0, small


MESH_ID = pl.DeviceIdType.MESH
ANY = pl.BlockSpec(memory_space=pl.ANY)


def _coords():
    return lax.axis_index("x"), lax.axis_index("y"), lax.axis_index("c")


def _shard_of(ref, kind, p):
    if kind == "col":
        n = ref.shape[1] // N_DEV
        return ref.at[:, pl.ds(pl.multiple_of(p * n, 128), n)]
    r = ref.shape[0] // N_DEV
    return ref.at[pl.ds(pl.multiple_of(p * r, 16), r), :]


def _full_shape(shard, kind):
    if kind == "col":
        return (shard.shape[0], shard.shape[1] * N_DEV)
    return (shard.shape[0] * N_DEV, shard.shape[1])


def _place(shards, layer, kind, block, after, name):
    _, rows, cols = shards.shape
    tr = _tile(rows, 512)

    def body(b_ref, s_ref, after_ref, o_ref):
        o_ref[...] = s_ref[...].astype(BF16)

    if kind == "col":
        out = pl.BlockSpec((tr, cols), lambda i, b_ref: (i, b_ref[0]))
    else:
        out = pl.BlockSpec((tr, cols), lambda i, b_ref: (b_ref[0] * (rows // tr) + i, 0))
    return pl.pallas_call(
        body, name=name,
        grid_spec=pltpu.PrefetchScalarGridSpec(
            num_scalar_prefetch=1, grid=(rows // tr,),
            in_specs=[pl.BlockSpec((None, tr, cols), lambda i, b_ref: (layer, i, 0)), ANY], out_specs=out),
        out_shape=jax.ShapeDtypeStruct(_full_shape(shards[0], kind), BF16),
        compiler_params=_params("parallel"),
    )(block, shards, after)


HBM = pl.BlockSpec(memory_space=pltpu.HBM)
SEM = pl.BlockSpec(memory_space=pltpu.SEMAPHORE)
FLOWS = pltpu.SideEffectType.DATAFLOW_SIDE_EFFECTING


def _in_hbm(a):
    return pltpu.with_memory_space_constraint(a, pltpu.HBM)


def _hbm_like(bufs):
    return [pltpu.HBM(b.shape, b.dtype) for b in bufs]


def _copies_start(name, bufs, plan, n, after):
    nb = len(bufs)

    def body(*refs):
        send_sems, recv_sems, token = refs[nb + 1], refs[nb + 2], refs[-1]
        for cp in plan(refs[:nb], send_sems, recv_sems):
            cp.start()
        token[...] = jnp.zeros_like(token)

    out = pl.pallas_call(
        body, name=name, in_specs=[HBM] * nb + [ANY],
        out_specs=[SEM, SEM] + [HBM] * nb + [pl.BlockSpec(memory_space=pltpu.VMEM)],
        out_shape=[pltpu.SemaphoreType.DMA((n,)), pltpu.SemaphoreType.DMA((n,))] + _hbm_like(bufs)
        + [jax.ShapeDtypeStruct((8, 128), F32)],
        input_output_aliases={i: 2 + i for i in range(nb)},
        compiler_params=pltpu.CompilerParams(has_side_effects=FLOWS),
    )(*[_in_hbm(b) for b in bufs], after)
    return (out[0], out[1]), list(out[2:2 + nb]), out[-1]


def _copies_wait(name, bufs, sems, after, plan):
    nb = len(bufs)

    def body(*refs):
        for cp in plan(refs[:nb], refs[nb], refs[nb + 1]):
            cp.wait_send()
            cp.wait_recv()

    out = pl.pallas_call(
        body, name=name, in_specs=[HBM] * nb + [SEM, SEM, ANY], out_specs=[HBM] * nb,
        out_shape=_hbm_like(bufs), input_output_aliases={i: i for i in range(nb)},
        compiler_params=pltpu.CompilerParams(has_side_effects=FLOWS),
    )(*bufs, *sems, after)
    return list(out)


def _copies_wait_start(name, bufs, sems, after, plan, next_plan, n_next):
    nb = len(bufs)

    def body(*refs):
        ins = refs[:nb]
        for cp in plan(ins, refs[nb], refs[nb + 1]):
            cp.wait_send()
            cp.wait_recv()
        send_sems, recv_sems, token = refs[nb + 3], refs[nb + 4], refs[-1]
        for cp in next_plan(ins, send_sems, recv_sems):
            cp.start()
        token[...] = jnp.zeros_like(token)

    out = pl.pallas_call(
        body, name=name, in_specs=[HBM] * nb + [SEM, SEM, ANY],
        out_specs=[SEM, SEM] + [HBM] * nb + [pl.BlockSpec(memory_space=pltpu.VMEM)],
        out_shape=[pltpu.SemaphoreType.DMA((n_next,)), pltpu.SemaphoreType.DMA((n_next,))] + _hbm_like(bufs)
        + [jax.ShapeDtypeStruct((8, 128), F32)],
        input_output_aliases={i: 2 + i for i in range(nb)},
        compiler_params=pltpu.CompilerParams(has_side_effects=FLOWS),
    )(*bufs, *sems, after)
    return (out[0], out[1]), list(out[2:2 + nb]), out[-1]


def _gather_plans(kinds):
    nt = len(kinds)

    def slot(refs, t, px, py, pc):
        return _shard_of(refs[t], kinds[t], 4 * px + 2 * py + pc)

    def to_chips(refs, send_sems, recv_sems):
        x, y, c = _coords()
        peers = [(x, y, 1 - c), (1 - x, y, c), (x, 1 - y, c), (1 - x, 1 - y, c)]
        return [pltpu.make_async_remote_copy(
            src_ref=slot(refs, t, x, y, c), dst_ref=slot(refs, t, x, y, c), send_sem=send_sems.at[4 * t + k],
            recv_sem=recv_sems.at[4 * t + k], device_id=peer, device_id_type=MESH_ID)
            for t in range(nt) for k, peer in enumerate(peers)]

    def to_sibling(refs, send_sems, recv_sems):
        x, y, c = _coords()
        chips = [(1 - x, y), (x, 1 - y), (1 - x, 1 - y)]
        return [pltpu.make_async_remote_copy(
            src_ref=slot(refs, t, *chip, c), dst_ref=slot(refs, t, *chip, c), send_sem=send_sems.at[3 * t + j],
            recv_sem=recv_sems.at[3 * t + j], device_id=(x, y, 1 - c), device_id_type=MESH_ID)
            for t in range(nt) for j, chip in enumerate(chips)]

    return to_chips, to_sibling


def _shard_shape(full, kind):
    if kind == "col":
        return (full.shape[0], full.shape[1] // N_DEV)
    return (full.shape[0] // N_DEV, full.shape[1])


def _scatter_plan(kinds):
    nt = len(kinds)

    def plan(refs, send_sems, recv_sems):
        x, y, c = _coords()
        copies = []
        for t in range(nt):
            for k in range(1, N_DEV):
                px = 1 - x if (k >> 2) & 1 else x
                py = 1 - y if (k >> 1) & 1 else y
                pc = 1 - c if k & 1 else c
                copies.append(pltpu.make_async_remote_copy(
                    src_ref=_shard_of(refs[t], kinds[t], 4 * px + 2 * py + pc),
                    dst_ref=refs[nt + t].at[4 * x + 2 * y + c],
                    send_sem=send_sems.at[7 * t + k - 1], recv_sem=recv_sems.at[7 * t + k - 1],
                    device_id=(px, py, pc), device_id_type=MESH_ID))
        return copies
    return plan


def _partial_specs(full, kind, tr):
    rows, cols = _shard_shape(full, kind)
    steps = rows // tr
    if kind == "col":
        own = pl.BlockSpec((tr, cols), lambda i, w: (i, w[0]))
    else:
        own = pl.BlockSpec((tr, cols), lambda i, w: (w[0] * steps + i, 0))
    return [own] + [pl.BlockSpec((None, tr, cols), lambda i, w, k=k: (w[k], i, 0)) for k in range(1, N_DEV)]


def _small_plan(refs, send_sems, recv_sems):
    x, y, c = _coords()
    copies = []
    for k in range(1, N_DEV):
        peer = (1 - x if (k >> 2) & 1 else x, 1 - y if (k >> 1) & 1 else y, 1 - c if k & 1 else c)
        copies.append(pltpu.make_async_remote_copy(
            src_ref=refs[0], dst_ref=refs[1].at[4 * x + 2 * y + c], send_sem=send_sems.at[k - 1],
            recv_sem=recv_sems.at[k - 1], device_id=peer, device_id_type=MESH_ID))
    return copies


def _sum_small(vec, land):
    def body(v_ref, l_ref, o_ref):
        x, y, c = _coords()
        me = 4 * x + 2 * y + c
        total = jnp.where(me == 0, v_ref[...], l_ref[0])
        for p in range(1, N_DEV):
            total = total + jnp.where(me == p, v_ref[...], l_ref[p])
        o_ref[...] = total

    whole = pl.BlockSpec(memory_space=pltpu.VMEM)
    return pl.pallas_call(
        body, name="sum_small", in_specs=[whole, whole], out_specs=whole,
        out_shape=jax.ShapeDtypeStruct(vec.shape, F32),
        compiler_params=pltpu.CompilerParams(vmem_limit_bytes=VMEM_LIMIT),
    )(vec, land)


def _adamw_math(w, g, m, v):
    m = ADAM_B1 * m + (1.0 - ADAM_B1) * g
    v = ADAM_B2 * v + (1.0 - ADAM_B2) * (g * g)
    m_hat = m / (1.0 - ADAM_B1 ** ADAM_STEP)
    v_hat = v / (1.0 - ADAM_B2 ** ADAM_STEP)
    delta = -ADAM_LR * (m_hat / (jnp.sqrt(v_hat) + ADAM_EPS) + ADAM_WD * w)
    return delta, m, v


def _adamw(w, grads, kind, where, m, v, after, name):
    layers, rows, cols = w.shape
    tr = _tile(rows, 128)
    out = None
    for layer, (grad, land) in enumerate(grads):
        def body(w_ref, *refs):
            parts, (x_ref, m_ref, v_ref) = refs[:N_DEV], refs[N_DEV:N_DEV + 3]
            g_ref, d_ref, mo_ref, vo_ref = refs[-4:]
            g = parts[0][...].astype(F32)
            for p_ref in parts[1:]:
                g = g + p_ref[...].astype(F32)
            g_ref[...] = g
            d_ref[...], mo_ref[...], vo_ref[...] = _adamw_math(x_ref[...], g, m_ref[...], v_ref[...])

        blk = pl.BlockSpec((None, tr, cols), lambda i, w_, layer=layer: (layer, i, 0))
        earlier = [] if out is None else list(out)
        out = pl.pallas_call(
            body, name=f"{name}_{layer}",
            grid_spec=pltpu.PrefetchScalarGridSpec(
                num_scalar_prefetch=1, grid=(rows // tr,),
                in_specs=_partial_specs(grad, kind, tr) + [blk] * 3 + [ANY] * (len(earlier) + len(after)),
                out_specs=[blk] * 4),
            out_shape=[jax.ShapeDtypeStruct((layers, rows, cols), F32)] * 4,
            input_output_aliases={N_DEV + 4 + k: k for k in range(len(earlier))},
            compiler_params=_params("parallel"),
        )(where, grad, *[land] * (N_DEV - 1), w, m, v, *earlier, *after)
    return out


def _adamw_small(w, g, m, v):
    def body(w_ref, g_ref, m_ref, v_ref, d_ref, mo_ref, vo_ref):
        d_ref[...], mo_ref[...], vo_ref[...] = _adamw_math(w_ref[...], g_ref[...], m_ref[...], v_ref[...])

    whole = pl.BlockSpec(memory_space=pltpu.VMEM)
    return pl.pallas_call(
        body, name="adamw_small", in_specs=[whole] * 4, out_specs=[whole] * 3,
        out_shape=[jax.ShapeDtypeStruct(w.shape, F32)] * 3,
        compiler_params=pltpu.CompilerParams(vmem_limit_bytes=VMEM_LIMIT),
    )(w, g, m, v)


def _pack(arrays):
    rows = []
    for a in arrays:
        flat = a.reshape(-1)
        pad = (-flat.shape[0]) % 1024
        rows.append(jnp.pad(flat, (0, pad)).reshape(-1, 128))
    return jnp.concatenate(rows, axis=0)


def _unpack(packed, like):
    out, r = [], 0
    for a in like:
        n = math.prod(a.shape)
        nr = (n + 1023) // 1024 * 8
        out.append(packed[r:r + nr].reshape(-1)[:n].reshape(a.shape))
        r += nr
    return out


KIND = {"ab_in": "col", "ab_out": "row", "sb_in": "col", "sb_out": "row",
        "w1_0": "col", "w1_1": "col", "w2_0": "row", "w2_1": "row"}
GATHERS = {"ab_in": ("ab_in",), "ab_out": ("ab_out",), "w1_0": ("w1_0",), "w2_0": ("w2_0",),
           "sb": ("sb_in", "sb_out"), "ffn1": ("w1_1", "w2_1")}


class _Exchange:
    def __init__(self, shards):
        x, y, c = _coords()
        me = (4 * x + 2 * y + c).astype(jnp.int32)
        self.where = jnp.stack([jnp.bitwise_xor(me, k) for k in range(N_DEV)])
        self.full = {}
        self.tokens = []
        self.gathers = {}
        self.scatters = {}
        self.settled = {}
        block = me.reshape(1)
        after = block
        for key, group in GATHERS.items():
            for n in group:
                self.full[n] = _place(*shards[n], KIND[n], block, after, f"place_{n}")
            to_chips, to_sibling = _gather_plans([KIND[n] for n in group])
            bufs = [self.full[n] for n in group]
            sems, bufs, after = _copies_start(f"gather_start_{key}", bufs, to_chips, 4 * len(group), after)
            self.tokens.append(after)
            self.gathers[key] = (group, sems, bufs, to_chips, to_sibling)

    def tie(self, small):
        for token in self.tokens:
            small = small + token[0:1, 0:1]
        self.tokens = []
        return small

    def started(self):
        return tuple(self.tokens)

    def weight(self, name):
        return self.full[name]

    def arrive(self, key, after):
        group, sems, bufs, to_chips, to_sibling = self.gathers[key]
        sems, bufs, token = _copies_wait_start(f"gather_pass_{key}", bufs, sems, after, to_chips, to_sibling,
                                               3 * len(group))
        self.tokens.append(token)
        self.gathers[key] = (group, sems, bufs, token, to_sibling)

    def land(self, key, after=None):
        group, sems, bufs, token, to_sibling = self.gathers.pop(key)
        after = token if after is None else after
        self.full.update(zip(group, _copies_wait(f"gather_done_{key}", bufs, sems, after, to_sibling)))

    def reduce(self, key, grads):
        names = list(grads)
        kinds = [KIND[n] for n in names]
        full = [grads[n] for n in names]
        lands = [lax.empty((N_DEV,) + _shard_shape(g, k), BF16) for g, k in zip(full, kinds)]
        plan = _scatter_plan(kinds)
        sems, bufs, token = _copies_start(f"scatter_start_{key}", full + lands, plan, (N_DEV - 1) * len(names),
                                          self.where)
        self.tokens.append(token)
        self.scatters[key] = (names, sems, bufs, plan)

    def settle(self, keys, after):
        for key in keys:
            names, sems, bufs, plan = self.scatters.pop(key)
            bufs = _copies_wait(f"scatter_done_{key}", bufs, sems, after, plan)
            self.settled.update({n: t for n, *t in zip(names, bufs[:len(names)], bufs[len(names):])})
        return self.settled


SMALL = ("norm_pre_mix", "norm_post_mix", "norm_pre_ffn", "norm_post_ffn", "sgu_ln_g", "sgu_ln_b", "sgu_w", "sgu_b")
ORDER = ("norm_pre_mix", "norm_post_mix", "norm_pre_ffn", "norm_post_ffn", "ab_w_in", "sgu_ln_g", "sgu_ln_b", "sgu_w",
         "sgu_b", "ab_w_out", "sb_w_in", "sb_w_out", "ffn_w1", "ffn_w2")


def kernel(x, norm_pre_mix, norm_post_mix, norm_pre_ffn, norm_post_ffn, ab_w_in, sgu_ln_g, sgu_ln_b, sgu_w, sgu_b, ab_w_out, sb_w_in, sb_w_out, ffn_w1, ffn_w2, loss_target, m_norm_pre_mix, m_norm_post_mix, m_norm_pre_ffn, m_norm_post_ffn, m_ab_w_in, m_sgu_ln_g, m_sgu_ln_b, m_sgu_w, m_sgu_b, m_ab_w_out, m_sb_w_in, m_sb_w_out, m_ffn_w1, m_ffn_w2, v_norm_pre_mix, v_norm_post_mix, v_norm_pre_ffn, v_norm_post_ffn, v_ab_w_in, v_sgu_ln_g, v_sgu_ln_b, v_sgu_w, v_sgu_b, v_ab_w_out, v_sb_w_in, v_sb_w_out, v_ffn_w1, v_ffn_w2):
    W = dict(norm_pre_mix=norm_pre_mix, norm_post_mix=norm_post_mix, norm_pre_ffn=norm_pre_ffn,
             norm_post_ffn=norm_post_ffn, ab_w_in=ab_w_in, sgu_ln_g=sgu_ln_g, sgu_ln_b=sgu_ln_b, sgu_w=sgu_w,
             sgu_b=sgu_b, ab_w_out=ab_w_out, sb_w_in=sb_w_in, sb_w_out=sb_w_out, ffn_w1=ffn_w1, ffn_w2=ffn_w2)
    M = dict(norm_pre_mix=m_norm_pre_mix, norm_post_mix=m_norm_post_mix, norm_pre_ffn=m_norm_pre_ffn,
             norm_post_ffn=m_norm_post_ffn, ab_w_in=m_ab_w_in, sgu_ln_g=m_sgu_ln_g, sgu_ln_b=m_sgu_ln_b,
             sgu_w=m_sgu_w, sgu_b=m_sgu_b, ab_w_out=m_ab_w_out, sb_w_in=m_sb_w_in, sb_w_out=m_sb_w_out,
             ffn_w1=m_ffn_w1, ffn_w2=m_ffn_w2)
    V = dict(norm_pre_mix=v_norm_pre_mix, norm_post_mix=v_norm_post_mix, norm_pre_ffn=v_norm_pre_ffn,
             norm_post_ffn=v_norm_post_ffn, ab_w_in=v_ab_w_in, sgu_ln_g=v_sgu_ln_g, sgu_ln_b=v_sgu_ln_b,
             sgu_w=v_sgu_w, sgu_b=v_sgu_b, ab_w_out=v_ab_w_out, sb_w_in=v_sb_w_in, sb_w_out=v_sb_w_out,
             ffn_w1=v_ffn_w1, ffn_w2=v_ffn_w2)

    shards = {"ab_in": (ab_w_in, 0), "ab_out": (ab_w_out, 0), "w1_0": (ffn_w1, 0), "w2_0": (ffn_w2, 0),
              "sb_in": (sb_w_in, 0), "sb_out": (sb_w_out, 0), "w1_1": (ffn_w1, 1), "w2_1": (ffn_w2, 1)}
    comm = _Exchange(shards)
    norms = (norm_pre_mix, norm_post_mix, norm_pre_ffn, norm_post_ffn)
    sgu = (sgu_ln_g, sgu_ln_b, sgu_w[0], sgu_b[0])
    loss, dx, small = _local_step(x[0], loss_target[0], norms, sgu, comm)

    out = {}

    def update(name, layers, landed, after=()):
        out[name] = _adamw(W[name], [landed[n] for n in layers], KIND[layers[0]], comm.where, M[name], V[name],
                           after, f"adamw_{name}")

    landed = comm.settle(("ffn1", "sb", "ffn0"), after=dx)
    small_g = [small["pre_mix"], small["post_mix"], small["pre_ffn"], small["post_ffn"], small["ln_g"],
               small["ln_b"], small["w_s"][None], small["b_s"][None]]
    mine = _pack(small_g + [loss.reshape(1)])
    sems, bufs, token = _copies_start("small_start", [mine, jnp.zeros((N_DEV,) + mine.shape, F32)], _small_plan,
                                      N_DEV - 1, dx)
    for name, layers in (("sb_w_in", ["sb_in"]), ("sb_w_out", ["sb_out"]), ("ffn_w1", ["w1_0", "w1_1"]),
                         ("ffn_w2", ["w2_0", "w2_1"])):
        update(name, layers, landed, after=(token,))
    summed = _sum_small(*_copies_wait("small_done", bufs, sems, out["ffn_w2"][3], _small_plan))
    g_small, loss = summed[:-8], summed[-8, 0]
    landed = comm.settle(("ab_out", "ab_in"), after=g_small)
    update("ab_w_out", ["ab_out"], landed)
    update("ab_w_in", ["ab_in"], landed)
    res = _adamw_small(_pack([W[n] for n in SMALL]), g_small, _pack([M[n] for n in SMALL]),
                       _pack([V[n] for n in SMALL]))
    like = [W[n] for n in SMALL]
    for n, *vals in zip(SMALL, *[_unpack(r, like) for r in [g_small] + list(res)]):
        out[n] = vals

    return (loss, dx[None], *[out[n][0] for n in ORDER], *[out[n][1] for n in ORDER],
            *[out[n][2] for n in ORDER], *[out[n][3] for n in ORDER])
```

```python
import math

import jax
import jax.numpy as jnp
from jax import lax
from jax.experimental import pallas as pl
from jax.experimental.pallas import tpu as pltpu

F32 = jnp.float32
BF16 = jnp.bfloat16

HEAD_DIM = 128
CHUNK = 128
ATT_BLOCK = 128
DILATED_PAIRS = ((128, 1), (512, 4), (2048, 16))
RMS_EPS = 1e-6
LN_EPS = 1e-5
ADAM_LR = 0.001
ADAM_B1 = 0.9
ADAM_B2 = 0.999
ADAM_EPS = 1e-08
ADAM_WD = 0.01
ADAM_STEP = 10
N_DEV = 8
MASKED = -1e30

V7X_VMEM_BYTES = 64 * 1024 * 1024
VMEM_LIMIT = V7X_VMEM_BYTES - 8 * 1024 * 1024

NN = (((1,), (0,)), ((), ()))
NT = (((1,), (1,)), ((), ()))
TN = (((0,), (0,)), ((), ()))


def _params(*sem):
    return pltpu.CompilerParams(dimension_semantics=sem, vmem_limit_bytes=VMEM_LIMIT)


def _dot(a, b, dims=NN):
    return lax.dot_general(a, b, dims, preferred_element_type=F32)


def _tile(n, preferred):
    if n <= preferred:
        return n
    t = preferred - preferred % 128
    while n % t:
        t -= 128
    assert t > 0, (n, preferred)
    return t


def _matmul(a, b, *, mode, name, out_dtype=F32, tm=1024, tn=1024, tk=2048, epi=None, extras=(), after=()):
    if mode == "nn":
        (M, K), N = a.shape, b.shape[1]
    elif mode == "nt":
        (M, K), N = a.shape, b.shape[0]
    else:
        (K, M), N = a.shape, b.shape[1]
    tm, tn, tk = _tile(M, tm), _tile(N, tn), _tile(K, tk)
    nk = K // tk
    if mode == "tn":
        a_spec = pl.BlockSpec((tk, tm), lambda i, j, k: (k, i))
    else:
        a_spec = pl.BlockSpec((tm, tk), lambda i, j, k: (i, k))
    if mode == "nt":
        b_spec = pl.BlockSpec((tn, tk), lambda i, j, k: (j, k))
    else:
        b_spec = pl.BlockSpec((tk, tn), lambda i, j, k: (k, j))
    o_spec = pl.BlockSpec((tm, tn), lambda i, j, k: (i, j))
    dims = {"nn": NN, "nt": NT, "tn": TN}[mode]
    n_extra = len(extras)
    n_in = n_extra + len(after)
    several = isinstance(out_dtype, tuple)
    n_out = len(out_dtype) if several else 1

    def finish(acc, rest):
        outs = acc if epi is None else epi(acc, pl.program_id(1), *[r[...] for r in rest[:n_extra]])
        for o_ref, o in zip(rest[n_in:n_in + n_out], outs if several else (outs,)):
            o_ref[...] = o.astype(o_ref.dtype)

    if nk == 1:
        def body(a_ref, b_ref, *rest):
            finish(_dot(a_ref[...], b_ref[...], dims), rest)
        scratch = []
    else:
        def body(a_ref, b_ref, *rest):
            acc_ref = rest[n_in + n_out]
            k = pl.program_id(2)

            @pl.when(k == 0)
            def _():
                acc_ref[...] = jnp.zeros_like(acc_ref)

            acc_ref[...] += _dot(a_ref[...], b_ref[...], dims)

            @pl.when(k == nk - 1)
            def _():
                finish(acc_ref[...], rest)
        scratch = [pltpu.VMEM((tm, tn), F32)]

    shapes = [jax.ShapeDtypeStruct((M, N), d) for d in (out_dtype if several else (out_dtype,))]
    return pl.pallas_call(
        body,
        name=name,
        grid=(M // tm, N // tn, nk),
        in_specs=[a_spec, b_spec] + [o_spec] * n_extra + [ANY] * len(after),
        out_specs=[o_spec] * n_out if several else o_spec,
        out_shape=shapes if several else shapes[0],
        scratch_shapes=scratch,
        compiler_params=_params("parallel", "parallel", "arbitrary"),
    )(a, b, *extras, *after)


ROWS = 512


def _rms(x):
    return lax.rsqrt(jnp.mean(x * x, axis=-1, keepdims=True) + RMS_EPS)


def _prenorm(x, g, name):
    T, D = x.shape

    def body(x_ref, g_ref, h_ref):
        xv = x_ref[...]
        h_ref[...] = (xv * _rms(xv) * g_ref[...]).astype(BF16)

    row = pl.BlockSpec((ROWS, D), lambda i: (i, 0))
    vec = pl.BlockSpec((1, D), lambda i: (0, 0))
    return pl.pallas_call(
        body, name=name, grid=(T // ROWS,), in_specs=[row, vec], out_specs=row,
        out_shape=jax.ShapeDtypeStruct((T, D), BF16), compiler_params=_params("parallel"),
    )(x, g)


def _postnorm_prenorm(x, y, g_post, g_pre, name):
    T, D = x.shape

    def body(x_ref, y_ref, gp_ref, gn_ref, xo_ref, h_ref):
        yv = y_ref[...]
        xn = x_ref[...] + yv * _rms(yv) * gp_ref[...]
        xo_ref[...] = xn
        h_ref[...] = (xn * _rms(xn) * gn_ref[...]).astype(BF16)

    row = pl.BlockSpec((ROWS, D), lambda i: (i, 0))
    vec = pl.BlockSpec((1, D), lambda i: (0, 0))
    return pl.pallas_call(
        body, name=name, grid=(T // ROWS,), in_specs=[row, row, vec, vec], out_specs=[row, row],
        out_shape=[jax.ShapeDtypeStruct((T, D), F32), jax.ShapeDtypeStruct((T, D), BF16)],
        compiler_params=_params("parallel"),
    )(x, y, g_post, g_pre)


def _postnorm_grads(dn, yh, r, g):
    gd = dn * g
    return r * (gd - yh * jnp.mean(yh * gd, axis=-1, keepdims=True)), dn * yh


def _postnorm_loss(x, y, g_post, target, name):
    T, D = x.shape

    def body(x_ref, y_ref, gp_ref, t_ref, loss_ref, dx_ref, dy_ref, dg_ref):
        @pl.when(pl.program_id(0) == 0)
        def _():
            loss_ref[...] = jnp.zeros_like(loss_ref)
            dg_ref[...] = jnp.zeros_like(dg_ref)

        yv = y_ref[...]
        r = _rms(yv)
        yh = yv * r
        err = x_ref[...] + yh * gp_ref[...] - t_ref[...]
        dx = err * (1.0 / D)
        dx_ref[...] = dx
        loss_ref[...] += 0.5 * jnp.sum(jnp.sum(err * err, axis=-1, keepdims=True) * (1.0 / D))
        dy, dg = _postnorm_grads(dx, yh, r, gp_ref[...])
        dy_ref[...] = dy.astype(BF16)
        dg_ref[...] += jnp.sum(dg, axis=0, keepdims=True)

    row = pl.BlockSpec((ROWS, D), lambda i: (i, 0))
    vec = pl.BlockSpec((1, D), lambda i: (0, 0))
    acc = pl.BlockSpec((8, 128), lambda i: (0, 0))
    return pl.pallas_call(
        body, name=name, grid=(T // ROWS,), in_specs=[row, row, vec, row], out_specs=[acc, row, row, vec],
        out_shape=[jax.ShapeDtypeStruct((8, 128), F32), jax.ShapeDtypeStruct((T, D), F32),
                   jax.ShapeDtypeStruct((T, D), BF16), jax.ShapeDtypeStruct((1, D), F32)],
        compiler_params=_params("arbitrary"),
    )(x, y, g_post, target)


def _norm_bwd_pair(dx_out, dh, x, g_pre, y_prev, g_post_prev, name):
    T, D = x.shape

    def body(dxo_ref, dh_ref, x_ref, g_ref, y_ref, gp_ref, dx_ref, dg_ref, dy_ref, dgp_ref):
        @pl.when(pl.program_id(0) == 0)
        def _():
            dg_ref[...] = jnp.zeros_like(dg_ref)
            dgp_ref[...] = jnp.zeros_like(dgp_ref)

        xv, dhv = x_ref[...], dh_ref[...]
        r = _rms(xv)
        xh = xv * r
        gd = dhv * g_ref[...]
        dx = dxo_ref[...] + r * (gd - xh * jnp.mean(xh * gd, axis=-1, keepdims=True))
        dx_ref[...] = dx
        dg_ref[...] += jnp.sum(dhv * xh, axis=0, keepdims=True)
        yv = y_ref[...]
        ry = _rms(yv)
        dy, dgp = _postnorm_grads(dx, yv * ry, ry, gp_ref[...])
        dy_ref[...] = dy.astype(BF16)
        dgp_ref[...] += jnp.sum(dgp, axis=0, keepdims=True)

    row = pl.BlockSpec((ROWS, D), lambda i: (i, 0))
    vec = pl.BlockSpec((1, D), lambda i: (0, 0))
    return pl.pallas_call(
        body, name=name, grid=(T // ROWS,), in_specs=[row, row, row, vec, row, vec],
        out_specs=[row, vec, row, vec],
        out_shape=[jax.ShapeDtypeStruct((T, D), F32), jax.ShapeDtypeStruct((1, D), F32),
                   jax.ShapeDtypeStruct((T, D), BF16), jax.ShapeDtypeStruct((1, D), F32)],
        compiler_params=_params("arbitrary"),
    )(dx_out, dh, x, g_pre, y_prev, g_post_prev)


def _prenorm_bwd(dx_out, dh, x, g_pre, name):
    T, D = x.shape

    def body(dxo_ref, dh_ref, x_ref, g_ref, dx_ref, dg_ref):
        @pl.when(pl.program_id(0) == 0)
        def _():
            dg_ref[...] = jnp.zeros_like(dg_ref)

        xv, dhv = x_ref[...], dh_ref[...]
        r = _rms(xv)
        xh = xv * r
        gd = dhv * g_ref[...]
        dx_ref[...] = dxo_ref[...] + r * (gd - xh * jnp.mean(xh * gd, axis=-1, keepdims=True))
        dg_ref[...] += jnp.sum(dhv * xh, axis=0, keepdims=True)

    row = pl.BlockSpec((ROWS, D), lambda i: (i, 0))
    vec = pl.BlockSpec((1, D), lambda i: (0, 0))
    return pl.pallas_call(
        body, name=name, grid=(T // ROWS,), in_specs=[row, row, row, vec], out_specs=[row, vec],
        out_shape=[jax.ShapeDtypeStruct((T, D), F32), jax.ShapeDtypeStruct((1, D), F32)],
        compiler_params=_params("arbitrary"),
    )(dx_out, dh, x, g_pre)


_INV_SQRT2 = 1.0 / math.sqrt(2.0)
_INV_SQRT2PI = 1.0 / math.sqrt(2.0 * math.pi)


def _gelu(x):
    return 0.5 * x * (1.0 + lax.erf(x * _INV_SQRT2))


def _gelu_grad(x):
    return 0.5 * (1.0 + lax.erf(x * _INV_SQRT2)) + x * jnp.exp(-0.5 * x * x) * _INV_SQRT2PI


def _layernorm_stats(x):
    mu = jnp.mean(x, axis=-1, keepdims=True)
    xc = x - mu
    rstd = lax.rsqrt(jnp.mean(xc * xc, axis=-1, keepdims=True) + LN_EPS)
    return xc * rstd, rstd


def _tril_mask():
    i = lax.broadcasted_iota(jnp.int32, (CHUNK, CHUNK), 0)
    j = lax.broadcasted_iota(jnp.int32, (CHUNK, CHUNK), 1)
    return j <= i


SGU_ROWS = 512


def _sgu_fwd(z, ln_g, ln_b, w_s, b_t, name):
    T = z.shape[0]
    A = ln_g.shape[1]
    G = A // 128
    rows = min(SGU_ROWS, T)

    def body(u_ref, v_ref, g_ref, b_ref, w_ref, bt_ref, o_ref):
        mask = _tril_mask()
        for c in range(rows // CHUNK):
            rs = pl.ds(c * CHUNK, CHUNK)
            xh, _ = _layernorm_stats(_gelu(v_ref[rs, :]))
            vn = (xh * g_ref[...] + b_ref[...]).astype(BF16)
            for g in range(G):
                cs = pl.ds(g * 128, 128)
                w = jnp.where(mask, w_ref[g], 0.0).astype(BF16)
                mixed = _dot(w, vn[:, g * 128:(g + 1) * 128]) + bt_ref[:, g:g + 1]
                o_ref[rs, cs] = (_gelu(u_ref[rs, cs]) * mixed).astype(BF16)

    return pl.pallas_call(
        body, name=name, grid=(T // rows,),
        in_specs=[
            pl.BlockSpec((rows, A), lambda i: (i, 0)),
            pl.BlockSpec((rows, A), lambda i: (i, 1)),
            pl.BlockSpec((1, A), lambda i: (0, 0)),
            pl.BlockSpec((1, A), lambda i: (0, 0)),
            pl.BlockSpec((G, CHUNK, CHUNK), lambda i: (0, 0, 0)),
            pl.BlockSpec((CHUNK, G), lambda i: (0, 0)),
        ],
        out_specs=pl.BlockSpec((rows, A), lambda i: (i, 0)),
        out_shape=jax.ShapeDtypeStruct((T, 2 * A), BF16),
        compiler_params=_params("parallel"),
    )(z, z, ln_g, ln_b, w_s, b_t)


def _sgu_bwd(z, dcat, ln_g, ln_b, w_s, b_t, name):
    T = z.shape[0]
    A = ln_g.shape[1]
    G = A // 128
    rows = min(SGU_ROWS, T)

    def body(u_ref, v_ref, da_ref, g_ref, b_ref, w_ref, bt_ref, dz_ref, dg_ref, db_ref, dw_ref, dbt_ref, dvn_ref):
        @pl.when(pl.program_id(0) == 0)
        def _():
            dg_ref[...] = jnp.zeros_like(dg_ref)
            db_ref[...] = jnp.zeros_like(db_ref)
            dw_ref[...] = jnp.zeros_like(dw_ref)
            dbt_ref[...] = jnp.zeros_like(dbt_ref)

        mask = _tril_mask()
        for c in range(rows // CHUNK):
            rs = pl.ds(c * CHUNK, CHUNK)
            vv = v_ref[rs, :]
            gv = _gelu(vv)
            xh, rstd = _layernorm_stats(gv)
            vn = (xh * g_ref[...] + b_ref[...]).astype(BF16)
            for g in range(G):
                cs = pl.ds(g * 128, 128)
                w = jnp.where(mask, w_ref[g], 0.0).astype(BF16)
                vg = vn[:, g * 128:(g + 1) * 128]
                mixed = _dot(w, vg) + bt_ref[:, g:g + 1]
                uu = u_ref[rs, cs]
                da = da_ref[rs, cs]
                dz_ref[rs, cs] = (da * mixed * _gelu_grad(uu)).astype(BF16)
                dm = da * _gelu(uu)
                dmb = dm.astype(BF16)
                dbt_ref[:, g:g + 1] += jnp.sum(dm, axis=1, keepdims=True)
                dw_ref[g] += jnp.where(mask, _dot(dmb, vg, NT), 0.0)
                dvn_ref[:, cs] = _dot(w, dmb, TN)
            dvn = dvn_ref[...]
            dg_ref[...] += jnp.sum(dvn * xh, axis=0, keepdims=True)
            db_ref[...] += jnp.sum(dvn, axis=0, keepdims=True)
            dxh = dvn * g_ref[...]
            dgv = rstd * (dxh - jnp.mean(dxh, axis=-1, keepdims=True)
                          - xh * jnp.mean(dxh * xh, axis=-1, keepdims=True))
            dz_ref[rs, pl.ds(A, A)] = (dgv * _gelu_grad(vv)).astype(BF16)

    vec = pl.BlockSpec((1, A), lambda i: (0, 0))
    wsp = pl.BlockSpec((G, CHUNK, CHUNK), lambda i: (0, 0, 0))
    bsp = pl.BlockSpec((CHUNK, G), lambda i: (0, 0))
    return pl.pallas_call(
        body, name=name, grid=(T // rows,),
        in_specs=[
            pl.BlockSpec((rows, A), lambda i: (i, 0)),
            pl.BlockSpec((rows, A), lambda i: (i, 1)),
            pl.BlockSpec((rows, A), lambda i: (i, 0)),
            vec, vec, wsp, bsp,
        ],
        out_specs=[pl.BlockSpec((rows, 2 * A), lambda i: (i, 0)), vec, vec, wsp, bsp],
        out_shape=[
            jax.ShapeDtypeStruct((T, 2 * A), BF16),
            jax.ShapeDtypeStruct((1, A), F32),
            jax.ShapeDtypeStruct((1, A), F32),
            jax.ShapeDtypeStruct((G, CHUNK, CHUNK), F32),
            jax.ShapeDtypeStruct((CHUNK, G), F32),
        ],
        scratch_shapes=[pltpu.VMEM((CHUNK, A), F32)],
        compiler_params=_params("arbitrary"),
    )(z, z, dcat, ln_g, ln_b, w_s, b_t)


def _alibi_row(B, d):
    H = B // HEAD_DIM
    slopes = [d * 2.0 ** (-8.0 * (h + 1.0) / H) for h in range(H)]
    return jnp.repeat(jnp.asarray(slopes, F32), HEAD_DIM)[None, :]


def _dil_scores(q, k, slope_d, valid, dist):
    s = _dot(q, k, NT) - slope_d * dist
    return jnp.where(valid, s, MASKED)


def _dil_block(n, r, d):
    if d == 1:
        return pl.ds(pl.multiple_of(n * ATT_BLOCK, ATT_BLOCK), ATT_BLOCK)
    return pl.ds(n * (d * ATT_BLOCK) + r, ATT_BLOCK, stride=d)


def _dilated_forward(z, cat, B, name):
    T = z.shape[0]
    H = B // HEAD_DIM
    A = cat.shape[1] - B
    scale = HEAD_DIM ** -0.5
    blk = ATT_BLOCK
    chunk = _tile(T, 512)

    def body(q_ref, k_ref, v_ref, sl_ref, cat_in, cat_ref, of_ref, lt_ref, *branch):
        o_refs, l_refs = branch[:3], branch[3:]
        slope = sl_ref[:, :1]
        qi = lax.broadcasted_iota(jnp.int32, (blk, 2 * blk), 0)
        kj = lax.broadcasted_iota(jnp.int32, (blk, 2 * blk), 1)
        dist = qi + blk - kj
        band = (dist >= 0) & (dist <= blk)
        distf = dist.astype(F32)

        for b, (_, d) in enumerate(DILATED_PAIRS):
            def one(n, r, b=b, d=d):
                rows, prev = _dil_block(n, r, d), _dil_block(jnp.maximum(n - 1, 0), r, d)
                q = (q_ref[rows, :] * scale).astype(BF16)
                k = jnp.concatenate([k_ref[prev, :], k_ref[rows, :]], axis=0).astype(BF16)
                v = jnp.concatenate([v_ref[prev, :], v_ref[rows, :]], axis=0).astype(BF16)
                s = _dil_scores(q, k, slope * float(d), band & ((kj >= blk) | (n > 0)), distf)
                m = jnp.max(s, axis=-1, keepdims=True)
                p = jnp.exp(s - m)
                den = jnp.sum(p, axis=-1, keepdims=True)
                o_refs[b][rows, :] = _dot(p.astype(BF16), v) / den
                l_refs[b][rows, :] = jnp.broadcast_to(m + jnp.log(den), (blk, HEAD_DIM))

            per = max(1, 4 // d)

            def step(i, _, d=d, per=per, one=one):
                for u in range(per):
                    for r in range(d):
                        one(i * per + u, r)
                return 0

            lax.fori_loop(0, T // (d * blk * per), step, 0)

        def merge(i, _):
            rs = pl.ds(pl.multiple_of(i * chunk, chunk), chunk)
            a, b, c = l_refs[0][rs, :], l_refs[1][rs, :], l_refs[2][rs, :]
            m = jnp.maximum(jnp.maximum(a, b), c)
            ea, eb, ec = jnp.exp(a - m), jnp.exp(b - m), jnp.exp(c - m)
            tot = ea + eb + ec
            o = (ea * o_refs[0][rs, :] + eb * o_refs[1][rs, :] + ec * o_refs[2][rs, :]) / tot
            of_ref[rs, :] = o
            cat_ref[rs, :] = o.astype(BF16)
            lt_ref[rs, :] = m + jnp.log(tot)
            return 0

        lax.fori_loop(0, T // chunk, merge, 0)

    def col(unit):
        return lambda h: (0, unit * H + h)

    seq = (T, HEAD_DIM)
    out = pl.BlockSpec(seq, lambda h: (0, h))
    return pl.pallas_call(
        body, name=name, grid=(H,),
        in_specs=[pl.BlockSpec(seq, col(2)), pl.BlockSpec(seq, col(3)), pl.BlockSpec(seq, col(4)),
                  pl.BlockSpec((1, HEAD_DIM), lambda h: (0, h)), ANY],
        out_specs=[pl.BlockSpec(seq, lambda h: (0, A // HEAD_DIM + h)), out, out],
        out_shape=[jax.ShapeDtypeStruct(cat.shape, BF16), jax.ShapeDtypeStruct((T, B), F32),
                   jax.ShapeDtypeStruct((T, B), F32)],
        input_output_aliases={4: 0},
        scratch_shapes=[pltpu.VMEM(seq, F32)] * 6,
        compiler_params=_params("parallel"),
    )(z, z, z, _alibi_row(B, 1), cat)


def _dilated_backward(z, dcat, o, lse, B, name):
    T = z.shape[0]
    H = B // HEAD_DIM
    scale = HEAD_DIM ** -0.5
    blk = ATT_BLOCK
    chunk = _tile(T, 512)

    def body(q_ref, k_ref, v_ref, do_ref, o_ref, l_ref, sl_ref, dq_ref, dk_ref, dv_ref, aq_ref, ak_ref, av_ref):
        slope = sl_ref[:, :1]
        qi = lax.broadcasted_iota(jnp.int32, (blk, 2 * blk), 0)
        kj = lax.broadcasted_iota(jnp.int32, (blk, 2 * blk), 1)
        dist = qi + blk - kj
        band = (dist >= 0) & (dist <= blk)
        distf = dist.astype(F32)
        for acc in (aq_ref, ak_ref, av_ref):
            acc[...] = jnp.zeros_like(acc)

        for _, d in DILATED_PAIRS:
            def one(n, r, d=d):
                rows, prev = _dil_block(n, r, d), _dil_block(jnp.maximum(n - 1, 0), r, d)
                q = (q_ref[rows, :] * scale).astype(BF16)
                k = jnp.concatenate([k_ref[prev, :], k_ref[rows, :]], axis=0).astype(BF16)
                v = jnp.concatenate([v_ref[prev, :], v_ref[rows, :]], axis=0).astype(BF16)
                do = do_ref[rows, :]
                delta = jnp.sum(do * o_ref[rows, :], axis=-1, keepdims=True)
                do = do.astype(BF16)
                s = _dil_scores(q, k, slope * float(d), band & ((kj >= blk) | (n > 0)), distf)
                p = jnp.exp(s - l_ref[rows, :][:, :1])
                ds = (p * (_dot(do, v, NT) - delta)).astype(BF16)
                aq_ref[rows, :] += _dot(ds, k)
                dk = _dot(ds, q, TN)
                dv = _dot(p.astype(BF16), do, TN)
                ak_ref[prev, :] += dk[:blk]
                av_ref[prev, :] += dv[:blk]
                ak_ref[rows, :] += dk[blk:]
                av_ref[rows, :] += dv[blk:]

            per = max(1, 4 // d)

            def step(i, _, d=d, per=per, one=one):
                for u in range(per):
                    for r in range(d):
                        one(i * per + u, r)
                return 0

            lax.fori_loop(0, T // (d * blk * per), step, 0)

        def emit(i, _):
            rs = pl.ds(pl.multiple_of(i * chunk, chunk), chunk)
            dq_ref[rs, :] = (aq_ref[rs, :] * scale).astype(BF16)
            dk_ref[rs, :] = ak_ref[rs, :].astype(BF16)
            dv_ref[rs, :] = av_ref[rs, :].astype(BF16)
            return 0

        lax.fori_loop(0, T // chunk, emit, 0)

    def col(unit):
        return lambda h: (0, unit * H + h)

    seq = (T, HEAD_DIM)
    own = pl.BlockSpec(seq, lambda h: (0, h))
    return pl.pallas_call(
        body, name=name, grid=(H,),
        in_specs=[pl.BlockSpec(seq, col(2)), pl.BlockSpec(seq, col(3)), pl.BlockSpec(seq, col(4)),
                  pl.BlockSpec(seq, col(1)), own, own, pl.BlockSpec((1, HEAD_DIM), lambda h: (0, h))],
        out_specs=[own] * 3,
        out_shape=[jax.ShapeDtypeStruct((T, B), BF16)] * 3,
        scratch_shapes=[pltpu.VMEM(seq, F32)] * 3,
        compiler_params=_params("parallel"),
    )(z, z, z, dcat, o, lse, _alibi_row(B, 1))


def _join_columns(parts, name):
    T = parts[0].shape[0]
    widths = [p.shape[1] for p in parts]

    def body(*refs):
        o_ref, at = refs[-1], 0
        for ref, w in zip(refs[:-1], widths):
            o_ref[:, pl.ds(at, w)] = ref[...]
            at += w

    return pl.pallas_call(
        body, name=name, grid=(T // ROWS,),
        in_specs=[pl.BlockSpec((ROWS, w), lambda i: (i, 0)) for w in widths],
        out_specs=pl.BlockSpec((ROWS, sum(widths)), lambda i: (i, 0)),
        out_shape=jax.ShapeDtypeStruct((T, sum(widths)), BF16), compiler_params=_params("parallel"),
    )(*parts)


SB_FORWARD_ROWS = 2048
SB_BACKWARD_ROWS = 1024
SB_KEYS = 2 * ATT_BLOCK


def _tri_and_ones(pred):
    rows = lax.broadcasted_iota(jnp.int32, (2 * ATT_BLOCK, 2 * ATT_BLOCK), 0) % ATT_BLOCK
    cols = lax.broadcasted_iota(jnp.int32, (2 * ATT_BLOCK, 2 * ATT_BLOCK), 1)
    return ((cols >= ATT_BLOCK) | pred(rows, cols)).astype(BF16)


def _running(x, tri):
    hi = x.astype(BF16)
    lo = (x - hi.astype(F32)).astype(BF16)
    return _dot(jnp.concatenate([hi, lo], axis=1), tri)


def _sb_mask(query_rows, s):
    rows = lax.broadcasted_iota(jnp.int32, (query_rows - s * SB_KEYS, SB_KEYS), 0)
    cols = lax.broadcasted_iota(jnp.int32, (query_rows - s * SB_KEYS, SB_KEYS), 1)
    return cols < rows


def _log_sigmoids(z):
    ls = jnp.minimum(z, 0.0) - jnp.log(1.0 + jnp.exp(-jnp.abs(z)))
    return ls, ls - z


def _sb_fwd(qkv, W, name):
    T = qkv.shape[0]
    H = W // HEAD_DIM
    blk = ATT_BLOCK
    qb = min(SB_FORWARD_ROWS, T)
    per = qb // SB_KEYS

    def body(q_ref, k_ref, v_ref, o_ref, lt_ref, acc_ref):
        i = pl.program_id(1)
        tri = _tri_and_ones(lambda r, c: r > c)
        lt_ref[...] = jnp.zeros_like(lt_ref)
        acc_ref[...] = jnp.zeros_like(acc_ref)

        def tile(j, mask, r0=0):
            ks = pl.ds(pl.multiple_of(j * SB_KEYS, SB_KEYS), SB_KEYS)
            qs = pl.ds(r0, qb - r0)
            z = _dot(q_ref[qs, :], k_ref[ks, :], NT)
            ls, lm = _log_sigmoids(z)
            if mask is not None:
                lm = jnp.where(mask, lm, 0.0)
            later = lt_ref[qs, :]
            second = _running(lm[:, blk:], tri)
            first = _running(lm[:, :blk], tri)
            after_first = later + second[:, blk:]
            a = jnp.exp(ls + jnp.concatenate([first[:, :blk] + after_first, second[:, :blk] + later], axis=1))
            if mask is not None:
                a = jnp.where(mask, a, 0.0)
            acc_ref[qs, :] += _dot(a.astype(BF16), v_ref[ks, :])
            lt_ref[qs, :] = after_first + first[:, blk:]

        for s in reversed(range(per)):
            tile(i * per + s, _sb_mask(qb, s), s * SB_KEYS)

        def step(jj, _):
            for s in range(per):
                tile((i - jj) * per - 1 - s, None)
            return 0

        lax.fori_loop(0, i, step, 0)
        o_ref[...] = acc_ref[...].astype(BF16)

    qs = pl.BlockSpec((qb, HEAD_DIM), lambda h, i: (i, h))
    return pl.pallas_call(
        body, name=name, grid=(H, T // qb),
        in_specs=[qs, pl.BlockSpec((T, HEAD_DIM), lambda h, i: (0, H + h)),
                  pl.BlockSpec((T, HEAD_DIM), lambda h, i: (0, 2 * H + h))],
        out_specs=[qs, qs],
        out_shape=[jax.ShapeDtypeStruct((T, W), BF16), jax.ShapeDtypeStruct((T, W), F32)],
        scratch_shapes=[pltpu.VMEM((qb, HEAD_DIM), F32)],
        compiler_params=_params("parallel", "arbitrary"),
    )(qkv, qkv, qkv)


def _sb_bwd(qkv, do, ltot, W, name):
    T = qkv.shape[0]
    H = W // HEAD_DIM
    blk = ATT_BLOCK
    nkb = T // SB_KEYS
    qb = min(SB_BACKWARD_ROWS, T)
    per = qb // SB_KEYS

    def body(q_ref, k_ref, v_ref, do_ref, lt_ref, dq_ref, dkt_ref, dvt_ref, qt_ref, dot_ref, plm_ref, pg_ref):
        i = pl.program_id(1)

        @pl.when(i == 0)
        def _():
            dkt_ref[...] = jnp.zeros_like(dkt_ref)
            dvt_ref[...] = jnp.zeros_like(dvt_ref)

        qt_ref[...] = q_ref[...].astype(F32).T.astype(BF16)
        dot_ref[...] = do_ref[...].astype(F32).T.astype(BF16)
        upto = _tri_and_ones(lambda r, c: r <= c)
        before = _tri_and_ones(lambda r, c: r < c)
        plm_ref[...] = jnp.zeros_like(plm_ref)
        pg_ref[...] = jnp.zeros_like(pg_ref)
        dq_ref[...] = jnp.zeros_like(dq_ref)

        def tile(j, mask, r0=0):
            ks = pl.ds(pl.multiple_of(j * SB_KEYS, SB_KEYS), SB_KEYS)
            qs = pl.ds(r0, qb - r0)
            k = k_ref[ks, :]
            v = v_ref[ks, :]
            z = _dot(q_ref[qs, :], k, NT)
            ls, lm = _log_sigmoids(z)
            nsig = jnp.exp(lm)
            if mask is not None:
                lm = jnp.where(mask, lm, 0.0)
            earlier = plm_ref[qs, :]
            first = _running(lm[:, :blk], upto)
            second = _running(lm[:, blk:], upto)
            upto_first = earlier + first[:, blk:]
            seen = jnp.concatenate([first[:, :blk] + earlier, second[:, :blk] + upto_first], axis=1)
            ltot = lt_ref[qs, :]
            a = jnp.exp(ls + (jnp.concatenate([ltot, ltot], axis=1) - seen))
            if mask is not None:
                a = jnp.where(mask, a, 0.0)
            g = a * _dot(do_ref[qs, :], v, NT)
            g_earlier = pg_ref[qs, :]
            g_first = _running(g[:, :blk], before)
            g_second = _running(g[:, blk:], before)
            g_upto_first = g_earlier + g_first[:, blk:]
            gsum = jnp.concatenate([g_first[:, :blk] + g_earlier, g_second[:, :blk] + g_upto_first], axis=1)
            dz = g * nsig - gsum * jnp.exp(ls)
            if mask is not None:
                dz = jnp.where(mask, dz, 0.0)
            dzb = dz.astype(BF16)
            dkt_ref[j] += _dot(qt_ref[:, qs], dzb)
            dvt_ref[j] += _dot(dot_ref[:, qs], a.astype(BF16))
            dq_ref[qs, :] += _dot(dzb, k)
            plm_ref[qs, :] = upto_first + second[:, blk:]
            pg_ref[qs, :] = g_upto_first + g_second[:, blk:]

        def step(jj, _):
            for s in range(per):
                tile(jj * per + s, None)
            return 0

        lax.fori_loop(0, i, step, 0)
        for s in range(per):
            tile(i * per + s, _sb_mask(qb, s), s * SB_KEYS)

    qs = pl.BlockSpec((qb, HEAD_DIM), lambda h, i: (i, h))
    res = pl.BlockSpec((None, nkb, HEAD_DIM, SB_KEYS), lambda h, i: (h, 0, 0, 0))
    return pl.pallas_call(
        body, name=name, grid=(H, T // qb),
        in_specs=[qs, pl.BlockSpec((T, HEAD_DIM), lambda h, i: (0, H + h)),
                  pl.BlockSpec((T, HEAD_DIM), lambda h, i: (0, 2 * H + h)), qs, qs],
        out_specs=[qs, res, res],
        out_shape=[jax.ShapeDtypeStruct((T, W), F32)] + [jax.ShapeDtypeStruct((H, nkb, HEAD_DIM, SB_KEYS), F32)] * 2,
        scratch_shapes=[pltpu.VMEM((HEAD_DIM, qb), BF16), pltpu.VMEM((HEAD_DIM, qb), BF16),
                        pltpu.VMEM((qb, HEAD_DIM), F32), pltpu.VMEM((qb, HEAD_DIM), F32)],
        compiler_params=_params("parallel", "arbitrary"),
    )(qkv, qkv, qkv, do, ltot)


def _sb_pack(dq, dkt, dvt, name):
    T, W = dq.shape
    H = W // HEAD_DIM
    blk = SB_KEYS
    scale = HEAD_DIM ** -0.5

    def body(q_ref, kt_ref, vt_ref, o_ref):
        o_ref[:, pl.ds(0, W)] = (q_ref[...] * scale).astype(BF16)
        for h in range(H):
            o_ref[:, pl.ds(W + h * HEAD_DIM, HEAD_DIM)] = kt_ref[h].T.astype(BF16)
            o_ref[:, pl.ds(2 * W + h * HEAD_DIM, HEAD_DIM)] = vt_ref[h].T.astype(BF16)

    tr = pl.BlockSpec((H, None, HEAD_DIM, blk), lambda i: (0, i, 0, 0))
    return pl.pallas_call(
        body, name=name, grid=(T // blk,), in_specs=[pl.BlockSpec((blk, W), lambda i: (i, 0)), tr, tr],
        out_specs=pl.BlockSpec((blk, 3 * W), lambda i: (i, 0)),
        out_shape=jax.ShapeDtypeStruct((T, 3 * W), BF16), compiler_params=_params("parallel"),
    )(dq, dkt, dvt)


def _local_step(x, target, norms, sgu, comm):
    T, D = x.shape
    A = D // 2
    pre_mix, post_mix, pre_ffn, post_ffn = norms
    ln_g, ln_b, w_s, b_s = sgu
    b_t = b_s.T
    scale = HEAD_DIM ** -0.5

    def vec(p, layer):
        return comm.tie(p[layer:layer + 1])

    h0 = _prenorm(x, vec(pre_mix, 0), "prenorm0")
    comm.arrive("ab_in", after=h0)
    comm.land("ab_in")
    z = _matmul(h0, comm.weight("ab_in"), mode="nn", name="ab_in_fwd")
    comm.arrive("ab_out", after=z)
    cat = _sgu_fwd(z, ln_g, ln_b, w_s, b_t, "sgu_fwd")
    cat, o_dil, lse_dil = _dilated_forward(z, cat, A, "dilated_fwd")
    comm.land("ab_out", after=o_dil)
    y0 = _matmul(cat, comm.weight("ab_out"), mode="nn", name="ab_out_fwd")
    comm.arrive("w1_0", after=y0)
    x1, h1 = _postnorm_prenorm(x, y0, vec(post_mix, 0), vec(pre_ffn, 0), "norm_mix0")
    comm.land("w1_0", after=h1)
    comm.arrive("w2_0", after=h1)

    def relu2(acc, j):
        r = jnp.maximum(acc, 0.0)
        return r * r, 2.0 * r

    f0, r0 = _matmul(h1, comm.weight("w1_0"), mode="nn", name="ffn0_w1_fwd", out_dtype=(BF16, BF16), epi=relu2)
    comm.land("w2_0", after=f0)
    y1 = _matmul(f0, comm.weight("w2_0"), mode="nn", name="ffn0_w2_fwd")
    comm.arrive("sb", after=y1)
    x2, h2 = _postnorm_prenorm(x1, y1, vec(post_ffn, 0), vec(pre_mix, 1), "norm_ffn0")
    comm.land("sb", after=h2)

    tn_qkv = _tile(D, 1024)
    nq = D // tn_qkv

    def scale_q(acc, j):
        return jnp.where(j < nq, acc * scale, acc)

    qkv = _matmul(h2, comm.weight("sb_in"), mode="nn", name="sb_in_fwd", out_dtype=BF16, tn=tn_qkv, epi=scale_q)
    comm.arrive("ffn1", after=qkv)
    o_sb, ltot = _sb_fwd(qkv, D, "sb_fwd")
    comm.land("ffn1", after=o_sb)
    y2 = _matmul(o_sb, comm.weight("sb_out"), mode="nn", name="sb_out_fwd")
    x3, h3 = _postnorm_prenorm(x2, y2, vec(post_mix, 1), vec(pre_ffn, 1), "norm_mix1")
    f1, r1 = _matmul(h3, comm.weight("w1_1"), mode="nn", name="ffn1_w1_fwd", out_dtype=(BF16, BF16), epi=relu2)
    y3 = _matmul(f1, comm.weight("w2_1"), mode="nn", name="ffn1_w2_fwd")
    loss_tile, dx4, dy3, dg_post_ffn1 = _postnorm_loss(x3, y3, vec(post_ffn, 1), target, "norm_loss")
    loss = loss_tile[0, 0]

    def relu2_bwd(acc, j, r):
        return acc * r.astype(F32)

    def ffn_bwd(dy, h, f, r, layer):
        g_w2 = _matmul(f, dy, mode="tn", name=f"ffn{layer}_w2_wgrad", out_dtype=BF16)
        da = _matmul(dy, comm.weight(f"w2_{layer}"), mode="nt", name=f"ffn{layer}_w2_dgrad", out_dtype=BF16,
                     epi=relu2_bwd, extras=(r,))
        g_w1 = _matmul(h, da, mode="tn", name=f"ffn{layer}_w1_wgrad", out_dtype=BF16)
        comm.reduce(f"ffn{layer}", {f"w2_{layer}": g_w2, f"w1_{layer}": g_w1})
        return _matmul(da, comm.weight(f"w1_{layer}"), mode="nt", name=f"ffn{layer}_w1_dgrad", after=comm.started())

    dh3 = ffn_bwd(dy3, h3, f1, r1, 1)
    dx3, dg_pre_ffn1, dy2, dg_post_mix1 = _norm_bwd_pair(dx4, dh3, x3, vec(pre_ffn, 1), y2, vec(post_mix, 1),
                                                         "ffn1_sb_norm_bwd")
    g_sb_out = _matmul(o_sb, dy2, mode="tn", name="sb_out_wgrad", out_dtype=BF16)
    do_sb = _matmul(dy2, comm.weight("sb_out"), mode="nt", name="sb_out_dgrad", out_dtype=BF16)
    dq, dk, dv = _sb_bwd(qkv, do_sb, ltot, D, "sb_bwd")
    dqkv = _sb_pack(dq, dk, dv, "sb_pack")
    g_sb_in = _matmul(h2, dqkv, mode="tn", name="sb_in_wgrad", out_dtype=BF16)
    comm.reduce("sb", {"sb_out": g_sb_out, "sb_in": g_sb_in})
    dh2 = _matmul(dqkv, comm.weight("sb_in"), mode="nt", name="sb_in_dgrad", after=comm.started())
    dx2, dg_pre_mix1, dy1, dg_post_ffn0 = _norm_bwd_pair(dx3, dh2, x2, vec(pre_mix, 1), y1, vec(post_ffn, 0),
                                                         "sb_ffn0_norm_bwd")
    dh1 = ffn_bwd(dy1, h1, f0, r0, 0)
    dx1, dg_pre_ffn0, dy0, dg_post_mix0 = _norm_bwd_pair(dx2, dh1, x1, vec(pre_ffn, 0), y0, vec(post_mix, 0),
                                                         "ffn0_ab_norm_bwd")
    g_ab_out = _matmul(cat, dy0, mode="tn", name="ab_out_wgrad", out_dtype=BF16)
    comm.reduce("ab_out", {"ab_out": g_ab_out})
    dcat = _matmul(dy0, comm.weight("ab_out"), mode="nt", name="ab_out_dgrad", after=comm.started())
    duv, d_ln_g, d_ln_b, d_w_s, d_b_t = _sgu_bwd(z, dcat, ln_g, ln_b, w_s, b_t, "sgu_bwd")
    dz = _join_columns([duv, *_dilated_backward(z, dcat, o_dil, lse_dil, A, "dilated_bwd")], "join_dz")
    g_ab_in = _matmul(h0, dz, mode="tn", name="ab_in_wgrad", out_dtype=BF16)
    comm.reduce("ab_in", {"ab_in": g_ab_in})
    dh0 = _matmul(dz, comm.weight("ab_in"), mode="nt", name="ab_in_dgrad", after=comm.started())
    dx0, dg_pre_mix0 = _prenorm_bwd(dx1, dh0, x, vec(pre_mix, 0), "ab_prenorm_bwd")

    small = {
        "pre_mix": jnp.concatenate([dg_pre_mix0, dg_pre_mix1], axis=0),
        "post_mix": jnp.concatenate([dg_post_mix0, dg_post_mix1], axis=0),
        "pre_ffn": jnp.concatenate([dg_pre_ffn0, dg_pre_ffn1], axis=0),
        "post_ffn": jnp.concatenate([dg_post_ffn0, dg_post_ffn1], axis=0),
        "ln_g": d_ln_g, "ln_b": d_ln_b, "w_s": d_w_s, "b_s": d_b_t.T,
    }
    return loss, dx0, small


MESH_ID = pl.DeviceIdType.MESH
ANY = pl.BlockSpec(memory_space=pl.ANY)


def _coords():
    return lax.axis_index("x"), lax.axis_index("y"), lax.axis_index("c")


def _shard_of(ref, kind, p):
    if kind == "col":
        n = ref.shape[1] // N_DEV
        return ref.at[:, pl.ds(pl.multiple_of(p * n, 128), n)]
    r = ref.shape[0] // N_DEV
    return ref.at[pl.ds(pl.multiple_of(p * r, 16), r), :]


def _full_shape(shard, kind):
    if kind == "col":
        return (shard.shape[0], shard.shape[1] * N_DEV)
    return (shard.shape[0] * N_DEV, shard.shape[1])


def _place(shards, layer, kind, block, after, name):
    _, rows, cols = shards.shape
    tr = _tile(rows, 512)

    def body(b_ref, s_ref, after_ref, o_ref):
        o_ref[...] = s_ref[...].astype(BF16)

    if kind == "col":
        out = pl.BlockSpec((tr, cols), lambda i, b_ref: (i, b_ref[0]))
    else:
        out = pl.BlockSpec((tr, cols), lambda i, b_ref: (b_ref[0] * (rows // tr) + i, 0))
    return pl.pallas_call(
        body, name=name,
        grid_spec=pltpu.PrefetchScalarGridSpec(
            num_scalar_prefetch=1, grid=(rows // tr,),
            in_specs=[pl.BlockSpec((None, tr, cols), lambda i, b_ref: (layer, i, 0)), ANY], out_specs=out),
        out_shape=jax.ShapeDtypeStruct(_full_shape(shards[0], kind), BF16),
        compiler_params=_params("parallel"),
    )(block, shards, after)


HBM = pl.BlockSpec(memory_space=pltpu.HBM)
SEM = pl.BlockSpec(memory_space=pltpu.SEMAPHORE)
FLOWS = pltpu.SideEffectType.DATAFLOW_SIDE_EFFECTING


def _in_hbm(a):
    return pltpu.with_memory_space_constraint(a, pltpu.HBM)


def _hbm_like(bufs):
    return [pltpu.HBM(b.shape, b.dtype) for b in bufs]


def _copies_start(name, bufs, plan, n, after):
    nb = len(bufs)

    def body(*refs):
        send_sems, recv_sems, token = refs[nb + 1], refs[nb + 2], refs[-1]
        for cp in plan(refs[:nb], send_sems, recv_sems):
            cp.start()
        token[...] = jnp.zeros_like(token)

    out = pl.pallas_call(
        body, name=name, in_specs=[HBM] * nb + [ANY],
        out_specs=[SEM, SEM] + [HBM] * nb + [pl.BlockSpec(memory_space=pltpu.VMEM)],
        out_shape=[pltpu.SemaphoreType.DMA((n,)), pltpu.SemaphoreType.DMA((n,))] + _hbm_like(bufs)
        + [jax.ShapeDtypeStruct((8, 128), F32)],
        input_output_aliases={i: 2 + i for i in range(nb)},
        compiler_params=pltpu.CompilerParams(has_side_effects=FLOWS),
    )(*[_in_hbm(b) for b in bufs], after)
    return (out[0], out[1]), list(out[2:2 + nb]), out[-1]


def _copies_wait(name, bufs, sems, after, plan):
    nb = len(bufs)

    def body(*refs):
        for cp in plan(refs[:nb], refs[nb], refs[nb + 1]):
            cp.wait_send()
            cp.wait_recv()

    out = pl.pallas_call(
        body, name=name, in_specs=[HBM] * nb + [SEM, SEM, ANY], out_specs=[HBM] * nb,
        out_shape=_hbm_like(bufs), input_output_aliases={i: i for i in range(nb)},
        compiler_params=pltpu.CompilerParams(has_side_effects=FLOWS),
    )(*bufs, *sems, after)
    return list(out)


def _copies_wait_start(name, bufs, sems, after, plan, next_plan, n_next):
    nb = len(bufs)

    def body(*refs):
        ins = refs[:nb]
        for cp in plan(ins, refs[nb], refs[nb + 1]):
            cp.wait_send()
            cp.wait_recv()
        send_sems, recv_sems, token = refs[nb + 3], refs[nb + 4], refs[-1]
        for cp in next_plan(ins, send_sems, recv_sems):
            cp.start()
        token[...] = jnp.zeros_like(token)

    out = pl.pallas_call(
        body, name=name, in_specs=[HBM] * nb + [SEM, SEM, ANY],
        out_specs=[SEM, SEM] + [HBM] * nb + [pl.BlockSpec(memory_space=pltpu.VMEM)],
        out_shape=[pltpu.SemaphoreType.DMA((n_next,)), pltpu.SemaphoreType.DMA((n_next,))] + _hbm_like(bufs)
        + [jax.ShapeDtypeStruct((8, 128), F32)],
        input_output_aliases={i: 2 + i for i in range(nb)},
        compiler_params=pltpu.CompilerParams(has_side_effects=FLOWS),
    )(*bufs, *sems, after)
    return (out[0], out[1]), list(out[2:2 + nb]), out[-1]


def _gather_plans(kinds):
    nt = len(kinds)

    def slot(refs, t, px, py, pc):
        return _shard_of(refs[t], kinds[t], 4 * px + 2 * py + pc)

    def to_chips(refs, send_sems, recv_sems):
        x, y, c = _coords()
        peers = [(x, y, 1 - c), (1 - x, y, c), (x, 1 - y, c), (1 - x, 1 - y, c)]
        return [pltpu.make_async_remote_copy(
            src_ref=slot(refs, t, x, y, c), dst_ref=slot(refs, t, x, y, c), send_sem=send_sems.at[4 * t + k],
            recv_sem=recv_sems.at[4 * t + k], device_id=peer, device_id_type=MESH_ID)
            for t in range(nt) for k, peer in enumerate(peers)]

    def to_sibling(refs, send_sems, recv_sems):
        x, y, c = _coords()
        chips = [(1 - x, y), (x, 1 - y), (1 - x, 1 - y)]
        return [pltpu.make_async_remote_copy(
            src_ref=slot(refs, t, *chip, c), dst_ref=slot(refs, t, *chip, c), send_sem=send_sems.at[3 * t + j],
            recv_sem=recv_sems.at[3 * t + j], device_id=(x, y, 1 - c), device_id_type=MESH_ID)
            for t in range(nt) for j, chip in enumerate(chips)]

    return to_chips, to_sibling


def _shard_shape(full, kind):
    if kind == "col":
        return (full.shape[0], full.shape[1] // N_DEV)
    return (full.shape[0] // N_DEV, full.shape[1])


def _scatter_plan(kinds):
    nt = len(kinds)

    def plan(refs, send_sems, recv_sems):
        x, y, c = _coords()
        copies = []
        for t in range(nt):
            for k in range(1, N_DEV):
                px = 1 - x if (k >> 2) & 1 else x
                py = 1 - y if (k >> 1) & 1 else y
                pc = 1 - c if k & 1 else c
                copies.append(pltpu.make_async_remote_copy(
                    src_ref=_shard_of(refs[t], kinds[t], 4 * px + 2 * py + pc),
                    dst_ref=refs[nt + t].at[4 * x + 2 * y + c],
                    send_sem=send_sems.at[7 * t + k - 1], recv_sem=recv_sems.at[7 * t + k - 1],
                    device_id=(px, py, pc), device_id_type=MESH_ID))
        return copies
    return plan


def _partial_specs(full, kind, tr):
    rows, cols = _shard_shape(full, kind)
    steps = rows // tr
    if kind == "col":
        own = pl.BlockSpec((tr, cols), lambda i, w: (i, w[0]))
    else:
        own = pl.BlockSpec((tr, cols), lambda i, w: (w[0] * steps + i, 0))
    return [own] + [pl.BlockSpec((None, tr, cols), lambda i, w, k=k: (w[k], i, 0)) for k in range(1, N_DEV)]


def _small_plan(refs, send_sems, recv_sems):
    x, y, c = _coords()
    copies = []
    for k in range(1, N_DEV):
        peer = (1 - x if (k >> 2) & 1 else x, 1 - y if (k >> 1) & 1 else y, 1 - c if k & 1 else c)
        copies.append(pltpu.make_async_remote_copy(
            src_ref=refs[0], dst_ref=refs[1].at[4 * x + 2 * y + c], send_sem=send_sems.at[k - 1],
            recv_sem=recv_sems.at[k - 1], device_id=peer, device_id_type=MESH_ID))
    return copies


def _sum_small(vec, land):
    def body(v_ref, l_ref, o_ref):
        x, y, c = _coords()
        me = 4 * x + 2 * y + c
        total = jnp.where(me == 0, v_ref[...], l_ref[0])
        for p in range(1, N_DEV):
            total = total + jnp.where(me == p, v_ref[...], l_ref[p])
        o_ref[...] = total

    whole = pl.BlockSpec(memory_space=pltpu.VMEM)
    return pl.pallas_call(
        body, name="sum_small", in_specs=[whole, whole], out_specs=whole,
        out_shape=jax.ShapeDtypeStruct(vec.shape, F32),
        compiler_params=pltpu.CompilerParams(vmem_limit_bytes=VMEM_LIMIT),
    )(vec, land)


def _adamw_math(w, g, m, v):
    m = ADAM_B1 * m + (1.0 - ADAM_B1) * g
    v = ADAM_B2 * v + (1.0 - ADAM_B2) * (g * g)
    m_hat = m / (1.0 - ADAM_B1 ** ADAM_STEP)
    v_hat = v / (1.0 - ADAM_B2 ** ADAM_STEP)
    delta = -ADAM_LR * (m_hat / (jnp.sqrt(v_hat) + ADAM_EPS) + ADAM_WD * w)
    return delta, m, v


def _adamw(w, grads, kind, where, m, v, after, name):
    layers, rows, cols = w.shape
    tr = _tile(rows, 128)
    out = None
    for layer, (grad, land) in enumerate(grads):
        def body(w_ref, *refs):
            parts, (x_ref, m_ref, v_ref) = refs[:N_DEV], refs[N_DEV:N_DEV + 3]
            g_ref, d_ref, mo_ref, vo_ref = refs[-4:]
            g = parts[0][...].astype(F32)
            for p_ref in parts[1:]:
                g = g + p_ref[...].astype(F32)
            g_ref[...] = g
            d_ref[...], mo_ref[...], vo_ref[...] = _adamw_math(x_ref[...], g, m_ref[...], v_ref[...])

        blk = pl.BlockSpec((None, tr, cols), lambda i, w_, layer=layer: (layer, i, 0))
        earlier = [] if out is None else list(out)
        out = pl.pallas_call(
            body, name=f"{name}_{layer}",
            grid_spec=pltpu.PrefetchScalarGridSpec(
                num_scalar_prefetch=1, grid=(rows // tr,),
                in_specs=_partial_specs(grad, kind, tr) + [blk] * 3 + [ANY] * (len(earlier) + len(after)),
                out_specs=[blk] * 4),
            out_shape=[jax.ShapeDtypeStruct((layers, rows, cols), F32)] * 4,
            input_output_aliases={N_DEV + 4 + k: k for k in range(len(earlier))},
            compiler_params=_params("parallel"),
        )(where, grad, *[land] * (N_DEV - 1), w, m, v, *earlier, *after)
    return out


def _adamw_small(w, g, m, v):
    def body(w_ref, g_ref, m_ref, v_ref, d_ref, mo_ref, vo_ref):
        d_ref[...], mo_ref[...], vo_ref[...] = _adamw_math(w_ref[...], g_ref[...], m_ref[...], v_ref[...])

    whole = pl.BlockSpec(memory_space=pltpu.VMEM)
    return pl.pallas_call(
        body, name="adamw_small", in_specs=[whole] * 4, out_specs=[whole] * 3,
        out_shape=[jax.ShapeDtypeStruct(w.shape, F32)] * 3,
        compiler_params=pltpu.CompilerParams(vmem_limit_bytes=VMEM_LIMIT),
    )(w, g, m, v)


def _pack(arrays):
    rows = []
    for a in arrays:
        flat = a.reshape(-1)
        pad = (-flat.shape[0]) % 1024
        rows.append(jnp.pad(flat, (0, pad)).reshape(-1, 128))
    return jnp.concatenate(rows, axis=0)


def _unpack(packed, like):
    out, r = [], 0
    for a in like:
        n = math.prod(a.shape)
        nr = (n + 1023) // 1024 * 8
        out.append(packed[r:r + nr].reshape(-1)[:n].reshape(a.shape))
        r += nr
    return out


KIND = {"ab_in": "col", "ab_out": "row", "sb_in": "col", "sb_out": "row",
        "w1_0": "col", "w1_1": "col", "w2_0": "row", "w2_1": "row"}
GATHERS = {"ab_in": ("ab_in",), "ab_out": ("ab_out",), "w1_0": ("w1_0",), "w2_0": ("w2_0",),
           "sb": ("sb_in", "sb_out"), "ffn1": ("w1_1", "w2_1")}


class _Exchange:
    def __init__(self, shards):
        x, y, c = _coords()
        me = (4 * x + 2 * y + c).astype(jnp.int32)
        self.where = jnp.stack([jnp.bitwise_xor(me, k) for k in range(N_DEV)])
        self.full = {}
        self.tokens = []
        self.gathers = {}
        self.scatters = {}
        self.settled = {}
        block = me.reshape(1)
        after = block
        for key, group in GATHERS.items():
            for n in group:
                self.full[n] = _place(*shards[n], KIND[n], block, after, f"place_{n}")
            to_chips, to_sibling = _gather_plans([KIND[n] for n in group])
            bufs = [self.full[n] for n in group]
            sems, bufs, after = _copies_start(f"gather_start_{key}", bufs, to_chips, 4 * len(group), after)
            self.tokens.append(after)
            self.gathers[key] = (group, sems, bufs, to_chips, to_sibling)

    def tie(self, small):
        for token in self.tokens:
            small = small + token[0:1, 0:1]
        self.tokens = []
        return small

    def started(self):
        return tuple(self.tokens)

    def weight(self, name):
        return self.full[name]

    def arrive(self, key, after):
        group, sems, bufs, to_chips, to_sibling = self.gathers[key]
        sems, bufs, token = _copies_wait_start(f"gather_pass_{key}", bufs, sems, after, to_chips, to_sibling,
                                               3 * len(group))
        self.tokens.append(token)
        self.gathers[key] = (group, sems, bufs, token, to_sibling)

    def land(self, key, after=None):
        group, sems, bufs, token, to_sibling = self.gathers.pop(key)
        after = token if after is None else after
        self.full.update(zip(group, _copies_wait(f"gather_done_{key}", bufs, sems, after, to_sibling)))

    def reduce(self, key, grads):
        names = list(grads)
        kinds = [KIND[n] for n in names]
        full = [grads[n] for n in names]
        lands = [lax.empty((N_DEV,) + _shard_shape(g, k), BF16) for g, k in zip(full, kinds)]
        plan = _scatter_plan(kinds)
        sems, bufs, token = _copies_start(f"scatter_start_{key}", full + lands, plan, (N_DEV - 1) * len(names),
                                          self.where)
        self.tokens.append(token)
        self.scatters[key] = (names, sems, bufs, plan)

    def settle(self, keys, after):
        for key in keys:
            names, sems, bufs, plan = self.scatters.pop(key)
            bufs = _copies_wait(f"scatter_done_{key}", bufs, sems, after, plan)
            self.settled.update({n: t for n, *t in zip(names, bufs[:len(names)], bufs[len(names):])})
        return self.settled


SMALL = ("norm_pre_mix", "norm_post_mix", "norm_pre_ffn", "norm_post_ffn", "sgu_ln_g", "sgu_ln_b", "sgu_w", "sgu_b")
ORDER = ("norm_pre_mix", "norm_post_mix", "norm_pre_ffn", "norm_post_ffn", "ab_w_in", "sgu_ln_g", "sgu_ln_b", "sgu_w",
         "sgu_b", "ab_w_out", "sb_w_in", "sb_w_out", "ffn_w1", "ffn_w2")


def kernel(x, norm_pre_mix, norm_post_mix, norm_pre_ffn, norm_post_ffn, ab_w_in, sgu_ln_g, sgu_ln_b, sgu_w, sgu_b, ab_w_out, sb_w_in, sb_w_out, ffn_w1, ffn_w2, loss_target, m_norm_pre_mix, m_norm_post_mix, m_norm_pre_ffn, m_norm_post_ffn, m_ab_w_in, m_sgu_ln_g, m_sgu_ln_b, m_sgu_w, m_sgu_b, m_ab_w_out, m_sb_w_in, m_sb_w_out, m_ffn_w1, m_ffn_w2, v_norm_pre_mix, v_norm_post_mix, v_norm_pre_ffn, v_norm_post_ffn, v_ab_w_in, v_sgu_ln_g, v_sgu_ln_b, v_sgu_w, v_sgu_b, v_ab_w_out, v_sb_w_in, v_sb_w_out, v_ffn_w1, v_ffn_w2):
    W = dict(norm_pre_mix=norm_pre_mix, norm_post_mix=norm_post_mix, norm_pre_ffn=norm_pre_ffn,
             norm_post_ffn=norm_post_ffn, ab_w_in=ab_w_in, sgu_ln_g=sgu_ln_g, sgu_ln_b=sgu_ln_b, sgu_w=sgu_w,
             sgu_b=sgu_b, ab_w_out=ab_w_out, sb_w_in=sb_w_in, sb_w_out=sb_w_out, ffn_w1=ffn_w1, ffn_w2=ffn_w2)
    M = dict(norm_pre_mix=m_norm_pre_mix, norm_post_mix=m_norm_post_mix, norm_pre_ffn=m_norm_pre_ffn,
             norm_post_ffn=m_norm_post_ffn, ab_w_in=m_ab_w_in, sgu_ln_g=m_sgu_ln_g, sgu_ln_b=m_sgu_ln_b,
             sgu_w=m_sgu_w, sgu_b=m_sgu_b, ab_w_out=m_ab_w_out, sb_w_in=m_sb_w_in, sb_w_out=m_sb_w_out,
             ffn_w1=m_ffn_w1, ffn_w2=m_ffn_w2)
    V = dict(norm_pre_mix=v_norm_pre_mix, norm_post_mix=v_norm_post_mix, norm_pre_ffn=v_norm_pre_ffn,
             norm_post_ffn=v_norm_post_ffn, ab_w_in=v_ab_w_in, sgu_ln_g=v_sgu_ln_g, sgu_ln_b=v_sgu_ln_b,
             sgu_w=v_sgu_w, sgu_b=v_sgu_b, ab_w_out=v_ab_w_out, sb_w_in=v_sb_w_in, sb_w_out=v_sb_w_out,
             ffn_w1=v_ffn_w1, ffn_w2=v_ffn_w2)

    shards = {"ab_in": (ab_w_in, 0), "ab_out": (ab_w_out, 0), "w1_0": (ffn_w1, 0), "w2_0": (ffn_w2, 0),
              "sb_in": (sb_w_in, 0), "sb_out": (sb_w_out, 0), "w1_1": (ffn_w1, 1), "w2_1": (ffn_w2, 1)}
    comm = _Exchange(shards)
    norms = (norm_pre_mix, norm_post_mix, norm_pre_ffn, norm_post_ffn)
    sgu = (sgu_ln_g, sgu_ln_b, sgu_w[0], sgu_b[0])
    loss, dx, small = _local_step(x[0], loss_target[0], norms, sgu, comm)

    out = {}

    def update(name, layers, landed, after=()):
        out[name] = _adamw(W[name], [landed[n] for n in layers], KIND[layers[0]], comm.where, M[name], V[name],
                           after, f"adamw_{name}")

    landed = comm.settle(("ffn1", "sb", "ffn0"), after=dx)
    small_g = [small["pre_mix"], small["post_mix"], small["pre_ffn"], small["post_ffn"], small["ln_g"],
               small["ln_b"], small["w_s"][None], small["b_s"][None]]
    mine = _pack(small_g + [loss.reshape(1)])
    sems, bufs, token = _copies_start("small_start", [mine, jnp.zeros((N_DEV,) + mine.shape, F32)], _small_plan,
                                      N_DEV - 1, dx)
    for name, layers in (("sb_w_in", ["sb_in"]), ("sb_w_out", ["sb_out"]), ("ffn_w1", ["w1_0", "w1_1"]),
                         ("ffn_w2", ["w2_0", "w2_1"])):
        update(name, layers, landed, after=(token,))
    summed = _sum_small(*_copies_wait("small_done", bufs, sems, out["ffn_w2"][3], _small_plan))
    g_small, loss = summed[:-8], summed[-8, 0]
    landed = comm.settle(("ab_out", "ab_in"), after=g_small)
    update("ab_w_out", ["ab_out"], landed)
    update("ab_w_in", ["ab_in"], landed)
    res = _adamw_small(_pack([W[n] for n in SMALL]), g_small, _pack([M[n] for n in SMALL]),
                       _pack([V[n] for n in SMALL]))
    like = [W[n] for n in SMALL]
    for n, *vals in zip(SMALL, *[_unpack(r, like) for r in [g_small] + list(res)]):
        out[n] = vals

    return (loss, dx[None], *[out[n][0] for n in ORDER], *[out[n][1] for n in ORDER],
            *[out[n][2] for n in ORDER], *[out[n][3] for n in ORDER])
```

```python
import math

import jax
import jax.numpy as jnp
from jax import lax
from jax.experimental import pallas as pl
from jax.experimental.pallas import tpu as pltpu

F32 = jnp.float32
BF16 = jnp.bfloat16

HEAD_DIM = 128
CHUNK = 128
ATT_BLOCK = 128
DILATED_PAIRS = ((128, 1), (512, 4), (2048, 16))
RMS_EPS = 1e-6
LN_EPS = 1e-5
ADAM_LR = 0.001
ADAM_B1 = 0.9
ADAM_B2 = 0.999
ADAM_EPS = 1e-08
ADAM_WD = 0.01
ADAM_STEP = 10
N_DEV = 8
MASKED = -1e30

V7X_VMEM_BYTES = 64 * 1024 * 1024
VMEM_LIMIT = V7X_VMEM_BYTES - 8 * 1024 * 1024

NN = (((1,), (0,)), ((), ()))
NT = (((1,), (1,)), ((), ()))
TN = (((0,), (0,)), ((), ()))


def _params(*sem):
    return pltpu.CompilerParams(dimension_semantics=sem, vmem_limit_bytes=VMEM_LIMIT)


def _dot(a, b, dims=NN):
    return lax.dot_general(a, b, dims, preferred_element_type=F32)


def _tile(n, preferred):
    if n <= preferred:
        return n
    t = preferred - preferred % 128
    while n % t:
        t -= 128
    assert t > 0, (n, preferred)
    return t


def _matmul(a, b, *, mode, name, out_dtype=F32, tm=1024, tn=1024, tk=2048, epi=None, extras=(), after=()):
    if mode == "nn":
        (M, K), N = a.shape, b.shape[1]
    elif mode == "nt":
        (M, K), N = a.shape, b.shape[0]
    else:
        (K, M), N = a.shape, b.shape[1]
    tm, tn, tk = _tile(M, tm), _tile(N, tn), _tile(K, tk)
    nk = K // tk
    if mode == "tn":
        a_spec = pl.BlockSpec((tk, tm), lambda i, j, k: (k, i))
    else:
        a_spec = pl.BlockSpec((tm, tk), lambda i, j, k: (i, k))
    if mode == "nt":
        b_spec = pl.BlockSpec((tn, tk), lambda i, j, k: (j, k))
    else:
        b_spec = pl.BlockSpec((tk, tn), lambda i, j, k: (k, j))
    o_spec = pl.BlockSpec((tm, tn), lambda i, j, k: (i, j))
    dims = {"nn": NN, "nt": NT, "tn": TN}[mode]
    n_extra = len(extras)
    n_in = n_extra + len(after)
    several = isinstance(out_dtype, tuple)
    n_out = len(out_dtype) if several else 1

    def finish(acc, rest):
        outs = acc if epi is None else epi(acc, pl.program_id(1), *[r[...] for r in rest[:n_extra]])
        for o_ref, o in zip(rest[n_in:n_in + n_out], outs if several else (outs,)):
            o_ref[...] = o.astype(o_ref.dtype)

    if nk == 1:
        def body(a_ref, b_ref, *rest):
            finish(_dot(a_ref[...], b_ref[...], dims), rest)
        scratch = []
    else:
        def body(a_ref, b_ref, *rest):
            acc_ref = rest[n_in + n_out]
            k = pl.program_id(2)

            @pl.when(k == 0)
            def _():
                acc_ref[...] = jnp.zeros_like(acc_ref)

            acc_ref[...] += _dot(a_ref[...], b_ref[...], dims)

            @pl.when(k == nk - 1)
            def _():
                finish(acc_ref[...], rest)
        scratch = [pltpu.VMEM((tm, tn), F32)]

    shapes = [jax.ShapeDtypeStruct((M, N), d) for d in (out_dtype if several else (out_dtype,))]
    return pl.pallas_call(
        body,
        name=name,
        grid=(M // tm, N // tn, nk),
        in_specs=[a_spec, b_spec] + [o_spec] * n_extra + [ANY] * len(after),
        out_specs=[o_spec] * n_out if several else o_spec,
        out_shape=shapes if several else shapes[0],
        scratch_shapes=scratch,
        compiler_params=_params("parallel", "parallel", "arbitrary"),
    )(a, b, *extras, *after)


ROWS = 512


def _rms(x):
    return lax.rsqrt(jnp.mean(x * x, axis=-1, keepdims=True) + RMS_EPS)


def _prenorm(x, g, name):
    T, D = x.shape

    def body(x_ref, g_ref, h_ref):
        xv = x_ref[...]
        h_ref[...] = (xv * _rms(xv) * g_ref[...]).astype(BF16)

    row = pl.BlockSpec((ROWS, D), lambda i: (i, 0))
    vec = pl.BlockSpec((1, D), lambda i: (0, 0))
    return pl.pallas_call(
        body, name=name, grid=(T // ROWS,), in_specs=[row, vec], out_specs=row,
        out_shape=jax.ShapeDtypeStruct((T, D), BF16), compiler_params=_params("parallel"),
    )(x, g)


def _postnorm_prenorm(x, y, g_post, g_pre, name):
    T, D = x.shape

    def body(x_ref, y_ref, gp_ref, gn_ref, xo_ref, h_ref):
        yv = y_ref[...]
        xn = x_ref[...] + yv * _rms(yv) * gp_ref[...]
        xo_ref[...] = xn
        h_ref[...] = (xn * _rms(xn) * gn_ref[...]).astype(BF16)

    row = pl.BlockSpec((ROWS, D), lambda i: (i, 0))
    vec = pl.BlockSpec((1, D), lambda i: (0, 0))
    return pl.pallas_call(
        body, name=name, grid=(T // ROWS,), in_specs=[row, row, vec, vec], out_specs=[row, row],
        out_shape=[jax.ShapeDtypeStruct((T, D), F32), jax.ShapeDtypeStruct((T, D), BF16)],
        compiler_params=_params("parallel"),
    )(x, y, g_post, g_pre)


def _postnorm_grads(dn, yh, r, g):
    gd = dn * g
    return r * (gd - yh * jnp.mean(yh * gd, axis=-1, keepdims=True)), dn * yh


def _postnorm_loss(x, y, g_post, target, name):
    T, D = x.shape

    def body(x_ref, y_ref, gp_ref, t_ref, loss_ref, dx_ref, dy_ref, dg_ref):
        @pl.when(pl.program_id(0) == 0)
        def _():
            loss_ref[...] = jnp.zeros_like(loss_ref)
            dg_ref[...] = jnp.zeros_like(dg_ref)

        yv = y_ref[...]
        r = _rms(yv)
        yh = yv * r
        err = x_ref[...] + yh * gp_ref[...] - t_ref[...]
        dx = err * (1.0 / D)
        dx_ref[...] = dx
        loss_ref[...] += 0.5 * jnp.sum(jnp.sum(err * err, axis=-1, keepdims=True) * (1.0 / D))
        dy, dg = _postnorm_grads(dx, yh, r, gp_ref[...])
        dy_ref[...] = dy.astype(BF16)
        dg_ref[...] += jnp.sum(dg, axis=0, keepdims=True)

    row = pl.BlockSpec((ROWS, D), lambda i: (i, 0))
    vec = pl.BlockSpec((1, D), lambda i: (0, 0))
    acc = pl.BlockSpec((8, 128), lambda i: (0, 0))
    return pl.pallas_call(
        body, name=name, grid=(T // ROWS,), in_specs=[row, row, vec, row], out_specs=[acc, row, row, vec],
        out_shape=[jax.ShapeDtypeStruct((8, 128), F32), jax.ShapeDtypeStruct((T, D), F32),
                   jax.ShapeDtypeStruct((T, D), BF16), jax.ShapeDtypeStruct((1, D), F32)],
        compiler_params=_params("arbitrary"),
    )(x, y, g_post, target)


def _norm_bwd_pair(dx_out, dh, x, g_pre, y_prev, g_post_prev, name):
    T, D = x.shape

    def body(dxo_ref, dh_ref, x_ref, g_ref, y_ref, gp_ref, dx_ref, dg_ref, dy_ref, dgp_ref):
        @pl.when(pl.program_id(0) == 0)
        def _():
            dg_ref[...] = jnp.zeros_like(dg_ref)
            dgp_ref[...] = jnp.zeros_like(dgp_ref)

        xv, dhv = x_ref[...], dh_ref[...]
        r = _rms(xv)
        xh = xv * r
        gd = dhv * g_ref[...]
        dx = dxo_ref[...] + r * (gd - xh * jnp.mean(xh * gd, axis=-1, keepdims=True))
        dx_ref[...] = dx
        dg_ref[...] += jnp.sum(dhv * xh, axis=0, keepdims=True)
        yv = y_ref[...]
        ry = _rms(yv)
        dy, dgp = _postnorm_grads(dx, yv * ry, ry, gp_ref[...])
        dy_ref[...] = dy.astype(BF16)
        dgp_ref[...] += jnp.sum(dgp, axis=0, keepdims=True)

    row = pl.BlockSpec((ROWS, D), lambda i: (i, 0))
    vec = pl.BlockSpec((1, D), lambda i: (0, 0))
    return pl.pallas_call(
        body, name=name, grid=(T // ROWS,), in_specs=[row, row, row, vec, row, vec],
        out_specs=[row, vec, row, vec],
        out_shape=[jax.ShapeDtypeStruct((T, D), F32), jax.ShapeDtypeStruct((1, D), F32),
                   jax.ShapeDtypeStruct((T, D), BF16), jax.ShapeDtypeStruct((1, D), F32)],
        compiler_params=_params("arbitrary"),
    )(dx_out, dh, x, g_pre, y_prev, g_post_prev)


def _prenorm_bwd(dx_out, dh, x, g_pre, name):
    T, D = x.shape

    def body(dxo_ref, dh_ref, x_ref, g_ref, dx_ref, dg_ref):
        @pl.when(pl.program_id(0) == 0)
        def _():
            dg_ref[...] = jnp.zeros_like(dg_ref)

        xv, dhv = x_ref[...], dh_ref[...]
        r = _rms(xv)
        xh = xv * r
        gd = dhv * g_ref[...]
        dx_ref[...] = dxo_ref[...] + r * (gd - xh * jnp.mean(xh * gd, axis=-1, keepdims=True))
        dg_ref[...] += jnp.sum(dhv * xh, axis=0, keepdims=True)

    row = pl.BlockSpec((ROWS, D), lambda i: (i, 0))
    vec = pl.BlockSpec((1, D), lambda i: (0, 0))
    return pl.pallas_call(
        body, name=name, grid=(T // ROWS,), in_specs=[row, row, row, vec], out_specs=[row, vec],
        out_shape=[jax.ShapeDtypeStruct((T, D), F32), jax.ShapeDtypeStruct((1, D), F32)],
        compiler_params=_params("arbitrary"),
    )(dx_out, dh, x, g_pre)


_INV_SQRT2 = 1.0 / math.sqrt(2.0)
_INV_SQRT2PI = 1.0 / math.sqrt(2.0 * math.pi)


def _gelu(x):
    return 0.5 * x * (1.0 + lax.erf(x * _INV_SQRT2))


def _gelu_grad(x):
    return 0.5 * (1.0 + lax.erf(x * _INV_SQRT2)) + x * jnp.exp(-0.5 * x * x) * _INV_SQRT2PI


def _layernorm_stats(x):
    mu = jnp.mean(x, axis=-1, keepdims=True)
    xc = x - mu
    rstd = lax.rsqrt(jnp.mean(xc * xc, axis=-1, keepdims=True) + LN_EPS)
    return xc * rstd, rstd


def _tril_mask():
    i = lax.broadcasted_iota(jnp.int32, (CHUNK, CHUNK), 0)
    j = lax.broadcasted_iota(jnp.int32, (CHUNK, CHUNK), 1)
    return j <= i


SGU_ROWS = 512


def _sgu_fwd(z, ln_g, ln_b, w_s, b_t, name):
    T = z.shape[0]
    A = ln_g.shape[1]
    G = A // 128
    rows = min(SGU_ROWS, T)

    def body(u_ref, v_ref, g_ref, b_ref, w_ref, bt_ref, o_ref):
        mask = _tril_mask()
        for c in range(rows // CHUNK):
            rs = pl.ds(c * CHUNK, CHUNK)
            xh, _ = _layernorm_stats(_gelu(v_ref[rs, :]))
            vn = (xh * g_ref[...] + b_ref[...]).astype(BF16)
            for g in range(G):
                cs = pl.ds(g * 128, 128)
                w = jnp.where(mask, w_ref[g], 0.0).astype(BF16)
                mixed = _dot(w, vn[:, g * 128:(g + 1) * 128]) + bt_ref[:, g:g + 1]
                o_ref[rs, cs] = (_gelu(u_ref[rs, cs]) * mixed).astype(BF16)

    return pl.pallas_call(
        body, name=name, grid=(T // rows,),
        in_specs=[
            pl.BlockSpec((rows, A), lambda i: (i, 0)),
            pl.BlockSpec((rows, A), lambda i: (i, 1)),
            pl.BlockSpec((1, A), lambda i: (0, 0)),
            pl.BlockSpec((1, A), lambda i: (0, 0)),
            pl.BlockSpec((G, CHUNK, CHUNK), lambda i: (0, 0, 0)),
            pl.BlockSpec((CHUNK, G), lambda i: (0, 0)),
        ],
        out_specs=pl.BlockSpec((rows, A), lambda i: (i, 0)),
        out_shape=jax.ShapeDtypeStruct((T, 2 * A), BF16),
        compiler_params=_params("parallel"),
    )(z, z, ln_g, ln_b, w_s, b_t)


def _sgu_bwd(z, dcat, ln_g, ln_b, w_s, b_t, name):
    T = z.shape[0]
    A = ln_g.shape[1]
    G = A // 128
    rows = min(SGU_ROWS, T)

    def body(u_ref, v_ref, da_ref, g_ref, b_ref, w_ref, bt_ref, dz_ref, dg_ref, db_ref, dw_ref, dbt_ref, dvn_ref):
        @pl.when(pl.program_id(0) == 0)
        def _():
            dg_ref[...] = jnp.zeros_like(dg_ref)
            db_ref[...] = jnp.zeros_like(db_ref)
            dw_ref[...] = jnp.zeros_like(dw_ref)
            dbt_ref[...] = jnp.zeros_like(dbt_ref)

        mask = _tril_mask()
        for c in range(rows // CHUNK):
            rs = pl.ds(c * CHUNK, CHUNK)
            vv = v_ref[rs, :]
            gv = _gelu(vv)
            xh, rstd = _layernorm_stats(gv)
            vn = (xh * g_ref[...] + b_ref[...]).astype(BF16)
            for g in range(G):
                cs = pl.ds(g * 128, 128)
                w = jnp.where(mask, w_ref[g], 0.0).astype(BF16)
                vg = vn[:, g * 128:(g + 1) * 128]
                mixed = _dot(w, vg) + bt_ref[:, g:g + 1]
                uu = u_ref[rs, cs]
                da = da_ref[rs, cs]
                dz_ref[rs, cs] = (da * mixed * _gelu_grad(uu)).astype(BF16)
                dm = da * _gelu(uu)
                dmb = dm.astype(BF16)
                dbt_ref[:, g:g + 1] += jnp.sum(dm, axis=1, keepdims=True)
                dw_ref[g] += jnp.where(mask, _dot(dmb, vg, NT), 0.0)
                dvn_ref[:, cs] = _dot(w, dmb, TN)
            dvn = dvn_ref[...]
            dg_ref[...] += jnp.sum(dvn * xh, axis=0, keepdims=True)
            db_ref[...] += jnp.sum(dvn, axis=0, keepdims=True)
            dxh = dvn * g_ref[...]
            dgv = rstd * (dxh - jnp.mean(dxh, axis=-1, keepdims=True)
                          - xh * jnp.mean(dxh * xh, axis=-1, keepdims=True))
            dz_ref[rs, pl.ds(A, A)] = (dgv * _gelu_grad(vv)).astype(BF16)

    vec = pl.BlockSpec((1, A), lambda i: (0, 0))
    wsp = pl.BlockSpec((G, CHUNK, CHUNK), lambda i: (0, 0, 0))
    bsp = pl.BlockSpec((CHUNK, G), lambda i: (0, 0))
    return pl.pallas_call(
        body, name=name, grid=(T // rows,),
        in_specs=[
            pl.BlockSpec((rows, A), lambda i: (i, 0)),
            pl.BlockSpec((rows, A), lambda i: (i, 1)),
            pl.BlockSpec((rows, A), lambda i: (i, 0)),
            vec, vec, wsp, bsp,
        ],
        out_specs=[pl.BlockSpec((rows, 2 * A), lambda i: (i, 0)), vec, vec, wsp, bsp],
        out_shape=[
            jax.ShapeDtypeStruct((T, 2 * A), BF16),
            jax.ShapeDtypeStruct((1, A), F32),
            jax.ShapeDtypeStruct((1, A), F32),
            jax.ShapeDtypeStruct((G, CHUNK, CHUNK), F32),
            jax.ShapeDtypeStruct((CHUNK, G), F32),
        ],
        scratch_shapes=[pltpu.VMEM((CHUNK, A), F32)],
        compiler_params=_params("arbitrary"),
    )(z, z, dcat, ln_g, ln_b, w_s, b_t)


def _alibi_row(B, d):
    H = B // HEAD_DIM
    slopes = [d * 2.0 ** (-8.0 * (h + 1.0) / H) for h in range(H)]
    return jnp.repeat(jnp.asarray(slopes, F32), HEAD_DIM)[None, :]


def _dil_scores(q, k, slope_d, valid, dist):
    s = _dot(q, k, NT) - slope_d * dist
    return jnp.where(valid, s, MASKED)


def _dil_block(n, r, d):
    if d == 1:
        return pl.ds(pl.multiple_of(n * ATT_BLOCK, ATT_BLOCK), ATT_BLOCK)
    return pl.ds(n * (d * ATT_BLOCK) + r, ATT_BLOCK, stride=d)


def _dilated_forward(z, cat, B, name):
    T = z.shape[0]
    H = B // HEAD_DIM
    A = cat.shape[1] - B
    scale = HEAD_DIM ** -0.5
    blk = ATT_BLOCK
    chunk = _tile(T, 512)

    def body(q_ref, k_ref, v_ref, sl_ref, cat_in, cat_ref, of_ref, lt_ref, *branch):
        o_refs, l_refs = branch[:3], branch[3:]
        slope = sl_ref[:, :1]
        qi = lax.broadcasted_iota(jnp.int32, (blk, 2 * blk), 0)
        kj = lax.broadcasted_iota(jnp.int32, (blk, 2 * blk), 1)
        dist = qi + blk - kj
        band = (dist >= 0) & (dist <= blk)
        distf = dist.astype(F32)

        for b, (_, d) in enumerate(DILATED_PAIRS):
            def one(n, r, b=b, d=d):
                rows, prev = _dil_block(n, r, d), _dil_block(jnp.maximum(n - 1, 0), r, d)
                q = (q_ref[rows, :] * scale).astype(BF16)
                k = jnp.concatenate([k_ref[prev, :], k_ref[rows, :]], axis=0).astype(BF16)
                v = jnp.concatenate([v_ref[prev, :], v_ref[rows, :]], axis=0).astype(BF16)
                s = _dil_scores(q, k, slope * float(d), band & ((kj >= blk) | (n > 0)), distf)
                m = jnp.max(s, axis=-1, keepdims=True)
                p = jnp.exp(s - m)
                den = jnp.sum(p, axis=-1, keepdims=True)
                o_refs[b][rows, :] = _dot(p.astype(BF16), v) / den
                l_refs[b][rows, :] = jnp.broadcast_to(m + jnp.log(den), (blk, HEAD_DIM))

            per = max(1, 4 // d)

            def step(i, _, d=d, per=per, one=one):
                for u in range(per):
                    for r in range(d):
                        one(i * per + u, r)
                return 0

            lax.fori_loop(0, T // (d * blk * per), step, 0)

        def merge(i, _):
            rs = pl.ds(pl.multiple_of(i * chunk, chunk), chunk)
            a, b, c = l_refs[0][rs, :], l_refs[1][rs, :], l_refs[2][rs, :]
            m = jnp.maximum(jnp.maximum(a, b), c)
            ea, eb, ec = jnp.exp(a - m), jnp.exp(b - m), jnp.exp(c - m)
            tot = ea + eb + ec
            o = (ea * o_refs[0][rs, :] + eb * o_refs[1][rs, :] + ec * o_refs[2][rs, :]) / tot
            of_ref[rs, :] = o
            cat_ref[rs, :] = o.astype(BF16)
            lt_ref[rs, :] = m + jnp.log(tot)
            return 0

        lax.fori_loop(0, T // chunk, merge, 0)

    def col(unit):
        return lambda h: (0, unit * H + h)

    seq = (T, HEAD_DIM)
    out = pl.BlockSpec(seq, lambda h: (0, h))
    return pl.pallas_call(
        body, name=name, grid=(H,),
        in_specs=[pl.BlockSpec(seq, col(2)), pl.BlockSpec(seq, col(3)), pl.BlockSpec(seq, col(4)),
                  pl.BlockSpec((1, HEAD_DIM), lambda h: (0, h)), ANY],
        out_specs=[pl.BlockSpec(seq, lambda h: (0, A // HEAD_DIM + h)), out, out],
        out_shape=[jax.ShapeDtypeStruct(cat.shape, BF16), jax.ShapeDtypeStruct((T, B), F32),
                   jax.ShapeDtypeStruct((T, B), F32)],
        input_output_aliases={4: 0},
        scratch_shapes=[pltpu.VMEM(seq, F32)] * 6,
        compiler_params=_params("parallel"),
    )(z, z, z, _alibi_row(B, 1), cat)


def _dilated_backward(z, dcat, o, lse, B, name):
    T = z.shape[0]
    H = B // HEAD_DIM
    scale = HEAD_DIM ** -0.5
    blk = ATT_BLOCK
    chunk = _tile(T, 512)

    def body(q_ref, k_ref, v_ref, do_ref, o_ref, l_ref, sl_ref, dq_ref, dk_ref, dv_ref, aq_ref, ak_ref, av_ref):
        slope = sl_ref[:, :1]
        qi = lax.broadcasted_iota(jnp.int32, (blk, 2 * blk), 0)
        kj = lax.broadcasted_iota(jnp.int32, (blk, 2 * blk), 1)
        dist = qi + blk - kj
        band = (dist >= 0) & (dist <= blk)
        distf = dist.astype(F32)
        for acc in (aq_ref, ak_ref, av_ref):
            acc[...] = jnp.zeros_like(acc)

        for _, d in DILATED_PAIRS:
            def one(n, r, d=d):
                rows, prev = _dil_block(n, r, d), _dil_block(jnp.maximum(n - 1, 0), r, d)
                q = (q_ref[rows, :] * scale).astype(BF16)
                k = jnp.concatenate([k_ref[prev, :], k_ref[rows, :]], axis=0).astype(BF16)
                v = jnp.concatenate([v_ref[prev, :], v_ref[rows, :]], axis=0).astype(BF16)
                do = do_ref[rows, :]
                delta = jnp.sum(do * o_ref[rows, :], axis=-1, keepdims=True)
                do = do.astype(BF16)
                s = _dil_scores(q, k, slope * float(d), band & ((kj >= blk) | (n > 0)), distf)
                p = jnp.exp(s - l_ref[rows, :][:, :1])
                ds = (p * (_dot(do, v, NT) - delta)).astype(BF16)
                aq_ref[rows, :] += _dot(ds, k)
                dk = _dot(ds, q, TN)
                dv = _dot(p.astype(BF16), do, TN)
                ak_ref[prev, :] += dk[:blk]
                av_ref[prev, :] += dv[:blk]
                ak_ref[rows, :] += dk[blk:]
                av_ref[rows, :] += dv[blk:]

            per = max(1, 4 // d)

            def step(i, _, d=d, per=per, one=one):
                for u in range(per):
                    for r in range(d):
                        one(i * per + u, r)
                return 0

            lax.fori_loop(0, T // (d * blk * per), step, 0)

        def emit(i, _):
            rs = pl.ds(pl.multiple_of(i * chunk, chunk), chunk)
            dq_ref[rs, :] = (aq_ref[rs, :] * scale).astype(BF16)
            dk_ref[rs, :] = ak_ref[rs, :].astype(BF16)
            dv_ref[rs, :] = av_ref[rs, :].astype(BF16)
            return 0

        lax.fori_loop(0, T // chunk, emit, 0)

    def col(unit):
        return lambda h: (0, unit * H + h)

    seq = (T, HEAD_DIM)
    own = pl.BlockSpec(seq, lambda h: (0, h))
    return pl.pallas_call(
        body, name=name, grid=(H,),
        in_specs=[pl.BlockSpec(seq, col(2)), pl.BlockSpec(seq, col(3)), pl.BlockSpec(seq, col(4)),
                  pl.BlockSpec(seq, col(1)), own, own, pl.BlockSpec((1, HEAD_DIM), lambda h: (0, h))],
        out_specs=[own] * 3,
        out_shape=[jax.ShapeDtypeStruct((T, B), BF16)] * 3,
        scratch_shapes=[pltpu.VMEM(seq, F32)] * 3,
        compiler_params=_params("parallel"),
    )(z, z, z, dcat, o, lse, _alibi_row(B, 1))


def _join_columns(parts, name):
    T = parts[0].shape[0]
    widths = [p.shape[1] for p in parts]

    def body(*refs):
        o_ref, at = refs[-1], 0
        for ref, w in zip(refs[:-1], widths):
            o_ref[:, pl.ds(at, w)] = ref[...]
            at += w

    return pl.pallas_call(
        body, name=name, grid=(T // ROWS,),
        in_specs=[pl.BlockSpec((ROWS, w), lambda i: (i, 0)) for w in widths],
        out_specs=pl.BlockSpec((ROWS, sum(widths)), lambda i: (i, 0)),
        out_shape=jax.ShapeDtypeStruct((T, sum(widths)), BF16), compiler_params=_params("parallel"),
    )(*parts)


SB_FORWARD_ROWS = 2048
SB_BACKWARD_ROWS = 1024
SB_KEYS = 2 * ATT_BLOCK


def _tri_and_ones(pred):
    rows = lax.broadcasted_iota(jnp.int32, (2 * ATT_BLOCK, 2 * ATT_BLOCK), 0) % ATT_BLOCK
    cols = lax.broadcasted_iota(jnp.int32, (2 * ATT_BLOCK, 2 * ATT_BLOCK), 1)
    return ((cols >= ATT_BLOCK) | pred(rows, cols)).astype(BF16)


def _running(x, tri):
    hi = x.astype(BF16)
    lo = (x - hi.astype(F32)).astype(BF16)
    return _dot(jnp.concatenate([hi, lo], axis=1), tri)


def _sb_mask(query_rows, s):
    rows = lax.broadcasted_iota(jnp.int32, (query_rows - s * SB_KEYS, SB_KEYS), 0)
    cols = lax.broadcasted_iota(jnp.int32, (query_rows - s * SB_KEYS, SB_KEYS), 1)
    return cols < rows


def _log_sigmoids(z):
    ls = jnp.minimum(z, 0.0) - jnp.log(1.0 + jnp.exp(-jnp.abs(z)))
    return ls, ls - z


def _sb_fwd(qkv, W, name):
    T = qkv.shape[0]
    H = W // HEAD_DIM
    blk = ATT_BLOCK
    qb = min(SB_FORWARD_ROWS, T)
    per = qb // SB_KEYS

    def body(q_ref, k_ref, v_ref, o_ref, lt_ref, acc_ref):
        i = pl.program_id(1)
        tri = _tri_and_ones(lambda r, c: r > c)
        lt_ref[...] = jnp.zeros_like(lt_ref)
        acc_ref[...] = jnp.zeros_like(acc_ref)

        def tile(j, mask, r0=0):
            ks = pl.ds(pl.multiple_of(j * SB_KEYS, SB_KEYS), SB_KEYS)
            qs = pl.ds(r0, qb - r0)
            z = _dot(q_ref[qs, :], k_ref[ks, :], NT)
            ls, lm = _log_sigmoids(z)
            if mask is not None:
                lm = jnp.where(mask, lm, 0.0)
            later = lt_ref[qs, :]
            second = _running(lm[:, blk:], tri)
            first = _running(lm[:, :blk], tri)
            after_first = later + second[:, blk:]
            a = jnp.exp(ls + jnp.concatenate([first[:, :blk] + after_first, second[:, :blk] + later], axis=1))
            if mask is not None:
                a = jnp.where(mask, a, 0.0)
            acc_ref[qs, :] += _dot(a.astype(BF16), v_ref[ks, :])
            lt_ref[qs, :] = after_first + first[:, blk:]

        for s in reversed(range(per)):
            tile(i * per + s, _sb_mask(qb, s), s * SB_KEYS)

        def step(jj, _):
            for s in range(per):
                tile((i - jj) * per - 1 - s, None)
            return 0

        lax.fori_loop(0, i, step, 0)
        o_ref[...] = acc_ref[...].astype(BF16)

    qs = pl.BlockSpec((qb, HEAD_DIM), lambda h, i: (i, h))
    return pl.pallas_call(
        body, name=name, grid=(H, T // qb),
        in_specs=[qs, pl.BlockSpec((T, HEAD_DIM), lambda h, i: (0, H + h)),
                  pl.BlockSpec((T, HEAD_DIM), lambda h, i: (0, 2 * H + h))],
        out_specs=[qs, qs],
        out_shape=[jax.ShapeDtypeStruct((T, W), BF16), jax.ShapeDtypeStruct((T, W), F32)],
        scratch_shapes=[pltpu.VMEM((qb, HEAD_DIM), F32)],
        compiler_params=_params("parallel", "arbitrary"),
    )(qkv, qkv, qkv)


def _sb_bwd(qkv, do, ltot, W, name):
    T = qkv.shape[0]
    H = W // HEAD_DIM
    blk = ATT_BLOCK
    nkb = T // SB_KEYS
    qb = min(SB_BACKWARD_ROWS, T)
    per = qb // SB_KEYS

    def body(q_ref, k_ref, v_ref, do_ref, lt_ref, dq_ref, dkt_ref, dvt_ref, qt_ref, dot_ref, plm_ref, pg_ref):
        i = pl.program_id(1)

        @pl.when(i == 0)
        def _():
            dkt_ref[...] = jnp.zeros_like(dkt_ref)
            dvt_ref[...] = jnp.zeros_like(dvt_ref)

        qt_ref[...] = q_ref[...].astype(F32).T.astype(BF16)
        dot_ref[...] = do_ref[...].astype(F32).T.astype(BF16)
        upto = _tri_and_ones(lambda r, c: r <= c)
        before = _tri_and_ones(lambda r, c: r < c)
        plm_ref[...] = jnp.zeros_like(plm_ref)
        pg_ref[...] = jnp.zeros_like(pg_ref)
        dq_ref[...] = jnp.zeros_like(dq_ref)

        def tile(j, mask, r0=0):
            ks = pl.ds(pl.multiple_of(j * SB_KEYS, SB_KEYS), SB_KEYS)
            qs = pl.ds(r0, qb - r0)
            k = k_ref[ks, :]
            v = v_ref[ks, :]
            z = _dot(q_ref[qs, :], k, NT)
            ls, lm = _log_sigmoids(z)
            nsig = jnp.exp(lm)
            if mask is not None:
                lm = jnp.where(mask, lm, 0.0)
            earlier = plm_ref[qs, :]
            first = _running(lm[:, :blk], upto)
            second = _running(lm[:, blk:], upto)
            upto_first = earlier + first[:, blk:]
            seen = jnp.concatenate([first[:, :blk] + earlier, second[:, :blk] + upto_first], axis=1)
            ltot = lt_ref[qs, :]
            a = jnp.exp(ls + (jnp.concatenate([ltot, ltot], axis=1) - seen))
            if mask is not None:
                a = jnp.where(mask, a, 0.0)
            g = a * _dot(do_ref[qs, :], v, NT)
            g_earlier = pg_ref[qs, :]
            g_first = _running(g[:, :blk], before)
            g_second = _running(g[:, blk:], before)
            g_upto_first = g_earlier + g_first[:, blk:]
            gsum = jnp.concatenate([g_first[:, :blk] + g_earlier, g_second[:, :blk] + g_upto_first], axis=1)
            dz = g * nsig - gsum * jnp.exp(ls)
            if mask is not None:
                dz = jnp.where(mask, dz, 0.0)
            dzb = dz.astype(BF16)
            dkt_ref[j] += _dot(qt_ref[:, qs], dzb)
            dvt_ref[j] += _dot(dot_ref[:, qs], a.astype(BF16))
            dq_ref[qs, :] += _dot(dzb, k)
            plm_ref[qs, :] = upto_first + second[:, blk:]
            pg_ref[qs, :] = g_upto_first + g_second[:, blk:]

        def step(jj, _):
            for s in range(per):
                tile(jj * per + s, None)
            return 0

        lax.fori_loop(0, i, step, 0)
        for s in range(per):
            tile(i * per + s, _sb_mask(qb, s), s * SB_KEYS)

    qs = pl.BlockSpec((qb, HEAD_DIM), lambda h, i: (i, h))
    res = pl.BlockSpec((None, nkb, HEAD_DIM, SB_KEYS), lambda h, i: (h, 0, 0, 0))
    return pl.pallas_call(
        body, name=name, grid=(H, T // qb),
        in_specs=[qs, pl.BlockSpec((T, HEAD_DIM), lambda h, i: (0, H + h)),
                  pl.BlockSpec((T, HEAD_DIM), lambda h, i: (0, 2 * H + h)), qs, qs],
        out_specs=[qs, res, res],
        out_shape=[jax.ShapeDtypeStruct((T, W), F32)] + [jax.ShapeDtypeStruct((H, nkb, HEAD_DIM, SB_KEYS), F32)] * 2,
        scratch_shapes=[pltpu.VMEM((HEAD_DIM, qb), BF16), pltpu.VMEM((HEAD_DIM, qb), BF16),
                        pltpu.VMEM((qb, HEAD_DIM), F32), pltpu.VMEM((qb, HEAD_DIM), F32)],
        compiler_params=_params("parallel", "arbitrary"),
    )(qkv, qkv, qkv, do, ltot)


def _sb_pack(dq, dkt, dvt, name):
    T, W = dq.shape
    H = W // HEAD_DIM
    blk = SB_KEYS
    scale = HEAD_DIM ** -0.5

    def body(q_ref, kt_ref, vt_ref, o_ref):
        o_ref[:, pl.ds(0, W)] = (q_ref[...] * scale).astype(BF16)
        for h in range(H):
            o_ref[:, pl.ds(W + h * HEAD_DIM, HEAD_DIM)] = kt_ref[h].T.astype(BF16)
            o_ref[:, pl.ds(2 * W + h * HEAD_DIM, HEAD_DIM)] = vt_ref[h].T.astype(BF16)

    tr = pl.BlockSpec((H, None, HEAD_DIM, blk), lambda i: (0, i, 0, 0))
    return pl.pallas_call(
        body, name=name, grid=(T // blk,), in_specs=[pl.BlockSpec((blk, W), lambda i: (i, 0)), tr, tr],
        out_specs=pl.BlockSpec((blk, 3 * W), lambda i: (i, 0)),
        out_shape=jax.ShapeDtypeStruct((T, 3 * W), BF16), compiler_params=_params("parallel"),
    )(dq, dkt, dvt)


def _local_step(x, target, norms, sgu, comm):
    T, D = x.shape
    A = D // 2
    pre_mix, post_mix, pre_ffn, post_ffn = norms
    ln_g, ln_b, w_s, b_s = sgu
    b_t = b_s.T
    scale = HEAD_DIM ** -0.5

    def vec(p, layer):
        return comm.tie(p[layer:layer + 1])

    h0 = _prenorm(x, vec(pre_mix, 0), "prenorm0")
    comm.arrive("ab_in", after=h0)
    comm.land("ab_in")
    z = _matmul(h0, comm.weight("ab_in"), mode="nn", name="ab_in_fwd")
    comm.arrive("ab_out", after=z)
    cat = _sgu_fwd(z, ln_g, ln_b, w_s, b_t, "sgu_fwd")
    cat, o_dil, lse_dil = _dilated_forward(z, cat, A, "dilated_fwd")
    comm.land("ab_out", after=o_dil)
    y0 = _matmul(cat, comm.weight("ab_out"), mode="nn", name="ab_out_fwd")
    comm.arrive("w1_0", after=y0)
    x1, h1 = _postnorm_prenorm(x, y0, vec(post_mix, 0), vec(pre_ffn, 0), "norm_mix0")
    comm.land("w1_0", after=h1)
    comm.arrive("w2_0", after=h1)

    def relu2(acc, j):
        r = jnp.maximum(acc, 0.0)
        return r * r, 2.0 * r

    f0, r0 = _matmul(h1, comm.weight("w1_0"), mode="nn", name="ffn0_w1_fwd", out_dtype=(BF16, BF16), epi=relu2)
    comm.land("w2_0", after=f0)
    y1 = _matmul(f0, comm.weight("w2_0"), mode="nn", name="ffn0_w2_fwd")
    comm.arrive("sb", after=y1)
    x2, h2 = _postnorm_prenorm(x1, y1, vec(post_ffn, 0), vec(pre_mix, 1), "norm_ffn0")
    comm.land("sb", after=h2)

    tn_qkv = _tile(D, 1024)
    nq = D // tn_qkv

    def scale_q(acc, j):
        return jnp.where(j < nq, acc * scale, acc)

    qkv = _matmul(h2, comm.weight("sb_in"), mode="nn", name="sb_in_fwd", out_dtype=BF16, tn=tn_qkv, epi=scale_q)
    comm.arrive("w1_1", after=qkv)
    o_sb, ltot = _sb_fwd(qkv, D, "sb_fwd")
    comm.land("w1_1", after=o_sb)
    comm.arrive("w2_1", after=o_sb)
    y2 = _matmul(o_sb, comm.weight("sb_out"), mode="nn", name="sb_out_fwd")
    comm.land("w2_1", after=y2)
    x3, h3 = _postnorm_prenorm(x2, y2, vec(post_mix, 1), vec(pre_ffn, 1), "norm_mix1")
    f1, r1 = _matmul(h3, comm.weight("w1_1"), mode="nn", name="ffn1_w1_fwd", out_dtype=(BF16, BF16), epi=relu2)
    y3 = _matmul(f1, comm.weight("w2_1"), mode="nn", name="ffn1_w2_fwd")
    loss_tile, dx4, dy3, dg_post_ffn1 = _postnorm_loss(x3, y3, vec(post_ffn, 1), target, "norm_loss")
    loss = loss_tile[0, 0]

    def relu2_bwd(acc, j, r):
        return acc * r.astype(F32)

    def ffn_bwd(dy, h, f, r, layer):
        g_w2 = _matmul(f, dy, mode="tn", name=f"ffn{layer}_w2_wgrad", out_dtype=BF16)
        da = _matmul(dy, comm.weight(f"w2_{layer}"), mode="nt", name=f"ffn{layer}_w2_dgrad", out_dtype=BF16,
                     epi=relu2_bwd, extras=(r,))
        g_w1 = _matmul(h, da, mode="tn", name=f"ffn{layer}_w1_wgrad", out_dtype=BF16)
        comm.reduce(f"ffn{layer}", {f"w2_{layer}": g_w2, f"w1_{layer}": g_w1})
        return _matmul(da, comm.weight(f"w1_{layer}"), mode="nt", name=f"ffn{layer}_w1_dgrad", after=comm.started())

    dh3 = ffn_bwd(dy3, h3, f1, r1, 1)
    dx3, dg_pre_ffn1, dy2, dg_post_mix1 = _norm_bwd_pair(dx4, dh3, x3, vec(pre_ffn, 1), y2, vec(post_mix, 1),
                                                         "ffn1_sb_norm_bwd")
    g_sb_out = _matmul(o_sb, dy2, mode="tn", name="sb_out_wgrad", out_dtype=BF16)
    do_sb = _matmul(dy2, comm.weight("sb_out"), mode="nt", name="sb_out_dgrad", out_dtype=BF16)
    dq, dk, dv = _sb_bwd(qkv, do_sb, ltot, D, "sb_bwd")
    dqkv = _sb_pack(dq, dk, dv, "sb_pack")
    g_sb_in = _matmul(h2, dqkv, mode="tn", name="sb_in_wgrad", out_dtype=BF16)
    comm.reduce("sb", {"sb_out": g_sb_out, "sb_in": g_sb_in})
    dh2 = _matmul(dqkv, comm.weight("sb_in"), mode="nt", name="sb_in_dgrad", after=comm.started())
    dx2, dg_pre_mix1, dy1, dg_post_ffn0 = _norm_bwd_pair(dx3, dh2, x2, vec(pre_mix, 1), y1, vec(post_ffn, 0),
                                                         "sb_ffn0_norm_bwd")
    dh1 = ffn_bwd(dy1, h1, f0, r0, 0)
    dx1, dg_pre_ffn0, dy0, dg_post_mix0 = _norm_bwd_pair(dx2, dh1, x1, vec(pre_ffn, 0), y0, vec(post_mix, 0),
                                                         "ffn0_ab_norm_bwd")
    g_ab_out = _matmul(cat, dy0, mode="tn", name="ab_out_wgrad", out_dtype=BF16)
    comm.reduce("ab_out", {"ab_out": g_ab_out})
    dcat = _matmul(dy0, comm.weight("ab_out"), mode="nt", name="ab_out_dgrad", after=comm.started())
    duv, d_ln_g, d_ln_b, d_w_s, d_b_t = _sgu_bwd(z, dcat, ln_g, ln_b, w_s, b_t, "sgu_bwd")
    dz = _join_columns([duv, *_dilated_backward(z, dcat, o_dil, lse_dil, A, "dilated_bwd")], "join_dz")
    g_ab_in = _matmul(h0, dz, mode="tn", name="ab_in_wgrad", out_dtype=BF16)
    comm.reduce("ab_in", {"ab_in": g_ab_in})
    dh0 = _matmul(dz, comm.weight("ab_in"), mode="nt", name="ab_in_dgrad", after=comm.started())
    dx0, dg_pre_mix0 = _prenorm_bwd(dx1, dh0, x, vec(pre_mix, 0), "ab_prenorm_bwd")

    small = {
        "pre_mix": jnp.concatenate([dg_pre_mix0, dg_pre_mix1], axis=0),
        "post_mix": jnp.concatenate([dg_post_mix0, dg_post_mix1], axis=0),
        "pre_ffn": jnp.concatenate([dg_pre_ffn0, dg_pre_ffn1], axis=0),
        "post_ffn": jnp.concatenate([dg_post_ffn0, dg_post_ffn1], axis=0),
        "ln_g": d_ln_g, "ln_b": d_ln_b, "w_s": d_w_s, "b_s": d_b_t.T,
    }
    return loss, dx0, small


MESH_ID = pl.DeviceIdType.MESH
ANY = pl.BlockSpec(memory_space=pl.ANY)


def _coords():
    return lax.axis_index("x"), lax.axis_index("y"), lax.axis_index("c")


def _shard_of(ref, kind, p):
    if kind == "col":
        n = ref.shape[1] // N_DEV
        return ref.at[:, pl.ds(pl.multiple_of(p * n, 128), n)]
    r = ref.shape[0] // N_DEV
    return ref.at[pl.ds(pl.multiple_of(p * r, 16), r), :]


def _full_shape(shard, kind):
    if kind == "col":
        return (shard.shape[0], shard.shape[1] * N_DEV)
    return (shard.shape[0] * N_DEV, shard.shape[1])


def _place(shards, layer, kind, block, after, name):
    _, rows, cols = shards.shape
    tr = _tile(rows, 512)

    def body(b_ref, s_ref, after_ref, o_ref):
        o_ref[...] = s_ref[...].astype(BF16)

    if kind == "col":
        out = pl.BlockSpec((tr, cols), lambda i, b_ref: (i, b_ref[0]))
    else:
        out = pl.BlockSpec((tr, cols), lambda i, b_ref: (b_ref[0] * (rows // tr) + i, 0))
    return pl.pallas_call(
        body, name=name,
        grid_spec=pltpu.PrefetchScalarGridSpec(
            num_scalar_prefetch=1, grid=(rows // tr,),
            in_specs=[pl.BlockSpec((None, tr, cols), lambda i, b_ref: (layer, i, 0)), ANY], out_specs=out),
        out_shape=jax.ShapeDtypeStruct(_full_shape(shards[0], kind), BF16),
        compiler_params=_params("parallel"),
    )(block, shards, after)


HBM = pl.BlockSpec(memory_space=pltpu.HBM)
SEM = pl.BlockSpec(memory_space=pltpu.SEMAPHORE)
FLOWS = pltpu.SideEffectType.DATAFLOW_SIDE_EFFECTING


def _in_hbm(a):
    return pltpu.with_memory_space_constraint(a, pltpu.HBM)


def _hbm_like(bufs):
    return [pltpu.HBM(b.shape, b.dtype) for b in bufs]


def _copies_start(name, bufs, plan, n, after):
    nb = len(bufs)

    def body(*refs):
        send_sems, recv_sems, token = refs[nb + 1], refs[nb + 2], refs[-1]
        for cp in plan(refs[:nb], send_sems, recv_sems):
            cp.start()
        token[...] = jnp.zeros_like(token)

    out = pl.pallas_call(
        body, name=name, in_specs=[HBM] * nb + [ANY],
        out_specs=[SEM, SEM] + [HBM] * nb + [pl.BlockSpec(memory_space=pltpu.VMEM)],
        out_shape=[pltpu.SemaphoreType.DMA((n,)), pltpu.SemaphoreType.DMA((n,))] + _hbm_like(bufs)
        + [jax.ShapeDtypeStruct((8, 128), F32)],
        input_output_aliases={i: 2 + i for i in range(nb)},
        compiler_params=pltpu.CompilerParams(has_side_effects=FLOWS),
    )(*[_in_hbm(b) for b in bufs], after)
    return (out[0], out[1]), list(out[2:2 + nb]), out[-1]


def _copies_wait(name, bufs, sems, after, plan):
    nb = len(bufs)

    def body(*refs):
        for cp in plan(refs[:nb], refs[nb], refs[nb + 1]):
            cp.wait_send()
            cp.wait_recv()

    out = pl.pallas_call(
        body, name=name, in_specs=[HBM] * nb + [SEM, SEM, ANY], out_specs=[HBM] * nb,
        out_shape=_hbm_like(bufs), input_output_aliases={i: i for i in range(nb)},
        compiler_params=pltpu.CompilerParams(has_side_effects=FLOWS),
    )(*bufs, *sems, after)
    return list(out)


def _copies_wait_start(name, bufs, sems, after, plan, next_plan, n_next):
    nb = len(bufs)

    def body(*refs):
        ins = refs[:nb]
        for cp in plan(ins, refs[nb], refs[nb + 1]):
            cp.wait_send()
            cp.wait_recv()
        send_sems, recv_sems, token = refs[nb + 3], refs[nb + 4], refs[-1]
        for cp in next_plan(ins, send_sems, recv_sems):
            cp.start()
        token[...] = jnp.zeros_like(token)

    out = pl.pallas_call(
        body, name=name, in_specs=[HBM] * nb + [SEM, SEM, ANY],
        out_specs=[SEM, SEM] + [HBM] * nb + [pl.BlockSpec(memory_space=pltpu.VMEM)],
        out_shape=[pltpu.SemaphoreType.DMA((n_next,)), pltpu.SemaphoreType.DMA((n_next,))] + _hbm_like(bufs)
        + [jax.ShapeDtypeStruct((8, 128), F32)],
        input_output_aliases={i: 2 + i for i in range(nb)},
        compiler_params=pltpu.CompilerParams(has_side_effects=FLOWS),
    )(*bufs, *sems, after)
    return (out[0], out[1]), list(out[2:2 + nb]), out[-1]


def _gather_plans(kinds):
    nt = len(kinds)

    def slot(refs, t, px, py, pc):
        return _shard_of(refs[t], kinds[t], 4 * px + 2 * py + pc)

    def to_chips(refs, send_sems, recv_sems):
        x, y, c = _coords()
        peers = [(x, y, 1 - c), (1 - x, y, c), (x, 1 - y, c), (1 - x, 1 - y, c)]
        return [pltpu.make_async_remote_copy(
            src_ref=slot(refs, t, x, y, c), dst_ref=slot(refs, t, x, y, c), send_sem=send_sems.at[4 * t + k],
            recv_sem=recv_sems.at[4 * t + k], device_id=peer, device_id_type=MESH_ID)
            for t in range(nt) for k, peer in enumerate(peers)]

    def to_sibling(refs, send_sems, recv_sems):
        x, y, c = _coords()
        chips = [(1 - x, y), (x, 1 - y), (1 - x, 1 - y)]
        return [pltpu.make_async_remote_copy(
            src_ref=slot(refs, t, *chip, c), dst_ref=slot(refs, t, *chip, c), send_sem=send_sems.at[3 * t + j],
            recv_sem=recv_sems.at[3 * t + j], device_id=(x, y, 1 - c), device_id_type=MESH_ID)
            for t in range(nt) for j, chip in enumerate(chips)]

    return to_chips, to_sibling


def _shard_shape(full, kind):
    if kind == "col":
        return (full.shape[0], full.shape[1] // N_DEV)
    return (full.shape[0] // N_DEV, full.shape[1])


def _scatter_plan(kinds):
    nt = len(kinds)

    def plan(refs, send_sems, recv_sems):
        x, y, c = _coords()
        copies = []
        for t in range(nt):
            for k in range(1, N_DEV):
                px = 1 - x if (k >> 2) & 1 else x
                py = 1 - y if (k >> 1) & 1 else y
                pc = 1 - c if k & 1 else c
                copies.append(pltpu.make_async_remote_copy(
                    src_ref=_shard_of(refs[t], kinds[t], 4 * px + 2 * py + pc),
                    dst_ref=refs[nt + t].at[4 * x + 2 * y + c],
                    send_sem=send_sems.at[7 * t + k - 1], recv_sem=recv_sems.at[7 * t + k - 1],
                    device_id=(px, py, pc), device_id_type=MESH_ID))
        return copies
    return plan


def _partial_specs(full, kind, tr):
    rows, cols = _shard_shape(full, kind)
    steps = rows // tr
    if kind == "col":
        own = pl.BlockSpec((tr, cols), lambda i, w: (i, w[0]))
    else:
        own = pl.BlockSpec((tr, cols), lambda i, w: (w[0] * steps + i, 0))
    return [own] + [pl.BlockSpec((None, tr, cols), lambda i, w, k=k: (w[k], i, 0)) for k in range(1, N_DEV)]


def _small_plan(refs, send_sems, recv_sems):
    x, y, c = _coords()
    copies = []
    for k in range(1, N_DEV):
        peer = (1 - x if (k >> 2) & 1 else x, 1 - y if (k >> 1) & 1 else y, 1 - c if k & 1 else c)
        copies.append(pltpu.make_async_remote_copy(
            src_ref=refs[0], dst_ref=refs[1].at[4 * x + 2 * y + c], send_sem=send_sems.at[k - 1],
            recv_sem=recv_sems.at[k - 1], device_id=peer, device_id_type=MESH_ID))
    return copies


def _sum_small(vec, land):
    def body(v_ref, l_ref, o_ref):
        x, y, c = _coords()
        me = 4 * x + 2 * y + c
        total = jnp.where(me == 0, v_ref[...], l_ref[0])
        for p in range(1, N_DEV):
            total = total + jnp.where(me == p, v_ref[...], l_ref[p])
        o_ref[...] = total

    whole = pl.BlockSpec(memory_space=pltpu.VMEM)
    return pl.pallas_call(
        body, name="sum_small", in_specs=[whole, whole], out_specs=whole,
        out_shape=jax.ShapeDtypeStruct(vec.shape, F32),
        compiler_params=pltpu.CompilerParams(vmem_limit_bytes=VMEM_LIMIT),
    )(vec, land)


def _adamw_math(w, g, m, v):
    m = ADAM_B1 * m + (1.0 - ADAM_B1) * g
    v = ADAM_B2 * v + (1.0 - ADAM_B2) * (g * g)
    m_hat = m / (1.0 - ADAM_B1 ** ADAM_STEP)
    v_hat = v / (1.0 - ADAM_B2 ** ADAM_STEP)
    delta = -ADAM_LR * (m_hat / (jnp.sqrt(v_hat) + ADAM_EPS) + ADAM_WD * w)
    return delta, m, v


def _adamw(w, grads, kind, where, m, v, after, name):
    layers, rows, cols = w.shape
    tr = _tile(rows, 128)
    out = None
    for layer, (grad, land) in enumerate(grads):
        def body(w_ref, *refs):
            parts, (x_ref, m_ref, v_ref) = refs[:N_DEV], refs[N_DEV:N_DEV + 3]
            g_ref, d_ref, mo_ref, vo_ref = refs[-4:]
            g = parts[0][...].astype(F32)
            for p_ref in parts[1:]:
                g = g + p_ref[...].astype(F32)
            g_ref[...] = g
            d_ref[...], mo_ref[...], vo_ref[...] = _adamw_math(x_ref[...], g, m_ref[...], v_ref[...])

        blk = pl.BlockSpec((None, tr, cols), lambda i, w_, layer=layer: (layer, i, 0))
        earlier = [] if out is None else list(out)
        out = pl.pallas_call(
            body, name=f"{name}_{layer}",
            grid_spec=pltpu.PrefetchScalarGridSpec(
                num_scalar_prefetch=1, grid=(rows // tr,),
                in_specs=_partial_specs(grad, kind, tr) + [blk] * 3 + [ANY] * (len(earlier) + len(after)),
                out_specs=[blk] * 4),
            out_shape=[jax.ShapeDtypeStruct((layers, rows, cols), F32)] * 4,
            input_output_aliases={N_DEV + 4 + k: k for k in range(len(earlier))},
            compiler_params=_params("parallel"),
        )(where, grad, *[land] * (N_DEV - 1), w, m, v, *earlier, *after)
    return out


def _adamw_small(w, g, m, v):
    def body(w_ref, g_ref, m_ref, v_ref, d_ref, mo_ref, vo_ref):
        d_ref[...], mo_ref[...], vo_ref[...] = _adamw_math(w_ref[...], g_ref[...], m_ref[...], v_ref[...])

    whole = pl.BlockSpec(memory_space=pltpu.VMEM)
    return pl.pallas_call(
        body, name="adamw_small", in_specs=[whole] * 4, out_specs=[whole] * 3,
        out_shape=[jax.ShapeDtypeStruct(w.shape, F32)] * 3,
        compiler_params=pltpu.CompilerParams(vmem_limit_bytes=VMEM_LIMIT),
    )(w, g, m, v)


def _pack(arrays):
    rows = []
    for a in arrays:
        flat = a.reshape(-1)
        pad = (-flat.shape[0]) % 1024
        rows.append(jnp.pad(flat, (0, pad)).reshape(-1, 128))
    return jnp.concatenate(rows, axis=0)


def _unpack(packed, like):
    out, r = [], 0
    for a in like:
        n = math.prod(a.shape)
        nr = (n + 1023) // 1024 * 8
        out.append(packed[r:r + nr].reshape(-1)[:n].reshape(a.shape))
        r += nr
    return out


KIND = {"ab_in": "col", "ab_out": "row", "sb_in": "col", "sb_out": "row",
        "w1_0": "col", "w1_1": "col", "w2_0": "row", "w2_1": "row"}
GATHERS = {"ab_in": ("ab_in",), "ab_out": ("ab_out",), "w1_0": ("w1_0",), "w2_0": ("w2_0",),
           "sb": ("sb_in", "sb_out"), "w1_1": ("w1_1",), "w2_1": ("w2_1",)}


class _Exchange:
    def __init__(self, shards):
        x, y, c = _coords()
        me = (4 * x + 2 * y + c).astype(jnp.int32)
        self.where = jnp.stack([jnp.bitwise_xor(me, k) for k in range(N_DEV)])
        self.full = {}
        self.tokens = []
        self.gathers = {}
        self.scatters = {}
        self.settled = {}
        block = me.reshape(1)
        after = block
        for key, group in GATHERS.items():
            for n in group:
                self.full[n] = _place(*shards[n], KIND[n], block, after, f"place_{n}")
            to_chips, to_sibling = _gather_plans([KIND[n] for n in group])
            bufs = [self.full[n] for n in group]
            sems, bufs, after = _copies_start(f"gather_start_{key}", bufs, to_chips, 4 * len(group), after)
            self.tokens.append(after)
            self.gathers[key] = (group, sems, bufs, to_chips, to_sibling)

    def tie(self, small):
        for token in self.tokens:
            small = small + token[0:1, 0:1]
        self.tokens = []
        return small

    def started(self):
        return tuple(self.tokens)

    def weight(self, name):
        return self.full[name]

    def arrive(self, key, after):
        group, sems, bufs, to_chips, to_sibling = self.gathers[key]
        sems, bufs, token = _copies_wait_start(f"gather_pass_{key}", bufs, sems, after, to_chips, to_sibling,
                                               3 * len(group))
        self.tokens.append(token)
        self.gathers[key] = (group, sems, bufs, token, to_sibling)

    def land(self, key, after=None):
        group, sems, bufs, token, to_sibling = self.gathers.pop(key)
        after = token if after is None else after
        self.full.update(zip(group, _copies_wait(f"gather_done_{key}", bufs, sems, after, to_sibling)))

    def reduce(self, key, grads):
        names = list(grads)
        kinds = [KIND[n] for n in names]
        full = [grads[n] for n in names]
        lands = [lax.empty((N_DEV,) + _shard_shape(g, k), BF16) for g, k in zip(full, kinds)]
        plan = _scatter_plan(kinds)
        sems, bufs, token = _copies_start(f"scatter_start_{key}", full + lands, plan, (N_DEV - 1) * len(names),
                                          self.where)
        self.tokens.append(token)
        self.scatters[key] = (names, sems, bufs, plan)

    def settle(self, keys, after):
        for key in keys:
            names, sems, bufs, plan = self.scatters.pop(key)
            bufs = _copies_wait(f"scatter_done_{key}", bufs, sems, after, plan)
            self.settled.update({n: t for n, *t in zip(names, bufs[:len(names)], bufs[len(names):])})
        return self.settled


SMALL = ("norm_pre_mix", "norm_post_mix", "norm_pre_ffn", "norm_post_ffn", "sgu_ln_g", "sgu_ln_b", "sgu_w", "sgu_b")
ORDER = ("norm_pre_mix", "norm_post_mix", "norm_pre_ffn", "norm_post_ffn", "ab_w_in", "sgu_ln_g", "sgu_ln_b", "sgu_w",
         "sgu_b", "ab_w_out", "sb_w_in", "sb_w_out", "ffn_w1", "ffn_w2")


def kernel(x, norm_pre_mix, norm_post_mix, norm_pre_ffn, norm_post_ffn, ab_w_in, sgu_ln_g, sgu_ln_b, sgu_w, sgu_b, ab_w_out, sb_w_in, sb_w_out, ffn_w1, ffn_w2, loss_target, m_norm_pre_mix, m_norm_post_mix, m_norm_pre_ffn, m_norm_post_ffn, m_ab_w_in, m_sgu_ln_g, m_sgu_ln_b, m_sgu_w, m_sgu_b, m_ab_w_out, m_sb_w_in, m_sb_w_out, m_ffn_w1, m_ffn_w2, v_norm_pre_mix, v_norm_post_mix, v_norm_pre_ffn, v_norm_post_ffn, v_ab_w_in, v_sgu_ln_g, v_sgu_ln_b, v_sgu_w, v_sgu_b, v_ab_w_out, v_sb_w_in, v_sb_w_out, v_ffn_w1, v_ffn_w2):
    W = dict(norm_pre_mix=norm_pre_mix, norm_post_mix=norm_post_mix, norm_pre_ffn=norm_pre_ffn,
             norm_post_ffn=norm_post_ffn, ab_w_in=ab_w_in, sgu_ln_g=sgu_ln_g, sgu_ln_b=sgu_ln_b, sgu_w=sgu_w,
             sgu_b=sgu_b, ab_w_out=ab_w_out, sb_w_in=sb_w_in, sb_w_out=sb_w_out, ffn_w1=ffn_w1, ffn_w2=ffn_w2)
    M = dict(norm_pre_mix=m_norm_pre_mix, norm_post_mix=m_norm_post_mix, norm_pre_ffn=m_norm_pre_ffn,
             norm_post_ffn=m_norm_post_ffn, ab_w_in=m_ab_w_in, sgu_ln_g=m_sgu_ln_g, sgu_ln_b=m_sgu_ln_b,
             sgu_w=m_sgu_w, sgu_b=m_sgu_b, ab_w_out=m_ab_w_out, sb_w_in=m_sb_w_in, sb_w_out=m_sb_w_out,
             ffn_w1=m_ffn_w1, ffn_w2=m_ffn_w2)
    V = dict(norm_pre_mix=v_norm_pre_mix, norm_post_mix=v_norm_post_mix, norm_pre_ffn=v_norm_pre_ffn,
             norm_post_ffn=v_norm_post_ffn, ab_w_in=v_ab_w_in, sgu_ln_g=v_sgu_ln_g, sgu_ln_b=v_sgu_ln_b,
             sgu_w=v_sgu_w, sgu_b=v_sgu_b, ab_w_out=v_ab_w_out, sb_w_in=v_sb_w_in, sb_w_out=v_sb_w_out,
             ffn_w1=v_ffn_w1, ffn_w2=v_ffn_w2)

    shards = {"ab_in": (ab_w_in, 0), "ab_out": (ab_w_out, 0), "w1_0": (ffn_w1, 0), "w2_0": (ffn_w2, 0),
              "sb_in": (sb_w_in, 0), "sb_out": (sb_w_out, 0), "w1_1": (ffn_w1, 1), "w2_1": (ffn_w2, 1)}
    comm = _Exchange(shards)
    norms = (norm_pre_mix, norm_post_mix, norm_pre_ffn, norm_post_ffn)
    sgu = (sgu_ln_g, sgu_ln_b, sgu_w[0], sgu_b[0])
    loss, dx, small = _local_step(x[0], loss_target[0], norms, sgu, comm)

    out = {}

    def update(name, layers, landed, after=()):
        out[name] = _adamw(W[name], [landed[n] for n in layers], KIND[layers[0]], comm.where, M[name], V[name],
                           after, f"adamw_{name}")

    landed = comm.settle(("ffn1", "sb", "ffn0"), after=dx)
    small_g = [small["pre_mix"], small["post_mix"], small["pre_ffn"], small["post_ffn"], small["ln_g"],
               small["ln_b"], small["w_s"][None], small["b_s"][None]]
    mine = _pack(small_g + [loss.reshape(1)])
    sems, bufs, token = _copies_start("small_start", [mine, jnp.zeros((N_DEV,) + mine.shape, F32)], _small_plan,
                                      N_DEV - 1, dx)
    for name, layers in (("sb_w_in", ["sb_in"]), ("sb_w_out", ["sb_out"]), ("ffn_w1", ["w1_0", "w1_1"]),
                         ("ffn_w2", ["w2_0", "w2_1"])):
        update(name, layers, landed, after=(token,))
    summed = _sum_small(*_copies_wait("small_done", bufs, sems, out["ffn_w2"][3], _small_plan))
    g_small, loss = summed[:-8], summed[-8, 0]
    landed = comm.settle(("ab_out", "ab_in"), after=g_small)
    update("ab_w_out", ["ab_out"], landed)
    update("ab_w_in", ["ab_in"], landed)
    res = _adamw_small(_pack([W[n] for n in SMALL]), g_small, _pack([M[n] for n in SMALL]),
                       _pack([V[n] for n in SMALL]))
    like = [W[n] for n in SMALL]
    for n, *vals in zip(SMALL, *[_unpack(r, like) for r in [g_small] + list(res)]):
        out[n] = vals

    return (loss, dx[None], *[out[n][0] for n in ORDER], *[out[n][1] for n in ORDER],
            *[out[n][2] for n in ORDER], *[out[n][3] for n in ORDER])
```

```python
import math

import jax
import jax.numpy as jnp
from jax import lax
from jax.experimental import pallas as pl
from jax.experimental.pallas import tpu as pltpu

F32 = jnp.float32
BF16 = jnp.bfloat16

HEAD_DIM = 128
CHUNK = 128
ATT_BLOCK = 128
DILATED_PAIRS = ((128, 1), (512, 4), (2048, 16))
RMS_EPS = 1e-6
LN_EPS = 1e-5
ADAM_LR = 0.001
ADAM_B1 = 0.9
ADAM_B2 = 0.999
ADAM_EPS = 1e-08
ADAM_WD = 0.01
ADAM_STEP = 10
N_DEV = 8
MASKED = -1e30

V7X_VMEM_BYTES = 64 * 1024 * 1024
VMEM_LIMIT = V7X_VMEM_BYTES - 8 * 1024 * 1024

NN = (((1,), (0,)), ((), ()))
NT = (((1,), (1,)), ((), ()))
TN = (((0,), (0,)), ((), ()))


def _params(*sem):
    return pltpu.CompilerParams(dimension_semantics=sem, vmem_limit_bytes=VMEM_LIMIT)


def _dot(a, b, dims=NN):
    return lax.dot_general(a, b, dims, preferred_element_type=F32)


def _tile(n, preferred):
    if n <= preferred:
        return n
    t = preferred - preferred % 128
    while n % t:
        t -= 128
    assert t > 0, (n, preferred)
    return t


def _matmul(a, b, *, mode, name, out_dtype=F32, tm=1024, tn=1024, tk=2048, epi=None, extras=(), after=()):
    if mode == "nn":
        (M, K), N = a.shape, b.shape[1]
    elif mode == "nt":
        (M, K), N = a.shape, b.shape[0]
    else:
        (K, M), N = a.shape, b.shape[1]
    tm, tn, tk = _tile(M, tm), _tile(N, tn), _tile(K, tk)
    nk = K // tk
    if mode == "tn":
        a_spec = pl.BlockSpec((tk, tm), lambda i, j, k: (k, i))
    else:
        a_spec = pl.BlockSpec((tm, tk), lambda i, j, k: (i, k))
    if mode == "nt":
        b_spec = pl.BlockSpec((tn, tk), lambda i, j, k: (j, k))
    else:
        b_spec = pl.BlockSpec((tk, tn), lambda i, j, k: (k, j))
    o_spec = pl.BlockSpec((tm, tn), lambda i, j, k: (i, j))
    dims = {"nn": NN, "nt": NT, "tn": TN}[mode]
    n_extra = len(extras)
    n_in = n_extra + len(after)
    several = isinstance(out_dtype, tuple)
    n_out = len(out_dtype) if several else 1

    def finish(acc, rest):
        outs = acc if epi is None else epi(acc, pl.program_id(1), *[r[...] for r in rest[:n_extra]])
        for o_ref, o in zip(rest[n_in:n_in + n_out], outs if several else (outs,)):
            o_ref[...] = o.astype(o_ref.dtype)

    if nk == 1:
        def body(a_ref, b_ref, *rest):
            finish(_dot(a_ref[...], b_ref[...], dims), rest)
        scratch = []
    else:
        def body(a_ref, b_ref, *rest):
            acc_ref = rest[n_in + n_out]
            k = pl.program_id(2)

            @pl.when(k == 0)
            def _():
                acc_ref[...] = jnp.zeros_like(acc_ref)

            acc_ref[...] += _dot(a_ref[...], b_ref[...], dims)

            @pl.when(k == nk - 1)
            def _():
                finish(acc_ref[...], rest)
        scratch = [pltpu.VMEM((tm, tn), F32)]

    shapes = [jax.ShapeDtypeStruct((M, N), d) for d in (out_dtype if several else (out_dtype,))]
    return pl.pallas_call(
        body,
        name=name,
        grid=(M // tm, N // tn, nk),
        in_specs=[a_spec, b_spec] + [o_spec] * n_extra + [ANY] * len(after),
        out_specs=[o_spec] * n_out if several else o_spec,
        out_shape=shapes if several else shapes[0],
        scratch_shapes=scratch,
        compiler_params=_params("parallel", "parallel", "arbitrary"),
    )(a, b, *extras, *after)


ROWS = 512


def _rms(x):
    return lax.rsqrt(jnp.mean(x * x, axis=-1, keepdims=True) + RMS_EPS)


def _prenorm(x, g, name):
    T, D = x.shape

    def body(x_ref, g_ref, h_ref):
        xv = x_ref[...]
        h_ref[...] = (xv * _rms(xv) * g_ref[...]).astype(BF16)

    row = pl.BlockSpec((ROWS, D), lambda i: (i, 0))
    vec = pl.BlockSpec((1, D), lambda i: (0, 0))
    return pl.pallas_call(
        body, name=name, grid=(T // ROWS,), in_specs=[row, vec], out_specs=row,
        out_shape=jax.ShapeDtypeStruct((T, D), BF16), compiler_params=_params("parallel"),
    )(x, g)


def _postnorm_prenorm(x, y, g_post, g_pre, name):
    T, D = x.shape

    def body(x_ref, y_ref, gp_ref, gn_ref, xo_ref, h_ref):
        yv = y_ref[...]
        xn = x_ref[...] + yv * _rms(yv) * gp_ref[...]
        xo_ref[...] = xn
        h_ref[...] = (xn * _rms(xn) * gn_ref[...]).astype(BF16)

    row = pl.BlockSpec((ROWS, D), lambda i: (i, 0))
    vec = pl.BlockSpec((1, D), lambda i: (0, 0))
    return pl.pallas_call(
        body, name=name, grid=(T // ROWS,), in_specs=[row, row, vec, vec], out_specs=[row, row],
        out_shape=[jax.ShapeDtypeStruct((T, D), F32), jax.ShapeDtypeStruct((T, D), BF16)],
        compiler_params=_params("parallel"),
    )(x, y, g_post, g_pre)


def _postnorm_grads(dn, yh, r, g):
    gd = dn * g
    return r * (gd - yh * jnp.mean(yh * gd, axis=-1, keepdims=True)), dn * yh


def _postnorm_loss(x, y, g_post, target, name):
    T, D = x.shape

    def body(x_ref, y_ref, gp_ref, t_ref, loss_ref, dx_ref, dy_ref, dg_ref):
        @pl.when(pl.program_id(0) == 0)
        def _():
            loss_ref[...] = jnp.zeros_like(loss_ref)
            dg_ref[...] = jnp.zeros_like(dg_ref)

        yv = y_ref[...]
        r = _rms(yv)
        yh = yv * r
        err = x_ref[...] + yh * gp_ref[...] - t_ref[...]
        dx = err * (1.0 / D)
        dx_ref[...] = dx
        loss_ref[...] += 0.5 * jnp.sum(jnp.sum(err * err, axis=-1, keepdims=True) * (1.0 / D))
        dy, dg = _postnorm_grads(dx, yh, r, gp_ref[...])
        dy_ref[...] = dy.astype(BF16)
        dg_ref[...] += jnp.sum(dg, axis=0, keepdims=True)

    row = pl.BlockSpec((ROWS, D), lambda i: (i, 0))
    vec = pl.BlockSpec((1, D), lambda i: (0, 0))
    acc = pl.BlockSpec((8, 128), lambda i: (0, 0))
    return pl.pallas_call(
        body, name=name, grid=(T // ROWS,), in_specs=[row, row, vec, row], out_specs=[acc, row, row, vec],
        out_shape=[jax.ShapeDtypeStruct((8, 128), F32), jax.ShapeDtypeStruct((T, D), F32),
                   jax.ShapeDtypeStruct((T, D), BF16), jax.ShapeDtypeStruct((1, D), F32)],
        compiler_params=_params("arbitrary"),
    )(x, y, g_post, target)


def _norm_bwd_pair(dx_out, dh, x, g_pre, y_prev, g_post_prev, name):
    T, D = x.shape

    def body(dxo_ref, dh_ref, x_ref, g_ref, y_ref, gp_ref, dx_ref, dg_ref, dy_ref, dgp_ref):
        @pl.when(pl.program_id(0) == 0)
        def _():
            dg_ref[...] = jnp.zeros_like(dg_ref)
            dgp_ref[...] = jnp.zeros_like(dgp_ref)

        xv, dhv = x_ref[...], dh_ref[...]
        r = _rms(xv)
        xh = xv * r
        gd = dhv * g_ref[...]
        dx = dxo_ref[...] + r * (gd - xh * jnp.mean(xh * gd, axis=-1, keepdims=True))
        dx_ref[...] = dx
        dg_ref[...] += jnp.sum(dhv * xh, axis=0, keepdims=True)
        yv = y_ref[...]
        ry = _rms(yv)
        dy, dgp = _postnorm_grads(dx, yv * ry, ry, gp_ref[...])
        dy_ref[...] = dy.astype(BF16)
        dgp_ref[...] += jnp.sum(dgp, axis=0, keepdims=True)

    row = pl.BlockSpec((ROWS, D), lambda i: (i, 0))
    vec = pl.BlockSpec((1, D), lambda i: (0, 0))
    return pl.pallas_call(
        body, name=name, grid=(T // ROWS,), in_specs=[row, row, row, vec, row, vec],
        out_specs=[row, vec, row, vec],
        out_shape=[jax.ShapeDtypeStruct((T, D), F32), jax.ShapeDtypeStruct((1, D), F32),
                   jax.ShapeDtypeStruct((T, D), BF16), jax.ShapeDtypeStruct((1, D), F32)],
        compiler_params=_params("arbitrary"),
    )(dx_out, dh, x, g_pre, y_prev, g_post_prev)


def _prenorm_bwd(dx_out, dh, x, g_pre, name):
    T, D = x.shape

    def body(dxo_ref, dh_ref, x_ref, g_ref, dx_ref, dg_ref):
        @pl.when(pl.program_id(0) == 0)
        def _():
            dg_ref[...] = jnp.zeros_like(dg_ref)

        xv, dhv = x_ref[...], dh_ref[...]
        r = _rms(xv)
        xh = xv * r
        gd = dhv * g_ref[...]
        dx_ref[...] = dxo_ref[...] + r * (gd - xh * jnp.mean(xh * gd, axis=-1, keepdims=True))
        dg_ref[...] += jnp.sum(dhv * xh, axis=0, keepdims=True)

    row = pl.BlockSpec((ROWS, D), lambda i: (i, 0))
    vec = pl.BlockSpec((1, D), lambda i: (0, 0))
    return pl.pallas_call(
        body, name=name, grid=(T // ROWS,), in_specs=[row, row, row, vec], out_specs=[row, vec],
        out_shape=[jax.ShapeDtypeStruct((T, D), F32), jax.ShapeDtypeStruct((1, D), F32)],
        compiler_params=_params("arbitrary"),
    )(dx_out, dh, x, g_pre)


_INV_SQRT2 = 1.0 / math.sqrt(2.0)
_INV_SQRT2PI = 1.0 / math.sqrt(2.0 * math.pi)


def _gelu(x):
    return 0.5 * x * (1.0 + lax.erf(x * _INV_SQRT2))


def _gelu_grad(x):
    return 0.5 * (1.0 + lax.erf(x * _INV_SQRT2)) + x * jnp.exp(-0.5 * x * x) * _INV_SQRT2PI


def _layernorm_stats(x):
    mu = jnp.mean(x, axis=-1, keepdims=True)
    xc = x - mu
    rstd = lax.rsqrt(jnp.mean(xc * xc, axis=-1, keepdims=True) + LN_EPS)
    return xc * rstd, rstd


def _tril_mask():
    i = lax.broadcasted_iota(jnp.int32, (CHUNK, CHUNK), 0)
    j = lax.broadcasted_iota(jnp.int32, (CHUNK, CHUNK), 1)
    return j <= i


SGU_ROWS = 512


def _sgu_fwd(z, ln_g, ln_b, w_s, b_t, name):
    T = z.shape[0]
    A = ln_g.shape[1]
    G = A // 128
    rows = min(SGU_ROWS, T)

    def body(u_ref, v_ref, g_ref, b_ref, w_ref, bt_ref, o_ref):
        mask = _tril_mask()
        for c in range(rows // CHUNK):
            rs = pl.ds(c * CHUNK, CHUNK)
            xh, _ = _layernorm_stats(_gelu(v_ref[rs, :]))
            vn = (xh * g_ref[...] + b_ref[...]).astype(BF16)
            for g in range(G):
                cs = pl.ds(g * 128, 128)
                w = jnp.where(mask, w_ref[g], 0.0).astype(BF16)
                mixed = _dot(w, vn[:, g * 128:(g + 1) * 128]) + bt_ref[:, g:g + 1]
                o_ref[rs, cs] = (_gelu(u_ref[rs, cs]) * mixed).astype(BF16)

    return pl.pallas_call(
        body, name=name, grid=(T // rows,),
        in_specs=[
            pl.BlockSpec((rows, A), lambda i: (i, 0)),
            pl.BlockSpec((rows, A), lambda i: (i, 1)),
            pl.BlockSpec((1, A), lambda i: (0, 0)),
            pl.BlockSpec((1, A), lambda i: (0, 0)),
            pl.BlockSpec((G, CHUNK, CHUNK), lambda i: (0, 0, 0)),
            pl.BlockSpec((CHUNK, G), lambda i: (0, 0)),
        ],
        out_specs=pl.BlockSpec((rows, A), lambda i: (i, 0)),
        out_shape=jax.ShapeDtypeStruct((T, 2 * A), BF16),
        compiler_params=_params("parallel"),
    )(z, z, ln_g, ln_b, w_s, b_t)


def _sgu_bwd(z, dcat, ln_g, ln_b, w_s, b_t, name):
    T = z.shape[0]
    A = ln_g.shape[1]
    G = A // 128
    rows = min(SGU_ROWS, T)

    def body(u_ref, v_ref, da_ref, g_ref, b_ref, w_ref, bt_ref, dz_ref, dg_ref, db_ref, dw_ref, dbt_ref, dvn_ref):
        @pl.when(pl.program_id(0) == 0)
        def _():
            dg_ref[...] = jnp.zeros_like(dg_ref)
            db_ref[...] = jnp.zeros_like(db_ref)
            dw_ref[...] = jnp.zeros_like(dw_ref)
            dbt_ref[...] = jnp.zeros_like(dbt_ref)

        mask = _tril_mask()
        for c in range(rows // CHUNK):
            rs = pl.ds(c * CHUNK, CHUNK)
            vv = v_ref[rs, :]
            gv = _gelu(vv)
            xh, rstd = _layernorm_stats(gv)
            vn = (xh * g_ref[...] + b_ref[...]).astype(BF16)
            for g in range(G):
                cs = pl.ds(g * 128, 128)
                w = jnp.where(mask, w_ref[g], 0.0).astype(BF16)
                vg = vn[:, g * 128:(g + 1) * 128]
                mixed = _dot(w, vg) + bt_ref[:, g:g + 1]
                uu = u_ref[rs, cs]
                da = da_ref[rs, cs]
                dz_ref[rs, cs] = (da * mixed * _gelu_grad(uu)).astype(BF16)
                dm = da * _gelu(uu)
                dmb = dm.astype(BF16)
                dbt_ref[:, g:g + 1] += jnp.sum(dm, axis=1, keepdims=True)
                dw_ref[g] += jnp.where(mask, _dot(dmb, vg, NT), 0.0)
                dvn_ref[:, cs] = _dot(w, dmb, TN)
            dvn = dvn_ref[...]
            dg_ref[...] += jnp.sum(dvn * xh, axis=0, keepdims=True)
            db_ref[...] += jnp.sum(dvn, axis=0, keepdims=True)
            dxh = dvn * g_ref[...]
            dgv = rstd * (dxh - jnp.mean(dxh, axis=-1, keepdims=True)
                          - xh * jnp.mean(dxh * xh, axis=-1, keepdims=True))
            dz_ref[rs, pl.ds(A, A)] = (dgv * _gelu_grad(vv)).astype(BF16)

    vec = pl.BlockSpec((1, A), lambda i: (0, 0))
    wsp = pl.BlockSpec((G, CHUNK, CHUNK), lambda i: (0, 0, 0))
    bsp = pl.BlockSpec((CHUNK, G), lambda i: (0, 0))
    return pl.pallas_call(
        body, name=name, grid=(T // rows,),
        in_specs=[
            pl.BlockSpec((rows, A), lambda i: (i, 0)),
            pl.BlockSpec((rows, A), lambda i: (i, 1)),
            pl.BlockSpec((rows, A), lambda i: (i, 0)),
            vec, vec, wsp, bsp,
        ],
        out_specs=[pl.BlockSpec((rows, 2 * A), lambda i: (i, 0)), vec, vec, wsp, bsp],
        out_shape=[
            jax.ShapeDtypeStruct((T, 2 * A), BF16),
            jax.ShapeDtypeStruct((1, A), F32),
            jax.ShapeDtypeStruct((1, A), F32),
            jax.ShapeDtypeStruct((G, CHUNK, CHUNK), F32),
            jax.ShapeDtypeStruct((CHUNK, G), F32),
        ],
        scratch_shapes=[pltpu.VMEM((CHUNK, A), F32)],
        compiler_params=_params("arbitrary"),
    )(z, z, dcat, ln_g, ln_b, w_s, b_t)


def _alibi_row(B, d):
    H = B // HEAD_DIM
    slopes = [d * 2.0 ** (-8.0 * (h + 1.0) / H) for h in range(H)]
    return jnp.repeat(jnp.asarray(slopes, F32), HEAD_DIM)[None, :]


def _dil_scores(q, k, slope_d, valid, dist):
    s = _dot(q, k, NT) - slope_d * dist
    return jnp.where(valid, s, MASKED)


def _dil_block(n, r, d):
    if d == 1:
        return pl.ds(pl.multiple_of(n * ATT_BLOCK, ATT_BLOCK), ATT_BLOCK)
    return pl.ds(n * (d * ATT_BLOCK) + r, ATT_BLOCK, stride=d)


def _dilated_forward(z, cat, B, name):
    T = z.shape[0]
    H = B // HEAD_DIM
    A = cat.shape[1] - B
    scale = HEAD_DIM ** -0.5
    blk = ATT_BLOCK
    chunk = _tile(T, 512)

    def body(q_ref, k_ref, v_ref, sl_ref, cat_in, cat_ref, of_ref, lt_ref, *branch):
        o_refs, l_refs = branch[:3], branch[3:]
        slope = sl_ref[:, :1]
        qi = lax.broadcasted_iota(jnp.int32, (blk, 2 * blk), 0)
        kj = lax.broadcasted_iota(jnp.int32, (blk, 2 * blk), 1)
        dist = qi + blk - kj
        band = (dist >= 0) & (dist <= blk)
        distf = dist.astype(F32)

        for b, (_, d) in enumerate(DILATED_PAIRS):
            def one(n, r, b=b, d=d):
                rows, prev = _dil_block(n, r, d), _dil_block(jnp.maximum(n - 1, 0), r, d)
                q = (q_ref[rows, :] * scale).astype(BF16)
                k = jnp.concatenate([k_ref[prev, :], k_ref[rows, :]], axis=0).astype(BF16)
                v = jnp.concatenate([v_ref[prev, :], v_ref[rows, :]], axis=0).astype(BF16)
                s = _dil_scores(q, k, slope * float(d), band & ((kj >= blk) | (n > 0)), distf)
                m = jnp.max(s, axis=-1, keepdims=True)
                p = jnp.exp(s - m)
                den = jnp.sum(p, axis=-1, keepdims=True)
                o_refs[b][rows, :] = _dot(p.astype(BF16), v) / den
                l_refs[b][rows, :] = jnp.broadcast_to(m + jnp.log(den), (blk, HEAD_DIM))

            per = max(1, 4 // d)

            def step(i, _, d=d, per=per, one=one):
                for u in range(per):
                    for r in range(d):
                        one(i * per + u, r)
                return 0

            lax.fori_loop(0, T // (d * blk * per), step, 0)

        def merge(i, _):
            rs = pl.ds(pl.multiple_of(i * chunk, chunk), chunk)
            a, b, c = l_refs[0][rs, :], l_refs[1][rs, :], l_refs[2][rs, :]
            m = jnp.maximum(jnp.maximum(a, b), c)
            ea, eb, ec = jnp.exp(a - m), jnp.exp(b - m), jnp.exp(c - m)
            tot = ea + eb + ec
            o = (ea * o_refs[0][rs, :] + eb * o_refs[1][rs, :] + ec * o_refs[2][rs, :]) / tot
            of_ref[rs, :] = o
            cat_ref[rs, :] = o.astype(BF16)
            lt_ref[rs, :] = m + jnp.log(tot)
            return 0

        lax.fori_loop(0, T // chunk, merge, 0)

    def col(unit):
        return lambda h: (0, unit * H + h)

    seq = (T, HEAD_DIM)
    out = pl.BlockSpec(seq, lambda h: (0, h))
    return pl.pallas_call(
        body, name=name, grid=(H,),
        in_specs=[pl.BlockSpec(seq, col(2)), pl.BlockSpec(seq, col(3)), pl.BlockSpec(seq, col(4)),
                  pl.BlockSpec((1, HEAD_DIM), lambda h: (0, h)), ANY],
        out_specs=[pl.BlockSpec(seq, lambda h: (0, A // HEAD_DIM + h)), out, out],
        out_shape=[jax.ShapeDtypeStruct(cat.shape, BF16), jax.ShapeDtypeStruct((T, B), F32),
                   jax.ShapeDtypeStruct((T, B), F32)],
        input_output_aliases={4: 0},
        scratch_shapes=[pltpu.VMEM(seq, F32)] * 6,
        compiler_params=_params("parallel"),
    )(z, z, z, _alibi_row(B, 1), cat)


def _dilated_backward(z, dcat, o, lse, B, name):
    T = z.shape[0]
    H = B // HEAD_DIM
    scale = HEAD_DIM ** -0.5
    blk = ATT_BLOCK
    chunk = _tile(T, 512)

    def body(q_ref, k_ref, v_ref, do_ref, o_ref, l_ref, sl_ref, dq_ref, dk_ref, dv_ref, aq_ref, ak_ref, av_ref):
        slope = sl_ref[:, :1]
        qi = lax.broadcasted_iota(jnp.int32, (blk, 2 * blk), 0)
        kj = lax.broadcasted_iota(jnp.int32, (blk, 2 * blk), 1)
        dist = qi + blk - kj
        band = (dist >= 0) & (dist <= blk)
        distf = dist.astype(F32)
        for acc in (aq_ref, ak_ref, av_ref):
            acc[...] = jnp.zeros_like(acc)

        for _, d in DILATED_PAIRS:
            def one(n, r, d=d):
                rows, prev = _dil_block(n, r, d), _dil_block(jnp.maximum(n - 1, 0), r, d)
                q = (q_ref[rows, :] * scale).astype(BF16)
                k = jnp.concatenate([k_ref[prev, :], k_ref[rows, :]], axis=0).astype(BF16)
                v = jnp.concatenate([v_ref[prev, :], v_ref[rows, :]], axis=0).astype(BF16)
                do = do_ref[rows, :]
                delta = jnp.sum(do * o_ref[rows, :], axis=-1, keepdims=True)
                do = do.astype(BF16)
                s = _dil_scores(q, k, slope * float(d), band & ((kj >= blk) | (n > 0)), distf)
                p = jnp.exp(s - l_ref[rows, :][:, :1])
                ds = (p * (_dot(do, v, NT) - delta)).astype(BF16)
                aq_ref[rows, :] += _dot(ds, k)
                dk = _dot(ds, q, TN)
                dv = _dot(p.astype(BF16), do, TN)
                ak_ref[prev, :] += dk[:blk]
                av_ref[prev, :] += dv[:blk]
                ak_ref[rows, :] += dk[blk:]
                av_ref[rows, :] += dv[blk:]

            per = max(1, 4 // d)

            def step(i, _, d=d, per=per, one=one):
                for u in range(per):
                    for r in range(d):
                        one(i * per + u, r)
                return 0

            lax.fori_loop(0, T // (d * blk * per), step, 0)

        def emit(i, _):
            rs = pl.ds(pl.multiple_of(i * chunk, chunk), chunk)
            dq_ref[rs, :] = (aq_ref[rs, :] * scale).astype(BF16)
            dk_ref[rs, :] = ak_ref[rs, :].astype(BF16)
            dv_ref[rs, :] = av_ref[rs, :].astype(BF16)
            return 0

        lax.fori_loop(0, T // chunk, emit, 0)

    def col(unit):
        return lambda h: (0, unit * H + h)

    seq = (T, HEAD_DIM)
    own = pl.BlockSpec(seq, lambda h: (0, h))
    return pl.pallas_call(
        body, name=name, grid=(H,),
        in_specs=[pl.BlockSpec(seq, col(2)), pl.BlockSpec(seq, col(3)), pl.BlockSpec(seq, col(4)),
                  pl.BlockSpec(seq, col(1)), own, own, pl.BlockSpec((1, HEAD_DIM), lambda h: (0, h))],
        out_specs=[own] * 3,
        out_shape=[jax.ShapeDtypeStruct((T, B), BF16)] * 3,
        scratch_shapes=[pltpu.VMEM(seq, F32)] * 3,
        compiler_params=_params("parallel"),
    )(z, z, z, dcat, o, lse, _alibi_row(B, 1))


def _join_columns(parts, name):
    T = parts[0].shape[0]
    widths = [p.shape[1] for p in parts]

    def body(*refs):
        o_ref, at = refs[-1], 0
        for ref, w in zip(refs[:-1], widths):
            o_ref[:, pl.ds(at, w)] = ref[...]
            at += w

    return pl.pallas_call(
        body, name=name, grid=(T // ROWS,),
        in_specs=[pl.BlockSpec((ROWS, w), lambda i: (i, 0)) for w in widths],
        out_specs=pl.BlockSpec((ROWS, sum(widths)), lambda i: (i, 0)),
        out_shape=jax.ShapeDtypeStruct((T, sum(widths)), BF16), compiler_params=_params("parallel"),
    )(*parts)


SB_FORWARD_ROWS = 2048
SB_BACKWARD_ROWS = 1024
SB_KEYS = 2 * ATT_BLOCK


def _tri_and_ones(pred):
    rows = lax.broadcasted_iota(jnp.int32, (2 * ATT_BLOCK, 2 * ATT_BLOCK), 0) % ATT_BLOCK
    cols = lax.broadcasted_iota(jnp.int32, (2 * ATT_BLOCK, 2 * ATT_BLOCK), 1)
    return ((cols >= ATT_BLOCK) | pred(rows, cols)).astype(BF16)


def _running(x, tri):
    hi = x.astype(BF16)
    lo = (x - hi.astype(F32)).astype(BF16)
    return _dot(jnp.concatenate([hi, lo], axis=1), tri)


def _sb_mask(query_rows, s):
    rows = lax.broadcasted_iota(jnp.int32, (query_rows - s * SB_KEYS, SB_KEYS), 0)
    cols = lax.broadcasted_iota(jnp.int32, (query_rows - s * SB_KEYS, SB_KEYS), 1)
    return cols < rows


def _log_sigmoids(z):
    ls = jnp.minimum(z, 0.0) - jnp.log(1.0 + jnp.exp(-jnp.abs(z)))
    return ls, ls - z


def _sb_fwd(qkv, W, name):
    T = qkv.shape[0]
    H = W // HEAD_DIM
    blk = ATT_BLOCK
    qb = min(SB_FORWARD_ROWS, T)
    per = qb // SB_KEYS

    def body(q_ref, k_ref, v_ref, o_ref, lt_ref, acc_ref):
        i = pl.program_id(1)
        tri = _tri_and_ones(lambda r, c: r > c)
        lt_ref[...] = jnp.zeros_like(lt_ref)
        acc_ref[...] = jnp.zeros_like(acc_ref)

        def tile(j, mask, r0=0):
            ks = pl.ds(pl.multiple_of(j * SB_KEYS, SB_KEYS), SB_KEYS)
            qs = pl.ds(r0, qb - r0)
            z = _dot(q_ref[qs, :], k_ref[ks, :], NT)
            ls, lm = _log_sigmoids(z)
            if mask is not None:
                lm = jnp.where(mask, lm, 0.0)
            later = lt_ref[qs, :]
            second = _running(lm[:, blk:], tri)
            first = _running(lm[:, :blk], tri)
            after_first = later + second[:, blk:]
            a = jnp.exp(ls + jnp.concatenate([first[:, :blk] + after_first, second[:, :blk] + later], axis=1))
            if mask is not None:
                a = jnp.where(mask, a, 0.0)
            acc_ref[qs, :] += _dot(a.astype(BF16), v_ref[ks, :])
            lt_ref[qs, :] = after_first + first[:, blk:]

        for s in reversed(range(per)):
            tile(i * per + s, _sb_mask(qb, s), s * SB_KEYS)

        def step(jj, _):
            for s in range(per):
                tile((i - jj) * per - 1 - s, None)
            return 0

        lax.fori_loop(0, i, step, 0)
        o_ref[...] = acc_ref[...].astype(BF16)

    qs = pl.BlockSpec((qb, HEAD_DIM), lambda h, i: (i, h))
    return pl.pallas_call(
        body, name=name, grid=(H, T // qb),
        in_specs=[qs, pl.BlockSpec((T, HEAD_DIM), lambda h, i: (0, H + h)),
                  pl.BlockSpec((T, HEAD_DIM), lambda h, i: (0, 2 * H + h))],
        out_specs=[qs, qs],
        out_shape=[jax.ShapeDtypeStruct((T, W), BF16), jax.ShapeDtypeStruct((T, W), F32)],
        scratch_shapes=[pltpu.VMEM((qb, HEAD_DIM), F32)],
        compiler_params=_params("parallel", "arbitrary"),
    )(qkv, qkv, qkv)


def _sb_bwd(qkv, do, ltot, W, name):
    T = qkv.shape[0]
    H = W // HEAD_DIM
    blk = ATT_BLOCK
    nkb = T // SB_KEYS
    qb = min(SB_BACKWARD_ROWS, T)
    per = qb // SB_KEYS

    def body(q_ref, k_ref, v_ref, do_ref, lt_ref, dq_ref, dkt_ref, dvt_ref, qt_ref, dot_ref, plm_ref, pg_ref):
        i = pl.program_id(1)

        @pl.when(i == 0)
        def _():
            dkt_ref[...] = jnp.zeros_like(dkt_ref)
            dvt_ref[...] = jnp.zeros_like(dvt_ref)

        qt_ref[...] = q_ref[...].astype(F32).T.astype(BF16)
        dot_ref[...] = do_ref[...].astype(F32).T.astype(BF16)
        upto = _tri_and_ones(lambda r, c: r <= c)
        before = _tri_and_ones(lambda r, c: r < c)
        plm_ref[...] = jnp.zeros_like(plm_ref)
        pg_ref[...] = jnp.zeros_like(pg_ref)
        dq_ref[...] = jnp.zeros_like(dq_ref)

        def tile(j, mask, r0=0):
            ks = pl.ds(pl.multiple_of(j * SB_KEYS, SB_KEYS), SB_KEYS)
            qs = pl.ds(r0, qb - r0)
            k = k_ref[ks, :]
            v = v_ref[ks, :]
            z = _dot(q_ref[qs, :], k, NT)
            ls, lm = _log_sigmoids(z)
            nsig = jnp.exp(lm)
            if mask is not None:
                lm = jnp.where(mask, lm, 0.0)
            earlier = plm_ref[qs, :]
            first = _running(lm[:, :blk], upto)
            second = _running(lm[:, blk:], upto)
            upto_first = earlier + first[:, blk:]
            seen = jnp.concatenate([first[:, :blk] + earlier, second[:, :blk] + upto_first], axis=1)
            ltot = lt_ref[qs, :]
            a = jnp.exp(ls + (jnp.concatenate([ltot, ltot], axis=1) - seen))
            if mask is not None:
                a = jnp.where(mask, a, 0.0)
            g = a * _dot(do_ref[qs, :], v, NT)
            g_earlier = pg_ref[qs, :]
            g_first = _running(g[:, :blk], before)
            g_second = _running(g[:, blk:], before)
            g_upto_first = g_earlier + g_first[:, blk:]
            gsum = jnp.concatenate([g_first[:, :blk] + g_earlier, g_second[:, :blk] + g_upto_first], axis=1)
            dz = g * nsig - gsum * jnp.exp(ls)
            if mask is not None:
                dz = jnp.where(mask, dz, 0.0)
            dzb = dz.astype(BF16)
            dkt_ref[j] += _dot(qt_ref[:, qs], dzb)
            dvt_ref[j] += _dot(dot_ref[:, qs], a.astype(BF16))
            dq_ref[qs, :] += _dot(dzb, k)
            plm_ref[qs, :] = upto_first + second[:, blk:]
            pg_ref[qs, :] = g_upto_first + g_second[:, blk:]

        def step(jj, _):
            for s in range(per):
                tile(jj * per + s, None)
            return 0

        lax.fori_loop(0, i, step, 0)
        for s in range(per):
            tile(i * per + s, _sb_mask(qb, s), s * SB_KEYS)

    qs = pl.BlockSpec((qb, HEAD_DIM), lambda h, i: (i, h))
    res = pl.BlockSpec((None, nkb, HEAD_DIM, SB_KEYS), lambda h, i: (h, 0, 0, 0))
    return pl.pallas_call(
        body, name=name, grid=(H, T // qb),
        in_specs=[qs, pl.BlockSpec((T, HEAD_DIM), lambda h, i: (0, H + h)),
                  pl.BlockSpec((T, HEAD_DIM), lambda h, i: (0, 2 * H + h)), qs, qs],
        out_specs=[qs, res, res],
        out_shape=[jax.ShapeDtypeStruct((T, W), F32)] + [jax.ShapeDtypeStruct((H, nkb, HEAD_DIM, SB_KEYS), F32)] * 2,
        scratch_shapes=[pltpu.VMEM((HEAD_DIM, qb), BF16), pltpu.VMEM((HEAD_DIM, qb), BF16),
                        pltpu.VMEM((qb, HEAD_DIM), F32), pltpu.VMEM((qb, HEAD_DIM), F32)],
        compiler_params=_params("parallel", "arbitrary"),
    )(qkv, qkv, qkv, do, ltot)


def _sb_pack(dq, dkt, dvt, name):
    T, W = dq.shape
    H = W // HEAD_DIM
    blk = SB_KEYS
    scale = HEAD_DIM ** -0.5

    def body(q_ref, kt_ref, vt_ref, o_ref):
        o_ref[:, pl.ds(0, W)] = (q_ref[...] * scale).astype(BF16)
        for h in range(H):
            o_ref[:, pl.ds(W + h * HEAD_DIM, HEAD_DIM)] = kt_ref[h].T.astype(BF16)
            o_ref[:, pl.ds(2 * W + h * HEAD_DIM, HEAD_DIM)] = vt_ref[h].T.astype(BF16)

    tr = pl.BlockSpec((H, None, HEAD_DIM, blk), lambda i: (0, i, 0, 0))
    return pl.pallas_call(
        body, name=name, grid=(T // blk,), in_specs=[pl.BlockSpec((blk, W), lambda i: (i, 0)), tr, tr],
        out_specs=pl.BlockSpec((blk, 3 * W), lambda i: (i, 0)),
        out_shape=jax.ShapeDtypeStruct((T, 3 * W), BF16), compiler_params=_params("parallel"),
    )(dq, dkt, dvt)


def _local_step(x, target, norms, sgu, comm):
    T, D = x.shape
    A = D // 2
    pre_mix, post_mix, pre_ffn, post_ffn = norms
    ln_g, ln_b, w_s, b_s = sgu
    b_t = b_s.T
    scale = HEAD_DIM ** -0.5

    def vec(p, layer):
        return comm.tie(p[layer:layer + 1])

    h0 = _prenorm(x, vec(pre_mix, 0), "prenorm0")
    comm.arrive("ab_in", after=h0)
    comm.land("ab_in")
    z = _matmul(h0, comm.weight("ab_in"), mode="nn", name="ab_in_fwd")
    comm.arrive("ab_out", after=z)
    cat = _sgu_fwd(z, ln_g, ln_b, w_s, b_t, "sgu_fwd")
    cat, o_dil, lse_dil = _dilated_forward(z, cat, A, "dilated_fwd")
    comm.land("ab_out", after=o_dil)
    y0 = _matmul(cat, comm.weight("ab_out"), mode="nn", name="ab_out_fwd")
    comm.arrive("w1_0", after=y0)
    x1, h1 = _postnorm_prenorm(x, y0, vec(post_mix, 0), vec(pre_ffn, 0), "norm_mix0")
    comm.land("w1_0", after=h1)
    comm.arrive("w2_0", after=h1)

    def relu2(acc, j):
        r = jnp.maximum(acc, 0.0)
        return r * r, 2.0 * r

    f0, r0 = _matmul(h1, comm.weight("w1_0"), mode="nn", name="ffn0_w1_fwd", out_dtype=(BF16, BF16), epi=relu2)
    comm.land("w2_0", after=f0)
    y1 = _matmul(f0, comm.weight("w2_0"), mode="nn", name="ffn0_w2_fwd")
    comm.arrive("sb", after=y1)
    x2, h2 = _postnorm_prenorm(x1, y1, vec(post_ffn, 0), vec(pre_mix, 1), "norm_ffn0")
    comm.land("sb", after=h2)

    tn_qkv = _tile(D, 1024)
    nq = D // tn_qkv

    def scale_q(acc, j):
        return jnp.where(j < nq, acc * scale, acc)

    qkv = _matmul(h2, comm.weight("sb_in"), mode="nn", name="sb_in_fwd", out_dtype=BF16, tn=tn_qkv, epi=scale_q)
    comm.arrive("ffn1", after=qkv)
    o_sb, ltot = _sb_fwd(qkv, D, "sb_fwd")
    comm.land("ffn1", after=o_sb)
    y2 = _matmul(o_sb, comm.weight("sb_out"), mode="nn", name="sb_out_fwd")
    x3, h3 = _postnorm_prenorm(x2, y2, vec(post_mix, 1), vec(pre_ffn, 1), "norm_mix1")
    f1, r1 = _matmul(h3, comm.weight("w1_1"), mode="nn", name="ffn1_w1_fwd", out_dtype=(BF16, BF16), epi=relu2)
    y3 = _matmul(f1, comm.weight("w2_1"), mode="nn", name="ffn1_w2_fwd")
    loss_tile, dx4, dy3, dg_post_ffn1 = _postnorm_loss(x3, y3, vec(post_ffn, 1), target, "norm_loss")
    loss = loss_tile[0, 0]

    def relu2_bwd(acc, j, r):
        return acc * r.astype(F32)

    def ffn_bwd(dy, h, f, r, layer):
        g_w2 = _matmul(f, dy, mode="tn", name=f"ffn{layer}_w2_wgrad", out_dtype=BF16)
        da = _matmul(dy, comm.weight(f"w2_{layer}"), mode="nt", name=f"ffn{layer}_w2_dgrad", out_dtype=BF16,
                     epi=relu2_bwd, extras=(r,))
        g_w1 = _matmul(h, da, mode="tn", name=f"ffn{layer}_w1_wgrad", out_dtype=BF16)
        comm.reduce(f"ffn{layer}", {f"w2_{layer}": g_w2, f"w1_{layer}": g_w1})
        return _matmul(da, comm.weight(f"w1_{layer}"), mode="nt", name=f"ffn{layer}_w1_dgrad", after=comm.started())

    dh3 = ffn_bwd(dy3, h3, f1, r1, 1)
    dx3, dg_pre_ffn1, dy2, dg_post_mix1 = _norm_bwd_pair(dx4, dh3, x3, vec(pre_ffn, 1), y2, vec(post_mix, 1),
                                                         "ffn1_sb_norm_bwd")
    g_sb_out = _matmul(o_sb, dy2, mode="tn", name="sb_out_wgrad", out_dtype=BF16)
    do_sb = _matmul(dy2, comm.weight("sb_out"), mode="nt", name="sb_out_dgrad", out_dtype=BF16)
    dq, dk, dv = _sb_bwd(qkv, do_sb, ltot, D, "sb_bwd")
    dqkv = _sb_pack(dq, dk, dv, "sb_pack")
    g_sb_in = _matmul(h2, dqkv, mode="tn", name="sb_in_wgrad", out_dtype=BF16)
    comm.reduce("sb", {"sb_out": g_sb_out, "sb_in": g_sb_in})
    dh2 = _matmul(dqkv, comm.weight("sb_in"), mode="nt", name="sb_in_dgrad", after=comm.started())
    dx2, dg_pre_mix1, dy1, dg_post_ffn0 = _norm_bwd_pair(dx3, dh2, x2, vec(pre_mix, 1), y1, vec(post_ffn, 0),
                                                         "sb_ffn0_norm_bwd")
    dh1 = ffn_bwd(dy1, h1, f0, r0, 0)
    dx1, dg_pre_ffn0, dy0, dg_post_mix0 = _norm_bwd_pair(dx2, dh1, x1, vec(pre_ffn, 0), y0, vec(post_mix, 0),
                                                         "ffn0_ab_norm_bwd")
    g_ab_out = _matmul(cat, dy0, mode="tn", name="ab_out_wgrad", out_dtype=BF16)
    comm.reduce("ab_out", {"ab_out": g_ab_out})
    dcat = _matmul(dy0, comm.weight("ab_out"), mode="nt", name="ab_out_dgrad", after=comm.started())
    duv, d_ln_g, d_ln_b, d_w_s, d_b_t = _sgu_bwd(z, dcat, ln_g, ln_b, w_s, b_t, "sgu_bwd")
    dz = _join_columns([duv, *_dilated_backward(z, dcat, o_dil, lse_dil, A, "dilated_bwd")], "join_dz")
    g_ab_in = _matmul(h0, dz, mode="tn", name="ab_in_wgrad", out_dtype=BF16)
    comm.reduce("ab_in", {"ab_in": g_ab_in})
    dh0 = _matmul(dz, comm.weight("ab_in"), mode="nt", name="ab_in_dgrad", after=comm.started())
    dx0, dg_pre_mix0 = _prenorm_bwd(dx1, dh0, x, vec(pre_mix, 0), "ab_prenorm_bwd")

    small = {
        "pre_mix": jnp.concatenate([dg_pre_mix0, dg_pre_mix1], axis=0),
        "post_mix": jnp.concatenate([dg_post_mix0, dg_post_mix1], axis=0),
        "pre_ffn": jnp.concatenate([dg_pre_ffn0, dg_pre_ffn1], axis=0),
        "post_ffn": jnp.concatenate([dg_post_ffn0, dg_post_ffn1], axis=0),
        "ln_g": d_ln_g, "ln_b": d_ln_b, "w_s": d_w_s, "b_s": d_b_t.T,
    }
    return loss, dx0, small


MESH_ID = pl.DeviceIdType.MESH
ANY = pl.BlockSpec(memory_space=pl.ANY)


def _coords():
    return lax.axis_index("x"), lax.axis_index("y"), lax.axis_index("c")


def _shard_of(ref, kind, p):
    if kind == "col":
        n = ref.shape[1] // N_DEV
        return ref.at[:, pl.ds(pl.multiple_of(p * n, 128), n)]
    r = ref.shape[0] // N_DEV
    return ref.at[pl.ds(pl.multiple_of(p * r, 16), r), :]


def _full_shape(shard, kind):
    if kind == "col":
        return (shard.shape[0], shard.shape[1] * N_DEV)
    return (shard.shape[0] * N_DEV, shard.shape[1])


def _place(shards, layer, kind, block, after, name):
    _, rows, cols = shards.shape
    tr = _tile(rows, 512)

    def body(b_ref, s_ref, after_ref, o_ref):
        o_ref[...] = s_ref[...].astype(BF16)

    if kind == "col":
        out = pl.BlockSpec((tr, cols), lambda i, b_ref: (i, b_ref[0]))
    else:
        out = pl.BlockSpec((tr, cols), lambda i, b_ref: (b_ref[0] * (rows // tr) + i, 0))
    return pl.pallas_call(
        body, name=name,
        grid_spec=pltpu.PrefetchScalarGridSpec(
            num_scalar_prefetch=1, grid=(rows // tr,),
            in_specs=[pl.BlockSpec((None, tr, cols), lambda i, b_ref: (layer, i, 0)), ANY], out_specs=out),
        out_shape=jax.ShapeDtypeStruct(_full_shape(shards[0], kind), BF16),
        compiler_params=_params("parallel"),
    )(block, shards, after)


HBM = pl.BlockSpec(memory_space=pltpu.HBM)
SEM = pl.BlockSpec(memory_space=pltpu.SEMAPHORE)
FLOWS = pltpu.SideEffectType.DATAFLOW_SIDE_EFFECTING


def _in_hbm(a):
    return pltpu.with_memory_space_constraint(a, pltpu.HBM)


def _hbm_like(bufs):
    return [pltpu.HBM(b.shape, b.dtype) for b in bufs]


def _copies_start(name, bufs, plan, n, after):
    nb = len(bufs)

    def body(*refs):
        send_sems, recv_sems, token = refs[nb + 1], refs[nb + 2], refs[-1]
        for cp in plan(refs[:nb], send_sems, recv_sems):
            cp.start()
        token[...] = jnp.zeros_like(token)

    out = pl.pallas_call(
        body, name=name, in_specs=[HBM] * nb + [ANY],
        out_specs=[SEM, SEM] + [HBM] * nb + [pl.BlockSpec(memory_space=pltpu.VMEM)],
        out_shape=[pltpu.SemaphoreType.DMA((n,)), pltpu.SemaphoreType.DMA((n,))] + _hbm_like(bufs)
        + [jax.ShapeDtypeStruct((8, 128), F32)],
        input_output_aliases={i: 2 + i for i in range(nb)},
        compiler_params=pltpu.CompilerParams(has_side_effects=FLOWS),
    )(*[_in_hbm(b) for b in bufs], after)
    return (out[0], out[1]), list(out[2:2 + nb]), out[-1]


def _copies_wait(name, bufs, sems, after, plan):
    nb = len(bufs)

    def body(*refs):
        for cp in plan(refs[:nb], refs[nb], refs[nb + 1]):
            cp.wait_send()
            cp.wait_recv()

    out = pl.pallas_call(
        body, name=name, in_specs=[HBM] * nb + [SEM, SEM, ANY], out_specs=[HBM] * nb,
        out_shape=_hbm_like(bufs), input_output_aliases={i: i for i in range(nb)},
        compiler_params=pltpu.CompilerParams(has_side_effects=FLOWS),
    )(*bufs, *sems, after)
    return list(out)


def _copies_wait_start(name, bufs, sems, after, plan, next_plan, n_next):
    nb = len(bufs)

    def body(*refs):
        ins = refs[:nb]
        for cp in plan(ins, refs[nb], refs[nb + 1]):
            cp.wait_send()
            cp.wait_recv()
        send_sems, recv_sems, token = refs[nb + 3], refs[nb + 4], refs[-1]
        for cp in next_plan(ins, send_sems, recv_sems):
            cp.start()
        token[...] = jnp.zeros_like(token)

    out = pl.pallas_call(
        body, name=name, in_specs=[HBM] * nb + [SEM, SEM, ANY],
        out_specs=[SEM, SEM] + [HBM] * nb + [pl.BlockSpec(memory_space=pltpu.VMEM)],
        out_shape=[pltpu.SemaphoreType.DMA((n_next,)), pltpu.SemaphoreType.DMA((n_next,))] + _hbm_like(bufs)
        + [jax.ShapeDtypeStruct((8, 128), F32)],
        input_output_aliases={i: 2 + i for i in range(nb)},
        compiler_params=pltpu.CompilerParams(has_side_effects=FLOWS),
    )(*bufs, *sems, after)
    return (out[0], out[1]), list(out[2:2 + nb]), out[-1]


def _gather_plans(kinds):
    nt = len(kinds)

    def slot(refs, t, px, py, pc):
        return _shard_of(refs[t], kinds[t], 4 * px + 2 * py + pc)

    def to_chips(refs, send_sems, recv_sems):
        x, y, c = _coords()
        peers = [(x, y, 1 - c), (1 - x, y, c), (x, 1 - y, c), (1 - x, 1 - y, c)]
        return [pltpu.make_async_remote_copy(
            src_ref=slot(refs, t, x, y, c), dst_ref=slot(refs, t, x, y, c), send_sem=send_sems.at[4 * t + k],
            recv_sem=recv_sems.at[4 * t + k], device_id=peer, device_id_type=MESH_ID)
            for t in range(nt) for k, peer in enumerate(peers)]

    def to_sibling(refs, send_sems, recv_sems):
        x, y, c = _coords()
        chips = [(1 - x, y), (x, 1 - y), (1 - x, 1 - y)]
        return [pltpu.make_async_remote_copy(
            src_ref=slot(refs, t, *chip, c), dst_ref=slot(refs, t, *chip, c), send_sem=send_sems.at[3 * t + j],
            recv_sem=recv_sems.at[3 * t + j], device_id=(x, y, 1 - c), device_id_type=MESH_ID)
            for t in range(nt) for j, chip in enumerate(chips)]

    return to_chips, to_sibling


def _shard_shape(full, kind):
    if kind == "col":
        return (full.shape[0], full.shape[1] // N_DEV)
    return (full.shape[0] // N_DEV, full.shape[1])


def _scatter_plan(kinds):
    nt = len(kinds)

    def plan(refs, send_sems, recv_sems):
        x, y, c = _coords()
        copies = []
        for t in range(nt):
            for k in range(1, N_DEV):
                px = 1 - x if (k >> 2) & 1 else x
                py = 1 - y if (k >> 1) & 1 else y
                pc = 1 - c if k & 1 else c
                copies.append(pltpu.make_async_remote_copy(
                    src_ref=_shard_of(refs[t], kinds[t], 4 * px + 2 * py + pc),
                    dst_ref=refs[nt + t].at[4 * x + 2 * y + c],
                    send_sem=send_sems.at[7 * t + k - 1], recv_sem=recv_sems.at[7 * t + k - 1],
                    device_id=(px, py, pc), device_id_type=MESH_ID))
        return copies
    return plan


def _partial_specs(full, kind, tr):
    rows, cols = _shard_shape(full, kind)
    steps = rows // tr
    if kind == "col":
        own = pl.BlockSpec((tr, cols), lambda i, w: (i, w[0]))
    else:
        own = pl.BlockSpec((tr, cols), lambda i, w: (w[0] * steps + i, 0))
    return [own] + [pl.BlockSpec((None, tr, cols), lambda i, w, k=k: (w[k], i, 0)) for k in range(1, N_DEV)]


def _small_plan(refs, send_sems, recv_sems):
    x, y, c = _coords()
    copies = []
    for k in range(1, N_DEV):
        peer = (1 - x if (k >> 2) & 1 else x, 1 - y if (k >> 1) & 1 else y, 1 - c if k & 1 else c)
        copies.append(pltpu.make_async_remote_copy(
            src_ref=refs[0], dst_ref=refs[1].at[4 * x + 2 * y + c], send_sem=send_sems.at[k - 1],
            recv_sem=recv_sems.at[k - 1], device_id=peer, device_id_type=MESH_ID))
    return copies


def _sum_small(vec, land):
    def body(v_ref, l_ref, o_ref):
        x, y, c = _coords()
        me = 4 * x + 2 * y + c
        total = jnp.where(me == 0, v_ref[...], l_ref[0])
        for p in range(1, N_DEV):
            total = total + jnp.where(me == p, v_ref[...], l_ref[p])
        o_ref[...] = total

    whole = pl.BlockSpec(memory_space=pltpu.VMEM)
    return pl.pallas_call(
        body, name="sum_small", in_specs=[whole, whole], out_specs=whole,
        out_shape=jax.ShapeDtypeStruct(vec.shape, F32),
        compiler_params=pltpu.CompilerParams(vmem_limit_bytes=VMEM_LIMIT),
    )(vec, land)


def _adamw_math(w, g, m, v):
    m = ADAM_B1 * m + (1.0 - ADAM_B1) * g
    v = ADAM_B2 * v + (1.0 - ADAM_B2) * (g * g)
    m_hat = m / (1.0 - ADAM_B1 ** ADAM_STEP)
    v_hat = v / (1.0 - ADAM_B2 ** ADAM_STEP)
    delta = -ADAM_LR * (m_hat / (jnp.sqrt(v_hat) + ADAM_EPS) + ADAM_WD * w)
    return delta, m, v


def _adamw(w, grads, kind, where, m, v, after, name):
    layers, rows, cols = w.shape
    tr = _tile(rows, 256)
    out = None
    for layer, (grad, land) in enumerate(grads):
        def body(w_ref, *refs):
            parts, (x_ref, m_ref, v_ref) = refs[:N_DEV], refs[N_DEV:N_DEV + 3]
            g_ref, d_ref, mo_ref, vo_ref = refs[-4:]
            g = parts[0][...].astype(F32)
            for p_ref in parts[1:]:
                g = g + p_ref[...].astype(F32)
            g_ref[...] = g
            d_ref[...], mo_ref[...], vo_ref[...] = _adamw_math(x_ref[...], g, m_ref[...], v_ref[...])

        blk = pl.BlockSpec((None, tr, cols), lambda i, w_, layer=layer: (layer, i, 0))
        earlier = [] if out is None else list(out)
        out = pl.pallas_call(
            body, name=f"{name}_{layer}",
            grid_spec=pltpu.PrefetchScalarGridSpec(
                num_scalar_prefetch=1, grid=(rows // tr,),
                in_specs=_partial_specs(grad, kind, tr) + [blk] * 3 + [ANY] * (len(earlier) + len(after)),
                out_specs=[blk] * 4),
            out_shape=[jax.ShapeDtypeStruct((layers, rows, cols), F32)] * 4,
            input_output_aliases={N_DEV + 4 + k: k for k in range(len(earlier))},
            compiler_params=_params("parallel"),
        )(where, grad, *[land] * (N_DEV - 1), w, m, v, *earlier, *after)
    return out


def _adamw_small(w, g, m, v):
    def body(w_ref, g_ref, m_ref, v_ref, d_ref, mo_ref, vo_ref):
        d_ref[...], mo_ref[...], vo_ref[...] = _adamw_math(w_ref[...], g_ref[...], m_ref[...], v_ref[...])

    whole = pl.BlockSpec(memory_space=pltpu.VMEM)
    return pl.pallas_call(
        body, name="adamw_small", in_specs=[whole] * 4, out_specs=[whole] * 3,
        out_shape=[jax.ShapeDtypeStruct(w.shape, F32)] * 3,
        compiler_params=pltpu.CompilerParams(vmem_limit_bytes=VMEM_LIMIT),
    )(w, g, m, v)


def _pack(arrays):
    rows = []
    for a in arrays:
        flat = a.reshape(-1)
        pad = (-flat.shape[0]) % 1024
        rows.append(jnp.pad(flat, (0, pad)).reshape(-1, 128))
    return jnp.concatenate(rows, axis=0)


def _unpack(packed, like):
    out, r = [], 0
    for a in like:
        n = math.prod(a.shape)
        nr = (n + 1023) // 1024 * 8
        out.append(packed[r:r + nr].reshape(-1)[:n].reshape(a.shape))
        r += nr
    return out


KIND = {"ab_in": "col", "ab_out": "row", "sb_in": "col", "sb_out": "row",
        "w1_0": "col", "w1_1": "col", "w2_0": "row", "w2_1": "row"}
GATHERS = {"ab_in": ("ab_in",), "ab_out": ("ab_out",), "w1_0": ("w1_0",), "w2_0": ("w2_0",),
           "sb": ("sb_in", "sb_out"), "ffn1": ("w1_1", "w2_1")}


class _Exchange:
    def __init__(self, shards):
        x, y, c = _coords()
        me = (4 * x + 2 * y + c).astype(jnp.int32)
        self.where = jnp.stack([jnp.bitwise_xor(me, k) for k in range(N_DEV)])
        self.full = {}
        self.tokens = []
        self.gathers = {}
        self.scatters = {}
        self.settled = {}
        block = me.reshape(1)
        after = block
        for key, group in GATHERS.items():
            for n in group:
                self.full[n] = _place(*shards[n], KIND[n], block, after, f"place_{n}")
            to_chips, to_sibling = _gather_plans([KIND[n] for n in group])
            bufs = [self.full[n] for n in group]
            sems, bufs, after = _copies_start(f"gather_start_{key}", bufs, to_chips, 4 * len(group), after)
            self.tokens.append(after)
            self.gathers[key] = (group, sems, bufs, to_chips, to_sibling)

    def tie(self, small):
        for token in self.tokens:
            small = small + token[0:1, 0:1]
        self.tokens = []
        return small

    def started(self):
        return tuple(self.tokens)

    def weight(self, name):
        return self.full[name]

    def arrive(self, key, after):
        group, sems, bufs, to_chips, to_sibling = self.gathers[key]
        sems, bufs, token = _copies_wait_start(f"gather_pass_{key}", bufs, sems, after, to_chips, to_sibling,
                                               3 * len(group))
        self.tokens.append(token)
        self.gathers[key] = (group, sems, bufs, token, to_sibling)

    def land(self, key, after=None):
        group, sems, bufs, token, to_sibling = self.gathers.pop(key)
        after = token if after is None else after
        self.full.update(zip(group, _copies_wait(f"gather_done_{key}", bufs, sems, after, to_sibling)))

    def reduce(self, key, grads):
        names = list(grads)
        kinds = [KIND[n] for n in names]
        full = [grads[n] for n in names]
        lands = [lax.empty((N_DEV,) + _shard_shape(g, k), BF16) for g, k in zip(full, kinds)]
        plan = _scatter_plan(kinds)
        sems, bufs, token = _copies_start(f"scatter_start_{key}", full + lands, plan, (N_DEV - 1) * len(names),
                                          self.where)
        self.tokens.append(token)
        self.scatters[key] = (names, sems, bufs, plan)

    def settle(self, keys, after):
        for key in keys:
            names, sems, bufs, plan = self.scatters.pop(key)
            bufs = _copies_wait(f"scatter_done_{key}", bufs, sems, after, plan)
            self.settled.update({n: t for n, *t in zip(names, bufs[:len(names)], bufs[len(names):])})
        return self.settled


SMALL = ("norm_pre_mix", "norm_post_mix", "norm_pre_ffn", "norm_post_ffn", "sgu_ln_g", "sgu_ln_b", "sgu_w", "sgu_b")
ORDER = ("norm_pre_mix", "norm_post_mix", "norm_pre_ffn", "norm_post_ffn", "ab_w_in", "sgu_ln_g", "sgu_ln_b", "sgu_w",
         "sgu_b", "ab_w_out", "sb_w_in", "sb_w_out", "ffn_w1", "ffn_w2")


def kernel(x, norm_pre_mix, norm_post_mix, norm_pre_ffn, norm_post_ffn, ab_w_in, sgu_ln_g, sgu_ln_b, sgu_w, sgu_b, ab_w_out, sb_w_in, sb_w_out, ffn_w1, ffn_w2, loss_target, m_norm_pre_mix, m_norm_post_mix, m_norm_pre_ffn, m_norm_post_ffn, m_ab_w_in, m_sgu_ln_g, m_sgu_ln_b, m_sgu_w, m_sgu_b, m_ab_w_out, m_sb_w_in, m_sb_w_out, m_ffn_w1, m_ffn_w2, v_norm_pre_mix, v_norm_post_mix, v_norm_pre_ffn, v_norm_post_ffn, v_ab_w_in, v_sgu_ln_g, v_sgu_ln_b, v_sgu_w, v_sgu_b, v_ab_w_out, v_sb_w_in, v_sb_w_out, v_ffn_w1, v_ffn_w2):
    W = dict(norm_pre_mix=norm_pre_mix, norm_post_mix=norm_post_mix, norm_pre_ffn=norm_pre_ffn,
             norm_post_ffn=norm_post_ffn, ab_w_in=ab_w_in, sgu_ln_g=sgu_ln_g, sgu_ln_b=sgu_ln_b, sgu_w=sgu_w,
             sgu_b=sgu_b, ab_w_out=ab_w_out, sb_w_in=sb_w_in, sb_w_out=sb_w_out, ffn_w1=ffn_w1, ffn_w2=ffn_w2)
    M = dict(norm_pre_mix=m_norm_pre_mix, norm_post_mix=m_norm_post_mix, norm_pre_ffn=m_norm_pre_ffn,
             norm_post_ffn=m_norm_post_ffn, ab_w_in=m_ab_w_in, sgu_ln_g=m_sgu_ln_g, sgu_ln_b=m_sgu_ln_b,
             sgu_w=m_sgu_w, sgu_b=m_sgu_b, ab_w_out=m_ab_w_out, sb_w_in=m_sb_w_in, sb_w_out=m_sb_w_out,
             ffn_w1=m_ffn_w1, ffn_w2=m_ffn_w2)
    V = dict(norm_pre_mix=v_norm_pre_mix, norm_post_mix=v_norm_post_mix, norm_pre_ffn=v_norm_pre_ffn,
             norm_post_ffn=v_norm_post_ffn, ab_w_in=v_ab_w_in, sgu_ln_g=v_sgu_ln_g, sgu_ln_b=v_sgu_ln_b,
             sgu_w=v_sgu_w, sgu_b=v_sgu_b, ab_w_out=v_ab_w_out, sb_w_in=v_sb_w_in, sb_w_out=v_sb_w_out,
             ffn_w1=v_ffn_w1, ffn_w2=v_ffn_w2)

    shards = {"ab_in": (ab_w_in, 0), "ab_out": (ab_w_out, 0), "w1_0": (ffn_w1, 0), "w2_0": (ffn_w2, 0),
              "sb_in": (sb_w_in, 0), "sb_out": (sb_w_out, 0), "w1_1": (ffn_w1, 1), "w2_1": (ffn_w2, 1)}
    comm = _Exchange(shards)
    norms = (norm_pre_mix, norm_post_mix, norm_pre_ffn, norm_post_ffn)
    sgu = (sgu_ln_g, sgu_ln_b, sgu_w[0], sgu_b[0])
    loss, dx, small = _local_step(x[0], loss_target[0], norms, sgu, comm)

    out = {}

    def update(name, layers, landed, after=()):
        out[name] = _adamw(W[name], [landed[n] for n in layers], KIND[layers[0]], comm.where, M[name], V[name],
                           after, f"adamw_{name}")

    landed = comm.settle(("ffn1", "sb", "ffn0"), after=dx)
    small_g = [small["pre_mix"], small["post_mix"], small["pre_ffn"], small["post_ffn"], small["ln_g"],
               small["ln_b"], small["w_s"][None], small["b_s"][None]]
    mine = _pack(small_g + [loss.reshape(1)])
    sems, bufs, token = _copies_start("small_start", [mine, jnp.zeros((N_DEV,) + mine.shape, F32)], _small_plan,
                                      N_DEV - 1, dx)
    for name, layers in (("sb_w_in", ["sb_in"]), ("sb_w_out", ["sb_out"]), ("ffn_w1", ["w1_0", "w1_1"]),
                         ("ffn_w2", ["w2_0", "w2_1"])):
        update(name, layers, landed, after=(token,))
    summed = _sum_small(*_copies_wait("small_done", bufs, sems, out["ffn_w2"][3], _small_plan))
    g_small, loss = summed[:-8], summed[-8, 0]
    landed = comm.settle(("ab_out", "ab_in"), after=g_small)
    update("ab_w_out", ["ab_out"], landed)
    update("ab_w_in", ["ab_in"], landed)
    res = _adamw_small(_pack([W[n] for n in SMALL]), g_small, _pack([M[n] for n in SMALL]),
                       _pack([V[n] for n in SMALL]))
    like = [W[n] for n in SMALL]
    for n, *vals in zip(SMALL, *[_unpack(r, like) for r in [g_small] + list(res)]):
        out[n] = vals

    return (loss, dx[None], *[out[n][0] for n in ORDER], *[out[n][1] for n in ORDER],
            *[out[n][2] for n in ORDER], *[out[n][3] for n in ORDER])
```
